```python
import jax, jax.numpy as jnp
from jax import lax
import numpy as np

D_MODEL = 2048
BATCH = 8
SEQ = 2048
DEPTH = 2

HEAD_DIM = 128
ATTN_WIDTH = D_MODEL // 2
CONV_DIM = D_MODEL - ATTN_WIDTH
MIX_DIM = ATTN_WIDTH + CONV_DIM
N_Q_HEADS = ATTN_WIDTH // HEAD_DIM
N_KV_HEADS = max(1, N_Q_HEADS // 4)
Q_PER_KV = N_Q_HEADS // N_KV_HEADS
Q_DIM = N_Q_HEADS * HEAD_DIM
KV_DIM = N_KV_HEADS * HEAD_DIM
N_CONV_GROUPS = CONV_DIM // HEAD_DIM
CONV_WIDTH = 3
IN_PROJ_DIM = Q_DIM + 2 * KV_DIM + 3 * CONV_DIM
DILATED_BRANCHES = ((128, 1), (512, 4), (2048, 16))
D_FF = 5632
FFN_RESIDUAL_WEIGHT = 0.5
NORM_EPS = 1e-6

kernel_name = 'hybrid_dilated_attn_shortconv_macaron'


def rms_norm(x, gain):
    xf = x.astype(jnp.float32)
    y = xf * lax.rsqrt(jnp.mean(xf * xf, axis=-1, keepdims=True) + NORM_EPS)
    return (y * gain.astype(jnp.float32)).astype(x.dtype)


def swiglu(x, w_gate_up, w_down):
    gu = x @ w_gate_up
    gate, up = jnp.split(gu, 2, axis=-1)
    return (jax.nn.silu(gate) * up) @ w_down


def dilated_branch(q, k, v, window, dilation):
    b, s, g, r, hd = q.shape
    span = window // dilation
    blk = span
    sub_len = -(-s // dilation)
    nb = -(-sub_len // blk)
    sp = nb * blk * dilation
    pad = sp - s
    qb = jnp.pad(q, ((0, 0), (0, pad), (0, 0), (0, 0), (0, 0))).reshape(b, nb, blk, dilation, g, r, hd)
    kb = jnp.pad(k, ((0, 0), (0, pad), (0, 0), (0, 0))).reshape(b, nb, blk, dilation, g, hd)
    vb = jnp.pad(v, ((0, 0), (0, pad), (0, 0), (0, 0))).reshape(b, nb, blk, dilation, g, hd)

    def with_prev(t):
        prev = jnp.pad(t[:, :-1], ((0, 0), (1, 0), (0, 0), (0, 0), (0, 0), (0, 0)))
        return jnp.concatenate([prev, t], axis=2)

    kw = with_prev(kb)
    vw = with_prev(vb)
    scores = jnp.einsum('bnqeghd,bnkegd->bneghqk', qb, kw,
                        preferred_element_type=jnp.float32) * (hd ** -0.5)
    qi = jnp.arange(blk)[:, None]
    kj = jnp.arange(2 * blk)[None, :]
    dist = qi - kj + blk
    band = (dist >= 0) & (dist <= span)
    valid = band[None] & ((jnp.arange(nb)[:, None, None] > 0) | (kj[None] >= blk))
    scores = jnp.where(valid[None, :, None, None, None], scores, -jnp.inf)
    m = jnp.max(scores, axis=-1, keepdims=True)
    p = jnp.exp(scores - m)
    den = jnp.sum(p, axis=-1, keepdims=True)
    o = jnp.einsum('bneghqk,bnkegd->bnqeghd', p / den, vw.astype(jnp.float32))
    lse = jnp.moveaxis((m + jnp.log(den))[..., 0], -1, 2)
    o = o.reshape(b, sp, g, r, hd)[:, :s]
    lse = lse.reshape(b, sp, g, r)[:, :s]
    return o, lse


def dilated_attention(q, k, v):
    b, s, _ = q.shape
    qh = q.reshape(b, s, N_KV_HEADS, Q_PER_KV, HEAD_DIM)
    kh = k.reshape(b, s, N_KV_HEADS, HEAD_DIM)
    vh = v.reshape(b, s, N_KV_HEADS, HEAD_DIM)
    outs, lses = [], []
    for window, dilation in DILATED_BRANCHES:
        o, lse = dilated_branch(qh, kh, vh, window, dilation)
        outs.append(o)
        lses.append(lse)
    weights = jax.nn.softmax(jnp.stack(lses, axis=0), axis=0)
    o = jnp.sum(weights[..., None] * jnp.stack(outs, axis=0), axis=0)
    return o.reshape(b, s, Q_DIM).astype(q.dtype)


def gated_short_conv(h, b_gate, c_gate, conv_w):
    u = c_gate * h
    y = lax.conv_general_dilated(
        u, conv_w[:, None, :].astype(u.dtype), window_strides=(1,),
        padding=[(CONV_WIDTH - 1, 0)], dimension_numbers=('NWC', 'WIO', 'NWC'),
        feature_group_count=u.shape[-1])
    return b_gate * y


def _fwd_setup_inputs(seed: int = 0) -> dict:
    key = jax.random.key(seed)
    ks = jax.random.split(key, 20)

    def w(k, shape, fan_in):
        return jax.random.normal(k, shape, jnp.float32) * (fan_in ** -0.5)

    def gain(k, shape):
        return 1.0 + 0.02 * jax.random.normal(k, shape, jnp.float32)

    return {
        'x': jax.random.normal(ks[0], (BATCH, SEQ, D_MODEL), jnp.float32),
        'ffn1_norm_pre': gain(ks[1], (DEPTH, D_MODEL)),
        'ffn1_w_gate_up': w(ks[2], (DEPTH, D_MODEL, 2 * D_FF), D_MODEL),
        'ffn1_w_down': w(ks[3], (DEPTH, D_FF, D_MODEL), D_FF),
        'ffn1_norm_post': gain(ks[4], (DEPTH, D_MODEL)),
        'mix_norm_pre': gain(ks[5], (DEPTH, D_MODEL)),
        'w_in': w(ks[6], (DEPTH, D_MODEL, IN_PROJ_DIM), D_MODEL),
        'conv_w': w(ks[7], (DEPTH, CONV_WIDTH, CONV_DIM), CONV_WIDTH),
        'attn_out_norm': gain(ks[8], (DEPTH, Q_DIM)),
        'conv_out_norm': gain(ks[9], (DEPTH, CONV_DIM)),
        'w_out': w(ks[10], (DEPTH, MIX_DIM, D_MODEL), MIX_DIM),
        'mix_norm_post': gain(ks[11], (DEPTH, D_MODEL)),
        'ffn2_norm_pre': gain(ks[12], (DEPTH, D_MODEL)),
        'ffn2_w_gate_up': w(ks[13], (DEPTH, D_MODEL, 2 * D_FF), D_MODEL),
        'ffn2_w_down': w(ks[14], (DEPTH, D_FF, D_MODEL), D_FF),
        'ffn2_norm_post': gain(ks[15], (DEPTH, D_MODEL)),
    }


def _fwd_reference(x, ffn1_norm_pre, ffn1_w_gate_up, ffn1_w_down, ffn1_norm_post,
              mix_norm_pre, w_in, conv_w, attn_out_norm, conv_out_norm, w_out,
              mix_norm_post, ffn2_norm_pre, ffn2_w_gate_up, ffn2_w_down,
              ffn2_norm_post):
    split_at = [Q_DIM, Q_DIM + KV_DIM, Q_DIM + 2 * KV_DIM,
                Q_DIM + 2 * KV_DIM + CONV_DIM, Q_DIM + 2 * KV_DIM + 2 * CONV_DIM]
    for l in range(DEPTH):
        h = swiglu(rms_norm(x, ffn1_norm_pre[l]), ffn1_w_gate_up[l], ffn1_w_down[l])
        x = x + FFN_RESIDUAL_WEIGHT * rms_norm(h, ffn1_norm_post[l])
        h = rms_norm(x, mix_norm_pre[l])
        z = h @ w_in[l]
        q, k, v, hc, b_gate, c_gate = jnp.split(z, split_at, axis=-1)
        a = dilated_attention(q, k, v)
        c = gated_short_conv(hc, b_gate, c_gate, conv_w[l])
        mixed = jnp.concatenate([rms_norm(a, attn_out_norm[l]),
                                 rms_norm(c, conv_out_norm[l])], axis=-1) @ w_out[l]
        x = x + rms_norm(mixed, mix_norm_post[l])
        h = swiglu(rms_norm(x, ffn2_norm_pre[l]), ffn2_w_gate_up[l], ffn2_w_down[l])
        x = x + FFN_RESIDUAL_WEIGHT * rms_norm(h, ffn2_norm_post[l])
    return x


import jax as _jax
import jax.numpy as _jnp

TWIN_FORMAT = 'train_step'
FWD_PARAMS = ['x', 'ffn1_norm_pre', 'ffn1_w_gate_up', 'ffn1_w_down', 'ffn1_norm_post', 'mix_norm_pre', 'w_in', 'conv_w', 'attn_out_norm', 'conv_out_norm', 'w_out', 'mix_norm_post', 'ffn2_norm_pre', 'ffn2_w_gate_up', 'ffn2_w_down', 'ffn2_norm_post']
TWIN_WEIGHTS = ['ffn1_norm_pre', 'ffn1_w_gate_up', 'ffn1_w_down', 'ffn1_norm_post', 'mix_norm_pre', 'w_in', 'conv_w', 'attn_out_norm', 'conv_out_norm', 'w_out', 'mix_norm_post', 'ffn2_norm_pre', 'ffn2_w_gate_up', 'ffn2_w_down', 'ffn2_norm_post']
TWIN_DIFF_INPUT = 'x'
TWIN_INPUTS = ['x', 'ffn1_norm_pre', 'ffn1_w_gate_up', 'ffn1_w_down', 'ffn1_norm_post', 'mix_norm_pre', 'w_in', 'conv_w', 'attn_out_norm', 'conv_out_norm', 'w_out', 'mix_norm_post', 'ffn2_norm_pre', 'ffn2_w_gate_up', 'ffn2_w_down', 'ffn2_norm_post', 'loss_target', 'm_ffn1_norm_pre', 'm_ffn1_w_gate_up', 'm_ffn1_w_down', 'm_ffn1_norm_post', 'm_mix_norm_pre', 'm_w_in', 'm_conv_w', 'm_attn_out_norm', 'm_conv_out_norm', 'm_w_out', 'm_mix_norm_post', 'm_ffn2_norm_pre', 'm_ffn2_w_gate_up', 'm_ffn2_w_down', 'm_ffn2_norm_post', 'v_ffn1_norm_pre', 'v_ffn1_w_gate_up', 'v_ffn1_w_down', 'v_ffn1_norm_post', 'v_mix_norm_pre', 'v_w_in', 'v_conv_w', 'v_attn_out_norm', 'v_conv_out_norm', 'v_w_out', 'v_mix_norm_post', 'v_ffn2_norm_pre', 'v_ffn2_w_gate_up', 'v_ffn2_w_down', 'v_ffn2_norm_post']
TWIN_OUTPUTS = ['loss', 'grad_x', 'grad_ffn1_norm_pre', 'grad_ffn1_w_gate_up', 'grad_ffn1_w_down', 'grad_ffn1_norm_post', 'grad_mix_norm_pre', 'grad_w_in', 'grad_conv_w', 'grad_attn_out_norm', 'grad_conv_out_norm', 'grad_w_out', 'grad_mix_norm_post', 'grad_ffn2_norm_pre', 'grad_ffn2_w_gate_up', 'grad_ffn2_w_down', 'grad_ffn2_norm_post', 'delta_ffn1_norm_pre', 'delta_ffn1_w_gate_up', 'delta_ffn1_w_down', 'delta_ffn1_norm_post', 'delta_mix_norm_pre', 'delta_w_in', 'delta_conv_w', 'delta_attn_out_norm', 'delta_conv_out_norm', 'delta_w_out', 'delta_mix_norm_post', 'delta_ffn2_norm_pre', 'delta_ffn2_w_gate_up', 'delta_ffn2_w_down', 'delta_ffn2_norm_post', 'new_m_ffn1_norm_pre', 'new_m_ffn1_w_gate_up', 'new_m_ffn1_w_down', 'new_m_ffn1_norm_post', 'new_m_mix_norm_pre', 'new_m_w_in', 'new_m_conv_w', 'new_m_attn_out_norm', 'new_m_conv_out_norm', 'new_m_w_out', 'new_m_mix_norm_post', 'new_m_ffn2_norm_pre', 'new_m_ffn2_w_gate_up', 'new_m_ffn2_w_down', 'new_m_ffn2_norm_post', 'new_v_ffn1_norm_pre', 'new_v_ffn1_w_gate_up', 'new_v_ffn1_w_down', 'new_v_ffn1_norm_post', 'new_v_mix_norm_pre', 'new_v_w_in', 'new_v_conv_w', 'new_v_attn_out_norm', 'new_v_conv_out_norm', 'new_v_w_out', 'new_v_mix_norm_post', 'new_v_ffn2_norm_pre', 'new_v_ffn2_w_gate_up', 'new_v_ffn2_w_down', 'new_v_ffn2_norm_post']
TWIN_LEAF_KINDS = {'loss': 'loss', 'grad_x': 'grad_x', 'grad_ffn1_norm_pre': 'grad_w', 'grad_ffn1_w_gate_up': 'grad_w', 'grad_ffn1_w_down': 'grad_w', 'grad_ffn1_norm_post': 'grad_w', 'grad_mix_norm_pre': 'grad_w', 'grad_w_in': 'grad_w', 'grad_conv_w': 'grad_w', 'grad_attn_out_norm': 'grad_w', 'grad_conv_out_norm': 'grad_w', 'grad_w_out': 'grad_w', 'grad_mix_norm_post': 'grad_w', 'grad_ffn2_norm_pre': 'grad_w', 'grad_ffn2_w_gate_up': 'grad_w', 'grad_ffn2_w_down': 'grad_w', 'grad_ffn2_norm_post': 'grad_w', 'delta_ffn1_norm_pre': 'delta_w', 'delta_ffn1_w_gate_up': 'delta_w', 'delta_ffn1_w_down': 'delta_w', 'delta_ffn1_norm_post': 'delta_w', 'delta_mix_norm_pre': 'delta_w', 'delta_w_in': 'delta_w', 'delta_conv_w': 'delta_w', 'delta_attn_out_norm': 'delta_w', 'delta_conv_out_norm': 'delta_w', 'delta_w_out': 'delta_w', 'delta_mix_norm_post': 'delta_w', 'delta_ffn2_norm_pre': 'delta_w', 'delta_ffn2_w_gate_up': 'delta_w', 'delta_ffn2_w_down': 'delta_w', 'delta_ffn2_norm_post': 'delta_w', 'new_m_ffn1_norm_pre': 'new_m', 'new_m_ffn1_w_gate_up': 'new_m', 'new_m_ffn1_w_down': 'new_m', 'new_m_ffn1_norm_post': 'new_m', 'new_m_mix_norm_pre': 'new_m', 'new_m_w_in': 'new_m', 'new_m_conv_w': 'new_m', 'new_m_attn_out_norm': 'new_m', 'new_m_conv_out_norm': 'new_m', 'new_m_w_out': 'new_m', 'new_m_mix_norm_post': 'new_m', 'new_m_ffn2_norm_pre': 'new_m', 'new_m_ffn2_w_gate_up': 'new_m', 'new_m_ffn2_w_down': 'new_m', 'new_m_ffn2_norm_post': 'new_m', 'new_v_ffn1_norm_pre': 'new_v', 'new_v_ffn1_w_gate_up': 'new_v', 'new_v_ffn1_w_down': 'new_v', 'new_v_ffn1_norm_post': 'new_v', 'new_v_mix_norm_pre': 'new_v', 'new_v_w_in': 'new_v', 'new_v_conv_w': 'new_v', 'new_v_attn_out_norm': 'new_v', 'new_v_conv_out_norm': 'new_v', 'new_v_w_out': 'new_v', 'new_v_mix_norm_post': 'new_v', 'new_v_ffn2_norm_pre': 'new_v', 'new_v_ffn2_w_gate_up': 'new_v', 'new_v_ffn2_w_down': 'new_v', 'new_v_ffn2_norm_post': 'new_v'}


def _forward(args):
    return _fwd_reference(*[args[k] for k in FWD_PARAMS])


def _output_shape():
    out = _jax.eval_shape(lambda: _forward(_fwd_setup_inputs(0)))
    return out.shape, out.dtype

N_MICROBATCH = 1
ADAM_LR = 0.001
ADAM_B1 = 0.9
ADAM_B2 = 0.999
ADAM_EPS = 1e-08
ADAM_WD = 0.01
ADAM_STEP = 10
PER_EXAMPLE_BATCH_AXIS = {'x': 0, 'loss_target': 0}
SHARED_INPUTS = []
_WEIGHT_DTYPES = {'ffn1_norm_pre': _jnp.float32, 'ffn1_w_gate_up': _jnp.float32, 'ffn1_w_down': _jnp.float32, 'ffn1_norm_post': _jnp.float32, 'mix_norm_pre': _jnp.float32, 'w_in': _jnp.float32, 'conv_w': _jnp.float32, 'attn_out_norm': _jnp.float32, 'conv_out_norm': _jnp.float32, 'w_out': _jnp.float32, 'mix_norm_post': _jnp.float32, 'ffn2_norm_pre': _jnp.float32, 'ffn2_w_gate_up': _jnp.float32, 'ffn2_w_down': _jnp.float32, 'ffn2_norm_post': _jnp.float32}
MOMENT_SCALE = {'ffn1_norm_pre': 3.049775e-01, 'ffn1_w_gate_up': 1.249792e-01, 'ffn1_w_down': 2.078516e-01, 'ffn1_norm_post': 1.945691e+00, 'mix_norm_pre': 4.339408e-01, 'w_in': 2.940538e-01, 'conv_w': 1.896194e-01, 'attn_out_norm': 4.791569e-01, 'conv_out_norm': 1.992443e-01, 'w_out': 3.661751e-01, 'mix_norm_post': 7.952906e+00, 'ffn2_norm_pre': 1.532177e-01, 'ffn2_w_gate_up': 6.424067e-02, 'ffn2_w_down': 1.146090e-01, 'ffn2_norm_post': 1.978537e+00}


def _to_microbatches(a, axis):
    t = _jnp.moveaxis(a, axis, 0)
    t = t.reshape((N_MICROBATCH, t.shape[0] // N_MICROBATCH) + t.shape[1:])
    return _jnp.moveaxis(t, 1, axis + 1)


def setup_inputs(seed: int = 0) -> dict:
    inp = _fwd_setup_inputs(seed)
    key = _jax.random.fold_in(_jax.random.key(seed), 7919)
    shape, _ = _output_shape()
    out = dict(inp)
    out["loss_target"] = _jax.random.normal(_jax.random.fold_in(key, 0), shape, _jnp.float32)
    for i, name in enumerate(TWIN_WEIGHTS):
        w = inp[name].astype(_jnp.float32)
        if MOMENT_SCALE is None:
            s = _jnp.sqrt(_jnp.mean(_jnp.square(w)) + 1e-30)
        else:
            s = MOMENT_SCALE[name]
        km, kv = _jax.random.split(_jax.random.fold_in(key, i + 1))
        out[name] = w
        out["m_" + name] = s * _jax.random.normal(km, w.shape, _jnp.float32)
        out["v_" + name] = (s * s) * _jax.random.uniform(kv, w.shape, _jnp.float32, 0.5, 1.5)
    if N_MICROBATCH > 1:
        for name, axis in PER_EXAMPLE_BATCH_AXIS.items():
            out[name] = _to_microbatches(out[name], axis)
    return {'x': out['x'], 'ffn1_norm_pre': out['ffn1_norm_pre'], 'ffn1_w_gate_up': out['ffn1_w_gate_up'], 'ffn1_w_down': out['ffn1_w_down'], 'ffn1_norm_post': out['ffn1_norm_post'], 'mix_norm_pre': out['mix_norm_pre'], 'w_in': out['w_in'], 'conv_w': out['conv_w'], 'attn_out_norm': out['attn_out_norm'], 'conv_out_norm': out['conv_out_norm'], 'w_out': out['w_out'], 'mix_norm_post': out['mix_norm_post'], 'ffn2_norm_pre': out['ffn2_norm_pre'], 'ffn2_w_gate_up': out['ffn2_w_gate_up'], 'ffn2_w_down': out['ffn2_w_down'], 'ffn2_norm_post': out['ffn2_norm_post'], 'loss_target': out['loss_target'], 'm_ffn1_norm_pre': out['m_ffn1_norm_pre'], 'm_ffn1_w_gate_up': out['m_ffn1_w_gate_up'], 'm_ffn1_w_down': out['m_ffn1_w_down'], 'm_ffn1_norm_post': out['m_ffn1_norm_post'], 'm_mix_norm_pre': out['m_mix_norm_pre'], 'm_w_in': out['m_w_in'], 'm_conv_w': out['m_conv_w'], 'm_attn_out_norm': out['m_attn_out_norm'], 'm_conv_out_norm': out['m_conv_out_norm'], 'm_w_out': out['m_w_out'], 'm_mix_norm_post': out['m_mix_norm_post'], 'm_ffn2_norm_pre': out['m_ffn2_norm_pre'], 'm_ffn2_w_gate_up': out['m_ffn2_w_gate_up'], 'm_ffn2_w_down': out['m_ffn2_w_down'], 'm_ffn2_norm_post': out['m_ffn2_norm_post'], 'v_ffn1_norm_pre': out['v_ffn1_norm_pre'], 'v_ffn1_w_gate_up': out['v_ffn1_w_gate_up'], 'v_ffn1_w_down': out['v_ffn1_w_down'], 'v_ffn1_norm_post': out['v_ffn1_norm_post'], 'v_mix_norm_pre': out['v_mix_norm_pre'], 'v_w_in': out['v_w_in'], 'v_conv_w': out['v_conv_w'], 'v_attn_out_norm': out['v_attn_out_norm'], 'v_conv_out_norm': out['v_conv_out_norm'], 'v_w_out': out['v_w_out'], 'v_mix_norm_post': out['v_mix_norm_post'], 'v_ffn2_norm_pre': out['v_ffn2_norm_pre'], 'v_ffn2_w_gate_up': out['v_ffn2_w_gate_up'], 'v_ffn2_w_down': out['v_ffn2_w_down'], 'v_ffn2_norm_post': out['v_ffn2_norm_post']}


def _loss(weights, diff, rest, loss_target):
    with _jax.named_scope("forward"):
        args = {**rest, TWIN_DIFF_INPUT: diff, **{k: w.astype(_WEIGHT_DTYPES[k]) for k, w in weights.items()}}
        y = _forward(args)
    with _jax.named_scope("loss_head"):
        err = _jnp.square(y.astype(_jnp.float32) - loss_target)
        return 0.5 * _jnp.sum(_jnp.mean(err, axis=-1)) if err.ndim else 0.5 * err


def _adamw(w, g, m, v):
    m = ADAM_B1 * m + (1.0 - ADAM_B1) * g
    v = ADAM_B2 * v + (1.0 - ADAM_B2) * _jnp.square(g)
    m_hat = m / (1.0 - ADAM_B1 ** ADAM_STEP)
    v_hat = v / (1.0 - ADAM_B2 ** ADAM_STEP)
    delta = -ADAM_LR * (m_hat / (_jnp.sqrt(v_hat) + ADAM_EPS) + ADAM_WD * w)
    return delta, m, v


def reference(x, ffn1_norm_pre, ffn1_w_gate_up, ffn1_w_down, ffn1_norm_post, mix_norm_pre, w_in, conv_w, attn_out_norm, conv_out_norm, w_out, mix_norm_post, ffn2_norm_pre, ffn2_w_gate_up, ffn2_w_down, ffn2_norm_post, loss_target, m_ffn1_norm_pre, m_ffn1_w_gate_up, m_ffn1_w_down, m_ffn1_norm_post, m_mix_norm_pre, m_w_in, m_conv_w, m_attn_out_norm, m_conv_out_norm, m_w_out, m_mix_norm_post, m_ffn2_norm_pre, m_ffn2_w_gate_up, m_ffn2_w_down, m_ffn2_norm_post, v_ffn1_norm_pre, v_ffn1_w_gate_up, v_ffn1_w_down, v_ffn1_norm_post, v_mix_norm_pre, v_w_in, v_conv_w, v_attn_out_norm, v_conv_out_norm, v_w_out, v_mix_norm_post, v_ffn2_norm_pre, v_ffn2_w_gate_up, v_ffn2_w_down, v_ffn2_norm_post):
    given = dict(x=x, ffn1_norm_pre=ffn1_norm_pre, ffn1_w_gate_up=ffn1_w_gate_up, ffn1_w_down=ffn1_w_down, ffn1_norm_post=ffn1_norm_post, mix_norm_pre=mix_norm_pre, w_in=w_in, conv_w=conv_w, attn_out_norm=attn_out_norm, conv_out_norm=conv_out_norm, w_out=w_out, mix_norm_post=mix_norm_post, ffn2_norm_pre=ffn2_norm_pre, ffn2_w_gate_up=ffn2_w_gate_up, ffn2_w_down=ffn2_w_down, ffn2_norm_post=ffn2_norm_post, loss_target=loss_target, m_ffn1_norm_pre=m_ffn1_norm_pre, m_ffn1_w_gate_up=m_ffn1_w_gate_up, m_ffn1_w_down=m_ffn1_w_down, m_ffn1_norm_post=m_ffn1_norm_post, m_mix_norm_pre=m_mix_norm_pre, m_w_in=m_w_in, m_conv_w=m_conv_w, m_attn_out_norm=m_attn_out_norm, m_conv_out_norm=m_conv_out_norm, m_w_out=m_w_out, m_mix_norm_post=m_mix_norm_post, m_ffn2_norm_pre=m_ffn2_norm_pre, m_ffn2_w_gate_up=m_ffn2_w_gate_up, m_ffn2_w_down=m_ffn2_w_down, m_ffn2_norm_post=m_ffn2_norm_post, v_ffn1_norm_pre=v_ffn1_norm_pre, v_ffn1_w_gate_up=v_ffn1_w_gate_up, v_ffn1_w_down=v_ffn1_w_down, v_ffn1_norm_post=v_ffn1_norm_post, v_mix_norm_pre=v_mix_norm_pre, v_w_in=v_w_in, v_conv_w=v_conv_w, v_attn_out_norm=v_attn_out_norm, v_conv_out_norm=v_conv_out_norm, v_w_out=v_w_out, v_mix_norm_post=v_mix_norm_post, v_ffn2_norm_pre=v_ffn2_norm_pre, v_ffn2_w_gate_up=v_ffn2_w_gate_up, v_ffn2_w_down=v_ffn2_w_down, v_ffn2_norm_post=v_ffn2_norm_post)
    weights = {n: given[n] for n in TWIN_WEIGHTS}
    shared = {n: given[n] for n in SHARED_INPUTS}
    per_example = {n: given[n] for n in ['x']}
    grad_fn = _jax.value_and_grad(_loss, argnums=(0, 1))

    def one_microbatch(ex, loss_target):
        ex = dict(ex)
        diff = ex.pop(TWIN_DIFF_INPUT)
        return grad_fn(weights, diff, {**shared, **ex}, loss_target)

    if N_MICROBATCH == 1:
        loss, (grad_w, grad_x) = one_microbatch(per_example, given["loss_target"])
    else:
        def body(carry, xs):
            loss_sum, grad_sum = carry
            l_k, (gw_k, gx_k) = one_microbatch(xs[0], xs[1])
            with _jax.named_scope("update"):
                return (loss_sum + l_k, _jax.tree.map(_jnp.add, grad_sum, gw_k)), gx_k

        init = (_jnp.zeros((), _jnp.float32), _jax.tree.map(_jnp.zeros_like, weights))
        (loss, grad_w), grad_x = _jax.lax.scan(body, init, (per_example, given["loss_target"]))
    with _jax.named_scope("update"):
        delta_w, new_m, new_v = {}, {}, {}
        for n in TWIN_WEIGHTS:
            delta_w[n], new_m[n], new_v[n] = _adamw(weights[n], grad_w[n], given["m_" + n], given["v_" + n])
    return (loss, grad_x, *[grad_w[n] for n in TWIN_WEIGHTS], *[delta_w[n] for n in TWIN_WEIGHTS],
            *[new_m[n] for n in TWIN_WEIGHTS], *[new_v[n] for n in TWIN_WEIGHTS])
```

```python
import functools

import jax
import jax.numpy as jnp
from jax import lax
from jax.experimental import pallas as pl
from jax.experimental.pallas import tpu as pltpu

F32 = jnp.float32
BF16 = jnp.bfloat16
MESH = pl.DeviceIdType.MESH

NORM_EPS = 1e-6
HEAD_DIM = 128
Q_PER_KV = 4
CONV_WIDTH = 3
FFN_RESIDUAL_WEIGHT = 0.5
DILATED_BRANCHES = ((128, 1), (512, 4), (2048, 16))
ADAM_LR = 0.001
ADAM_B1 = 0.9
ADAM_B2 = 0.999
ADAM_EPS = 1e-08
ADAM_WD = 0.01
ADAM_STEP = 10

N_CHIPS = 4
N_DEV = 8
V7X_VMEM_BYTES = 64 << 20
VMEM_LIMIT = V7X_VMEM_BYTES - (12 << 20)
SUBLANES = 8
LANES = 128
SMALL_ROWS = 16


def _params(*sem):
    return pltpu.CompilerParams(dimension_semantics=sem, vmem_limit_bytes=VMEM_LIMIT)


def _row_tile(rows, cols, itemsize=4, budget=2 << 20):
    t = rows
    while t * cols * itemsize > budget and t % 32 == 0:
        t //= 2
    return t


def _sum_to_sublanes(v):
    r, n = v.shape
    return v.reshape(r // SUBLANES, SUBLANES, n).sum(axis=0)


_DIMS = {
    "nn": (((1,), (0,)), ((), ())),
    "nt": (((1,), (1,)), ((), ())),
    "tn": (((0,), (0,)), ((), ())),
}


def _dot(a, b, mode):
    return lax.dot_general(a, b, _DIMS[mode], preferred_element_type=F32)


def _mm(name, a, b, *, mode, grid, a_spec, b_spec, o_spec, out_shape, nk=1, acc_shape=None):
    def body(a_ref, b_ref, o_ref, *scratch):
        r = _dot(a_ref[...], b_ref[...], mode)
        if nk == 1:
            o_ref[...] = r.astype(o_ref.dtype)
        else:
            acc = scratch[0]
            k = pl.program_id(len(grid) - 1)

            @pl.when(k == 0)
            def _():
                acc[...] = r

            @pl.when(k > 0)
            def _():
                acc[...] += r

            @pl.when(k == nk - 1)
            def _():
                o_ref[...] = acc[...].astype(o_ref.dtype)

    sem = ("parallel",) * (len(grid) - (1 if nk > 1 else 0)) + (("arbitrary",) if nk > 1 else ())
    return pl.pallas_call(
        body, name=name, grid=grid, in_specs=[a_spec, b_spec], out_specs=o_spec, out_shape=out_shape,
        scratch_shapes=[pltpu.VMEM(acc_shape, F32)] if nk > 1 else [],
        compiler_params=_params(*sem),
    )(a, b)


def _tile(n, want):
    if n <= want:
        return n
    best = None
    for t in range(LANES, want + 1, LANES):
        if n % t == 0:
            best = t
    assert best is not None, (n, want)
    return best


def _norm_fwd(name, x, gain):
    s, d = x.shape
    tr = _row_tile(s, d)

    def body(x_ref, g_ref, o_ref):
        xv = x_ref[...]
        r = lax.rsqrt(jnp.mean(xv * xv, axis=-1, keepdims=True) + NORM_EPS)
        o_ref[...] = (xv * r * g_ref[...]).astype(o_ref.dtype)

    return pl.pallas_call(
        body, name=name, grid=(s // tr,),
        in_specs=[pl.BlockSpec((tr, d), lambda i: (i, 0)), pl.BlockSpec((1, d), lambda i: (0, 0))],
        out_specs=pl.BlockSpec((tr, d), lambda i: (i, 0)),
        out_shape=jax.ShapeDtypeStruct((s, d), BF16), compiler_params=_params("parallel"),
    )(x, gain)


def _res_norm(name, x, y, gain, scale):
    s, d = x.shape
    tr = _row_tile(s, d)

    def body(x_ref, y_ref, g_ref, o_ref):
        yv = y_ref[...]
        r = lax.rsqrt(jnp.mean(yv * yv, axis=-1, keepdims=True) + NORM_EPS)
        o_ref[...] = x_ref[...] + scale * (yv * r * g_ref[...])

    row = pl.BlockSpec((tr, d), lambda i: (i, 0))
    return pl.pallas_call(
        body, name=name, grid=(s // tr,),
        in_specs=[row, row, pl.BlockSpec((1, d), lambda i: (0, 0))], out_specs=row,
        out_shape=jax.ShapeDtypeStruct((s, d), F32), compiler_params=_params("parallel"),
    )(x, y, gain)


def _norm_bwd(name, dout, yin, gain, scale, resid, out_dtype):
    s, d = yin.shape
    tr = _row_tile(s, d)
    has_resid = resid is not None

    def body(*refs):
        if has_resid:
            do_ref, y_ref, g_ref, r_ref, di_ref, dg_ref = refs
        else:
            do_ref, y_ref, g_ref, di_ref, dg_ref = refs
        yv = y_ref[...]
        r = lax.rsqrt(jnp.mean(yv * yv, axis=-1, keepdims=True) + NORM_EPS)
        xhat = yv * r
        dn = scale * do_ref[...]
        part = _sum_to_sublanes(dn * xhat)

        @pl.when(pl.program_id(0) == 0)
        def _():
            dg_ref[...] = part

        @pl.when(pl.program_id(0) > 0)
        def _():
            dg_ref[...] += part

        dxn = dn * g_ref[...]
        din = r * (dxn - xhat * jnp.mean(dxn * xhat, axis=-1, keepdims=True))
        if has_resid:
            din = din + r_ref[...]
        di_ref[...] = din.astype(di_ref.dtype)

    row = pl.BlockSpec((tr, d), lambda i: (i, 0))
    vec = pl.BlockSpec((1, d), lambda i: (0, 0))
    ins = [row, row, vec] + ([row] if has_resid else [])
    args = (dout, yin, gain) + ((resid,) if has_resid else ())
    return pl.pallas_call(
        body, name=name, grid=(s // tr,), in_specs=ins,
        out_specs=[row, pl.BlockSpec((SUBLANES, d), lambda i: (0, 0))],
        out_shape=[jax.ShapeDtypeStruct((s, d), out_dtype), jax.ShapeDtypeStruct((SUBLANES, d), F32)],
        compiler_params=_params("arbitrary"),
    )(*args)


def _loss_head(name, y, target):
    s, d = y.shape
    tr = _row_tile(s, d)

    def body(y_ref, t_ref, dy_ref, l_ref):
        e = y_ref[...] - t_ref[...]
        dy_ref[...] = e * (1.0 / d)
        part = _sum_to_sublanes(e * e) * (0.5 / d)

        @pl.when(pl.program_id(0) == 0)
        def _():
            l_ref[...] = part

        @pl.when(pl.program_id(0) > 0)
        def _():
            l_ref[...] += part

    row = pl.BlockSpec((tr, d), lambda i: (i, 0))
    return pl.pallas_call(
        body, name=name, grid=(s // tr,), in_specs=[row, row],
        out_specs=[row, pl.BlockSpec((SUBLANES, d), lambda i: (0, 0))],
        out_shape=[jax.ShapeDtypeStruct((s, d), F32), jax.ShapeDtypeStruct((SUBLANES, d), F32)],
        compiler_params=_params("arbitrary"),
    )(y, target)


def _ffn_up(name, h, gu_w, layer):
    s, d = h.shape
    nl, nb, _, fs = gu_w.shape
    hb = nb // 2
    w = gu_w.reshape(nl, 2, hb, d, fs)
    tm = _tile(s, 512)
    tn = _tile(fs, 1408)
    nj = fs // tn

    def body(h_ref, w_ref, gu_ref, a_ref):
        hv = h_ref[...]
        g = _dot(hv, w_ref[0], "nn")
        u = _dot(hv, w_ref[1], "nn")
        gu_ref[0] = g.astype(gu_ref.dtype)
        gu_ref[1] = u.astype(gu_ref.dtype)
        a_ref[...] = (g * jax.nn.sigmoid(g) * u).astype(a_ref.dtype)

    return pl.pallas_call(
        body, name=name, grid=(hb, nj, s // tm),
        in_specs=[pl.BlockSpec((tm, d), lambda jb, jo, i: (i, 0)),
                  pl.BlockSpec((None, 2, None, d, tn), lambda jb, jo, i: (layer, 0, jb, 0, jo))],
        out_specs=[pl.BlockSpec((2, None, tm, tn), lambda jb, jo, i: (0, jb, i, jo)),
                   pl.BlockSpec((tm, tn), lambda jb, jo, i: (i, jb * nj + jo))],
        out_shape=[jax.ShapeDtypeStruct((2, hb, s, fs), BF16), jax.ShapeDtypeStruct((s, hb * fs), BF16)],
        compiler_params=_params("parallel", "parallel", "parallel"),
    )(h, w)


def _ffn_dact(name, dy, dn_w, gu, layer):
    s, d = dy.shape
    _, hb, _, fs = gu.shape
    tm = _tile(s, 512)
    tn = _tile(fs, 1408)
    nj = fs // tn

    def body(dy_ref, w_ref, gu_ref, o_ref):
        da = _dot(dy_ref[...], w_ref[...], "nt")
        g = gu_ref[0].astype(F32)
        u = gu_ref[1].astype(F32)
        sg = jax.nn.sigmoid(g)
        o_ref[0] = (da * u * (sg * (1.0 + g * (1.0 - sg)))).astype(o_ref.dtype)
        o_ref[1] = (da * (g * sg)).astype(o_ref.dtype)

    blk = pl.BlockSpec((2, None, tm, tn), lambda jb, jo, i: (0, jb, i, jo))
    return pl.pallas_call(
        body, name=name, grid=(hb, nj, s // tm),
        in_specs=[pl.BlockSpec((tm, d), lambda jb, jo, i: (i, 0)),
                  pl.BlockSpec((None, tn, d), lambda jb, jo, i: (layer, jb * nj + jo, 0)),
                  blk],
        out_specs=blk, out_shape=jax.ShapeDtypeStruct(gu.shape, BF16),
        compiler_params=_params("parallel", "parallel", "parallel"),
    )(dy, dn_w, gu)


def _multiplicity(q0, tq, s):
    row = q0 + lax.broadcasted_iota(jnp.int32, (tq, s), 0)
    col = lax.broadcasted_iota(jnp.int32, (tq, s), 1)
    dist = row - col
    mult = jnp.zeros((tq, s), F32)
    for window, dilation in DILATED_BRANCHES:
        hit = (dist <= window) & ((dist & (dilation - 1)) == 0)
        mult = mult + hit.astype(F32)
    return jnp.where(dist >= 0, mult, 0.0)


_MASKED = -1e30


def _attn_specs(s, qd, kvd, tq):
    rw = Q_PER_KV * HEAD_DIM
    qspec = pl.BlockSpec((tq, rw), lambda g, i: (i, g))
    kspec = pl.BlockSpec((s, HEAD_DIM), lambda g, i: (0, qd // HEAD_DIM + g))
    vspec = pl.BlockSpec((s, HEAD_DIM), lambda g, i: (0, (qd + kvd) // HEAD_DIM + g))
    return rw, qspec, kspec, vspec


def _attn_fwd(name, z, qd, kvd):
    s = z.shape[0]
    tq = _tile(s, 256)
    nkv = kvd // HEAD_DIM
    rw, qspec, kspec, vspec = _attn_specs(s, qd, kvd, tq)
    scale = HEAD_DIM ** -0.5

    def body(q_ref, k_ref, v_ref, o_ref, l_ref):
        mult = _multiplicity(pl.program_id(1) * tq, tq, s)
        live = mult > 0.0
        kv, vv = k_ref[...], v_ref[...]
        for h in range(Q_PER_KV):
            cols = slice(h * HEAD_DIM, (h + 1) * HEAD_DIM)
            sc = jnp.where(live, _dot(q_ref[:, cols], kv, "nt") * scale, _MASKED)
            mx = jnp.max(sc, axis=-1, keepdims=True)
            p = jnp.exp(sc - mx) * mult
            den = jnp.sum(p, axis=-1, keepdims=True)
            o_ref[:, cols] = _dot(p.astype(BF16), vv, "nn") / den
            l_ref[:, cols] = jnp.broadcast_to(mx + jnp.log(den), (tq, HEAD_DIM))

    return pl.pallas_call(
        body, name=name, grid=(nkv, s // tq), in_specs=[qspec, kspec, vspec], out_specs=[qspec, qspec],
        out_shape=[jax.ShapeDtypeStruct((s, qd), F32), jax.ShapeDtypeStruct((s, qd), F32)],
        compiler_params=_params("parallel", "parallel"),
    )(z, z, z)


def _attn_bwd(name, z, o, lse, do, qd, kvd):
    s = z.shape[0]
    tq = _tile(s, 256)
    nkv = kvd // HEAD_DIM
    nq = s // tq
    rw, qspec, kspec, vspec = _attn_specs(s, qd, kvd, tq)
    scale = HEAD_DIM ** -0.5

    def body(q_ref, k_ref, v_ref, o_ref, l_ref, do_ref, dq_ref, dk_ref, dv_ref, dk_acc, dv_acc):
        i = pl.program_id(1)
        mult = _multiplicity(i * tq, tq, s)
        live = mult > 0.0
        kv, vv = k_ref[...], v_ref[...]

        @pl.when(i == 0)
        def _():
            dk_acc[...] = jnp.zeros_like(dk_acc)
            dv_acc[...] = jnp.zeros_like(dv_acc)

        for h in range(Q_PER_KV):
            cols = slice(h * HEAD_DIM, (h + 1) * HEAD_DIM)
            q = q_ref[:, cols]
            dov = do_ref[:, cols]
            sc = jnp.where(live, _dot(q, kv, "nt") * scale, _MASKED)
            p = jnp.exp(sc - l_ref[:, cols][:, :1]) * mult
            dob = dov.astype(BF16)
            dp = _dot(dob, vv, "nt")
            delta = jnp.sum(dov * o_ref[:, cols], axis=-1, keepdims=True)
            ds = (p * (dp - delta) * scale).astype(BF16)
            dq_ref[:, cols] = _dot(ds, kv, "nn").astype(dq_ref.dtype)
            dk_acc[...] += _dot(ds, q, "tn")
            dv_acc[...] += _dot(p.astype(BF16), dob, "tn")

        @pl.when(i == nq - 1)
        def _():
            dk_ref[...] = dk_acc[...].astype(dk_ref.dtype)
            dv_ref[...] = dv_acc[...].astype(dv_ref.dtype)

    kvout = pl.BlockSpec((s, HEAD_DIM), lambda g, i: (0, g))
    return pl.pallas_call(
        body, name=name, grid=(nkv, nq), in_specs=[qspec, kspec, vspec, qspec, qspec, qspec],
        out_specs=[qspec, kvout, kvout],
        out_shape=[jax.ShapeDtypeStruct((s, qd), BF16), jax.ShapeDtypeStruct((s, kvd), BF16),
                   jax.ShapeDtypeStruct((s, kvd), BF16)],
        scratch_shapes=[pltpu.VMEM((s, HEAD_DIM), F32), pltpu.VMEM((s, HEAD_DIM), F32)],
        compiler_params=_params("parallel", "arbitrary"),
    )(z, z, z, o, lse, do)


def _shift_down(v, n):
    rolled = pltpu.roll(v, n, 0)
    t = lax.broadcasted_iota(jnp.int32, v.shape, 0)
    return jnp.where(t >= n, rolled, 0.0)


def _shift_up(v, n):
    rows = v.shape[0]
    rolled = pltpu.roll(v, rows - n, 0)
    t = lax.broadcasted_iota(jnp.int32, v.shape, 0)
    return jnp.where(t < rows - n, rolled, 0.0)


def _conv_specs(s, base, cd, tc):
    zs = [pl.BlockSpec((s, tc), functools.partial(lambda j, off: (0, off + j), off=(base + n * cd) // tc))
          for n in range(3)]
    wspec = pl.BlockSpec((SUBLANES, tc), lambda j: (0, j))
    cspec = pl.BlockSpec((s, tc), lambda j: (0, j))
    return zs, wspec, cspec


def _conv_fwd(name, z, conv_w, base, cd):
    s = z.shape[0]
    tc = _tile(cd, 256)
    zs, wspec, cspec = _conv_specs(s, base, cd, tc)

    def body(h_ref, b_ref, c_ref, w_ref, o_ref):
        u = c_ref[...].astype(F32) * h_ref[...].astype(F32)
        y = w_ref[0:1, :] * _shift_down(u, 2) + w_ref[1:2, :] * _shift_down(u, 1) + w_ref[2:3, :] * u
        o_ref[...] = b_ref[...].astype(F32) * y

    return pl.pallas_call(
        body, name=name, grid=(cd // tc,), in_specs=zs + [wspec], out_specs=cspec,
        out_shape=jax.ShapeDtypeStruct((s, cd), F32), compiler_params=_params("parallel"),
    )(z, z, z, conv_w)


def _conv_bwd(name, z, conv_w, dc, base, cd):
    s = z.shape[0]
    tc = _tile(cd, 256)
    zs, wspec, cspec = _conv_specs(s, base, cd, tc)

    def body(h_ref, b_ref, c_ref, w_ref, dc_ref, dh_ref, db_ref, dcg_ref, dw_ref):
        hv, bv, cv = h_ref[...].astype(F32), b_ref[...].astype(F32), c_ref[...].astype(F32)
        u = cv * hv
        u1, u2 = _shift_down(u, 1), _shift_down(u, 2)
        w0, w1, w2 = w_ref[0:1, :], w_ref[1:2, :], w_ref[2:3, :]
        y = w0 * u2 + w1 * u1 + w2 * u
        dcv = dc_ref[...]
        db_ref[...] = (dcv * y).astype(db_ref.dtype)
        dy = dcv * bv
        du = w2 * dy + w1 * _shift_up(dy, 1) + w0 * _shift_up(dy, 2)
        dh_ref[...] = (du * cv).astype(dh_ref.dtype)
        dcg_ref[...] = (du * hv).astype(dcg_ref.dtype)
        g0 = jnp.sum(dy * u2, axis=0, keepdims=True)
        g1 = jnp.sum(dy * u1, axis=0, keepdims=True)
        g2 = jnp.sum(dy * u, axis=0, keepdims=True)
        r = lax.broadcasted_iota(jnp.int32, (SUBLANES, tc), 0)
        dw_ref[...] = jnp.where(r == 0, g0, jnp.where(r == 1, g1, jnp.where(r == 2, g2, 0.0)))

    return pl.pallas_call(
        body, name=name, grid=(cd // tc,), in_specs=zs + [wspec, cspec],
        out_specs=[cspec, cspec, cspec, wspec],
        out_shape=[jax.ShapeDtypeStruct((s, cd), BF16)] * 3 + [jax.ShapeDtypeStruct((SUBLANES, cd), F32)],
        compiler_params=_params("parallel"),
    )(z, z, z, conv_w, dc)


def _cat_norm_fwd(name, a, c, ga, gc):
    s, qd = a.shape
    cd = c.shape[1]
    tr = _row_tile(s, qd + cd)

    def body(a_ref, c_ref, ga_ref, gc_ref, o_ref):
        av, cv = a_ref[...], c_ref[...]
        ra = lax.rsqrt(jnp.mean(av * av, axis=-1, keepdims=True) + NORM_EPS)
        rc = lax.rsqrt(jnp.mean(cv * cv, axis=-1, keepdims=True) + NORM_EPS)
        o_ref[:, :qd] = (av * ra * ga_ref[...]).astype(o_ref.dtype)
        o_ref[:, qd:] = (cv * rc * gc_ref[...]).astype(o_ref.dtype)

    return pl.pallas_call(
        body, name=name, grid=(s // tr,),
        in_specs=[pl.BlockSpec((tr, qd), lambda i: (i, 0)), pl.BlockSpec((tr, cd), lambda i: (i, 0)),
                  pl.BlockSpec((1, qd), lambda i: (0, 0)), pl.BlockSpec((1, cd), lambda i: (0, 0))],
        out_specs=pl.BlockSpec((tr, qd + cd), lambda i: (i, 0)),
        out_shape=jax.ShapeDtypeStruct((s, qd + cd), BF16), compiler_params=_params("parallel"),
    )(a, c, ga, gc)


def _cat_norm_bwd(name, dcat, a, c, ga, gc):
    s, qd = a.shape
    cd = c.shape[1]
    tr = _row_tile(s, qd + cd)

    def one(dn, yv, gv):
        r = lax.rsqrt(jnp.mean(yv * yv, axis=-1, keepdims=True) + NORM_EPS)
        xhat = yv * r
        dxn = dn * gv
        return r * (dxn - xhat * jnp.mean(dxn * xhat, axis=-1, keepdims=True)), _sum_to_sublanes(dn * xhat)

    def body(d_ref, a_ref, c_ref, ga_ref, gc_ref, da_ref, dc_ref, dga_ref, dgc_ref):
        da, pa = one(d_ref[:, :qd], a_ref[...], ga_ref[...])
        dc, pc = one(d_ref[:, qd:], c_ref[...], gc_ref[...])
        da_ref[...] = da
        dc_ref[...] = dc

        @pl.when(pl.program_id(0) == 0)
        def _():
            dga_ref[...] = pa
            dgc_ref[...] = pc

        @pl.when(pl.program_id(0) > 0)
        def _():
            dga_ref[...] += pa
            dgc_ref[...] += pc

    ra = pl.BlockSpec((tr, qd), lambda i: (i, 0))
    rc = pl.BlockSpec((tr, cd), lambda i: (i, 0))
    return pl.pallas_call(
        body, name=name, grid=(s // tr,),
        in_specs=[pl.BlockSpec((tr, qd + cd), lambda i: (i, 0)), ra, rc,
                  pl.BlockSpec((1, qd), lambda i: (0, 0)), pl.BlockSpec((1, cd), lambda i: (0, 0))],
        out_specs=[ra, rc, pl.BlockSpec((SUBLANES, qd), lambda i: (0, 0)),
                   pl.BlockSpec((SUBLANES, cd), lambda i: (0, 0))],
        out_shape=[jax.ShapeDtypeStruct((s, qd), F32), jax.ShapeDtypeStruct((s, cd), F32),
                   jax.ShapeDtypeStruct((SUBLANES, qd), F32), jax.ShapeDtypeStruct((SUBLANES, cd), F32)],
        compiler_params=_params("arbitrary"),
    )(dcat, a, c, ga, gc)


def _adamw(name, w, g, m, v):
    shape = w.shape
    cols = shape[-1]
    rows = w.size // cols
    tr = _row_tile(rows, cols, budget=3 << 19)
    bc1 = 1.0 - ADAM_B1 ** ADAM_STEP
    bc2 = 1.0 - ADAM_B2 ** ADAM_STEP

    def body(w_ref, g_ref, m_ref, v_ref, d_ref, nm_ref, nv_ref):
        gv = g_ref[...]
        mv = ADAM_B1 * m_ref[...] + (1.0 - ADAM_B1) * gv
        vv = ADAM_B2 * v_ref[...] + (1.0 - ADAM_B2) * (gv * gv)
        nm_ref[...] = mv
        nv_ref[...] = vv
        d_ref[...] = -ADAM_LR * ((mv / bc1) / (jnp.sqrt(vv / bc2) + ADAM_EPS) + ADAM_WD * w_ref[...])

    row = pl.BlockSpec((tr, cols), lambda i: (i, 0))
    outs = pl.pallas_call(
        body, name=name, grid=(rows // tr,), in_specs=[row] * 4, out_specs=[row] * 3,
        out_shape=[jax.ShapeDtypeStruct((rows, cols), F32)] * 3, compiler_params=_params("parallel"),
    )(*(t.reshape(rows, cols) for t in (w, g, m, v)))
    return tuple(t.reshape(shape) for t in outs)


HBM_SPEC = pl.BlockSpec(memory_space=pltpu.HBM)


def _mesh_place():
    x, y, c = lax.axis_index("x"), lax.axis_index("y"), lax.axis_index("c")
    other_chips = [(1 - x, y), (x, 1 - y), (1 - x, 1 - y)]
    return x, y, c, other_chips


def _gather_weights(ws, conv_w):
    na = len(ws)

    def body(*refs):
        w_refs, cw_ref = refs[:na], refs[na]
        f_refs, cwf_ref = refs[na + 1:2 * na + 1], refs[2 * na + 1]
        ici_send, ici_recv, d2d_send, d2d_recv, cw_send, cw_recv, local_sems = refs[2 * na + 2:]
        x, y, c, chips = _mesh_place()
        k_me = 2 * x + y
        sibling = (x, y, 1 - c)

        local = [pltpu.make_async_copy(w_refs[a], f_refs[a].at[:, k_me], local_sems.at[a]) for a in range(na)]
        local.append(pltpu.make_async_copy(cw_ref, cwf_ref.at[k_me], local_sems.at[na]))
        for cp in local:
            cp.start()

        def half(a, chip_idx, core):
            r2 = w_refs[a].shape[1] // 2
            return f_refs[a].at[:, chip_idx, pl.ds(core * r2, r2), :]

        def ici(a, j, to=None):
            r2 = w_refs[a].shape[1] // 2
            cx, cy = chips[j]
            return pltpu.make_async_remote_copy(
                src_ref=w_refs[a].at[:, pl.ds(c * r2, r2), :], dst_ref=half(a, k_me if to else 2 * cx + cy, c),
                send_sem=ici_send.at[a * 3 + j], recv_sem=ici_recv.at[a * 3 + j],
                device_id=(cx, cy, c), device_id_type=MESH)

        def d2d(a, j, core):
            cx, cy = chips[j]
            blk = half(a, 2 * cx + cy, core)
            return pltpu.make_async_remote_copy(
                src_ref=blk, dst_ref=blk, send_sem=d2d_send.at[a * 3 + j], recv_sem=d2d_recv.at[a * 3 + j],
                device_id=sibling, device_id_type=MESH)

        def taps(j, to=None):
            cx, cy = chips[j]
            return pltpu.make_async_remote_copy(
                src_ref=cw_ref, dst_ref=cwf_ref.at[k_me if to else 2 * cx + cy],
                send_sem=cw_send.at[j], recv_sem=cw_recv.at[j], device_id=(cx, cy, c), device_id_type=MESH)

        sends = [ici(a, j, to=True) for a in range(na) for j in range(3)]
        for cp in sends:
            cp.start()
        tap_sends = [taps(j, to=True) for j in range(3)]
        for cp in tap_sends:
            cp.start()
        passed = []
        for a in range(na):
            for j in range(3):
                ici(a, j).wait_recv()
                fwd = d2d(a, j, c)
                fwd.start()
                passed.append(fwd)
        for a in range(na):
            for j in range(3):
                d2d(a, j, 1 - c).wait_recv()
        for j in range(3):
            taps(j).wait_recv()
        for cp in sends + passed + tap_sends:
            cp.wait_send()
        for cp in local:
            cp.wait()

    out_shape = [jax.ShapeDtypeStruct((w.shape[0], N_CHIPS) + w.shape[1:], w.dtype) for w in ws]
    out_shape.append(jax.ShapeDtypeStruct((N_CHIPS,) + conv_w.shape, conv_w.dtype))
    return pl.pallas_call(
        body, name="gather_weights", in_specs=[HBM_SPEC] * (na + 1), out_specs=[HBM_SPEC] * (na + 1),
        out_shape=out_shape,
        scratch_shapes=[pltpu.SemaphoreType.DMA((na * 3,))] * 4
        + [pltpu.SemaphoreType.DMA((3,))] * 2 + [pltpu.SemaphoreType.DMA((na + 1,))],
    )(*ws, conv_w)


def _swap_core_halves(gs):
    na = len(gs)

    def body(*refs):
        g_refs, o_refs = refs[:na], refs[na:2 * na]
        send_sems, recv_sems = refs[2 * na:]
        x, y, c, _ = _mesh_place()
        copies = []
        for a in range(na):
            r2 = g_refs[a].shape[1] // 2
            copies.append(pltpu.make_async_remote_copy(
                src_ref=g_refs[a].at[:, pl.ds((1 - c) * r2, r2), :], dst_ref=o_refs[a],
                send_sem=send_sems.at[a], recv_sem=recv_sems.at[a], device_id=(x, y, 1 - c), device_id_type=MESH))
        for cp in copies:
            cp.start()
        for cp in copies:
            cp.wait()

    return pl.pallas_call(
        body, name="swap_core_halves", in_specs=[HBM_SPEC] * na, out_specs=[HBM_SPEC] * na,
        out_shape=[jax.ShapeDtypeStruct((g.shape[0], g.shape[1] // 2, g.shape[2]), g.dtype) for g in gs],
        scratch_shapes=[pltpu.SemaphoreType.DMA((na,))] * 2,
    )(*gs)


def _add_core_halves(name, g, sib, core):
    nb, r, cols = g.shape
    r2 = r // 2
    tr = _row_tile(r2, cols, itemsize=2, budget=1 << 20)
    nrt = r2 // tr

    def body(core_ref, g_ref, s_ref, o_ref):
        o_ref[...] = (g_ref[...].astype(F32) + s_ref[...].astype(F32)).astype(o_ref.dtype)

    return pl.pallas_call(
        body, name=name,
        grid_spec=pltpu.PrefetchScalarGridSpec(
            num_scalar_prefetch=1, grid=(nb, nrt),
            in_specs=[pl.BlockSpec((None, tr, cols), lambda k, i, core_ref: (k, core_ref[0] * nrt + i, 0)),
                      pl.BlockSpec((None, tr, cols), lambda k, i, core_ref: (k, i, 0))],
            out_specs=pl.BlockSpec((None, tr, cols), lambda k, i, core_ref: (k, i, 0))),
        out_shape=jax.ShapeDtypeStruct((nb, r2, cols), BF16), compiler_params=_params("parallel", "parallel"),
    )(core, g, sib)


def _scatter_to_chips(hs):
    na = len(hs)

    def body(*refs):
        h_refs, o_refs = refs[:na], refs[na:2 * na]
        send_sems, recv_sems, local_sems = refs[2 * na:]
        x, y, c, chips = _mesh_place()
        k_me = 2 * x + y
        local = [pltpu.make_async_copy(h_refs[a].at[k_me], o_refs[a].at[k_me], local_sems.at[a]) for a in range(na)]
        for cp in local:
            cp.start()
        copies = []
        for a in range(na):
            for j, (cx, cy) in enumerate(chips):
                copies.append(pltpu.make_async_remote_copy(
                    src_ref=h_refs[a].at[2 * cx + cy], dst_ref=o_refs[a].at[k_me],
                    send_sem=send_sems.at[a * 3 + j], recv_sem=recv_sems.at[a * 3 + j],
                    device_id=(cx, cy, c), device_id_type=MESH))
        for cp in copies:
            cp.start()
        for cp in copies:
            cp.wait()
        for cp in local:
            cp.wait()

    return pl.pallas_call(
        body, name="scatter_to_chips", in_specs=[HBM_SPEC] * na, out_specs=[HBM_SPEC] * na,
        out_shape=[jax.ShapeDtypeStruct(h.shape, h.dtype) for h in hs],
        scratch_shapes=[pltpu.SemaphoreType.DMA((na * 3,))] * 2 + [pltpu.SemaphoreType.DMA((na,))],
    )(*hs)


def _sum_chips(name, rcv):
    nb, r2, cols = rcv.shape
    tr = _row_tile(r2, cols, itemsize=2, budget=1 << 19)

    def body(r_ref, o_ref):
        acc = r_ref[0].astype(F32)
        for k in range(1, nb):
            acc = acc + r_ref[k].astype(F32)
        o_ref[...] = acc

    return pl.pallas_call(
        body, name=name, grid=(r2 // tr,), in_specs=[pl.BlockSpec((nb, tr, cols), lambda i: (0, i, 0))],
        out_specs=pl.BlockSpec((tr, cols), lambda i: (i, 0)),
        out_shape=jax.ShapeDtypeStruct((r2, cols), F32), compiler_params=_params("parallel"),
    )(rcv)


def _join_core_halves(ts, n_layers):
    na = len(ts)
    nk = na // n_layers

    def body(*refs):
        t_refs, o_refs = refs[:na], refs[na:na + nk]
        send_sems, recv_sems, local_sems = refs[na + nk:]
        x, y, c, _ = _mesh_place()
        local, copies = [], []
        for a in range(na):
            kind, layer = divmod(a, n_layers)
            r2 = t_refs[a].shape[0]
            mine = o_refs[kind].at[layer, pl.ds(c * r2, r2), :]
            local.append(pltpu.make_async_copy(t_refs[a], mine, local_sems.at[a]))
            copies.append(pltpu.make_async_remote_copy(
                src_ref=t_refs[a], dst_ref=mine, send_sem=send_sems.at[a], recv_sem=recv_sems.at[a],
                device_id=(x, y, 1 - c), device_id_type=MESH))
        for cp in local + copies:
            cp.start()
        for cp in copies:
            cp.wait()
        for cp in local:
            cp.wait()

    out_shape = [jax.ShapeDtypeStruct((n_layers, 2 * ts[k * n_layers].shape[0], ts[k * n_layers].shape[1]), F32)
                 for k in range(nk)]
    return pl.pallas_call(
        body, name="join_core_halves", in_specs=[HBM_SPEC] * na, out_specs=[HBM_SPEC] * nk, out_shape=out_shape,
        scratch_shapes=[pltpu.SemaphoreType.DMA((na,))] * 3,
    )(*ts)


def _allreduce_small(p):
    n, _, w = p.shape

    def body(p_ref, o_ref, buf, send_sems, recv_sems):
        x, y, c, _ = _mesh_place()
        me = 4 * x + 2 * y + c
        buf[me] = jnp.sum(p_ref[...], axis=1)
        copies = []
        for pat in range(1, N_DEV):
            fx, fy, fc = (pat >> 2) & 1, (pat >> 1) & 1, pat & 1
            copies.append(pltpu.make_async_remote_copy(
                src_ref=buf.at[me], dst_ref=buf.at[me], send_sem=send_sems.at[pat - 1], recv_sem=recv_sems.at[pat - 1],
                device_id=(x ^ fx, y ^ fy, c ^ fc), device_id_type=MESH))
        for cp in copies:
            cp.start()
        for cp in copies:
            cp.wait()
        acc = buf[0]
        for dev in range(1, N_DEV):
            acc = acc + buf[dev]
        o_ref[...] = acc

    return pl.pallas_call(
        body, name="allreduce_small", in_specs=[pl.BlockSpec(memory_space=pltpu.VMEM)],
        out_specs=pl.BlockSpec(memory_space=pltpu.VMEM), out_shape=jax.ShapeDtypeStruct((n, w), F32),
        scratch_shapes=[pltpu.VMEM((N_DEV, n, w), F32), pltpu.SemaphoreType.DMA((N_DEV - 1,)),
                        pltpu.SemaphoreType.DMA((N_DEV - 1,))],
    )(p)


def _ffn_forward(tag, x, g_pre, g_post, gu_w, dn_w, layer):
    s, d = x.shape
    f = dn_w.shape[1]
    h = _norm_fwd(f"{tag}_norm", x, g_pre)
    gu, a = _ffn_up(f"{tag}_up", h, gu_w, layer)
    tk = _tile(f, 1408)
    tm, tn = _tile(s, 1024), _tile(d, 1024)
    y = _mm(f"{tag}_down", a, dn_w, mode="nn", grid=(s // tm, d // tn, f // tk),
            a_spec=pl.BlockSpec((tm, tk), lambda i, j, k: (i, k)),
            b_spec=pl.BlockSpec((None, tk, tn), lambda i, j, k: (layer, k, j)),
            o_spec=pl.BlockSpec((tm, tn), lambda i, j, k: (i, j)),
            out_shape=jax.ShapeDtypeStruct((s, d), F32), nk=f // tk, acc_shape=(tm, tn))
    x_new = _res_norm(f"{tag}_post", x, y, g_post, FFN_RESIDUAL_WEIGHT)
    return x_new, (x, h, gu, a, y)


def _ffn_backward(tag, dx_new, saved, g_pre, g_post, gu_w, dn_w, layer):
    x, h, gu, a, y = saved
    s, d = x.shape
    nb, fs = gu_w.shape[1], gu_w.shape[3]
    f = dn_w.shape[1]
    fr = f // nb
    dy, dg_post = _norm_bwd(f"{tag}_post_bwd", dx_new, y, g_post, FFN_RESIDUAL_WEIGHT, None, BF16)
    dgu = _ffn_dact(f"{tag}_dact", dy, dn_w, gu, layer)
    dgu4 = dgu.reshape(nb, s, fs)
    tn = _tile(d, 1024)
    d_wd = _mm(f"{tag}_dwd", a, dy, mode="tn", grid=(nb, d // tn),
               a_spec=pl.BlockSpec((s, fr), lambda i, j: (0, i)),
               b_spec=pl.BlockSpec((s, tn), lambda i, j: (0, j)),
               o_spec=pl.BlockSpec((None, fr, tn), lambda i, j: (i, 0, j)),
               out_shape=jax.ShapeDtypeStruct((nb, fr, d), BF16))
    tm, tw = _tile(d, 512), _tile(fs, 1408)
    nw = fs // tw
    d_wgu = _mm(f"{tag}_dwgu", h, dgu4, mode="tn", grid=(nb, nw, d // tm),
                a_spec=pl.BlockSpec((s, tm), lambda k, j, i: (0, i)),
                b_spec=pl.BlockSpec((None, s, tw), lambda k, j, i: (k, 0, j)),
                o_spec=pl.BlockSpec((None, tm, tw), lambda k, j, i: (k, i, j)),
                out_shape=jax.ShapeDtypeStruct((nb, d, fs), BF16))
    ts, td = _tile(s, 1024), _tile(d, 1024)
    dh = _mm(f"{tag}_dh", dgu4, gu_w, mode="nt", grid=(s // ts, d // td, nb),
             a_spec=pl.BlockSpec((None, ts, fs), lambda i, j, k: (k, i, 0)),
             b_spec=pl.BlockSpec((None, None, td, fs), lambda i, j, k: (layer, k, j, 0)),
             o_spec=pl.BlockSpec((ts, td), lambda i, j, k: (i, j)),
             out_shape=jax.ShapeDtypeStruct((s, d), F32), nk=nb, acc_shape=(ts, td))
    dx, dg_pre = _norm_bwd(f"{tag}_pre_bwd", dh, x, g_pre, 1.0, dx_new, F32)
    return dx, d_wgu, d_wd, dg_pre, dg_post


def _mixer_forward(tag, x, gains, win_w, conv_taps, wout_w, layer, dims):
    qd, kvd, cd = dims
    s, d = x.shape
    nb, cw = win_w.shape[1], win_w.shape[3]
    g_pre, g_a, g_c, g_post = gains
    h = _norm_fwd(f"{tag}_norm", x, g_pre)
    tm = _tile(s, 1024)
    z = _mm(f"{tag}_in", h, win_w, mode="nn", grid=(nb, s // tm),
            a_spec=pl.BlockSpec((tm, d), lambda j, i: (i, 0)),
            b_spec=pl.BlockSpec((None, None, d, cw), lambda j, i: (layer, j, 0, 0)),
            o_spec=pl.BlockSpec((tm, cw), lambda j, i: (i, j)),
            out_shape=jax.ShapeDtypeStruct((s, nb * cw), BF16))
    a, lse = _attn_fwd(f"{tag}_attn", z, qd, kvd)
    c = _conv_fwd(f"{tag}_conv", z, conv_taps, qd + 2 * kvd, cd)
    cat = _cat_norm_fwd(f"{tag}_cat", a, c, g_a, g_c)
    mw = qd + cd
    tn = _tile(d, 1024)
    mixed = _mm(f"{tag}_out", cat, wout_w, mode="nn", grid=(s // tm, d // tn),
                a_spec=pl.BlockSpec((tm, mw), lambda i, j: (i, 0)),
                b_spec=pl.BlockSpec((None, mw, tn), lambda i, j: (layer, 0, j)),
                o_spec=pl.BlockSpec((tm, tn), lambda i, j: (i, j)),
                out_shape=jax.ShapeDtypeStruct((s, d), F32))
    x_new = _res_norm(f"{tag}_post", x, mixed, g_post, 1.0)
    return x_new, (x, h, z, a, lse, c, cat, mixed)


def _mixer_backward(tag, dx_new, saved, gains, win_w, conv_taps, wout_w, layer, dims):
    qd, kvd, cd = dims
    x, h, z, a, lse, c, cat, mixed = saved
    s, d = x.shape
    nb, cw = win_w.shape[1], win_w.shape[3]
    g_pre, g_a, g_c, g_post = gains
    mw = qd + cd
    dmixed, dg_post = _norm_bwd(f"{tag}_post_bwd", dx_new, mixed, g_post, 1.0, None, BF16)
    tm, tn = _tile(s, 1024), _tile(mw, 1024)
    dcat = _mm(f"{tag}_dcat", dmixed, wout_w, mode="nt", grid=(s // tm, mw // tn),
               a_spec=pl.BlockSpec((tm, d), lambda i, j: (i, 0)),
               b_spec=pl.BlockSpec((None, tn, d), lambda i, j: (layer, j, 0)),
               o_spec=pl.BlockSpec((tm, tn), lambda i, j: (i, j)),
               out_shape=jax.ShapeDtypeStruct((s, mw), F32))
    wr = mw // nb
    td = _tile(d, 1024)
    d_wout = _mm(f"{tag}_dwout", cat, dmixed, mode="tn", grid=(nb, d // td),
                 a_spec=pl.BlockSpec((s, wr), lambda i, j: (0, i)),
                 b_spec=pl.BlockSpec((s, td), lambda i, j: (0, j)),
                 o_spec=pl.BlockSpec((None, wr, td), lambda i, j: (i, 0, j)),
                 out_shape=jax.ShapeDtypeStruct((nb, wr, d), BF16))
    da, dc, dg_a, dg_c = _cat_norm_bwd(f"{tag}_cat_bwd", dcat, a, c, g_a, g_c)
    dhc, dbg, dcg, d_taps = _conv_bwd(f"{tag}_conv_bwd", z, conv_taps, dc, qd + 2 * kvd, cd)
    dq, dk, dv = _attn_bwd(f"{tag}_attn_bwd", z, a, lse, da, qd, kvd)
    dz = jnp.concatenate([dq, dk, dv, dhc, dbg, dcg], axis=1)
    th = _tile(d, 512)
    d_win = _mm(f"{tag}_dwin", h, dz, mode="tn", grid=(nb, d // th),
                a_spec=pl.BlockSpec((s, th), lambda k, i: (0, i)),
                b_spec=pl.BlockSpec((s, cw), lambda k, i: (0, k)),
                o_spec=pl.BlockSpec((None, th, cw), lambda k, i: (k, i, 0)),
                out_shape=jax.ShapeDtypeStruct((nb, d, cw), BF16))
    dh = _mm(f"{tag}_dh", dz, win_w, mode="nt", grid=(s // tm, d // td, nb),
             a_spec=pl.BlockSpec((tm, cw), lambda i, j, k: (i, k)),
             b_spec=pl.BlockSpec((None, None, td, cw), lambda i, j, k: (layer, k, j, 0)),
             o_spec=pl.BlockSpec((tm, td), lambda i, j, k: (i, j)),
             out_shape=jax.ShapeDtypeStruct((s, d), F32), nk=nb, acc_shape=(tm, td))
    dx, dg_pre = _norm_bwd(f"{tag}_pre_bwd", dh, x, g_pre, 1.0, dx_new, F32)
    return dx, d_win, d_wout, d_taps, (dg_pre, dg_a, dg_c, dg_post)


def _pad_cols(v, width):
    return jnp.pad(v, ((0, 0), (0, width - v.shape[1])))


def kernel(x, ffn1_norm_pre, ffn1_w_gate_up, ffn1_w_down, ffn1_norm_post, mix_norm_pre, w_in, conv_w, attn_out_norm, conv_out_norm, w_out, mix_norm_post, ffn2_norm_pre, ffn2_w_gate_up, ffn2_w_down, ffn2_norm_post, loss_target, m_ffn1_norm_pre, m_ffn1_w_gate_up, m_ffn1_w_down, m_ffn1_norm_post, m_mix_norm_pre, m_w_in, m_conv_w, m_attn_out_norm, m_conv_out_norm, m_w_out, m_mix_norm_post, m_ffn2_norm_pre, m_ffn2_w_gate_up, m_ffn2_w_down, m_ffn2_norm_post, v_ffn1_norm_pre, v_ffn1_w_gate_up, v_ffn1_w_down, v_ffn1_norm_post, v_mix_norm_pre, v_w_in, v_conv_w, v_attn_out_norm, v_conv_out_norm, v_w_out, v_mix_norm_post, v_ffn2_norm_pre, v_ffn2_w_gate_up, v_ffn2_w_down, v_ffn2_norm_post):
    _, s, d = x.shape
    n_layers = ffn1_norm_pre.shape[0]
    qd = attn_out_norm.shape[1]
    cd = conv_out_norm.shape[1]
    kvd = qd // Q_PER_KV
    dims = (qd, kvd, cd)
    assert N_CHIPS * w_in.shape[2] == qd + 2 * kvd + 3 * cd and qd + cd == N_CHIPS * w_out.shape[1]
    assert 2 * d <= SMALL_ROWS * LANES * SUBLANES
    chip = 2 * lax.axis_index("x") + lax.axis_index("y")
    core = lax.axis_index("c").astype(jnp.int32).reshape(1)

    big = (ffn1_w_gate_up, ffn1_w_down, w_in, w_out, ffn2_w_gate_up, ffn2_w_down)
    gathered = _gather_weights([w.astype(BF16) for w in big], conv_w)
    gu1_w, dn1_w, win_w, wout_w, gu2_w, dn2_w = gathered[:6]
    dn1_w = dn1_w.reshape(n_layers, -1, d)
    dn2_w = dn2_w.reshape(n_layers, -1, d)
    wout_w = wout_w.reshape(n_layers, -1, d)
    taps_all = gathered[6]
    taps = jnp.transpose(taps_all, (1, 2, 0, 3)).reshape(n_layers, CONV_WIDTH, cd)
    taps = jnp.pad(taps, ((0, 0), (0, SUBLANES - CONV_WIDTH), (0, 0)))

    def gain(g, layer):
        return g[layer][None, :]

    xs = x[0]
    saved = []
    for layer in range(n_layers):
        t = f"l{layer}"
        xs, s1 = _ffn_forward(f"{t}_ffn1", xs, gain(ffn1_norm_pre, layer), gain(ffn1_norm_post, layer), gu1_w, dn1_w, layer)
        mix_gains = (gain(mix_norm_pre, layer), gain(attn_out_norm, layer), gain(conv_out_norm, layer), gain(mix_norm_post, layer))
        xs, s2 = _mixer_forward(f"{t}_mix", xs, mix_gains, win_w, taps[layer], wout_w, layer, dims)
        xs, s3 = _ffn_forward(f"{t}_ffn2", xs, gain(ffn2_norm_pre, layer), gain(ffn2_norm_post, layer), gu2_w, dn2_w, layer)
        saved.append((s1, s2, s3, mix_gains))
    dxs, loss_part = _loss_head("loss_head", xs, loss_target[0])
    loss = lax.psum(jnp.sum(loss_part), ("x", "y", "c"))

    grads = {k: [None] * n_layers for k in ("gu1", "dn1", "win", "wout", "gu2", "dn2")}
    small = [None] * n_layers
    for layer in reversed(range(n_layers)):
        t = f"l{layer}"
        s1, s2, s3, mix_gains = saved[layer]
        dxs, grads["gu2"][layer], grads["dn2"][layer], p_pre2, p_post2 = _ffn_backward(
            f"{t}_ffn2", dxs, s3, gain(ffn2_norm_pre, layer), gain(ffn2_norm_post, layer), gu2_w, dn2_w, layer)
        dxs, grads["win"][layer], grads["wout"][layer], p_taps, (p_mpre, p_a, p_c, p_mpost) = _mixer_backward(
            f"{t}_mix", dxs, s2, mix_gains, win_w, taps[layer], wout_w, layer, dims)
        dxs, grads["gu1"][layer], grads["dn1"][layer], p_pre1, p_post1 = _ffn_backward(
            f"{t}_ffn1", dxs, s1, gain(ffn1_norm_pre, layer), gain(ffn1_norm_post, layer), gu1_w, dn1_w, layer)
        tap_rows = jnp.zeros((CONV_WIDTH, SUBLANES, d), F32).at[:, 0, :cd].set(p_taps[:CONV_WIDTH])
        rows = [p_pre1, p_post1, p_mpre, jnp.concatenate([p_a, p_c], axis=1), p_mpost, p_pre2, p_post2]
        rows = jnp.concatenate([jnp.stack(rows), tap_rows], axis=0)
        small[layer] = jnp.pad(rows, ((0, SMALL_ROWS - rows.shape[0]), (0, 0), (0, 0)))
    grad_x = dxs[None]

    kinds = ("gu1", "dn1", "win", "wout", "gu2", "dn2")
    parts = [grads[k][layer] for k in kinds for layer in range(n_layers)]
    from_sibling = _swap_core_halves(parts)
    core_sums = [_add_core_halves(f"add_cores_{i}", g, sb, core) for i, (g, sb) in enumerate(zip(parts, from_sibling))]
    from_chips = _scatter_to_chips(core_sums)
    reduced = [_sum_chips(f"sum_chips_{i}", r) for i, r in enumerate(from_chips)]
    g_gu1, g_dn1, g_win, g_wout, g_gu2, g_dn2 = _join_core_halves(reduced, n_layers)

    small_sum = _allreduce_small(jnp.concatenate(small, axis=0)).reshape(n_layers, SMALL_ROWS, d)
    g_ffn1_pre, g_ffn1_post, g_mix_pre = small_sum[:, 0], small_sum[:, 1], small_sum[:, 2]
    g_attn_out, g_conv_out = small_sum[:, 3, :qd], small_sum[:, 3, qd:qd + cd]
    g_mix_post, g_ffn2_pre, g_ffn2_post = small_sum[:, 4], small_sum[:, 5], small_sum[:, 6]
    cc = conv_w.shape[2]
    g_conv = lax.dynamic_slice_in_dim(small_sum[:, 7:7 + CONV_WIDTH, :cd], chip * cc, cc, axis=2)

    weights = dict(ffn1_norm_pre=ffn1_norm_pre, ffn1_w_gate_up=ffn1_w_gate_up, ffn1_w_down=ffn1_w_down, ffn1_norm_post=ffn1_norm_post, mix_norm_pre=mix_norm_pre, w_in=w_in, conv_w=conv_w, attn_out_norm=attn_out_norm, conv_out_norm=conv_out_norm, w_out=w_out, mix_norm_post=mix_norm_post, ffn2_norm_pre=ffn2_norm_pre, ffn2_w_gate_up=ffn2_w_gate_up, ffn2_w_down=ffn2_w_down, ffn2_norm_post=ffn2_norm_post)
    m_in = dict(ffn1_norm_pre=m_ffn1_norm_pre, ffn1_w_gate_up=m_ffn1_w_gate_up, ffn1_w_down=m_ffn1_w_down, ffn1_norm_post=m_ffn1_norm_post, mix_norm_pre=m_mix_norm_pre, w_in=m_w_in, conv_w=m_conv_w, attn_out_norm=m_attn_out_norm, conv_out_norm=m_conv_out_norm, w_out=m_w_out, mix_norm_post=m_mix_norm_post, ffn2_norm_pre=m_ffn2_norm_pre, ffn2_w_gate_up=m_ffn2_w_gate_up, ffn2_w_down=m_ffn2_w_down, ffn2_norm_post=m_ffn2_norm_post)
    v_in = dict(ffn1_norm_pre=v_ffn1_norm_pre, ffn1_w_gate_up=v_ffn1_w_gate_up, ffn1_w_down=v_ffn1_w_down, ffn1_norm_post=v_ffn1_norm_post, mix_norm_pre=v_mix_norm_pre, w_in=v_w_in, conv_w=v_conv_w, attn_out_norm=v_attn_out_norm, conv_out_norm=v_conv_out_norm, w_out=v_w_out, mix_norm_post=v_mix_norm_post, ffn2_norm_pre=v_ffn2_norm_pre, ffn2_w_gate_up=v_ffn2_w_gate_up, ffn2_w_down=v_ffn2_w_down, ffn2_norm_post=v_ffn2_norm_post)
    grad = dict(ffn1_norm_pre=g_ffn1_pre, ffn1_w_gate_up=g_gu1, ffn1_w_down=g_dn1, ffn1_norm_post=g_ffn1_post, mix_norm_pre=g_mix_pre, w_in=g_win, conv_w=g_conv, attn_out_norm=g_attn_out, conv_out_norm=g_conv_out, w_out=g_wout, mix_norm_post=g_mix_post, ffn2_norm_pre=g_ffn2_pre, ffn2_w_gate_up=g_gu2, ffn2_w_down=g_dn2, ffn2_norm_post=g_ffn2_post)
    names = list(weights)

    delta, new_m, new_v = {}, {}, {}
    matrices = ("ffn1_w_gate_up", "ffn1_w_down", "w_in", "w_out", "ffn2_w_gate_up", "ffn2_w_down")
    for n in matrices:
        delta[n], new_m[n], new_v[n] = _adamw(f"adamw_{n}", weights[n], grad[n], m_in[n], v_in[n])
    vectors = [n for n in names if n not in matrices]

    def pack(tree):
        flat = jnp.concatenate([tree[n].reshape(-1) for n in vectors])
        return jnp.pad(flat, (0, -flat.size % (SUBLANES * LANES))).reshape(-1, LANES)

    packed = _adamw("adamw_small", pack(weights), pack(grad), pack(m_in), pack(v_in))
    offset = 0
    for n in vectors:
        size = weights[n].size
        for tree, flat in zip((delta, new_m, new_v), packed):
            tree[n] = flat.reshape(-1)[offset:offset + size].reshape(weights[n].shape)
        offset += size

    return (loss, grad_x, *[grad[n] for n in names], *[delta[n] for n in names],
            *[new_m[n] for n in names], *[new_v[n] for n in names])
```

```python
import functools

import jax
import jax.numpy as jnp
from jax import lax
from jax.experimental import pallas as pl
from jax.experimental.pallas import tpu as pltpu

F32 = jnp.float32
BF16 = jnp.bfloat16
MESH = pl.DeviceIdType.MESH

NORM_EPS = 1e-6
HEAD_DIM = 128
Q_PER_KV = 4
CONV_WIDTH = 3
FFN_RESIDUAL_WEIGHT = 0.5
DILATED_BRANCHES = ((128, 1), (512, 4), (2048, 16))
ADAM_LR = 0.001
ADAM_B1 = 0.9
ADAM_B2 = 0.999
ADAM_EPS = 1e-08
ADAM_WD = 0.01
ADAM_STEP = 10

N_CHIPS = 4
N_DEV = 8
V7X_VMEM_BYTES = 64 << 20
VMEM_LIMIT = V7X_VMEM_BYTES - (12 << 20)
SUBLANES = 8
LANES = 128
SMALL_ROWS = 16


def _params(*sem):
    return pltpu.CompilerParams(dimension_semantics=sem, vmem_limit_bytes=VMEM_LIMIT)


def _row_tile(rows, cols, itemsize=4, budget=2 << 20):
    t = rows
    while t * cols * itemsize > budget and t % 32 == 0:
        t //= 2
    return t


def _sum_to_sublanes(v):
    r, n = v.shape
    return v.reshape(r // SUBLANES, SUBLANES, n).sum(axis=0)


_DIMS = {
    "nn": (((1,), (0,)), ((), ())),
    "nt": (((1,), (1,)), ((), ())),
    "tn": (((0,), (0,)), ((), ())),
}


def _dot(a, b, mode):
    return lax.dot_general(a, b, _DIMS[mode], preferred_element_type=F32)


def _mm(name, a, b, *, mode, grid, a_spec, b_spec, o_spec, out_shape, nk=1, acc_shape=None):
    def body(a_ref, b_ref, o_ref, *scratch):
        r = _dot(a_ref[...], b_ref[...], mode)
        if nk == 1:
            o_ref[...] = r.astype(o_ref.dtype)
        else:
            acc = scratch[0]
            k = pl.program_id(len(grid) - 1)

            @pl.when(k == 0)
            def _():
                acc[...] = r

            @pl.when(k > 0)
            def _():
                acc[...] += r

            @pl.when(k == nk - 1)
            def _():
                o_ref[...] = acc[...].astype(o_ref.dtype)

    sem = ("parallel",) * (len(grid) - (1 if nk > 1 else 0)) + (("arbitrary",) if nk > 1 else ())
    return pl.pallas_call(
        body, name=name, grid=grid, in_specs=[a_spec, b_spec], out_specs=o_spec, out_shape=out_shape,
        scratch_shapes=[pltpu.VMEM(acc_shape, F32)] if nk > 1 else [],
        compiler_params=_params(*sem),
    )(a, b)


def _tile(n, want):
    if n <= want:
        return n
    best = None
    for t in range(LANES, want + 1, LANES):
        if n % t == 0:
            best = t
    assert best is not None, (n, want)
    return best


def _norm_fwd(name, x, gain):
    s, d = x.shape
    tr = _row_tile(s, d)

    def body(x_ref, g_ref, o_ref):
        xv = x_ref[...]
        r = lax.rsqrt(jnp.mean(xv * xv, axis=-1, keepdims=True) + NORM_EPS)
        o_ref[...] = (xv * r * g_ref[...]).astype(o_ref.dtype)

    return pl.pallas_call(
        body, name=name, grid=(s // tr,),
        in_specs=[pl.BlockSpec((tr, d), lambda i: (i, 0)), pl.BlockSpec((1, d), lambda i: (0, 0))],
        out_specs=pl.BlockSpec((tr, d), lambda i: (i, 0)),
        out_shape=jax.ShapeDtypeStruct((s, d), BF16), compiler_params=_params("parallel"),
    )(x, gain)


def _res_norm(name, x, y, gain, scale):
    s, d = x.shape
    tr = _row_tile(s, d)

    def body(x_ref, y_ref, g_ref, o_ref):
        yv = y_ref[...]
        r = lax.rsqrt(jnp.mean(yv * yv, axis=-1, keepdims=True) + NORM_EPS)
        o_ref[...] = x_ref[...] + scale * (yv * r * g_ref[...])

    row = pl.BlockSpec((tr, d), lambda i: (i, 0))
    return pl.pallas_call(
        body, name=name, grid=(s // tr,),
        in_specs=[row, row, pl.BlockSpec((1, d), lambda i: (0, 0))], out_specs=row,
        out_shape=jax.ShapeDtypeStruct((s, d), F32), compiler_params=_params("parallel"),
    )(x, y, gain)


def _norm_bwd(name, dout, yin, gain, scale, resid, out_dtype):
    s, d = yin.shape
    tr = _row_tile(s, d)
    has_resid = resid is not None

    def body(*refs):
        if has_resid:
            do_ref, y_ref, g_ref, r_ref, di_ref, dg_ref = refs
        else:
            do_ref, y_ref, g_ref, di_ref, dg_ref = refs
        yv = y_ref[...]
        r = lax.rsqrt(jnp.mean(yv * yv, axis=-1, keepdims=True) + NORM_EPS)
        xhat = yv * r
        dn = scale * do_ref[...]
        part = _sum_to_sublanes(dn * xhat)

        @pl.when(pl.program_id(0) == 0)
        def _():
            dg_ref[...] = part

        @pl.when(pl.program_id(0) > 0)
        def _():
            dg_ref[...] += part

        dxn = dn * g_ref[...]
        din = r * (dxn - xhat * jnp.mean(dxn * xhat, axis=-1, keepdims=True))
        if has_resid:
            din = din + r_ref[...]
        di_ref[...] = din.astype(di_ref.dtype)

    row = pl.BlockSpec((tr, d), lambda i: (i, 0))
    vec = pl.BlockSpec((1, d), lambda i: (0, 0))
    ins = [row, row, vec] + ([row] if has_resid else [])
    args = (dout, yin, gain) + ((resid,) if has_resid else ())
    return pl.pallas_call(
        body, name=name, grid=(s // tr,), in_specs=ins,
        out_specs=[row, pl.BlockSpec((SUBLANES, d), lambda i: (0, 0))],
        out_shape=[jax.ShapeDtypeStruct((s, d), out_dtype), jax.ShapeDtypeStruct((SUBLANES, d), F32)],
        compiler_params=_params("arbitrary"),
    )(*args)


def _loss_head(name, y, target):
    s, d = y.shape
    tr = _row_tile(s, d)

    def body(y_ref, t_ref, dy_ref, l_ref):
        e = y_ref[...] - t_ref[...]
        dy_ref[...] = e * (1.0 / d)
        part = _sum_to_sublanes(e * e) * (0.5 / d)

        @pl.when(pl.program_id(0) == 0)
        def _():
            l_ref[...] = part

        @pl.when(pl.program_id(0) > 0)
        def _():
            l_ref[...] += part

    row = pl.BlockSpec((tr, d), lambda i: (i, 0))
    return pl.pallas_call(
        body, name=name, grid=(s // tr,), in_specs=[row, row],
        out_specs=[row, pl.BlockSpec((SUBLANES, d), lambda i: (0, 0))],
        out_shape=[jax.ShapeDtypeStruct((s, d), F32), jax.ShapeDtypeStruct((SUBLANES, d), F32)],
        compiler_params=_params("arbitrary"),
    )(y, target)


def _ffn_up(name, h, gu_w):
    s, d = h.shape
    nb, _, fs = gu_w.shape
    hb = nb // 2
    w = gu_w.reshape(2, hb, d, fs)
    tm = _tile(s, 512)
    tn = _tile(fs, 1408)
    nj = fs // tn

    def body(h_ref, w_ref, gu_ref, a_ref):
        hv = h_ref[...]
        g = _dot(hv, w_ref[0], "nn")
        u = _dot(hv, w_ref[1], "nn")
        gu_ref[0] = g.astype(gu_ref.dtype)
        gu_ref[1] = u.astype(gu_ref.dtype)
        a_ref[...] = (g * jax.nn.sigmoid(g) * u).astype(a_ref.dtype)

    return pl.pallas_call(
        body, name=name, grid=(hb, nj, s // tm),
        in_specs=[pl.BlockSpec((tm, d), lambda jb, jo, i: (i, 0)),
                  pl.BlockSpec((2, None, d, tn), lambda jb, jo, i: (0, jb, 0, jo))],
        out_specs=[pl.BlockSpec((2, None, tm, tn), lambda jb, jo, i: (0, jb, i, jo)),
                   pl.BlockSpec((tm, tn), lambda jb, jo, i: (i, jb * nj + jo))],
        out_shape=[jax.ShapeDtypeStruct((2, hb, s, fs), BF16), jax.ShapeDtypeStruct((s, hb * fs), BF16)],
        compiler_params=_params("parallel", "parallel", "parallel"),
    )(h, w)


def _ffn_dact(name, dy, dn_w, gu):
    s, d = dy.shape
    _, hb, _, fs = gu.shape
    tm = _tile(s, 512)
    tn = _tile(fs, 1408)
    nj = fs // tn

    def body(dy_ref, w_ref, gu_ref, o_ref):
        da = _dot(dy_ref[...], w_ref[...], "nt")
        g = gu_ref[0].astype(F32)
        u = gu_ref[1].astype(F32)
        sg = jax.nn.sigmoid(g)
        o_ref[0] = (da * u * (sg * (1.0 + g * (1.0 - sg)))).astype(o_ref.dtype)
        o_ref[1] = (da * (g * sg)).astype(o_ref.dtype)

    blk = pl.BlockSpec((2, None, tm, tn), lambda jb, jo, i: (0, jb, i, jo))
    return pl.pallas_call(
        body, name=name, grid=(hb, nj, s // tm),
        in_specs=[pl.BlockSpec((tm, d), lambda jb, jo, i: (i, 0)),
                  pl.BlockSpec((tn, d), lambda jb, jo, i: (jb * nj + jo, 0)),
                  blk],
        out_specs=blk, out_shape=jax.ShapeDtypeStruct(gu.shape, BF16),
        compiler_params=_params("parallel", "parallel", "parallel"),
    )(dy, dn_w, gu)


def _multiplicity(q0, tq, s):
    row = q0 + lax.broadcasted_iota(jnp.int32, (tq, s), 0)
    col = lax.broadcasted_iota(jnp.int32, (tq, s), 1)
    dist = row - col
    mult = jnp.zeros((tq, s), F32)
    for window, dilation in DILATED_BRANCHES:
        hit = (dist <= window) & ((dist & (dilation - 1)) == 0)
        mult = mult + hit.astype(F32)
    return jnp.where(dist >= 0, mult, 0.0)


_MASKED = -1e30


def _attn_specs(s, qd, kvd, tq):
    rw = Q_PER_KV * HEAD_DIM
    qspec = pl.BlockSpec((tq, rw), lambda g, i: (i, g))
    kspec = pl.BlockSpec((s, HEAD_DIM), lambda g, i: (0, qd // HEAD_DIM + g))
    vspec = pl.BlockSpec((s, HEAD_DIM), lambda g, i: (0, (qd + kvd) // HEAD_DIM + g))
    return rw, qspec, kspec, vspec


def _attn_fwd(name, z, qd, kvd):
    s = z.shape[0]
    tq = _tile(s, 256)
    nkv = kvd // HEAD_DIM
    rw, qspec, kspec, vspec = _attn_specs(s, qd, kvd, tq)
    scale = HEAD_DIM ** -0.5

    def body(q_ref, k_ref, v_ref, o_ref, l_ref):
        mult = _multiplicity(pl.program_id(1) * tq, tq, s)
        live = mult > 0.0
        kv, vv = k_ref[...], v_ref[...]
        for h in range(Q_PER_KV):
            cols = slice(h * HEAD_DIM, (h + 1) * HEAD_DIM)
            sc = jnp.where(live, _dot(q_ref[:, cols], kv, "nt") * scale, _MASKED)
            mx = jnp.max(sc, axis=-1, keepdims=True)
            p = jnp.exp(sc - mx) * mult
            den = jnp.sum(p, axis=-1, keepdims=True)
            o_ref[:, cols] = _dot(p.astype(BF16), vv, "nn") / den
            l_ref[:, cols] = jnp.broadcast_to(mx + jnp.log(den), (tq, HEAD_DIM))

    return pl.pallas_call(
        body, name=name, grid=(nkv, s // tq), in_specs=[qspec, kspec, vspec], out_specs=[qspec, qspec],
        out_shape=[jax.ShapeDtypeStruct((s, qd), F32), jax.ShapeDtypeStruct((s, qd), F32)],
        compiler_params=_params("parallel", "parallel"),
    )(z, z, z)


def _attn_bwd(name, z, o, lse, do, qd, kvd):
    s = z.shape[0]
    tq = _tile(s, 256)
    nkv = kvd // HEAD_DIM
    nq = s // tq
    rw, qspec, kspec, vspec = _attn_specs(s, qd, kvd, tq)
    scale = HEAD_DIM ** -0.5

    def body(q_ref, k_ref, v_ref, o_ref, l_ref, do_ref, dq_ref, dk_ref, dv_ref, dk_acc, dv_acc):
        i = pl.program_id(1)
        mult = _multiplicity(i * tq, tq, s)
        live = mult > 0.0
        kv, vv = k_ref[...], v_ref[...]

        @pl.when(i == 0)
        def _():
            dk_acc[...] = jnp.zeros_like(dk_acc)
            dv_acc[...] = jnp.zeros_like(dv_acc)

        for h in range(Q_PER_KV):
            cols = slice(h * HEAD_DIM, (h + 1) * HEAD_DIM)
            q = q_ref[:, cols]
            dov = do_ref[:, cols]
            sc = jnp.where(live, _dot(q, kv, "nt") * scale, _MASKED)
            p = jnp.exp(sc - l_ref[:, cols][:, :1]) * mult
            dob = dov.astype(BF16)
            dp = _dot(dob, vv, "nt")
            delta = jnp.sum(dov * o_ref[:, cols], axis=-1, keepdims=True)
            ds = (p * (dp - delta) * scale).astype(BF16)
            dq_ref[:, cols] = _dot(ds, kv, "nn").astype(dq_ref.dtype)
            dk_acc[...] += _dot(ds, q, "tn")
            dv_acc[...] += _dot(p.astype(BF16), dob, "tn")

        @pl.when(i == nq - 1)
        def _():
            dk_ref[...] = dk_acc[...].astype(dk_ref.dtype)
            dv_ref[...] = dv_acc[...].astype(dv_ref.dtype)

    kvout = pl.BlockSpec((s, HEAD_DIM), lambda g, i: (0, g))
    return pl.pallas_call(
        body, name=name, grid=(nkv, nq), in_specs=[qspec, kspec, vspec, qspec, qspec, qspec],
        out_specs=[qspec, kvout, kvout],
        out_shape=[jax.ShapeDtypeStruct((s, qd), BF16), jax.ShapeDtypeStruct((s, kvd), BF16),
                   jax.ShapeDtypeStruct((s, kvd), BF16)],
        scratch_shapes=[pltpu.VMEM((s, HEAD_DIM), F32), pltpu.VMEM((s, HEAD_DIM), F32)],
        compiler_params=_params("parallel", "arbitrary"),
    )(z, z, z, o, lse, do)


def _shift_down(v, n):
    rolled = pltpu.roll(v, n, 0)
    t = lax.broadcasted_iota(jnp.int32, v.shape, 0)
    return jnp.where(t >= n, rolled, 0.0)


def _shift_up(v, n):
    rows = v.shape[0]
    rolled = pltpu.roll(v, rows - n, 0)
    t = lax.broadcasted_iota(jnp.int32, v.shape, 0)
    return jnp.where(t < rows - n, rolled, 0.0)


def _conv_specs(s, base, cd, tc):
    zs = [pl.BlockSpec((s, tc), functools.partial(lambda j, off: (0, off + j), off=(base + n * cd) // tc))
          for n in range(3)]
    wspec = pl.BlockSpec((SUBLANES, tc), lambda j: (0, j))
    cspec = pl.BlockSpec((s, tc), lambda j: (0, j))
    return zs, wspec, cspec


def _conv_fwd(name, z, conv_w, base, cd):
    s = z.shape[0]
    tc = _tile(cd, 256)
    zs, wspec, cspec = _conv_specs(s, base, cd, tc)

    def body(h_ref, b_ref, c_ref, w_ref, o_ref):
        u = c_ref[...].astype(F32) * h_ref[...].astype(F32)
        y = w_ref[0:1, :] * _shift_down(u, 2) + w_ref[1:2, :] * _shift_down(u, 1) + w_ref[2:3, :] * u
        o_ref[...] = b_ref[...].astype(F32) * y

    return pl.pallas_call(
        body, name=name, grid=(cd // tc,), in_specs=zs + [wspec], out_specs=cspec,
        out_shape=jax.ShapeDtypeStruct((s, cd), F32), compiler_params=_params("parallel"),
    )(z, z, z, conv_w)


def _conv_bwd(name, z, conv_w, dc, base, cd):
    s = z.shape[0]
    tc = _tile(cd, 256)
    zs, wspec, cspec = _conv_specs(s, base, cd, tc)

    def body(h_ref, b_ref, c_ref, w_ref, dc_ref, dh_ref, db_ref, dcg_ref, dw_ref):
        hv, bv, cv = h_ref[...].astype(F32), b_ref[...].astype(F32), c_ref[...].astype(F32)
        u = cv * hv
        u1, u2 = _shift_down(u, 1), _shift_down(u, 2)
        w0, w1, w2 = w_ref[0:1, :], w_ref[1:2, :], w_ref[2:3, :]
        y = w0 * u2 + w1 * u1 + w2 * u
        dcv = dc_ref[...]
        db_ref[...] = (dcv * y).astype(db_ref.dtype)
        dy = dcv * bv
        du = w2 * dy + w1 * _shift_up(dy, 1) + w0 * _shift_up(dy, 2)
        dh_ref[...] = (du * cv).astype(dh_ref.dtype)
        dcg_ref[...] = (du * hv).astype(dcg_ref.dtype)
        g0 = jnp.sum(dy * u2, axis=0, keepdims=True)
        g1 = jnp.sum(dy * u1, axis=0, keepdims=True)
        g2 = jnp.sum(dy * u, axis=0, keepdims=True)
        r = lax.broadcasted_iota(jnp.int32, (SUBLANES, tc), 0)
        dw_ref[...] = jnp.where(r == 0, g0, jnp.where(r == 1, g1, jnp.where(r == 2, g2, 0.0)))

    return pl.pallas_call(
        body, name=name, grid=(cd // tc,), in_specs=zs + [wspec, cspec],
        out_specs=[cspec, cspec, cspec, wspec],
        out_shape=[jax.ShapeDtypeStruct((s, cd), BF16)] * 3 + [jax.ShapeDtypeStruct((SUBLANES, cd), F32)],
        compiler_params=_params("parallel"),
    )(z, z, z, conv_w, dc)


def _cat_norm_fwd(name, a, c, ga, gc):
    s, qd = a.shape
    cd = c.shape[1]
    tr = _row_tile(s, qd + cd)

    def body(a_ref, c_ref, ga_ref, gc_ref, o_ref):
        av, cv = a_ref[...], c_ref[...]
        ra = lax.rsqrt(jnp.mean(av * av, axis=-1, keepdims=True) + NORM_EPS)
        rc = lax.rsqrt(jnp.mean(cv * cv, axis=-1, keepdims=True) + NORM_EPS)
        o_ref[:, :qd] = (av * ra * ga_ref[...]).astype(o_ref.dtype)
        o_ref[:, qd:] = (cv * rc * gc_ref[...]).astype(o_ref.dtype)

    return pl.pallas_call(
        body, name=name, grid=(s // tr,),
        in_specs=[pl.BlockSpec((tr, qd), lambda i: (i, 0)), pl.BlockSpec((tr, cd), lambda i: (i, 0)),
                  pl.BlockSpec((1, qd), lambda i: (0, 0)), pl.BlockSpec((1, cd), lambda i: (0, 0))],
        out_specs=pl.BlockSpec((tr, qd + cd), lambda i: (i, 0)),
        out_shape=jax.ShapeDtypeStruct((s, qd + cd), BF16), compiler_params=_params("parallel"),
    )(a, c, ga, gc)


def _cat_norm_bwd(name, dcat, a, c, ga, gc):
    s, qd = a.shape
    cd = c.shape[1]
    tr = _row_tile(s, qd + cd)

    def one(dn, yv, gv):
        r = lax.rsqrt(jnp.mean(yv * yv, axis=-1, keepdims=True) + NORM_EPS)
        xhat = yv * r
        dxn = dn * gv
        return r * (dxn - xhat * jnp.mean(dxn * xhat, axis=-1, keepdims=True)), _sum_to_sublanes(dn * xhat)

    def body(d_ref, a_ref, c_ref, ga_ref, gc_ref, da_ref, dc_ref, dga_ref, dgc_ref):
        da, pa = one(d_ref[:, :qd], a_ref[...], ga_ref[...])
        dc, pc = one(d_ref[:, qd:], c_ref[...], gc_ref[...])
        da_ref[...] = da
        dc_ref[...] = dc

        @pl.when(pl.program_id(0) == 0)
        def _():
            dga_ref[...] = pa
            dgc_ref[...] = pc

        @pl.when(pl.program_id(0) > 0)
        def _():
            dga_ref[...] += pa
            dgc_ref[...] += pc

    ra = pl.BlockSpec((tr, qd), lambda i: (i, 0))
    rc = pl.BlockSpec((tr, cd), lambda i: (i, 0))
    return pl.pallas_call(
        body, name=name, grid=(s // tr,),
        in_specs=[pl.BlockSpec((tr, qd + cd), lambda i: (i, 0)), ra, rc,
                  pl.BlockSpec((1, qd), lambda i: (0, 0)), pl.BlockSpec((1, cd), lambda i: (0, 0))],
        out_specs=[ra, rc, pl.BlockSpec((SUBLANES, qd), lambda i: (0, 0)),
                   pl.BlockSpec((SUBLANES, cd), lambda i: (0, 0))],
        out_shape=[jax.ShapeDtypeStruct((s, qd), F32), jax.ShapeDtypeStruct((s, cd), F32),
                   jax.ShapeDtypeStruct((SUBLANES, qd), F32), jax.ShapeDtypeStruct((SUBLANES, cd), F32)],
        compiler_params=_params("arbitrary"),
    )(dcat, a, c, ga, gc)


def _adamw(name, w, g, m, v):
    shape = w.shape
    cols = shape[-1]
    rows = w.size // cols
    tr = _row_tile(rows, cols, budget=3 << 19)
    bc1 = 1.0 - ADAM_B1 ** ADAM_STEP
    bc2 = 1.0 - ADAM_B2 ** ADAM_STEP

    def body(w_ref, g_ref, m_ref, v_ref, d_ref, nm_ref, nv_ref):
        gv = g_ref[...]
        mv = ADAM_B1 * m_ref[...] + (1.0 - ADAM_B1) * gv
        vv = ADAM_B2 * v_ref[...] + (1.0 - ADAM_B2) * (gv * gv)
        nm_ref[...] = mv
        nv_ref[...] = vv
        d_ref[...] = -ADAM_LR * ((mv / bc1) / (jnp.sqrt(vv / bc2) + ADAM_EPS) + ADAM_WD * w_ref[...])

    row = pl.BlockSpec((tr, cols), lambda i: (i, 0))
    outs = pl.pallas_call(
        body, name=name, grid=(rows // tr,), in_specs=[row] * 4, out_specs=[row] * 3,
        out_shape=[jax.ShapeDtypeStruct((rows, cols), F32)] * 3, compiler_params=_params("parallel"),
    )(*(t.reshape(rows, cols) for t in (w, g, m, v)))
    return tuple(t.reshape(shape) for t in outs)


HBM_SPEC = pl.BlockSpec(memory_space=pltpu.HBM)


def _mesh_place():
    x, y, c = lax.axis_index("x"), lax.axis_index("y"), lax.axis_index("c")
    other_chips = [(1 - x, y), (x, 1 - y), (1 - x, 1 - y)]
    return x, y, c, other_chips


def _cast_into_slot(name, w, layer, chip):
    _, r, cols = w.shape
    tr = _row_tile(r, cols)

    def body(chip_ref, w_ref, o_ref):
        o_ref[...] = w_ref[...].astype(o_ref.dtype)

    return pl.pallas_call(
        body, name=name,
        grid_spec=pltpu.PrefetchScalarGridSpec(
            num_scalar_prefetch=1, grid=(r // tr,),
            in_specs=[pl.BlockSpec((None, tr, cols), lambda i, chip_ref: (layer, i, 0))],
            out_specs=pl.BlockSpec((None, tr, cols), lambda i, chip_ref: (chip_ref[0], i, 0))),
        out_shape=jax.ShapeDtypeStruct((N_CHIPS, r, cols), BF16), compiler_params=_params("parallel"),
    )(chip, w)


def _gather_weights(fulls, conv_w):
    na = len(fulls)

    def body(*refs):
        cw_ref = refs[na]
        f_refs, cwf_ref = refs[na + 1:2 * na + 1], refs[2 * na + 1]
        ici_send, ici_recv, d2d_send, d2d_recv, cw_send, cw_recv, local_sem = refs[2 * na + 2:]
        x, y, c, chips = _mesh_place()
        k_me = 2 * x + y
        sibling = (x, y, 1 - c)

        local = [pltpu.make_async_copy(cw_ref, cwf_ref.at[k_me], local_sem)]
        for cp in local:
            cp.start()

        def half(a, chip_idx, core):
            r2 = f_refs[a].shape[1] // 2
            return f_refs[a].at[chip_idx, pl.ds(core * r2, r2), :]

        def ici(a, j, to=None):
            cx, cy = chips[j]
            return pltpu.make_async_remote_copy(
                src_ref=half(a, k_me, c), dst_ref=half(a, k_me if to else 2 * cx + cy, c),
                send_sem=ici_send.at[a * 3 + j], recv_sem=ici_recv.at[a * 3 + j],
                device_id=(cx, cy, c), device_id_type=MESH)

        def d2d(a, j, core):
            cx, cy = chips[j]
            blk = half(a, 2 * cx + cy, core)
            return pltpu.make_async_remote_copy(
                src_ref=blk, dst_ref=blk, send_sem=d2d_send.at[a * 3 + j], recv_sem=d2d_recv.at[a * 3 + j],
                device_id=sibling, device_id_type=MESH)

        def taps(j, to=None):
            cx, cy = chips[j]
            return pltpu.make_async_remote_copy(
                src_ref=cw_ref, dst_ref=cwf_ref.at[k_me if to else 2 * cx + cy],
                send_sem=cw_send.at[j], recv_sem=cw_recv.at[j], device_id=(cx, cy, c), device_id_type=MESH)

        sends = [ici(a, j, to=True) for a in range(na) for j in range(3)]
        for cp in sends:
            cp.start()
        tap_sends = [taps(j, to=True) for j in range(3)]
        for cp in tap_sends:
            cp.start()
        passed = []
        for a in range(na):
            for j in range(3):
                ici(a, j).wait_recv()
                fwd = d2d(a, j, c)
                fwd.start()
                passed.append(fwd)
        for a in range(na):
            for j in range(3):
                d2d(a, j, 1 - c).wait_recv()
        for j in range(3):
            taps(j).wait_recv()
        for cp in sends + passed + tap_sends:
            cp.wait_send()
        for cp in local:
            cp.wait()

    out_shape = [jax.ShapeDtypeStruct(f.shape, f.dtype) for f in fulls]
    out_shape.append(jax.ShapeDtypeStruct((N_CHIPS,) + conv_w.shape, conv_w.dtype))
    return pl.pallas_call(
        body, name="gather_weights", in_specs=[HBM_SPEC] * (na + 1), out_specs=[HBM_SPEC] * (na + 1),
        out_shape=out_shape, input_output_aliases={a: a for a in range(na)},
        scratch_shapes=[pltpu.SemaphoreType.DMA((na * 3,))] * 4
        + [pltpu.SemaphoreType.DMA((3,))] * 2 + [pltpu.SemaphoreType.DMA],
    )(*fulls, conv_w)


def _swap_core_halves(gs):
    na = len(gs)

    def body(*refs):
        g_refs, o_refs = refs[:na], refs[na:2 * na]
        send_sems, recv_sems = refs[2 * na:]
        x, y, c, _ = _mesh_place()
        copies = []
        for a in range(na):
            r2 = g_refs[a].shape[1] // 2
            copies.append(pltpu.make_async_remote_copy(
                src_ref=g_refs[a].at[:, pl.ds((1 - c) * r2, r2), :], dst_ref=o_refs[a],
                send_sem=send_sems.at[a], recv_sem=recv_sems.at[a], device_id=(x, y, 1 - c), device_id_type=MESH))
        for cp in copies:
            cp.start()
        for cp in copies:
            cp.wait()

    return pl.pallas_call(
        body, name="swap_core_halves", in_specs=[HBM_SPEC] * na, out_specs=[HBM_SPEC] * na,
        out_shape=[jax.ShapeDtypeStruct((g.shape[0], g.shape[1] // 2, g.shape[2]), g.dtype) for g in gs],
        scratch_shapes=[pltpu.SemaphoreType.DMA((na,))] * 2,
    )(*gs)


def _add_core_halves(name, g, sib, core):
    nb, r, cols = g.shape
    r2 = r // 2
    tr = _row_tile(r2, cols, itemsize=2, budget=1 << 20)
    nrt = r2 // tr

    def body(core_ref, g_ref, s_ref, o_ref):
        o_ref[...] = (g_ref[...].astype(F32) + s_ref[...].astype(F32)).astype(o_ref.dtype)

    return pl.pallas_call(
        body, name=name,
        grid_spec=pltpu.PrefetchScalarGridSpec(
            num_scalar_prefetch=1, grid=(nb, nrt),
            in_specs=[pl.BlockSpec((None, tr, cols), lambda k, i, core_ref: (k, core_ref[0] * nrt + i, 0)),
                      pl.BlockSpec((None, tr, cols), lambda k, i, core_ref: (k, i, 0))],
            out_specs=pl.BlockSpec((None, tr, cols), lambda k, i, core_ref: (k, i, 0))),
        out_shape=jax.ShapeDtypeStruct((nb, r2, cols), BF16), compiler_params=_params("parallel", "parallel"),
    )(core, g, sib)


def _scatter_to_chips(hs):
    na = len(hs)

    def body(*refs):
        h_refs, o_refs = refs[:na], refs[na:2 * na]
        send_sems, recv_sems = refs[2 * na:]
        x, y, c, chips = _mesh_place()
        copies = []
        for a in range(na):
            for j, (cx, cy) in enumerate(chips):
                copies.append(pltpu.make_async_remote_copy(
                    src_ref=h_refs[a].at[2 * cx + cy], dst_ref=o_refs[a].at[j],
                    send_sem=send_sems.at[a * 3 + j], recv_sem=recv_sems.at[a * 3 + j],
                    device_id=(cx, cy, c), device_id_type=MESH))
        for cp in copies:
            cp.start()
        for cp in copies:
            cp.wait()

    return pl.pallas_call(
        body, name="scatter_to_chips", in_specs=[HBM_SPEC] * na, out_specs=[HBM_SPEC] * na,
        out_shape=[jax.ShapeDtypeStruct((N_CHIPS - 1,) + h.shape[1:], h.dtype) for h in hs],
        scratch_shapes=[pltpu.SemaphoreType.DMA((na * 3,))] * 2,
    )(*hs)


def _sum_chips(name, hs, rcv, core, chip, layer, n_layers, prev):
    _, r2, cols = hs.shape
    tr = _row_tile(r2, cols, budget=1 << 20)
    nrt = r2 // tr

    def body(core_ref, chip_ref, h_ref, r_ref, *rest):
        o_ref = rest[-1]
        acc = h_ref[...].astype(F32)
        for j in range(N_CHIPS - 1):
            acc = acc + r_ref[j].astype(F32)
        o_ref[...] = acc

    in_specs = [pl.BlockSpec((None, tr, cols), lambda i, core_ref, chip_ref: (chip_ref[0], i, 0)),
                pl.BlockSpec((N_CHIPS - 1, tr, cols), lambda i, core_ref, chip_ref: (0, i, 0))]
    args = [core, chip, hs, rcv]
    aliases = {}
    if prev is not None:
        in_specs.append(pl.BlockSpec(memory_space=pl.ANY))
        args.append(prev)
        aliases = {4: 0}
    return pl.pallas_call(
        body, name=name,
        grid_spec=pltpu.PrefetchScalarGridSpec(
            num_scalar_prefetch=2, grid=(nrt,), in_specs=in_specs,
            out_specs=pl.BlockSpec((None, tr, cols), lambda i, core_ref, chip_ref: (layer, core_ref[0] * nrt + i, 0))),
        out_shape=jax.ShapeDtypeStruct((n_layers, 2 * r2, cols), F32), input_output_aliases=aliases,
        compiler_params=_params("parallel"),
    )(*args)


def _join_core_halves(ts):
    na = len(ts)

    def body(*refs):
        o_refs = refs[na:2 * na]
        send_sems, recv_sems = refs[2 * na:]
        x, y, c, _ = _mesh_place()
        copies = []
        for a in range(na):
            r2 = o_refs[a].shape[1] // 2
            mine = o_refs[a].at[:, pl.ds(c * r2, r2), :]
            copies.append(pltpu.make_async_remote_copy(
                src_ref=mine, dst_ref=mine, send_sem=send_sems.at[a], recv_sem=recv_sems.at[a],
                device_id=(x, y, 1 - c), device_id_type=MESH))
        for cp in copies:
            cp.start()
        for cp in copies:
            cp.wait()

    return pl.pallas_call(
        body, name="join_core_halves", in_specs=[HBM_SPEC] * na, out_specs=[HBM_SPEC] * na,
        out_shape=[jax.ShapeDtypeStruct(t.shape, t.dtype) for t in ts],
        input_output_aliases={a: a for a in range(na)},
        scratch_shapes=[pltpu.SemaphoreType.DMA((na,))] * 2,
    )(*ts)


def _allreduce_small(p):
    n, _, w = p.shape

    def body(p_ref, o_ref, buf, send_sems, recv_sems):
        x, y, c, _ = _mesh_place()
        me = 4 * x + 2 * y + c
        buf[me] = jnp.sum(p_ref[...], axis=1)
        copies = []
        for pat in range(1, N_DEV):
            fx, fy, fc = (pat >> 2) & 1, (pat >> 1) & 1, pat & 1
            copies.append(pltpu.make_async_remote_copy(
                src_ref=buf.at[me], dst_ref=buf.at[me], send_sem=send_sems.at[pat - 1], recv_sem=recv_sems.at[pat - 1],
                device_id=(x ^ fx, y ^ fy, c ^ fc), device_id_type=MESH))
        for cp in copies:
            cp.start()
        for cp in copies:
            cp.wait()
        acc = buf[0]
        for dev in range(1, N_DEV):
            acc = acc + buf[dev]
        o_ref[...] = acc

    return pl.pallas_call(
        body, name="allreduce_small", in_specs=[pl.BlockSpec(memory_space=pltpu.VMEM)],
        out_specs=pl.BlockSpec(memory_space=pltpu.VMEM), out_shape=jax.ShapeDtypeStruct((n, w), F32),
        scratch_shapes=[pltpu.VMEM((N_DEV, n, w), F32), pltpu.SemaphoreType.DMA((N_DEV - 1,)),
                        pltpu.SemaphoreType.DMA((N_DEV - 1,))],
    )(p)


def _ffn_forward(tag, x, g_pre, g_post, gu_w, dn_w):
    s, d = x.shape
    f = dn_w.shape[0]
    h = _norm_fwd(f"{tag}_norm", x, g_pre)
    gu, a = _ffn_up(f"{tag}_up", h, gu_w)
    tk = _tile(f, 1408)
    tm, tn = _tile(s, 1024), _tile(d, 1024)
    y = _mm(f"{tag}_down", a, dn_w, mode="nn", grid=(s // tm, d // tn, f // tk),
            a_spec=pl.BlockSpec((tm, tk), lambda i, j, k: (i, k)),
            b_spec=pl.BlockSpec((tk, tn), lambda i, j, k: (k, j)),
            o_spec=pl.BlockSpec((tm, tn), lambda i, j, k: (i, j)),
            out_shape=jax.ShapeDtypeStruct((s, d), F32), nk=f // tk, acc_shape=(tm, tn))
    x_new = _res_norm(f"{tag}_post", x, y, g_post, FFN_RESIDUAL_WEIGHT)
    return x_new, (x, h, gu, a, y)


def _ffn_backward(tag, dx_new, saved, g_pre, g_post, gu_w, dn_w):
    x, h, gu, a, y = saved
    s, d = x.shape
    nb, fs = gu_w.shape[0], gu_w.shape[2]
    f = dn_w.shape[0]
    fr = f // nb
    dy, dg_post = _norm_bwd(f"{tag}_post_bwd", dx_new, y, g_post, FFN_RESIDUAL_WEIGHT, None, BF16)
    dgu = _ffn_dact(f"{tag}_dact", dy, dn_w, gu)
    dgu4 = dgu.reshape(nb, s, fs)
    tn = _tile(d, 1024)
    d_wd = _mm(f"{tag}_dwd", a, dy, mode="tn", grid=(nb, d // tn),
               a_spec=pl.BlockSpec((s, fr), lambda i, j: (0, i)),
               b_spec=pl.BlockSpec((s, tn), lambda i, j: (0, j)),
               o_spec=pl.BlockSpec((None, fr, tn), lambda i, j: (i, 0, j)),
               out_shape=jax.ShapeDtypeStruct((nb, fr, d), BF16))
    tm, tw = _tile(d, 512), _tile(fs, 1408)
    nw = fs // tw
    d_wgu = _mm(f"{tag}_dwgu", h, dgu4, mode="tn", grid=(nb, nw, d // tm),
                a_spec=pl.BlockSpec((s, tm), lambda k, j, i: (0, i)),
                b_spec=pl.BlockSpec((None, s, tw), lambda k, j, i: (k, 0, j)),
                o_spec=pl.BlockSpec((None, tm, tw), lambda k, j, i: (k, i, j)),
                out_shape=jax.ShapeDtypeStruct((nb, d, fs), BF16))
    ts, td = _tile(s, 1024), _tile(d, 1024)
    dh = _mm(f"{tag}_dh", dgu4, gu_w, mode="nt", grid=(s // ts, d // td, nb),
             a_spec=pl.BlockSpec((None, ts, fs), lambda i, j, k: (k, i, 0)),
             b_spec=pl.BlockSpec((None, td, fs), lambda i, j, k: (k, j, 0)),
             o_spec=pl.BlockSpec((ts, td), lambda i, j, k: (i, j)),
             out_shape=jax.ShapeDtypeStruct((s, d), F32), nk=nb, acc_shape=(ts, td))
    dx, dg_pre = _norm_bwd(f"{tag}_pre_bwd", dh, x, g_pre, 1.0, dx_new, F32)
    return dx, d_wgu, d_wd, dg_pre, dg_post


def _mixer_forward(tag, x, gains, win_w, conv_taps, wout_w, dims):
    qd, kvd, cd = dims
    s, d = x.shape
    nb, cw = win_w.shape[0], win_w.shape[2]
    g_pre, g_a, g_c, g_post = gains
    h = _norm_fwd(f"{tag}_norm", x, g_pre)
    tm = _tile(s, 1024)
    z = _mm(f"{tag}_in", h, win_w, mode="nn", grid=(nb, s // tm),
            a_spec=pl.BlockSpec((tm, d), lambda j, i: (i, 0)),
            b_spec=pl.BlockSpec((None, d, cw), lambda j, i: (j, 0, 0)),
            o_spec=pl.BlockSpec((tm, cw), lambda j, i: (i, j)),
            out_shape=jax.ShapeDtypeStruct((s, nb * cw), BF16))
    a, lse = _attn_fwd(f"{tag}_attn", z, qd, kvd)
    c = _conv_fwd(f"{tag}_conv", z, conv_taps, qd + 2 * kvd, cd)
    cat = _cat_norm_fwd(f"{tag}_cat", a, c, g_a, g_c)
    mw = qd + cd
    tn = _tile(d, 1024)
    mixed = _mm(f"{tag}_out", cat, wout_w, mode="nn", grid=(s // tm, d // tn),
                a_spec=pl.BlockSpec((tm, mw), lambda i, j: (i, 0)),
                b_spec=pl.BlockSpec((mw, tn), lambda i, j: (0, j)),
                o_spec=pl.BlockSpec((tm, tn), lambda i, j: (i, j)),
                out_shape=jax.ShapeDtypeStruct((s, d), F32))
    x_new = _res_norm(f"{tag}_post", x, mixed, g_post, 1.0)
    return x_new, (x, h, z, a, lse, c, cat, mixed)


def _mixer_backward(tag, dx_new, saved, gains, win_w, conv_taps, wout_w, dims):
    qd, kvd, cd = dims
    x, h, z, a, lse, c, cat, mixed = saved
    s, d = x.shape
    nb, cw = win_w.shape[0], win_w.shape[2]
    g_pre, g_a, g_c, g_post = gains
    mw = qd + cd
    dmixed, dg_post = _norm_bwd(f"{tag}_post_bwd", dx_new, mixed, g_post, 1.0, None, BF16)
    tm, tn = _tile(s, 1024), _tile(mw, 1024)
    dcat = _mm(f"{tag}_dcat", dmixed, wout_w, mode="nt", grid=(s // tm, mw // tn),
               a_spec=pl.BlockSpec((tm, d), lambda i, j: (i, 0)),
               b_spec=pl.BlockSpec((tn, d), lambda i, j: (j, 0)),
               o_spec=pl.BlockSpec((tm, tn), lambda i, j: (i, j)),
               out_shape=jax.ShapeDtypeStruct((s, mw), F32))
    wr = mw // nb
    td = _tile(d, 1024)
    d_wout = _mm(f"{tag}_dwout", cat, dmixed, mode="tn", grid=(nb, d // td),
                 a_spec=pl.BlockSpec((s, wr), lambda i, j: (0, i)),
                 b_spec=pl.BlockSpec((s, td), lambda i, j: (0, j)),
                 o_spec=pl.BlockSpec((None, wr, td), lambda i, j: (i, 0, j)),
                 out_shape=jax.ShapeDtypeStruct((nb, wr, d), BF16))
    da, dc, dg_a, dg_c = _cat_norm_bwd(f"{tag}_cat_bwd", dcat, a, c, g_a, g_c)
    dhc, dbg, dcg, d_taps = _conv_bwd(f"{tag}_conv_bwd", z, conv_taps, dc, qd + 2 * kvd, cd)
    dq, dk, dv = _attn_bwd(f"{tag}_attn_bwd", z, a, lse, da, qd, kvd)
    dz = jnp.concatenate([dq, dk, dv, dhc, dbg, dcg], axis=1)
    th = _tile(d, 512)
    d_win = _mm(f"{tag}_dwin", h, dz, mode="tn", grid=(nb, d // th),
                a_spec=pl.BlockSpec((s, th), lambda k, i: (0, i)),
                b_spec=pl.BlockSpec((s, cw), lambda k, i: (0, k)),
                o_spec=pl.BlockSpec((None, th, cw), lambda k, i: (k, i, 0)),
                out_shape=jax.ShapeDtypeStruct((nb, d, cw), BF16))
    dh = _mm(f"{tag}_dh", dz, win_w, mode="nt", grid=(s // tm, d // td, nb),
             a_spec=pl.BlockSpec((tm, cw), lambda i, j, k: (i, k)),
             b_spec=pl.BlockSpec((None, td, cw), lambda i, j, k: (k, j, 0)),
             o_spec=pl.BlockSpec((tm, td), lambda i, j, k: (i, j)),
             out_shape=jax.ShapeDtypeStruct((s, d), F32), nk=nb, acc_shape=(tm, td))
    dx, dg_pre = _norm_bwd(f"{tag}_pre_bwd", dh, x, g_pre, 1.0, dx_new, F32)
    return dx, d_win, d_wout, d_taps, (dg_pre, dg_a, dg_c, dg_post)


def _pad_cols(v, width):
    return jnp.pad(v, ((0, 0), (0, width - v.shape[1])))


def kernel(x, ffn1_norm_pre, ffn1_w_gate_up, ffn1_w_down, ffn1_norm_post, mix_norm_pre, w_in, conv_w, attn_out_norm, conv_out_norm, w_out, mix_norm_post, ffn2_norm_pre, ffn2_w_gate_up, ffn2_w_down, ffn2_norm_post, loss_target, m_ffn1_norm_pre, m_ffn1_w_gate_up, m_ffn1_w_down, m_ffn1_norm_post, m_mix_norm_pre, m_w_in, m_conv_w, m_attn_out_norm, m_conv_out_norm, m_w_out, m_mix_norm_post, m_ffn2_norm_pre, m_ffn2_w_gate_up, m_ffn2_w_down, m_ffn2_norm_post, v_ffn1_norm_pre, v_ffn1_w_gate_up, v_ffn1_w_down, v_ffn1_norm_post, v_mix_norm_pre, v_w_in, v_conv_w, v_attn_out_norm, v_conv_out_norm, v_w_out, v_mix_norm_post, v_ffn2_norm_pre, v_ffn2_w_gate_up, v_ffn2_w_down, v_ffn2_norm_post):
    _, s, d = x.shape
    n_layers = ffn1_norm_pre.shape[0]
    qd = attn_out_norm.shape[1]
    cd = conv_out_norm.shape[1]
    kvd = qd // Q_PER_KV
    dims = (qd, kvd, cd)
    assert N_CHIPS * w_in.shape[2] == qd + 2 * kvd + 3 * cd and qd + cd == N_CHIPS * w_out.shape[1]
    assert 2 * d <= SMALL_ROWS * LANES * SUBLANES
    chip = 2 * lax.axis_index("x") + lax.axis_index("y")
    chip_arr = chip.astype(jnp.int32).reshape(1)
    core = lax.axis_index("c").astype(jnp.int32).reshape(1)
    kinds = ("gu1", "dn1", "win", "wout", "gu2", "dn2")

    big = (ffn1_w_gate_up, ffn1_w_down, w_in, w_out, ffn2_w_gate_up, ffn2_w_down)
    own = [_cast_into_slot(f"cast_{k}_{layer}", w, layer, chip_arr)
           for k, w in zip(kinds, big) for layer in range(n_layers)]
    gathered = _gather_weights(own, conv_w)
    wts = {k: [gathered[i * n_layers + layer] for layer in range(n_layers)] for i, k in enumerate(kinds)}
    for k in ("dn1", "wout", "dn2"):
        wts[k] = [w.reshape(-1, d) for w in wts[k]]
    taps_all = gathered[-1]
    taps = jnp.transpose(taps_all, (1, 2, 0, 3)).reshape(n_layers, CONV_WIDTH, cd)
    taps = jnp.pad(taps, ((0, 0), (0, SUBLANES - CONV_WIDTH), (0, 0)))

    def gain(g, layer):
        return g[layer][None, :]

    xs = x[0]
    saved = []
    for layer in range(n_layers):
        t = f"l{layer}"
        xs, s1 = _ffn_forward(f"{t}_ffn1", xs, gain(ffn1_norm_pre, layer), gain(ffn1_norm_post, layer),
                              wts["gu1"][layer], wts["dn1"][layer])
        mix_gains = (gain(mix_norm_pre, layer), gain(attn_out_norm, layer), gain(conv_out_norm, layer), gain(mix_norm_post, layer))
        xs, s2 = _mixer_forward(f"{t}_mix", xs, mix_gains, wts["win"][layer], taps[layer], wts["wout"][layer], dims)
        xs, s3 = _ffn_forward(f"{t}_ffn2", xs, gain(ffn2_norm_pre, layer), gain(ffn2_norm_post, layer),
                              wts["gu2"][layer], wts["dn2"][layer])
        saved.append((s1, s2, s3, mix_gains))
    dxs, loss_part = _loss_head("loss_head", xs, loss_target[0])
    loss = lax.psum(jnp.sum(loss_part), ("x", "y", "c"))

    grads = {k: [None] * n_layers for k in kinds}
    small = [None] * n_layers
    for layer in reversed(range(n_layers)):
        t = f"l{layer}"
        s1, s2, s3, mix_gains = saved[layer]
        dxs, grads["gu2"][layer], grads["dn2"][layer], p_pre2, p_post2 = _ffn_backward(
            f"{t}_ffn2", dxs, s3, gain(ffn2_norm_pre, layer), gain(ffn2_norm_post, layer),
            wts["gu2"][layer], wts["dn2"][layer])
        dxs, grads["win"][layer], grads["wout"][layer], p_taps, (p_mpre, p_a, p_c, p_mpost) = _mixer_backward(
            f"{t}_mix", dxs, s2, mix_gains, wts["win"][layer], taps[layer], wts["wout"][layer], dims)
        dxs, grads["gu1"][layer], grads["dn1"][layer], p_pre1, p_post1 = _ffn_backward(
            f"{t}_ffn1", dxs, s1, gain(ffn1_norm_pre, layer), gain(ffn1_norm_post, layer),
            wts["gu1"][layer], wts["dn1"][layer])
        tap_rows = jnp.zeros((CONV_WIDTH, SUBLANES, d), F32).at[:, 0, :cd].set(p_taps[:CONV_WIDTH])
        rows = [p_pre1, p_post1, p_mpre, jnp.concatenate([p_a, p_c], axis=1), p_mpost, p_pre2, p_post2]
        rows = jnp.concatenate([jnp.stack(rows), tap_rows], axis=0)
        small[layer] = jnp.pad(rows, ((0, SMALL_ROWS - rows.shape[0]), (0, 0), (0, 0)))
    grad_x = dxs[None]

    parts = [grads[k][layer] for k in kinds for layer in range(n_layers)]
    from_sibling = _swap_core_halves(parts)
    core_sums = [_add_core_halves(f"add_cores_{i}", g, sb, core) for i, (g, sb) in enumerate(zip(parts, from_sibling))]
    from_chips = _scatter_to_chips(core_sums)
    reduced = []
    for i in range(len(kinds)):
        buf = None
        for layer in range(n_layers):
            a = i * n_layers + layer
            buf = _sum_chips(f"sum_chips_{a}", core_sums[a], from_chips[a], core, chip_arr, layer, n_layers, buf)
        reduced.append(buf)
    g_gu1, g_dn1, g_win, g_wout, g_gu2, g_dn2 = _join_core_halves(reduced)

    small_sum = _allreduce_small(jnp.concatenate(small, axis=0)).reshape(n_layers, SMALL_ROWS, d)
    g_ffn1_pre, g_ffn1_post, g_mix_pre = small_sum[:, 0], small_sum[:, 1], small_sum[:, 2]
    g_attn_out, g_conv_out = small_sum[:, 3, :qd], small_sum[:, 3, qd:qd + cd]
    g_mix_post, g_ffn2_pre, g_ffn2_post = small_sum[:, 4], small_sum[:, 5], small_sum[:, 6]
    cc = conv_w.shape[2]
    g_conv = lax.dynamic_slice_in_dim(small_sum[:, 7:7 + CONV_WIDTH, :cd], chip * cc, cc, axis=2)

    weights = dict(ffn1_norm_pre=ffn1_norm_pre, ffn1_w_gate_up=ffn1_w_gate_up, ffn1_w_down=ffn1_w_down, ffn1_norm_post=ffn1_norm_post, mix_norm_pre=mix_norm_pre, w_in=w_in, conv_w=conv_w, attn_out_norm=attn_out_norm, conv_out_norm=conv_out_norm, w_out=w_out, mix_norm_post=mix_norm_post, ffn2_norm_pre=ffn2_norm_pre, ffn2_w_gate_up=ffn2_w_gate_up, ffn2_w_down=ffn2_w_down, ffn2_norm_post=ffn2_norm_post)
    m_in = dict(ffn1_norm_pre=m_ffn1_norm_pre, ffn1_w_gate_up=m_ffn1_w_gate_up, ffn1_w_down=m_ffn1_w_down, ffn1_norm_post=m_ffn1_norm_post, mix_norm_pre=m_mix_norm_pre, w_in=m_w_in, conv_w=m_conv_w, attn_out_norm=m_attn_out_norm, conv_out_norm=m_conv_out_norm, w_out=m_w_out, mix_norm_post=m_mix_norm_post, ffn2_norm_pre=m_ffn2_norm_pre, ffn2_w_gate_up=m_ffn2_w_gate_up, ffn2_w_down=m_ffn2_w_down, ffn2_norm_post=m_ffn2_norm_post)
    v_in = dict(ffn1_norm_pre=v_ffn1_norm_pre, ffn1_w_gate_up=v_ffn1_w_gate_up, ffn1_w_down=v_ffn1_w_down, ffn1_norm_post=v_ffn1_norm_post, mix_norm_pre=v_mix_norm_pre, w_in=v_w_in, conv_w=v_conv_w, attn_out_norm=v_attn_out_norm, conv_out_norm=v_conv_out_norm, w_out=v_w_out, mix_norm_post=v_mix_norm_post, ffn2_norm_pre=v_ffn2_norm_pre, ffn2_w_gate_up=v_ffn2_w_gate_up, ffn2_w_down=v_ffn2_w_down, ffn2_norm_post=v_ffn2_norm_post)
    grad = dict(ffn1_norm_pre=g_ffn1_pre, ffn1_w_gate_up=g_gu1, ffn1_w_down=g_dn1, ffn1_norm_post=g_ffn1_post, mix_norm_pre=g_mix_pre, w_in=g_win, conv_w=g_conv, attn_out_norm=g_attn_out, conv_out_norm=g_conv_out, w_out=g_wout, mix_norm_post=g_mix_post, ffn2_norm_pre=g_ffn2_pre, ffn2_w_gate_up=g_gu2, ffn2_w_down=g_dn2, ffn2_norm_post=g_ffn2_post)
    names = list(weights)

    delta, new_m, new_v = {}, {}, {}
    matrices = ("ffn1_w_gate_up", "ffn1_w_down", "w_in", "w_out", "ffn2_w_gate_up", "ffn2_w_down")
    for n in matrices:
        delta[n], new_m[n], new_v[n] = _adamw(f"adamw_{n}", weights[n], grad[n], m_in[n], v_in[n])
    vectors = [n for n in names if n not in matrices]

    def pack(tree):
        flat = jnp.concatenate([tree[n].reshape(-1) for n in vectors])
        return jnp.pad(flat, (0, -flat.size % (SUBLANES * LANES))).reshape(-1, LANES)

    packed = _adamw("adamw_small", pack(weights), pack(grad), pack(m_in), pack(v_in))
    offset = 0
    for n in vectors:
        size = weights[n].size
        for tree, flat in zip((delta, new_m, new_v), packed):
            tree[n] = flat.reshape(-1)[offset:offset + size].reshape(weights[n].shape)
        offset += size

    return (loss, grad_x, *[grad[n] for n in names], *[delta[n] for n in names],
            *[new_m[n] for n in names], *[new_v[n] for n in names])
```

```python
import functools

import jax
import jax.numpy as jnp
from jax import lax
from jax.experimental import pallas as pl
from jax.experimental.pallas import tpu as pltpu

F32 = jnp.float32
BF16 = jnp.bfloat16
MESH = pl.DeviceIdType.MESH

NORM_EPS = 1e-6
HEAD_DIM = 128
Q_PER_KV = 4
CONV_WIDTH = 3
FFN_RESIDUAL_WEIGHT = 0.5
DILATED_BRANCHES = ((128, 1), (512, 4), (2048, 16))
ADAM_LR = 0.001
ADAM_B1 = 0.9
ADAM_B2 = 0.999
ADAM_EPS = 1e-08
ADAM_WD = 0.01
ADAM_STEP = 10

N_CHIPS = 4
N_DEV = 8
V7X_VMEM_BYTES = 64 << 20
VMEM_LIMIT = V7X_VMEM_BYTES - (12 << 20)
SUBLANES = 8
LANES = 128
SMALL_ROWS = 16


def _params(*sem):
    return pltpu.CompilerParams(dimension_semantics=sem, vmem_limit_bytes=VMEM_LIMIT)


def _row_tile(rows, cols, itemsize=4, budget=2 << 20):
    t = rows
    while t * cols * itemsize > budget and t % 32 == 0:
        t //= 2
    return t


def _sum_to_sublanes(v):
    r, n = v.shape
    return v.reshape(r // SUBLANES, SUBLANES, n).sum(axis=0)


_DIMS = {
    "nn": (((1,), (0,)), ((), ())),
    "nt": (((1,), (1,)), ((), ())),
    "tn": (((0,), (0,)), ((), ())),
}


def _dot(a, b, mode):
    return lax.dot_general(a, b, _DIMS[mode], preferred_element_type=F32)


def _mm(name, a, b, *, mode, grid, a_spec, b_spec, o_spec, out_shape, nk=1, acc_shape=None):
    def body(a_ref, b_ref, o_ref, *scratch):
        r = _dot(a_ref[...], b_ref[...], mode)
        if nk == 1:
            o_ref[...] = r.astype(o_ref.dtype)
        else:
            acc = scratch[0]
            k = pl.program_id(len(grid) - 1)

            @pl.when(k == 0)
            def _():
                acc[...] = r

            @pl.when(k > 0)
            def _():
                acc[...] += r

            @pl.when(k == nk - 1)
            def _():
                o_ref[...] = acc[...].astype(o_ref.dtype)

    sem = ("parallel",) * (len(grid) - (1 if nk > 1 else 0)) + (("arbitrary",) if nk > 1 else ())
    return pl.pallas_call(
        body, name=name, grid=grid, in_specs=[a_spec, b_spec], out_specs=o_spec, out_shape=out_shape,
        scratch_shapes=[pltpu.VMEM(acc_shape, F32)] if nk > 1 else [],
        compiler_params=_params(*sem),
    )(a, b)


def _tile(n, want):
    if n <= want:
        return n
    best = None
    for t in range(LANES, want + 1, LANES):
        if n % t == 0:
            best = t
    assert best is not None, (n, want)
    return best


def _norm_fwd(name, x, gain):
    s, d = x.shape
    tr = _row_tile(s, d)

    def body(x_ref, g_ref, o_ref):
        xv = x_ref[...]
        r = lax.rsqrt(jnp.mean(xv * xv, axis=-1, keepdims=True) + NORM_EPS)
        o_ref[...] = (xv * r * g_ref[...]).astype(o_ref.dtype)

    return pl.pallas_call(
        body, name=name, grid=(s // tr,),
        in_specs=[pl.BlockSpec((tr, d), lambda i: (i, 0)), pl.BlockSpec((1, d), lambda i: (0, 0))],
        out_specs=pl.BlockSpec((tr, d), lambda i: (i, 0)),
        out_shape=jax.ShapeDtypeStruct((s, d), BF16), compiler_params=_params("parallel"),
    )(x, gain)


def _res_norm(name, x, y, gain, scale):
    s, d = x.shape
    tr = _row_tile(s, d)

    def body(x_ref, y_ref, g_ref, o_ref):
        yv = y_ref[...]
        r = lax.rsqrt(jnp.mean(yv * yv, axis=-1, keepdims=True) + NORM_EPS)
        o_ref[...] = x_ref[...] + scale * (yv * r * g_ref[...])

    row = pl.BlockSpec((tr, d), lambda i: (i, 0))
    return pl.pallas_call(
        body, name=name, grid=(s // tr,),
        in_specs=[row, row, pl.BlockSpec((1, d), lambda i: (0, 0))], out_specs=row,
        out_shape=jax.ShapeDtypeStruct((s, d), F32), compiler_params=_params("parallel"),
    )(x, y, gain)


def _norm_bwd(name, dout, yin, gain, scale, resid, out_dtype):
    s, d = yin.shape
    tr = _row_tile(s, d)
    has_resid = resid is not None

    def body(*refs):
        if has_resid:
            do_ref, y_ref, g_ref, r_ref, di_ref, dg_ref = refs
        else:
            do_ref, y_ref, g_ref, di_ref, dg_ref = refs
        yv = y_ref[...]
        r = lax.rsqrt(jnp.mean(yv * yv, axis=-1, keepdims=True) + NORM_EPS)
        xhat = yv * r
        dn = scale * do_ref[...]
        part = _sum_to_sublanes(dn * xhat)

        @pl.when(pl.program_id(0) == 0)
        def _():
            dg_ref[...] = part

        @pl.when(pl.program_id(0) > 0)
        def _():
            dg_ref[...] += part

        dxn = dn * g_ref[...]
        din = r * (dxn - xhat * jnp.mean(dxn * xhat, axis=-1, keepdims=True))
        if has_resid:
            din = din + r_ref[...]
        di_ref[...] = din.astype(di_ref.dtype)

    row = pl.BlockSpec((tr, d), lambda i: (i, 0))
    vec = pl.BlockSpec((1, d), lambda i: (0, 0))
    ins = [row, row, vec] + ([row] if has_resid else [])
    args = (dout, yin, gain) + ((resid,) if has_resid else ())
    return pl.pallas_call(
        body, name=name, grid=(s // tr,), in_specs=ins,
        out_specs=[row, pl.BlockSpec((SUBLANES, d), lambda i: (0, 0))],
        out_shape=[jax.ShapeDtypeStruct((s, d), out_dtype), jax.ShapeDtypeStruct((SUBLANES, d), F32)],
        compiler_params=_params("arbitrary"),
    )(*args)


def _loss_head(name, y, target):
    s, d = y.shape
    tr = _row_tile(s, d)

    def body(y_ref, t_ref, dy_ref, l_ref):
        e = y_ref[...] - t_ref[...]
        dy_ref[...] = e * (1.0 / d)
        part = _sum_to_sublanes(e * e) * (0.5 / d)

        @pl.when(pl.program_id(0) == 0)
        def _():
            l_ref[...] = part

        @pl.when(pl.program_id(0) > 0)
        def _():
            l_ref[...] += part

    row = pl.BlockSpec((tr, d), lambda i: (i, 0))
    return pl.pallas_call(
        body, name=name, grid=(s // tr,), in_specs=[row, row],
        out_specs=[row, pl.BlockSpec((SUBLANES, d), lambda i: (0, 0))],
        out_shape=[jax.ShapeDtypeStruct((s, d), F32), jax.ShapeDtypeStruct((SUBLANES, d), F32)],
        compiler_params=_params("arbitrary"),
    )(y, target)


def _ffn_up(name, h, gu_w):
    s, d = h.shape
    nb, _, fs = gu_w.shape
    hb = nb // 2
    w = gu_w.reshape(2, hb, d, fs)
    tm = _tile(s, 512)
    tn = _tile(fs, 1408)
    nj = fs // tn

    def body(h_ref, w_ref, gu_ref, a_ref):
        hv = h_ref[...]
        g = _dot(hv, w_ref[0], "nn")
        u = _dot(hv, w_ref[1], "nn")
        gu_ref[0] = g.astype(gu_ref.dtype)
        gu_ref[1] = u.astype(gu_ref.dtype)
        a_ref[...] = (g * jax.nn.sigmoid(g) * u).astype(a_ref.dtype)

    return pl.pallas_call(
        body, name=name, grid=(hb, nj, s // tm),
        in_specs=[pl.BlockSpec((tm, d), lambda jb, jo, i: (i, 0)),
                  pl.BlockSpec((2, None, d, tn), lambda jb, jo, i: (0, jb, 0, jo))],
        out_specs=[pl.BlockSpec((2, None, tm, tn), lambda jb, jo, i: (0, jb, i, jo)),
                   pl.BlockSpec((tm, tn), lambda jb, jo, i: (i, jb * nj + jo))],
        out_shape=[jax.ShapeDtypeStruct((2, hb, s, fs), BF16), jax.ShapeDtypeStruct((s, hb * fs), BF16)],
        compiler_params=_params("parallel", "parallel", "parallel"),
    )(h, w)


def _ffn_dact(name, dy, dn_w, gu):
    s, d = dy.shape
    _, hb, _, fs = gu.shape
    tm = _tile(s, 512)
    tn = _tile(fs, 1408)
    nj = fs // tn

    def body(dy_ref, w_ref, gu_ref, o_ref):
        da = _dot(dy_ref[...], w_ref[...], "nt")
        g = gu_ref[0].astype(F32)
        u = gu_ref[1].astype(F32)
        sg = jax.nn.sigmoid(g)
        o_ref[0] = (da * u * (sg * (1.0 + g * (1.0 - sg)))).astype(o_ref.dtype)
        o_ref[1] = (da * (g * sg)).astype(o_ref.dtype)

    blk = pl.BlockSpec((2, None, tm, tn), lambda jb, jo, i: (0, jb, i, jo))
    return pl.pallas_call(
        body, name=name, grid=(hb, nj, s // tm),
        in_specs=[pl.BlockSpec((tm, d), lambda jb, jo, i: (i, 0)),
                  pl.BlockSpec((tn, d), lambda jb, jo, i: (jb * nj + jo, 0)),
                  blk],
        out_specs=blk, out_shape=jax.ShapeDtypeStruct(gu.shape, BF16),
        compiler_params=_params("parallel", "parallel", "parallel"),
    )(dy, dn_w, gu)


def _multiplicity(q0, tq, s):
    row = q0 + lax.broadcasted_iota(jnp.int32, (tq, s), 0)
    col = lax.broadcasted_iota(jnp.int32, (tq, s), 1)
    dist = row - col
    mult = jnp.zeros((tq, s), F32)
    for window, dilation in DILATED_BRANCHES:
        hit = (dist <= window) & ((dist & (dilation - 1)) == 0)
        mult = mult + hit.astype(F32)
    return jnp.where(dist >= 0, mult, 0.0)


_MASKED = -1e30


def _attn_specs(s, qd, kvd, tq):
    rw = Q_PER_KV * HEAD_DIM
    qspec = pl.BlockSpec((tq, rw), lambda g, i: (i, g))
    kspec = pl.BlockSpec((s, HEAD_DIM), lambda g, i: (0, qd // HEAD_DIM + g))
    vspec = pl.BlockSpec((s, HEAD_DIM), lambda g, i: (0, (qd + kvd) // HEAD_DIM + g))
    return rw, qspec, kspec, vspec


def _attn_fwd(name, z, qd, kvd):
    s = z.shape[0]
    tq = _tile(s, 256)
    nkv = kvd // HEAD_DIM
    rw, qspec, kspec, vspec = _attn_specs(s, qd, kvd, tq)
    scale = HEAD_DIM ** -0.5

    def body(q_ref, k_ref, v_ref, o_ref, l_ref):
        mult = _multiplicity(pl.program_id(1) * tq, tq, s)
        live = mult > 0.0
        kv, vv = k_ref[...], v_ref[...]
        for h in range(Q_PER_KV):
            cols = slice(h * HEAD_DIM, (h + 1) * HEAD_DIM)
            sc = jnp.where(live, _dot(q_ref[:, cols], kv, "nt") * scale, _MASKED)
            mx = jnp.max(sc, axis=-1, keepdims=True)
            p = jnp.exp(sc - mx) * mult
            den = jnp.sum(p, axis=-1, keepdims=True)
            o_ref[:, cols] = _dot(p.astype(BF16), vv, "nn") / den
            l_ref[:, cols] = jnp.broadcast_to(mx + jnp.log(den), (tq, HEAD_DIM))

    return pl.pallas_call(
        body, name=name, grid=(nkv, s // tq), in_specs=[qspec, kspec, vspec], out_specs=[qspec, qspec],
        out_shape=[jax.ShapeDtypeStruct((s, qd), F32), jax.ShapeDtypeStruct((s, qd), F32)],
        compiler_params=_params("parallel", "parallel"),
    )(z, z, z)


def _attn_bwd(name, z, o, lse, do, qd, kvd):
    s = z.shape[0]
    tq = _tile(s, 256)
    nkv = kvd // HEAD_DIM
    nq = s // tq
    rw, qspec, kspec, vspec = _attn_specs(s, qd, kvd, tq)
    scale = HEAD_DIM ** -0.5

    def body(q_ref, k_ref, v_ref, o_ref, l_ref, do_ref, dq_ref, dk_ref, dv_ref, dk_acc, dv_acc):
        i = pl.program_id(1)
        mult = _multiplicity(i * tq, tq, s)
        live = mult > 0.0
        kv, vv = k_ref[...], v_ref[...]

        @pl.when(i == 0)
        def _():
            dk_acc[...] = jnp.zeros_like(dk_acc)
            dv_acc[...] = jnp.zeros_like(dv_acc)

        for h in range(Q_PER_KV):
            cols = slice(h * HEAD_DIM, (h + 1) * HEAD_DIM)
            q = q_ref[:, cols]
            dov = do_ref[:, cols]
            sc = jnp.where(live, _dot(q, kv, "nt") * scale, _MASKED)
            p = jnp.exp(sc - l_ref[:, cols][:, :1]) * mult
            dob = dov.astype(BF16)
            dp = _dot(dob, vv, "nt")
            delta = jnp.sum(dov * o_ref[:, cols], axis=-1, keepdims=True)
            ds = (p * (dp - delta) * scale).astype(BF16)
            dq_ref[:, cols] = _dot(ds, kv, "nn").astype(dq_ref.dtype)
            dk_acc[...] += _dot(ds, q, "tn")
            dv_acc[...] += _dot(p.astype(BF16), dob, "tn")

        @pl.when(i == nq - 1)
        def _():
            dk_ref[...] = dk_acc[...].astype(dk_ref.dtype)
            dv_ref[...] = dv_acc[...].astype(dv_ref.dtype)

    kvout = pl.BlockSpec((s, HEAD_DIM), lambda g, i: (0, g))
    return pl.pallas_call(
        body, name=name, grid=(nkv, nq), in_specs=[qspec, kspec, vspec, qspec, qspec, qspec],
        out_specs=[qspec, kvout, kvout],
        out_shape=[jax.ShapeDtypeStruct((s, qd), BF16), jax.ShapeDtypeStruct((s, kvd), BF16),
                   jax.ShapeDtypeStruct((s, kvd), BF16)],
        scratch_shapes=[pltpu.VMEM((s, HEAD_DIM), F32), pltpu.VMEM((s, HEAD_DIM), F32)],
        compiler_params=_params("parallel", "arbitrary"),
    )(z, z, z, o, lse, do)


def _shift_down(v, n):
    rolled = pltpu.roll(v, n, 0)
    t = lax.broadcasted_iota(jnp.int32, v.shape, 0)
    return jnp.where(t >= n, rolled, 0.0)


def _shift_up(v, n):
    rows = v.shape[0]
    rolled = pltpu.roll(v, rows - n, 0)
    t = lax.broadcasted_iota(jnp.int32, v.shape, 0)
    return jnp.where(t < rows - n, rolled, 0.0)


def _conv_specs(s, base, cd, tc):
    zs = [pl.BlockSpec((s, tc), functools.partial(lambda j, off: (0, off + j), off=(base + n * cd) // tc))
          for n in range(3)]
    wspec = pl.BlockSpec((SUBLANES, tc), lambda j: (0, j))
    cspec = pl.BlockSpec((s, tc), lambda j: (0, j))
    return zs, wspec, cspec


def _conv_fwd(name, z, conv_w, base, cd):
    s = z.shape[0]
    tc = _tile(cd, 256)
    zs, wspec, cspec = _conv_specs(s, base, cd, tc)

    def body(h_ref, b_ref, c_ref, w_ref, o_ref):
        u = c_ref[...].astype(F32) * h_ref[...].astype(F32)
        y = w_ref[0:1, :] * _shift_down(u, 2) + w_ref[1:2, :] * _shift_down(u, 1) + w_ref[2:3, :] * u
        o_ref[...] = b_ref[...].astype(F32) * y

    return pl.pallas_call(
        body, name=name, grid=(cd // tc,), in_specs=zs + [wspec], out_specs=cspec,
        out_shape=jax.ShapeDtypeStruct((s, cd), F32), compiler_params=_params("parallel"),
    )(z, z, z, conv_w)


def _conv_bwd(name, z, conv_w, dc, base, cd):
    s = z.shape[0]
    tc = _tile(cd, 256)
    zs, wspec, cspec = _conv_specs(s, base, cd, tc)

    def body(h_ref, b_ref, c_ref, w_ref, dc_ref, dh_ref, db_ref, dcg_ref, dw_ref):
        hv, bv, cv = h_ref[...].astype(F32), b_ref[...].astype(F32), c_ref[...].astype(F32)
        u = cv * hv
        u1, u2 = _shift_down(u, 1), _shift_down(u, 2)
        w0, w1, w2 = w_ref[0:1, :], w_ref[1:2, :], w_ref[2:3, :]
        y = w0 * u2 + w1 * u1 + w2 * u
        dcv = dc_ref[...]
        db_ref[...] = (dcv * y).astype(db_ref.dtype)
        dy = dcv * bv
        du = w2 * dy + w1 * _shift_up(dy, 1) + w0 * _shift_up(dy, 2)
        dh_ref[...] = (du * cv).astype(dh_ref.dtype)
        dcg_ref[...] = (du * hv).astype(dcg_ref.dtype)
        g0 = jnp.sum(dy * u2, axis=0, keepdims=True)
        g1 = jnp.sum(dy * u1, axis=0, keepdims=True)
        g2 = jnp.sum(dy * u, axis=0, keepdims=True)
        r = lax.broadcasted_iota(jnp.int32, (SUBLANES, tc), 0)
        dw_ref[...] = jnp.where(r == 0, g0, jnp.where(r == 1, g1, jnp.where(r == 2, g2, 0.0)))

    return pl.pallas_call(
        body, name=name, grid=(cd // tc,), in_specs=zs + [wspec, cspec],
        out_specs=[cspec, cspec, cspec, wspec],
        out_shape=[jax.ShapeDtypeStruct((s, cd), BF16)] * 3 + [jax.ShapeDtypeStruct((SUBLANES, cd), F32)],
        compiler_params=_params("parallel"),
    )(z, z, z, conv_w, dc)


def _cat_norm_fwd(name, a, c, ga, gc):
    s, qd = a.shape
    cd = c.shape[1]
    tr = _row_tile(s, qd + cd)

    def body(a_ref, c_ref, ga_ref, gc_ref, o_ref):
        av, cv = a_ref[...], c_ref[...]
        ra = lax.rsqrt(jnp.mean(av * av, axis=-1, keepdims=True) + NORM_EPS)
        rc = lax.rsqrt(jnp.mean(cv * cv, axis=-1, keepdims=True) + NORM_EPS)
        o_ref[:, :qd] = (av * ra * ga_ref[...]).astype(o_ref.dtype)
        o_ref[:, qd:] = (cv * rc * gc_ref[...]).astype(o_ref.dtype)

    return pl.pallas_call(
        body, name=name, grid=(s // tr,),
        in_specs=[pl.BlockSpec((tr, qd), lambda i: (i, 0)), pl.BlockSpec((tr, cd), lambda i: (i, 0)),
                  pl.BlockSpec((1, qd), lambda i: (0, 0)), pl.BlockSpec((1, cd), lambda i: (0, 0))],
        out_specs=pl.BlockSpec((tr, qd + cd), lambda i: (i, 0)),
        out_shape=jax.ShapeDtypeStruct((s, qd + cd), BF16), compiler_params=_params("parallel"),
    )(a, c, ga, gc)


def _cat_norm_bwd(name, dcat, a, c, ga, gc):
    s, qd = a.shape
    cd = c.shape[1]
    tr = _row_tile(s, qd + cd)

    def one(dn, yv, gv):
        r = lax.rsqrt(jnp.mean(yv * yv, axis=-1, keepdims=True) + NORM_EPS)
        xhat = yv * r
        dxn = dn * gv
        return r * (dxn - xhat * jnp.mean(dxn * xhat, axis=-1, keepdims=True)), _sum_to_sublanes(dn * xhat)

    def body(d_ref, a_ref, c_ref, ga_ref, gc_ref, da_ref, dc_ref, dga_ref, dgc_ref):
        da, pa = one(d_ref[:, :qd], a_ref[...], ga_ref[...])
        dc, pc = one(d_ref[:, qd:], c_ref[...], gc_ref[...])
        da_ref[...] = da
        dc_ref[...] = dc

        @pl.when(pl.program_id(0) == 0)
        def _():
            dga_ref[...] = pa
            dgc_ref[...] = pc

        @pl.when(pl.program_id(0) > 0)
        def _():
            dga_ref[...] += pa
            dgc_ref[...] += pc

    ra = pl.BlockSpec((tr, qd), lambda i: (i, 0))
    rc = pl.BlockSpec((tr, cd), lambda i: (i, 0))
    return pl.pallas_call(
        body, name=name, grid=(s // tr,),
        in_specs=[pl.BlockSpec((tr, qd + cd), lambda i: (i, 0)), ra, rc,
                  pl.BlockSpec((1, qd), lambda i: (0, 0)), pl.BlockSpec((1, cd), lambda i: (0, 0))],
        out_specs=[ra, rc, pl.BlockSpec((SUBLANES, qd), lambda i: (0, 0)),
                   pl.BlockSpec((SUBLANES, cd), lambda i: (0, 0))],
        out_shape=[jax.ShapeDtypeStruct((s, qd), F32), jax.ShapeDtypeStruct((s, cd), F32),
                   jax.ShapeDtypeStruct((SUBLANES, qd), F32), jax.ShapeDtypeStruct((SUBLANES, cd), F32)],
        compiler_params=_params("arbitrary"),
    )(dcat, a, c, ga, gc)


def _adamw(name, w, g, m, v):
    shape = w.shape
    cols = shape[-1]
    rows = w.size // cols
    tr = _row_tile(rows, cols, budget=3 << 19)
    bc1 = 1.0 - ADAM_B1 ** ADAM_STEP
    bc2 = 1.0 - ADAM_B2 ** ADAM_STEP

    def body(w_ref, g_ref, m_ref, v_ref, d_ref, nm_ref, nv_ref):
        gv = g_ref[...]
        mv = ADAM_B1 * m_ref[...] + (1.0 - ADAM_B1) * gv
        vv = ADAM_B2 * v_ref[...] + (1.0 - ADAM_B2) * (gv * gv)
        nm_ref[...] = mv
        nv_ref[...] = vv
        d_ref[...] = -ADAM_LR * ((mv / bc1) / (jnp.sqrt(vv / bc2) + ADAM_EPS) + ADAM_WD * w_ref[...])

    row = pl.BlockSpec((tr, cols), lambda i: (i, 0))
    outs = pl.pallas_call(
        body, name=name, grid=(rows // tr,), in_specs=[row] * 4, out_specs=[row] * 3,
        out_shape=[jax.ShapeDtypeStruct((rows, cols), F32)] * 3, compiler_params=_params("parallel"),
    )(*(t.reshape(rows, cols) for t in (w, g, m, v)))
    return tuple(t.reshape(shape) for t in outs)


HBM_SPEC = pl.BlockSpec(memory_space=pltpu.HBM)


def _mesh_place():
    x, y, c = lax.axis_index("x"), lax.axis_index("y"), lax.axis_index("c")
    other_chips = [(1 - x, y), (x, 1 - y), (1 - x, 1 - y)]
    return x, y, c, other_chips


def _cast_into_slot(name, w, layer, chip):
    _, r, cols = w.shape
    tr = _row_tile(r, cols)

    def body(chip_ref, w_ref, o_ref):
        o_ref[...] = w_ref[...].astype(o_ref.dtype)

    return pl.pallas_call(
        body, name=name,
        grid_spec=pltpu.PrefetchScalarGridSpec(
            num_scalar_prefetch=1, grid=(r // tr,),
            in_specs=[pl.BlockSpec((None, tr, cols), lambda i, chip_ref: (layer, i, 0))],
            out_specs=pl.BlockSpec((None, tr, cols), lambda i, chip_ref: (chip_ref[0], i, 0))),
        out_shape=jax.ShapeDtypeStruct((N_CHIPS, r, cols), BF16), compiler_params=_params("parallel"),
    )(chip, w)


SEM_SPEC = pl.BlockSpec(memory_space=pltpu.SEMAPHORE)
ANY_SPEC = pl.BlockSpec(memory_space=pl.ANY)
SPLIT_COPY = pltpu.CompilerParams(has_side_effects=pltpu.SideEffectType.DATAFLOW_SIDE_EFFECTING)
N_OTHER = N_CHIPS - 1


def _in_hbm(arr):
    return pltpu.with_memory_space_constraint(arr, pltpu.HBM)


def _half_rows(ref, chip_idx, core):
    r2 = ref.shape[1] // 2
    return ref.at[chip_idx, pl.ds(core * r2, r2), :]


def _gather_start(fulls):
    na = len(fulls)

    def body(*refs):
        f_refs = refs[na:2 * na]
        send_sems, recv_sems = refs[2 * na:3 * na], refs[3 * na:4 * na]
        x, y, c, chips = _mesh_place()
        for a in range(na):
            mine = _half_rows(f_refs[a], 2 * x + y, c)
            for j, (cx, cy) in enumerate(chips):
                pltpu.make_async_remote_copy(
                    src_ref=mine, dst_ref=mine, send_sem=send_sems[a].at[j], recv_sem=recv_sems[a].at[j],
                    device_id=(cx, cy, c), device_id_type=MESH).start()

    outs = pl.pallas_call(
        body, name="gather_start", in_specs=[HBM_SPEC] * na,
        out_specs=[HBM_SPEC] * na + [SEM_SPEC] * (2 * na),
        out_shape=[pltpu.HBM(f.shape, f.dtype) for f in fulls] + [pltpu.SemaphoreType.DMA((N_OTHER,))] * (2 * na),
        input_output_aliases={a: a for a in range(na)}, compiler_params=SPLIT_COPY,
    )(*[_in_hbm(f) for f in fulls])
    return outs[:na], outs[na:2 * na], outs[2 * na:]


def _gather_pass_on(name, full, recv_sems, after):
    def body(f_in, recv_sems, after_ref, f_ref, d2d_send, d2d_recv):
        x, y, c, chips = _mesh_place()
        for j, (cx, cy) in enumerate(chips):
            blk = _half_rows(f_ref, 2 * cx + cy, c)
            pltpu.make_async_remote_copy(
                src_ref=blk, dst_ref=blk, send_sem=d2d_send.at[j], recv_sem=recv_sems.at[j],
                device_id=(cx, cy, c), device_id_type=MESH).wait_recv()
            pltpu.make_async_remote_copy(
                src_ref=blk, dst_ref=blk, send_sem=d2d_send.at[j], recv_sem=d2d_recv.at[j],
                device_id=(x, y, 1 - c), device_id_type=MESH).start()

    return pl.pallas_call(
        body, name=name, in_specs=[HBM_SPEC, SEM_SPEC, ANY_SPEC], out_specs=[HBM_SPEC, SEM_SPEC, SEM_SPEC],
        out_shape=[pltpu.HBM(full.shape, full.dtype)] + [pltpu.SemaphoreType.DMA((N_OTHER,))] * 2,
        input_output_aliases={0: 0}, compiler_params=SPLIT_COPY,
    )(full, recv_sems, after)


def _gather_arrive(name, full, ici_send, d2d_send, d2d_recv, after):
    def body(f_in, ici_send, d2d_send, d2d_recv, after_ref, f_ref):
        x, y, c, chips = _mesh_place()
        for j, (cx, cy) in enumerate(chips):
            mine = _half_rows(f_ref, 2 * x + y, c)
            passed = _half_rows(f_ref, 2 * cx + cy, c)
            theirs = _half_rows(f_ref, 2 * cx + cy, 1 - c)
            pltpu.make_async_remote_copy(
                src_ref=mine, dst_ref=mine, send_sem=ici_send.at[j], recv_sem=d2d_recv.at[j],
                device_id=(cx, cy, c), device_id_type=MESH).wait_send()
            pltpu.make_async_remote_copy(
                src_ref=passed, dst_ref=passed, send_sem=d2d_send.at[j], recv_sem=d2d_recv.at[j],
                device_id=(x, y, 1 - c), device_id_type=MESH).wait_send()
            pltpu.make_async_remote_copy(
                src_ref=theirs, dst_ref=theirs, send_sem=d2d_send.at[j], recv_sem=d2d_recv.at[j],
                device_id=(x, y, 1 - c), device_id_type=MESH).wait_recv()

    return pl.pallas_call(
        body, name=name, in_specs=[HBM_SPEC, SEM_SPEC, SEM_SPEC, SEM_SPEC, ANY_SPEC], out_specs=HBM_SPEC,
        out_shape=pltpu.HBM(full.shape, full.dtype), input_output_aliases={0: 0}, compiler_params=SPLIT_COPY,
    )(full, ici_send, d2d_send, d2d_recv, after)


def _gather_taps(conv_w):
    def body(cw_ref, cwf_ref, send_sems, recv_sems, local_sem):
        x, y, c, chips = _mesh_place()
        k_me = 2 * x + y
        local = pltpu.make_async_copy(cw_ref, cwf_ref.at[k_me], local_sem)
        local.start()
        copies = [pltpu.make_async_remote_copy(
            src_ref=cw_ref, dst_ref=cwf_ref.at[k_me], send_sem=send_sems.at[j], recv_sem=recv_sems.at[j],
            device_id=(cx, cy, c), device_id_type=MESH) for j, (cx, cy) in enumerate(chips)]
        for cp in copies:
            cp.start()
        for j, (cx, cy) in enumerate(chips):
            pltpu.make_async_remote_copy(
                src_ref=cw_ref, dst_ref=cwf_ref.at[2 * cx + cy], send_sem=send_sems.at[j], recv_sem=recv_sems.at[j],
                device_id=(cx, cy, c), device_id_type=MESH).wait_recv()
        for cp in copies:
            cp.wait_send()
        local.wait()

    return pl.pallas_call(
        body, name="gather_taps", in_specs=[HBM_SPEC], out_specs=HBM_SPEC,
        out_shape=jax.ShapeDtypeStruct((N_CHIPS,) + conv_w.shape, conv_w.dtype),
        scratch_shapes=[pltpu.SemaphoreType.DMA((N_OTHER,))] * 2 + [pltpu.SemaphoreType.DMA],
    )(conv_w)


def _swap_core_halves(gs):
    na = len(gs)

    def body(*refs):
        g_refs, o_refs = refs[:na], refs[na:2 * na]
        send_sems, recv_sems = refs[2 * na:]
        x, y, c, _ = _mesh_place()
        copies = []
        for a in range(na):
            r2 = g_refs[a].shape[1] // 2
            copies.append(pltpu.make_async_remote_copy(
                src_ref=g_refs[a].at[:, pl.ds((1 - c) * r2, r2), :], dst_ref=o_refs[a],
                send_sem=send_sems.at[a], recv_sem=recv_sems.at[a], device_id=(x, y, 1 - c), device_id_type=MESH))
        for cp in copies:
            cp.start()
        for cp in copies:
            cp.wait()

    return pl.pallas_call(
        body, name="swap_core_halves", in_specs=[HBM_SPEC] * na, out_specs=[HBM_SPEC] * na,
        out_shape=[jax.ShapeDtypeStruct((g.shape[0], g.shape[1] // 2, g.shape[2]), g.dtype) for g in gs],
        scratch_shapes=[pltpu.SemaphoreType.DMA((na,))] * 2,
    )(*gs)


def _add_core_halves(name, g, sib, core):
    nb, r, cols = g.shape
    r2 = r // 2
    tr = _row_tile(r2, cols, itemsize=2, budget=1 << 20)
    nrt = r2 // tr

    def body(core_ref, g_ref, s_ref, o_ref):
        o_ref[...] = (g_ref[...].astype(F32) + s_ref[...].astype(F32)).astype(o_ref.dtype)

    return pl.pallas_call(
        body, name=name,
        grid_spec=pltpu.PrefetchScalarGridSpec(
            num_scalar_prefetch=1, grid=(nb, nrt),
            in_specs=[pl.BlockSpec((None, tr, cols), lambda k, i, core_ref: (k, core_ref[0] * nrt + i, 0)),
                      pl.BlockSpec((None, tr, cols), lambda k, i, core_ref: (k, i, 0))],
            out_specs=pl.BlockSpec((None, tr, cols), lambda k, i, core_ref: (k, i, 0))),
        out_shape=jax.ShapeDtypeStruct((nb, r2, cols), BF16), compiler_params=_params("parallel", "parallel"),
    )(core, g, sib)


def _scatter_to_chips(hs):
    na = len(hs)

    def body(*refs):
        h_refs, o_refs = refs[:na], refs[na:2 * na]
        send_sems, recv_sems = refs[2 * na:]
        x, y, c, chips = _mesh_place()
        copies = []
        for a in range(na):
            for j, (cx, cy) in enumerate(chips):
                copies.append(pltpu.make_async_remote_copy(
                    src_ref=h_refs[a].at[2 * cx + cy], dst_ref=o_refs[a].at[j],
                    send_sem=send_sems.at[a * 3 + j], recv_sem=recv_sems.at[a * 3 + j],
                    device_id=(cx, cy, c), device_id_type=MESH))
        for cp in copies:
            cp.start()
        for cp in copies:
            cp.wait()

    return pl.pallas_call(
        body, name="scatter_to_chips", in_specs=[HBM_SPEC] * na, out_specs=[HBM_SPEC] * na,
        out_shape=[jax.ShapeDtypeStruct((N_CHIPS - 1,) + h.shape[1:], h.dtype) for h in hs],
        scratch_shapes=[pltpu.SemaphoreType.DMA((na * 3,))] * 2,
    )(*hs)


def _sum_chips(name, hs, rcv, core, chip, layer, n_layers, prev):
    _, r2, cols = hs.shape
    tr = _row_tile(r2, cols, budget=1 << 20)
    nrt = r2 // tr

    def body(core_ref, chip_ref, h_ref, r_ref, *rest):
        o_ref = rest[-1]
        acc = h_ref[...].astype(F32)
        for j in range(N_CHIPS - 1):
            acc = acc + r_ref[j].astype(F32)
        o_ref[...] = acc

    in_specs = [pl.BlockSpec((None, tr, cols), lambda i, core_ref, chip_ref: (chip_ref[0], i, 0)),
                pl.BlockSpec((N_CHIPS - 1, tr, cols), lambda i, core_ref, chip_ref: (0, i, 0))]
    args = [core, chip, hs, rcv]
    aliases = {}
    if prev is not None:
        in_specs.append(pl.BlockSpec(memory_space=pl.ANY))
        args.append(prev)
        aliases = {4: 0}
    return pl.pallas_call(
        body, name=name,
        grid_spec=pltpu.PrefetchScalarGridSpec(
            num_scalar_prefetch=2, grid=(nrt,), in_specs=in_specs,
            out_specs=pl.BlockSpec((None, tr, cols), lambda i, core_ref, chip_ref: (layer, core_ref[0] * nrt + i, 0))),
        out_shape=jax.ShapeDtypeStruct((n_layers, 2 * r2, cols), F32), input_output_aliases=aliases,
        compiler_params=_params("parallel"),
    )(*args)


def _join_core_halves(ts):
    na = len(ts)

    def body(*refs):
        o_refs = refs[na:2 * na]
        send_sems, recv_sems = refs[2 * na:]
        x, y, c, _ = _mesh_place()
        copies = []
        for a in range(na):
            r2 = o_refs[a].shape[1] // 2
            mine = o_refs[a].at[:, pl.ds(c * r2, r2), :]
            copies.append(pltpu.make_async_remote_copy(
                src_ref=mine, dst_ref=mine, send_sem=send_sems.at[a], recv_sem=recv_sems.at[a],
                device_id=(x, y, 1 - c), device_id_type=MESH))
        for cp in copies:
            cp.start()
        for cp in copies:
            cp.wait()

    return pl.pallas_call(
        body, name="join_core_halves", in_specs=[HBM_SPEC] * na, out_specs=[HBM_SPEC] * na,
        out_shape=[jax.ShapeDtypeStruct(t.shape, t.dtype) for t in ts],
        input_output_aliases={a: a for a in range(na)},
        scratch_shapes=[pltpu.SemaphoreType.DMA((na,))] * 2,
    )(*ts)


def _allreduce_small(p):
    n, _, w = p.shape

    def body(p_ref, o_ref, buf, send_sems, recv_sems):
        x, y, c, _ = _mesh_place()
        me = 4 * x + 2 * y + c
        buf[me] = jnp.sum(p_ref[...], axis=1)
        copies = []
        for pat in range(1, N_DEV):
            fx, fy, fc = (pat >> 2) & 1, (pat >> 1) & 1, pat & 1
            copies.append(pltpu.make_async_remote_copy(
                src_ref=buf.at[me], dst_ref=buf.at[me], send_sem=send_sems.at[pat - 1], recv_sem=recv_sems.at[pat - 1],
                device_id=(x ^ fx, y ^ fy, c ^ fc), device_id_type=MESH))
        for cp in copies:
            cp.start()
        for cp in copies:
            cp.wait()
        acc = buf[0]
        for dev in range(1, N_DEV):
            acc = acc + buf[dev]
        o_ref[...] = acc

    return pl.pallas_call(
        body, name="allreduce_small", in_specs=[pl.BlockSpec(memory_space=pltpu.VMEM)],
        out_specs=pl.BlockSpec(memory_space=pltpu.VMEM), out_shape=jax.ShapeDtypeStruct((n, w), F32),
        scratch_shapes=[pltpu.VMEM((N_DEV, n, w), F32), pltpu.SemaphoreType.DMA((N_DEV - 1,)),
                        pltpu.SemaphoreType.DMA((N_DEV - 1,))],
    )(p)


class _WeightFeed:
    def __init__(self, fulls):
        self.fulls, self.ici_send, self.ici_recv = (list(t) for t in _gather_start(fulls))
        self.d2d = [None] * len(fulls)

    def _pass_on(self, k, after):
        if k < len(self.fulls) and self.d2d[k] is None:
            self.fulls[k], send, recv = _gather_pass_on(f"gather_pass_{k}", self.fulls[k], self.ici_recv[k], after)
            self.d2d[k] = (send, recv)

    def take(self, k, after):
        self._pass_on(k, after)
        self._pass_on(k + 1, after)
        if k + 1 < len(self.fulls):
            after = self.fulls[k + 1]
        self.fulls[k] = _gather_arrive(f"gather_arrive_{k}", self.fulls[k], self.ici_send[k], *self.d2d[k], after)
        return self.fulls[k]


def _ffn_forward(tag, x, g_pre, g_post, feed, k):
    s, d = x.shape
    h = _norm_fwd(f"{tag}_norm", x, g_pre)
    gu_w = feed.take(k, h)
    gu, a = _ffn_up(f"{tag}_up", h, gu_w)
    dn_w = feed.take(k + 1, a).reshape(-1, d)
    f = dn_w.shape[0]
    tk = _tile(f, 1408)
    tm, tn = _tile(s, 1024), _tile(d, 1024)
    y = _mm(f"{tag}_down", a, dn_w, mode="nn", grid=(s // tm, d // tn, f // tk),
            a_spec=pl.BlockSpec((tm, tk), lambda i, j, k: (i, k)),
            b_spec=pl.BlockSpec((tk, tn), lambda i, j, k: (k, j)),
            o_spec=pl.BlockSpec((tm, tn), lambda i, j, k: (i, j)),
            out_shape=jax.ShapeDtypeStruct((s, d), F32), nk=f // tk, acc_shape=(tm, tn))
    x_new = _res_norm(f"{tag}_post", x, y, g_post, FFN_RESIDUAL_WEIGHT)
    return x_new, (x, h, gu, a, y)


def _ffn_backward(tag, dx_new, saved, g_pre, g_post, gu_w, dn_w):
    x, h, gu, a, y = saved
    s, d = x.shape
    nb, fs = gu_w.shape[0], gu_w.shape[2]
    f = dn_w.shape[0]
    fr = f // nb
    dy, dg_post = _norm_bwd(f"{tag}_post_bwd", dx_new, y, g_post, FFN_RESIDUAL_WEIGHT, None, BF16)
    dgu = _ffn_dact(f"{tag}_dact", dy, dn_w, gu)
    dgu4 = dgu.reshape(nb, s, fs)
    tn = _tile(d, 1024)
    d_wd = _mm(f"{tag}_dwd", a, dy, mode="tn", grid=(nb, d // tn),
               a_spec=pl.BlockSpec((s, fr), lambda i, j: (0, i)),
               b_spec=pl.BlockSpec((s, tn), lambda i, j: (0, j)),
               o_spec=pl.BlockSpec((None, fr, tn), lambda i, j: (i, 0, j)),
               out_shape=jax.ShapeDtypeStruct((nb, fr, d), BF16))
    tm, tw = _tile(d, 512), _tile(fs, 1408)
    nw = fs // tw
    d_wgu = _mm(f"{tag}_dwgu", h, dgu4, mode="tn", grid=(nb, nw, d // tm),
                a_spec=pl.BlockSpec((s, tm), lambda k, j, i: (0, i)),
                b_spec=pl.BlockSpec((None, s, tw), lambda k, j, i: (k, 0, j)),
                o_spec=pl.BlockSpec((None, tm, tw), lambda k, j, i: (k, i, j)),
                out_shape=jax.ShapeDtypeStruct((nb, d, fs), BF16))
    ts, td = _tile(s, 1024), _tile(d, 1024)
    dh = _mm(f"{tag}_dh", dgu4, gu_w, mode="nt", grid=(s // ts, d // td, nb),
             a_spec=pl.BlockSpec((None, ts, fs), lambda i, j, k: (k, i, 0)),
             b_spec=pl.BlockSpec((None, td, fs), lambda i, j, k: (k, j, 0)),
             o_spec=pl.BlockSpec((ts, td), lambda i, j, k: (i, j)),
             out_shape=jax.ShapeDtypeStruct((s, d), F32), nk=nb, acc_shape=(ts, td))
    dx, dg_pre = _norm_bwd(f"{tag}_pre_bwd", dh, x, g_pre, 1.0, dx_new, F32)
    return dx, d_wgu, d_wd, dg_pre, dg_post


def _mixer_forward(tag, x, gains, feed, k, conv_taps, dims):
    qd, kvd, cd = dims
    s, d = x.shape
    g_pre, g_a, g_c, g_post = gains
    h = _norm_fwd(f"{tag}_norm", x, g_pre)
    win_w = feed.take(k, h)
    nb, cw = win_w.shape[0], win_w.shape[2]
    tm = _tile(s, 1024)
    z = _mm(f"{tag}_in", h, win_w, mode="nn", grid=(nb, s // tm),
            a_spec=pl.BlockSpec((tm, d), lambda j, i: (i, 0)),
            b_spec=pl.BlockSpec((None, d, cw), lambda j, i: (j, 0, 0)),
            o_spec=pl.BlockSpec((tm, cw), lambda j, i: (i, j)),
            out_shape=jax.ShapeDtypeStruct((s, nb * cw), BF16))
    a, lse = _attn_fwd(f"{tag}_attn", z, qd, kvd)
    c = _conv_fwd(f"{tag}_conv", z, conv_taps, qd + 2 * kvd, cd)
    cat = _cat_norm_fwd(f"{tag}_cat", a, c, g_a, g_c)
    wout_w = feed.take(k + 1, cat).reshape(-1, d)
    mw = qd + cd
    tn = _tile(d, 1024)
    mixed = _mm(f"{tag}_out", cat, wout_w, mode="nn", grid=(s // tm, d // tn),
                a_spec=pl.BlockSpec((tm, mw), lambda i, j: (i, 0)),
                b_spec=pl.BlockSpec((mw, tn), lambda i, j: (0, j)),
                o_spec=pl.BlockSpec((tm, tn), lambda i, j: (i, j)),
                out_shape=jax.ShapeDtypeStruct((s, d), F32))
    x_new = _res_norm(f"{tag}_post", x, mixed, g_post, 1.0)
    return x_new, (x, h, z, a, lse, c, cat, mixed)


def _mixer_backward(tag, dx_new, saved, gains, win_w, conv_taps, wout_w, dims):
    qd, kvd, cd = dims
    x, h, z, a, lse, c, cat, mixed = saved
    s, d = x.shape
    nb, cw = win_w.shape[0], win_w.shape[2]
    g_pre, g_a, g_c, g_post = gains
    mw = qd + cd
    dmixed, dg_post = _norm_bwd(f"{tag}_post_bwd", dx_new, mixed, g_post, 1.0, None, BF16)
    tm, tn = _tile(s, 1024), _tile(mw, 1024)
    dcat = _mm(f"{tag}_dcat", dmixed, wout_w, mode="nt", grid=(s // tm, mw // tn),
               a_spec=pl.BlockSpec((tm, d), lambda i, j: (i, 0)),
               b_spec=pl.BlockSpec((tn, d), lambda i, j: (j, 0)),
               o_spec=pl.BlockSpec((tm, tn), lambda i, j: (i, j)),
               out_shape=jax.ShapeDtypeStruct((s, mw), F32))
    wr = mw // nb
    td = _tile(d, 1024)
    d_wout = _mm(f"{tag}_dwout", cat, dmixed, mode="tn", grid=(nb, d // td),
                 a_spec=pl.BlockSpec((s, wr), lambda i, j: (0, i)),
                 b_spec=pl.BlockSpec((s, td), lambda i, j: (0, j)),
                 o_spec=pl.BlockSpec((None, wr, td), lambda i, j: (i, 0, j)),
                 out_shape=jax.ShapeDtypeStruct((nb, wr, d), BF16))
    da, dc, dg_a, dg_c = _cat_norm_bwd(f"{tag}_cat_bwd", dcat, a, c, g_a, g_c)
    dhc, dbg, dcg, d_taps = _conv_bwd(f"{tag}_conv_bwd", z, conv_taps, dc, qd + 2 * kvd, cd)
    dq, dk, dv = _attn_bwd(f"{tag}_attn_bwd", z, a, lse, da, qd, kvd)
    dz = jnp.concatenate([dq, dk, dv, dhc, dbg, dcg], axis=1)
    th = _tile(d, 512)
    d_win = _mm(f"{tag}_dwin", h, dz, mode="tn", grid=(nb, d // th),
                a_spec=pl.BlockSpec((s, th), lambda k, i: (0, i)),
                b_spec=pl.BlockSpec((s, cw), lambda k, i: (0, k)),
                o_spec=pl.BlockSpec((None, th, cw), lambda k, i: (k, i, 0)),
                out_shape=jax.ShapeDtypeStruct((nb, d, cw), BF16))
    dh = _mm(f"{tag}_dh", dz, win_w, mode="nt", grid=(s // tm, d // td, nb),
             a_spec=pl.BlockSpec((tm, cw), lambda i, j, k: (i, k)),
             b_spec=pl.BlockSpec((None, td, cw), lambda i, j, k: (k, j, 0)),
             o_spec=pl.BlockSpec((tm, td), lambda i, j, k: (i, j)),
             out_shape=jax.ShapeDtypeStruct((s, d), F32), nk=nb, acc_shape=(tm, td))
    dx, dg_pre = _norm_bwd(f"{tag}_pre_bwd", dh, x, g_pre, 1.0, dx_new, F32)
    return dx, d_win, d_wout, d_taps, (dg_pre, dg_a, dg_c, dg_post)


def _pad_cols(v, width):
    return jnp.pad(v, ((0, 0), (0, width - v.shape[1])))


def kernel(x, ffn1_norm_pre, ffn1_w_gate_up, ffn1_w_down, ffn1_norm_post, mix_norm_pre, w_in, conv_w, attn_out_norm, conv_out_norm, w_out, mix_norm_post, ffn2_norm_pre, ffn2_w_gate_up, ffn2_w_down, ffn2_norm_post, loss_target, m_ffn1_norm_pre, m_ffn1_w_gate_up, m_ffn1_w_down, m_ffn1_norm_post, m_mix_norm_pre, m_w_in, m_conv_w, m_attn_out_norm, m_conv_out_norm, m_w_out, m_mix_norm_post, m_ffn2_norm_pre, m_ffn2_w_gate_up, m_ffn2_w_down, m_ffn2_norm_post, v_ffn1_norm_pre, v_ffn1_w_gate_up, v_ffn1_w_down, v_ffn1_norm_post, v_mix_norm_pre, v_w_in, v_conv_w, v_attn_out_norm, v_conv_out_norm, v_w_out, v_mix_norm_post, v_ffn2_norm_pre, v_ffn2_w_gate_up, v_ffn2_w_down, v_ffn2_norm_post):
    _, s, d = x.shape
    n_layers = ffn1_norm_pre.shape[0]
    qd = attn_out_norm.shape[1]
    cd = conv_out_norm.shape[1]
    kvd = qd // Q_PER_KV
    dims = (qd, kvd, cd)
    assert N_CHIPS * w_in.shape[2] == qd + 2 * kvd + 3 * cd and qd + cd == N_CHIPS * w_out.shape[1]
    assert 2 * d <= SMALL_ROWS * LANES * SUBLANES
    chip = 2 * lax.axis_index("x") + lax.axis_index("y")
    chip_arr = chip.astype(jnp.int32).reshape(1)
    core = lax.axis_index("c").astype(jnp.int32).reshape(1)
    kinds = ("gu1", "dn1", "win", "wout", "gu2", "dn2")

    big = (ffn1_w_gate_up, ffn1_w_down, w_in, w_out, ffn2_w_gate_up, ffn2_w_down)
    nk = len(kinds)
    feed = _WeightFeed([_cast_into_slot(f"cast_{k}_{layer}", w, layer, chip_arr)
                        for layer in range(n_layers) for k, w in zip(kinds, big)])
    taps_all = _gather_taps(conv_w)
    taps = jnp.transpose(taps_all, (1, 2, 0, 3)).reshape(n_layers, CONV_WIDTH, cd)
    taps = jnp.pad(taps, ((0, 0), (0, SUBLANES - CONV_WIDTH), (0, 0)))

    def gain(g, layer):
        return g[layer][None, :]

    xs = x[0]
    saved = []
    for layer in range(n_layers):
        t = f"l{layer}"
        k0 = layer * nk
        xs, s1 = _ffn_forward(f"{t}_ffn1", xs, gain(ffn1_norm_pre, layer), gain(ffn1_norm_post, layer), feed, k0)
        mix_gains = (gain(mix_norm_pre, layer), gain(attn_out_norm, layer), gain(conv_out_norm, layer), gain(mix_norm_post, layer))
        xs, s2 = _mixer_forward(f"{t}_mix", xs, mix_gains, feed, k0 + 2, taps[layer], dims)
        xs, s3 = _ffn_forward(f"{t}_ffn2", xs, gain(ffn2_norm_pre, layer), gain(ffn2_norm_post, layer), feed, k0 + 4)
        saved.append((s1, s2, s3, mix_gains))
    wts = {k: [feed.fulls[layer * nk + i] for layer in range(n_layers)] for i, k in enumerate(kinds)}
    for k in ("dn1", "wout", "dn2"):
        wts[k] = [w.reshape(-1, d) for w in wts[k]]
    dxs, loss_part = _loss_head("loss_head", xs, loss_target[0])
    loss = lax.psum(jnp.sum(loss_part), ("x", "y", "c"))

    grads = {k: [None] * n_layers for k in kinds}
    small = [None] * n_layers
    for layer in reversed(range(n_layers)):
        t = f"l{layer}"
        s1, s2, s3, mix_gains = saved[layer]
        dxs, grads["gu2"][layer], grads["dn2"][layer], p_pre2, p_post2 = _ffn_backward(
            f"{t}_ffn2", dxs, s3, gain(ffn2_norm_pre, layer), gain(ffn2_norm_post, layer),
            wts["gu2"][layer], wts["dn2"][layer])
        dxs, grads["win"][layer], grads["wout"][layer], p_taps, (p_mpre, p_a, p_c, p_mpost) = _mixer_backward(
            f"{t}_mix", dxs, s2, mix_gains, wts["win"][layer], taps[layer], wts["wout"][layer], dims)
        dxs, grads["gu1"][layer], grads["dn1"][layer], p_pre1, p_post1 = _ffn_backward(
            f"{t}_ffn1", dxs, s1, gain(ffn1_norm_pre, layer), gain(ffn1_norm_post, layer),
            wts["gu1"][layer], wts["dn1"][layer])
        tap_rows = jnp.zeros((CONV_WIDTH, SUBLANES, d), F32).at[:, 0, :cd].set(p_taps[:CONV_WIDTH])
        rows = [p_pre1, p_post1, p_mpre, jnp.concatenate([p_a, p_c], axis=1), p_mpost, p_pre2, p_post2]
        rows = jnp.concatenate([jnp.stack(rows), tap_rows], axis=0)
        small[layer] = jnp.pad(rows, ((0, SMALL_ROWS - rows.shape[0]), (0, 0), (0, 0)))
    grad_x = dxs[None]

    parts = [grads[k][layer] for k in kinds for layer in range(n_layers)]
    from_sibling = _swap_core_halves(parts)
    core_sums = [_add_core_halves(f"add_cores_{i}", g, sb, core) for i, (g, sb) in enumerate(zip(parts, from_sibling))]
    from_chips = _scatter_to_chips(core_sums)
    reduced = []
    for i in range(len(kinds)):
        buf = None
        for layer in range(n_layers):
            a = i * n_layers + layer
            buf = _sum_chips(f"sum_chips_{a}", core_sums[a], from_chips[a], core, chip_arr, layer, n_layers, buf)
        reduced.append(buf)
    g_gu1, g_dn1, g_win, g_wout, g_gu2, g_dn2 = _join_core_halves(reduced)

    small_sum = _allreduce_small(jnp.concatenate(small, axis=0)).reshape(n_layers, SMALL_ROWS, d)
    g_ffn1_pre, g_ffn1_post, g_mix_pre = small_sum[:, 0], small_sum[:, 1], small_sum[:, 2]
    g_attn_out, g_conv_out = small_sum[:, 3, :qd], small_sum[:, 3, qd:qd + cd]
    g_mix_post, g_ffn2_pre, g_ffn2_post = small_sum[:, 4], small_sum[:, 5], small_sum[:, 6]
    cc = conv_w.shape[2]
    g_conv = lax.dynamic_slice_in_dim(small_sum[:, 7:7 + CONV_WIDTH, :cd], chip * cc, cc, axis=2)

    weights = dict(ffn1_norm_pre=ffn1_norm_pre, ffn1_w_gate_up=ffn1_w_gate_up, ffn1_w_down=ffn1_w_down, ffn1_norm_post=ffn1_norm_post, mix_norm_pre=mix_norm_pre, w_in=w_in, conv_w=conv_w, attn_out_norm=attn_out_norm, conv_out_norm=conv_out_norm, w_out=w_out, mix_norm_post=mix_norm_post, ffn2_norm_pre=ffn2_norm_pre, ffn2_w_gate_up=ffn2_w_gate_up, ffn2_w_down=ffn2_w_down, ffn2_norm_post=ffn2_norm_post)
    m_in = dict(ffn1_norm_pre=m_ffn1_norm_pre, ffn1_w_gate_up=m_ffn1_w_gate_up, ffn1_w_down=m_ffn1_w_down, ffn1_norm_post=m_ffn1_norm_post, mix_norm_pre=m_mix_norm_pre, w_in=m_w_in, conv_w=m_conv_w, attn_out_norm=m_attn_out_norm, conv_out_norm=m_conv_out_norm, w_out=m_w_out, mix_norm_post=m_mix_norm_post, ffn2_norm_pre=m_ffn2_norm_pre, ffn2_w_gate_up=m_ffn2_w_gate_up, ffn2_w_down=m_ffn2_w_down, ffn2_norm_post=m_ffn2_norm_post)
    v_in = dict(ffn1_norm_pre=v_ffn1_norm_pre, ffn1_w_gate_up=v_ffn1_w_gate_up, ffn1_w_down=v_ffn1_w_down, ffn1_norm_post=v_ffn1_norm_post, mix_norm_pre=v_mix_norm_pre, w_in=v_w_in, conv_w=v_conv_w, attn_out_norm=v_attn_out_norm, conv_out_norm=v_conv_out_norm, w_out=v_w_out, mix_norm_post=v_mix_norm_post, ffn2_norm_pre=v_ffn2_norm_pre, ffn2_w_gate_up=v_ffn2_w_gate_up, ffn2_w_down=v_ffn2_w_down, ffn2_norm_post=v_ffn2_norm_post)
    grad = dict(ffn1_norm_pre=g_ffn1_pre, ffn1_w_gate_up=g_gu1, ffn1_w_down=g_dn1, ffn1_norm_post=g_ffn1_post, mix_norm_pre=g_mix_pre, w_in=g_win, conv_w=g_conv, attn_out_norm=g_attn_out, conv_out_norm=g_conv_out, w_out=g_wout, mix_norm_post=g_mix_post, ffn2_norm_pre=g_ffn2_pre, ffn2_w_gate_up=g_gu2, ffn2_w_down=g_dn2, ffn2_norm_post=g_ffn2_post)
    names = list(weights)

    delta, new_m, new_v = {}, {}, {}
    matrices = ("ffn1_w_gate_up", "ffn1_w_down", "w_in", "w_out", "ffn2_w_gate_up", "ffn2_w_down")
    for n in matrices:
        delta[n], new_m[n], new_v[n] = _adamw(f"adamw_{n}", weights[n], grad[n], m_in[n], v_in[n])
    vectors = [n for n in names if n not in matrices]

    def pack(tree):
        flat = jnp.concatenate([tree[n].reshape(-1) for n in vectors])
        return jnp.pad(flat, (0, -flat.size % (SUBLANES * LANES))).reshape(-1, LANES)

    packed = _adamw("adamw_small", pack(weights), pack(grad), pack(m_in), pack(v_in))
    offset = 0
    for n in vectors:
        size = weights[n].size
        for tree, flat in zip((delta, new_m, new_v), packed):
            tree[n] = flat.reshape(-1)[offset:offset + size].reshape(weights[n].shape)
        offset += size

    return (loss, grad_x, *[grad[n] for n in names], *[delta[n] for n in names],
            *[new_m[n] for n in names], *[new_v[n] for n in names])
```

```python
import functools

import jax
import jax.numpy as jnp
from jax import lax
from jax.experimental import pallas as pl
from jax.experimental.pallas import tpu as pltpu

F32 = jnp.float32
BF16 = jnp.bfloat16
MESH = pl.DeviceIdType.MESH

NORM_EPS = 1e-6
HEAD_DIM = 128
Q_PER_KV = 4
CONV_WIDTH = 3
FFN_RESIDUAL_WEIGHT = 0.5
DILATED_BRANCHES = ((128, 1), (512, 4), (2048, 16))
ADAM_LR = 0.001
ADAM_B1 = 0.9
ADAM_B2 = 0.999
ADAM_EPS = 1e-08
ADAM_WD = 0.01
ADAM_STEP = 10

N_CHIPS = 4
N_DEV = 8
V7X_VMEM_BYTES = 64 << 20
VMEM_LIMIT = V7X_VMEM_BYTES - (12 << 20)
SUBLANES = 8
LANES = 128
SMALL_ROWS = 16


def _params(*sem):
    return pltpu.CompilerParams(dimension_semantics=sem, vmem_limit_bytes=VMEM_LIMIT)


def _row_tile(rows, cols, itemsize=4, budget=2 << 20):
    t = rows
    while t * cols * itemsize > budget and t % 32 == 0:
        t //= 2
    return t


def _sum_to_sublanes(v):
    r, n = v.shape
    return v.reshape(r // SUBLANES, SUBLANES, n).sum(axis=0)


_DIMS = {
    "nn": (((1,), (0,)), ((), ())),
    "nt": (((1,), (1,)), ((), ())),
    "tn": (((0,), (0,)), ((), ())),
}


def _dot(a, b, mode):
    return lax.dot_general(a, b, _DIMS[mode], preferred_element_type=F32)


def _mm(name, a, b, *, mode, grid, a_spec, b_spec, o_spec, out_shape, nk=1, acc_shape=None):
    def body(a_ref, b_ref, o_ref, *scratch):
        r = _dot(a_ref[...], b_ref[...], mode)
        if nk == 1:
            o_ref[...] = r.astype(o_ref.dtype)
        else:
            acc = scratch[0]
            k = pl.program_id(len(grid) - 1)

            @pl.when(k == 0)
            def _():
                acc[...] = r

            @pl.when(k > 0)
            def _():
                acc[...] += r

            @pl.when(k == nk - 1)
            def _():
                o_ref[...] = acc[...].astype(o_ref.dtype)

    sem = ("parallel",) * (len(grid) - (1 if nk > 1 else 0)) + (("arbitrary",) if nk > 1 else ())
    return pl.pallas_call(
        body, name=name, grid=grid, in_specs=[a_spec, b_spec], out_specs=o_spec, out_shape=out_shape,
        scratch_shapes=[pltpu.VMEM(acc_shape, F32)] if nk > 1 else [],
        compiler_params=_params(*sem),
    )(a, b)


def _tile(n, want):
    if n <= want:
        return n
    best = None
    for t in range(LANES, want + 1, LANES):
        if n % t == 0:
            best = t
    assert best is not None, (n, want)
    return best


def _norm_fwd(name, x, gain):
    s, d = x.shape
    tr = _row_tile(s, d)

    def body(x_ref, g_ref, o_ref):
        xv = x_ref[...]
        r = lax.rsqrt(jnp.mean(xv * xv, axis=-1, keepdims=True) + NORM_EPS)
        o_ref[...] = (xv * r * g_ref[...]).astype(o_ref.dtype)

    return pl.pallas_call(
        body, name=name, grid=(s // tr,),
        in_specs=[pl.BlockSpec((tr, d), lambda i: (i, 0)), pl.BlockSpec((1, d), lambda i: (0, 0))],
        out_specs=pl.BlockSpec((tr, d), lambda i: (i, 0)),
        out_shape=jax.ShapeDtypeStruct((s, d), BF16), compiler_params=_params("parallel"),
    )(x, gain)


def _res_norm(name, x, y, gain, scale):
    s, d = x.shape
    tr = _row_tile(s, d)

    def body(x_ref, y_ref, g_ref, o_ref):
        yv = y_ref[...]
        r = lax.rsqrt(jnp.mean(yv * yv, axis=-1, keepdims=True) + NORM_EPS)
        o_ref[...] = x_ref[...] + scale * (yv * r * g_ref[...])

    row = pl.BlockSpec((tr, d), lambda i: (i, 0))
    return pl.pallas_call(
        body, name=name, grid=(s // tr,),
        in_specs=[row, row, pl.BlockSpec((1, d), lambda i: (0, 0))], out_specs=row,
        out_shape=jax.ShapeDtypeStruct((s, d), F32), compiler_params=_params("parallel"),
    )(x, y, gain)


def _norm_bwd(name, dout, yin, gain, scale, resid, out_dtype):
    s, d = yin.shape
    tr = _row_tile(s, d)
    has_resid = resid is not None

    def body(*refs):
        if has_resid:
            do_ref, y_ref, g_ref, r_ref, di_ref, dg_ref = refs
        else:
            do_ref, y_ref, g_ref, di_ref, dg_ref = refs
        yv = y_ref[...]
        r = lax.rsqrt(jnp.mean(yv * yv, axis=-1, keepdims=True) + NORM_EPS)
        xhat = yv * r
        dn = scale * do_ref[...]
        part = _sum_to_sublanes(dn * xhat)

        @pl.when(pl.program_id(0) == 0)
        def _():
            dg_ref[...] = part

        @pl.when(pl.program_id(0) > 0)
        def _():
            dg_ref[...] += part

        dxn = dn * g_ref[...]
        din = r * (dxn - xhat * jnp.mean(dxn * xhat, axis=-1, keepdims=True))
        if has_resid:
            din = din + r_ref[...]
        di_ref[...] = din.astype(di_ref.dtype)

    row = pl.BlockSpec((tr, d), lambda i: (i, 0))
    vec = pl.BlockSpec((1, d), lambda i: (0, 0))
    ins = [row, row, vec] + ([row] if has_resid else [])
    args = (dout, yin, gain) + ((resid,) if has_resid else ())
    return pl.pallas_call(
        body, name=name, grid=(s // tr,), in_specs=ins,
        out_specs=[row, pl.BlockSpec((SUBLANES, d), lambda i: (0, 0))],
        out_shape=[jax.ShapeDtypeStruct((s, d), out_dtype), jax.ShapeDtypeStruct((SUBLANES, d), F32)],
        compiler_params=_params("arbitrary"),
    )(*args)


def _loss_head(name, y, target):
    s, d = y.shape
    tr = _row_tile(s, d)

    def body(y_ref, t_ref, dy_ref, l_ref):
        e = y_ref[...] - t_ref[...]
        dy_ref[...] = e * (1.0 / d)
        part = _sum_to_sublanes(e * e) * (0.5 / d)

        @pl.when(pl.program_id(0) == 0)
        def _():
            l_ref[...] = part

        @pl.when(pl.program_id(0) > 0)
        def _():
            l_ref[...] += part

    row = pl.BlockSpec((tr, d), lambda i: (i, 0))
    return pl.pallas_call(
        body, name=name, grid=(s // tr,), in_specs=[row, row],
        out_specs=[row, pl.BlockSpec((SUBLANES, d), lambda i: (0, 0))],
        out_shape=[jax.ShapeDtypeStruct((s, d), F32), jax.ShapeDtypeStruct((SUBLANES, d), F32)],
        compiler_params=_params("arbitrary"),
    )(y, target)


def _ffn_up(name, h, gu_w):
    s, d = h.shape
    nb, _, fs = gu_w.shape
    hb = nb // 2
    w = gu_w.reshape(2, hb, d, fs)
    tm = _tile(s, 512)
    tn = _tile(fs, 1408)
    nj = fs // tn

    def body(h_ref, w_ref, gu_ref, a_ref):
        hv = h_ref[...]
        g = _dot(hv, w_ref[0], "nn")
        u = _dot(hv, w_ref[1], "nn")
        gu_ref[0] = g.astype(gu_ref.dtype)
        gu_ref[1] = u.astype(gu_ref.dtype)
        a_ref[...] = (g * jax.nn.sigmoid(g) * u).astype(a_ref.dtype)

    return pl.pallas_call(
        body, name=name, grid=(hb, nj, s // tm),
        in_specs=[pl.BlockSpec((tm, d), lambda jb, jo, i: (i, 0)),
                  pl.BlockSpec((2, None, d, tn), lambda jb, jo, i: (0, jb, 0, jo))],
        out_specs=[pl.BlockSpec((2, None, tm, tn), lambda jb, jo, i: (0, jb, i, jo)),
                   pl.BlockSpec((tm, tn), lambda jb, jo, i: (i, jb * nj + jo))],
        out_shape=[jax.ShapeDtypeStruct((2, hb, s, fs), BF16), jax.ShapeDtypeStruct((s, hb * fs), BF16)],
        compiler_params=_params("parallel", "parallel", "parallel"),
    )(h, w)


def _ffn_dact(name, dy, dn_w, gu):
    s, d = dy.shape
    _, hb, _, fs = gu.shape
    tm = _tile(s, 512)
    tn = _tile(fs, 1408)
    nj = fs // tn

    def body(dy_ref, w_ref, gu_ref, o_ref):
        da = _dot(dy_ref[...], w_ref[...], "nt")
        g = gu_ref[0].astype(F32)
        u = gu_ref[1].astype(F32)
        sg = jax.nn.sigmoid(g)
        o_ref[0] = (da * u * (sg * (1.0 + g * (1.0 - sg)))).astype(o_ref.dtype)
        o_ref[1] = (da * (g * sg)).astype(o_ref.dtype)

    blk = pl.BlockSpec((2, None, tm, tn), lambda jb, jo, i: (0, jb, i, jo))
    return pl.pallas_call(
        body, name=name, grid=(hb, nj, s // tm),
        in_specs=[pl.BlockSpec((tm, d), lambda jb, jo, i: (i, 0)),
                  pl.BlockSpec((tn, d), lambda jb, jo, i: (jb * nj + jo, 0)),
                  blk],
        out_specs=blk, out_shape=jax.ShapeDtypeStruct(gu.shape, BF16),
        compiler_params=_params("parallel", "parallel", "parallel"),
    )(dy, dn_w, gu)


def _multiplicity(q0, tq, s):
    row = q0 + lax.broadcasted_iota(jnp.int32, (tq, s), 0)
    col = lax.broadcasted_iota(jnp.int32, (tq, s), 1)
    dist = row - col
    mult = jnp.zeros((tq, s), F32)
    for window, dilation in DILATED_BRANCHES:
        hit = (dist <= window) & ((dist & (dilation - 1)) == 0)
        mult = mult + hit.astype(F32)
    return jnp.where(dist >= 0, mult, 0.0)


_MASKED = -1e30


def _attn_specs(s, qd, kvd, tq):
    rw = Q_PER_KV * HEAD_DIM
    qspec = pl.BlockSpec((tq, rw), lambda g, i: (i, g))
    kspec = pl.BlockSpec((s, HEAD_DIM), lambda g, i: (0, qd // HEAD_DIM + g))
    vspec = pl.BlockSpec((s, HEAD_DIM), lambda g, i: (0, (qd + kvd) // HEAD_DIM + g))
    return rw, qspec, kspec, vspec


def _attn_fwd(name, z, qd, kvd):
    s = z.shape[0]
    tq = _tile(s, 256)
    nkv = kvd // HEAD_DIM
    rw, qspec, kspec, vspec = _attn_specs(s, qd, kvd, tq)
    scale = HEAD_DIM ** -0.5

    def body(q_ref, k_ref, v_ref, o_ref, l_ref):
        mult = _multiplicity(pl.program_id(1) * tq, tq, s)
        live = mult > 0.0
        kv, vv = k_ref[...], v_ref[...]
        for h in range(Q_PER_KV):
            cols = slice(h * HEAD_DIM, (h + 1) * HEAD_DIM)
            sc = jnp.where(live, _dot(q_ref[:, cols], kv, "nt") * scale, _MASKED)
            mx = jnp.max(sc, axis=-1, keepdims=True)
            p = jnp.exp(sc - mx) * mult
            den = jnp.sum(p, axis=-1, keepdims=True)
            o_ref[:, cols] = _dot(p.astype(BF16), vv, "nn") / den
            l_ref[:, cols] = jnp.broadcast_to(mx + jnp.log(den), (tq, HEAD_DIM))

    return pl.pallas_call(
        body, name=name, grid=(nkv, s // tq), in_specs=[qspec, kspec, vspec], out_specs=[qspec, qspec],
        out_shape=[jax.ShapeDtypeStruct((s, qd), F32), jax.ShapeDtypeStruct((s, qd), F32)],
        compiler_params=_params("parallel", "parallel"),
    )(z, z, z)


def _attn_bwd(name, z, o, lse, do, qd, kvd):
    s = z.shape[0]
    tq = _tile(s, 256)
    nkv = kvd // HEAD_DIM
    nq = s // tq
    rw, qspec, kspec, vspec = _attn_specs(s, qd, kvd, tq)
    scale = HEAD_DIM ** -0.5

    def body(q_ref, k_ref, v_ref, o_ref, l_ref, do_ref, dq_ref, dk_ref, dv_ref, dk_acc, dv_acc):
        i = pl.program_id(1)
        mult = _multiplicity(i * tq, tq, s)
        live = mult > 0.0
        kv, vv = k_ref[...], v_ref[...]

        @pl.when(i == 0)
        def _():
            dk_acc[...] = jnp.zeros_like(dk_acc)
            dv_acc[...] = jnp.zeros_like(dv_acc)

        for h in range(Q_PER_KV):
            cols = slice(h * HEAD_DIM, (h + 1) * HEAD_DIM)
            q = q_ref[:, cols]
            dov = do_ref[:, cols]
            sc = jnp.where(live, _dot(q, kv, "nt") * scale, _MASKED)
            p = jnp.exp(sc - l_ref[:, cols][:, :1]) * mult
            dob = dov.astype(BF16)
            dp = _dot(dob, vv, "nt")
            delta = jnp.sum(dov * o_ref[:, cols], axis=-1, keepdims=True)
            ds = (p * (dp - delta) * scale).astype(BF16)
            dq_ref[:, cols] = _dot(ds, kv, "nn").astype(dq_ref.dtype)
            dk_acc[...] += _dot(ds, q, "tn")
            dv_acc[...] += _dot(p.astype(BF16), dob, "tn")

        @pl.when(i == nq - 1)
        def _():
            dk_ref[...] = dk_acc[...].astype(dk_ref.dtype)
            dv_ref[...] = dv_acc[...].astype(dv_ref.dtype)

    kvout = pl.BlockSpec((s, HEAD_DIM), lambda g, i: (0, g))
    return pl.pallas_call(
        body, name=name, grid=(nkv, nq), in_specs=[qspec, kspec, vspec, qspec, qspec, qspec],
        out_specs=[qspec, kvout, kvout],
        out_shape=[jax.ShapeDtypeStruct((s, qd), BF16), jax.ShapeDtypeStruct((s, kvd), BF16),
                   jax.ShapeDtypeStruct((s, kvd), BF16)],
        scratch_shapes=[pltpu.VMEM((s, HEAD_DIM), F32), pltpu.VMEM((s, HEAD_DIM), F32)],
        compiler_params=_params("parallel", "arbitrary"),
    )(z, z, z, o, lse, do)


def _shift_down(v, n):
    rolled = pltpu.roll(v, n, 0)
    t = lax.broadcasted_iota(jnp.int32, v.shape, 0)
    return jnp.where(t >= n, rolled, 0.0)


def _shift_up(v, n):
    rows = v.shape[0]
    rolled = pltpu.roll(v, rows - n, 0)
    t = lax.broadcasted_iota(jnp.int32, v.shape, 0)
    return jnp.where(t < rows - n, rolled, 0.0)


def _conv_specs(s, base, cd, tc):
    zs = [pl.BlockSpec((s, tc), functools.partial(lambda j, off: (0, off + j), off=(base + n * cd) // tc))
          for n in range(3)]
    wspec = pl.BlockSpec((SUBLANES, tc), lambda j: (0, j))
    cspec = pl.BlockSpec((s, tc), lambda j: (0, j))
    return zs, wspec, cspec


def _conv_fwd(name, z, conv_w, base, cd):
    s = z.shape[0]
    tc = _tile(cd, 256)
    zs, wspec, cspec = _conv_specs(s, base, cd, tc)

    def body(h_ref, b_ref, c_ref, w_ref, o_ref):
        u = c_ref[...].astype(F32) * h_ref[...].astype(F32)
        y = w_ref[0:1, :] * _shift_down(u, 2) + w_ref[1:2, :] * _shift_down(u, 1) + w_ref[2:3, :] * u
        o_ref[...] = b_ref[...].astype(F32) * y

    return pl.pallas_call(
        body, name=name, grid=(cd // tc,), in_specs=zs + [wspec], out_specs=cspec,
        out_shape=jax.ShapeDtypeStruct((s, cd), F32), compiler_params=_params("parallel"),
    )(z, z, z, conv_w)


def _conv_bwd(name, z, conv_w, dc, base, cd):
    s = z.shape[0]
    tc = _tile(cd, 256)
    zs, wspec, cspec = _conv_specs(s, base, cd, tc)

    def body(h_ref, b_ref, c_ref, w_ref, dc_ref, dh_ref, db_ref, dcg_ref, dw_ref):
        hv, bv, cv = h_ref[...].astype(F32), b_ref[...].astype(F32), c_ref[...].astype(F32)
        u = cv * hv
        u1, u2 = _shift_down(u, 1), _shift_down(u, 2)
        w0, w1, w2 = w_ref[0:1, :], w_ref[1:2, :], w_ref[2:3, :]
        y = w0 * u2 + w1 * u1 + w2 * u
        dcv = dc_ref[...]
        db_ref[...] = (dcv * y).astype(db_ref.dtype)
        dy = dcv * bv
        du = w2 * dy + w1 * _shift_up(dy, 1) + w0 * _shift_up(dy, 2)
        dh_ref[...] = (du * cv).astype(dh_ref.dtype)
        dcg_ref[...] = (du * hv).astype(dcg_ref.dtype)
        g0 = jnp.sum(dy * u2, axis=0, keepdims=True)
        g1 = jnp.sum(dy * u1, axis=0, keepdims=True)
        g2 = jnp.sum(dy * u, axis=0, keepdims=True)
        r = lax.broadcasted_iota(jnp.int32, (SUBLANES, tc), 0)
        dw_ref[...] = jnp.where(r == 0, g0, jnp.where(r == 1, g1, jnp.where(r == 2, g2, 0.0)))

    return pl.pallas_call(
        body, name=name, grid=(cd // tc,), in_specs=zs + [wspec, cspec],
        out_specs=[cspec, cspec, cspec, wspec],
        out_shape=[jax.ShapeDtypeStruct((s, cd), BF16)] * 3 + [jax.ShapeDtypeStruct((SUBLANES, cd), F32)],
        compiler_params=_params("parallel"),
    )(z, z, z, conv_w, dc)


def _cat_norm_fwd(name, a, c, ga, gc):
    s, qd = a.shape
    cd = c.shape[1]
    tr = _row_tile(s, qd + cd)

    def body(a_ref, c_ref, ga_ref, gc_ref, o_ref):
        av, cv = a_ref[...], c_ref[...]
        ra = lax.rsqrt(jnp.mean(av * av, axis=-1, keepdims=True) + NORM_EPS)
        rc = lax.rsqrt(jnp.mean(cv * cv, axis=-1, keepdims=True) + NORM_EPS)
        o_ref[:, :qd] = (av * ra * ga_ref[...]).astype(o_ref.dtype)
        o_ref[:, qd:] = (cv * rc * gc_ref[...]).astype(o_ref.dtype)

    return pl.pallas_call(
        body, name=name, grid=(s // tr,),
        in_specs=[pl.BlockSpec((tr, qd), lambda i: (i, 0)), pl.BlockSpec((tr, cd), lambda i: (i, 0)),
                  pl.BlockSpec((1, qd), lambda i: (0, 0)), pl.BlockSpec((1, cd), lambda i: (0, 0))],
        out_specs=pl.BlockSpec((tr, qd + cd), lambda i: (i, 0)),
        out_shape=jax.ShapeDtypeStruct((s, qd + cd), BF16), compiler_params=_params("parallel"),
    )(a, c, ga, gc)


def _cat_norm_bwd(name, dcat, a, c, ga, gc):
    s, qd = a.shape
    cd = c.shape[1]
    tr = _row_tile(s, qd + cd)

    def one(dn, yv, gv):
        r = lax.rsqrt(jnp.mean(yv * yv, axis=-1, keepdims=True) + NORM_EPS)
        xhat = yv * r
        dxn = dn * gv
        return r * (dxn - xhat * jnp.mean(dxn * xhat, axis=-1, keepdims=True)), _sum_to_sublanes(dn * xhat)

    def body(d_ref, a_ref, c_ref, ga_ref, gc_ref, da_ref, dc_ref, dga_ref, dgc_ref):
        da, pa = one(d_ref[:, :qd], a_ref[...], ga_ref[...])
        dc, pc = one(d_ref[:, qd:], c_ref[...], gc_ref[...])
        da_ref[...] = da
        dc_ref[...] = dc

        @pl.when(pl.program_id(0) == 0)
        def _():
            dga_ref[...] = pa
            dgc_ref[...] = pc

        @pl.when(pl.program_id(0) > 0)
        def _():
            dga_ref[...] += pa
            dgc_ref[...] += pc

    ra = pl.BlockSpec((tr, qd), lambda i: (i, 0))
    rc = pl.BlockSpec((tr, cd), lambda i: (i, 0))
    return pl.pallas_call(
        body, name=name, grid=(s // tr,),
        in_specs=[pl.BlockSpec((tr, qd + cd), lambda i: (i, 0)), ra, rc,
                  pl.BlockSpec((1, qd), lambda i: (0, 0)), pl.BlockSpec((1, cd), lambda i: (0, 0))],
        out_specs=[ra, rc, pl.BlockSpec((SUBLANES, qd), lambda i: (0, 0)),
                   pl.BlockSpec((SUBLANES, cd), lambda i: (0, 0))],
        out_shape=[jax.ShapeDtypeStruct((s, qd), F32), jax.ShapeDtypeStruct((s, cd), F32),
                   jax.ShapeDtypeStruct((SUBLANES, qd), F32), jax.ShapeDtypeStruct((SUBLANES, cd), F32)],
        compiler_params=_params("arbitrary"),
    )(dcat, a, c, ga, gc)


def _adamw(name, w, g, m, v):
    shape = w.shape
    cols = shape[-1]
    rows = w.size // cols
    tr = _row_tile(rows, cols, budget=3 << 19)
    bc1 = 1.0 - ADAM_B1 ** ADAM_STEP
    bc2 = 1.0 - ADAM_B2 ** ADAM_STEP

    def body(w_ref, g_ref, m_ref, v_ref, d_ref, nm_ref, nv_ref):
        gv = g_ref[...]
        mv = ADAM_B1 * m_ref[...] + (1.0 - ADAM_B1) * gv
        vv = ADAM_B2 * v_ref[...] + (1.0 - ADAM_B2) * (gv * gv)
        nm_ref[...] = mv
        nv_ref[...] = vv
        d_ref[...] = -ADAM_LR * ((mv / bc1) / (jnp.sqrt(vv / bc2) + ADAM_EPS) + ADAM_WD * w_ref[...])

    row = pl.BlockSpec((tr, cols), lambda i: (i, 0))
    outs = pl.pallas_call(
        body, name=name, grid=(rows // tr,), in_specs=[row] * 4, out_specs=[row] * 3,
        out_shape=[jax.ShapeDtypeStruct((rows, cols), F32)] * 3, compiler_params=_params("parallel"),
    )(*(t.reshape(rows, cols) for t in (w, g, m, v)))
    return tuple(t.reshape(shape) for t in outs)


HBM_SPEC = pl.BlockSpec(memory_space=pltpu.HBM)


def _mesh_place():
    x, y, c = lax.axis_index("x"), lax.axis_index("y"), lax.axis_index("c")
    other_chips = [(1 - x, y), (x, 1 - y), (1 - x, 1 - y)]
    return x, y, c, other_chips


def _cast_into_slot(name, w, layer, chip):
    _, r, cols = w.shape
    tr = _row_tile(r, cols)

    def body(chip_ref, w_ref, o_ref):
        o_ref[...] = w_ref[...].astype(o_ref.dtype)

    return pl.pallas_call(
        body, name=name,
        grid_spec=pltpu.PrefetchScalarGridSpec(
            num_scalar_prefetch=1, grid=(r // tr,),
            in_specs=[pl.BlockSpec((None, tr, cols), lambda i, chip_ref: (layer, i, 0))],
            out_specs=pl.BlockSpec((None, tr, cols), lambda i, chip_ref: (chip_ref[0], i, 0))),
        out_shape=jax.ShapeDtypeStruct((N_CHIPS, r, cols), BF16), compiler_params=_params("parallel"),
    )(chip, w)


SEM_SPEC = pl.BlockSpec(memory_space=pltpu.SEMAPHORE)
ANY_SPEC = pl.BlockSpec(memory_space=pl.ANY)
SPLIT_COPY = pltpu.CompilerParams(has_side_effects=pltpu.SideEffectType.DATAFLOW_SIDE_EFFECTING)
N_OTHER = N_CHIPS - 1


def _in_hbm(arr):
    return pltpu.with_memory_space_constraint(arr, pltpu.HBM)


def _half_rows(ref, chip_idx, core):
    r2 = ref.shape[1] // 2
    return ref.at[chip_idx, pl.ds(core * r2, r2), :]


def _gather_start(fulls, after):
    na = len(fulls)

    def body(*refs):
        f_refs = refs[na + 1:2 * na + 1]
        send_sems, recv_sems = refs[2 * na + 1:3 * na + 1], refs[3 * na + 1:4 * na + 1]
        x, y, c, chips = _mesh_place()
        for a in range(na):
            mine = _half_rows(f_refs[a], 2 * x + y, c)
            for j, (cx, cy) in enumerate(chips):
                pltpu.make_async_remote_copy(
                    src_ref=mine, dst_ref=mine, send_sem=send_sems[a].at[j], recv_sem=recv_sems[a].at[j],
                    device_id=(cx, cy, c), device_id_type=MESH).start()

    outs = pl.pallas_call(
        body, name="gather_start", in_specs=[HBM_SPEC] * na + [ANY_SPEC],
        out_specs=[HBM_SPEC] * na + [SEM_SPEC] * (2 * na),
        out_shape=[pltpu.HBM(f.shape, f.dtype) for f in fulls] + [pltpu.SemaphoreType.DMA((N_OTHER,))] * (2 * na),
        input_output_aliases={a: a for a in range(na)}, compiler_params=SPLIT_COPY,
    )(*[_in_hbm(f) for f in fulls], after)
    return outs[:na], outs[na:2 * na], outs[2 * na:]


def _gather_pass_on(name, full, recv_sems, after):
    def body(f_in, recv_sems, after_ref, f_ref, d2d_send, d2d_recv):
        x, y, c, chips = _mesh_place()
        for j, (cx, cy) in enumerate(chips):
            blk = _half_rows(f_ref, 2 * cx + cy, c)
            pltpu.make_async_remote_copy(
                src_ref=blk, dst_ref=blk, send_sem=d2d_send.at[j], recv_sem=recv_sems.at[j],
                device_id=(cx, cy, c), device_id_type=MESH).wait_recv()
            pltpu.make_async_remote_copy(
                src_ref=blk, dst_ref=blk, send_sem=d2d_send.at[j], recv_sem=d2d_recv.at[j],
                device_id=(x, y, 1 - c), device_id_type=MESH).start()

    return pl.pallas_call(
        body, name=name, in_specs=[HBM_SPEC, SEM_SPEC, ANY_SPEC], out_specs=[HBM_SPEC, SEM_SPEC, SEM_SPEC],
        out_shape=[pltpu.HBM(full.shape, full.dtype)] + [pltpu.SemaphoreType.DMA((N_OTHER,))] * 2,
        input_output_aliases={0: 0}, compiler_params=SPLIT_COPY,
    )(full, recv_sems, after)


def _gather_arrive(name, full, ici_send, d2d_send, d2d_recv, after):
    def body(f_in, ici_send, d2d_send, d2d_recv, after_ref, f_ref):
        x, y, c, chips = _mesh_place()
        for j, (cx, cy) in enumerate(chips):
            mine = _half_rows(f_ref, 2 * x + y, c)
            passed = _half_rows(f_ref, 2 * cx + cy, c)
            theirs = _half_rows(f_ref, 2 * cx + cy, 1 - c)
            pltpu.make_async_remote_copy(
                src_ref=mine, dst_ref=mine, send_sem=ici_send.at[j], recv_sem=d2d_recv.at[j],
                device_id=(cx, cy, c), device_id_type=MESH).wait_send()
            pltpu.make_async_remote_copy(
                src_ref=passed, dst_ref=passed, send_sem=d2d_send.at[j], recv_sem=d2d_recv.at[j],
                device_id=(x, y, 1 - c), device_id_type=MESH).wait_send()
            pltpu.make_async_remote_copy(
                src_ref=theirs, dst_ref=theirs, send_sem=d2d_send.at[j], recv_sem=d2d_recv.at[j],
                device_id=(x, y, 1 - c), device_id_type=MESH).wait_recv()

    return pl.pallas_call(
        body, name=name, in_specs=[HBM_SPEC, SEM_SPEC, SEM_SPEC, SEM_SPEC, ANY_SPEC], out_specs=HBM_SPEC,
        out_shape=pltpu.HBM(full.shape, full.dtype), input_output_aliases={0: 0}, compiler_params=SPLIT_COPY,
    )(full, ici_send, d2d_send, d2d_recv, after)


def _gather_taps(conv_w):
    def body(cw_ref, cwf_ref, send_sems, recv_sems, local_sem):
        x, y, c, chips = _mesh_place()
        k_me = 2 * x + y
        local = pltpu.make_async_copy(cw_ref, cwf_ref.at[k_me], local_sem)
        local.start()
        copies = [pltpu.make_async_remote_copy(
            src_ref=cw_ref, dst_ref=cwf_ref.at[k_me], send_sem=send_sems.at[j], recv_sem=recv_sems.at[j],
            device_id=(cx, cy, c), device_id_type=MESH) for j, (cx, cy) in enumerate(chips)]
        for cp in copies:
            cp.start()
        for j, (cx, cy) in enumerate(chips):
            pltpu.make_async_remote_copy(
                src_ref=cw_ref, dst_ref=cwf_ref.at[2 * cx + cy], send_sem=send_sems.at[j], recv_sem=recv_sems.at[j],
                device_id=(cx, cy, c), device_id_type=MESH).wait_recv()
        for cp in copies:
            cp.wait_send()
        local.wait()

    return pl.pallas_call(
        body, name="gather_taps", in_specs=[HBM_SPEC], out_specs=HBM_SPEC,
        out_shape=jax.ShapeDtypeStruct((N_CHIPS,) + conv_w.shape, conv_w.dtype),
        scratch_shapes=[pltpu.SemaphoreType.DMA((N_OTHER,))] * 2 + [pltpu.SemaphoreType.DMA],
    )(conv_w)


def _swap_core_halves(gs):
    na = len(gs)

    def body(*refs):
        g_refs, o_refs = refs[:na], refs[na:2 * na]
        send_sems, recv_sems = refs[2 * na:]
        x, y, c, _ = _mesh_place()
        copies = []
        for a in range(na):
            r2 = g_refs[a].shape[1] // 2
            copies.append(pltpu.make_async_remote_copy(
                src_ref=g_refs[a].at[:, pl.ds((1 - c) * r2, r2), :], dst_ref=o_refs[a],
                send_sem=send_sems.at[a], recv_sem=recv_sems.at[a], device_id=(x, y, 1 - c), device_id_type=MESH))
        for cp in copies:
            cp.start()
        for cp in copies:
            cp.wait()

    return pl.pallas_call(
        body, name="swap_core_halves", in_specs=[HBM_SPEC] * na, out_specs=[HBM_SPEC] * na,
        out_shape=[jax.ShapeDtypeStruct((g.shape[0], g.shape[1] // 2, g.shape[2]), g.dtype) for g in gs],
        scratch_shapes=[pltpu.SemaphoreType.DMA((na,))] * 2,
    )(*gs)


def _add_core_halves(name, g, sib, core):
    nb, r, cols = g.shape
    r2 = r // 2
    tr = _row_tile(r2, cols, itemsize=2, budget=1 << 20)
    nrt = r2 // tr

    def body(core_ref, g_ref, s_ref, o_ref):
        o_ref[...] = (g_ref[...].astype(F32) + s_ref[...].astype(F32)).astype(o_ref.dtype)

    return pl.pallas_call(
        body, name=name,
        grid_spec=pltpu.PrefetchScalarGridSpec(
            num_scalar_prefetch=1, grid=(nb, nrt),
            in_specs=[pl.BlockSpec((None, tr, cols), lambda k, i, core_ref: (k, core_ref[0] * nrt + i, 0)),
                      pl.BlockSpec((None, tr, cols), lambda k, i, core_ref: (k, i, 0))],
            out_specs=pl.BlockSpec((None, tr, cols), lambda k, i, core_ref: (k, i, 0))),
        out_shape=jax.ShapeDtypeStruct((nb, r2, cols), BF16), compiler_params=_params("parallel", "parallel"),
    )(core, g, sib)


def _scatter_to_chips(hs):
    na = len(hs)

    def body(*refs):
        h_refs, o_refs = refs[:na], refs[na:2 * na]
        send_sems, recv_sems = refs[2 * na:]
        x, y, c, chips = _mesh_place()
        copies = []
        for a in range(na):
            for j, (cx, cy) in enumerate(chips):
                copies.append(pltpu.make_async_remote_copy(
                    src_ref=h_refs[a].at[2 * cx + cy], dst_ref=o_refs[a].at[j],
                    send_sem=send_sems.at[a * 3 + j], recv_sem=recv_sems.at[a * 3 + j],
                    device_id=(cx, cy, c), device_id_type=MESH))
        for cp in copies:
            cp.start()
        for cp in copies:
            cp.wait()

    return pl.pallas_call(
        body, name="scatter_to_chips", in_specs=[HBM_SPEC] * na, out_specs=[HBM_SPEC] * na,
        out_shape=[jax.ShapeDtypeStruct((N_CHIPS - 1,) + h.shape[1:], h.dtype) for h in hs],
        scratch_shapes=[pltpu.SemaphoreType.DMA((na * 3,))] * 2,
    )(*hs)


def _sum_chips(name, hs, rcv, core, chip, layer, n_layers, prev):
    _, r2, cols = hs.shape
    tr = _row_tile(r2, cols, budget=1 << 20)
    nrt = r2 // tr

    def body(core_ref, chip_ref, h_ref, r_ref, *rest):
        o_ref = rest[-1]
        acc = h_ref[...].astype(F32)
        for j in range(N_CHIPS - 1):
            acc = acc + r_ref[j].astype(F32)
        o_ref[...] = acc

    in_specs = [pl.BlockSpec((None, tr, cols), lambda i, core_ref, chip_ref: (chip_ref[0], i, 0)),
                pl.BlockSpec((N_CHIPS - 1, tr, cols), lambda i, core_ref, chip_ref: (0, i, 0))]
    args = [core, chip, hs, rcv]
    aliases = {}
    if prev is not None:
        in_specs.append(pl.BlockSpec(memory_space=pl.ANY))
        args.append(prev)
        aliases = {4: 0}
    return pl.pallas_call(
        body, name=name,
        grid_spec=pltpu.PrefetchScalarGridSpec(
            num_scalar_prefetch=2, grid=(nrt,), in_specs=in_specs,
            out_specs=pl.BlockSpec((None, tr, cols), lambda i, core_ref, chip_ref: (layer, core_ref[0] * nrt + i, 0))),
        out_shape=jax.ShapeDtypeStruct((n_layers, 2 * r2, cols), F32), input_output_aliases=aliases,
        compiler_params=_params("parallel"),
    )(*args)


def _join_core_halves(ts):
    na = len(ts)

    def body(*refs):
        o_refs = refs[na:2 * na]
        send_sems, recv_sems = refs[2 * na:]
        x, y, c, _ = _mesh_place()
        copies = []
        for a in range(na):
            r2 = o_refs[a].shape[1] // 2
            mine = o_refs[a].at[:, pl.ds(c * r2, r2), :]
            copies.append(pltpu.make_async_remote_copy(
                src_ref=mine, dst_ref=mine, send_sem=send_sems.at[a], recv_sem=recv_sems.at[a],
                device_id=(x, y, 1 - c), device_id_type=MESH))
        for cp in copies:
            cp.start()
        for cp in copies:
            cp.wait()

    return pl.pallas_call(
        body, name="join_core_halves", in_specs=[HBM_SPEC] * na, out_specs=[HBM_SPEC] * na,
        out_shape=[jax.ShapeDtypeStruct(t.shape, t.dtype) for t in ts],
        input_output_aliases={a: a for a in range(na)},
        scratch_shapes=[pltpu.SemaphoreType.DMA((na,))] * 2,
    )(*ts)


def _allreduce_small(p):
    n, _, w = p.shape

    def body(p_ref, o_ref, buf, send_sems, recv_sems):
        x, y, c, _ = _mesh_place()
        me = 4 * x + 2 * y + c
        buf[me] = jnp.sum(p_ref[...], axis=1)
        copies = []
        for pat in range(1, N_DEV):
            fx, fy, fc = (pat >> 2) & 1, (pat >> 1) & 1, pat & 1
            copies.append(pltpu.make_async_remote_copy(
                src_ref=buf.at[me], dst_ref=buf.at[me], send_sem=send_sems.at[pat - 1], recv_sem=recv_sems.at[pat - 1],
                device_id=(x ^ fx, y ^ fy, c ^ fc), device_id_type=MESH))
        for cp in copies:
            cp.start()
        for cp in copies:
            cp.wait()
        acc = buf[0]
        for dev in range(1, N_DEV):
            acc = acc + buf[dev]
        o_ref[...] = acc

    return pl.pallas_call(
        body, name="allreduce_small", in_specs=[pl.BlockSpec(memory_space=pltpu.VMEM)],
        out_specs=pl.BlockSpec(memory_space=pltpu.VMEM), out_shape=jax.ShapeDtypeStruct((n, w), F32),
        scratch_shapes=[pltpu.VMEM((N_DEV, n, w), F32), pltpu.SemaphoreType.DMA((N_DEV - 1,)),
                        pltpu.SemaphoreType.DMA((N_DEV - 1,))],
    )(p)


class _WeightFeed:
    def __init__(self, fulls, after):
        self.fulls, self.ici_send, self.ici_recv = (list(t) for t in _gather_start(fulls, after))
        self.d2d = [None] * len(fulls)

    def _pass_on(self, k, after):
        if k < len(self.fulls) and self.d2d[k] is None:
            self.fulls[k], send, recv = _gather_pass_on(f"gather_pass_{k}", self.fulls[k], self.ici_recv[k], after)
            self.d2d[k] = (send, recv)

    def take(self, k, after):
        self._pass_on(k, after)
        self._pass_on(k + 1, after)
        if k + 1 < len(self.fulls):
            after = self.fulls[k + 1]
        self.fulls[k] = _gather_arrive(f"gather_arrive_{k}", self.fulls[k], self.ici_send[k], *self.d2d[k], after)
        return self.fulls[k]


def _ffn_forward(tag, x, g_pre, g_post, feed, k):
    s, d = x.shape
    h = _norm_fwd(f"{tag}_norm", x, g_pre)
    gu_w = feed.take(k, h)
    gu, a = _ffn_up(f"{tag}_up", h, gu_w)
    dn_w = feed.take(k + 1, a).reshape(-1, d)
    f = dn_w.shape[0]
    tk = _tile(f, 1408)
    tm, tn = _tile(s, 1024), _tile(d, 1024)
    y = _mm(f"{tag}_down", a, dn_w, mode="nn", grid=(s // tm, d // tn, f // tk),
            a_spec=pl.BlockSpec((tm, tk), lambda i, j, k: (i, k)),
            b_spec=pl.BlockSpec((tk, tn), lambda i, j, k: (k, j)),
            o_spec=pl.BlockSpec((tm, tn), lambda i, j, k: (i, j)),
            out_shape=jax.ShapeDtypeStruct((s, d), F32), nk=f // tk, acc_shape=(tm, tn))
    x_new = _res_norm(f"{tag}_post", x, y, g_post, FFN_RESIDUAL_WEIGHT)
    return x_new, (x, h, gu, a, y)


def _ffn_backward(tag, dx_new, saved, g_pre, g_post, gu_w, dn_w):
    x, h, gu, a, y = saved
    s, d = x.shape
    nb, fs = gu_w.shape[0], gu_w.shape[2]
    f = dn_w.shape[0]
    fr = f // nb
    dy, dg_post = _norm_bwd(f"{tag}_post_bwd", dx_new, y, g_post, FFN_RESIDUAL_WEIGHT, None, BF16)
    dgu = _ffn_dact(f"{tag}_dact", dy, dn_w, gu)
    dgu4 = dgu.reshape(nb, s, fs)
    tn = _tile(d, 1024)
    d_wd = _mm(f"{tag}_dwd", a, dy, mode="tn", grid=(nb, d // tn),
               a_spec=pl.BlockSpec((s, fr), lambda i, j: (0, i)),
               b_spec=pl.BlockSpec((s, tn), lambda i, j: (0, j)),
               o_spec=pl.BlockSpec((None, fr, tn), lambda i, j: (i, 0, j)),
               out_shape=jax.ShapeDtypeStruct((nb, fr, d), BF16))
    tm, tw = _tile(d, 512), _tile(fs, 1408)
    nw = fs // tw
    d_wgu = _mm(f"{tag}_dwgu", h, dgu4, mode="tn", grid=(nb, nw, d // tm),
                a_spec=pl.BlockSpec((s, tm), lambda k, j, i: (0, i)),
                b_spec=pl.BlockSpec((None, s, tw), lambda k, j, i: (k, 0, j)),
                o_spec=pl.BlockSpec((None, tm, tw), lambda k, j, i: (k, i, j)),
                out_shape=jax.ShapeDtypeStruct((nb, d, fs), BF16))
    ts, td = _tile(s, 1024), _tile(d, 1024)
    dh = _mm(f"{tag}_dh", dgu4, gu_w, mode="nt", grid=(s // ts, d // td, nb),
             a_spec=pl.BlockSpec((None, ts, fs), lambda i, j, k: (k, i, 0)),
             b_spec=pl.BlockSpec((None, td, fs), lambda i, j, k: (k, j, 0)),
             o_spec=pl.BlockSpec((ts, td), lambda i, j, k: (i, j)),
             out_shape=jax.ShapeDtypeStruct((s, d), F32), nk=nb, acc_shape=(ts, td))
    dx, dg_pre = _norm_bwd(f"{tag}_pre_bwd", dh, x, g_pre, 1.0, dx_new, F32)
    return dx, d_wgu, d_wd, dg_pre, dg_post


def _mixer_forward(tag, x, gains, feed, k, conv_taps, dims):
    qd, kvd, cd = dims
    s, d = x.shape
    g_pre, g_a, g_c, g_post = gains
    h = _norm_fwd(f"{tag}_norm", x, g_pre)
    win_w = feed.take(k, h)
    nb, cw = win_w.shape[0], win_w.shape[2]
    tm = _tile(s, 1024)
    z = _mm(f"{tag}_in", h, win_w, mode="nn", grid=(nb, s // tm),
            a_spec=pl.BlockSpec((tm, d), lambda j, i: (i, 0)),
            b_spec=pl.BlockSpec((None, d, cw), lambda j, i: (j, 0, 0)),
            o_spec=pl.BlockSpec((tm, cw), lambda j, i: (i, j)),
            out_shape=jax.ShapeDtypeStruct((s, nb * cw), BF16))
    a, lse = _attn_fwd(f"{tag}_attn", z, qd, kvd)
    c = _conv_fwd(f"{tag}_conv", z, conv_taps, qd + 2 * kvd, cd)
    cat = _cat_norm_fwd(f"{tag}_cat", a, c, g_a, g_c)
    wout_w = feed.take(k + 1, cat).reshape(-1, d)
    mw = qd + cd
    tn = _tile(d, 1024)
    mixed = _mm(f"{tag}_out", cat, wout_w, mode="nn", grid=(s // tm, d // tn),
                a_spec=pl.BlockSpec((tm, mw), lambda i, j: (i, 0)),
                b_spec=pl.BlockSpec((mw, tn), lambda i, j: (0, j)),
                o_spec=pl.BlockSpec((tm, tn), lambda i, j: (i, j)),
                out_shape=jax.ShapeDtypeStruct((s, d), F32))
    x_new = _res_norm(f"{tag}_post", x, mixed, g_post, 1.0)
    return x_new, (x, h, z, a, lse, c, cat, mixed)


def _mixer_backward(tag, dx_new, saved, gains, win_w, conv_taps, wout_w, dims):
    qd, kvd, cd = dims
    x, h, z, a, lse, c, cat, mixed = saved
    s, d = x.shape
    nb, cw = win_w.shape[0], win_w.shape[2]
    g_pre, g_a, g_c, g_post = gains
    mw = qd + cd
    dmixed, dg_post = _norm_bwd(f"{tag}_post_bwd", dx_new, mixed, g_post, 1.0, None, BF16)
    tm, tn = _tile(s, 1024), _tile(mw, 1024)
    dcat = _mm(f"{tag}_dcat", dmixed, wout_w, mode="nt", grid=(s // tm, mw // tn),
               a_spec=pl.BlockSpec((tm, d), lambda i, j: (i, 0)),
               b_spec=pl.BlockSpec((tn, d), lambda i, j: (j, 0)),
               o_spec=pl.BlockSpec((tm, tn), lambda i, j: (i, j)),
               out_shape=jax.ShapeDtypeStruct((s, mw), F32))
    wr = mw // nb
    td = _tile(d, 1024)
    d_wout = _mm(f"{tag}_dwout", cat, dmixed, mode="tn", grid=(nb, d // td),
                 a_spec=pl.BlockSpec((s, wr), lambda i, j: (0, i)),
                 b_spec=pl.BlockSpec((s, td), lambda i, j: (0, j)),
                 o_spec=pl.BlockSpec((None, wr, td), lambda i, j: (i, 0, j)),
                 out_shape=jax.ShapeDtypeStruct((nb, wr, d), BF16))
    da, dc, dg_a, dg_c = _cat_norm_bwd(f"{tag}_cat_bwd", dcat, a, c, g_a, g_c)
    dhc, dbg, dcg, d_taps = _conv_bwd(f"{tag}_conv_bwd", z, conv_taps, dc, qd + 2 * kvd, cd)
    dq, dk, dv = _attn_bwd(f"{tag}_attn_bwd", z, a, lse, da, qd, kvd)
    dz = jnp.concatenate([dq, dk, dv, dhc, dbg, dcg], axis=1)
    th = _tile(d, 512)
    d_win = _mm(f"{tag}_dwin", h, dz, mode="tn", grid=(nb, d // th),
                a_spec=pl.BlockSpec((s, th), lambda k, i: (0, i)),
                b_spec=pl.BlockSpec((s, cw), lambda k, i: (0, k)),
                o_spec=pl.BlockSpec((None, th, cw), lambda k, i: (k, i, 0)),
                out_shape=jax.ShapeDtypeStruct((nb, d, cw), BF16))
    dh = _mm(f"{tag}_dh", dz, win_w, mode="nt", grid=(s // tm, d // td, nb),
             a_spec=pl.BlockSpec((tm, cw), lambda i, j, k: (i, k)),
             b_spec=pl.BlockSpec((None, td, cw), lambda i, j, k: (k, j, 0)),
             o_spec=pl.BlockSpec((tm, td), lambda i, j, k: (i, j)),
             out_shape=jax.ShapeDtypeStruct((s, d), F32), nk=nb, acc_shape=(tm, td))
    dx, dg_pre = _norm_bwd(f"{tag}_pre_bwd", dh, x, g_pre, 1.0, dx_new, F32)
    return dx, d_win, d_wout, d_taps, (dg_pre, dg_a, dg_c, dg_post)


def _pad_cols(v, width):
    return jnp.pad(v, ((0, 0), (0, width - v.shape[1])))


def kernel(x, ffn1_norm_pre, ffn1_w_gate_up, ffn1_w_down, ffn1_norm_post, mix_norm_pre, w_in, conv_w, attn_out_norm, conv_out_norm, w_out, mix_norm_post, ffn2_norm_pre, ffn2_w_gate_up, ffn2_w_down, ffn2_norm_post, loss_target, m_ffn1_norm_pre, m_ffn1_w_gate_up, m_ffn1_w_down, m_ffn1_norm_post, m_mix_norm_pre, m_w_in, m_conv_w, m_attn_out_norm, m_conv_out_norm, m_w_out, m_mix_norm_post, m_ffn2_norm_pre, m_ffn2_w_gate_up, m_ffn2_w_down, m_ffn2_norm_post, v_ffn1_norm_pre, v_ffn1_w_gate_up, v_ffn1_w_down, v_ffn1_norm_post, v_mix_norm_pre, v_w_in, v_conv_w, v_attn_out_norm, v_conv_out_norm, v_w_out, v_mix_norm_post, v_ffn2_norm_pre, v_ffn2_w_gate_up, v_ffn2_w_down, v_ffn2_norm_post):
    _, s, d = x.shape
    n_layers = ffn1_norm_pre.shape[0]
    qd = attn_out_norm.shape[1]
    cd = conv_out_norm.shape[1]
    kvd = qd // Q_PER_KV
    dims = (qd, kvd, cd)
    assert N_CHIPS * w_in.shape[2] == qd + 2 * kvd + 3 * cd and qd + cd == N_CHIPS * w_out.shape[1]
    assert 2 * d <= SMALL_ROWS * LANES * SUBLANES
    chip = 2 * lax.axis_index("x") + lax.axis_index("y")
    chip_arr = chip.astype(jnp.int32).reshape(1)
    core = lax.axis_index("c").astype(jnp.int32).reshape(1)
    kinds = ("gu1", "dn1", "win", "wout", "gu2", "dn2")

    big = (ffn1_w_gate_up, ffn1_w_down, w_in, w_out, ffn2_w_gate_up, ffn2_w_down)
    nk = len(kinds)
    taps_all = _gather_taps(conv_w)
    feed = _WeightFeed([_cast_into_slot(f"cast_{k}_{layer}", w, layer, chip_arr)
                        for layer in range(n_layers) for k, w in zip(kinds, big)], taps_all)
    taps = jnp.transpose(taps_all, (1, 2, 0, 3)).reshape(n_layers, CONV_WIDTH, cd)
    taps = jnp.pad(taps, ((0, 0), (0, SUBLANES - CONV_WIDTH), (0, 0)))

    def gain(g, layer):
        return g[layer][None, :]

    xs = x[0]
    saved = []
    for layer in range(n_layers):
        t = f"l{layer}"
        k0 = layer * nk
        xs, s1 = _ffn_forward(f"{t}_ffn1", xs, gain(ffn1_norm_pre, layer), gain(ffn1_norm_post, layer), feed, k0)
        mix_gains = (gain(mix_norm_pre, layer), gain(attn_out_norm, layer), gain(conv_out_norm, layer), gain(mix_norm_post, layer))
        xs, s2 = _mixer_forward(f"{t}_mix", xs, mix_gains, feed, k0 + 2, taps[layer], dims)
        xs, s3 = _ffn_forward(f"{t}_ffn2", xs, gain(ffn2_norm_pre, layer), gain(ffn2_norm_post, layer), feed, k0 + 4)
        saved.append((s1, s2, s3, mix_gains))
    wts = {k: [feed.fulls[layer * nk + i] for layer in range(n_layers)] for i, k in enumerate(kinds)}
    for k in ("dn1", "wout", "dn2"):
        wts[k] = [w.reshape(-1, d) for w in wts[k]]
    dxs, loss_part = _loss_head("loss_head", xs, loss_target[0])
    loss = lax.psum(jnp.sum(loss_part), ("x", "y", "c"))

    grads = {k: [None] * n_layers for k in kinds}
    small = [None] * n_layers
    for layer in reversed(range(n_layers)):
        t = f"l{layer}"
        s1, s2, s3, mix_gains = saved[layer]
        dxs, grads["gu2"][layer], grads["dn2"][layer], p_pre2, p_post2 = _ffn_backward(
            f"{t}_ffn2", dxs, s3, gain(ffn2_norm_pre, layer), gain(ffn2_norm_post, layer),
            wts["gu2"][layer], wts["dn2"][layer])
        dxs, grads["win"][layer], grads["wout"][layer], p_taps, (p_mpre, p_a, p_c, p_mpost) = _mixer_backward(
            f"{t}_mix", dxs, s2, mix_gains, wts["win"][layer], taps[layer], wts["wout"][layer], dims)
        dxs, grads["gu1"][layer], grads["dn1"][layer], p_pre1, p_post1 = _ffn_backward(
            f"{t}_ffn1", dxs, s1, gain(ffn1_norm_pre, layer), gain(ffn1_norm_post, layer),
            wts["gu1"][layer], wts["dn1"][layer])
        tap_rows = jnp.zeros((CONV_WIDTH, SUBLANES, d), F32).at[:, 0, :cd].set(p_taps[:CONV_WIDTH])
        rows = [p_pre1, p_post1, p_mpre, jnp.concatenate([p_a, p_c], axis=1), p_mpost, p_pre2, p_post2]
        rows = jnp.concatenate([jnp.stack(rows), tap_rows], axis=0)
        small[layer] = jnp.pad(rows, ((0, SMALL_ROWS - rows.shape[0]), (0, 0), (0, 0)))
    grad_x = dxs[None]

    parts = [grads[k][layer] for k in kinds for layer in range(n_layers)]
    from_sibling = _swap_core_halves(parts)
    core_sums = [_add_core_halves(f"add_cores_{i}", g, sb, core) for i, (g, sb) in enumerate(zip(parts, from_sibling))]
    from_chips = _scatter_to_chips(core_sums)
    reduced = []
    for i in range(len(kinds)):
        buf = None
        for layer in range(n_layers):
            a = i * n_layers + layer
            buf = _sum_chips(f"sum_chips_{a}", core_sums[a], from_chips[a], core, chip_arr, layer, n_layers, buf)
        reduced.append(buf)
    g_gu1, g_dn1, g_win, g_wout, g_gu2, g_dn2 = _join_core_halves(reduced)

    small_sum = _allreduce_small(jnp.concatenate(small, axis=0)).reshape(n_layers, SMALL_ROWS, d)
    g_ffn1_pre, g_ffn1_post, g_mix_pre = small_sum[:, 0], small_sum[:, 1], small_sum[:, 2]
    g_attn_out, g_conv_out = small_sum[:, 3, :qd], small_sum[:, 3, qd:qd + cd]
    g_mix_post, g_ffn2_pre, g_ffn2_post = small_sum[:, 4], small_sum[:, 5], small_sum[:, 6]
    cc = conv_w.shape[2]
    g_conv = lax.dynamic_slice_in_dim(small_sum[:, 7:7 + CONV_WIDTH, :cd], chip * cc, cc, axis=2)

    weights = dict(ffn1_norm_pre=ffn1_norm_pre, ffn1_w_gate_up=ffn1_w_gate_up, ffn1_w_down=ffn1_w_down, ffn1_norm_post=ffn1_norm_post, mix_norm_pre=mix_norm_pre, w_in=w_in, conv_w=conv_w, attn_out_norm=attn_out_norm, conv_out_norm=conv_out_norm, w_out=w_out, mix_norm_post=mix_norm_post, ffn2_norm_pre=ffn2_norm_pre, ffn2_w_gate_up=ffn2_w_gate_up, ffn2_w_down=ffn2_w_down, ffn2_norm_post=ffn2_norm_post)
    m_in = dict(ffn1_norm_pre=m_ffn1_norm_pre, ffn1_w_gate_up=m_ffn1_w_gate_up, ffn1_w_down=m_ffn1_w_down, ffn1_norm_post=m_ffn1_norm_post, mix_norm_pre=m_mix_norm_pre, w_in=m_w_in, conv_w=m_conv_w, attn_out_norm=m_attn_out_norm, conv_out_norm=m_conv_out_norm, w_out=m_w_out, mix_norm_post=m_mix_norm_post, ffn2_norm_pre=m_ffn2_norm_pre, ffn2_w_gate_up=m_ffn2_w_gate_up, ffn2_w_down=m_ffn2_w_down, ffn2_norm_post=m_ffn2_norm_post)
    v_in = dict(ffn1_norm_pre=v_ffn1_norm_pre, ffn1_w_gate_up=v_ffn1_w_gate_up, ffn1_w_down=v_ffn1_w_down, ffn1_norm_post=v_ffn1_norm_post, mix_norm_pre=v_mix_norm_pre, w_in=v_w_in, conv_w=v_conv_w, attn_out_norm=v_attn_out_norm, conv_out_norm=v_conv_out_norm, w_out=v_w_out, mix_norm_post=v_mix_norm_post, ffn2_norm_pre=v_ffn2_norm_pre, ffn2_w_gate_up=v_ffn2_w_gate_up, ffn2_w_down=v_ffn2_w_down, ffn2_norm_post=v_ffn2_norm_post)
    grad = dict(ffn1_norm_pre=g_ffn1_pre, ffn1_w_gate_up=g_gu1, ffn1_w_down=g_dn1, ffn1_norm_post=g_ffn1_post, mix_norm_pre=g_mix_pre, w_in=g_win, conv_w=g_conv, attn_out_norm=g_attn_out, conv_out_norm=g_conv_out, w_out=g_wout, mix_norm_post=g_mix_post, ffn2_norm_pre=g_ffn2_pre, ffn2_w_gate_up=g_gu2, ffn2_w_down=g_dn2, ffn2_norm_post=g_ffn2_post)
    names = list(weights)

    delta, new_m, new_v = {}, {}, {}
    matrices = ("ffn1_w_gate_up", "ffn1_w_down", "w_in", "w_out", "ffn2_w_gate_up", "ffn2_w_down")
    for n in matrices:
        delta[n], new_m[n], new_v[n] = _adamw(f"adamw_{n}", weights[n], grad[n], m_in[n], v_in[n])
    vectors = [n for n in names if n not in matrices]

    def pack(tree):
        flat = jnp.concatenate([tree[n].reshape(-1) for n in vectors])
        return jnp.pad(flat, (0, -flat.size % (SUBLANES * LANES))).reshape(-1, LANES)

    packed = _adamw("adamw_small", pack(weights), pack(grad), pack(m_in), pack(v_in))
    offset = 0
    for n in vectors:
        size = weights[n].size
        for tree, flat in zip((delta, new_m, new_v), packed):
            tree[n] = flat.reshape(-1)[offset:offset + size].reshape(weights[n].shape)
        offset += size

    return (loss, grad_x, *[grad[n] for n in names], *[delta[n] for n in names],
            *[new_m[n] for n in names], *[new_v[n] for n in names])
```

```python
import functools

import jax
import jax.numpy as jnp
from jax import lax
from jax.experimental import pallas as pl
from jax.experimental.pallas import tpu as pltpu

F32 = jnp.float32
BF16 = jnp.bfloat16
MESH = pl.DeviceIdType.MESH

NORM_EPS = 1e-6
HEAD_DIM = 128
Q_PER_KV = 4
CONV_WIDTH = 3
FFN_RESIDUAL_WEIGHT = 0.5
DILATED_BRANCHES = ((128, 1), (512, 4), (2048, 16))
ADAM_LR = 0.001
ADAM_B1 = 0.9
ADAM_B2 = 0.999
ADAM_EPS = 1e-08
ADAM_WD = 0.01
ADAM_STEP = 10

N_CHIPS = 4
N_DEV = 8
V7X_VMEM_BYTES = 64 << 20
VMEM_LIMIT = V7X_VMEM_BYTES - (12 << 20)
SUBLANES = 8
LANES = 128
SMALL_ROWS = 16


def _params(*sem):
    return pltpu.CompilerParams(dimension_semantics=sem, vmem_limit_bytes=VMEM_LIMIT)


def _row_tile(rows, cols, itemsize=4, budget=2 << 20):
    t = rows
    while t * cols * itemsize > budget and t % 32 == 0:
        t //= 2
    return t


def _sum_to_sublanes(v):
    r, n = v.shape
    return v.reshape(r // SUBLANES, SUBLANES, n).sum(axis=0)


_DIMS = {
    "nn": (((1,), (0,)), ((), ())),
    "nt": (((1,), (1,)), ((), ())),
    "tn": (((0,), (0,)), ((), ())),
}


ANY_SPEC = pl.BlockSpec(memory_space=pl.ANY)


def _dot(a, b, mode):
    return lax.dot_general(a, b, _DIMS[mode], preferred_element_type=F32)


def _mm(name, a, b, *, mode, grid, a_spec, b_spec, o_spec, out_shape, nk=1, acc_shape=None, deps=()):
    nd = len(deps)

    def body(a_ref, b_ref, *rest):
        o_ref, scratch = rest[nd], rest[nd + 1:]
        r = _dot(a_ref[...], b_ref[...], mode)
        if nk == 1:
            o_ref[...] = r.astype(o_ref.dtype)
        else:
            acc = scratch[0]
            k = pl.program_id(len(grid) - 1)

            @pl.when(k == 0)
            def _():
                acc[...] = r

            @pl.when(k > 0)
            def _():
                acc[...] += r

            @pl.when(k == nk - 1)
            def _():
                o_ref[...] = acc[...].astype(o_ref.dtype)

    sem = ("parallel",) * (len(grid) - (1 if nk > 1 else 0)) + (("arbitrary",) if nk > 1 else ())
    return pl.pallas_call(
        body, name=name, grid=grid, in_specs=[a_spec, b_spec] + [ANY_SPEC] * nd, out_specs=o_spec,
        out_shape=out_shape, scratch_shapes=[pltpu.VMEM(acc_shape, F32)] if nk > 1 else [],
        compiler_params=_params(*sem),
    )(a, b, *deps)


def _tile(n, want):
    if n <= want:
        return n
    best = None
    for t in range(LANES, want + 1, LANES):
        if n % t == 0:
            best = t
    assert best is not None, (n, want)
    return best


def _norm_fwd(name, x, gain):
    s, d = x.shape
    tr = _row_tile(s, d)

    def body(x_ref, g_ref, o_ref):
        xv = x_ref[...]
        r = lax.rsqrt(jnp.mean(xv * xv, axis=-1, keepdims=True) + NORM_EPS)
        o_ref[...] = (xv * r * g_ref[...]).astype(o_ref.dtype)

    return pl.pallas_call(
        body, name=name, grid=(s // tr,),
        in_specs=[pl.BlockSpec((tr, d), lambda i: (i, 0)), pl.BlockSpec((1, d), lambda i: (0, 0))],
        out_specs=pl.BlockSpec((tr, d), lambda i: (i, 0)),
        out_shape=jax.ShapeDtypeStruct((s, d), BF16), compiler_params=_params("parallel"),
    )(x, gain)


def _res_norm(name, x, y, gain, scale):
    s, d = x.shape
    tr = _row_tile(s, d)

    def body(x_ref, y_ref, g_ref, o_ref):
        yv = y_ref[...]
        r = lax.rsqrt(jnp.mean(yv * yv, axis=-1, keepdims=True) + NORM_EPS)
        o_ref[...] = x_ref[...] + scale * (yv * r * g_ref[...])

    row = pl.BlockSpec((tr, d), lambda i: (i, 0))
    return pl.pallas_call(
        body, name=name, grid=(s // tr,),
        in_specs=[row, row, pl.BlockSpec((1, d), lambda i: (0, 0))], out_specs=row,
        out_shape=jax.ShapeDtypeStruct((s, d), F32), compiler_params=_params("parallel"),
    )(x, y, gain)


def _norm_bwd(name, dout, yin, gain, scale, resid, out_dtype):
    s, d = yin.shape
    tr = _row_tile(s, d)
    has_resid = resid is not None

    def body(*refs):
        if has_resid:
            do_ref, y_ref, g_ref, r_ref, di_ref, dg_ref = refs
        else:
            do_ref, y_ref, g_ref, di_ref, dg_ref = refs
        yv = y_ref[...]
        r = lax.rsqrt(jnp.mean(yv * yv, axis=-1, keepdims=True) + NORM_EPS)
        xhat = yv * r
        dn = scale * do_ref[...]
        part = _sum_to_sublanes(dn * xhat)

        @pl.when(pl.program_id(0) == 0)
        def _():
            dg_ref[...] = part

        @pl.when(pl.program_id(0) > 0)
        def _():
            dg_ref[...] += part

        dxn = dn * g_ref[...]
        din = r * (dxn - xhat * jnp.mean(dxn * xhat, axis=-1, keepdims=True))
        if has_resid:
            din = din + r_ref[...]
        di_ref[...] = din.astype(di_ref.dtype)

    row = pl.BlockSpec((tr, d), lambda i: (i, 0))
    vec = pl.BlockSpec((1, d), lambda i: (0, 0))
    ins = [row, row, vec] + ([row] if has_resid else [])
    args = (dout, yin, gain) + ((resid,) if has_resid else ())
    return pl.pallas_call(
        body, name=name, grid=(s // tr,), in_specs=ins,
        out_specs=[row, pl.BlockSpec((SUBLANES, d), lambda i: (0, 0))],
        out_shape=[jax.ShapeDtypeStruct((s, d), out_dtype), jax.ShapeDtypeStruct((SUBLANES, d), F32)],
        compiler_params=_params("arbitrary"),
    )(*args)


def _loss_head(name, y, target):
    s, d = y.shape
    tr = _row_tile(s, d)

    def body(y_ref, t_ref, dy_ref, l_ref):
        e = y_ref[...] - t_ref[...]
        dy_ref[...] = e * (1.0 / d)
        part = _sum_to_sublanes(e * e) * (0.5 / d)

        @pl.when(pl.program_id(0) == 0)
        def _():
            l_ref[...] = part

        @pl.when(pl.program_id(0) > 0)
        def _():
            l_ref[...] += part

    row = pl.BlockSpec((tr, d), lambda i: (i, 0))
    return pl.pallas_call(
        body, name=name, grid=(s // tr,), in_specs=[row, row],
        out_specs=[row, pl.BlockSpec((SUBLANES, d), lambda i: (0, 0))],
        out_shape=[jax.ShapeDtypeStruct((s, d), F32), jax.ShapeDtypeStruct((SUBLANES, d), F32)],
        compiler_params=_params("arbitrary"),
    )(y, target)


def _ffn_up(name, h, gu_w):
    s, d = h.shape
    nb, _, fs = gu_w.shape
    hb = nb // 2
    w = gu_w.reshape(2, hb, d, fs)
    tm = _tile(s, 512)
    tn = _tile(fs, 1408)
    nj = fs // tn

    def body(h_ref, w_ref, gu_ref, a_ref):
        hv = h_ref[...]
        g = _dot(hv, w_ref[0], "nn")
        u = _dot(hv, w_ref[1], "nn")
        gu_ref[0] = g.astype(gu_ref.dtype)
        gu_ref[1] = u.astype(gu_ref.dtype)
        a_ref[...] = (g * jax.nn.sigmoid(g) * u).astype(a_ref.dtype)

    return pl.pallas_call(
        body, name=name, grid=(hb, nj, s // tm),
        in_specs=[pl.BlockSpec((tm, d), lambda jb, jo, i: (i, 0)),
                  pl.BlockSpec((2, None, d, tn), lambda jb, jo, i: (0, jb, 0, jo))],
        out_specs=[pl.BlockSpec((2, None, tm, tn), lambda jb, jo, i: (0, jb, i, jo)),
                   pl.BlockSpec((tm, tn), lambda jb, jo, i: (i, jb * nj + jo))],
        out_shape=[jax.ShapeDtypeStruct((2, hb, s, fs), BF16), jax.ShapeDtypeStruct((s, hb * fs), BF16)],
        compiler_params=_params("parallel", "parallel", "parallel"),
    )(h, w)


def _ffn_dact(name, dy, dn_w, gu, deps=()):
    s, d = dy.shape
    _, hb, _, fs = gu.shape
    tm = _tile(s, 512)
    tn = _tile(fs, 1408)
    nj = fs // tn

    def body(dy_ref, w_ref, gu_ref, *rest):
        o_ref = rest[-1]
        da = _dot(dy_ref[...], w_ref[...], "nt")
        g = gu_ref[0].astype(F32)
        u = gu_ref[1].astype(F32)
        sg = jax.nn.sigmoid(g)
        o_ref[0] = (da * u * (sg * (1.0 + g * (1.0 - sg)))).astype(o_ref.dtype)
        o_ref[1] = (da * (g * sg)).astype(o_ref.dtype)

    blk = pl.BlockSpec((2, None, tm, tn), lambda jb, jo, i: (0, jb, i, jo))
    return pl.pallas_call(
        body, name=name, grid=(hb, nj, s // tm),
        in_specs=[pl.BlockSpec((tm, d), lambda jb, jo, i: (i, 0)),
                  pl.BlockSpec((tn, d), lambda jb, jo, i: (jb * nj + jo, 0)),
                  blk] + [ANY_SPEC] * len(deps),
        out_specs=blk, out_shape=jax.ShapeDtypeStruct(gu.shape, BF16),
        compiler_params=_params("parallel", "parallel", "parallel"),
    )(dy, dn_w, gu, *deps)


def _multiplicity(q0, tq, s):
    row = q0 + lax.broadcasted_iota(jnp.int32, (tq, s), 0)
    col = lax.broadcasted_iota(jnp.int32, (tq, s), 1)
    dist = row - col
    mult = jnp.zeros((tq, s), F32)
    for window, dilation in DILATED_BRANCHES:
        hit = (dist <= window) & ((dist & (dilation - 1)) == 0)
        mult = mult + hit.astype(F32)
    return jnp.where(dist >= 0, mult, 0.0)


_MASKED = -1e30


def _attn_specs(s, qd, kvd, tq):
    rw = Q_PER_KV * HEAD_DIM
    qspec = pl.BlockSpec((tq, rw), lambda g, i: (i, g))
    kspec = pl.BlockSpec((s, HEAD_DIM), lambda g, i: (0, qd // HEAD_DIM + g))
    vspec = pl.BlockSpec((s, HEAD_DIM), lambda g, i: (0, (qd + kvd) // HEAD_DIM + g))
    return rw, qspec, kspec, vspec


def _attn_fwd(name, z, qd, kvd):
    s = z.shape[0]
    tq = _tile(s, 256)
    nkv = kvd // HEAD_DIM
    rw, qspec, kspec, vspec = _attn_specs(s, qd, kvd, tq)
    scale = HEAD_DIM ** -0.5

    def body(q_ref, k_ref, v_ref, o_ref, l_ref):
        mult = _multiplicity(pl.program_id(1) * tq, tq, s)
        live = mult > 0.0
        kv, vv = k_ref[...], v_ref[...]
        for h in range(Q_PER_KV):
            cols = slice(h * HEAD_DIM, (h + 1) * HEAD_DIM)
            sc = jnp.where(live, _dot(q_ref[:, cols], kv, "nt") * scale, _MASKED)
            mx = jnp.max(sc, axis=-1, keepdims=True)
            p = jnp.exp(sc - mx) * mult
            den = jnp.sum(p, axis=-1, keepdims=True)
            o_ref[:, cols] = _dot(p.astype(BF16), vv, "nn") / den
            l_ref[:, cols] = jnp.broadcast_to(mx + jnp.log(den), (tq, HEAD_DIM))

    return pl.pallas_call(
        body, name=name, grid=(nkv, s // tq), in_specs=[qspec, kspec, vspec], out_specs=[qspec, qspec],
        out_shape=[jax.ShapeDtypeStruct((s, qd), F32), jax.ShapeDtypeStruct((s, qd), F32)],
        compiler_params=_params("parallel", "parallel"),
    )(z, z, z)


def _attn_bwd(name, z, o, lse, do, qd, kvd):
    s = z.shape[0]
    tq = _tile(s, 256)
    nkv = kvd // HEAD_DIM
    nq = s // tq
    rw, qspec, kspec, vspec = _attn_specs(s, qd, kvd, tq)
    scale = HEAD_DIM ** -0.5

    def body(q_ref, k_ref, v_ref, o_ref, l_ref, do_ref, dq_ref, dk_ref, dv_ref, dk_acc, dv_acc):
        i = pl.program_id(1)
        mult = _multiplicity(i * tq, tq, s)
        live = mult > 0.0
        kv, vv = k_ref[...], v_ref[...]

        @pl.when(i == 0)
        def _():
            dk_acc[...] = jnp.zeros_like(dk_acc)
            dv_acc[...] = jnp.zeros_like(dv_acc)

        for h in range(Q_PER_KV):
            cols = slice(h * HEAD_DIM, (h + 1) * HEAD_DIM)
            q = q_ref[:, cols]
            dov = do_ref[:, cols]
            sc = jnp.where(live, _dot(q, kv, "nt") * scale, _MASKED)
            p = jnp.exp(sc - l_ref[:, cols][:, :1]) * mult
            dob = dov.astype(BF16)
            dp = _dot(dob, vv, "nt")
            delta = jnp.sum(dov * o_ref[:, cols], axis=-1, keepdims=True)
            ds = (p * (dp - delta) * scale).astype(BF16)
            dq_ref[:, cols] = _dot(ds, kv, "nn").astype(dq_ref.dtype)
            dk_acc[...] += _dot(ds, q, "tn")
            dv_acc[...] += _dot(p.astype(BF16), dob, "tn")

        @pl.when(i == nq - 1)
        def _():
            dk_ref[...] = dk_acc[...].astype(dk_ref.dtype)
            dv_ref[...] = dv_acc[...].astype(dv_ref.dtype)

    kvout = pl.BlockSpec((s, HEAD_DIM), lambda g, i: (0, g))
    return pl.pallas_call(
        body, name=name, grid=(nkv, nq), in_specs=[qspec, kspec, vspec, qspec, qspec, qspec],
        out_specs=[qspec, kvout, kvout],
        out_shape=[jax.ShapeDtypeStruct((s, qd), BF16), jax.ShapeDtypeStruct((s, kvd), BF16),
                   jax.ShapeDtypeStruct((s, kvd), BF16)],
        scratch_shapes=[pltpu.VMEM((s, HEAD_DIM), F32), pltpu.VMEM((s, HEAD_DIM), F32)],
        compiler_params=_params("parallel", "arbitrary"),
    )(z, z, z, o, lse, do)


def _shift_down(v, n):
    rolled = pltpu.roll(v, n, 0)
    t = lax.broadcasted_iota(jnp.int32, v.shape, 0)
    return jnp.where(t >= n, rolled, 0.0)


def _shift_up(v, n):
    rows = v.shape[0]
    rolled = pltpu.roll(v, rows - n, 0)
    t = lax.broadcasted_iota(jnp.int32, v.shape, 0)
    return jnp.where(t < rows - n, rolled, 0.0)


def _conv_specs(s, base, cd, tc):
    zs = [pl.BlockSpec((s, tc), functools.partial(lambda j, off: (0, off + j), off=(base + n * cd) // tc))
          for n in range(3)]
    wspec = pl.BlockSpec((SUBLANES, tc), lambda j: (0, j))
    cspec = pl.BlockSpec((s, tc), lambda j: (0, j))
    return zs, wspec, cspec


def _conv_fwd(name, z, conv_w, base, cd):
    s = z.shape[0]
    tc = _tile(cd, 256)
    zs, wspec, cspec = _conv_specs(s, base, cd, tc)

    def body(h_ref, b_ref, c_ref, w_ref, o_ref):
        u = c_ref[...].astype(F32) * h_ref[...].astype(F32)
        y = w_ref[0:1, :] * _shift_down(u, 2) + w_ref[1:2, :] * _shift_down(u, 1) + w_ref[2:3, :] * u
        o_ref[...] = b_ref[...].astype(F32) * y

    return pl.pallas_call(
        body, name=name, grid=(cd // tc,), in_specs=zs + [wspec], out_specs=cspec,
        out_shape=jax.ShapeDtypeStruct((s, cd), F32), compiler_params=_params("parallel"),
    )(z, z, z, conv_w)


def _conv_bwd(name, z, conv_w, dc, base, cd):
    s = z.shape[0]
    tc = _tile(cd, 256)
    zs, wspec, cspec = _conv_specs(s, base, cd, tc)

    def body(h_ref, b_ref, c_ref, w_ref, dc_ref, dh_ref, db_ref, dcg_ref, dw_ref):
        hv, bv, cv = h_ref[...].astype(F32), b_ref[...].astype(F32), c_ref[...].astype(F32)
        u = cv * hv
        u1, u2 = _shift_down(u, 1), _shift_down(u, 2)
        w0, w1, w2 = w_ref[0:1, :], w_ref[1:2, :], w_ref[2:3, :]
        y = w0 * u2 + w1 * u1 + w2 * u
        dcv = dc_ref[...]
        db_ref[...] = (dcv * y).astype(db_ref.dtype)
        dy = dcv * bv
        du = w2 * dy + w1 * _shift_up(dy, 1) + w0 * _shift_up(dy, 2)
        dh_ref[...] = (du * cv).astype(dh_ref.dtype)
        dcg_ref[...] = (du * hv).astype(dcg_ref.dtype)
        g0 = jnp.sum(dy * u2, axis=0, keepdims=True)
        g1 = jnp.sum(dy * u1, axis=0, keepdims=True)
        g2 = jnp.sum(dy * u, axis=0, keepdims=True)
        r = lax.broadcasted_iota(jnp.int32, (SUBLANES, tc), 0)
        dw_ref[...] = jnp.where(r == 0, g0, jnp.where(r == 1, g1, jnp.where(r == 2, g2, 0.0)))

    return pl.pallas_call(
        body, name=name, grid=(cd // tc,), in_specs=zs + [wspec, cspec],
        out_specs=[cspec, cspec, cspec, wspec],
        out_shape=[jax.ShapeDtypeStruct((s, cd), BF16)] * 3 + [jax.ShapeDtypeStruct((SUBLANES, cd), F32)],
        compiler_params=_params("parallel"),
    )(z, z, z, conv_w, dc)


def _cat_norm_fwd(name, a, c, ga, gc):
    s, qd = a.shape
    cd = c.shape[1]
    tr = _row_tile(s, qd + cd)

    def body(a_ref, c_ref, ga_ref, gc_ref, o_ref):
        av, cv = a_ref[...], c_ref[...]
        ra = lax.rsqrt(jnp.mean(av * av, axis=-1, keepdims=True) + NORM_EPS)
        rc = lax.rsqrt(jnp.mean(cv * cv, axis=-1, keepdims=True) + NORM_EPS)
        o_ref[:, :qd] = (av * ra * ga_ref[...]).astype(o_ref.dtype)
        o_ref[:, qd:] = (cv * rc * gc_ref[...]).astype(o_ref.dtype)

    return pl.pallas_call(
        body, name=name, grid=(s // tr,),
        in_specs=[pl.BlockSpec((tr, qd), lambda i: (i, 0)), pl.BlockSpec((tr, cd), lambda i: (i, 0)),
                  pl.BlockSpec((1, qd), lambda i: (0, 0)), pl.BlockSpec((1, cd), lambda i: (0, 0))],
        out_specs=pl.BlockSpec((tr, qd + cd), lambda i: (i, 0)),
        out_shape=jax.ShapeDtypeStruct((s, qd + cd), BF16), compiler_params=_params("parallel"),
    )(a, c, ga, gc)


def _cat_norm_bwd(name, dcat, a, c, ga, gc):
    s, qd = a.shape
    cd = c.shape[1]
    tr = _row_tile(s, qd + cd)

    def one(dn, yv, gv):
        r = lax.rsqrt(jnp.mean(yv * yv, axis=-1, keepdims=True) + NORM_EPS)
        xhat = yv * r
        dxn = dn * gv
        return r * (dxn - xhat * jnp.mean(dxn * xhat, axis=-1, keepdims=True)), _sum_to_sublanes(dn * xhat)

    def body(d_ref, a_ref, c_ref, ga_ref, gc_ref, da_ref, dc_ref, dga_ref, dgc_ref):
        da, pa = one(d_ref[:, :qd], a_ref[...], ga_ref[...])
        dc, pc = one(d_ref[:, qd:], c_ref[...], gc_ref[...])
        da_ref[...] = da
        dc_ref[...] = dc

        @pl.when(pl.program_id(0) == 0)
        def _():
            dga_ref[...] = pa
            dgc_ref[...] = pc

        @pl.when(pl.program_id(0) > 0)
        def _():
            dga_ref[...] += pa
            dgc_ref[...] += pc

    ra = pl.BlockSpec((tr, qd), lambda i: (i, 0))
    rc = pl.BlockSpec((tr, cd), lambda i: (i, 0))
    return pl.pallas_call(
        body, name=name, grid=(s // tr,),
        in_specs=[pl.BlockSpec((tr, qd + cd), lambda i: (i, 0)), ra, rc,
                  pl.BlockSpec((1, qd), lambda i: (0, 0)), pl.BlockSpec((1, cd), lambda i: (0, 0))],
        out_specs=[ra, rc, pl.BlockSpec((SUBLANES, qd), lambda i: (0, 0)),
                   pl.BlockSpec((SUBLANES, cd), lambda i: (0, 0))],
        out_shape=[jax.ShapeDtypeStruct((s, qd), F32), jax.ShapeDtypeStruct((s, cd), F32),
                   jax.ShapeDtypeStruct((SUBLANES, qd), F32), jax.ShapeDtypeStruct((SUBLANES, cd), F32)],
        compiler_params=_params("arbitrary"),
    )(dcat, a, c, ga, gc)


def _adamw(name, w, g, m, v):
    shape = w.shape
    cols = shape[-1]
    rows = w.size // cols
    tr = _row_tile(rows, cols, budget=3 << 19)
    bc1 = 1.0 - ADAM_B1 ** ADAM_STEP
    bc2 = 1.0 - ADAM_B2 ** ADAM_STEP

    def body(w_ref, g_ref, m_ref, v_ref, d_ref, nm_ref, nv_ref):
        gv = g_ref[...]
        mv = ADAM_B1 * m_ref[...] + (1.0 - ADAM_B1) * gv
        vv = ADAM_B2 * v_ref[...] + (1.0 - ADAM_B2) * (gv * gv)
        nm_ref[...] = mv
        nv_ref[...] = vv
        d_ref[...] = -ADAM_LR * ((mv / bc1) / (jnp.sqrt(vv / bc2) + ADAM_EPS) + ADAM_WD * w_ref[...])

    row = pl.BlockSpec((tr, cols), lambda i: (i, 0))
    outs = pl.pallas_call(
        body, name=name, grid=(rows // tr,), in_specs=[row] * 4, out_specs=[row] * 3,
        out_shape=[jax.ShapeDtypeStruct((rows, cols), F32)] * 3, compiler_params=_params("parallel"),
    )(*(t.reshape(rows, cols) for t in (w, g, m, v)))
    return tuple(t.reshape(shape) for t in outs)


HBM_SPEC = pl.BlockSpec(memory_space=pltpu.HBM)


def _mesh_place():
    x, y, c = lax.axis_index("x"), lax.axis_index("y"), lax.axis_index("c")
    other_chips = [(1 - x, y), (x, 1 - y), (1 - x, 1 - y)]
    return x, y, c, other_chips


def _cast_into_slot(name, w, layer, chip):
    _, r, cols = w.shape
    tr = _row_tile(r, cols)

    def body(chip_ref, w_ref, o_ref):
        o_ref[...] = w_ref[...].astype(o_ref.dtype)

    return pl.pallas_call(
        body, name=name,
        grid_spec=pltpu.PrefetchScalarGridSpec(
            num_scalar_prefetch=1, grid=(r // tr,),
            in_specs=[pl.BlockSpec((None, tr, cols), lambda i, chip_ref: (layer, i, 0))],
            out_specs=pl.BlockSpec((None, tr, cols), lambda i, chip_ref: (chip_ref[0], i, 0))),
        out_shape=jax.ShapeDtypeStruct((N_CHIPS, r, cols), BF16), compiler_params=_params("parallel"),
    )(chip, w)


SEM_SPEC = pl.BlockSpec(memory_space=pltpu.SEMAPHORE)
SPLIT_COPY = pltpu.CompilerParams(has_side_effects=pltpu.SideEffectType.DATAFLOW_SIDE_EFFECTING)
N_OTHER = N_CHIPS - 1
TOKEN_SPEC = pl.BlockSpec(memory_space=pltpu.VMEM)
TOKEN_SHAPE = jax.ShapeDtypeStruct((SUBLANES, LANES), F32)


def _in_hbm(arr):
    return pltpu.with_memory_space_constraint(arr, pltpu.HBM)


def _half_rows(ref, chip_idx, core):
    r2 = ref.shape[1] // 2
    return ref.at[chip_idx, pl.ds(core * r2, r2), :]


def _gather_start(fulls, after):
    na = len(fulls)

    def body(*refs):
        f_refs = refs[na + 1:2 * na + 1]
        send_sems, recv_sems = refs[2 * na + 1:3 * na + 1], refs[3 * na + 1:4 * na + 1]
        x, y, c, chips = _mesh_place()
        for a in range(na):
            mine = _half_rows(f_refs[a], 2 * x + y, c)
            for j, (cx, cy) in enumerate(chips):
                pltpu.make_async_remote_copy(
                    src_ref=mine, dst_ref=mine, send_sem=send_sems[a].at[j], recv_sem=recv_sems[a].at[j],
                    device_id=(cx, cy, c), device_id_type=MESH).start()

    outs = pl.pallas_call(
        body, name="gather_start", in_specs=[HBM_SPEC] * na + [ANY_SPEC],
        out_specs=[HBM_SPEC] * na + [SEM_SPEC] * (2 * na),
        out_shape=[pltpu.HBM(f.shape, f.dtype) for f in fulls] + [pltpu.SemaphoreType.DMA((N_OTHER,))] * (2 * na),
        input_output_aliases={a: a for a in range(na)}, compiler_params=SPLIT_COPY,
    )(*[_in_hbm(f) for f in fulls], after)
    return outs[:na], outs[na:2 * na], outs[2 * na:]


def _gather_pass_on(name, full, recv_sems, after):
    def body(f_in, recv_sems, after_ref, f_ref, d2d_send, d2d_recv):
        x, y, c, chips = _mesh_place()
        for j, (cx, cy) in enumerate(chips):
            blk = _half_rows(f_ref, 2 * cx + cy, c)
            pltpu.make_async_remote_copy(
                src_ref=blk, dst_ref=blk, send_sem=d2d_send.at[j], recv_sem=recv_sems.at[j],
                device_id=(cx, cy, c), device_id_type=MESH).wait_recv()
            pltpu.make_async_remote_copy(
                src_ref=blk, dst_ref=blk, send_sem=d2d_send.at[j], recv_sem=d2d_recv.at[j],
                device_id=(x, y, 1 - c), device_id_type=MESH).start()

    return pl.pallas_call(
        body, name=name, in_specs=[HBM_SPEC, SEM_SPEC, ANY_SPEC], out_specs=[HBM_SPEC, SEM_SPEC, SEM_SPEC],
        out_shape=[pltpu.HBM(full.shape, full.dtype)] + [pltpu.SemaphoreType.DMA((N_OTHER,))] * 2,
        input_output_aliases={0: 0}, compiler_params=SPLIT_COPY,
    )(full, recv_sems, after)


def _gather_arrive(name, full, ici_send, d2d_send, d2d_recv, after):
    def body(f_in, ici_send, d2d_send, d2d_recv, after_ref, f_ref):
        x, y, c, chips = _mesh_place()
        for j, (cx, cy) in enumerate(chips):
            mine = _half_rows(f_ref, 2 * x + y, c)
            passed = _half_rows(f_ref, 2 * cx + cy, c)
            theirs = _half_rows(f_ref, 2 * cx + cy, 1 - c)
            pltpu.make_async_remote_copy(
                src_ref=mine, dst_ref=mine, send_sem=ici_send.at[j], recv_sem=d2d_recv.at[j],
                device_id=(cx, cy, c), device_id_type=MESH).wait_send()
            pltpu.make_async_remote_copy(
                src_ref=passed, dst_ref=passed, send_sem=d2d_send.at[j], recv_sem=d2d_recv.at[j],
                device_id=(x, y, 1 - c), device_id_type=MESH).wait_send()
            pltpu.make_async_remote_copy(
                src_ref=theirs, dst_ref=theirs, send_sem=d2d_send.at[j], recv_sem=d2d_recv.at[j],
                device_id=(x, y, 1 - c), device_id_type=MESH).wait_recv()

    return pl.pallas_call(
        body, name=name, in_specs=[HBM_SPEC, SEM_SPEC, SEM_SPEC, SEM_SPEC, ANY_SPEC], out_specs=HBM_SPEC,
        out_shape=pltpu.HBM(full.shape, full.dtype), input_output_aliases={0: 0}, compiler_params=SPLIT_COPY,
    )(full, ici_send, d2d_send, d2d_recv, after)


def _gather_taps(conv_w):
    def body(cw_ref, cwf_ref, send_sems, recv_sems, local_sem):
        x, y, c, chips = _mesh_place()
        k_me = 2 * x + y
        local = pltpu.make_async_copy(cw_ref, cwf_ref.at[k_me], local_sem)
        local.start()
        copies = [pltpu.make_async_remote_copy(
            src_ref=cw_ref, dst_ref=cwf_ref.at[k_me], send_sem=send_sems.at[j], recv_sem=recv_sems.at[j],
            device_id=(cx, cy, c), device_id_type=MESH) for j, (cx, cy) in enumerate(chips)]
        for cp in copies:
            cp.start()
        for j, (cx, cy) in enumerate(chips):
            pltpu.make_async_remote_copy(
                src_ref=cw_ref, dst_ref=cwf_ref.at[2 * cx + cy], send_sem=send_sems.at[j], recv_sem=recv_sems.at[j],
                device_id=(cx, cy, c), device_id_type=MESH).wait_recv()
        for cp in copies:
            cp.wait_send()
        local.wait()

    return pl.pallas_call(
        body, name="gather_taps", in_specs=[HBM_SPEC], out_specs=HBM_SPEC,
        out_shape=jax.ShapeDtypeStruct((N_CHIPS,) + conv_w.shape, conv_w.dtype),
        scratch_shapes=[pltpu.SemaphoreType.DMA((N_OTHER,))] * 2 + [pltpu.SemaphoreType.DMA],
    )(conv_w)


def _sibling_half(g_ref, c):
    r2 = g_ref.shape[1] // 2
    return g_ref.at[:, pl.ds((1 - c) * r2, r2), :]


def _swap_start(name, g):
    def body(g_in, g_ref, land_ref, send_sem, recv_sem, token):
        x, y, c, _ = _mesh_place()
        pltpu.make_async_remote_copy(
            src_ref=_sibling_half(g_ref, c), dst_ref=land_ref, send_sem=send_sem, recv_sem=recv_sem,
            device_id=(x, y, 1 - c), device_id_type=MESH).start()
        token[...] = jnp.zeros_like(token)

    nb, r, cols = g.shape
    return pl.pallas_call(
        body, name=name, in_specs=[HBM_SPEC], out_specs=[HBM_SPEC, HBM_SPEC, SEM_SPEC, SEM_SPEC, TOKEN_SPEC],
        out_shape=[pltpu.HBM(g.shape, g.dtype), pltpu.HBM((nb, r // 2, cols), g.dtype),
                   pltpu.SemaphoreType.DMA(()), pltpu.SemaphoreType.DMA(()), TOKEN_SHAPE],
        input_output_aliases={0: 0}, compiler_params=SPLIT_COPY,
    )(_in_hbm(g))


def _swap_wait(name, g, land, send_sem, recv_sem, after):
    def body(g_in, land_in, send_sem, recv_sem, after_ref, g_ref, land_ref):
        x, y, c, _ = _mesh_place()
        copy = pltpu.make_async_remote_copy(
            src_ref=_sibling_half(g_ref, c), dst_ref=land_ref, send_sem=send_sem, recv_sem=recv_sem,
            device_id=(x, y, 1 - c), device_id_type=MESH)
        copy.wait_send()
        copy.wait_recv()

    return pl.pallas_call(
        body, name=name, in_specs=[HBM_SPEC, HBM_SPEC, SEM_SPEC, SEM_SPEC, ANY_SPEC], out_specs=[HBM_SPEC, HBM_SPEC],
        out_shape=[pltpu.HBM(g.shape, g.dtype), pltpu.HBM(land.shape, land.dtype)],
        input_output_aliases={0: 0, 1: 1}, compiler_params=SPLIT_COPY,
    )(g, land, send_sem, recv_sem, after)


def _add_core_halves(name, g, sib, core):
    nb, r, cols = g.shape
    r2 = r // 2
    tr = _row_tile(r2, cols, itemsize=2, budget=1 << 20)
    nrt = r2 // tr

    def body(core_ref, g_ref, s_ref, o_ref):
        o_ref[...] = (g_ref[...].astype(F32) + s_ref[...].astype(F32)).astype(o_ref.dtype)

    return pl.pallas_call(
        body, name=name,
        grid_spec=pltpu.PrefetchScalarGridSpec(
            num_scalar_prefetch=1, grid=(nb, nrt),
            in_specs=[pl.BlockSpec((None, tr, cols), lambda k, i, core_ref: (k, core_ref[0] * nrt + i, 0)),
                      pl.BlockSpec((None, tr, cols), lambda k, i, core_ref: (k, i, 0))],
            out_specs=pl.BlockSpec((None, tr, cols), lambda k, i, core_ref: (k, i, 0))),
        out_shape=jax.ShapeDtypeStruct((nb, r2, cols), BF16), compiler_params=_params("parallel", "parallel"),
    )(core, g, sib)


def _scatter_copy(h_ref, land_ref, send_sems, recv_sems, j, chip_xy, c):
    cx, cy = chip_xy
    return pltpu.make_async_remote_copy(
        src_ref=h_ref.at[2 * cx + cy], dst_ref=land_ref.at[j], send_sem=send_sems.at[j], recv_sem=recv_sems.at[j],
        device_id=(cx, cy, c), device_id_type=MESH)


def _scatter_start(name, h):
    def body(h_in, h_ref, land_ref, send_sems, recv_sems, token):
        x, y, c, chips = _mesh_place()
        for j, chip_xy in enumerate(chips):
            _scatter_copy(h_ref, land_ref, send_sems, recv_sems, j, chip_xy, c).start()
        token[...] = jnp.zeros_like(token)

    return pl.pallas_call(
        body, name=name, in_specs=[HBM_SPEC], out_specs=[HBM_SPEC, HBM_SPEC, SEM_SPEC, SEM_SPEC, TOKEN_SPEC],
        out_shape=[pltpu.HBM(h.shape, h.dtype), pltpu.HBM((N_OTHER,) + h.shape[1:], h.dtype),
                   pltpu.SemaphoreType.DMA((N_OTHER,)), pltpu.SemaphoreType.DMA((N_OTHER,)), TOKEN_SHAPE],
        input_output_aliases={0: 0}, compiler_params=SPLIT_COPY,
    )(_in_hbm(h))


def _scatter_wait(name, h, land, send_sems, recv_sems, after):
    afters = tuple(after) if isinstance(after, (tuple, list)) else (after,)

    def body(h_in, land_in, send_sems, recv_sems, *rest):
        h_ref, land_ref = rest[-2:]
        x, y, c, chips = _mesh_place()
        for j, chip_xy in enumerate(chips):
            copy = _scatter_copy(h_ref, land_ref, send_sems, recv_sems, j, chip_xy, c)
            copy.wait_send()
            copy.wait_recv()

    return pl.pallas_call(
        body, name=name, in_specs=[HBM_SPEC, HBM_SPEC, SEM_SPEC, SEM_SPEC] + [ANY_SPEC] * len(afters),
        out_specs=[HBM_SPEC, HBM_SPEC],
        out_shape=[pltpu.HBM(h.shape, h.dtype), pltpu.HBM(land.shape, land.dtype)],
        input_output_aliases={0: 0, 1: 1}, compiler_params=SPLIT_COPY,
    )(h, land, send_sems, recv_sems, *afters)


def _sum_chips(name, hs, rcv, core, chip, layer, n_layers, prev):
    _, r2, cols = hs.shape
    tr = _row_tile(r2, cols, budget=1 << 20)
    nrt = r2 // tr

    def body(core_ref, chip_ref, h_ref, r_ref, *rest):
        o_ref = rest[-1]
        acc = h_ref[...].astype(F32)
        for j in range(N_CHIPS - 1):
            acc = acc + r_ref[j].astype(F32)
        o_ref[...] = acc

    in_specs = [pl.BlockSpec((None, tr, cols), lambda i, core_ref, chip_ref: (chip_ref[0], i, 0)),
                pl.BlockSpec((N_CHIPS - 1, tr, cols), lambda i, core_ref, chip_ref: (0, i, 0))]
    args = [core, chip, hs, rcv]
    aliases = {}
    if prev is not None:
        in_specs.append(pl.BlockSpec(memory_space=pl.ANY))
        args.append(prev)
        aliases = {4: 0}
    return pl.pallas_call(
        body, name=name,
        grid_spec=pltpu.PrefetchScalarGridSpec(
            num_scalar_prefetch=2, grid=(nrt,), in_specs=in_specs,
            out_specs=pl.BlockSpec((None, tr, cols), lambda i, core_ref, chip_ref: (layer, core_ref[0] * nrt + i, 0))),
        out_shape=jax.ShapeDtypeStruct((n_layers, 2 * r2, cols), F32), input_output_aliases=aliases,
        compiler_params=_params("parallel"),
    )(*args)


def _join_core_halves(name, ts, deps=()):
    na, nd = len(ts), len(deps)

    def body(*refs):
        o_refs = refs[na + nd:2 * na + nd]
        send_sems, recv_sems = refs[2 * na + nd:]
        x, y, c, _ = _mesh_place()
        copies = []
        for a in range(na):
            r2 = o_refs[a].shape[1] // 2
            mine = o_refs[a].at[:, pl.ds(c * r2, r2), :]
            copies.append(pltpu.make_async_remote_copy(
                src_ref=mine, dst_ref=mine, send_sem=send_sems.at[a], recv_sem=recv_sems.at[a],
                device_id=(x, y, 1 - c), device_id_type=MESH))
        for cp in copies:
            cp.start()
        for cp in copies:
            cp.wait()

    return pl.pallas_call(
        body, name=name, in_specs=[HBM_SPEC] * na + [ANY_SPEC] * nd, out_specs=[HBM_SPEC] * na,
        out_shape=[jax.ShapeDtypeStruct(t.shape, t.dtype) for t in ts],
        input_output_aliases={a: a for a in range(na)},
        scratch_shapes=[pltpu.SemaphoreType.DMA((na,))] * 2,
    )(*ts, *deps)


def _allreduce_small(p):
    n, _, w = p.shape

    def body(p_ref, o_ref, buf, send_sems, recv_sems):
        x, y, c, _ = _mesh_place()
        me = 4 * x + 2 * y + c
        buf[me] = jnp.sum(p_ref[...], axis=1)
        copies = []
        for pat in range(1, N_DEV):
            fx, fy, fc = (pat >> 2) & 1, (pat >> 1) & 1, pat & 1
            copies.append(pltpu.make_async_remote_copy(
                src_ref=buf.at[me], dst_ref=buf.at[me], send_sem=send_sems.at[pat - 1], recv_sem=recv_sems.at[pat - 1],
                device_id=(x ^ fx, y ^ fy, c ^ fc), device_id_type=MESH))
        for cp in copies:
            cp.start()
        for cp in copies:
            cp.wait()
        acc = buf[0]
        for dev in range(1, N_DEV):
            acc = acc + buf[dev]
        o_ref[...] = acc

    return pl.pallas_call(
        body, name="allreduce_small", in_specs=[pl.BlockSpec(memory_space=pltpu.VMEM)],
        out_specs=pl.BlockSpec(memory_space=pltpu.VMEM), out_shape=jax.ShapeDtypeStruct((n, w), F32),
        scratch_shapes=[pltpu.VMEM((N_DEV, n, w), F32), pltpu.SemaphoreType.DMA((N_DEV - 1,)),
                        pltpu.SemaphoreType.DMA((N_DEV - 1,))],
    )(p)


class _WeightFeed:
    def __init__(self, fulls, after):
        self.fulls, self.ici_send, self.ici_recv = (list(t) for t in _gather_start(fulls, after))
        self.d2d = [None] * len(fulls)

    def _pass_on(self, k, after):
        if k < len(self.fulls) and self.d2d[k] is None:
            self.fulls[k], send, recv = _gather_pass_on(f"gather_pass_{k}", self.fulls[k], self.ici_recv[k], after)
            self.d2d[k] = (send, recv)

    def take(self, k, after):
        self._pass_on(k, after)
        self._pass_on(k + 1, after)
        if k + 1 < len(self.fulls):
            after = self.fulls[k + 1]
        self.fulls[k] = _gather_arrive(f"gather_arrive_{k}", self.fulls[k], self.ici_send[k], *self.d2d[k], after)
        return self.fulls[k]


def _ffn_forward(tag, x, g_pre, g_post, feed, k):
    s, d = x.shape
    h = _norm_fwd(f"{tag}_norm", x, g_pre)
    gu_w = feed.take(k, h)
    gu, a = _ffn_up(f"{tag}_up", h, gu_w)
    dn_w = feed.take(k + 1, a).reshape(-1, d)
    f = dn_w.shape[0]
    tk = _tile(f, 1408)
    tm, tn = _tile(s, 1024), _tile(d, 1024)
    y = _mm(f"{tag}_down", a, dn_w, mode="nn", grid=(s // tm, d // tn, f // tk),
            a_spec=pl.BlockSpec((tm, tk), lambda i, j, k: (i, k)),
            b_spec=pl.BlockSpec((tk, tn), lambda i, j, k: (k, j)),
            o_spec=pl.BlockSpec((tm, tn), lambda i, j, k: (i, j)),
            out_shape=jax.ShapeDtypeStruct((s, d), F32), nk=f // tk, acc_shape=(tm, tn))
    x_new = _res_norm(f"{tag}_post", x, y, g_post, FFN_RESIDUAL_WEIGHT)
    return x_new, (x, h, gu, a, y)


class _GradReduce:
    def __init__(self, core, chip, n_layers):
        self.core, self.chip, self.n_layers = core, chip, n_layers
        self.state = {}
        self.bufs = {}

    def start(self, kind, layer, g):
        g, land, send, recv, token = _swap_start(f"swap_start_{kind}_{layer}", g)
        self.state[kind, layer] = (g, land, send, recv)
        return token

    def exchange(self, kind, layer, after):
        tag = f"{kind}_{layer}"
        g, sib = _swap_wait(f"swap_wait_{tag}", *self.state[kind, layer], after)
        h = _add_core_halves(f"add_cores_{tag}", g, sib, self.core)
        h, land, send, recv, token = _scatter_start(f"scatter_start_{tag}", h)
        self.state[kind, layer] = (h, land, send, recv)
        return token

    def finish(self, kind, layer, after):
        tag = f"{kind}_{layer}"
        h, rcv = _scatter_wait(f"scatter_wait_{tag}", *self.state.pop((kind, layer)), after)
        self.bufs[kind] = _sum_chips(f"sum_chips_{tag}", h, rcv, self.core, self.chip, layer, self.n_layers,
                                     self.bufs.get(kind))
        return self.bufs[kind]


def _ffn_backward(tag, dx_new, saved, g_pre, g_post, gu_w, dn_w, red, kinds, layer, deps):
    x, h, gu, a, y = saved
    s, d = x.shape
    nb, fs = gu_w.shape[0], gu_w.shape[2]
    f = dn_w.shape[0]
    fr = f // nb
    dy, dg_post = _norm_bwd(f"{tag}_post_bwd", dx_new, y, g_post, FFN_RESIDUAL_WEIGHT, None, BF16)
    dgu = _ffn_dact(f"{tag}_dact", dy, dn_w, gu, deps)
    dgu4 = dgu.reshape(nb, s, fs)
    tn = _tile(d, 1024)
    d_wd = _mm(f"{tag}_dwd", a, dy, mode="tn", grid=(nb, d // tn),
               a_spec=pl.BlockSpec((s, fr), lambda i, j: (0, i)),
               b_spec=pl.BlockSpec((s, tn), lambda i, j: (0, j)),
               o_spec=pl.BlockSpec((None, fr, tn), lambda i, j: (i, 0, j)),
               out_shape=jax.ShapeDtypeStruct((nb, fr, d), BF16))
    tm, tw = _tile(d, 512), _tile(fs, 1408)
    nw = fs // tw
    d_wgu = _mm(f"{tag}_dwgu", h, dgu4, mode="tn", grid=(nb, nw, d // tm),
                a_spec=pl.BlockSpec((s, tm), lambda k, j, i: (0, i)),
                b_spec=pl.BlockSpec((None, s, tw), lambda k, j, i: (k, 0, j)),
                o_spec=pl.BlockSpec((None, tm, tw), lambda k, j, i: (k, i, j)),
                out_shape=jax.ShapeDtypeStruct((nb, d, fs), BF16))
    started = (red.start(kinds[0], layer, d_wgu), red.start(kinds[1], layer, d_wd))
    ts, td = _tile(s, 1024), _tile(d, 1024)
    dh = _mm(f"{tag}_dh", dgu4, gu_w, mode="nt", grid=(s // ts, d // td, nb),
             a_spec=pl.BlockSpec((None, ts, fs), lambda i, j, k: (k, i, 0)),
             b_spec=pl.BlockSpec((None, td, fs), lambda i, j, k: (k, j, 0)),
             o_spec=pl.BlockSpec((ts, td), lambda i, j, k: (i, j)),
             out_shape=jax.ShapeDtypeStruct((s, d), F32), nk=nb, acc_shape=(ts, td), deps=started)
    dx, dg_pre = _norm_bwd(f"{tag}_pre_bwd", dh, x, g_pre, 1.0, dx_new, F32)
    return dx, dg_pre, dg_post


def _mixer_forward(tag, x, gains, feed, k, conv_taps, dims):
    qd, kvd, cd = dims
    s, d = x.shape
    g_pre, g_a, g_c, g_post = gains
    h = _norm_fwd(f"{tag}_norm", x, g_pre)
    win_w = feed.take(k, h)
    nb, cw = win_w.shape[0], win_w.shape[2]
    tm = _tile(s, 1024)
    z = _mm(f"{tag}_in", h, win_w, mode="nn", grid=(nb, s // tm),
            a_spec=pl.BlockSpec((tm, d), lambda j, i: (i, 0)),
            b_spec=pl.BlockSpec((None, d, cw), lambda j, i: (j, 0, 0)),
            o_spec=pl.BlockSpec((tm, cw), lambda j, i: (i, j)),
            out_shape=jax.ShapeDtypeStruct((s, nb * cw), BF16))
    a, lse = _attn_fwd(f"{tag}_attn", z, qd, kvd)
    c = _conv_fwd(f"{tag}_conv", z, conv_taps, qd + 2 * kvd, cd)
    cat = _cat_norm_fwd(f"{tag}_cat", a, c, g_a, g_c)
    wout_w = feed.take(k + 1, cat).reshape(-1, d)
    mw = qd + cd
    tn = _tile(d, 1024)
    mixed = _mm(f"{tag}_out", cat, wout_w, mode="nn", grid=(s // tm, d // tn),
                a_spec=pl.BlockSpec((tm, mw), lambda i, j: (i, 0)),
                b_spec=pl.BlockSpec((mw, tn), lambda i, j: (0, j)),
                o_spec=pl.BlockSpec((tm, tn), lambda i, j: (i, j)),
                out_shape=jax.ShapeDtypeStruct((s, d), F32))
    x_new = _res_norm(f"{tag}_post", x, mixed, g_post, 1.0)
    return x_new, (x, h, z, a, lse, c, cat, mixed)


def _mixer_backward(tag, dx_new, saved, gains, win_w, conv_taps, wout_w, dims, red, kinds, layer, deps):
    qd, kvd, cd = dims
    x, h, z, a, lse, c, cat, mixed = saved
    s, d = x.shape
    nb, cw = win_w.shape[0], win_w.shape[2]
    g_pre, g_a, g_c, g_post = gains
    mw = qd + cd
    dmixed, dg_post = _norm_bwd(f"{tag}_post_bwd", dx_new, mixed, g_post, 1.0, None, BF16)
    tm, tn = _tile(s, 1024), _tile(mw, 1024)
    dcat = _mm(f"{tag}_dcat", dmixed, wout_w, mode="nt", grid=(s // tm, mw // tn),
               a_spec=pl.BlockSpec((tm, d), lambda i, j: (i, 0)),
               b_spec=pl.BlockSpec((tn, d), lambda i, j: (j, 0)),
               o_spec=pl.BlockSpec((tm, tn), lambda i, j: (i, j)),
               out_shape=jax.ShapeDtypeStruct((s, mw), F32), deps=deps)
    wr = mw // nb
    td = _tile(d, 1024)
    d_wout = _mm(f"{tag}_dwout", cat, dmixed, mode="tn", grid=(nb, d // td),
                 a_spec=pl.BlockSpec((s, wr), lambda i, j: (0, i)),
                 b_spec=pl.BlockSpec((s, td), lambda i, j: (0, j)),
                 o_spec=pl.BlockSpec((None, wr, td), lambda i, j: (i, 0, j)),
                 out_shape=jax.ShapeDtypeStruct((nb, wr, d), BF16))
    da, dc, dg_a, dg_c = _cat_norm_bwd(f"{tag}_cat_bwd", dcat, a, c, g_a, g_c)
    dhc, dbg, dcg, d_taps = _conv_bwd(f"{tag}_conv_bwd", z, conv_taps, dc, qd + 2 * kvd, cd)
    dq, dk, dv = _attn_bwd(f"{tag}_attn_bwd", z, a, lse, da, qd, kvd)
    dz = jnp.concatenate([dq, dk, dv, dhc, dbg, dcg], axis=1)
    th = _tile(d, 512)
    d_win = _mm(f"{tag}_dwin", h, dz, mode="tn", grid=(nb, d // th),
                a_spec=pl.BlockSpec((s, th), lambda k, i: (0, i)),
                b_spec=pl.BlockSpec((s, cw), lambda k, i: (0, k)),
                o_spec=pl.BlockSpec((None, th, cw), lambda k, i: (k, i, 0)),
                out_shape=jax.ShapeDtypeStruct((nb, d, cw), BF16))
    started = (red.start(kinds[0], layer, d_win), red.start(kinds[1], layer, d_wout))
    dh = _mm(f"{tag}_dh", dz, win_w, mode="nt", grid=(s // tm, d // td, nb),
             a_spec=pl.BlockSpec((tm, cw), lambda i, j, k: (i, k)),
             b_spec=pl.BlockSpec((None, td, cw), lambda i, j, k: (k, j, 0)),
             o_spec=pl.BlockSpec((tm, td), lambda i, j, k: (i, j)),
             out_shape=jax.ShapeDtypeStruct((s, d), F32), nk=nb, acc_shape=(tm, td), deps=started)
    dx, dg_pre = _norm_bwd(f"{tag}_pre_bwd", dh, x, g_pre, 1.0, dx_new, F32)
    return dx, d_taps, (dg_pre, dg_a, dg_c, dg_post)


def _pad_cols(v, width):
    return jnp.pad(v, ((0, 0), (0, width - v.shape[1])))


def kernel(x, ffn1_norm_pre, ffn1_w_gate_up, ffn1_w_down, ffn1_norm_post, mix_norm_pre, w_in, conv_w, attn_out_norm, conv_out_norm, w_out, mix_norm_post, ffn2_norm_pre, ffn2_w_gate_up, ffn2_w_down, ffn2_norm_post, loss_target, m_ffn1_norm_pre, m_ffn1_w_gate_up, m_ffn1_w_down, m_ffn1_norm_post, m_mix_norm_pre, m_w_in, m_conv_w, m_attn_out_norm, m_conv_out_norm, m_w_out, m_mix_norm_post, m_ffn2_norm_pre, m_ffn2_w_gate_up, m_ffn2_w_down, m_ffn2_norm_post, v_ffn1_norm_pre, v_ffn1_w_gate_up, v_ffn1_w_down, v_ffn1_norm_post, v_mix_norm_pre, v_w_in, v_conv_w, v_attn_out_norm, v_conv_out_norm, v_w_out, v_mix_norm_post, v_ffn2_norm_pre, v_ffn2_w_gate_up, v_ffn2_w_down, v_ffn2_norm_post):
    _, s, d = x.shape
    n_layers = ffn1_norm_pre.shape[0]
    qd = attn_out_norm.shape[1]
    cd = conv_out_norm.shape[1]
    kvd = qd // Q_PER_KV
    dims = (qd, kvd, cd)
    assert N_CHIPS * w_in.shape[2] == qd + 2 * kvd + 3 * cd and qd + cd == N_CHIPS * w_out.shape[1]
    assert 2 * d <= SMALL_ROWS * LANES * SUBLANES
    chip = 2 * lax.axis_index("x") + lax.axis_index("y")
    chip_arr = chip.astype(jnp.int32).reshape(1)
    core = lax.axis_index("c").astype(jnp.int32).reshape(1)
    kinds = ("gu1", "dn1", "win", "wout", "gu2", "dn2")

    big = (ffn1_w_gate_up, ffn1_w_down, w_in, w_out, ffn2_w_gate_up, ffn2_w_down)
    nk = len(kinds)
    taps_all = _gather_taps(conv_w)
    feed = _WeightFeed([_cast_into_slot(f"cast_{k}_{layer}", w, layer, chip_arr)
                        for layer in range(n_layers) for k, w in zip(kinds, big)], taps_all)
    taps = jnp.transpose(taps_all, (1, 2, 0, 3)).reshape(n_layers, CONV_WIDTH, cd)
    taps = jnp.pad(taps, ((0, 0), (0, SUBLANES - CONV_WIDTH), (0, 0)))

    def gain(g, layer):
        return g[layer][None, :]

    xs = x[0]
    saved = []
    for layer in range(n_layers):
        t = f"l{layer}"
        k0 = layer * nk
        xs, s1 = _ffn_forward(f"{t}_ffn1", xs, gain(ffn1_norm_pre, layer), gain(ffn1_norm_post, layer), feed, k0)
        mix_gains = (gain(mix_norm_pre, layer), gain(attn_out_norm, layer), gain(conv_out_norm, layer), gain(mix_norm_post, layer))
        xs, s2 = _mixer_forward(f"{t}_mix", xs, mix_gains, feed, k0 + 2, taps[layer], dims)
        xs, s3 = _ffn_forward(f"{t}_ffn2", xs, gain(ffn2_norm_pre, layer), gain(ffn2_norm_post, layer), feed, k0 + 4)
        saved.append((s1, s2, s3, mix_gains))
    wts = {k: [feed.fulls[layer * nk + i] for layer in range(n_layers)] for i, k in enumerate(kinds)}
    for k in ("dn1", "wout", "dn2"):
        wts[k] = [w.reshape(-1, d) for w in wts[k]]
    dxs, loss_part = _loss_head("loss_head", xs, loss_target[0])
    loss = lax.psum(jnp.sum(loss_part), ("x", "y", "c"))

    red = _GradReduce(core, chip_arr, n_layers)
    small = [None] * n_layers
    flow = {"deps": (), "in_flight": []}

    def between(dx, new_keys):
        after = dx
        for key in flow["in_flight"]:
            after = red.finish(*key, after)
        flow["deps"] = tuple(red.exchange(*key, after) for key in new_keys)
        flow["in_flight"] = list(new_keys)

    for layer in reversed(range(n_layers)):
        t = f"l{layer}"
        s1, s2, s3, mix_gains = saved[layer]
        dxs, p_pre2, p_post2 = _ffn_backward(
            f"{t}_ffn2", dxs, s3, gain(ffn2_norm_pre, layer), gain(ffn2_norm_post, layer),
            wts["gu2"][layer], wts["dn2"][layer], red, ("gu2", "dn2"), layer, flow["deps"])
        between(dxs, [("gu2", layer), ("dn2", layer)])
        dxs, p_taps, (p_mpre, p_a, p_c, p_mpost) = _mixer_backward(
            f"{t}_mix", dxs, s2, mix_gains, wts["win"][layer], taps[layer], wts["wout"][layer], dims,
            red, ("win", "wout"), layer, flow["deps"])
        between(dxs, [("win", layer), ("wout", layer)])
        dxs, p_pre1, p_post1 = _ffn_backward(
            f"{t}_ffn1", dxs, s1, gain(ffn1_norm_pre, layer), gain(ffn1_norm_post, layer),
            wts["gu1"][layer], wts["dn1"][layer], red, ("gu1", "dn1"), layer, flow["deps"])
        between(dxs, [("gu1", layer), ("dn1", layer)])
        tap_rows = jnp.zeros((CONV_WIDTH, SUBLANES, d), F32).at[:, 0, :cd].set(p_taps[:CONV_WIDTH])
        rows = [p_pre1, p_post1, p_mpre, jnp.concatenate([p_a, p_c], axis=1), p_mpost, p_pre2, p_post2]
        rows = jnp.concatenate([jnp.stack(rows), tap_rows], axis=0)
        small[layer] = jnp.pad(rows, ((0, SMALL_ROWS - rows.shape[0]), (0, 0), (0, 0)))
    grad_x = dxs[None]

    weights = dict(ffn1_norm_pre=ffn1_norm_pre, ffn1_w_gate_up=ffn1_w_gate_up, ffn1_w_down=ffn1_w_down, ffn1_norm_post=ffn1_norm_post, mix_norm_pre=mix_norm_pre, w_in=w_in, conv_w=conv_w, attn_out_norm=attn_out_norm, conv_out_norm=conv_out_norm, w_out=w_out, mix_norm_post=mix_norm_post, ffn2_norm_pre=ffn2_norm_pre, ffn2_w_gate_up=ffn2_w_gate_up, ffn2_w_down=ffn2_w_down, ffn2_norm_post=ffn2_norm_post)
    m_in = dict(ffn1_norm_pre=m_ffn1_norm_pre, ffn1_w_gate_up=m_ffn1_w_gate_up, ffn1_w_down=m_ffn1_w_down, ffn1_norm_post=m_ffn1_norm_post, mix_norm_pre=m_mix_norm_pre, w_in=m_w_in, conv_w=m_conv_w, attn_out_norm=m_attn_out_norm, conv_out_norm=m_conv_out_norm, w_out=m_w_out, mix_norm_post=m_mix_norm_post, ffn2_norm_pre=m_ffn2_norm_pre, ffn2_w_gate_up=m_ffn2_w_gate_up, ffn2_w_down=m_ffn2_w_down, ffn2_norm_post=m_ffn2_norm_post)
    v_in = dict(ffn1_norm_pre=v_ffn1_norm_pre, ffn1_w_gate_up=v_ffn1_w_gate_up, ffn1_w_down=v_ffn1_w_down, ffn1_norm_post=v_ffn1_norm_post, mix_norm_pre=v_mix_norm_pre, w_in=v_w_in, conv_w=v_conv_w, attn_out_norm=v_attn_out_norm, conv_out_norm=v_conv_out_norm, w_out=v_w_out, mix_norm_post=v_mix_norm_post, ffn2_norm_pre=v_ffn2_norm_pre, ffn2_w_gate_up=v_ffn2_w_gate_up, ffn2_w_down=v_ffn2_w_down, ffn2_norm_post=v_ffn2_norm_post)
    kind_name = dict(gu1="ffn1_w_gate_up", dn1="ffn1_w_down", win="w_in", wout="w_out", gu2="ffn2_w_gate_up", dn2="ffn2_w_down")
    delta, new_m, new_v, grad = {}, {}, {}, {}

    def update(kind_list, joined):
        for k, g in zip(kind_list, joined):
            n = kind_name[k]
            grad[n] = g
            delta[n], new_m[n], new_v[n] = _adamw(f"adamw_{n}", weights[n], g, m_in[n], v_in[n])

    early = ("gu2", "dn2", "win", "wout")
    update(early, _join_core_halves("join_early", [red.bufs[k] for k in early], flow["deps"]))
    late = [k for (k, _) in flow["in_flight"]]
    for k, layer in flow["in_flight"]:
        red.finish(k, layer, [delta[kind_name[e]] for e in early])
    update(late, _join_core_halves("join_late", [red.bufs[k] for k in late]))

    small_sum = _allreduce_small(jnp.concatenate(small, axis=0)).reshape(n_layers, SMALL_ROWS, d)
    g_ffn1_pre, g_ffn1_post, g_mix_pre = small_sum[:, 0], small_sum[:, 1], small_sum[:, 2]
    g_attn_out, g_conv_out = small_sum[:, 3, :qd], small_sum[:, 3, qd:qd + cd]
    g_mix_post, g_ffn2_pre, g_ffn2_post = small_sum[:, 4], small_sum[:, 5], small_sum[:, 6]
    cc = conv_w.shape[2]
    g_conv = lax.dynamic_slice_in_dim(small_sum[:, 7:7 + CONV_WIDTH, :cd], chip * cc, cc, axis=2)

    grad.update(ffn1_norm_pre=g_ffn1_pre, ffn1_norm_post=g_ffn1_post, mix_norm_pre=g_mix_pre, conv_w=g_conv, attn_out_norm=g_attn_out, conv_out_norm=g_conv_out, mix_norm_post=g_mix_post, ffn2_norm_pre=g_ffn2_pre, ffn2_norm_post=g_ffn2_post)
    names = list(weights)

    vectors = [n for n in names if n not in kind_name.values()]

    def pack(tree):
        flat = jnp.concatenate([tree[n].reshape(-1) for n in vectors])
        return jnp.pad(flat, (0, -flat.size % (SUBLANES * LANES))).reshape(-1, LANES)

    packed = _adamw("adamw_small", pack(weights), pack(grad), pack(m_in), pack(v_in))
    offset = 0
    for n in vectors:
        size = weights[n].size
        for tree, flat in zip((delta, new_m, new_v), packed):
            tree[n] = flat.reshape(-1)[offset:offset + size].reshape(weights[n].shape)
        offset += size

    return (loss, grad_x, *[grad[n] for n in names], *[delta[n] for n in names],
            *[new_m[n] for n in names], *[new_v[n] for n in names])
```

```python
import functools

import jax
import jax.numpy as jnp
from jax import lax
from jax.experimental import pallas as pl
from jax.experimental.pallas import tpu as pltpu

F32 = jnp.float32
BF16 = jnp.bfloat16
MESH = pl.DeviceIdType.MESH

NORM_EPS = 1e-6
HEAD_DIM = 128
Q_PER_KV = 4
CONV_WIDTH = 3
FFN_RESIDUAL_WEIGHT = 0.5
DILATED_BRANCHES = ((128, 1), (512, 4), (2048, 16))
ADAM_LR = 0.001
ADAM_B1 = 0.9
ADAM_B2 = 0.999
ADAM_EPS = 1e-08
ADAM_WD = 0.01
ADAM_STEP = 10

N_CHIPS = 4
N_DEV = 8
V7X_VMEM_BYTES = 64 << 20
VMEM_LIMIT = V7X_VMEM_BYTES - (12 << 20)
SUBLANES = 8
LANES = 128
SMALL_ROWS = 16


def _params(*sem):
    return pltpu.CompilerParams(dimension_semantics=sem, vmem_limit_bytes=VMEM_LIMIT)


def _row_tile(rows, cols, itemsize=4, budget=2 << 20):
    t = rows
    while t * cols * itemsize > budget and t % 32 == 0:
        t //= 2
    return t


def _sum_to_sublanes(v):
    r, n = v.shape
    return v.reshape(r // SUBLANES, SUBLANES, n).sum(axis=0)


_DIMS = {
    "nn": (((1,), (0,)), ((), ())),
    "nt": (((1,), (1,)), ((), ())),
    "tn": (((0,), (0,)), ((), ())),
}


ANY_SPEC = pl.BlockSpec(memory_space=pl.ANY)


def _dot(a, b, mode):
    return lax.dot_general(a, b, _DIMS[mode], preferred_element_type=F32)


def _mm(name, a, b, *, mode, grid, a_spec, b_spec, o_spec, out_shape, nk=1, acc_shape=None, deps=()):
    nd = len(deps)

    def body(a_ref, b_ref, *rest):
        o_ref, scratch = rest[nd], rest[nd + 1:]
        r = _dot(a_ref[...], b_ref[...], mode)
        if nk == 1:
            o_ref[...] = r.astype(o_ref.dtype)
        else:
            acc = scratch[0]
            k = pl.program_id(len(grid) - 1)

            @pl.when(k == 0)
            def _():
                acc[...] = r

            @pl.when(k > 0)
            def _():
                acc[...] += r

            @pl.when(k == nk - 1)
            def _():
                o_ref[...] = acc[...].astype(o_ref.dtype)

    sem = ("parallel",) * (len(grid) - (1 if nk > 1 else 0)) + (("arbitrary",) if nk > 1 else ())
    return pl.pallas_call(
        body, name=name, grid=grid, in_specs=[a_spec, b_spec] + [ANY_SPEC] * nd, out_specs=o_spec,
        out_shape=out_shape, scratch_shapes=[pltpu.VMEM(acc_shape, F32)] if nk > 1 else [],
        compiler_params=_params(*sem),
    )(a, b, *deps)


def _tile(n, want):
    if n <= want:
        return n
    best = None
    for t in range(LANES, want + 1, LANES):
        if n % t == 0:
            best = t
    assert best is not None, (n, want)
    return best


def _norm_fwd(name, x, gain):
    s, d = x.shape
    tr = _row_tile(s, d)

    def body(x_ref, g_ref, o_ref):
        xv = x_ref[...]
        r = lax.rsqrt(jnp.mean(xv * xv, axis=-1, keepdims=True) + NORM_EPS)
        o_ref[...] = (xv * r * g_ref[...]).astype(o_ref.dtype)

    return pl.pallas_call(
        body, name=name, grid=(s // tr,),
        in_specs=[pl.BlockSpec((tr, d), lambda i: (i, 0)), pl.BlockSpec((1, d), lambda i: (0, 0))],
        out_specs=pl.BlockSpec((tr, d), lambda i: (i, 0)),
        out_shape=jax.ShapeDtypeStruct((s, d), BF16), compiler_params=_params("parallel"),
    )(x, gain)


def _res_norm(name, x, y, gain, scale):
    s, d = x.shape
    tr = _row_tile(s, d)

    def body(x_ref, y_ref, g_ref, o_ref):
        yv = y_ref[...]
        r = lax.rsqrt(jnp.mean(yv * yv, axis=-1, keepdims=True) + NORM_EPS)
        o_ref[...] = x_ref[...] + scale * (yv * r * g_ref[...])

    row = pl.BlockSpec((tr, d), lambda i: (i, 0))
    return pl.pallas_call(
        body, name=name, grid=(s // tr,),
        in_specs=[row, row, pl.BlockSpec((1, d), lambda i: (0, 0))], out_specs=row,
        out_shape=jax.ShapeDtypeStruct((s, d), F32), compiler_params=_params("parallel"),
    )(x, y, gain)


def _norm_bwd(name, dout, yin, gain, scale, resid, out_dtype):
    s, d = yin.shape
    tr = _row_tile(s, d)
    has_resid = resid is not None

    def body(*refs):
        if has_resid:
            do_ref, y_ref, g_ref, r_ref, di_ref, dg_ref = refs
        else:
            do_ref, y_ref, g_ref, di_ref, dg_ref = refs
        yv = y_ref[...]
        r = lax.rsqrt(jnp.mean(yv * yv, axis=-1, keepdims=True) + NORM_EPS)
        xhat = yv * r
        dn = scale * do_ref[...]
        part = _sum_to_sublanes(dn * xhat)

        @pl.when(pl.program_id(0) == 0)
        def _():
            dg_ref[...] = part

        @pl.when(pl.program_id(0) > 0)
        def _():
            dg_ref[...] += part

        dxn = dn * g_ref[...]
        din = r * (dxn - xhat * jnp.mean(dxn * xhat, axis=-1, keepdims=True))
        if has_resid:
            din = din + r_ref[...]
        di_ref[...] = din.astype(di_ref.dtype)

    row = pl.BlockSpec((tr, d), lambda i: (i, 0))
    vec = pl.BlockSpec((1, d), lambda i: (0, 0))
    ins = [row, row, vec] + ([row] if has_resid else [])
    args = (dout, yin, gain) + ((resid,) if has_resid else ())
    return pl.pallas_call(
        body, name=name, grid=(s // tr,), in_specs=ins,
        out_specs=[row, pl.BlockSpec((SUBLANES, d), lambda i: (0, 0))],
        out_shape=[jax.ShapeDtypeStruct((s, d), out_dtype), jax.ShapeDtypeStruct((SUBLANES, d), F32)],
        compiler_params=_params("arbitrary"),
    )(*args)


def _loss_head(name, y, target):
    s, d = y.shape
    tr = _row_tile(s, d)

    def body(y_ref, t_ref, dy_ref, l_ref):
        e = y_ref[...] - t_ref[...]
        dy_ref[...] = e * (1.0 / d)
        part = _sum_to_sublanes(e * e) * (0.5 / d)

        @pl.when(pl.program_id(0) == 0)
        def _():
            l_ref[...] = part

        @pl.when(pl.program_id(0) > 0)
        def _():
            l_ref[...] += part

    row = pl.BlockSpec((tr, d), lambda i: (i, 0))
    return pl.pallas_call(
        body, name=name, grid=(s // tr,), in_specs=[row, row],
        out_specs=[row, pl.BlockSpec((SUBLANES, d), lambda i: (0, 0))],
        out_shape=[jax.ShapeDtypeStruct((s, d), F32), jax.ShapeDtypeStruct((SUBLANES, d), F32)],
        compiler_params=_params("arbitrary"),
    )(y, target)


def _ffn_up(name, h, gu_w):
    s, d = h.shape
    nb, _, fs = gu_w.shape
    hb = nb // 2
    w = gu_w.reshape(2, hb, d, fs)
    tm = _tile(s, 512)
    tn = _tile(fs, 1408)
    nj = fs // tn

    def body(h_ref, w_ref, gu_ref, a_ref):
        hv = h_ref[...]
        g = _dot(hv, w_ref[0], "nn")
        u = _dot(hv, w_ref[1], "nn")
        gu_ref[0] = g.astype(gu_ref.dtype)
        gu_ref[1] = u.astype(gu_ref.dtype)
        a_ref[...] = (g * jax.nn.sigmoid(g) * u).astype(a_ref.dtype)

    return pl.pallas_call(
        body, name=name, grid=(hb, nj, s // tm),
        in_specs=[pl.BlockSpec((tm, d), lambda jb, jo, i: (i, 0)),
                  pl.BlockSpec((2, None, d, tn), lambda jb, jo, i: (0, jb, 0, jo))],
        out_specs=[pl.BlockSpec((2, None, tm, tn), lambda jb, jo, i: (0, jb, i, jo)),
                   pl.BlockSpec((tm, tn), lambda jb, jo, i: (i, jb * nj + jo))],
        out_shape=[jax.ShapeDtypeStruct((2, hb, s, fs), BF16), jax.ShapeDtypeStruct((s, hb * fs), BF16)],
        compiler_params=_params("parallel", "parallel", "parallel"),
    )(h, w)


def _ffn_dact(name, dy, dn_w, gu, deps=()):
    s, d = dy.shape
    _, hb, _, fs = gu.shape
    tm = _tile(s, 512)
    tn = _tile(fs, 1408)
    nj = fs // tn

    def body(dy_ref, w_ref, gu_ref, *rest):
        o_ref = rest[-1]
        da = _dot(dy_ref[...], w_ref[...], "nt")
        g = gu_ref[0].astype(F32)
        u = gu_ref[1].astype(F32)
        sg = jax.nn.sigmoid(g)
        o_ref[0] = (da * u * (sg * (1.0 + g * (1.0 - sg)))).astype(o_ref.dtype)
        o_ref[1] = (da * (g * sg)).astype(o_ref.dtype)

    blk = pl.BlockSpec((2, None, tm, tn), lambda jb, jo, i: (0, jb, i, jo))
    return pl.pallas_call(
        body, name=name, grid=(hb, nj, s // tm),
        in_specs=[pl.BlockSpec((tm, d), lambda jb, jo, i: (i, 0)),
                  pl.BlockSpec((tn, d), lambda jb, jo, i: (jb * nj + jo, 0)),
                  blk] + [ANY_SPEC] * len(deps),
        out_specs=blk, out_shape=jax.ShapeDtypeStruct(gu.shape, BF16),
        compiler_params=_params("parallel", "parallel", "parallel"),
    )(dy, dn_w, gu, *deps)


def _multiplicity(q0, k0, tq, tk):
    row = q0 + lax.broadcasted_iota(jnp.int32, (tq, tk), 0)
    col = k0 + lax.broadcasted_iota(jnp.int32, (tq, tk), 1)
    dist = row - col
    mult = jnp.zeros((tq, tk), F32)
    for window, dilation in DILATED_BRANCHES:
        hit = (dist <= window) & ((dist & (dilation - 1)) == 0)
        mult = mult + hit.astype(F32)
    return jnp.where(dist >= 0, mult, 0.0)


_MASKED = -1e30


def _attn_specs(s, qd, kvd, tq):
    rw = Q_PER_KV * HEAD_DIM
    qspec = pl.BlockSpec((tq, rw), lambda g, i: (i, g))
    kspec = pl.BlockSpec((s, HEAD_DIM), lambda g, i: (0, qd // HEAD_DIM + g))
    vspec = pl.BlockSpec((s, HEAD_DIM), lambda g, i: (0, (qd + kvd) // HEAD_DIM + g))
    return rw, qspec, kspec, vspec


def _attn_fwd(name, z, qd, kvd):
    s = z.shape[0]
    tq = _tile(s, 256)
    nkv = kvd // HEAD_DIM
    rw, qspec, kspec, vspec = _attn_specs(s, qd, kvd, tq)
    scale = HEAD_DIM ** -0.5

    def body(q_ref, k_ref, v_ref, o_ref, l_ref):
        i = pl.program_id(1)
        heads = [slice(h * HEAD_DIM, (h + 1) * HEAD_DIM) for h in range(Q_PER_KV)]

        def chunk(j, carry):
            k0 = pl.multiple_of(j * tq, tq)
            kc, vc = k_ref[pl.ds(k0, tq), :], v_ref[pl.ds(k0, tq), :]
            mult = _multiplicity(i * tq, k0, tq, tq)
            live = mult > 0.0
            out = []
            for cols, (mx, den, acc) in zip(heads, carry):
                sc = jnp.where(live, _dot(q_ref[:, cols], kc, "nt") * scale, _MASKED)
                mx_new = jnp.maximum(mx, jnp.max(sc, axis=-1, keepdims=True))
                alpha = jnp.exp(mx - mx_new)
                p = jnp.exp(sc - mx_new) * mult
                out.append((mx_new, alpha * den + jnp.sum(p, axis=-1, keepdims=True),
                            alpha * acc + _dot(p.astype(BF16), vc, "nn")))
            return tuple(out)

        init = tuple((jnp.full((tq, 1), _MASKED, F32), jnp.zeros((tq, 1), F32), jnp.zeros((tq, HEAD_DIM), F32))
                     for _ in heads)
        for cols, (mx, den, acc) in zip(heads, lax.fori_loop(0, i + 1, chunk, init)):
            o_ref[:, cols] = acc / den
            l_ref[:, cols] = jnp.broadcast_to(mx + jnp.log(den), (tq, HEAD_DIM))

    return pl.pallas_call(
        body, name=name, grid=(nkv, s // tq), in_specs=[qspec, kspec, vspec], out_specs=[qspec, qspec],
        out_shape=[jax.ShapeDtypeStruct((s, qd), F32), jax.ShapeDtypeStruct((s, qd), F32)],
        compiler_params=_params("parallel", "parallel"),
    )(z, z, z)


def _attn_bwd(name, z, o, lse, do, qd, kvd):
    s = z.shape[0]
    tq = _tile(s, 256)
    nkv = kvd // HEAD_DIM
    nq = s // tq
    rw, qspec, kspec, vspec = _attn_specs(s, qd, kvd, tq)
    scale = HEAD_DIM ** -0.5

    def body(q_ref, k_ref, v_ref, o_ref, l_ref, do_ref, dq_ref, dk_ref, dv_ref, dk_acc, dv_acc):
        i = pl.program_id(1)
        heads = [slice(h * HEAD_DIM, (h + 1) * HEAD_DIM) for h in range(Q_PER_KV)]

        @pl.when(i == 0)
        def _():
            dk_acc[...] = jnp.zeros_like(dk_acc)
            dv_acc[...] = jnp.zeros_like(dv_acc)

        qs = [q_ref[:, cols] for cols in heads]
        dos = [do_ref[:, cols].astype(BF16) for cols in heads]
        lses = [l_ref[:, cols][:, :1] for cols in heads]
        deltas = [jnp.sum(do_ref[:, cols] * o_ref[:, cols], axis=-1, keepdims=True) for cols in heads]

        def chunk(j, dqs):
            k0 = pl.multiple_of(j * tq, tq)
            kc, vc = k_ref[pl.ds(k0, tq), :], v_ref[pl.ds(k0, tq), :]
            mult = _multiplicity(i * tq, k0, tq, tq)
            live = mult > 0.0
            dk_c = jnp.zeros((tq, HEAD_DIM), F32)
            dv_c = jnp.zeros((tq, HEAD_DIM), F32)
            out = []
            for q, dob, lse, delta, dq in zip(qs, dos, lses, deltas, dqs):
                sc = jnp.where(live, _dot(q, kc, "nt") * scale, _MASKED)
                p = jnp.exp(sc - lse) * mult
                ds = (p * (_dot(dob, vc, "nt") - delta) * scale).astype(BF16)
                out.append(dq + _dot(ds, kc, "nn"))
                dk_c = dk_c + _dot(ds, q, "tn")
                dv_c = dv_c + _dot(p.astype(BF16), dob, "tn")
            dk_acc[pl.ds(k0, tq), :] += dk_c
            dv_acc[pl.ds(k0, tq), :] += dv_c
            return tuple(out)

        dqs = lax.fori_loop(0, i + 1, chunk, tuple(jnp.zeros((tq, HEAD_DIM), F32) for _ in heads))
        for cols, dq in zip(heads, dqs):
            dq_ref[:, cols] = dq.astype(dq_ref.dtype)

        @pl.when(i == nq - 1)
        def _():
            dk_ref[...] = dk_acc[...].astype(dk_ref.dtype)
            dv_ref[...] = dv_acc[...].astype(dv_ref.dtype)

    kvout = pl.BlockSpec((s, HEAD_DIM), lambda g, i: (0, g))
    return pl.pallas_call(
        body, name=name, grid=(nkv, nq), in_specs=[qspec, kspec, vspec, qspec, qspec, qspec],
        out_specs=[qspec, kvout, kvout],
        out_shape=[jax.ShapeDtypeStruct((s, qd), BF16), jax.ShapeDtypeStruct((s, kvd), BF16),
                   jax.ShapeDtypeStruct((s, kvd), BF16)],
        scratch_shapes=[pltpu.VMEM((s, HEAD_DIM), F32), pltpu.VMEM((s, HEAD_DIM), F32)],
        compiler_params=_params("parallel", "arbitrary"),
    )(z, z, z, o, lse, do)


def _shift_down(v, n):
    rolled = pltpu.roll(v, n, 0)
    t = lax.broadcasted_iota(jnp.int32, v.shape, 0)
    return jnp.where(t >= n, rolled, 0.0)


def _shift_up(v, n):
    rows = v.shape[0]
    rolled = pltpu.roll(v, rows - n, 0)
    t = lax.broadcasted_iota(jnp.int32, v.shape, 0)
    return jnp.where(t < rows - n, rolled, 0.0)


def _conv_specs(s, base, cd, tc):
    zs = [pl.BlockSpec((s, tc), functools.partial(lambda j, off: (0, off + j), off=(base + n * cd) // tc))
          for n in range(3)]
    wspec = pl.BlockSpec((SUBLANES, tc), lambda j: (0, j))
    cspec = pl.BlockSpec((s, tc), lambda j: (0, j))
    return zs, wspec, cspec


def _conv_fwd(name, z, conv_w, base, cd):
    s = z.shape[0]
    tc = _tile(cd, 256)
    zs, wspec, cspec = _conv_specs(s, base, cd, tc)

    def body(h_ref, b_ref, c_ref, w_ref, o_ref):
        u = c_ref[...].astype(F32) * h_ref[...].astype(F32)
        y = w_ref[0:1, :] * _shift_down(u, 2) + w_ref[1:2, :] * _shift_down(u, 1) + w_ref[2:3, :] * u
        o_ref[...] = b_ref[...].astype(F32) * y

    return pl.pallas_call(
        body, name=name, grid=(cd // tc,), in_specs=zs + [wspec], out_specs=cspec,
        out_shape=jax.ShapeDtypeStruct((s, cd), F32), compiler_params=_params("parallel"),
    )(z, z, z, conv_w)


def _conv_bwd(name, z, conv_w, dc, base, cd):
    s = z.shape[0]
    tc = _tile(cd, 256)
    zs, wspec, cspec = _conv_specs(s, base, cd, tc)

    def body(h_ref, b_ref, c_ref, w_ref, dc_ref, dh_ref, db_ref, dcg_ref, dw_ref):
        hv, bv, cv = h_ref[...].astype(F32), b_ref[...].astype(F32), c_ref[...].astype(F32)
        u = cv * hv
        u1, u2 = _shift_down(u, 1), _shift_down(u, 2)
        w0, w1, w2 = w_ref[0:1, :], w_ref[1:2, :], w_ref[2:3, :]
        y = w0 * u2 + w1 * u1 + w2 * u
        dcv = dc_ref[...]
        db_ref[...] = (dcv * y).astype(db_ref.dtype)
        dy = dcv * bv
        du = w2 * dy + w1 * _shift_up(dy, 1) + w0 * _shift_up(dy, 2)
        dh_ref[...] = (du * cv).astype(dh_ref.dtype)
        dcg_ref[...] = (du * hv).astype(dcg_ref.dtype)
        g0 = jnp.sum(dy * u2, axis=0, keepdims=True)
        g1 = jnp.sum(dy * u1, axis=0, keepdims=True)
        g2 = jnp.sum(dy * u, axis=0, keepdims=True)
        r = lax.broadcasted_iota(jnp.int32, (SUBLANES, tc), 0)
        dw_ref[...] = jnp.where(r == 0, g0, jnp.where(r == 1, g1, jnp.where(r == 2, g2, 0.0)))

    return pl.pallas_call(
        body, name=name, grid=(cd // tc,), in_specs=zs + [wspec, cspec],
        out_specs=[cspec, cspec, cspec, wspec],
        out_shape=[jax.ShapeDtypeStruct((s, cd), BF16)] * 3 + [jax.ShapeDtypeStruct((SUBLANES, cd), F32)],
        compiler_params=_params("parallel"),
    )(z, z, z, conv_w, dc)


def _cat_norm_fwd(name, a, c, ga, gc):
    s, qd = a.shape
    cd = c.shape[1]
    tr = _row_tile(s, qd + cd)

    def body(a_ref, c_ref, ga_ref, gc_ref, o_ref):
        av, cv = a_ref[...], c_ref[...]
        ra = lax.rsqrt(jnp.mean(av * av, axis=-1, keepdims=True) + NORM_EPS)
        rc = lax.rsqrt(jnp.mean(cv * cv, axis=-1, keepdims=True) + NORM_EPS)
        o_ref[:, :qd] = (av * ra * ga_ref[...]).astype(o_ref.dtype)
        o_ref[:, qd:] = (cv * rc * gc_ref[...]).astype(o_ref.dtype)

    return pl.pallas_call(
        body, name=name, grid=(s // tr,),
        in_specs=[pl.BlockSpec((tr, qd), lambda i: (i, 0)), pl.BlockSpec((tr, cd), lambda i: (i, 0)),
                  pl.BlockSpec((1, qd), lambda i: (0, 0)), pl.BlockSpec((1, cd), lambda i: (0, 0))],
        out_specs=pl.BlockSpec((tr, qd + cd), lambda i: (i, 0)),
        out_shape=jax.ShapeDtypeStruct((s, qd + cd), BF16), compiler_params=_params("parallel"),
    )(a, c, ga, gc)


def _cat_norm_bwd(name, dcat, a, c, ga, gc):
    s, qd = a.shape
    cd = c.shape[1]
    tr = _row_tile(s, qd + cd)

    def one(dn, yv, gv):
        r = lax.rsqrt(jnp.mean(yv * yv, axis=-1, keepdims=True) + NORM_EPS)
        xhat = yv * r
        dxn = dn * gv
        return r * (dxn - xhat * jnp.mean(dxn * xhat, axis=-1, keepdims=True)), _sum_to_sublanes(dn * xhat)

    def body(d_ref, a_ref, c_ref, ga_ref, gc_ref, da_ref, dc_ref, dga_ref, dgc_ref):
        da, pa = one(d_ref[:, :qd], a_ref[...], ga_ref[...])
        dc, pc = one(d_ref[:, qd:], c_ref[...], gc_ref[...])
        da_ref[...] = da
        dc_ref[...] = dc

        @pl.when(pl.program_id(0) == 0)
        def _():
            dga_ref[...] = pa
            dgc_ref[...] = pc

        @pl.when(pl.program_id(0) > 0)
        def _():
            dga_ref[...] += pa
            dgc_ref[...] += pc

    ra = pl.BlockSpec((tr, qd), lambda i: (i, 0))
    rc = pl.BlockSpec((tr, cd), lambda i: (i, 0))
    return pl.pallas_call(
        body, name=name, grid=(s // tr,),
        in_specs=[pl.BlockSpec((tr, qd + cd), lambda i: (i, 0)), ra, rc,
                  pl.BlockSpec((1, qd), lambda i: (0, 0)), pl.BlockSpec((1, cd), lambda i: (0, 0))],
        out_specs=[ra, rc, pl.BlockSpec((SUBLANES, qd), lambda i: (0, 0)),
                   pl.BlockSpec((SUBLANES, cd), lambda i: (0, 0))],
        out_shape=[jax.ShapeDtypeStruct((s, qd), F32), jax.ShapeDtypeStruct((s, cd), F32),
                   jax.ShapeDtypeStruct((SUBLANES, qd), F32), jax.ShapeDtypeStruct((SUBLANES, cd), F32)],
        compiler_params=_params("arbitrary"),
    )(dcat, a, c, ga, gc)


def _adamw(name, w, g, m, v, emit_grad=False):
    shape = w.shape
    cols = shape[-1]
    rows = w.size // cols
    tr = _row_tile(rows, cols, budget=3 << 19)
    bc1 = 1.0 - ADAM_B1 ** ADAM_STEP
    bc2 = 1.0 - ADAM_B2 ** ADAM_STEP
    n_out = 4 if emit_grad else 3

    def body(w_ref, g_ref, m_ref, v_ref, d_ref, nm_ref, nv_ref, *g_out):
        gv = g_ref[...]
        mv = ADAM_B1 * m_ref[...] + (1.0 - ADAM_B1) * gv
        vv = ADAM_B2 * v_ref[...] + (1.0 - ADAM_B2) * (gv * gv)
        nm_ref[...] = mv
        nv_ref[...] = vv
        d_ref[...] = -ADAM_LR * ((mv / bc1) / (jnp.sqrt(vv / bc2) + ADAM_EPS) + ADAM_WD * w_ref[...])
        for ref in g_out:
            ref[...] = gv

    row = pl.BlockSpec((tr, cols), lambda i: (i, 0))
    outs = pl.pallas_call(
        body, name=name, grid=(rows // tr,), in_specs=[row] * 4, out_specs=[row] * n_out,
        out_shape=[jax.ShapeDtypeStruct((rows, cols), F32)] * n_out, compiler_params=_params("parallel"),
    )(*(t.reshape(rows, cols) for t in (w, g, m, v)))
    return tuple(t.reshape(shape) for t in outs)


HBM_SPEC = pl.BlockSpec(memory_space=pltpu.HBM)


def _mesh_place():
    x, y, c = lax.axis_index("x"), lax.axis_index("y"), lax.axis_index("c")
    other_chips = [(1 - x, y), (x, 1 - y), (1 - x, 1 - y)]
    return x, y, c, other_chips


def _cast_into_slot(name, w, layer, chip, deps=()):
    _, r, cols = w.shape
    tr = _row_tile(r, cols)

    def body(chip_ref, w_ref, *rest):
        o_ref = rest[-1]
        o_ref[...] = w_ref[...].astype(o_ref.dtype)

    return pl.pallas_call(
        body, name=name,
        grid_spec=pltpu.PrefetchScalarGridSpec(
            num_scalar_prefetch=1, grid=(r // tr,),
            in_specs=[pl.BlockSpec((None, tr, cols), lambda i, chip_ref: (layer, i, 0))] + [ANY_SPEC] * len(deps),
            out_specs=pl.BlockSpec((None, tr, cols), lambda i, chip_ref: (chip_ref[0], i, 0))),
        out_shape=jax.ShapeDtypeStruct((N_CHIPS, r, cols), BF16), compiler_params=_params("parallel"),
    )(chip, w, *deps)


SEM_SPEC = pl.BlockSpec(memory_space=pltpu.SEMAPHORE)
SPLIT_COPY = pltpu.CompilerParams(has_side_effects=pltpu.SideEffectType.DATAFLOW_SIDE_EFFECTING)
N_OTHER = N_CHIPS - 1
TOKEN_SPEC = pl.BlockSpec(memory_space=pltpu.VMEM)
TOKEN_SHAPE = jax.ShapeDtypeStruct((SUBLANES, LANES), F32)


def _in_hbm(arr):
    return pltpu.with_memory_space_constraint(arr, pltpu.HBM)


def _half_rows(ref, chip_idx, core):
    r2 = ref.shape[1] // 2
    return ref.at[chip_idx, pl.ds(core * r2, r2), :]


def _gather_start(name, fulls, after):
    na = len(fulls)

    def body(*refs):
        f_refs = refs[na + 1:2 * na + 1]
        send_sems, recv_sems = refs[2 * na + 1:3 * na + 1], refs[3 * na + 1:4 * na + 1]
        token = refs[4 * na + 1]
        x, y, c, chips = _mesh_place()
        for a in range(na):
            mine = _half_rows(f_refs[a], 2 * x + y, c)
            for j, (cx, cy) in enumerate(chips):
                pltpu.make_async_remote_copy(
                    src_ref=mine, dst_ref=mine, send_sem=send_sems[a].at[j], recv_sem=recv_sems[a].at[j],
                    device_id=(cx, cy, c), device_id_type=MESH).start()
        token[...] = jnp.zeros_like(token)

    outs = pl.pallas_call(
        body, name=name, in_specs=[HBM_SPEC] * na + [ANY_SPEC],
        out_specs=[HBM_SPEC] * na + [SEM_SPEC] * (2 * na) + [TOKEN_SPEC],
        out_shape=[pltpu.HBM(f.shape, f.dtype) for f in fulls] + [pltpu.SemaphoreType.DMA((N_OTHER,))] * (2 * na)
        + [TOKEN_SHAPE],
        input_output_aliases={a: a for a in range(na)}, compiler_params=SPLIT_COPY,
    )(*[_in_hbm(f) for f in fulls], after)
    return list(outs[:na]), list(outs[na:2 * na]), list(outs[2 * na:3 * na]), outs[3 * na]


def _gather_pass_on(name, full, recv_sems, after):
    def body(f_in, recv_sems, after_ref, f_ref, d2d_send, d2d_recv):
        x, y, c, chips = _mesh_place()
        for j, (cx, cy) in enumerate(chips):
            blk = _half_rows(f_ref, 2 * cx + cy, c)
            pltpu.make_async_remote_copy(
                src_ref=blk, dst_ref=blk, send_sem=d2d_send.at[j], recv_sem=recv_sems.at[j],
                device_id=(cx, cy, c), device_id_type=MESH).wait_recv()
            pltpu.make_async_remote_copy(
                src_ref=blk, dst_ref=blk, send_sem=d2d_send.at[j], recv_sem=d2d_recv.at[j],
                device_id=(x, y, 1 - c), device_id_type=MESH).start()

    return pl.pallas_call(
        body, name=name, in_specs=[HBM_SPEC, SEM_SPEC, ANY_SPEC], out_specs=[HBM_SPEC, SEM_SPEC, SEM_SPEC],
        out_shape=[pltpu.HBM(full.shape, full.dtype)] + [pltpu.SemaphoreType.DMA((N_OTHER,))] * 2,
        input_output_aliases={0: 0}, compiler_params=SPLIT_COPY,
    )(full, recv_sems, after)


def _gather_arrive(name, full, ici_send, d2d_send, d2d_recv, after):
    def body(f_in, ici_send, d2d_send, d2d_recv, after_ref, f_ref):
        x, y, c, chips = _mesh_place()
        for j, (cx, cy) in enumerate(chips):
            mine = _half_rows(f_ref, 2 * x + y, c)
            passed = _half_rows(f_ref, 2 * cx + cy, c)
            theirs = _half_rows(f_ref, 2 * cx + cy, 1 - c)
            pltpu.make_async_remote_copy(
                src_ref=mine, dst_ref=mine, send_sem=ici_send.at[j], recv_sem=d2d_recv.at[j],
                device_id=(cx, cy, c), device_id_type=MESH).wait_send()
            pltpu.make_async_remote_copy(
                src_ref=passed, dst_ref=passed, send_sem=d2d_send.at[j], recv_sem=d2d_recv.at[j],
                device_id=(x, y, 1 - c), device_id_type=MESH).wait_send()
            pltpu.make_async_remote_copy(
                src_ref=theirs, dst_ref=theirs, send_sem=d2d_send.at[j], recv_sem=d2d_recv.at[j],
                device_id=(x, y, 1 - c), device_id_type=MESH).wait_recv()

    return pl.pallas_call(
        body, name=name, in_specs=[HBM_SPEC, SEM_SPEC, SEM_SPEC, SEM_SPEC, ANY_SPEC], out_specs=HBM_SPEC,
        out_shape=pltpu.HBM(full.shape, full.dtype), input_output_aliases={0: 0}, compiler_params=SPLIT_COPY,
    )(full, ici_send, d2d_send, d2d_recv, after)


def _gather_taps(conv_w):
    def body(cw_ref, cwf_ref, send_sems, recv_sems, local_sem):
        x, y, c, chips = _mesh_place()
        k_me = 2 * x + y
        local = pltpu.make_async_copy(cw_ref, cwf_ref.at[k_me], local_sem)
        local.start()
        copies = [pltpu.make_async_remote_copy(
            src_ref=cw_ref, dst_ref=cwf_ref.at[k_me], send_sem=send_sems.at[j], recv_sem=recv_sems.at[j],
            device_id=(cx, cy, c), device_id_type=MESH) for j, (cx, cy) in enumerate(chips)]
        for cp in copies:
            cp.start()
        for j, (cx, cy) in enumerate(chips):
            pltpu.make_async_remote_copy(
                src_ref=cw_ref, dst_ref=cwf_ref.at[2 * cx + cy], send_sem=send_sems.at[j], recv_sem=recv_sems.at[j],
                device_id=(cx, cy, c), device_id_type=MESH).wait_recv()
        for cp in copies:
            cp.wait_send()
        local.wait()

    return pl.pallas_call(
        body, name="gather_taps", in_specs=[HBM_SPEC], out_specs=HBM_SPEC,
        out_shape=jax.ShapeDtypeStruct((N_CHIPS,) + conv_w.shape, conv_w.dtype),
        scratch_shapes=[pltpu.SemaphoreType.DMA((N_OTHER,))] * 2 + [pltpu.SemaphoreType.DMA],
    )(conv_w)


def _sibling_half(g_ref, c):
    r2 = g_ref.shape[1] // 2
    return g_ref.at[:, pl.ds((1 - c) * r2, r2), :]


def _swap_start(name, g):
    def body(g_in, g_ref, land_ref, send_sem, recv_sem, token):
        x, y, c, _ = _mesh_place()
        pltpu.make_async_remote_copy(
            src_ref=_sibling_half(g_ref, c), dst_ref=land_ref, send_sem=send_sem, recv_sem=recv_sem,
            device_id=(x, y, 1 - c), device_id_type=MESH).start()
        token[...] = jnp.zeros_like(token)

    nb, r, cols = g.shape
    return pl.pallas_call(
        body, name=name, in_specs=[HBM_SPEC], out_specs=[HBM_SPEC, HBM_SPEC, SEM_SPEC, SEM_SPEC, TOKEN_SPEC],
        out_shape=[pltpu.HBM(g.shape, g.dtype), pltpu.HBM((nb, r // 2, cols), g.dtype),
                   pltpu.SemaphoreType.DMA(()), pltpu.SemaphoreType.DMA(()), TOKEN_SHAPE],
        input_output_aliases={0: 0}, compiler_params=SPLIT_COPY,
    )(_in_hbm(g))


def _swap_wait(name, g, land, send_sem, recv_sem, after):
    def body(g_in, land_in, send_sem, recv_sem, after_ref, g_ref, land_ref):
        x, y, c, _ = _mesh_place()
        copy = pltpu.make_async_remote_copy(
            src_ref=_sibling_half(g_ref, c), dst_ref=land_ref, send_sem=send_sem, recv_sem=recv_sem,
            device_id=(x, y, 1 - c), device_id_type=MESH)
        copy.wait_send()
        copy.wait_recv()

    return pl.pallas_call(
        body, name=name, in_specs=[HBM_SPEC, HBM_SPEC, SEM_SPEC, SEM_SPEC, ANY_SPEC], out_specs=[HBM_SPEC, HBM_SPEC],
        out_shape=[pltpu.HBM(g.shape, g.dtype), pltpu.HBM(land.shape, land.dtype)],
        input_output_aliases={0: 0, 1: 1}, compiler_params=SPLIT_COPY,
    )(g, land, send_sem, recv_sem, after)


def _add_core_halves(name, g, sib, core):
    nb, r, cols = g.shape
    r2 = r // 2
    tr = _row_tile(r2, cols, itemsize=2, budget=1 << 20)
    nrt = r2 // tr

    def body(core_ref, g_ref, s_ref, o_ref):
        o_ref[...] = (g_ref[...].astype(F32) + s_ref[...].astype(F32)).astype(o_ref.dtype)

    return pl.pallas_call(
        body, name=name,
        grid_spec=pltpu.PrefetchScalarGridSpec(
            num_scalar_prefetch=1, grid=(nb, nrt),
            in_specs=[pl.BlockSpec((None, tr, cols), lambda k, i, core_ref: (k, core_ref[0] * nrt + i, 0)),
                      pl.BlockSpec((None, tr, cols), lambda k, i, core_ref: (k, i, 0))],
            out_specs=pl.BlockSpec((None, tr, cols), lambda k, i, core_ref: (k, i, 0))),
        out_shape=jax.ShapeDtypeStruct((nb, r2, cols), BF16), compiler_params=_params("parallel", "parallel"),
    )(core, g, sib)


def _scatter_copy(h_ref, land_ref, send_sems, recv_sems, j, chip_xy, c):
    cx, cy = chip_xy
    return pltpu.make_async_remote_copy(
        src_ref=h_ref.at[2 * cx + cy], dst_ref=land_ref.at[j], send_sem=send_sems.at[j], recv_sem=recv_sems.at[j],
        device_id=(cx, cy, c), device_id_type=MESH)


def _scatter_start(name, h):
    def body(h_in, h_ref, land_ref, send_sems, recv_sems, token):
        x, y, c, chips = _mesh_place()
        for j, chip_xy in enumerate(chips):
            _scatter_copy(h_ref, land_ref, send_sems, recv_sems, j, chip_xy, c).start()
        token[...] = jnp.zeros_like(token)

    return pl.pallas_call(
        body, name=name, in_specs=[HBM_SPEC], out_specs=[HBM_SPEC, HBM_SPEC, SEM_SPEC, SEM_SPEC, TOKEN_SPEC],
        out_shape=[pltpu.HBM(h.shape, h.dtype), pltpu.HBM((N_OTHER,) + h.shape[1:], h.dtype),
                   pltpu.SemaphoreType.DMA((N_OTHER,)), pltpu.SemaphoreType.DMA((N_OTHER,)), TOKEN_SHAPE],
        input_output_aliases={0: 0}, compiler_params=SPLIT_COPY,
    )(_in_hbm(h))


def _scatter_wait(name, h, land, send_sems, recv_sems, after):
    afters = tuple(after) if isinstance(after, (tuple, list)) else (after,)

    def body(h_in, land_in, send_sems, recv_sems, *rest):
        h_ref, land_ref = rest[-2:]
        x, y, c, chips = _mesh_place()
        for j, chip_xy in enumerate(chips):
            copy = _scatter_copy(h_ref, land_ref, send_sems, recv_sems, j, chip_xy, c)
            copy.wait_send()
            copy.wait_recv()

    return pl.pallas_call(
        body, name=name, in_specs=[HBM_SPEC, HBM_SPEC, SEM_SPEC, SEM_SPEC] + [ANY_SPEC] * len(afters),
        out_specs=[HBM_SPEC, HBM_SPEC],
        out_shape=[pltpu.HBM(h.shape, h.dtype), pltpu.HBM(land.shape, land.dtype)],
        input_output_aliases={0: 0, 1: 1}, compiler_params=SPLIT_COPY,
    )(h, land, send_sems, recv_sems, *afters)


def _sum_chips(name, hs, rcv, core, chip, layer, n_layers, prev):
    _, r2, cols = hs.shape
    tr = _row_tile(r2, cols, budget=1 << 20)
    nrt = r2 // tr

    def body(core_ref, chip_ref, h_ref, r_ref, *rest):
        o_ref = rest[-1]
        acc = h_ref[...].astype(F32)
        for j in range(N_CHIPS - 1):
            acc = acc + r_ref[j].astype(F32)
        o_ref[...] = acc

    in_specs = [pl.BlockSpec((None, tr, cols), lambda i, core_ref, chip_ref: (chip_ref[0], i, 0)),
                pl.BlockSpec((N_CHIPS - 1, tr, cols), lambda i, core_ref, chip_ref: (0, i, 0))]
    args = [core, chip, hs, rcv]
    aliases = {}
    if prev is not None:
        in_specs.append(pl.BlockSpec(memory_space=pl.ANY))
        args.append(prev)
        aliases = {4: 0}
    return pl.pallas_call(
        body, name=name,
        grid_spec=pltpu.PrefetchScalarGridSpec(
            num_scalar_prefetch=2, grid=(nrt,), in_specs=in_specs,
            out_specs=pl.BlockSpec((None, tr, cols), lambda i, core_ref, chip_ref: (layer, core_ref[0] * nrt + i, 0))),
        out_shape=jax.ShapeDtypeStruct((n_layers, 2 * r2, cols), F32), input_output_aliases=aliases,
        compiler_params=_params("parallel"),
    )(*args)


def _join_core_halves(name, ts, deps=()):
    na, nd = len(ts), len(deps)

    def body(*refs):
        o_refs = refs[na + nd:2 * na + nd]
        send_sems, recv_sems = refs[2 * na + nd:]
        x, y, c, _ = _mesh_place()
        copies = []
        for a in range(na):
            r2 = o_refs[a].shape[1] // 2
            mine = o_refs[a].at[:, pl.ds(c * r2, r2), :]
            copies.append(pltpu.make_async_remote_copy(
                src_ref=mine, dst_ref=mine, send_sem=send_sems.at[a], recv_sem=recv_sems.at[a],
                device_id=(x, y, 1 - c), device_id_type=MESH))
        for cp in copies:
            cp.start()
        for cp in copies:
            cp.wait()

    return pl.pallas_call(
        body, name=name, in_specs=[HBM_SPEC] * na + [ANY_SPEC] * nd, out_specs=[HBM_SPEC] * na,
        out_shape=[jax.ShapeDtypeStruct(t.shape, t.dtype) for t in ts],
        input_output_aliases={a: a for a in range(na)},
        scratch_shapes=[pltpu.SemaphoreType.DMA((na,))] * 2,
    )(*ts, *deps)


def _allreduce_small(p):
    n, _, w = p.shape

    def body(p_ref, o_ref, buf, send_sems, recv_sems):
        x, y, c, _ = _mesh_place()
        me = 4 * x + 2 * y + c
        buf[me] = jnp.sum(p_ref[...], axis=1)
        copies = []
        for pat in range(1, N_DEV):
            fx, fy, fc = (pat >> 2) & 1, (pat >> 1) & 1, pat & 1
            copies.append(pltpu.make_async_remote_copy(
                src_ref=buf.at[me], dst_ref=buf.at[me], send_sem=send_sems.at[pat - 1], recv_sem=recv_sems.at[pat - 1],
                device_id=(x ^ fx, y ^ fy, c ^ fc), device_id_type=MESH))
        for cp in copies:
            cp.start()
        for cp in copies:
            cp.wait()
        acc = buf[0]
        for dev in range(1, N_DEV):
            acc = acc + buf[dev]
        o_ref[...] = acc

    return pl.pallas_call(
        body, name="allreduce_small", in_specs=[pl.BlockSpec(memory_space=pltpu.VMEM)],
        out_specs=pl.BlockSpec(memory_space=pltpu.VMEM), out_shape=jax.ShapeDtypeStruct((n, w), F32),
        scratch_shapes=[pltpu.VMEM((N_DEV, n, w), F32), pltpu.SemaphoreType.DMA((N_DEV - 1,)),
                        pltpu.SemaphoreType.DMA((N_DEV - 1,))],
    )(p)


class _WeightFeed:
    def __init__(self):
        self.fulls, self.ici_send, self.ici_recv, self.d2d = [], [], [], []

    def start(self, name, fulls, after):
        started, send, recv, token = _gather_start(name, fulls, after)
        self.fulls += started
        self.ici_send += send
        self.ici_recv += recv
        self.d2d += [None] * len(fulls)
        return token

    def _pass_on(self, k, after):
        if k < len(self.fulls) and self.d2d[k] is None:
            self.fulls[k], send, recv = _gather_pass_on(f"gather_pass_{k}", self.fulls[k], self.ici_recv[k], after)
            self.d2d[k] = (send, recv)

    def take(self, k, after):
        self._pass_on(k, after)
        self._pass_on(k + 1, after)
        if k + 1 < len(self.fulls):
            after = self.fulls[k + 1]
        self.fulls[k] = _gather_arrive(f"gather_arrive_{k}", self.fulls[k], self.ici_send[k], *self.d2d[k], after)
        return self.fulls[k]


def _ffn_forward(tag, x, g_pre, g_post, feed, k):
    s, d = x.shape
    h = _norm_fwd(f"{tag}_norm", x, g_pre)
    gu_w = feed.take(k, h)
    gu, a = _ffn_up(f"{tag}_up", h, gu_w)
    dn_w = feed.take(k + 1, a).reshape(-1, d)
    f = dn_w.shape[0]
    tk = _tile(f, 1408)
    tm, tn = _tile(s, 1024), _tile(d, 1024)
    y = _mm(f"{tag}_down", a, dn_w, mode="nn", grid=(s // tm, d // tn, f // tk),
            a_spec=pl.BlockSpec((tm, tk), lambda i, j, k: (i, k)),
            b_spec=pl.BlockSpec((tk, tn), lambda i, j, k: (k, j)),
            o_spec=pl.BlockSpec((tm, tn), lambda i, j, k: (i, j)),
            out_shape=jax.ShapeDtypeStruct((s, d), F32), nk=f // tk, acc_shape=(tm, tn))
    x_new = _res_norm(f"{tag}_post", x, y, g_post, FFN_RESIDUAL_WEIGHT)
    return x_new, (x, h, gu, a, y)


class _GradReduce:
    def __init__(self, core, chip, n_layers):
        self.core, self.chip, self.n_layers = core, chip, n_layers
        self.state = {}
        self.bufs = {}

    def start(self, kind, layer, g):
        g, land, send, recv, token = _swap_start(f"swap_start_{kind}_{layer}", g)
        self.state[kind, layer] = (g, land, send, recv)
        return token

    def exchange(self, kind, layer, after):
        tag = f"{kind}_{layer}"
        g, sib = _swap_wait(f"swap_wait_{tag}", *self.state[kind, layer], after)
        h = _add_core_halves(f"add_cores_{tag}", g, sib, self.core)
        h, land, send, recv, token = _scatter_start(f"scatter_start_{tag}", h)
        self.state[kind, layer] = (h, land, send, recv)
        return token

    def finish(self, kind, layer, after):
        tag = f"{kind}_{layer}"
        h, rcv = _scatter_wait(f"scatter_wait_{tag}", *self.state.pop((kind, layer)), after)
        self.bufs[kind] = _sum_chips(f"sum_chips_{tag}", h, rcv, self.core, self.chip, layer, self.n_layers,
                                     self.bufs.get(kind))
        return self.bufs[kind]


def _ffn_backward(tag, dx_new, saved, g_pre, g_post, gu_w, dn_w, red, kinds, layer, deps):
    x, h, gu, a, y = saved
    s, d = x.shape
    nb, fs = gu_w.shape[0], gu_w.shape[2]
    f = dn_w.shape[0]
    fr = f // nb
    dy, dg_post = _norm_bwd(f"{tag}_post_bwd", dx_new, y, g_post, FFN_RESIDUAL_WEIGHT, None, BF16)
    dgu = _ffn_dact(f"{tag}_dact", dy, dn_w, gu, deps)
    dgu4 = dgu.reshape(nb, s, fs)
    tn = _tile(d, 1024)
    d_wd = _mm(f"{tag}_dwd", a, dy, mode="tn", grid=(nb, d // tn),
               a_spec=pl.BlockSpec((s, fr), lambda i, j: (0, i)),
               b_spec=pl.BlockSpec((s, tn), lambda i, j: (0, j)),
               o_spec=pl.BlockSpec((None, fr, tn), lambda i, j: (i, 0, j)),
               out_shape=jax.ShapeDtypeStruct((nb, fr, d), BF16))
    tm, tw = _tile(d, 512), _tile(fs, 1408)
    nw = fs // tw
    d_wgu = _mm(f"{tag}_dwgu", h, dgu4, mode="tn", grid=(nb, nw, d // tm),
                a_spec=pl.BlockSpec((s, tm), lambda k, j, i: (0, i)),
                b_spec=pl.BlockSpec((None, s, tw), lambda k, j, i: (k, 0, j)),
                o_spec=pl.BlockSpec((None, tm, tw), lambda k, j, i: (k, i, j)),
                out_shape=jax.ShapeDtypeStruct((nb, d, fs), BF16))
    started = (red.start(kinds[0], layer, d_wgu), red.start(kinds[1], layer, d_wd))
    ts, td = _tile(s, 1024), _tile(d, 1024)
    dh = _mm(f"{tag}_dh", dgu4, gu_w, mode="nt", grid=(s // ts, d // td, nb),
             a_spec=pl.BlockSpec((None, ts, fs), lambda i, j, k: (k, i, 0)),
             b_spec=pl.BlockSpec((None, td, fs), lambda i, j, k: (k, j, 0)),
             o_spec=pl.BlockSpec((ts, td), lambda i, j, k: (i, j)),
             out_shape=jax.ShapeDtypeStruct((s, d), F32), nk=nb, acc_shape=(ts, td), deps=started)
    dx, dg_pre = _norm_bwd(f"{tag}_pre_bwd", dh, x, g_pre, 1.0, dx_new, F32)
    return dx, dg_pre, dg_post


def _mixer_forward(tag, x, gains, feed, k, conv_taps, dims):
    qd, kvd, cd = dims
    s, d = x.shape
    g_pre, g_a, g_c, g_post = gains
    h = _norm_fwd(f"{tag}_norm", x, g_pre)
    win_w = feed.take(k, h)
    nb, cw = win_w.shape[0], win_w.shape[2]
    tm = _tile(s, 1024)
    z = _mm(f"{tag}_in", h, win_w, mode="nn", grid=(nb, s // tm),
            a_spec=pl.BlockSpec((tm, d), lambda j, i: (i, 0)),
            b_spec=pl.BlockSpec((None, d, cw), lambda j, i: (j, 0, 0)),
            o_spec=pl.BlockSpec((tm, cw), lambda j, i: (i, j)),
            out_shape=jax.ShapeDtypeStruct((s, nb * cw), BF16))
    a, lse = _attn_fwd(f"{tag}_attn", z, qd, kvd)
    c = _conv_fwd(f"{tag}_conv", z, conv_taps, qd + 2 * kvd, cd)
    cat = _cat_norm_fwd(f"{tag}_cat", a, c, g_a, g_c)
    wout_w = feed.take(k + 1, cat).reshape(-1, d)
    mw = qd + cd
    tn = _tile(d, 1024)
    mixed = _mm(f"{tag}_out", cat, wout_w, mode="nn", grid=(s // tm, d // tn),
                a_spec=pl.BlockSpec((tm, mw), lambda i, j: (i, 0)),
                b_spec=pl.BlockSpec((mw, tn), lambda i, j: (0, j)),
                o_spec=pl.BlockSpec((tm, tn), lambda i, j: (i, j)),
                out_shape=jax.ShapeDtypeStruct((s, d), F32))
    x_new = _res_norm(f"{tag}_post", x, mixed, g_post, 1.0)
    return x_new, (x, h, z, a, lse, c, cat, mixed)


def _mixer_backward(tag, dx_new, saved, gains, win_w, conv_taps, wout_w, dims, red, kinds, layer, deps):
    qd, kvd, cd = dims
    x, h, z, a, lse, c, cat, mixed = saved
    s, d = x.shape
    nb, cw = win_w.shape[0], win_w.shape[2]
    g_pre, g_a, g_c, g_post = gains
    mw = qd + cd
    dmixed, dg_post = _norm_bwd(f"{tag}_post_bwd", dx_new, mixed, g_post, 1.0, None, BF16)
    tm, tn = _tile(s, 1024), _tile(mw, 1024)
    dcat = _mm(f"{tag}_dcat", dmixed, wout_w, mode="nt", grid=(s // tm, mw // tn),
               a_spec=pl.BlockSpec((tm, d), lambda i, j: (i, 0)),
               b_spec=pl.BlockSpec((tn, d), lambda i, j: (j, 0)),
               o_spec=pl.BlockSpec((tm, tn), lambda i, j: (i, j)),
               out_shape=jax.ShapeDtypeStruct((s, mw), F32), deps=deps)
    wr = mw // nb
    td = _tile(d, 1024)
    d_wout = _mm(f"{tag}_dwout", cat, dmixed, mode="tn", grid=(nb, d // td),
                 a_spec=pl.BlockSpec((s, wr), lambda i, j: (0, i)),
                 b_spec=pl.BlockSpec((s, td), lambda i, j: (0, j)),
                 o_spec=pl.BlockSpec((None, wr, td), lambda i, j: (i, 0, j)),
                 out_shape=jax.ShapeDtypeStruct((nb, wr, d), BF16))
    da, dc, dg_a, dg_c = _cat_norm_bwd(f"{tag}_cat_bwd", dcat, a, c, g_a, g_c)
    dhc, dbg, dcg, d_taps = _conv_bwd(f"{tag}_conv_bwd", z, conv_taps, dc, qd + 2 * kvd, cd)
    dq, dk, dv = _attn_bwd(f"{tag}_attn_bwd", z, a, lse, da, qd, kvd)
    dz = jnp.concatenate([dq, dk, dv, dhc, dbg, dcg], axis=1)
    th = _tile(d, 512)
    d_win = _mm(f"{tag}_dwin", h, dz, mode="tn", grid=(nb, d // th),
                a_spec=pl.BlockSpec((s, th), lambda k, i: (0, i)),
                b_spec=pl.BlockSpec((s, cw), lambda k, i: (0, k)),
                o_spec=pl.BlockSpec((None, th, cw), lambda k, i: (k, i, 0)),
                out_shape=jax.ShapeDtypeStruct((nb, d, cw), BF16))
    started = (red.start(kinds[0], layer, d_win), red.start(kinds[1], layer, d_wout))
    dh = _mm(f"{tag}_dh", dz, win_w, mode="nt", grid=(s // tm, d // td, nb),
             a_spec=pl.BlockSpec((tm, cw), lambda i, j, k: (i, k)),
             b_spec=pl.BlockSpec((None, td, cw), lambda i, j, k: (k, j, 0)),
             o_spec=pl.BlockSpec((tm, td), lambda i, j, k: (i, j)),
             out_shape=jax.ShapeDtypeStruct((s, d), F32), nk=nb, acc_shape=(tm, td), deps=started)
    dx, dg_pre = _norm_bwd(f"{tag}_pre_bwd", dh, x, g_pre, 1.0, dx_new, F32)
    return dx, d_taps, (dg_pre, dg_a, dg_c, dg_post)


def _pad_cols(v, width):
    return jnp.pad(v, ((0, 0), (0, width - v.shape[1])))


def kernel(x, ffn1_norm_pre, ffn1_w_gate_up, ffn1_w_down, ffn1_norm_post, mix_norm_pre, w_in, conv_w, attn_out_norm, conv_out_norm, w_out, mix_norm_post, ffn2_norm_pre, ffn2_w_gate_up, ffn2_w_down, ffn2_norm_post, loss_target, m_ffn1_norm_pre, m_ffn1_w_gate_up, m_ffn1_w_down, m_ffn1_norm_post, m_mix_norm_pre, m_w_in, m_conv_w, m_attn_out_norm, m_conv_out_norm, m_w_out, m_mix_norm_post, m_ffn2_norm_pre, m_ffn2_w_gate_up, m_ffn2_w_down, m_ffn2_norm_post, v_ffn1_norm_pre, v_ffn1_w_gate_up, v_ffn1_w_down, v_ffn1_norm_post, v_mix_norm_pre, v_w_in, v_conv_w, v_attn_out_norm, v_conv_out_norm, v_w_out, v_mix_norm_post, v_ffn2_norm_pre, v_ffn2_w_gate_up, v_ffn2_w_down, v_ffn2_norm_post):
    _, s, d = x.shape
    n_layers = ffn1_norm_pre.shape[0]
    qd = attn_out_norm.shape[1]
    cd = conv_out_norm.shape[1]
    kvd = qd // Q_PER_KV
    dims = (qd, kvd, cd)
    assert N_CHIPS * w_in.shape[2] == qd + 2 * kvd + 3 * cd and qd + cd == N_CHIPS * w_out.shape[1]
    assert 2 * d <= SMALL_ROWS * LANES * SUBLANES
    chip = 2 * lax.axis_index("x") + lax.axis_index("y")
    chip_arr = chip.astype(jnp.int32).reshape(1)
    core = lax.axis_index("c").astype(jnp.int32).reshape(1)
    kinds = ("gu1", "dn1", "win", "wout", "gu2", "dn2")

    big = (ffn1_w_gate_up, ffn1_w_down, w_in, w_out, ffn2_w_gate_up, ffn2_w_down)
    nk = len(kinds)
    taps_all = _gather_taps(conv_w)
    feed = _WeightFeed()
    order = [(k, w, layer) for layer in range(n_layers) for k, w in zip(kinds, big)]
    k, w, layer = order[0]
    token = feed.start("gather_start_first", [_cast_into_slot(f"cast_{k}_{layer}", w, layer, chip_arr)], taps_all)
    feed.start("gather_start_rest", [_cast_into_slot(f"cast_{k}_{layer}", w, layer, chip_arr, (token,))
                                     for k, w, layer in order[1:]], token)
    taps = jnp.transpose(taps_all, (1, 2, 0, 3)).reshape(n_layers, CONV_WIDTH, cd)
    taps = jnp.pad(taps, ((0, 0), (0, SUBLANES - CONV_WIDTH), (0, 0)))

    def gain(g, layer):
        return g[layer][None, :]

    xs = x[0]
    saved = []
    for layer in range(n_layers):
        t = f"l{layer}"
        k0 = layer * nk
        xs, s1 = _ffn_forward(f"{t}_ffn1", xs, gain(ffn1_norm_pre, layer), gain(ffn1_norm_post, layer), feed, k0)
        mix_gains = (gain(mix_norm_pre, layer), gain(attn_out_norm, layer), gain(conv_out_norm, layer), gain(mix_norm_post, layer))
        xs, s2 = _mixer_forward(f"{t}_mix", xs, mix_gains, feed, k0 + 2, taps[layer], dims)
        xs, s3 = _ffn_forward(f"{t}_ffn2", xs, gain(ffn2_norm_pre, layer), gain(ffn2_norm_post, layer), feed, k0 + 4)
        saved.append((s1, s2, s3, mix_gains))
    wts = {k: [feed.fulls[layer * nk + i] for layer in range(n_layers)] for i, k in enumerate(kinds)}
    for k in ("dn1", "wout", "dn2"):
        wts[k] = [w.reshape(-1, d) for w in wts[k]]
    dxs, loss_part = _loss_head("loss_head", xs, loss_target[0])
    loss = lax.psum(jnp.sum(loss_part), ("x", "y", "c"))

    red = _GradReduce(core, chip_arr, n_layers)
    small = [None] * n_layers
    flow = {"deps": (), "in_flight": []}

    def between(dx, new_keys):
        after = dx
        for key in flow["in_flight"]:
            after = red.finish(*key, after)
        flow["deps"] = tuple(red.exchange(*key, after) for key in new_keys)
        flow["in_flight"] = list(new_keys)

    for layer in reversed(range(n_layers)):
        t = f"l{layer}"
        s1, s2, s3, mix_gains = saved[layer]
        dxs, p_pre2, p_post2 = _ffn_backward(
            f"{t}_ffn2", dxs, s3, gain(ffn2_norm_pre, layer), gain(ffn2_norm_post, layer),
            wts["gu2"][layer], wts["dn2"][layer], red, ("gu2", "dn2"), layer, flow["deps"])
        between(dxs, [("gu2", layer), ("dn2", layer)])
        dxs, p_taps, (p_mpre, p_a, p_c, p_mpost) = _mixer_backward(
            f"{t}_mix", dxs, s2, mix_gains, wts["win"][layer], taps[layer], wts["wout"][layer], dims,
            red, ("win", "wout"), layer, flow["deps"])
        between(dxs, [("win", layer), ("wout", layer)])
        dxs, p_pre1, p_post1 = _ffn_backward(
            f"{t}_ffn1", dxs, s1, gain(ffn1_norm_pre, layer), gain(ffn1_norm_post, layer),
            wts["gu1"][layer], wts["dn1"][layer], red, ("gu1", "dn1"), layer, flow["deps"])
        between(dxs, [("gu1", layer), ("dn1", layer)])
        tap_rows = jnp.zeros((CONV_WIDTH, SUBLANES, d), F32).at[:, 0, :cd].set(p_taps[:CONV_WIDTH])
        rows = [p_pre1, p_post1, p_mpre, jnp.concatenate([p_a, p_c], axis=1), p_mpost, p_pre2, p_post2]
        rows = jnp.concatenate([jnp.stack(rows), tap_rows], axis=0)
        small[layer] = jnp.pad(rows, ((0, SMALL_ROWS - rows.shape[0]), (0, 0), (0, 0)))
    grad_x = dxs[None]

    weights = dict(ffn1_norm_pre=ffn1_norm_pre, ffn1_w_gate_up=ffn1_w_gate_up, ffn1_w_down=ffn1_w_down, ffn1_norm_post=ffn1_norm_post, mix_norm_pre=mix_norm_pre, w_in=w_in, conv_w=conv_w, attn_out_norm=attn_out_norm, conv_out_norm=conv_out_norm, w_out=w_out, mix_norm_post=mix_norm_post, ffn2_norm_pre=ffn2_norm_pre, ffn2_w_gate_up=ffn2_w_gate_up, ffn2_w_down=ffn2_w_down, ffn2_norm_post=ffn2_norm_post)
    m_in = dict(ffn1_norm_pre=m_ffn1_norm_pre, ffn1_w_gate_up=m_ffn1_w_gate_up, ffn1_w_down=m_ffn1_w_down, ffn1_norm_post=m_ffn1_norm_post, mix_norm_pre=m_mix_norm_pre, w_in=m_w_in, conv_w=m_conv_w, attn_out_norm=m_attn_out_norm, conv_out_norm=m_conv_out_norm, w_out=m_w_out, mix_norm_post=m_mix_norm_post, ffn2_norm_pre=m_ffn2_norm_pre, ffn2_w_gate_up=m_ffn2_w_gate_up, ffn2_w_down=m_ffn2_w_down, ffn2_norm_post=m_ffn2_norm_post)
    v_in = dict(ffn1_norm_pre=v_ffn1_norm_pre, ffn1_w_gate_up=v_ffn1_w_gate_up, ffn1_w_down=v_ffn1_w_down, ffn1_norm_post=v_ffn1_norm_post, mix_norm_pre=v_mix_norm_pre, w_in=v_w_in, conv_w=v_conv_w, attn_out_norm=v_attn_out_norm, conv_out_norm=v_conv_out_norm, w_out=v_w_out, mix_norm_post=v_mix_norm_post, ffn2_norm_pre=v_ffn2_norm_pre, ffn2_w_gate_up=v_ffn2_w_gate_up, ffn2_w_down=v_ffn2_w_down, ffn2_norm_post=v_ffn2_norm_post)
    kind_name = dict(gu1="ffn1_w_gate_up", dn1="ffn1_w_down", win="w_in", wout="w_out", gu2="ffn2_w_gate_up", dn2="ffn2_w_down")
    delta, new_m, new_v, grad = {}, {}, {}, {}

    def update(kind_list, joined):
        for k, g in zip(kind_list, joined):
            n = kind_name[k]
            delta[n], new_m[n], new_v[n], grad[n] = _adamw(f"adamw_{n}", weights[n], g, m_in[n], v_in[n], True)

    early = ("gu2", "dn2", "win", "wout")
    update(early, _join_core_halves("join_early", [red.bufs[k] for k in early], flow["deps"]))
    late = [k for (k, _) in flow["in_flight"]]
    for k, layer in flow["in_flight"]:
        red.finish(k, layer, [delta[kind_name[e]] for e in early])
    update(late, _join_core_halves("join_late", [red.bufs[k] for k in late]))

    small_sum = _allreduce_small(jnp.concatenate(small, axis=0)).reshape(n_layers, SMALL_ROWS, d)
    g_ffn1_pre, g_ffn1_post, g_mix_pre = small_sum[:, 0], small_sum[:, 1], small_sum[:, 2]
    g_attn_out, g_conv_out = small_sum[:, 3, :qd], small_sum[:, 3, qd:qd + cd]
    g_mix_post, g_ffn2_pre, g_ffn2_post = small_sum[:, 4], small_sum[:, 5], small_sum[:, 6]
    cc = conv_w.shape[2]
    g_conv = lax.dynamic_slice_in_dim(small_sum[:, 7:7 + CONV_WIDTH, :cd], chip * cc, cc, axis=2)

    grad.update(ffn1_norm_pre=g_ffn1_pre, ffn1_norm_post=g_ffn1_post, mix_norm_pre=g_mix_pre, conv_w=g_conv, attn_out_norm=g_attn_out, conv_out_norm=g_conv_out, mix_norm_post=g_mix_post, ffn2_norm_pre=g_ffn2_pre, ffn2_norm_post=g_ffn2_post)
    names = list(weights)

    vectors = [n for n in names if n not in kind_name.values()]

    def pack(tree):
        flat = jnp.concatenate([tree[n].reshape(-1) for n in vectors])
        return jnp.pad(flat, (0, -flat.size % (SUBLANES * LANES))).reshape(-1, LANES)

    packed = _adamw("adamw_small", pack(weights), pack(grad), pack(m_in), pack(v_in))
    offset = 0
    for n in vectors:
        size = weights[n].size
        for tree, flat in zip((delta, new_m, new_v), packed):
            tree[n] = flat.reshape(-1)[offset:offset + size].reshape(weights[n].shape)
        offset += size

    return (loss, grad_x, *[grad[n] for n in names], *[delta[n] for n in names],
            *[new_m[n] for n in names], *[new_v[n] for n in names])
```

```python
import functools

import jax
import jax.numpy as jnp
from jax import lax
from jax.experimental import pallas as pl
from jax.experimental.pallas import tpu as pltpu

F32 = jnp.float32
BF16 = jnp.bfloat16
MESH = pl.DeviceIdType.MESH

NORM_EPS = 1e-6
HEAD_DIM = 128
Q_PER_KV = 4
CONV_WIDTH = 3
FFN_RESIDUAL_WEIGHT = 0.5
DILATED_BRANCHES = ((128, 1), (512, 4), (2048, 16))
ADAM_LR = 0.001
ADAM_B1 = 0.9
ADAM_B2 = 0.999
ADAM_EPS = 1e-08
ADAM_WD = 0.01
ADAM_STEP = 10

N_CHIPS = 4
N_DEV = 8
V7X_VMEM_BYTES = 64 << 20
VMEM_LIMIT = V7X_VMEM_BYTES - (12 << 20)
SUBLANES = 8
LANES = 128
SMALL_ROWS = 16


def _params(*sem):
    return pltpu.CompilerParams(dimension_semantics=sem, vmem_limit_bytes=VMEM_LIMIT)


def _row_tile(rows, cols, itemsize=4, budget=2 << 20):
    t = rows
    while t * cols * itemsize > budget and t % 32 == 0:
        t //= 2
    return t


def _sum_to_sublanes(v):
    r, n = v.shape
    return v.reshape(r // SUBLANES, SUBLANES, n).sum(axis=0)


_DIMS = {
    "nn": (((1,), (0,)), ((), ())),
    "nt": (((1,), (1,)), ((), ())),
    "tn": (((0,), (0,)), ((), ())),
}


ANY_SPEC = pl.BlockSpec(memory_space=pl.ANY)


def _dot(a, b, mode):
    return lax.dot_general(a, b, _DIMS[mode], preferred_element_type=F32)


def _mm(name, a, b, *, mode, grid, a_spec, b_spec, o_spec, out_shape, nk=1, acc_shape=None, deps=()):
    nd = len(deps)

    def body(a_ref, b_ref, *rest):
        o_ref, scratch = rest[nd], rest[nd + 1:]
        r = _dot(a_ref[...], b_ref[...], mode)
        if nk == 1:
            o_ref[...] = r.astype(o_ref.dtype)
        else:
            acc = scratch[0]
            k = pl.program_id(len(grid) - 1)

            @pl.when(k == 0)
            def _():
                acc[...] = r

            @pl.when(k > 0)
            def _():
                acc[...] += r

            @pl.when(k == nk - 1)
            def _():
                o_ref[...] = acc[...].astype(o_ref.dtype)

    sem = ("parallel",) * (len(grid) - (1 if nk > 1 else 0)) + (("arbitrary",) if nk > 1 else ())
    return pl.pallas_call(
        body, name=name, grid=grid, in_specs=[a_spec, b_spec] + [ANY_SPEC] * nd, out_specs=o_spec,
        out_shape=out_shape, scratch_shapes=[pltpu.VMEM(acc_shape, F32)] if nk > 1 else [],
        compiler_params=_params(*sem),
    )(a, b, *deps)


def _tile(n, want):
    if n <= want:
        return n
    best = None
    for t in range(LANES, want + 1, LANES):
        if n % t == 0:
            best = t
    assert best is not None, (n, want)
    return best


def _norm_fwd(name, x, gain):
    s, d = x.shape
    tr = _row_tile(s, d)

    def body(x_ref, g_ref, o_ref):
        xv = x_ref[...]
        r = lax.rsqrt(jnp.mean(xv * xv, axis=-1, keepdims=True) + NORM_EPS)
        o_ref[...] = (xv * r * g_ref[...]).astype(o_ref.dtype)

    return pl.pallas_call(
        body, name=name, grid=(s // tr,),
        in_specs=[pl.BlockSpec((tr, d), lambda i: (i, 0)), pl.BlockSpec((1, d), lambda i: (0, 0))],
        out_specs=pl.BlockSpec((tr, d), lambda i: (i, 0)),
        out_shape=jax.ShapeDtypeStruct((s, d), BF16), compiler_params=_params("parallel"),
    )(x, gain)


def _res_norm(name, x, y, gain, scale):
    s, d = x.shape
    tr = _row_tile(s, d)

    def body(x_ref, y_ref, g_ref, o_ref):
        yv = y_ref[...]
        r = lax.rsqrt(jnp.mean(yv * yv, axis=-1, keepdims=True) + NORM_EPS)
        o_ref[...] = x_ref[...] + scale * (yv * r * g_ref[...])

    row = pl.BlockSpec((tr, d), lambda i: (i, 0))
    return pl.pallas_call(
        body, name=name, grid=(s // tr,),
        in_specs=[row, row, pl.BlockSpec((1, d), lambda i: (0, 0))], out_specs=row,
        out_shape=jax.ShapeDtypeStruct((s, d), F32), compiler_params=_params("parallel"),
    )(x, y, gain)


def _norm_bwd(name, dout, yin, gain, scale, resid, out_dtype):
    s, d = yin.shape
    tr = _row_tile(s, d)
    has_resid = resid is not None

    def body(*refs):
        if has_resid:
            do_ref, y_ref, g_ref, r_ref, di_ref, dg_ref = refs
        else:
            do_ref, y_ref, g_ref, di_ref, dg_ref = refs
        yv = y_ref[...]
        r = lax.rsqrt(jnp.mean(yv * yv, axis=-1, keepdims=True) + NORM_EPS)
        xhat = yv * r
        dn = scale * do_ref[...]
        part = _sum_to_sublanes(dn * xhat)

        @pl.when(pl.program_id(0) == 0)
        def _():
            dg_ref[...] = part

        @pl.when(pl.program_id(0) > 0)
        def _():
            dg_ref[...] += part

        dxn = dn * g_ref[...]
        din = r * (dxn - xhat * jnp.mean(dxn * xhat, axis=-1, keepdims=True))
        if has_resid:
            din = din + r_ref[...]
        di_ref[...] = din.astype(di_ref.dtype)

    row = pl.BlockSpec((tr, d), lambda i: (i, 0))
    vec = pl.BlockSpec((1, d), lambda i: (0, 0))
    ins = [row, row, vec] + ([row] if has_resid else [])
    args = (dout, yin, gain) + ((resid,) if has_resid else ())
    return pl.pallas_call(
        body, name=name, grid=(s // tr,), in_specs=ins,
        out_specs=[row, pl.BlockSpec((SUBLANES, d), lambda i: (0, 0))],
        out_shape=[jax.ShapeDtypeStruct((s, d), out_dtype), jax.ShapeDtypeStruct((SUBLANES, d), F32)],
        compiler_params=_params("arbitrary"),
    )(*args)


def _loss_head(name, y, target):
    s, d = y.shape
    tr = _row_tile(s, d)

    def body(y_ref, t_ref, dy_ref, l_ref):
        e = y_ref[...] - t_ref[...]
        dy_ref[...] = e * (1.0 / d)
        part = _sum_to_sublanes(e * e) * (0.5 / d)

        @pl.when(pl.program_id(0) == 0)
        def _():
            l_ref[...] = part

        @pl.when(pl.program_id(0) > 0)
        def _():
            l_ref[...] += part

    row = pl.BlockSpec((tr, d), lambda i: (i, 0))
    return pl.pallas_call(
        body, name=name, grid=(s // tr,), in_specs=[row, row],
        out_specs=[row, pl.BlockSpec((SUBLANES, d), lambda i: (0, 0))],
        out_shape=[jax.ShapeDtypeStruct((s, d), F32), jax.ShapeDtypeStruct((SUBLANES, d), F32)],
        compiler_params=_params("arbitrary"),
    )(y, target)


def _ffn_up(name, h, gu_w):
    s, d = h.shape
    nb, _, fs = gu_w.shape
    hb = nb // 2
    w = gu_w.reshape(2, hb, d, fs)
    tm = _tile(s, 512)
    tn = _tile(fs, 1408)
    nj = fs // tn

    def body(h_ref, w_ref, gu_ref, a_ref):
        hv = h_ref[...]
        g = _dot(hv, w_ref[0], "nn")
        u = _dot(hv, w_ref[1], "nn")
        gu_ref[0] = g.astype(gu_ref.dtype)
        gu_ref[1] = u.astype(gu_ref.dtype)
        a_ref[...] = (g * jax.nn.sigmoid(g) * u).astype(a_ref.dtype)

    return pl.pallas_call(
        body, name=name, grid=(hb, nj, s // tm),
        in_specs=[pl.BlockSpec((tm, d), lambda jb, jo, i: (i, 0)),
                  pl.BlockSpec((2, None, d, tn), lambda jb, jo, i: (0, jb, 0, jo))],
        out_specs=[pl.BlockSpec((2, None, tm, tn), lambda jb, jo, i: (0, jb, i, jo)),
                   pl.BlockSpec((tm, tn), lambda jb, jo, i: (i, jb * nj + jo))],
        out_shape=[jax.ShapeDtypeStruct((2, hb, s, fs), BF16), jax.ShapeDtypeStruct((s, hb * fs), BF16)],
        compiler_params=_params("parallel", "parallel", "parallel"),
    )(h, w)


def _ffn_dact(name, dy, dn_w, gu, deps=()):
    s, d = dy.shape
    _, hb, _, fs = gu.shape
    tm = _tile(s, 512)
    tn = _tile(fs, 1408)
    nj = fs // tn

    def body(dy_ref, w_ref, gu_ref, *rest):
        o_ref = rest[-1]
        da = _dot(dy_ref[...], w_ref[...], "nt")
        g = gu_ref[0].astype(F32)
        u = gu_ref[1].astype(F32)
        sg = jax.nn.sigmoid(g)
        o_ref[0] = (da * u * (sg * (1.0 + g * (1.0 - sg)))).astype(o_ref.dtype)
        o_ref[1] = (da * (g * sg)).astype(o_ref.dtype)

    blk = pl.BlockSpec((2, None, tm, tn), lambda jb, jo, i: (0, jb, i, jo))
    return pl.pallas_call(
        body, name=name, grid=(hb, nj, s // tm),
        in_specs=[pl.BlockSpec((tm, d), lambda jb, jo, i: (i, 0)),
                  pl.BlockSpec((tn, d), lambda jb, jo, i: (jb * nj + jo, 0)),
                  blk] + [ANY_SPEC] * len(deps),
        out_specs=blk, out_shape=jax.ShapeDtypeStruct(gu.shape, BF16),
        compiler_params=_params("parallel", "parallel", "parallel"),
    )(dy, dn_w, gu, *deps)


def _multiplicity(q0, k0, tq, tk):
    row = q0 + lax.broadcasted_iota(jnp.int32, (tq, tk), 0)
    col = k0 + lax.broadcasted_iota(jnp.int32, (tq, tk), 1)
    dist = row - col
    mult = jnp.zeros((tq, tk), F32)
    for window, dilation in DILATED_BRANCHES:
        hit = (dist <= window) & ((dist & (dilation - 1)) == 0)
        mult = mult + hit.astype(F32)
    return jnp.where(dist >= 0, mult, 0.0)


_MASKED = -1e30


def _attn_specs(s, qd, kvd, tq):
    rw = Q_PER_KV * HEAD_DIM
    qspec = pl.BlockSpec((tq, rw), lambda g, i: (i, g))
    kspec = pl.BlockSpec((s, HEAD_DIM), lambda g, i: (0, qd // HEAD_DIM + g))
    vspec = pl.BlockSpec((s, HEAD_DIM), lambda g, i: (0, (qd + kvd) // HEAD_DIM + g))
    return rw, qspec, kspec, vspec


def _attn_fwd(name, z, qd, kvd):
    s = z.shape[0]
    tq = _tile(s, 256)
    nkv = kvd // HEAD_DIM
    rw, qspec, kspec, vspec = _attn_specs(s, qd, kvd, tq)
    scale = HEAD_DIM ** -0.5

    def body(q_ref, k_ref, v_ref, o_ref, l_ref):
        i = pl.program_id(1)
        heads = [slice(h * HEAD_DIM, (h + 1) * HEAD_DIM) for h in range(Q_PER_KV)]
        q_all = jnp.concatenate([q_ref[:, cols] for cols in heads], axis=0)

        def chunk(j, carry):
            mx, den, acc = carry
            k0 = pl.multiple_of(j * tq, tq)
            kc, vc = k_ref[pl.ds(k0, tq), :], v_ref[pl.ds(k0, tq), :]
            mult = jnp.tile(_multiplicity(i * tq, k0, tq, tq), (Q_PER_KV, 1))
            sc = jnp.where(mult > 0.0, _dot(q_all, kc, "nt") * scale, _MASKED)
            mx_new = jnp.maximum(mx, jnp.max(sc, axis=-1, keepdims=True))
            alpha = jnp.exp(mx - mx_new)
            p = jnp.exp(sc - mx_new) * mult
            return (mx_new, alpha * den + jnp.sum(p, axis=-1, keepdims=True),
                    alpha * acc + _dot(p.astype(BF16), vc, "nn"))

        rows = Q_PER_KV * tq
        init = (jnp.full((rows, 1), _MASKED, F32), jnp.zeros((rows, 1), F32), jnp.zeros((rows, HEAD_DIM), F32))
        mx, den, acc = lax.fori_loop(0, i + 1, chunk, init)
        out = acc / den
        lse = mx + jnp.log(den)
        for h, cols in enumerate(heads):
            o_ref[:, cols] = out[h * tq:(h + 1) * tq]
            l_ref[:, cols] = jnp.broadcast_to(lse[h * tq:(h + 1) * tq], (tq, HEAD_DIM))

    return pl.pallas_call(
        body, name=name, grid=(nkv, s // tq), in_specs=[qspec, kspec, vspec], out_specs=[qspec, qspec],
        out_shape=[jax.ShapeDtypeStruct((s, qd), F32), jax.ShapeDtypeStruct((s, qd), F32)],
        compiler_params=_params("parallel", "parallel"),
    )(z, z, z)


def _attn_bwd(name, z, o, lse, do, qd, kvd):
    s = z.shape[0]
    tq = _tile(s, 256)
    nkv = kvd // HEAD_DIM
    nq = s // tq
    rw, qspec, kspec, vspec = _attn_specs(s, qd, kvd, tq)
    scale = HEAD_DIM ** -0.5

    def body(q_ref, k_ref, v_ref, o_ref, l_ref, do_ref, dq_ref, dk_ref, dv_ref, dk_acc, dv_acc):
        i = pl.program_id(1)
        heads = [slice(h * HEAD_DIM, (h + 1) * HEAD_DIM) for h in range(Q_PER_KV)]

        @pl.when(i == 0)
        def _():
            dk_acc[...] = jnp.zeros_like(dk_acc)
            dv_acc[...] = jnp.zeros_like(dv_acc)

        q_all = jnp.concatenate([q_ref[:, cols] for cols in heads], axis=0)
        do_all = jnp.concatenate([do_ref[:, cols].astype(BF16) for cols in heads], axis=0)
        lse_all = jnp.concatenate([l_ref[:, cols][:, :1] for cols in heads], axis=0)
        delta_all = jnp.concatenate(
            [jnp.sum(do_ref[:, cols] * o_ref[:, cols], axis=-1, keepdims=True) for cols in heads], axis=0)

        def chunk(j, dq):
            k0 = pl.multiple_of(j * tq, tq)
            kc, vc = k_ref[pl.ds(k0, tq), :], v_ref[pl.ds(k0, tq), :]
            mult = jnp.tile(_multiplicity(i * tq, k0, tq, tq), (Q_PER_KV, 1))
            sc = jnp.where(mult > 0.0, _dot(q_all, kc, "nt") * scale, _MASKED)
            p = jnp.exp(sc - lse_all) * mult
            ds = (p * (_dot(do_all, vc, "nt") - delta_all) * scale).astype(BF16)
            dk_acc[pl.ds(k0, tq), :] += _dot(ds, q_all, "tn")
            dv_acc[pl.ds(k0, tq), :] += _dot(p.astype(BF16), do_all, "tn")
            return dq + _dot(ds, kc, "nn")

        dq = lax.fori_loop(0, i + 1, chunk, jnp.zeros((Q_PER_KV * tq, HEAD_DIM), F32))
        for h, cols in enumerate(heads):
            dq_ref[:, cols] = dq[h * tq:(h + 1) * tq].astype(dq_ref.dtype)

        @pl.when(i == nq - 1)
        def _():
            dk_ref[...] = dk_acc[...].astype(dk_ref.dtype)
            dv_ref[...] = dv_acc[...].astype(dv_ref.dtype)

    kvout = pl.BlockSpec((s, HEAD_DIM), lambda g, i: (0, g))
    return pl.pallas_call(
        body, name=name, grid=(nkv, nq), in_specs=[qspec, kspec, vspec, qspec, qspec, qspec],
        out_specs=[qspec, kvout, kvout],
        out_shape=[jax.ShapeDtypeStruct((s, qd), BF16), jax.ShapeDtypeStruct((s, kvd), BF16),
                   jax.ShapeDtypeStruct((s, kvd), BF16)],
        scratch_shapes=[pltpu.VMEM((s, HEAD_DIM), F32), pltpu.VMEM((s, HEAD_DIM), F32)],
        compiler_params=_params("parallel", "arbitrary"),
    )(z, z, z, o, lse, do)


def _shift_down(v, n):
    rolled = pltpu.roll(v, n, 0)
    t = lax.broadcasted_iota(jnp.int32, v.shape, 0)
    return jnp.where(t >= n, rolled, 0.0)


def _shift_up(v, n):
    rows = v.shape[0]
    rolled = pltpu.roll(v, rows - n, 0)
    t = lax.broadcasted_iota(jnp.int32, v.shape, 0)
    return jnp.where(t < rows - n, rolled, 0.0)


def _conv_specs(s, base, cd, tc):
    zs = [pl.BlockSpec((s, tc), functools.partial(lambda j, off: (0, off + j), off=(base + n * cd) // tc))
          for n in range(3)]
    wspec = pl.BlockSpec((SUBLANES, tc), lambda j: (0, j))
    cspec = pl.BlockSpec((s, tc), lambda j: (0, j))
    return zs, wspec, cspec


def _conv_fwd(name, z, conv_w, base, cd):
    s = z.shape[0]
    tc = _tile(cd, 256)
    zs, wspec, cspec = _conv_specs(s, base, cd, tc)

    def body(h_ref, b_ref, c_ref, w_ref, o_ref):
        u = c_ref[...].astype(F32) * h_ref[...].astype(F32)
        y = w_ref[0:1, :] * _shift_down(u, 2) + w_ref[1:2, :] * _shift_down(u, 1) + w_ref[2:3, :] * u
        o_ref[...] = b_ref[...].astype(F32) * y

    return pl.pallas_call(
        body, name=name, grid=(cd // tc,), in_specs=zs + [wspec], out_specs=cspec,
        out_shape=jax.ShapeDtypeStruct((s, cd), F32), compiler_params=_params("parallel"),
    )(z, z, z, conv_w)


def _conv_bwd(name, z, conv_w, dc, base, cd):
    s = z.shape[0]
    tc = _tile(cd, 256)
    zs, wspec, cspec = _conv_specs(s, base, cd, tc)

    def body(h_ref, b_ref, c_ref, w_ref, dc_ref, dh_ref, db_ref, dcg_ref, dw_ref):
        hv, bv, cv = h_ref[...].astype(F32), b_ref[...].astype(F32), c_ref[...].astype(F32)
        u = cv * hv
        u1, u2 = _shift_down(u, 1), _shift_down(u, 2)
        w0, w1, w2 = w_ref[0:1, :], w_ref[1:2, :], w_ref[2:3, :]
        y = w0 * u2 + w1 * u1 + w2 * u
        dcv = dc_ref[...]
        db_ref[...] = (dcv * y).astype(db_ref.dtype)
        dy = dcv * bv
        du = w2 * dy + w1 * _shift_up(dy, 1) + w0 * _shift_up(dy, 2)
        dh_ref[...] = (du * cv).astype(dh_ref.dtype)
        dcg_ref[...] = (du * hv).astype(dcg_ref.dtype)
        g0 = jnp.sum(dy * u2, axis=0, keepdims=True)
        g1 = jnp.sum(dy * u1, axis=0, keepdims=True)
        g2 = jnp.sum(dy * u, axis=0, keepdims=True)
        r = lax.broadcasted_iota(jnp.int32, (SUBLANES, tc), 0)
        dw_ref[...] = jnp.where(r == 0, g0, jnp.where(r == 1, g1, jnp.where(r == 2, g2, 0.0)))

    return pl.pallas_call(
        body, name=name, grid=(cd // tc,), in_specs=zs + [wspec, cspec],
        out_specs=[cspec, cspec, cspec, wspec],
        out_shape=[jax.ShapeDtypeStruct((s, cd), BF16)] * 3 + [jax.ShapeDtypeStruct((SUBLANES, cd), F32)],
        compiler_params=_params("parallel"),
    )(z, z, z, conv_w, dc)


def _cat_norm_fwd(name, a, c, ga, gc):
    s, qd = a.shape
    cd = c.shape[1]
    tr = _row_tile(s, qd + cd)

    def body(a_ref, c_ref, ga_ref, gc_ref, o_ref):
        av, cv = a_ref[...], c_ref[...]
        ra = lax.rsqrt(jnp.mean(av * av, axis=-1, keepdims=True) + NORM_EPS)
        rc = lax.rsqrt(jnp.mean(cv * cv, axis=-1, keepdims=True) + NORM_EPS)
        o_ref[:, :qd] = (av * ra * ga_ref[...]).astype(o_ref.dtype)
        o_ref[:, qd:] = (cv * rc * gc_ref[...]).astype(o_ref.dtype)

    return pl.pallas_call(
        body, name=name, grid=(s // tr,),
        in_specs=[pl.BlockSpec((tr, qd), lambda i: (i, 0)), pl.BlockSpec((tr, cd), lambda i: (i, 0)),
                  pl.BlockSpec((1, qd), lambda i: (0, 0)), pl.BlockSpec((1, cd), lambda i: (0, 0))],
        out_specs=pl.BlockSpec((tr, qd + cd), lambda i: (i, 0)),
        out_shape=jax.ShapeDtypeStruct((s, qd + cd), BF16), compiler_params=_params("parallel"),
    )(a, c, ga, gc)


def _cat_norm_bwd(name, dcat, a, c, ga, gc):
    s, qd = a.shape
    cd = c.shape[1]
    tr = _row_tile(s, qd + cd)

    def one(dn, yv, gv):
        r = lax.rsqrt(jnp.mean(yv * yv, axis=-1, keepdims=True) + NORM_EPS)
        xhat = yv * r
        dxn = dn * gv
        return r * (dxn - xhat * jnp.mean(dxn * xhat, axis=-1, keepdims=True)), _sum_to_sublanes(dn * xhat)

    def body(d_ref, a_ref, c_ref, ga_ref, gc_ref, da_ref, dc_ref, dga_ref, dgc_ref):
        da, pa = one(d_ref[:, :qd], a_ref[...], ga_ref[...])
        dc, pc = one(d_ref[:, qd:], c_ref[...], gc_ref[...])
        da_ref[...] = da
        dc_ref[...] = dc

        @pl.when(pl.program_id(0) == 0)
        def _():
            dga_ref[...] = pa
            dgc_ref[...] = pc

        @pl.when(pl.program_id(0) > 0)
        def _():
            dga_ref[...] += pa
            dgc_ref[...] += pc

    ra = pl.BlockSpec((tr, qd), lambda i: (i, 0))
    rc = pl.BlockSpec((tr, cd), lambda i: (i, 0))
    return pl.pallas_call(
        body, name=name, grid=(s // tr,),
        in_specs=[pl.BlockSpec((tr, qd + cd), lambda i: (i, 0)), ra, rc,
                  pl.BlockSpec((1, qd), lambda i: (0, 0)), pl.BlockSpec((1, cd), lambda i: (0, 0))],
        out_specs=[ra, rc, pl.BlockSpec((SUBLANES, qd), lambda i: (0, 0)),
                   pl.BlockSpec((SUBLANES, cd), lambda i: (0, 0))],
        out_shape=[jax.ShapeDtypeStruct((s, qd), F32), jax.ShapeDtypeStruct((s, cd), F32),
                   jax.ShapeDtypeStruct((SUBLANES, qd), F32), jax.ShapeDtypeStruct((SUBLANES, cd), F32)],
        compiler_params=_params("arbitrary"),
    )(dcat, a, c, ga, gc)


def _adamw(name, w, g, m, v, emit_grad=False):
    shape = w.shape
    cols = shape[-1]
    rows = w.size // cols
    tr = _row_tile(rows, cols, budget=3 << 19)
    bc1 = 1.0 - ADAM_B1 ** ADAM_STEP
    bc2 = 1.0 - ADAM_B2 ** ADAM_STEP
    n_out = 4 if emit_grad else 3

    def body(w_ref, g_ref, m_ref, v_ref, d_ref, nm_ref, nv_ref, *g_out):
        gv = g_ref[...]
        mv = ADAM_B1 * m_ref[...] + (1.0 - ADAM_B1) * gv
        vv = ADAM_B2 * v_ref[...] + (1.0 - ADAM_B2) * (gv * gv)
        nm_ref[...] = mv
        nv_ref[...] = vv
        d_ref[...] = -ADAM_LR * ((mv / bc1) / (jnp.sqrt(vv / bc2) + ADAM_EPS) + ADAM_WD * w_ref[...])
        for ref in g_out:
            ref[...] = gv

    row = pl.BlockSpec((tr, cols), lambda i: (i, 0))
    outs = pl.pallas_call(
        body, name=name, grid=(rows // tr,), in_specs=[row] * 4, out_specs=[row] * n_out,
        out_shape=[jax.ShapeDtypeStruct((rows, cols), F32)] * n_out, compiler_params=_params("parallel"),
    )(*(t.reshape(rows, cols) for t in (w, g, m, v)))
    return tuple(t.reshape(shape) for t in outs)


HBM_SPEC = pl.BlockSpec(memory_space=pltpu.HBM)


def _mesh_place():
    x, y, c = lax.axis_index("x"), lax.axis_index("y"), lax.axis_index("c")
    other_chips = [(1 - x, y), (x, 1 - y), (1 - x, 1 - y)]
    return x, y, c, other_chips


def _cast_into_slot(name, w, layer, chip, deps=()):
    _, r, cols = w.shape
    tr = _row_tile(r, cols)

    def body(chip_ref, w_ref, *rest):
        o_ref = rest[-1]
        o_ref[...] = w_ref[...].astype(o_ref.dtype)

    return pl.pallas_call(
        body, name=name,
        grid_spec=pltpu.PrefetchScalarGridSpec(
            num_scalar_prefetch=1, grid=(r // tr,),
            in_specs=[pl.BlockSpec((None, tr, cols), lambda i, chip_ref: (layer, i, 0))] + [ANY_SPEC] * len(deps),
            out_specs=pl.BlockSpec((None, tr, cols), lambda i, chip_ref: (chip_ref[0], i, 0))),
        out_shape=jax.ShapeDtypeStruct((N_CHIPS, r, cols), BF16), compiler_params=_params("parallel"),
    )(chip, w, *deps)


SEM_SPEC = pl.BlockSpec(memory_space=pltpu.SEMAPHORE)
SPLIT_COPY = pltpu.CompilerParams(has_side_effects=pltpu.SideEffectType.DATAFLOW_SIDE_EFFECTING)
N_OTHER = N_CHIPS - 1
TOKEN_SPEC = pl.BlockSpec(memory_space=pltpu.VMEM)
TOKEN_SHAPE = jax.ShapeDtypeStruct((SUBLANES, LANES), F32)


def _in_hbm(arr):
    return pltpu.with_memory_space_constraint(arr, pltpu.HBM)


def _half_rows(ref, chip_idx, core):
    r2 = ref.shape[1] // 2
    return ref.at[chip_idx, pl.ds(core * r2, r2), :]


def _gather_start(name, fulls, after):
    na = len(fulls)

    def body(*refs):
        f_refs = refs[na + 1:2 * na + 1]
        send_sems, recv_sems = refs[2 * na + 1:3 * na + 1], refs[3 * na + 1:4 * na + 1]
        token = refs[4 * na + 1]
        x, y, c, chips = _mesh_place()
        for a in range(na):
            mine = _half_rows(f_refs[a], 2 * x + y, c)
            for j, (cx, cy) in enumerate(chips):
                pltpu.make_async_remote_copy(
                    src_ref=mine, dst_ref=mine, send_sem=send_sems[a].at[j], recv_sem=recv_sems[a].at[j],
                    device_id=(cx, cy, c), device_id_type=MESH).start()
        token[...] = jnp.zeros_like(token)

    outs = pl.pallas_call(
        body, name=name, in_specs=[HBM_SPEC] * na + [ANY_SPEC],
        out_specs=[HBM_SPEC] * na + [SEM_SPEC] * (2 * na) + [TOKEN_SPEC],
        out_shape=[pltpu.HBM(f.shape, f.dtype) for f in fulls] + [pltpu.SemaphoreType.DMA((N_OTHER,))] * (2 * na)
        + [TOKEN_SHAPE],
        input_output_aliases={a: a for a in range(na)}, compiler_params=SPLIT_COPY,
    )(*[_in_hbm(f) for f in fulls], after)
    return list(outs[:na]), list(outs[na:2 * na]), list(outs[2 * na:3 * na]), outs[3 * na]


def _gather_pass_on(name, full, recv_sems, after):
    def body(f_in, recv_sems, after_ref, f_ref, d2d_send, d2d_recv):
        x, y, c, chips = _mesh_place()
        for j, (cx, cy) in enumerate(chips):
            blk = _half_rows(f_ref, 2 * cx + cy, c)
            pltpu.make_async_remote_copy(
                src_ref=blk, dst_ref=blk, send_sem=d2d_send.at[j], recv_sem=recv_sems.at[j],
                device_id=(cx, cy, c), device_id_type=MESH).wait_recv()
            pltpu.make_async_remote_copy(
                src_ref=blk, dst_ref=blk, send_sem=d2d_send.at[j], recv_sem=d2d_recv.at[j],
                device_id=(x, y, 1 - c), device_id_type=MESH).start()

    return pl.pallas_call(
        body, name=name, in_specs=[HBM_SPEC, SEM_SPEC, ANY_SPEC], out_specs=[HBM_SPEC, SEM_SPEC, SEM_SPEC],
        out_shape=[pltpu.HBM(full.shape, full.dtype)] + [pltpu.SemaphoreType.DMA((N_OTHER,))] * 2,
        input_output_aliases={0: 0}, compiler_params=SPLIT_COPY,
    )(full, recv_sems, after)


def _gather_arrive(name, full, ici_send, d2d_send, d2d_recv, after):
    def body(f_in, ici_send, d2d_send, d2d_recv, after_ref, f_ref):
        x, y, c, chips = _mesh_place()
        for j, (cx, cy) in enumerate(chips):
            mine = _half_rows(f_ref, 2 * x + y, c)
            passed = _half_rows(f_ref, 2 * cx + cy, c)
            theirs = _half_rows(f_ref, 2 * cx + cy, 1 - c)
            pltpu.make_async_remote_copy(
                src_ref=mine, dst_ref=mine, send_sem=ici_send.at[j], recv_sem=d2d_recv.at[j],
                device_id=(cx, cy, c), device_id_type=MESH).wait_send()
            pltpu.make_async_remote_copy(
                src_ref=passed, dst_ref=passed, send_sem=d2d_send.at[j], recv_sem=d2d_recv.at[j],
                device_id=(x, y, 1 - c), device_id_type=MESH).wait_send()
            pltpu.make_async_remote_copy(
                src_ref=theirs, dst_ref=theirs, send_sem=d2d_send.at[j], recv_sem=d2d_recv.at[j],
                device_id=(x, y, 1 - c), device_id_type=MESH).wait_recv()

    return pl.pallas_call(
        body, name=name, in_specs=[HBM_SPEC, SEM_SPEC, SEM_SPEC, SEM_SPEC, ANY_SPEC], out_specs=HBM_SPEC,
        out_shape=pltpu.HBM(full.shape, full.dtype), input_output_aliases={0: 0}, compiler_params=SPLIT_COPY,
    )(full, ici_send, d2d_send, d2d_recv, after)


def _gather_taps(conv_w):
    def body(cw_ref, cwf_ref, send_sems, recv_sems, local_sem):
        x, y, c, chips = _mesh_place()
        k_me = 2 * x + y
        local = pltpu.make_async_copy(cw_ref, cwf_ref.at[k_me], local_sem)
        local.start()
        copies = [pltpu.make_async_remote_copy(
            src_ref=cw_ref, dst_ref=cwf_ref.at[k_me], send_sem=send_sems.at[j], recv_sem=recv_sems.at[j],
            device_id=(cx, cy, c), device_id_type=MESH) for j, (cx, cy) in enumerate(chips)]
        for cp in copies:
            cp.start()
        for j, (cx, cy) in enumerate(chips):
            pltpu.make_async_remote_copy(
                src_ref=cw_ref, dst_ref=cwf_ref.at[2 * cx + cy], send_sem=send_sems.at[j], recv_sem=recv_sems.at[j],
                device_id=(cx, cy, c), device_id_type=MESH).wait_recv()
        for cp in copies:
            cp.wait_send()
        local.wait()

    return pl.pallas_call(
        body, name="gather_taps", in_specs=[HBM_SPEC], out_specs=HBM_SPEC,
        out_shape=jax.ShapeDtypeStruct((N_CHIPS,) + conv_w.shape, conv_w.dtype),
        scratch_shapes=[pltpu.SemaphoreType.DMA((N_OTHER,))] * 2 + [pltpu.SemaphoreType.DMA],
    )(conv_w)


def _sibling_half(g_ref, c):
    r2 = g_ref.shape[1] // 2
    return g_ref.at[:, pl.ds((1 - c) * r2, r2), :]


def _swap_start(name, g):
    def body(g_in, g_ref, land_ref, send_sem, recv_sem, token):
        x, y, c, _ = _mesh_place()
        pltpu.make_async_remote_copy(
            src_ref=_sibling_half(g_ref, c), dst_ref=land_ref, send_sem=send_sem, recv_sem=recv_sem,
            device_id=(x, y, 1 - c), device_id_type=MESH).start()
        token[...] = jnp.zeros_like(token)

    nb, r, cols = g.shape
    return pl.pallas_call(
        body, name=name, in_specs=[HBM_SPEC], out_specs=[HBM_SPEC, HBM_SPEC, SEM_SPEC, SEM_SPEC, TOKEN_SPEC],
        out_shape=[pltpu.HBM(g.shape, g.dtype), pltpu.HBM((nb, r // 2, cols), g.dtype),
                   pltpu.SemaphoreType.DMA(()), pltpu.SemaphoreType.DMA(()), TOKEN_SHAPE],
        input_output_aliases={0: 0}, compiler_params=SPLIT_COPY,
    )(_in_hbm(g))


def _swap_wait(name, g, land, send_sem, recv_sem, after):
    def body(g_in, land_in, send_sem, recv_sem, after_ref, g_ref, land_ref):
        x, y, c, _ = _mesh_place()
        copy = pltpu.make_async_remote_copy(
            src_ref=_sibling_half(g_ref, c), dst_ref=land_ref, send_sem=send_sem, recv_sem=recv_sem,
            device_id=(x, y, 1 - c), device_id_type=MESH)
        copy.wait_send()
        copy.wait_recv()

    return pl.pallas_call(
        body, name=name, in_specs=[HBM_SPEC, HBM_SPEC, SEM_SPEC, SEM_SPEC, ANY_SPEC], out_specs=[HBM_SPEC, HBM_SPEC],
        out_shape=[pltpu.HBM(g.shape, g.dtype), pltpu.HBM(land.shape, land.dtype)],
        input_output_aliases={0: 0, 1: 1}, compiler_params=SPLIT_COPY,
    )(g, land, send_sem, recv_sem, after)


def _add_core_halves(name, g, sib, core):
    nb, r, cols = g.shape
    r2 = r // 2
    tr = _row_tile(r2, cols, itemsize=2, budget=1 << 20)
    nrt = r2 // tr

    def body(core_ref, g_ref, s_ref, o_ref):
        o_ref[...] = (g_ref[...].astype(F32) + s_ref[...].astype(F32)).astype(o_ref.dtype)

    return pl.pallas_call(
        body, name=name,
        grid_spec=pltpu.PrefetchScalarGridSpec(
            num_scalar_prefetch=1, grid=(nb, nrt),
            in_specs=[pl.BlockSpec((None, tr, cols), lambda k, i, core_ref: (k, core_ref[0] * nrt + i, 0)),
                      pl.BlockSpec((None, tr, cols), lambda k, i, core_ref: (k, i, 0))],
            out_specs=pl.BlockSpec((None, tr, cols), lambda k, i, core_ref: (k, i, 0))),
        out_shape=jax.ShapeDtypeStruct((nb, r2, cols), BF16), compiler_params=_params("parallel", "parallel"),
    )(core, g, sib)


def _scatter_copy(h_ref, land_ref, send_sems, recv_sems, j, chip_xy, c):
    cx, cy = chip_xy
    return pltpu.make_async_remote_copy(
        src_ref=h_ref.at[2 * cx + cy], dst_ref=land_ref.at[j], send_sem=send_sems.at[j], recv_sem=recv_sems.at[j],
        device_id=(cx, cy, c), device_id_type=MESH)


def _scatter_start(name, h):
    def body(h_in, h_ref, land_ref, send_sems, recv_sems, token):
        x, y, c, chips = _mesh_place()
        for j, chip_xy in enumerate(chips):
            _scatter_copy(h_ref, land_ref, send_sems, recv_sems, j, chip_xy, c).start()
        token[...] = jnp.zeros_like(token)

    return pl.pallas_call(
        body, name=name, in_specs=[HBM_SPEC], out_specs=[HBM_SPEC, HBM_SPEC, SEM_SPEC, SEM_SPEC, TOKEN_SPEC],
        out_shape=[pltpu.HBM(h.shape, h.dtype), pltpu.HBM((N_OTHER,) + h.shape[1:], h.dtype),
                   pltpu.SemaphoreType.DMA((N_OTHER,)), pltpu.SemaphoreType.DMA((N_OTHER,)), TOKEN_SHAPE],
        input_output_aliases={0: 0}, compiler_params=SPLIT_COPY,
    )(_in_hbm(h))


def _scatter_wait(name, h, land, send_sems, recv_sems, after):
    afters = tuple(after) if isinstance(after, (tuple, list)) else (after,)

    def body(h_in, land_in, send_sems, recv_sems, *rest):
        h_ref, land_ref = rest[-2:]
        x, y, c, chips = _mesh_place()
        for j, chip_xy in enumerate(chips):
            copy = _scatter_copy(h_ref, land_ref, send_sems, recv_sems, j, chip_xy, c)
            copy.wait_send()
            copy.wait_recv()

    return pl.pallas_call(
        body, name=name, in_specs=[HBM_SPEC, HBM_SPEC, SEM_SPEC, SEM_SPEC] + [ANY_SPEC] * len(afters),
        out_specs=[HBM_SPEC, HBM_SPEC],
        out_shape=[pltpu.HBM(h.shape, h.dtype), pltpu.HBM(land.shape, land.dtype)],
        input_output_aliases={0: 0, 1: 1}, compiler_params=SPLIT_COPY,
    )(h, land, send_sems, recv_sems, *afters)


def _sum_chips(name, hs, rcv, core, chip, layer, n_layers, prev):
    _, r2, cols = hs.shape
    tr = _row_tile(r2, cols, budget=1 << 20)
    nrt = r2 // tr

    def body(core_ref, chip_ref, h_ref, r_ref, *rest):
        o_ref = rest[-1]
        acc = h_ref[...].astype(F32)
        for j in range(N_CHIPS - 1):
            acc = acc + r_ref[j].astype(F32)
        o_ref[...] = acc

    in_specs = [pl.BlockSpec((None, tr, cols), lambda i, core_ref, chip_ref: (chip_ref[0], i, 0)),
                pl.BlockSpec((N_CHIPS - 1, tr, cols), lambda i, core_ref, chip_ref: (0, i, 0))]
    args = [core, chip, hs, rcv]
    aliases = {}
    if prev is not None:
        in_specs.append(pl.BlockSpec(memory_space=pl.ANY))
        args.append(prev)
        aliases = {4: 0}
    return pl.pallas_call(
        body, name=name,
        grid_spec=pltpu.PrefetchScalarGridSpec(
            num_scalar_prefetch=2, grid=(nrt,), in_specs=in_specs,
            out_specs=pl.BlockSpec((None, tr, cols), lambda i, core_ref, chip_ref: (layer, core_ref[0] * nrt + i, 0))),
        out_shape=jax.ShapeDtypeStruct((n_layers, 2 * r2, cols), F32), input_output_aliases=aliases,
        compiler_params=_params("parallel"),
    )(*args)


def _join_core_halves(name, ts, deps=()):
    na, nd = len(ts), len(deps)

    def body(*refs):
        o_refs = refs[na + nd:2 * na + nd]
        send_sems, recv_sems = refs[2 * na + nd:]
        x, y, c, _ = _mesh_place()
        copies = []
        for a in range(na):
            r2 = o_refs[a].shape[1] // 2
            mine = o_refs[a].at[:, pl.ds(c * r2, r2), :]
            copies.append(pltpu.make_async_remote_copy(
                src_ref=mine, dst_ref=mine, send_sem=send_sems.at[a], recv_sem=recv_sems.at[a],
                device_id=(x, y, 1 - c), device_id_type=MESH))
        for cp in copies:
            cp.start()
        for cp in copies:
            cp.wait()

    return pl.pallas_call(
        body, name=name, in_specs=[HBM_SPEC] * na + [ANY_SPEC] * nd, out_specs=[HBM_SPEC] * na,
        out_shape=[jax.ShapeDtypeStruct(t.shape, t.dtype) for t in ts],
        input_output_aliases={a: a for a in range(na)},
        scratch_shapes=[pltpu.SemaphoreType.DMA((na,))] * 2,
    )(*ts, *deps)


def _allreduce_small(p):
    n, _, w = p.shape

    def body(p_ref, o_ref, buf, send_sems, recv_sems):
        x, y, c, _ = _mesh_place()
        me = 4 * x + 2 * y + c
        buf[me] = jnp.sum(p_ref[...], axis=1)
        copies = []
        for pat in range(1, N_DEV):
            fx, fy, fc = (pat >> 2) & 1, (pat >> 1) & 1, pat & 1
            copies.append(pltpu.make_async_remote_copy(
                src_ref=buf.at[me], dst_ref=buf.at[me], send_sem=send_sems.at[pat - 1], recv_sem=recv_sems.at[pat - 1],
                device_id=(x ^ fx, y ^ fy, c ^ fc), device_id_type=MESH))
        for cp in copies:
            cp.start()
        for cp in copies:
            cp.wait()
        acc = buf[0]
        for dev in range(1, N_DEV):
            acc = acc + buf[dev]
        o_ref[...] = acc

    return pl.pallas_call(
        body, name="allreduce_small", in_specs=[pl.BlockSpec(memory_space=pltpu.VMEM)],
        out_specs=pl.BlockSpec(memory_space=pltpu.VMEM), out_shape=jax.ShapeDtypeStruct((n, w), F32),
        scratch_shapes=[pltpu.VMEM((N_DEV, n, w), F32), pltpu.SemaphoreType.DMA((N_DEV - 1,)),
                        pltpu.SemaphoreType.DMA((N_DEV - 1,))],
    )(p)


class _WeightFeed:
    def __init__(self):
        self.fulls, self.ici_send, self.ici_recv, self.d2d = [], [], [], []

    def start(self, name, fulls, after):
        started, send, recv, token = _gather_start(name, fulls, after)
        self.fulls += started
        self.ici_send += send
        self.ici_recv += recv
        self.d2d += [None] * len(fulls)
        self.token = token
        return token

    def _pass_on(self, k, after):
        if k == 0:
            after = self.token
        if k < len(self.fulls) and self.d2d[k] is None:
            self.fulls[k], send, recv = _gather_pass_on(f"gather_pass_{k}", self.fulls[k], self.ici_recv[k], after)
            self.d2d[k] = (send, recv)

    def take(self, k, after):
        self._pass_on(k, after)
        self._pass_on(k + 1, after)
        if k + 1 < len(self.fulls):
            after = self.fulls[k + 1]
        self.fulls[k] = _gather_arrive(f"gather_arrive_{k}", self.fulls[k], self.ici_send[k], *self.d2d[k], after)
        return self.fulls[k]


def _ffn_forward(tag, x, g_pre, g_post, feed, k):
    s, d = x.shape
    h = _norm_fwd(f"{tag}_norm", x, g_pre)
    gu_w = feed.take(k, h)
    gu, a = _ffn_up(f"{tag}_up", h, gu_w)
    dn_w = feed.take(k + 1, a).reshape(-1, d)
    f = dn_w.shape[0]
    tk = _tile(f, 1408)
    tm, tn = _tile(s, 1024), _tile(d, 1024)
    y = _mm(f"{tag}_down", a, dn_w, mode="nn", grid=(s // tm, d // tn, f // tk),
            a_spec=pl.BlockSpec((tm, tk), lambda i, j, k: (i, k)),
            b_spec=pl.BlockSpec((tk, tn), lambda i, j, k: (k, j)),
            o_spec=pl.BlockSpec((tm, tn), lambda i, j, k: (i, j)),
            out_shape=jax.ShapeDtypeStruct((s, d), F32), nk=f // tk, acc_shape=(tm, tn))
    x_new = _res_norm(f"{tag}_post", x, y, g_post, FFN_RESIDUAL_WEIGHT)
    return x_new, (x, h, gu, a, y)


class _GradReduce:
    def __init__(self, core, chip, n_layers):
        self.core, self.chip, self.n_layers = core, chip, n_layers
        self.state = {}
        self.bufs = {}

    def start(self, kind, layer, g):
        g, land, send, recv, token = _swap_start(f"swap_start_{kind}_{layer}", g)
        self.state[kind, layer] = (g, land, send, recv)
        return token

    def exchange(self, kind, layer, after):
        tag = f"{kind}_{layer}"
        g, sib = _swap_wait(f"swap_wait_{tag}", *self.state[kind, layer], after)
        h = _add_core_halves(f"add_cores_{tag}", g, sib, self.core)
        h, land, send, recv, token = _scatter_start(f"scatter_start_{tag}", h)
        self.state[kind, layer] = (h, land, send, recv)
        return token

    def finish(self, kind, layer, after):
        tag = f"{kind}_{layer}"
        h, rcv = _scatter_wait(f"scatter_wait_{tag}", *self.state.pop((kind, layer)), after)
        self.bufs[kind] = _sum_chips(f"sum_chips_{tag}", h, rcv, self.core, self.chip, layer, self.n_layers,
                                     self.bufs.get(kind))
        return self.bufs[kind]


def _ffn_backward(tag, dx_new, saved, g_pre, g_post, gu_w, dn_w, red, kinds, layer, deps):
    x, h, gu, a, y = saved
    s, d = x.shape
    nb, fs = gu_w.shape[0], gu_w.shape[2]
    f = dn_w.shape[0]
    fr = f // nb
    dy, dg_post = _norm_bwd(f"{tag}_post_bwd", dx_new, y, g_post, FFN_RESIDUAL_WEIGHT, None, BF16)
    dgu = _ffn_dact(f"{tag}_dact", dy, dn_w, gu, deps)
    dgu4 = dgu.reshape(nb, s, fs)
    tn = _tile(d, 1024)
    d_wd = _mm(f"{tag}_dwd", a, dy, mode="tn", grid=(nb, d // tn),
               a_spec=pl.BlockSpec((s, fr), lambda i, j: (0, i)),
               b_spec=pl.BlockSpec((s, tn), lambda i, j: (0, j)),
               o_spec=pl.BlockSpec((None, fr, tn), lambda i, j: (i, 0, j)),
               out_shape=jax.ShapeDtypeStruct((nb, fr, d), BF16))
    tm, tw = _tile(d, 512), _tile(fs, 1408)
    nw = fs // tw
    d_wgu = _mm(f"{tag}_dwgu", h, dgu4, mode="tn", grid=(nb, nw, d // tm),
                a_spec=pl.BlockSpec((s, tm), lambda k, j, i: (0, i)),
                b_spec=pl.BlockSpec((None, s, tw), lambda k, j, i: (k, 0, j)),
                o_spec=pl.BlockSpec((None, tm, tw), lambda k, j, i: (k, i, j)),
                out_shape=jax.ShapeDtypeStruct((nb, d, fs), BF16))
    started = (red.start(kinds[0], layer, d_wgu), red.start(kinds[1], layer, d_wd))
    ts, td = _tile(s, 1024), _tile(d, 1024)
    dh = _mm(f"{tag}_dh", dgu4, gu_w, mode="nt", grid=(s // ts, d // td, nb),
             a_spec=pl.BlockSpec((None, ts, fs), lambda i, j, k: (k, i, 0)),
             b_spec=pl.BlockSpec((None, td, fs), lambda i, j, k: (k, j, 0)),
             o_spec=pl.BlockSpec((ts, td), lambda i, j, k: (i, j)),
             out_shape=jax.ShapeDtypeStruct((s, d), F32), nk=nb, acc_shape=(ts, td), deps=started)
    dx, dg_pre = _norm_bwd(f"{tag}_pre_bwd", dh, x, g_pre, 1.0, dx_new, F32)
    return dx, dg_pre, dg_post


def _mixer_forward(tag, x, gains, feed, k, conv_taps, dims):
    qd, kvd, cd = dims
    s, d = x.shape
    g_pre, g_a, g_c, g_post = gains
    h = _norm_fwd(f"{tag}_norm", x, g_pre)
    win_w = feed.take(k, h)
    nb, cw = win_w.shape[0], win_w.shape[2]
    tm = _tile(s, 1024)
    z = _mm(f"{tag}_in", h, win_w, mode="nn", grid=(nb, s // tm),
            a_spec=pl.BlockSpec((tm, d), lambda j, i: (i, 0)),
            b_spec=pl.BlockSpec((None, d, cw), lambda j, i: (j, 0, 0)),
            o_spec=pl.BlockSpec((tm, cw), lambda j, i: (i, j)),
            out_shape=jax.ShapeDtypeStruct((s, nb * cw), BF16))
    a, lse = _attn_fwd(f"{tag}_attn", z, qd, kvd)
    c = _conv_fwd(f"{tag}_conv", z, conv_taps, qd + 2 * kvd, cd)
    cat = _cat_norm_fwd(f"{tag}_cat", a, c, g_a, g_c)
    wout_w = feed.take(k + 1, cat).reshape(-1, d)
    mw = qd + cd
    tn = _tile(d, 1024)
    mixed = _mm(f"{tag}_out", cat, wout_w, mode="nn", grid=(s // tm, d // tn),
                a_spec=pl.BlockSpec((tm, mw), lambda i, j: (i, 0)),
                b_spec=pl.BlockSpec((mw, tn), lambda i, j: (0, j)),
                o_spec=pl.BlockSpec((tm, tn), lambda i, j: (i, j)),
                out_shape=jax.ShapeDtypeStruct((s, d), F32))
    x_new = _res_norm(f"{tag}_post", x, mixed, g_post, 1.0)
    return x_new, (x, h, z, a, lse, c, cat, mixed)


def _mixer_backward(tag, dx_new, saved, gains, win_w, conv_taps, wout_w, dims, red, kinds, layer, deps):
    qd, kvd, cd = dims
    x, h, z, a, lse, c, cat, mixed = saved
    s, d = x.shape
    nb, cw = win_w.shape[0], win_w.shape[2]
    g_pre, g_a, g_c, g_post = gains
    mw = qd + cd
    dmixed, dg_post = _norm_bwd(f"{tag}_post_bwd", dx_new, mixed, g_post, 1.0, None, BF16)
    tm, tn = _tile(s, 1024), _tile(mw, 1024)
    dcat = _mm(f"{tag}_dcat", dmixed, wout_w, mode="nt", grid=(s // tm, mw // tn),
               a_spec=pl.BlockSpec((tm, d), lambda i, j: (i, 0)),
               b_spec=pl.BlockSpec((tn, d), lambda i, j: (j, 0)),
               o_spec=pl.BlockSpec((tm, tn), lambda i, j: (i, j)),
               out_shape=jax.ShapeDtypeStruct((s, mw), F32), deps=deps)
    wr = mw // nb
    td = _tile(d, 1024)
    d_wout = _mm(f"{tag}_dwout", cat, dmixed, mode="tn", grid=(nb, d // td),
                 a_spec=pl.BlockSpec((s, wr), lambda i, j: (0, i)),
                 b_spec=pl.BlockSpec((s, td), lambda i, j: (0, j)),
                 o_spec=pl.BlockSpec((None, wr, td), lambda i, j: (i, 0, j)),
                 out_shape=jax.ShapeDtypeStruct((nb, wr, d), BF16))
    da, dc, dg_a, dg_c = _cat_norm_bwd(f"{tag}_cat_bwd", dcat, a, c, g_a, g_c)
    dhc, dbg, dcg, d_taps = _conv_bwd(f"{tag}_conv_bwd", z, conv_taps, dc, qd + 2 * kvd, cd)
    dq, dk, dv = _attn_bwd(f"{tag}_attn_bwd", z, a, lse, da, qd, kvd)
    dz = jnp.concatenate([dq, dk, dv, dhc, dbg, dcg], axis=1)
    th = _tile(d, 512)
    d_win = _mm(f"{tag}_dwin", h, dz, mode="tn", grid=(nb, d // th),
                a_spec=pl.BlockSpec((s, th), lambda k, i: (0, i)),
                b_spec=pl.BlockSpec((s, cw), lambda k, i: (0, k)),
                o_spec=pl.BlockSpec((None, th, cw), lambda k, i: (k, i, 0)),
                out_shape=jax.ShapeDtypeStruct((nb, d, cw), BF16))
    started = (red.start(kinds[0], layer, d_win), red.start(kinds[1], layer, d_wout))
    dh = _mm(f"{tag}_dh", dz, win_w, mode="nt", grid=(s // tm, d // td, nb),
             a_spec=pl.BlockSpec((tm, cw), lambda i, j, k: (i, k)),
             b_spec=pl.BlockSpec((None, td, cw), lambda i, j, k: (k, j, 0)),
             o_spec=pl.BlockSpec((tm, td), lambda i, j, k: (i, j)),
             out_shape=jax.ShapeDtypeStruct((s, d), F32), nk=nb, acc_shape=(tm, td), deps=started)
    dx, dg_pre = _norm_bwd(f"{tag}_pre_bwd", dh, x, g_pre, 1.0, dx_new, F32)
    return dx, d_taps, (dg_pre, dg_a, dg_c, dg_post)


def _pad_cols(v, width):
    return jnp.pad(v, ((0, 0), (0, width - v.shape[1])))


def kernel(x, ffn1_norm_pre, ffn1_w_gate_up, ffn1_w_down, ffn1_norm_post, mix_norm_pre, w_in, conv_w, attn_out_norm, conv_out_norm, w_out, mix_norm_post, ffn2_norm_pre, ffn2_w_gate_up, ffn2_w_down, ffn2_norm_post, loss_target, m_ffn1_norm_pre, m_ffn1_w_gate_up, m_ffn1_w_down, m_ffn1_norm_post, m_mix_norm_pre, m_w_in, m_conv_w, m_attn_out_norm, m_conv_out_norm, m_w_out, m_mix_norm_post, m_ffn2_norm_pre, m_ffn2_w_gate_up, m_ffn2_w_down, m_ffn2_norm_post, v_ffn1_norm_pre, v_ffn1_w_gate_up, v_ffn1_w_down, v_ffn1_norm_post, v_mix_norm_pre, v_w_in, v_conv_w, v_attn_out_norm, v_conv_out_norm, v_w_out, v_mix_norm_post, v_ffn2_norm_pre, v_ffn2_w_gate_up, v_ffn2_w_down, v_ffn2_norm_post):
    _, s, d = x.shape
    n_layers = ffn1_norm_pre.shape[0]
    qd = attn_out_norm.shape[1]
    cd = conv_out_norm.shape[1]
    kvd = qd // Q_PER_KV
    dims = (qd, kvd, cd)
    assert N_CHIPS * w_in.shape[2] == qd + 2 * kvd + 3 * cd and qd + cd == N_CHIPS * w_out.shape[1]
    assert 2 * d <= SMALL_ROWS * LANES * SUBLANES
    chip = 2 * lax.axis_index("x") + lax.axis_index("y")
    chip_arr = chip.astype(jnp.int32).reshape(1)
    core = lax.axis_index("c").astype(jnp.int32).reshape(1)
    kinds = ("gu1", "dn1", "win", "wout", "gu2", "dn2")

    big = (ffn1_w_gate_up, ffn1_w_down, w_in, w_out, ffn2_w_gate_up, ffn2_w_down)
    nk = len(kinds)
    taps_all = _gather_taps(conv_w)
    feed = _WeightFeed()
    order = [(k, w, layer) for layer in range(n_layers) for k, w in zip(kinds, big)]
    k, w, layer = order[0]
    token = feed.start("gather_start_first", [_cast_into_slot(f"cast_{k}_{layer}", w, layer, chip_arr)], taps_all)
    feed.start("gather_start_rest", [_cast_into_slot(f"cast_{k}_{layer}", w, layer, chip_arr, (token,))
                                     for k, w, layer in order[1:]], token)
    taps = jnp.transpose(taps_all, (1, 2, 0, 3)).reshape(n_layers, CONV_WIDTH, cd)
    taps = jnp.pad(taps, ((0, 0), (0, SUBLANES - CONV_WIDTH), (0, 0)))

    def gain(g, layer):
        return g[layer][None, :]

    xs = x[0]
    saved = []
    for layer in range(n_layers):
        t = f"l{layer}"
        k0 = layer * nk
        xs, s1 = _ffn_forward(f"{t}_ffn1", xs, gain(ffn1_norm_pre, layer), gain(ffn1_norm_post, layer), feed, k0)
        mix_gains = (gain(mix_norm_pre, layer), gain(attn_out_norm, layer), gain(conv_out_norm, layer), gain(mix_norm_post, layer))
        xs, s2 = _mixer_forward(f"{t}_mix", xs, mix_gains, feed, k0 + 2, taps[layer], dims)
        xs, s3 = _ffn_forward(f"{t}_ffn2", xs, gain(ffn2_norm_pre, layer), gain(ffn2_norm_post, layer), feed, k0 + 4)
        saved.append((s1, s2, s3, mix_gains))
    wts = {k: [feed.fulls[layer * nk + i] for layer in range(n_layers)] for i, k in enumerate(kinds)}
    for k in ("dn1", "wout", "dn2"):
        wts[k] = [w.reshape(-1, d) for w in wts[k]]
    dxs, loss_part = _loss_head("loss_head", xs, loss_target[0])
    loss = lax.psum(jnp.sum(loss_part), ("x", "y", "c"))

    red = _GradReduce(core, chip_arr, n_layers)
    small = [None] * n_layers
    flow = {"deps": (), "in_flight": []}

    def between(dx, new_keys):
        after = dx
        for key in flow["in_flight"]:
            after = red.finish(*key, after)
        flow["deps"] = tuple(red.exchange(*key, after) for key in new_keys)
        flow["in_flight"] = list(new_keys)

    for layer in reversed(range(n_layers)):
        t = f"l{layer}"
        s1, s2, s3, mix_gains = saved[layer]
        dxs, p_pre2, p_post2 = _ffn_backward(
            f"{t}_ffn2", dxs, s3, gain(ffn2_norm_pre, layer), gain(ffn2_norm_post, layer),
            wts["gu2"][layer], wts["dn2"][layer], red, ("gu2", "dn2"), layer, flow["deps"])
        between(dxs, [("gu2", layer), ("dn2", layer)])
        dxs, p_taps, (p_mpre, p_a, p_c, p_mpost) = _mixer_backward(
            f"{t}_mix", dxs, s2, mix_gains, wts["win"][layer], taps[layer], wts["wout"][layer], dims,
            red, ("win", "wout"), layer, flow["deps"])
        between(dxs, [("win", layer), ("wout", layer)])
        dxs, p_pre1, p_post1 = _ffn_backward(
            f"{t}_ffn1", dxs, s1, gain(ffn1_norm_pre, layer), gain(ffn1_norm_post, layer),
            wts["gu1"][layer], wts["dn1"][layer], red, ("gu1", "dn1"), layer, flow["deps"])
        between(dxs, [("gu1", layer), ("dn1", layer)])
        tap_rows = jnp.zeros((CONV_WIDTH, SUBLANES, d), F32).at[:, 0, :cd].set(p_taps[:CONV_WIDTH])
        rows = [p_pre1, p_post1, p_mpre, jnp.concatenate([p_a, p_c], axis=1), p_mpost, p_pre2, p_post2]
        rows = jnp.concatenate([jnp.stack(rows), tap_rows], axis=0)
        small[layer] = jnp.pad(rows, ((0, SMALL_ROWS - rows.shape[0]), (0, 0), (0, 0)))
    grad_x = dxs[None]

    weights = dict(ffn1_norm_pre=ffn1_norm_pre, ffn1_w_gate_up=ffn1_w_gate_up, ffn1_w_down=ffn1_w_down, ffn1_norm_post=ffn1_norm_post, mix_norm_pre=mix_norm_pre, w_in=w_in, conv_w=conv_w, attn_out_norm=attn_out_norm, conv_out_norm=conv_out_norm, w_out=w_out, mix_norm_post=mix_norm_post, ffn2_norm_pre=ffn2_norm_pre, ffn2_w_gate_up=ffn2_w_gate_up, ffn2_w_down=ffn2_w_down, ffn2_norm_post=ffn2_norm_post)
    m_in = dict(ffn1_norm_pre=m_ffn1_norm_pre, ffn1_w_gate_up=m_ffn1_w_gate_up, ffn1_w_down=m_ffn1_w_down, ffn1_norm_post=m_ffn1_norm_post, mix_norm_pre=m_mix_norm_pre, w_in=m_w_in, conv_w=m_conv_w, attn_out_norm=m_attn_out_norm, conv_out_norm=m_conv_out_norm, w_out=m_w_out, mix_norm_post=m_mix_norm_post, ffn2_norm_pre=m_ffn2_norm_pre, ffn2_w_gate_up=m_ffn2_w_gate_up, ffn2_w_down=m_ffn2_w_down, ffn2_norm_post=m_ffn2_norm_post)
    v_in = dict(ffn1_norm_pre=v_ffn1_norm_pre, ffn1_w_gate_up=v_ffn1_w_gate_up, ffn1_w_down=v_ffn1_w_down, ffn1_norm_post=v_ffn1_norm_post, mix_norm_pre=v_mix_norm_pre, w_in=v_w_in, conv_w=v_conv_w, attn_out_norm=v_attn_out_norm, conv_out_norm=v_conv_out_norm, w_out=v_w_out, mix_norm_post=v_mix_norm_post, ffn2_norm_pre=v_ffn2_norm_pre, ffn2_w_gate_up=v_ffn2_w_gate_up, ffn2_w_down=v_ffn2_w_down, ffn2_norm_post=v_ffn2_norm_post)
    kind_name = dict(gu1="ffn1_w_gate_up", dn1="ffn1_w_down", win="w_in", wout="w_out", gu2="ffn2_w_gate_up", dn2="ffn2_w_down")
    delta, new_m, new_v, grad = {}, {}, {}, {}

    def update(kind_list, joined):
        for k, g in zip(kind_list, joined):
            n = kind_name[k]
            delta[n], new_m[n], new_v[n], grad[n] = _adamw(f"adamw_{n}", weights[n], g, m_in[n], v_in[n], True)

    early = ("gu2", "dn2", "win", "wout")
    update(early, _join_core_halves("join_early", [red.bufs[k] for k in early], flow["deps"]))
    late = [k for (k, _) in flow["in_flight"]]
    for k, layer in flow["in_flight"]:
        red.finish(k, layer, [delta[kind_name[e]] for e in early])
    update(late, _join_core_halves("join_late", [red.bufs[k] for k in late]))

    small_sum = _allreduce_small(jnp.concatenate(small, axis=0)).reshape(n_layers, SMALL_ROWS, d)
    g_ffn1_pre, g_ffn1_post, g_mix_pre = small_sum[:, 0], small_sum[:, 1], small_sum[:, 2]
    g_attn_out, g_conv_out = small_sum[:, 3, :qd], small_sum[:, 3, qd:qd + cd]
    g_mix_post, g_ffn2_pre, g_ffn2_post = small_sum[:, 4], small_sum[:, 5], small_sum[:, 6]
    cc = conv_w.shape[2]
    g_conv = lax.dynamic_slice_in_dim(small_sum[:, 7:7 + CONV_WIDTH, :cd], chip * cc, cc, axis=2)

    grad.update(ffn1_norm_pre=g_ffn1_pre, ffn1_norm_post=g_ffn1_post, mix_norm_pre=g_mix_pre, conv_w=g_conv, attn_out_norm=g_attn_out, conv_out_norm=g_conv_out, mix_norm_post=g_mix_post, ffn2_norm_pre=g_ffn2_pre, ffn2_norm_post=g_ffn2_post)
    names = list(weights)

    vectors = [n for n in names if n not in kind_name.values()]

    def pack(tree):
        flat = jnp.concatenate([tree[n].reshape(-1) for n in vectors])
        return jnp.pad(flat, (0, -flat.size % (SUBLANES * LANES))).reshape(-1, LANES)

    packed = _adamw("adamw_small", pack(weights), pack(grad), pack(m_in), pack(v_in))
    offset = 0
    for n in vectors:
        size = weights[n].size
        for tree, flat in zip((delta, new_m, new_v), packed):
            tree[n] = flat.reshape(-1)[offset:offset + size].reshape(weights[n].shape)
        offset += size

    return (loss, grad_x, *[grad[n] for n in names], *[delta[n] for n in names],
            *[new_m[n] for n in names], *[new_v[n] for n in names])
```

```python
import functools

import jax
import jax.numpy as jnp
from jax import lax
from jax.experimental import pallas as pl
from jax.experimental.pallas import tpu as pltpu

F32 = jnp.float32
BF16 = jnp.bfloat16
MESH = pl.DeviceIdType.MESH

NORM_EPS = 1e-6
HEAD_DIM = 128
Q_PER_KV = 4
CONV_WIDTH = 3
FFN_RESIDUAL_WEIGHT = 0.5
DILATED_BRANCHES = ((128, 1), (512, 4), (2048, 16))
ADAM_LR = 0.001
ADAM_B1 = 0.9
ADAM_B2 = 0.999
ADAM_EPS = 1e-08
ADAM_WD = 0.01
ADAM_STEP = 10

N_CHIPS = 4
N_DEV = 8
V7X_VMEM_BYTES = 64 << 20
VMEM_LIMIT = V7X_VMEM_BYTES - (12 << 20)
SUBLANES = 8
LANES = 128
SMALL_ROWS = 16
BIG_BLOCK = 4 << 20


def _params(*sem):
    return pltpu.CompilerParams(dimension_semantics=sem, vmem_limit_bytes=VMEM_LIMIT)


def _row_tile(rows, cols, itemsize=4, budget=2 << 20):
    t = rows
    while t * cols * itemsize > budget and t % 32 == 0:
        t //= 2
    return t


def _sum_to_sublanes(v):
    r, n = v.shape
    return v.reshape(r // SUBLANES, SUBLANES, n).sum(axis=0)


_DIMS = {
    "nn": (((1,), (0,)), ((), ())),
    "nt": (((1,), (1,)), ((), ())),
    "tn": (((0,), (0,)), ((), ())),
}


ANY_SPEC = pl.BlockSpec(memory_space=pl.ANY)


def _dot(a, b, mode):
    return lax.dot_general(a, b, _DIMS[mode], preferred_element_type=F32)


def _mm(name, a, b, *, mode, grid, a_spec, b_spec, o_spec, out_shape, nk=1, acc_shape=None, deps=()):
    nd = len(deps)

    def body(a_ref, b_ref, *rest):
        o_ref, scratch = rest[nd], rest[nd + 1:]
        r = _dot(a_ref[...], b_ref[...], mode)
        if nk == 1:
            o_ref[...] = r.astype(o_ref.dtype)
        else:
            acc = scratch[0]
            k = pl.program_id(len(grid) - 1)

            @pl.when(k == 0)
            def _():
                acc[...] = r

            @pl.when(k > 0)
            def _():
                acc[...] += r

            @pl.when(k == nk - 1)
            def _():
                o_ref[...] = acc[...].astype(o_ref.dtype)

    sem = ("parallel",) * (len(grid) - (1 if nk > 1 else 0)) + (("arbitrary",) if nk > 1 else ())
    return pl.pallas_call(
        body, name=name, grid=grid, in_specs=[a_spec, b_spec] + [ANY_SPEC] * nd, out_specs=o_spec,
        out_shape=out_shape, scratch_shapes=[pltpu.VMEM(acc_shape, F32)] if nk > 1 else [],
        compiler_params=_params(*sem),
    )(a, b, *deps)


def _tile(n, want):
    if n <= want:
        return n
    best = None
    for t in range(LANES, want + 1, LANES):
        if n % t == 0:
            best = t
    assert best is not None, (n, want)
    return best


def _norm_fwd(name, x, gain):
    s, d = x.shape
    tr = _row_tile(s, d, budget=BIG_BLOCK)

    def body(x_ref, g_ref, o_ref):
        xv = x_ref[...]
        r = lax.rsqrt(jnp.mean(xv * xv, axis=-1, keepdims=True) + NORM_EPS)
        o_ref[...] = (xv * r * g_ref[...]).astype(o_ref.dtype)

    return pl.pallas_call(
        body, name=name, grid=(s // tr,),
        in_specs=[pl.BlockSpec((tr, d), lambda i: (i, 0)), pl.BlockSpec((1, d), lambda i: (0, 0))],
        out_specs=pl.BlockSpec((tr, d), lambda i: (i, 0)),
        out_shape=jax.ShapeDtypeStruct((s, d), BF16), compiler_params=_params("parallel"),
    )(x, gain)


def _res_norm(name, x, y, gain, scale):
    s, d = x.shape
    tr = _row_tile(s, d, budget=BIG_BLOCK)

    def body(x_ref, y_ref, g_ref, o_ref):
        yv = y_ref[...]
        r = lax.rsqrt(jnp.mean(yv * yv, axis=-1, keepdims=True) + NORM_EPS)
        o_ref[...] = x_ref[...] + scale * (yv * r * g_ref[...])

    row = pl.BlockSpec((tr, d), lambda i: (i, 0))
    return pl.pallas_call(
        body, name=name, grid=(s // tr,),
        in_specs=[row, row, pl.BlockSpec((1, d), lambda i: (0, 0))], out_specs=row,
        out_shape=jax.ShapeDtypeStruct((s, d), F32), compiler_params=_params("parallel"),
    )(x, y, gain)


def _norm_bwd(name, dout, yin, gain, scale, resid, out_dtype):
    s, d = yin.shape
    tr = _row_tile(s, d)
    has_resid = resid is not None

    def body(*refs):
        if has_resid:
            do_ref, y_ref, g_ref, r_ref, di_ref, dg_ref = refs
        else:
            do_ref, y_ref, g_ref, di_ref, dg_ref = refs
        yv = y_ref[...]
        r = lax.rsqrt(jnp.mean(yv * yv, axis=-1, keepdims=True) + NORM_EPS)
        xhat = yv * r
        dn = scale * do_ref[...]
        part = _sum_to_sublanes(dn * xhat)

        @pl.when(pl.program_id(0) == 0)
        def _():
            dg_ref[...] = part

        @pl.when(pl.program_id(0) > 0)
        def _():
            dg_ref[...] += part

        dxn = dn * g_ref[...]
        din = r * (dxn - xhat * jnp.mean(dxn * xhat, axis=-1, keepdims=True))
        if has_resid:
            din = din + r_ref[...]
        di_ref[...] = din.astype(di_ref.dtype)

    row = pl.BlockSpec((tr, d), lambda i: (i, 0))
    vec = pl.BlockSpec((1, d), lambda i: (0, 0))
    ins = [row, row, vec] + ([row] if has_resid else [])
    args = (dout, yin, gain) + ((resid,) if has_resid else ())
    return pl.pallas_call(
        body, name=name, grid=(s // tr,), in_specs=ins,
        out_specs=[row, pl.BlockSpec((SUBLANES, d), lambda i: (0, 0))],
        out_shape=[jax.ShapeDtypeStruct((s, d), out_dtype), jax.ShapeDtypeStruct((SUBLANES, d), F32)],
        compiler_params=_params("arbitrary"),
    )(*args)


def _loss_head(name, y, target):
    s, d = y.shape
    tr = _row_tile(s, d)

    def body(y_ref, t_ref, dy_ref, l_ref):
        e = y_ref[...] - t_ref[...]
        dy_ref[...] = e * (1.0 / d)
        part = _sum_to_sublanes(e * e) * (0.5 / d)

        @pl.when(pl.program_id(0) == 0)
        def _():
            l_ref[...] = part

        @pl.when(pl.program_id(0) > 0)
        def _():
            l_ref[...] += part

    row = pl.BlockSpec((tr, d), lambda i: (i, 0))
    return pl.pallas_call(
        body, name=name, grid=(s // tr,), in_specs=[row, row],
        out_specs=[row, pl.BlockSpec((SUBLANES, d), lambda i: (0, 0))],
        out_shape=[jax.ShapeDtypeStruct((s, d), F32), jax.ShapeDtypeStruct((SUBLANES, d), F32)],
        compiler_params=_params("arbitrary"),
    )(y, target)


def _ffn_up(name, h, gu_w):
    s, d = h.shape
    nb, _, fs = gu_w.shape
    hb = nb // 2
    w = gu_w.reshape(2, hb, d, fs)
    tm = _tile(s, 512)
    tn = _tile(fs, 1408)
    nj = fs // tn

    def body(h_ref, w_ref, gu_ref, a_ref):
        hv = h_ref[...]
        g = _dot(hv, w_ref[0], "nn")
        u = _dot(hv, w_ref[1], "nn")
        gu_ref[0] = g.astype(gu_ref.dtype)
        gu_ref[1] = u.astype(gu_ref.dtype)
        a_ref[...] = (g * jax.nn.sigmoid(g) * u).astype(a_ref.dtype)

    return pl.pallas_call(
        body, name=name, grid=(hb, nj, s // tm),
        in_specs=[pl.BlockSpec((tm, d), lambda jb, jo, i: (i, 0)),
                  pl.BlockSpec((2, None, d, tn), lambda jb, jo, i: (0, jb, 0, jo))],
        out_specs=[pl.BlockSpec((2, None, tm, tn), lambda jb, jo, i: (0, jb, i, jo)),
                   pl.BlockSpec((tm, tn), lambda jb, jo, i: (i, jb * nj + jo))],
        out_shape=[jax.ShapeDtypeStruct((2, hb, s, fs), BF16), jax.ShapeDtypeStruct((s, hb * fs), BF16)],
        compiler_params=_params("parallel", "parallel", "parallel"),
    )(h, w)


def _ffn_dact(name, dy, dn_w, gu, deps=()):
    s, d = dy.shape
    _, hb, _, fs = gu.shape
    tm = _tile(s, 512)
    tn = _tile(fs, 1408)
    nj = fs // tn

    def body(dy_ref, w_ref, gu_ref, *rest):
        o_ref = rest[-1]
        da = _dot(dy_ref[...], w_ref[...], "nt")
        g = gu_ref[0].astype(F32)
        u = gu_ref[1].astype(F32)
        sg = jax.nn.sigmoid(g)
        o_ref[0] = (da * u * (sg * (1.0 + g * (1.0 - sg)))).astype(o_ref.dtype)
        o_ref[1] = (da * (g * sg)).astype(o_ref.dtype)

    blk = pl.BlockSpec((2, None, tm, tn), lambda jb, jo, i: (0, jb, i, jo))
    return pl.pallas_call(
        body, name=name, grid=(hb, nj, s // tm),
        in_specs=[pl.BlockSpec((tm, d), lambda jb, jo, i: (i, 0)),
                  pl.BlockSpec((tn, d), lambda jb, jo, i: (jb * nj + jo, 0)),
                  blk] + [ANY_SPEC] * len(deps),
        out_specs=blk, out_shape=jax.ShapeDtypeStruct(gu.shape, BF16),
        compiler_params=_params("parallel", "parallel", "parallel"),
    )(dy, dn_w, gu, *deps)


_MASKED = -1e30


def _attn_bias(s, tq):
    nd = s // tq
    dist = (jnp.arange(nd)[:, None, None] * tq + jnp.arange(tq)[None, :, None]) - jnp.arange(tq)[None, None, :]
    mult = jnp.zeros(dist.shape, F32)
    for window, dilation in DILATED_BRANCHES:
        mult = mult + ((dist >= 0) & (dist <= window) & (dist % dilation == 0)).astype(F32)
    return jnp.where(mult > 0.0, jnp.log(jnp.maximum(mult, 1.0)), _MASKED)


def _biased(sc, bias, scale):
    tq, tk = bias.shape
    return (sc.reshape(-1, tq, tk) * scale + bias[None]).reshape(sc.shape)


def _attn_specs(s, qd, kvd, tq):
    rw = Q_PER_KV * HEAD_DIM
    qspec = pl.BlockSpec((tq, rw), lambda g, i: (i, g))
    kspec = pl.BlockSpec((s, HEAD_DIM), lambda g, i: (0, qd // HEAD_DIM + g))
    vspec = pl.BlockSpec((s, HEAD_DIM), lambda g, i: (0, (qd + kvd) // HEAD_DIM + g))
    return rw, qspec, kspec, vspec


def _attn_fwd(name, z, qd, kvd):
    s = z.shape[0]
    tq = _tile(s, 256)
    nkv = kvd // HEAD_DIM
    rw, qspec, kspec, vspec = _attn_specs(s, qd, kvd, tq)
    scale = HEAD_DIM ** -0.5

    def body(q_ref, k_ref, v_ref, b_ref, o_ref, l_ref):
        i = pl.program_id(1)
        heads = [slice(h * HEAD_DIM, (h + 1) * HEAD_DIM) for h in range(Q_PER_KV)]
        q_all = jnp.concatenate([q_ref[:, cols] for cols in heads], axis=0)

        def chunk(j, carry):
            mx, den, acc = carry
            k0 = pl.multiple_of(j * tq, tq)
            kc, vc = k_ref[pl.ds(k0, tq), :], v_ref[pl.ds(k0, tq), :]
            sc = _biased(_dot(q_all, kc, "nt"), b_ref[i - j], scale)
            mx_new = jnp.maximum(mx, jnp.max(sc, axis=-1, keepdims=True))
            alpha = jnp.exp(mx - mx_new)
            p = jnp.exp(sc - mx_new)
            return (mx_new, alpha * den + jnp.sum(p, axis=-1, keepdims=True),
                    alpha * acc + _dot(p.astype(BF16), vc, "nn"))

        rows = Q_PER_KV * tq
        init = (jnp.full((rows, 1), _MASKED, F32), jnp.zeros((rows, 1), F32), jnp.zeros((rows, HEAD_DIM), F32))
        mx, den, acc = lax.fori_loop(0, i + 1, chunk, init)
        out = acc / den
        lse = mx + jnp.log(den)
        for h, cols in enumerate(heads):
            o_ref[:, cols] = out[h * tq:(h + 1) * tq]
            l_ref[:, cols] = jnp.broadcast_to(lse[h * tq:(h + 1) * tq], (tq, HEAD_DIM))

    bias = _attn_bias(s, tq)
    return pl.pallas_call(
        body, name=name, grid=(nkv, s // tq),
        in_specs=[qspec, kspec, vspec, pl.BlockSpec(bias.shape, lambda g, i: (0, 0, 0))], out_specs=[qspec, qspec],
        out_shape=[jax.ShapeDtypeStruct((s, qd), F32), jax.ShapeDtypeStruct((s, qd), F32)],
        compiler_params=_params("parallel", "parallel"),
    )(z, z, z, bias)


def _attn_bwd(name, z, o, lse, do, qd, kvd):
    s = z.shape[0]
    tq = _tile(s, 256)
    nkv = kvd // HEAD_DIM
    nq = s // tq
    rw, qspec, kspec, vspec = _attn_specs(s, qd, kvd, tq)
    scale = HEAD_DIM ** -0.5

    def body(q_ref, k_ref, v_ref, o_ref, l_ref, do_ref, b_ref, dq_ref, dk_ref, dv_ref, dk_acc, dv_acc):
        i = pl.program_id(1)
        heads = [slice(h * HEAD_DIM, (h + 1) * HEAD_DIM) for h in range(Q_PER_KV)]

        @pl.when(i == 0)
        def _():
            dk_acc[...] = jnp.zeros_like(dk_acc)
            dv_acc[...] = jnp.zeros_like(dv_acc)

        q_all = jnp.concatenate([q_ref[:, cols] for cols in heads], axis=0)
        do_all = jnp.concatenate([do_ref[:, cols].astype(BF16) for cols in heads], axis=0)
        lse_all = jnp.concatenate([l_ref[:, cols][:, :1] for cols in heads], axis=0)
        delta_all = jnp.concatenate(
            [jnp.sum(do_ref[:, cols] * o_ref[:, cols], axis=-1, keepdims=True) for cols in heads], axis=0)

        def chunk(j, dq):
            k0 = pl.multiple_of(j * tq, tq)
            kc, vc = k_ref[pl.ds(k0, tq), :], v_ref[pl.ds(k0, tq), :]
            p = jnp.exp(_biased(_dot(q_all, kc, "nt"), b_ref[i - j], scale) - lse_all)
            ds = (p * (_dot(do_all, vc, "nt") - delta_all) * scale).astype(BF16)
            dk_acc[pl.ds(k0, tq), :] += _dot(ds, q_all, "tn")
            dv_acc[pl.ds(k0, tq), :] += _dot(p.astype(BF16), do_all, "tn")
            return dq + _dot(ds, kc, "nn")

        dq = lax.fori_loop(0, i + 1, chunk, jnp.zeros((Q_PER_KV * tq, HEAD_DIM), F32))
        for h, cols in enumerate(heads):
            dq_ref[:, cols] = dq[h * tq:(h + 1) * tq].astype(dq_ref.dtype)

        @pl.when(i == nq - 1)
        def _():
            dk_ref[...] = dk_acc[...].astype(dk_ref.dtype)
            dv_ref[...] = dv_acc[...].astype(dv_ref.dtype)

    kvout = pl.BlockSpec((s, HEAD_DIM), lambda g, i: (0, g))
    bias = _attn_bias(s, tq)
    return pl.pallas_call(
        body, name=name, grid=(nkv, nq),
        in_specs=[qspec, kspec, vspec, qspec, qspec, qspec, pl.BlockSpec(bias.shape, lambda g, i: (0, 0, 0))],
        out_specs=[qspec, kvout, kvout],
        out_shape=[jax.ShapeDtypeStruct((s, qd), BF16), jax.ShapeDtypeStruct((s, kvd), BF16),
                   jax.ShapeDtypeStruct((s, kvd), BF16)],
        scratch_shapes=[pltpu.VMEM((s, HEAD_DIM), F32), pltpu.VMEM((s, HEAD_DIM), F32)],
        compiler_params=_params("parallel", "arbitrary"),
    )(z, z, z, o, lse, do, bias)


def _shift_down(v, n):
    rolled = pltpu.roll(v, n, 0)
    t = lax.broadcasted_iota(jnp.int32, v.shape, 0)
    return jnp.where(t >= n, rolled, 0.0)


def _shift_up(v, n):
    rows = v.shape[0]
    rolled = pltpu.roll(v, rows - n, 0)
    t = lax.broadcasted_iota(jnp.int32, v.shape, 0)
    return jnp.where(t < rows - n, rolled, 0.0)


def _conv_specs(s, base, cd, tc):
    zs = [pl.BlockSpec((s, tc), functools.partial(lambda j, off: (0, off + j), off=(base + n * cd) // tc))
          for n in range(3)]
    wspec = pl.BlockSpec((SUBLANES, tc), lambda j: (0, j))
    cspec = pl.BlockSpec((s, tc), lambda j: (0, j))
    return zs, wspec, cspec


def _conv_fwd(name, z, conv_w, base, cd):
    s = z.shape[0]
    tc = _tile(cd, 256)
    zs, wspec, cspec = _conv_specs(s, base, cd, tc)

    def body(h_ref, b_ref, c_ref, w_ref, o_ref):
        u = c_ref[...].astype(F32) * h_ref[...].astype(F32)
        y = w_ref[0:1, :] * _shift_down(u, 2) + w_ref[1:2, :] * _shift_down(u, 1) + w_ref[2:3, :] * u
        o_ref[...] = b_ref[...].astype(F32) * y

    return pl.pallas_call(
        body, name=name, grid=(cd // tc,), in_specs=zs + [wspec], out_specs=cspec,
        out_shape=jax.ShapeDtypeStruct((s, cd), F32), compiler_params=_params("parallel"),
    )(z, z, z, conv_w)


def _conv_bwd(name, z, conv_w, dc, base, cd):
    s = z.shape[0]
    tc = _tile(cd, 256)
    zs, wspec, cspec = _conv_specs(s, base, cd, tc)

    def body(h_ref, b_ref, c_ref, w_ref, dc_ref, dh_ref, db_ref, dcg_ref, dw_ref):
        hv, bv, cv = h_ref[...].astype(F32), b_ref[...].astype(F32), c_ref[...].astype(F32)
        u = cv * hv
        u1, u2 = _shift_down(u, 1), _shift_down(u, 2)
        w0, w1, w2 = w_ref[0:1, :], w_ref[1:2, :], w_ref[2:3, :]
        y = w0 * u2 + w1 * u1 + w2 * u
        dcv = dc_ref[...]
        db_ref[...] = (dcv * y).astype(db_ref.dtype)
        dy = dcv * bv
        du = w2 * dy + w1 * _shift_up(dy, 1) + w0 * _shift_up(dy, 2)
        dh_ref[...] = (du * cv).astype(dh_ref.dtype)
        dcg_ref[...] = (du * hv).astype(dcg_ref.dtype)
        g0 = jnp.sum(dy * u2, axis=0, keepdims=True)
        g1 = jnp.sum(dy * u1, axis=0, keepdims=True)
        g2 = jnp.sum(dy * u, axis=0, keepdims=True)
        r = lax.broadcasted_iota(jnp.int32, (SUBLANES, tc), 0)
        dw_ref[...] = jnp.where(r == 0, g0, jnp.where(r == 1, g1, jnp.where(r == 2, g2, 0.0)))

    return pl.pallas_call(
        body, name=name, grid=(cd // tc,), in_specs=zs + [wspec, cspec],
        out_specs=[cspec, cspec, cspec, wspec],
        out_shape=[jax.ShapeDtypeStruct((s, cd), BF16)] * 3 + [jax.ShapeDtypeStruct((SUBLANES, cd), F32)],
        compiler_params=_params("parallel"),
    )(z, z, z, conv_w, dc)


def _cat_norm_fwd(name, a, c, ga, gc):
    s, qd = a.shape
    cd = c.shape[1]
    tr = _row_tile(s, qd + cd)

    def body(a_ref, c_ref, ga_ref, gc_ref, o_ref):
        av, cv = a_ref[...], c_ref[...]
        ra = lax.rsqrt(jnp.mean(av * av, axis=-1, keepdims=True) + NORM_EPS)
        rc = lax.rsqrt(jnp.mean(cv * cv, axis=-1, keepdims=True) + NORM_EPS)
        o_ref[:, :qd] = (av * ra * ga_ref[...]).astype(o_ref.dtype)
        o_ref[:, qd:] = (cv * rc * gc_ref[...]).astype(o_ref.dtype)

    return pl.pallas_call(
        body, name=name, grid=(s // tr,),
        in_specs=[pl.BlockSpec((tr, qd), lambda i: (i, 0)), pl.BlockSpec((tr, cd), lambda i: (i, 0)),
                  pl.BlockSpec((1, qd), lambda i: (0, 0)), pl.BlockSpec((1, cd), lambda i: (0, 0))],
        out_specs=pl.BlockSpec((tr, qd + cd), lambda i: (i, 0)),
        out_shape=jax.ShapeDtypeStruct((s, qd + cd), BF16), compiler_params=_params("parallel"),
    )(a, c, ga, gc)


def _cat_norm_bwd(name, dcat, a, c, ga, gc):
    s, qd = a.shape
    cd = c.shape[1]
    tr = _row_tile(s, qd + cd)

    def one(dn, yv, gv):
        r = lax.rsqrt(jnp.mean(yv * yv, axis=-1, keepdims=True) + NORM_EPS)
        xhat = yv * r
        dxn = dn * gv
        return r * (dxn - xhat * jnp.mean(dxn * xhat, axis=-1, keepdims=True)), _sum_to_sublanes(dn * xhat)

    def body(d_ref, a_ref, c_ref, ga_ref, gc_ref, da_ref, dc_ref, dga_ref, dgc_ref):
        da, pa = one(d_ref[:, :qd], a_ref[...], ga_ref[...])
        dc, pc = one(d_ref[:, qd:], c_ref[...], gc_ref[...])
        da_ref[...] = da
        dc_ref[...] = dc

        @pl.when(pl.program_id(0) == 0)
        def _():
            dga_ref[...] = pa
            dgc_ref[...] = pc

        @pl.when(pl.program_id(0) > 0)
        def _():
            dga_ref[...] += pa
            dgc_ref[...] += pc

    ra = pl.BlockSpec((tr, qd), lambda i: (i, 0))
    rc = pl.BlockSpec((tr, cd), lambda i: (i, 0))
    return pl.pallas_call(
        body, name=name, grid=(s // tr,),
        in_specs=[pl.BlockSpec((tr, qd + cd), lambda i: (i, 0)), ra, rc,
                  pl.BlockSpec((1, qd), lambda i: (0, 0)), pl.BlockSpec((1, cd), lambda i: (0, 0))],
        out_specs=[ra, rc, pl.BlockSpec((SUBLANES, qd), lambda i: (0, 0)),
                   pl.BlockSpec((SUBLANES, cd), lambda i: (0, 0))],
        out_shape=[jax.ShapeDtypeStruct((s, qd), F32), jax.ShapeDtypeStruct((s, cd), F32),
                   jax.ShapeDtypeStruct((SUBLANES, qd), F32), jax.ShapeDtypeStruct((SUBLANES, cd), F32)],
        compiler_params=_params("arbitrary"),
    )(dcat, a, c, ga, gc)


def _adamw(name, w, g, m, v, emit_grad=False):
    shape = w.shape
    cols = shape[-1]
    rows = w.size // cols
    tr = _row_tile(rows, cols, budget=3 << 19)
    bc1 = 1.0 - ADAM_B1 ** ADAM_STEP
    bc2 = 1.0 - ADAM_B2 ** ADAM_STEP
    n_out = 4 if emit_grad else 3

    def body(w_ref, g_ref, m_ref, v_ref, d_ref, nm_ref, nv_ref, *g_out):
        gv = g_ref[...]
        mv = ADAM_B1 * m_ref[...] + (1.0 - ADAM_B1) * gv
        vv = ADAM_B2 * v_ref[...] + (1.0 - ADAM_B2) * (gv * gv)
        nm_ref[...] = mv
        nv_ref[...] = vv
        d_ref[...] = -ADAM_LR * ((mv / bc1) / (jnp.sqrt(vv / bc2) + ADAM_EPS) + ADAM_WD * w_ref[...])
        for ref in g_out:
            ref[...] = gv

    row = pl.BlockSpec((tr, cols), lambda i: (i, 0))
    outs = pl.pallas_call(
        body, name=name, grid=(rows // tr,), in_specs=[row] * 4, out_specs=[row] * n_out,
        out_shape=[jax.ShapeDtypeStruct((rows, cols), F32)] * n_out, compiler_params=_params("parallel"),
    )(*(t.reshape(rows, cols) for t in (w, g, m, v)))
    return tuple(t.reshape(shape) for t in outs)


HBM_SPEC = pl.BlockSpec(memory_space=pltpu.HBM)


def _mesh_place():
    x, y, c = lax.axis_index("x"), lax.axis_index("y"), lax.axis_index("c")
    other_chips = [(1 - x, y), (x, 1 - y), (1 - x, 1 - y)]
    return x, y, c, other_chips


def _cast_into_slot(name, w, layer, chip, deps=()):
    _, r, cols = w.shape
    tr = _row_tile(r, cols, budget=BIG_BLOCK)

    def body(chip_ref, w_ref, *rest):
        o_ref = rest[-1]
        o_ref[...] = w_ref[...].astype(o_ref.dtype)

    return pl.pallas_call(
        body, name=name,
        grid_spec=pltpu.PrefetchScalarGridSpec(
            num_scalar_prefetch=1, grid=(r // tr,),
            in_specs=[pl.BlockSpec((None, tr, cols), lambda i, chip_ref: (layer, i, 0))] + [ANY_SPEC] * len(deps),
            out_specs=pl.BlockSpec((None, tr, cols), lambda i, chip_ref: (chip_ref[0], i, 0))),
        out_shape=jax.ShapeDtypeStruct((N_CHIPS, r, cols), BF16), compiler_params=_params("parallel"),
    )(chip, w, *deps)


SEM_SPEC = pl.BlockSpec(memory_space=pltpu.SEMAPHORE)
SPLIT_COPY = pltpu.CompilerParams(has_side_effects=pltpu.SideEffectType.DATAFLOW_SIDE_EFFECTING)
N_OTHER = N_CHIPS - 1
TOKEN_SPEC = pl.BlockSpec(memory_space=pltpu.VMEM)
TOKEN_SHAPE = jax.ShapeDtypeStruct((SUBLANES, LANES), F32)


def _in_hbm(arr):
    return pltpu.with_memory_space_constraint(arr, pltpu.HBM)


def _half_rows(ref, chip_idx, core):
    r2 = ref.shape[1] // 2
    return ref.at[chip_idx, pl.ds(core * r2, r2), :]


def _gather_start(name, fulls, after):
    na = len(fulls)

    def body(*refs):
        f_refs = refs[na + 1:2 * na + 1]
        send_sems, recv_sems = refs[2 * na + 1:3 * na + 1], refs[3 * na + 1:4 * na + 1]
        token = refs[4 * na + 1]
        x, y, c, chips = _mesh_place()
        for a in range(na):
            mine = _half_rows(f_refs[a], 2 * x + y, c)
            for j, (cx, cy) in enumerate(chips):
                pltpu.make_async_remote_copy(
                    src_ref=mine, dst_ref=mine, send_sem=send_sems[a].at[j], recv_sem=recv_sems[a].at[j],
                    device_id=(cx, cy, c), device_id_type=MESH).start()
        token[...] = jnp.zeros_like(token)

    outs = pl.pallas_call(
        body, name=name, in_specs=[HBM_SPEC] * na + [ANY_SPEC],
        out_specs=[HBM_SPEC] * na + [SEM_SPEC] * (2 * na) + [TOKEN_SPEC],
        out_shape=[pltpu.HBM(f.shape, f.dtype) for f in fulls] + [pltpu.SemaphoreType.DMA((N_OTHER,))] * (2 * na)
        + [TOKEN_SHAPE],
        input_output_aliases={a: a for a in range(na)}, compiler_params=SPLIT_COPY,
    )(*[_in_hbm(f) for f in fulls], after)
    return list(outs[:na]), list(outs[na:2 * na]), list(outs[2 * na:3 * na]), outs[3 * na]


def _gather_pass_on(name, full, recv_sems, after):
    def body(f_in, recv_sems, after_ref, f_ref, d2d_send, d2d_recv):
        x, y, c, chips = _mesh_place()
        for j, (cx, cy) in enumerate(chips):
            blk = _half_rows(f_ref, 2 * cx + cy, c)
            pltpu.make_async_remote_copy(
                src_ref=blk, dst_ref=blk, send_sem=d2d_send.at[j], recv_sem=recv_sems.at[j],
                device_id=(cx, cy, c), device_id_type=MESH).wait_recv()
            pltpu.make_async_remote_copy(
                src_ref=blk, dst_ref=blk, send_sem=d2d_send.at[j], recv_sem=d2d_recv.at[j],
                device_id=(x, y, 1 - c), device_id_type=MESH).start()

    return pl.pallas_call(
        body, name=name, in_specs=[HBM_SPEC, SEM_SPEC, ANY_SPEC], out_specs=[HBM_SPEC, SEM_SPEC, SEM_SPEC],
        out_shape=[pltpu.HBM(full.shape, full.dtype)] + [pltpu.SemaphoreType.DMA((N_OTHER,))] * 2,
        input_output_aliases={0: 0}, compiler_params=SPLIT_COPY,
    )(full, recv_sems, after)


def _gather_arrive(name, full, ici_send, d2d_send, d2d_recv, after):
    def body(f_in, ici_send, d2d_send, d2d_recv, after_ref, f_ref):
        x, y, c, chips = _mesh_place()
        for j, (cx, cy) in enumerate(chips):
            mine = _half_rows(f_ref, 2 * x + y, c)
            passed = _half_rows(f_ref, 2 * cx + cy, c)
            theirs = _half_rows(f_ref, 2 * cx + cy, 1 - c)
            pltpu.make_async_remote_copy(
                src_ref=mine, dst_ref=mine, send_sem=ici_send.at[j], recv_sem=d2d_recv.at[j],
                device_id=(cx, cy, c), device_id_type=MESH).wait_send()
            pltpu.make_async_remote_copy(
                src_ref=passed, dst_ref=passed, send_sem=d2d_send.at[j], recv_sem=d2d_recv.at[j],
                device_id=(x, y, 1 - c), device_id_type=MESH).wait_send()
            pltpu.make_async_remote_copy(
                src_ref=theirs, dst_ref=theirs, send_sem=d2d_send.at[j], recv_sem=d2d_recv.at[j],
                device_id=(x, y, 1 - c), device_id_type=MESH).wait_recv()

    return pl.pallas_call(
        body, name=name, in_specs=[HBM_SPEC, SEM_SPEC, SEM_SPEC, SEM_SPEC, ANY_SPEC], out_specs=HBM_SPEC,
        out_shape=pltpu.HBM(full.shape, full.dtype), input_output_aliases={0: 0}, compiler_params=SPLIT_COPY,
    )(full, ici_send, d2d_send, d2d_recv, after)


def _gather_taps(conv_w):
    def body(cw_ref, cwf_ref, send_sems, recv_sems, local_sem):
        x, y, c, chips = _mesh_place()
        k_me = 2 * x + y
        local = pltpu.make_async_copy(cw_ref, cwf_ref.at[k_me], local_sem)
        local.start()
        copies = [pltpu.make_async_remote_copy(
            src_ref=cw_ref, dst_ref=cwf_ref.at[k_me], send_sem=send_sems.at[j], recv_sem=recv_sems.at[j],
            device_id=(cx, cy, c), device_id_type=MESH) for j, (cx, cy) in enumerate(chips)]
        for cp in copies:
            cp.start()
        for j, (cx, cy) in enumerate(chips):
            pltpu.make_async_remote_copy(
                src_ref=cw_ref, dst_ref=cwf_ref.at[2 * cx + cy], send_sem=send_sems.at[j], recv_sem=recv_sems.at[j],
                device_id=(cx, cy, c), device_id_type=MESH).wait_recv()
        for cp in copies:
            cp.wait_send()
        local.wait()

    return pl.pallas_call(
        body, name="gather_taps", in_specs=[HBM_SPEC], out_specs=HBM_SPEC,
        out_shape=jax.ShapeDtypeStruct((N_CHIPS,) + conv_w.shape, conv_w.dtype),
        scratch_shapes=[pltpu.SemaphoreType.DMA((N_OTHER,))] * 2 + [pltpu.SemaphoreType.DMA],
    )(conv_w)


def _sibling_half(g_ref, c):
    r2 = g_ref.shape[1] // 2
    return g_ref.at[:, pl.ds((1 - c) * r2, r2), :]


def _swap_start(name, g):
    def body(g_in, g_ref, land_ref, send_sem, recv_sem, token):
        x, y, c, _ = _mesh_place()
        pltpu.make_async_remote_copy(
            src_ref=_sibling_half(g_ref, c), dst_ref=land_ref, send_sem=send_sem, recv_sem=recv_sem,
            device_id=(x, y, 1 - c), device_id_type=MESH).start()
        token[...] = jnp.zeros_like(token)

    nb, r, cols = g.shape
    return pl.pallas_call(
        body, name=name, in_specs=[HBM_SPEC], out_specs=[HBM_SPEC, HBM_SPEC, SEM_SPEC, SEM_SPEC, TOKEN_SPEC],
        out_shape=[pltpu.HBM(g.shape, g.dtype), pltpu.HBM((nb, r // 2, cols), g.dtype),
                   pltpu.SemaphoreType.DMA(()), pltpu.SemaphoreType.DMA(()), TOKEN_SHAPE],
        input_output_aliases={0: 0}, compiler_params=SPLIT_COPY,
    )(_in_hbm(g))


def _swap_wait(name, g, land, send_sem, recv_sem, after):
    def body(g_in, land_in, send_sem, recv_sem, after_ref, g_ref, land_ref):
        x, y, c, _ = _mesh_place()
        copy = pltpu.make_async_remote_copy(
            src_ref=_sibling_half(g_ref, c), dst_ref=land_ref, send_sem=send_sem, recv_sem=recv_sem,
            device_id=(x, y, 1 - c), device_id_type=MESH)
        copy.wait_send()
        copy.wait_recv()

    return pl.pallas_call(
        body, name=name, in_specs=[HBM_SPEC, HBM_SPEC, SEM_SPEC, SEM_SPEC, ANY_SPEC], out_specs=[HBM_SPEC, HBM_SPEC],
        out_shape=[pltpu.HBM(g.shape, g.dtype), pltpu.HBM(land.shape, land.dtype)],
        input_output_aliases={0: 0, 1: 1}, compiler_params=SPLIT_COPY,
    )(g, land, send_sem, recv_sem, after)


def _add_core_halves(name, g, sib, core):
    nb, r, cols = g.shape
    r2 = r // 2
    tr = _row_tile(r2, cols, itemsize=2, budget=BIG_BLOCK)
    nrt = r2 // tr

    def body(core_ref, g_ref, s_ref, o_ref):
        o_ref[...] = (g_ref[...].astype(F32) + s_ref[...].astype(F32)).astype(o_ref.dtype)

    return pl.pallas_call(
        body, name=name,
        grid_spec=pltpu.PrefetchScalarGridSpec(
            num_scalar_prefetch=1, grid=(nb, nrt),
            in_specs=[pl.BlockSpec((None, tr, cols), lambda k, i, core_ref: (k, core_ref[0] * nrt + i, 0)),
                      pl.BlockSpec((None, tr, cols), lambda k, i, core_ref: (k, i, 0))],
            out_specs=pl.BlockSpec((None, tr, cols), lambda k, i, core_ref: (k, i, 0))),
        out_shape=jax.ShapeDtypeStruct((nb, r2, cols), BF16), compiler_params=_params("parallel", "parallel"),
    )(core, g, sib)


def _scatter_copy(h_ref, land_ref, send_sems, recv_sems, j, chip_xy, c):
    cx, cy = chip_xy
    return pltpu.make_async_remote_copy(
        src_ref=h_ref.at[2 * cx + cy], dst_ref=land_ref.at[j], send_sem=send_sems.at[j], recv_sem=recv_sems.at[j],
        device_id=(cx, cy, c), device_id_type=MESH)


def _scatter_start(name, h):
    def body(h_in, h_ref, land_ref, send_sems, recv_sems, token):
        x, y, c, chips = _mesh_place()
        for j, chip_xy in enumerate(chips):
            _scatter_copy(h_ref, land_ref, send_sems, recv_sems, j, chip_xy, c).start()
        token[...] = jnp.zeros_like(token)

    return pl.pallas_call(
        body, name=name, in_specs=[HBM_SPEC], out_specs=[HBM_SPEC, HBM_SPEC, SEM_SPEC, SEM_SPEC, TOKEN_SPEC],
        out_shape=[pltpu.HBM(h.shape, h.dtype), pltpu.HBM((N_OTHER,) + h.shape[1:], h.dtype),
                   pltpu.SemaphoreType.DMA((N_OTHER,)), pltpu.SemaphoreType.DMA((N_OTHER,)), TOKEN_SHAPE],
        input_output_aliases={0: 0}, compiler_params=SPLIT_COPY,
    )(_in_hbm(h))


def _scatter_wait(name, h, land, send_sems, recv_sems, after):
    afters = tuple(after) if isinstance(after, (tuple, list)) else (after,)

    def body(h_in, land_in, send_sems, recv_sems, *rest):
        h_ref, land_ref = rest[-2:]
        x, y, c, chips = _mesh_place()
        for j, chip_xy in enumerate(chips):
            copy = _scatter_copy(h_ref, land_ref, send_sems, recv_sems, j, chip_xy, c)
            copy.wait_send()
            copy.wait_recv()

    return pl.pallas_call(
        body, name=name, in_specs=[HBM_SPEC, HBM_SPEC, SEM_SPEC, SEM_SPEC] + [ANY_SPEC] * len(afters),
        out_specs=[HBM_SPEC, HBM_SPEC],
        out_shape=[pltpu.HBM(h.shape, h.dtype), pltpu.HBM(land.shape, land.dtype)],
        input_output_aliases={0: 0, 1: 1}, compiler_params=SPLIT_COPY,
    )(h, land, send_sems, recv_sems, *afters)


def _sum_chips(name, hs, rcv, core, chip, layer, n_layers, prev):
    _, r2, cols = hs.shape
    tr = _row_tile(r2, cols, budget=BIG_BLOCK)
    nrt = r2 // tr

    def body(core_ref, chip_ref, h_ref, r_ref, *rest):
        o_ref = rest[-1]
        acc = h_ref[...].astype(F32)
        for j in range(N_CHIPS - 1):
            acc = acc + r_ref[j].astype(F32)
        o_ref[...] = acc

    in_specs = [pl.BlockSpec((None, tr, cols), lambda i, core_ref, chip_ref: (chip_ref[0], i, 0)),
                pl.BlockSpec((N_CHIPS - 1, tr, cols), lambda i, core_ref, chip_ref: (0, i, 0))]
    args = [core, chip, hs, rcv]
    aliases = {}
    if prev is not None:
        in_specs.append(pl.BlockSpec(memory_space=pl.ANY))
        args.append(prev)
        aliases = {4: 0}
    return pl.pallas_call(
        body, name=name,
        grid_spec=pltpu.PrefetchScalarGridSpec(
            num_scalar_prefetch=2, grid=(nrt,), in_specs=in_specs,
            out_specs=pl.BlockSpec((None, tr, cols), lambda i, core_ref, chip_ref: (layer, core_ref[0] * nrt + i, 0))),
        out_shape=jax.ShapeDtypeStruct((n_layers, 2 * r2, cols), F32), input_output_aliases=aliases,
        compiler_params=_params("parallel"),
    )(*args)


def _join_core_halves(name, ts, deps=()):
    na, nd = len(ts), len(deps)

    def body(*refs):
        o_refs = refs[na + nd:2 * na + nd]
        send_sems, recv_sems = refs[2 * na + nd:]
        x, y, c, _ = _mesh_place()
        copies = []
        for a in range(na):
            r2 = o_refs[a].shape[1] // 2
            mine = o_refs[a].at[:, pl.ds(c * r2, r2), :]
            copies.append(pltpu.make_async_remote_copy(
                src_ref=mine, dst_ref=mine, send_sem=send_sems.at[a], recv_sem=recv_sems.at[a],
                device_id=(x, y, 1 - c), device_id_type=MESH))
        for cp in copies:
            cp.start()
        for cp in copies:
            cp.wait()

    return pl.pallas_call(
        body, name=name, in_specs=[HBM_SPEC] * na + [ANY_SPEC] * nd, out_specs=[HBM_SPEC] * na,
        out_shape=[jax.ShapeDtypeStruct(t.shape, t.dtype) for t in ts],
        input_output_aliases={a: a for a in range(na)},
        scratch_shapes=[pltpu.SemaphoreType.DMA((na,))] * 2,
    )(*ts, *deps)


def _allreduce_small(p):
    n, _, w = p.shape

    def body(p_ref, o_ref, buf, send_sems, recv_sems):
        x, y, c, _ = _mesh_place()
        me = 4 * x + 2 * y + c
        buf[me] = jnp.sum(p_ref[...], axis=1)
        copies = []
        for pat in range(1, N_DEV):
            fx, fy, fc = (pat >> 2) & 1, (pat >> 1) & 1, pat & 1
            copies.append(pltpu.make_async_remote_copy(
                src_ref=buf.at[me], dst_ref=buf.at[me], send_sem=send_sems.at[pat - 1], recv_sem=recv_sems.at[pat - 1],
                device_id=(x ^ fx, y ^ fy, c ^ fc), device_id_type=MESH))
        for cp in copies:
            cp.start()
        for cp in copies:
            cp.wait()
        acc = buf[0]
        for dev in range(1, N_DEV):
            acc = acc + buf[dev]
        o_ref[...] = acc

    return pl.pallas_call(
        body, name="allreduce_small", in_specs=[pl.BlockSpec(memory_space=pltpu.VMEM)],
        out_specs=pl.BlockSpec(memory_space=pltpu.VMEM), out_shape=jax.ShapeDtypeStruct((n, w), F32),
        scratch_shapes=[pltpu.VMEM((N_DEV, n, w), F32), pltpu.SemaphoreType.DMA((N_DEV - 1,)),
                        pltpu.SemaphoreType.DMA((N_DEV - 1,))],
    )(p)


class _WeightFeed:
    def __init__(self):
        self.fulls, self.ici_send, self.ici_recv, self.d2d = [], [], [], []

    def start(self, name, fulls, after):
        started, send, recv, token = _gather_start(name, fulls, after)
        self.fulls += started
        self.ici_send += send
        self.ici_recv += recv
        self.d2d += [None] * len(fulls)
        self.token = token
        return token

    def _pass_on(self, k, after):
        if k == 0:
            after = self.token
        if k < len(self.fulls) and self.d2d[k] is None:
            self.fulls[k], send, recv = _gather_pass_on(f"gather_pass_{k}", self.fulls[k], self.ici_recv[k], after)
            self.d2d[k] = (send, recv)

    def take(self, k, after):
        self._pass_on(k, after)
        self.fulls[k] = _gather_arrive(f"gather_arrive_{k}", self.fulls[k], self.ici_send[k], *self.d2d[k], after)
        return self.fulls[k]


def _ffn_forward(tag, x, g_pre, g_post, feed, k):
    s, d = x.shape
    h = _norm_fwd(f"{tag}_norm", x, g_pre)
    gu_w = feed.take(k, h)
    gu, a = _ffn_up(f"{tag}_up", h, gu_w)
    dn_w = feed.take(k + 1, a).reshape(-1, d)
    f = dn_w.shape[0]
    tk = _tile(f, 1408)
    tm, tn = _tile(s, 1024), _tile(d, 1024)
    y = _mm(f"{tag}_down", a, dn_w, mode="nn", grid=(s // tm, d // tn, f // tk),
            a_spec=pl.BlockSpec((tm, tk), lambda i, j, k: (i, k)),
            b_spec=pl.BlockSpec((tk, tn), lambda i, j, k: (k, j)),
            o_spec=pl.BlockSpec((tm, tn), lambda i, j, k: (i, j)),
            out_shape=jax.ShapeDtypeStruct((s, d), F32), nk=f // tk, acc_shape=(tm, tn))
    x_new = _res_norm(f"{tag}_post", x, y, g_post, FFN_RESIDUAL_WEIGHT)
    return x_new, (x, h, gu, a, y)


class _GradReduce:
    def __init__(self, core, chip, n_layers):
        self.core, self.chip, self.n_layers = core, chip, n_layers
        self.state = {}
        self.bufs = {}

    def start(self, kind, layer, g):
        g, land, send, recv, token = _swap_start(f"swap_start_{kind}_{layer}", g)
        self.state[kind, layer] = (g, land, send, recv)
        return token

    def exchange(self, kind, layer, after):
        tag = f"{kind}_{layer}"
        g, sib = _swap_wait(f"swap_wait_{tag}", *self.state[kind, layer], after)
        h = _add_core_halves(f"add_cores_{tag}", g, sib, self.core)
        h, land, send, recv, token = _scatter_start(f"scatter_start_{tag}", h)
        self.state[kind, layer] = (h, land, send, recv)
        return token

    def finish(self, kind, layer, after):
        tag = f"{kind}_{layer}"
        h, rcv = _scatter_wait(f"scatter_wait_{tag}", *self.state.pop((kind, layer)), after)
        self.bufs[kind] = _sum_chips(f"sum_chips_{tag}", h, rcv, self.core, self.chip, layer, self.n_layers,
                                     self.bufs.get(kind))
        return self.bufs[kind]


def _ffn_backward(tag, dx_new, saved, g_pre, g_post, gu_w, dn_w, red, kinds, layer, deps):
    x, h, gu, a, y = saved
    s, d = x.shape
    nb, fs = gu_w.shape[0], gu_w.shape[2]
    f = dn_w.shape[0]
    fr = f // nb
    dy, dg_post = _norm_bwd(f"{tag}_post_bwd", dx_new, y, g_post, FFN_RESIDUAL_WEIGHT, None, BF16)
    dgu = _ffn_dact(f"{tag}_dact", dy, dn_w, gu, deps)
    dgu4 = dgu.reshape(nb, s, fs)
    tn = _tile(d, 1024)
    d_wd = _mm(f"{tag}_dwd", a, dy, mode="tn", grid=(nb, d // tn),
               a_spec=pl.BlockSpec((s, fr), lambda i, j: (0, i)),
               b_spec=pl.BlockSpec((s, tn), lambda i, j: (0, j)),
               o_spec=pl.BlockSpec((None, fr, tn), lambda i, j: (i, 0, j)),
               out_shape=jax.ShapeDtypeStruct((nb, fr, d), BF16))
    tm, tw = _tile(d, 512), _tile(fs, 1408)
    nw = fs // tw
    d_wgu = _mm(f"{tag}_dwgu", h, dgu4, mode="tn", grid=(nb, nw, d // tm),
                a_spec=pl.BlockSpec((s, tm), lambda k, j, i: (0, i)),
                b_spec=pl.BlockSpec((None, s, tw), lambda k, j, i: (k, 0, j)),
                o_spec=pl.BlockSpec((None, tm, tw), lambda k, j, i: (k, i, j)),
                out_shape=jax.ShapeDtypeStruct((nb, d, fs), BF16))
    started = (red.start(kinds[0], layer, d_wgu), red.start(kinds[1], layer, d_wd))
    ts, td = _tile(s, 1024), _tile(d, 1024)
    dh = _mm(f"{tag}_dh", dgu4, gu_w, mode="nt", grid=(s // ts, d // td, nb),
             a_spec=pl.BlockSpec((None, ts, fs), lambda i, j, k: (k, i, 0)),
             b_spec=pl.BlockSpec((None, td, fs), lambda i, j, k: (k, j, 0)),
             o_spec=pl.BlockSpec((ts, td), lambda i, j, k: (i, j)),
             out_shape=jax.ShapeDtypeStruct((s, d), F32), nk=nb, acc_shape=(ts, td), deps=started)
    dx, dg_pre = _norm_bwd(f"{tag}_pre_bwd", dh, x, g_pre, 1.0, dx_new, F32)
    return dx, dg_pre, dg_post


def _mixer_forward(tag, x, gains, feed, k, conv_taps, dims):
    qd, kvd, cd = dims
    s, d = x.shape
    g_pre, g_a, g_c, g_post = gains
    h = _norm_fwd(f"{tag}_norm", x, g_pre)
    win_w = feed.take(k, h)
    nb, cw = win_w.shape[0], win_w.shape[2]
    tm = _tile(s, 1024)
    z = _mm(f"{tag}_in", h, win_w, mode="nn", grid=(nb, s // tm),
            a_spec=pl.BlockSpec((tm, d), lambda j, i: (i, 0)),
            b_spec=pl.BlockSpec((None, d, cw), lambda j, i: (j, 0, 0)),
            o_spec=pl.BlockSpec((tm, cw), lambda j, i: (i, j)),
            out_shape=jax.ShapeDtypeStruct((s, nb * cw), BF16))
    a, lse = _attn_fwd(f"{tag}_attn", z, qd, kvd)
    c = _conv_fwd(f"{tag}_conv", z, conv_taps, qd + 2 * kvd, cd)
    cat = _cat_norm_fwd(f"{tag}_cat", a, c, g_a, g_c)
    wout_w = feed.take(k + 1, cat).reshape(-1, d)
    mw = qd + cd
    tn = _tile(d, 1024)
    mixed = _mm(f"{tag}_out", cat, wout_w, mode="nn", grid=(s // tm, d // tn),
                a_spec=pl.BlockSpec((tm, mw), lambda i, j: (i, 0)),
                b_spec=pl.BlockSpec((mw, tn), lambda i, j: (0, j)),
                o_spec=pl.BlockSpec((tm, tn), lambda i, j: (i, j)),
                out_shape=jax.ShapeDtypeStruct((s, d), F32))
    x_new = _res_norm(f"{tag}_post", x, mixed, g_post, 1.0)
    return x_new, (x, h, z, a, lse, c, cat, mixed)


def _mixer_backward(tag, dx_new, saved, gains, win_w, conv_taps, wout_w, dims, red, kinds, layer, deps):
    qd, kvd, cd = dims
    x, h, z, a, lse, c, cat, mixed = saved
    s, d = x.shape
    nb, cw = win_w.shape[0], win_w.shape[2]
    g_pre, g_a, g_c, g_post = gains
    mw = qd + cd
    dmixed, dg_post = _norm_bwd(f"{tag}_post_bwd", dx_new, mixed, g_post, 1.0, None, BF16)
    tm, tn = _tile(s, 1024), _tile(mw, 1024)
    dcat = _mm(f"{tag}_dcat", dmixed, wout_w, mode="nt", grid=(s // tm, mw // tn),
               a_spec=pl.BlockSpec((tm, d), lambda i, j: (i, 0)),
               b_spec=pl.BlockSpec((tn, d), lambda i, j: (j, 0)),
               o_spec=pl.BlockSpec((tm, tn), lambda i, j: (i, j)),
               out_shape=jax.ShapeDtypeStruct((s, mw), F32), deps=deps)
    wr = mw // nb
    td = _tile(d, 1024)
    d_wout = _mm(f"{tag}_dwout", cat, dmixed, mode="tn", grid=(nb, d // td),
                 a_spec=pl.BlockSpec((s, wr), lambda i, j: (0, i)),
                 b_spec=pl.BlockSpec((s, td), lambda i, j: (0, j)),
                 o_spec=pl.BlockSpec((None, wr, td), lambda i, j: (i, 0, j)),
                 out_shape=jax.ShapeDtypeStruct((nb, wr, d), BF16))
    da, dc, dg_a, dg_c = _cat_norm_bwd(f"{tag}_cat_bwd", dcat, a, c, g_a, g_c)
    dhc, dbg, dcg, d_taps = _conv_bwd(f"{tag}_conv_bwd", z, conv_taps, dc, qd + 2 * kvd, cd)
    dq, dk, dv = _attn_bwd(f"{tag}_attn_bwd", z, a, lse, da, qd, kvd)
    dz = jnp.concatenate([dq, dk, dv, dhc, dbg, dcg], axis=1)
    th = _tile(d, 512)
    d_win = _mm(f"{tag}_dwin", h, dz, mode="tn", grid=(nb, d // th),
                a_spec=pl.BlockSpec((s, th), lambda k, i: (0, i)),
                b_spec=pl.BlockSpec((s, cw), lambda k, i: (0, k)),
                o_spec=pl.BlockSpec((None, th, cw), lambda k, i: (k, i, 0)),
                out_shape=jax.ShapeDtypeStruct((nb, d, cw), BF16))
    started = (red.start(kinds[0], layer, d_win), red.start(kinds[1], layer, d_wout))
    dh = _mm(f"{tag}_dh", dz, win_w, mode="nt", grid=(s // tm, d // td, nb),
             a_spec=pl.BlockSpec((tm, cw), lambda i, j, k: (i, k)),
             b_spec=pl.BlockSpec((None, td, cw), lambda i, j, k: (k, j, 0)),
             o_spec=pl.BlockSpec((tm, td), lambda i, j, k: (i, j)),
             out_shape=jax.ShapeDtypeStruct((s, d), F32), nk=nb, acc_shape=(tm, td), deps=started)
    dx, dg_pre = _norm_bwd(f"{tag}_pre_bwd", dh, x, g_pre, 1.0, dx_new, F32)
    return dx, d_taps, (dg_pre, dg_a, dg_c, dg_post)


def _pad_cols(v, width):
    return jnp.pad(v, ((0, 0), (0, width - v.shape[1])))


def kernel(x, ffn1_norm_pre, ffn1_w_gate_up, ffn1_w_down, ffn1_norm_post, mix_norm_pre, w_in, conv_w, attn_out_norm, conv_out_norm, w_out, mix_norm_post, ffn2_norm_pre, ffn2_w_gate_up, ffn2_w_down, ffn2_norm_post, loss_target, m_ffn1_norm_pre, m_ffn1_w_gate_up, m_ffn1_w_down, m_ffn1_norm_post, m_mix_norm_pre, m_w_in, m_conv_w, m_attn_out_norm, m_conv_out_norm, m_w_out, m_mix_norm_post, m_ffn2_norm_pre, m_ffn2_w_gate_up, m_ffn2_w_down, m_ffn2_norm_post, v_ffn1_norm_pre, v_ffn1_w_gate_up, v_ffn1_w_down, v_ffn1_norm_post, v_mix_norm_pre, v_w_in, v_conv_w, v_attn_out_norm, v_conv_out_norm, v_w_out, v_mix_norm_post, v_ffn2_norm_pre, v_ffn2_w_gate_up, v_ffn2_w_down, v_ffn2_norm_post):
    _, s, d = x.shape
    n_layers = ffn1_norm_pre.shape[0]
    qd = attn_out_norm.shape[1]
    cd = conv_out_norm.shape[1]
    kvd = qd // Q_PER_KV
    dims = (qd, kvd, cd)
    assert N_CHIPS * w_in.shape[2] == qd + 2 * kvd + 3 * cd and qd + cd == N_CHIPS * w_out.shape[1]
    assert 2 * d <= SMALL_ROWS * LANES * SUBLANES
    chip = 2 * lax.axis_index("x") + lax.axis_index("y")
    chip_arr = chip.astype(jnp.int32).reshape(1)
    core = lax.axis_index("c").astype(jnp.int32).reshape(1)
    kinds = ("gu1", "dn1", "win", "wout", "gu2", "dn2")

    big = (ffn1_w_gate_up, ffn1_w_down, w_in, w_out, ffn2_w_gate_up, ffn2_w_down)
    nk = len(kinds)
    taps_all = _gather_taps(conv_w)
    feed = _WeightFeed()
    order = [(k, w, layer) for layer in range(n_layers) for k, w in zip(kinds, big)]
    k, w, layer = order[0]
    token = feed.start("gather_start_first", [_cast_into_slot(f"cast_{k}_{layer}", w, layer, chip_arr)], taps_all)
    feed.start("gather_start_rest", [_cast_into_slot(f"cast_{k}_{layer}", w, layer, chip_arr, (token,))
                                     for k, w, layer in order[1:]], token)
    taps = jnp.transpose(taps_all, (1, 2, 0, 3)).reshape(n_layers, CONV_WIDTH, cd)
    taps = jnp.pad(taps, ((0, 0), (0, SUBLANES - CONV_WIDTH), (0, 0)))

    def gain(g, layer):
        return g[layer][None, :]

    xs = x[0]
    saved = []
    for layer in range(n_layers):
        t = f"l{layer}"
        k0 = layer * nk
        xs, s1 = _ffn_forward(f"{t}_ffn1", xs, gain(ffn1_norm_pre, layer), gain(ffn1_norm_post, layer), feed, k0)
        mix_gains = (gain(mix_norm_pre, layer), gain(attn_out_norm, layer), gain(conv_out_norm, layer), gain(mix_norm_post, layer))
        xs, s2 = _mixer_forward(f"{t}_mix", xs, mix_gains, feed, k0 + 2, taps[layer], dims)
        xs, s3 = _ffn_forward(f"{t}_ffn2", xs, gain(ffn2_norm_pre, layer), gain(ffn2_norm_post, layer), feed, k0 + 4)
        saved.append((s1, s2, s3, mix_gains))
    wts = {k: [feed.fulls[layer * nk + i] for layer in range(n_layers)] for i, k in enumerate(kinds)}
    for k in ("dn1", "wout", "dn2"):
        wts[k] = [w.reshape(-1, d) for w in wts[k]]
    dxs, loss_part = _loss_head("loss_head", xs, loss_target[0])
    loss = lax.psum(jnp.sum(loss_part), ("x", "y", "c"))

    red = _GradReduce(core, chip_arr, n_layers)
    small = [None] * n_layers
    flow = {"deps": (), "in_flight": []}

    def between(dx, new_keys):
        after = dx
        for key in flow["in_flight"]:
            after = red.finish(*key, after)
        flow["deps"] = tuple(red.exchange(*key, after) for key in new_keys)
        flow["in_flight"] = list(new_keys)

    for layer in reversed(range(n_layers)):
        t = f"l{layer}"
        s1, s2, s3, mix_gains = saved[layer]
        dxs, p_pre2, p_post2 = _ffn_backward(
            f"{t}_ffn2", dxs, s3, gain(ffn2_norm_pre, layer), gain(ffn2_norm_post, layer),
            wts["gu2"][layer], wts["dn2"][layer], red, ("gu2", "dn2"), layer, flow["deps"])
        between(dxs, [("gu2", layer), ("dn2", layer)])
        dxs, p_taps, (p_mpre, p_a, p_c, p_mpost) = _mixer_backward(
            f"{t}_mix", dxs, s2, mix_gains, wts["win"][layer], taps[layer], wts["wout"][layer], dims,
            red, ("win", "wout"), layer, flow["deps"])
        between(dxs, [("win", layer), ("wout", layer)])
        dxs, p_pre1, p_post1 = _ffn_backward(
            f"{t}_ffn1", dxs, s1, gain(ffn1_norm_pre, layer), gain(ffn1_norm_post, layer),
            wts["gu1"][layer], wts["dn1"][layer], red, ("gu1", "dn1"), layer, flow["deps"])
        between(dxs, [("gu1", layer), ("dn1", layer)])
        tap_rows = jnp.zeros((CONV_WIDTH, SUBLANES, d), F32).at[:, 0, :cd].set(p_taps[:CONV_WIDTH])
        rows = [p_pre1, p_post1, p_mpre, jnp.concatenate([p_a, p_c], axis=1), p_mpost, p_pre2, p_post2]
        rows = jnp.concatenate([jnp.stack(rows), tap_rows], axis=0)
        small[layer] = jnp.pad(rows, ((0, SMALL_ROWS - rows.shape[0]), (0, 0), (0, 0)))
    grad_x = dxs[None]

    weights = dict(ffn1_norm_pre=ffn1_norm_pre, ffn1_w_gate_up=ffn1_w_gate_up, ffn1_w_down=ffn1_w_down, ffn1_norm_post=ffn1_norm_post, mix_norm_pre=mix_norm_pre, w_in=w_in, conv_w=conv_w, attn_out_norm=attn_out_norm, conv_out_norm=conv_out_norm, w_out=w_out, mix_norm_post=mix_norm_post, ffn2_norm_pre=ffn2_norm_pre, ffn2_w_gate_up=ffn2_w_gate_up, ffn2_w_down=ffn2_w_down, ffn2_norm_post=ffn2_norm_post)
    m_in = dict(ffn1_norm_pre=m_ffn1_norm_pre, ffn1_w_gate_up=m_ffn1_w_gate_up, ffn1_w_down=m_ffn1_w_down, ffn1_norm_post=m_ffn1_norm_post, mix_norm_pre=m_mix_norm_pre, w_in=m_w_in, conv_w=m_conv_w, attn_out_norm=m_attn_out_norm, conv_out_norm=m_conv_out_norm, w_out=m_w_out, mix_norm_post=m_mix_norm_post, ffn2_norm_pre=m_ffn2_norm_pre, ffn2_w_gate_up=m_ffn2_w_gate_up, ffn2_w_down=m_ffn2_w_down, ffn2_norm_post=m_ffn2_norm_post)
    v_in = dict(ffn1_norm_pre=v_ffn1_norm_pre, ffn1_w_gate_up=v_ffn1_w_gate_up, ffn1_w_down=v_ffn1_w_down, ffn1_norm_post=v_ffn1_norm_post, mix_norm_pre=v_mix_norm_pre, w_in=v_w_in, conv_w=v_conv_w, attn_out_norm=v_attn_out_norm, conv_out_norm=v_conv_out_norm, w_out=v_w_out, mix_norm_post=v_mix_norm_post, ffn2_norm_pre=v_ffn2_norm_pre, ffn2_w_gate_up=v_ffn2_w_gate_up, ffn2_w_down=v_ffn2_w_down, ffn2_norm_post=v_ffn2_norm_post)
    kind_name = dict(gu1="ffn1_w_gate_up", dn1="ffn1_w_down", win="w_in", wout="w_out", gu2="ffn2_w_gate_up", dn2="ffn2_w_down")
    delta, new_m, new_v, grad = {}, {}, {}, {}

    def update(kind_list, joined):
        for k, g in zip(kind_list, joined):
            n = kind_name[k]
            delta[n], new_m[n], new_v[n], grad[n] = _adamw(f"adamw_{n}", weights[n], g, m_in[n], v_in[n], True)

    early = ("gu2", "dn2", "win", "wout")
    update(early, _join_core_halves("join_early", [red.bufs[k] for k in early], flow["deps"]))
    late = [k for (k, _) in flow["in_flight"]]
    for k, layer in flow["in_flight"]:
        red.finish(k, layer, [delta[kind_name[e]] for e in early])
    update(late, _join_core_halves("join_late", [red.bufs[k] for k in late]))

    small_sum = _allreduce_small(jnp.concatenate(small, axis=0)).reshape(n_layers, SMALL_ROWS, d)
    g_ffn1_pre, g_ffn1_post, g_mix_pre = small_sum[:, 0], small_sum[:, 1], small_sum[:, 2]
    g_attn_out, g_conv_out = small_sum[:, 3, :qd], small_sum[:, 3, qd:qd + cd]
    g_mix_post, g_ffn2_pre, g_ffn2_post = small_sum[:, 4], small_sum[:, 5], small_sum[:, 6]
    cc = conv_w.shape[2]
    g_conv = lax.dynamic_slice_in_dim(small_sum[:, 7:7 + CONV_WIDTH, :cd], chip * cc, cc, axis=2)

    grad.update(ffn1_norm_pre=g_ffn1_pre, ffn1_norm_post=g_ffn1_post, mix_norm_pre=g_mix_pre, conv_w=g_conv, attn_out_norm=g_attn_out, conv_out_norm=g_conv_out, mix_norm_post=g_mix_post, ffn2_norm_pre=g_ffn2_pre, ffn2_norm_post=g_ffn2_post)
    names = list(weights)

    vectors = [n for n in names if n not in kind_name.values()]

    def pack(tree):
        flat = jnp.concatenate([tree[n].reshape(-1) for n in vectors])
        return jnp.pad(flat, (0, -flat.size % (SUBLANES * LANES))).reshape(-1, LANES)

    packed = _adamw("adamw_small", pack(weights), pack(grad), pack(m_in), pack(v_in))
    offset = 0
    for n in vectors:
        size = weights[n].size
        for tree, flat in zip((delta, new_m, new_v), packed):
            tree[n] = flat.reshape(-1)[offset:offset + size].reshape(weights[n].shape)
        offset += size

    return (loss, grad_x, *[grad[n] for n in names], *[delta[n] for n in names],
            *[new_m[n] for n in names], *[new_v[n] for n in names])
```

```python
import functools

import jax
import jax.numpy as jnp
from jax import lax
from jax.experimental import pallas as pl
from jax.experimental.pallas import tpu as pltpu

F32 = jnp.float32
BF16 = jnp.bfloat16
MESH = pl.DeviceIdType.MESH

NORM_EPS = 1e-6
HEAD_DIM = 128
Q_PER_KV = 4
CONV_WIDTH = 3
FFN_RESIDUAL_WEIGHT = 0.5
DILATED_BRANCHES = ((128, 1), (512, 4), (2048, 16))
ADAM_LR = 0.001
ADAM_B1 = 0.9
ADAM_B2 = 0.999
ADAM_EPS = 1e-08
ADAM_WD = 0.01
ADAM_STEP = 10

N_CHIPS = 4
N_DEV = 8
V7X_VMEM_BYTES = 64 << 20
VMEM_LIMIT = V7X_VMEM_BYTES - (12 << 20)
SUBLANES = 8
LANES = 128
SMALL_ROWS = 16
BIG_BLOCK = 4 << 20


def _params(*sem):
    return pltpu.CompilerParams(dimension_semantics=sem, vmem_limit_bytes=VMEM_LIMIT)


def _row_tile(rows, cols, itemsize=4, budget=2 << 20):
    t = rows
    while t * cols * itemsize > budget and t % 32 == 0:
        t //= 2
    return t


def _sum_to_sublanes(v):
    r, n = v.shape
    return v.reshape(r // SUBLANES, SUBLANES, n).sum(axis=0)


_DIMS = {
    "nn": (((1,), (0,)), ((), ())),
    "nt": (((1,), (1,)), ((), ())),
    "tn": (((0,), (0,)), ((), ())),
}


ANY_SPEC = pl.BlockSpec(memory_space=pl.ANY)


def _dot(a, b, mode):
    return lax.dot_general(a, b, _DIMS[mode], preferred_element_type=F32)


def _mm(name, a, b, *, mode, grid, a_spec, b_spec, o_spec, out_shape, nk=1, acc_shape=None, deps=()):
    nd = len(deps)

    def body(a_ref, b_ref, *rest):
        o_ref, scratch = rest[nd], rest[nd + 1:]
        r = _dot(a_ref[...], b_ref[...], mode)
        if nk == 1:
            o_ref[...] = r.astype(o_ref.dtype)
        else:
            acc = scratch[0]
            k = pl.program_id(len(grid) - 1)

            @pl.when(k == 0)
            def _():
                acc[...] = r

            @pl.when(k > 0)
            def _():
                acc[...] += r

            @pl.when(k == nk - 1)
            def _():
                o_ref[...] = acc[...].astype(o_ref.dtype)

    sem = ("parallel",) * (len(grid) - (1 if nk > 1 else 0)) + (("arbitrary",) if nk > 1 else ())
    return pl.pallas_call(
        body, name=name, grid=grid, in_specs=[a_spec, b_spec] + [ANY_SPEC] * nd, out_specs=o_spec,
        out_shape=out_shape, scratch_shapes=[pltpu.VMEM(acc_shape, F32)] if nk > 1 else [],
        compiler_params=_params(*sem),
    )(a, b, *deps)


def _tile(n, want):
    if n <= want:
        return n
    best = None
    for t in range(LANES, want + 1, LANES):
        if n % t == 0:
            best = t
    assert best is not None, (n, want)
    return best


def _norm_fwd(name, x, gain):
    s, d = x.shape
    tr = _row_tile(s, d, budget=BIG_BLOCK)

    def body(x_ref, g_ref, o_ref):
        xv = x_ref[...]
        r = lax.rsqrt(jnp.mean(xv * xv, axis=-1, keepdims=True) + NORM_EPS)
        o_ref[...] = (xv * r * g_ref[...]).astype(o_ref.dtype)

    return pl.pallas_call(
        body, name=name, grid=(s // tr,),
        in_specs=[pl.BlockSpec((tr, d), lambda i: (i, 0)), pl.BlockSpec((1, d), lambda i: (0, 0))],
        out_specs=pl.BlockSpec((tr, d), lambda i: (i, 0)),
        out_shape=jax.ShapeDtypeStruct((s, d), BF16), compiler_params=_params("parallel"),
    )(x, gain)


def _res_norm(name, x, y, gain, scale):
    s, d = x.shape
    tr = _row_tile(s, d, budget=BIG_BLOCK)

    def body(x_ref, y_ref, g_ref, o_ref):
        yv = y_ref[...]
        r = lax.rsqrt(jnp.mean(yv * yv, axis=-1, keepdims=True) + NORM_EPS)
        o_ref[...] = x_ref[...] + scale * (yv * r * g_ref[...])

    row = pl.BlockSpec((tr, d), lambda i: (i, 0))
    return pl.pallas_call(
        body, name=name, grid=(s // tr,),
        in_specs=[row, row, pl.BlockSpec((1, d), lambda i: (0, 0))], out_specs=row,
        out_shape=jax.ShapeDtypeStruct((s, d), F32), compiler_params=_params("parallel"),
    )(x, y, gain)


def _norm_bwd(name, dout, yin, gain, scale, resid, out_dtype):
    s, d = yin.shape
    tr = _row_tile(s, d)
    has_resid = resid is not None

    def body(*refs):
        if has_resid:
            do_ref, y_ref, g_ref, r_ref, di_ref, dg_ref = refs
        else:
            do_ref, y_ref, g_ref, di_ref, dg_ref = refs
        yv = y_ref[...]
        r = lax.rsqrt(jnp.mean(yv * yv, axis=-1, keepdims=True) + NORM_EPS)
        xhat = yv * r
        dn = scale * do_ref[...]
        part = _sum_to_sublanes(dn * xhat)

        @pl.when(pl.program_id(0) == 0)
        def _():
            dg_ref[...] = part

        @pl.when(pl.program_id(0) > 0)
        def _():
            dg_ref[...] += part

        dxn = dn * g_ref[...]
        din = r * (dxn - xhat * jnp.mean(dxn * xhat, axis=-1, keepdims=True))
        if has_resid:
            din = din + r_ref[...]
        di_ref[...] = din.astype(di_ref.dtype)

    row = pl.BlockSpec((tr, d), lambda i: (i, 0))
    vec = pl.BlockSpec((1, d), lambda i: (0, 0))
    ins = [row, row, vec] + ([row] if has_resid else [])
    args = (dout, yin, gain) + ((resid,) if has_resid else ())
    return pl.pallas_call(
        body, name=name, grid=(s // tr,), in_specs=ins,
        out_specs=[row, pl.BlockSpec((SUBLANES, d), lambda i: (0, 0))],
        out_shape=[jax.ShapeDtypeStruct((s, d), out_dtype), jax.ShapeDtypeStruct((SUBLANES, d), F32)],
        compiler_params=_params("arbitrary"),
    )(*args)


def _loss_head(name, y, target):
    s, d = y.shape
    tr = _row_tile(s, d)

    def body(y_ref, t_ref, dy_ref, l_ref):
        e = y_ref[...] - t_ref[...]
        dy_ref[...] = e * (1.0 / d)
        part = _sum_to_sublanes(e * e) * (0.5 / d)

        @pl.when(pl.program_id(0) == 0)
        def _():
            l_ref[...] = part

        @pl.when(pl.program_id(0) > 0)
        def _():
            l_ref[...] += part

    row = pl.BlockSpec((tr, d), lambda i: (i, 0))
    return pl.pallas_call(
        body, name=name, grid=(s // tr,), in_specs=[row, row],
        out_specs=[row, pl.BlockSpec((SUBLANES, d), lambda i: (0, 0))],
        out_shape=[jax.ShapeDtypeStruct((s, d), F32), jax.ShapeDtypeStruct((SUBLANES, d), F32)],
        compiler_params=_params("arbitrary"),
    )(y, target)


def _ffn_up(name, h, gu_w):
    s, d = h.shape
    nb, _, fs = gu_w.shape
    hb = nb // 2
    w = gu_w.reshape(2, hb, d, fs)
    tm = _tile(s, 512)
    tn = _tile(fs, 1408)
    nj = fs // tn

    def body(h_ref, w_ref, gu_ref, a_ref):
        hv = h_ref[...]
        g = _dot(hv, w_ref[0], "nn")
        u = _dot(hv, w_ref[1], "nn")
        gu_ref[0] = g.astype(gu_ref.dtype)
        gu_ref[1] = u.astype(gu_ref.dtype)
        a_ref[...] = (g * jax.nn.sigmoid(g) * u).astype(a_ref.dtype)

    return pl.pallas_call(
        body, name=name, grid=(hb, nj, s // tm),
        in_specs=[pl.BlockSpec((tm, d), lambda jb, jo, i: (i, 0)),
                  pl.BlockSpec((2, None, d, tn), lambda jb, jo, i: (0, jb, 0, jo))],
        out_specs=[pl.BlockSpec((2, None, tm, tn), lambda jb, jo, i: (0, jb, i, jo)),
                   pl.BlockSpec((tm, tn), lambda jb, jo, i: (i, jb * nj + jo))],
        out_shape=[jax.ShapeDtypeStruct((2, hb, s, fs), BF16), jax.ShapeDtypeStruct((s, hb * fs), BF16)],
        compiler_params=_params("parallel", "parallel", "parallel"),
    )(h, w)


def _ffn_dact(name, dy, dn_w, gu, deps=()):
    s, d = dy.shape
    _, hb, _, fs = gu.shape
    tm = _tile(s, 512)
    tn = _tile(fs, 1408)
    nj = fs // tn

    def body(dy_ref, w_ref, gu_ref, *rest):
        o_ref = rest[-1]
        da = _dot(dy_ref[...], w_ref[...], "nt")
        g = gu_ref[0].astype(F32)
        u = gu_ref[1].astype(F32)
        sg = jax.nn.sigmoid(g)
        o_ref[0] = (da * u * (sg * (1.0 + g * (1.0 - sg)))).astype(o_ref.dtype)
        o_ref[1] = (da * (g * sg)).astype(o_ref.dtype)

    blk = pl.BlockSpec((2, None, tm, tn), lambda jb, jo, i: (0, jb, i, jo))
    return pl.pallas_call(
        body, name=name, grid=(hb, nj, s // tm),
        in_specs=[pl.BlockSpec((tm, d), lambda jb, jo, i: (i, 0)),
                  pl.BlockSpec((tn, d), lambda jb, jo, i: (jb * nj + jo, 0)),
                  blk] + [ANY_SPEC] * len(deps),
        out_specs=blk, out_shape=jax.ShapeDtypeStruct(gu.shape, BF16),
        compiler_params=_params("parallel", "parallel", "parallel"),
    )(dy, dn_w, gu, *deps)


_MASKED = -1e30


def _attn_bias(s, tq):
    nd = s // tq
    dist = (jnp.arange(nd)[:, None, None] * tq + jnp.arange(tq)[None, :, None]) - jnp.arange(tq)[None, None, :]
    mult = jnp.zeros(dist.shape, F32)
    for window, dilation in DILATED_BRANCHES:
        mult = mult + ((dist >= 0) & (dist <= window) & (dist % dilation == 0)).astype(F32)
    return jnp.where(mult > 0.0, jnp.log(jnp.maximum(mult, 1.0)), _MASKED)


def _biased(sc, bias, scale):
    tq, tk = bias.shape
    return (sc.reshape(-1, tq, tk) * scale + bias[None]).reshape(sc.shape)


def _attn_specs(s, qd, kvd, tq):
    rw = Q_PER_KV * HEAD_DIM
    qspec = pl.BlockSpec((tq, rw), lambda g, i: (i, g))
    kspec = pl.BlockSpec((s, HEAD_DIM), lambda g, i: (0, qd // HEAD_DIM + g))
    vspec = pl.BlockSpec((s, HEAD_DIM), lambda g, i: (0, (qd + kvd) // HEAD_DIM + g))
    return rw, qspec, kspec, vspec


def _attn_fwd(name, z, qd, kvd):
    s = z.shape[0]
    tq = _tile(s, 256)
    nkv = kvd // HEAD_DIM
    rw, qspec, kspec, vspec = _attn_specs(s, qd, kvd, tq)
    scale = HEAD_DIM ** -0.5

    def body(q_ref, k_ref, v_ref, b_ref, o_ref, l_ref):
        i = pl.program_id(1)
        heads = [slice(h * HEAD_DIM, (h + 1) * HEAD_DIM) for h in range(Q_PER_KV)]
        q_all = jnp.concatenate([q_ref[:, cols] for cols in heads], axis=0)

        def chunk(j, carry):
            mx, den, acc = carry
            k0 = pl.multiple_of(j * tq, tq)
            kc, vc = k_ref[pl.ds(k0, tq), :], v_ref[pl.ds(k0, tq), :]
            sc = _biased(_dot(q_all, kc, "nt"), b_ref[i - j], scale)
            mx_new = jnp.maximum(mx, jnp.max(sc, axis=-1, keepdims=True))
            alpha = jnp.exp(mx - mx_new)
            p = jnp.exp(sc - mx_new)
            return (mx_new, alpha * den + jnp.sum(p, axis=-1, keepdims=True),
                    alpha * acc + _dot(p.astype(BF16), vc, "nn"))

        rows = Q_PER_KV * tq
        init = (jnp.full((rows, 1), _MASKED, F32), jnp.zeros((rows, 1), F32), jnp.zeros((rows, HEAD_DIM), F32))
        mx, den, acc = lax.fori_loop(0, i + 1, chunk, init)
        out = acc / den
        lse = mx + jnp.log(den)
        for h, cols in enumerate(heads):
            o_ref[:, cols] = out[h * tq:(h + 1) * tq]
            l_ref[:, cols] = jnp.broadcast_to(lse[h * tq:(h + 1) * tq], (tq, HEAD_DIM))

    bias = _attn_bias(s, tq)
    return pl.pallas_call(
        body, name=name, grid=(nkv, s // tq),
        in_specs=[qspec, kspec, vspec, pl.BlockSpec(bias.shape, lambda g, i: (0, 0, 0))], out_specs=[qspec, qspec],
        out_shape=[jax.ShapeDtypeStruct((s, qd), F32), jax.ShapeDtypeStruct((s, qd), F32)],
        compiler_params=_params("parallel", "parallel"),
    )(z, z, z, bias)


def _attn_bwd(name, z, o, lse, do, qd, kvd):
    s = z.shape[0]
    tq = _tile(s, 256)
    nkv = kvd // HEAD_DIM
    nq = s // tq
    rw, qspec, kspec, vspec = _attn_specs(s, qd, kvd, tq)
    scale = HEAD_DIM ** -0.5

    def body(q_ref, k_ref, v_ref, o_ref, l_ref, do_ref, b_ref, dq_ref, dk_ref, dv_ref, dk_acc, dv_acc):
        i = pl.program_id(1)
        heads = [slice(h * HEAD_DIM, (h + 1) * HEAD_DIM) for h in range(Q_PER_KV)]

        @pl.when(i == 0)
        def _():
            dk_acc[...] = jnp.zeros_like(dk_acc)
            dv_acc[...] = jnp.zeros_like(dv_acc)

        q_all = jnp.concatenate([q_ref[:, cols] for cols in heads], axis=0)
        do_all = jnp.concatenate([do_ref[:, cols].astype(BF16) for cols in heads], axis=0)
        lse_all = jnp.concatenate([l_ref[:, cols][:, :1] for cols in heads], axis=0)
        delta_all = jnp.concatenate(
            [jnp.sum(do_ref[:, cols] * o_ref[:, cols], axis=-1, keepdims=True) for cols in heads], axis=0)

        def chunk(j, dq):
            k0 = pl.multiple_of(j * tq, tq)
            kc, vc = k_ref[pl.ds(k0, tq), :], v_ref[pl.ds(k0, tq), :]
            p = jnp.exp(_biased(_dot(q_all, kc, "nt"), b_ref[i - j], scale) - lse_all)
            ds = (p * (_dot(do_all, vc, "nt") - delta_all) * scale).astype(BF16)
            dk_acc[pl.ds(k0, tq), :] += _dot(ds, q_all, "tn")
            dv_acc[pl.ds(k0, tq), :] += _dot(p.astype(BF16), do_all, "tn")
            return dq + _dot(ds, kc, "nn")

        dq = lax.fori_loop(0, i + 1, chunk, jnp.zeros((Q_PER_KV * tq, HEAD_DIM), F32))
        for h, cols in enumerate(heads):
            dq_ref[:, cols] = dq[h * tq:(h + 1) * tq].astype(dq_ref.dtype)

        @pl.when(i == nq - 1)
        def _():
            dk_ref[...] = dk_acc[...].astype(dk_ref.dtype)
            dv_ref[...] = dv_acc[...].astype(dv_ref.dtype)

    kvout = pl.BlockSpec((s, HEAD_DIM), lambda g, i: (0, g))
    bias = _attn_bias(s, tq)
    return pl.pallas_call(
        body, name=name, grid=(nkv, nq),
        in_specs=[qspec, kspec, vspec, qspec, qspec, qspec, pl.BlockSpec(bias.shape, lambda g, i: (0, 0, 0))],
        out_specs=[qspec, kvout, kvout],
        out_shape=[jax.ShapeDtypeStruct((s, qd), BF16), jax.ShapeDtypeStruct((s, kvd), BF16),
                   jax.ShapeDtypeStruct((s, kvd), BF16)],
        scratch_shapes=[pltpu.VMEM((s, HEAD_DIM), F32), pltpu.VMEM((s, HEAD_DIM), F32)],
        compiler_params=_params("parallel", "arbitrary"),
    )(z, z, z, o, lse, do, bias)


def _shift_down(v, n):
    rolled = pltpu.roll(v, n, 0)
    t = lax.broadcasted_iota(jnp.int32, v.shape, 0)
    return jnp.where(t >= n, rolled, 0.0)


def _shift_up(v, n):
    rows = v.shape[0]
    rolled = pltpu.roll(v, rows - n, 0)
    t = lax.broadcasted_iota(jnp.int32, v.shape, 0)
    return jnp.where(t < rows - n, rolled, 0.0)


def _conv_specs(s, base, cd, tc):
    zs = [pl.BlockSpec((s, tc), functools.partial(lambda j, off: (0, off + j), off=(base + n * cd) // tc))
          for n in range(3)]
    wspec = pl.BlockSpec((SUBLANES, tc), lambda j: (0, j))
    cspec = pl.BlockSpec((s, tc), lambda j: (0, j))
    return zs, wspec, cspec


def _conv_fwd(name, z, conv_w, base, cd):
    s = z.shape[0]
    tc = _tile(cd, 256)
    zs, wspec, cspec = _conv_specs(s, base, cd, tc)

    def body(h_ref, b_ref, c_ref, w_ref, o_ref):
        u = c_ref[...].astype(F32) * h_ref[...].astype(F32)
        y = w_ref[0:1, :] * _shift_down(u, 2) + w_ref[1:2, :] * _shift_down(u, 1) + w_ref[2:3, :] * u
        o_ref[...] = b_ref[...].astype(F32) * y

    return pl.pallas_call(
        body, name=name, grid=(cd // tc,), in_specs=zs + [wspec], out_specs=cspec,
        out_shape=jax.ShapeDtypeStruct((s, cd), F32), compiler_params=_params("parallel"),
    )(z, z, z, conv_w)


def _conv_bwd(name, z, conv_w, dc, base, cd):
    s = z.shape[0]
    tc = _tile(cd, 256)
    zs, wspec, cspec = _conv_specs(s, base, cd, tc)

    def body(h_ref, b_ref, c_ref, w_ref, dc_ref, dh_ref, db_ref, dcg_ref, dw_ref):
        hv, bv, cv = h_ref[...].astype(F32), b_ref[...].astype(F32), c_ref[...].astype(F32)
        u = cv * hv
        u1, u2 = _shift_down(u, 1), _shift_down(u, 2)
        w0, w1, w2 = w_ref[0:1, :], w_ref[1:2, :], w_ref[2:3, :]
        y = w0 * u2 + w1 * u1 + w2 * u
        dcv = dc_ref[...]
        db_ref[...] = (dcv * y).astype(db_ref.dtype)
        dy = dcv * bv
        du = w2 * dy + w1 * _shift_up(dy, 1) + w0 * _shift_up(dy, 2)
        dh_ref[...] = (du * cv).astype(dh_ref.dtype)
        dcg_ref[...] = (du * hv).astype(dcg_ref.dtype)
        g0 = jnp.sum(dy * u2, axis=0, keepdims=True)
        g1 = jnp.sum(dy * u1, axis=0, keepdims=True)
        g2 = jnp.sum(dy * u, axis=0, keepdims=True)
        r = lax.broadcasted_iota(jnp.int32, (SUBLANES, tc), 0)
        dw_ref[...] = jnp.where(r == 0, g0, jnp.where(r == 1, g1, jnp.where(r == 2, g2, 0.0)))

    return pl.pallas_call(
        body, name=name, grid=(cd // tc,), in_specs=zs + [wspec, cspec],
        out_specs=[cspec, cspec, cspec, wspec],
        out_shape=[jax.ShapeDtypeStruct((s, cd), BF16)] * 3 + [jax.ShapeDtypeStruct((SUBLANES, cd), F32)],
        compiler_params=_params("parallel"),
    )(z, z, z, conv_w, dc)


def _cat_norm_fwd(name, a, c, ga, gc):
    s, qd = a.shape
    cd = c.shape[1]
    tr = _row_tile(s, qd + cd)

    def body(a_ref, c_ref, ga_ref, gc_ref, o_ref):
        av, cv = a_ref[...], c_ref[...]
        ra = lax.rsqrt(jnp.mean(av * av, axis=-1, keepdims=True) + NORM_EPS)
        rc = lax.rsqrt(jnp.mean(cv * cv, axis=-1, keepdims=True) + NORM_EPS)
        o_ref[:, :qd] = (av * ra * ga_ref[...]).astype(o_ref.dtype)
        o_ref[:, qd:] = (cv * rc * gc_ref[...]).astype(o_ref.dtype)

    return pl.pallas_call(
        body, name=name, grid=(s // tr,),
        in_specs=[pl.BlockSpec((tr, qd), lambda i: (i, 0)), pl.BlockSpec((tr, cd), lambda i: (i, 0)),
                  pl.BlockSpec((1, qd), lambda i: (0, 0)), pl.BlockSpec((1, cd), lambda i: (0, 0))],
        out_specs=pl.BlockSpec((tr, qd + cd), lambda i: (i, 0)),
        out_shape=jax.ShapeDtypeStruct((s, qd + cd), BF16), compiler_params=_params("parallel"),
    )(a, c, ga, gc)


def _cat_norm_bwd(name, dcat, a, c, ga, gc):
    s, qd = a.shape
    cd = c.shape[1]
    tr = _row_tile(s, qd + cd)

    def one(dn, yv, gv):
        r = lax.rsqrt(jnp.mean(yv * yv, axis=-1, keepdims=True) + NORM_EPS)
        xhat = yv * r
        dxn = dn * gv
        return r * (dxn - xhat * jnp.mean(dxn * xhat, axis=-1, keepdims=True)), _sum_to_sublanes(dn * xhat)

    def body(d_ref, a_ref, c_ref, ga_ref, gc_ref, da_ref, dc_ref, dga_ref, dgc_ref):
        da, pa = one(d_ref[:, :qd], a_ref[...], ga_ref[...])
        dc, pc = one(d_ref[:, qd:], c_ref[...], gc_ref[...])
        da_ref[...] = da
        dc_ref[...] = dc

        @pl.when(pl.program_id(0) == 0)
        def _():
            dga_ref[...] = pa
            dgc_ref[...] = pc

        @pl.when(pl.program_id(0) > 0)
        def _():
            dga_ref[...] += pa
            dgc_ref[...] += pc

    ra = pl.BlockSpec((tr, qd), lambda i: (i, 0))
    rc = pl.BlockSpec((tr, cd), lambda i: (i, 0))
    return pl.pallas_call(
        body, name=name, grid=(s // tr,),
        in_specs=[pl.BlockSpec((tr, qd + cd), lambda i: (i, 0)), ra, rc,
                  pl.BlockSpec((1, qd), lambda i: (0, 0)), pl.BlockSpec((1, cd), lambda i: (0, 0))],
        out_specs=[ra, rc, pl.BlockSpec((SUBLANES, qd), lambda i: (0, 0)),
                   pl.BlockSpec((SUBLANES, cd), lambda i: (0, 0))],
        out_shape=[jax.ShapeDtypeStruct((s, qd), F32), jax.ShapeDtypeStruct((s, cd), F32),
                   jax.ShapeDtypeStruct((SUBLANES, qd), F32), jax.ShapeDtypeStruct((SUBLANES, cd), F32)],
        compiler_params=_params("arbitrary"),
    )(dcat, a, c, ga, gc)


def _adamw(name, w, g, m, v, emit_grad=False):
    shape = w.shape
    cols = shape[-1]
    rows = w.size // cols
    tr = _row_tile(rows, cols, budget=3 << 19)
    bc1 = 1.0 - ADAM_B1 ** ADAM_STEP
    bc2 = 1.0 - ADAM_B2 ** ADAM_STEP
    n_out = 4 if emit_grad else 3

    def body(w_ref, g_ref, m_ref, v_ref, d_ref, nm_ref, nv_ref, *g_out):
        gv = g_ref[...]
        mv = ADAM_B1 * m_ref[...] + (1.0 - ADAM_B1) * gv
        vv = ADAM_B2 * v_ref[...] + (1.0 - ADAM_B2) * (gv * gv)
        nm_ref[...] = mv
        nv_ref[...] = vv
        d_ref[...] = -ADAM_LR * ((mv / bc1) / (jnp.sqrt(vv / bc2) + ADAM_EPS) + ADAM_WD * w_ref[...])
        for ref in g_out:
            ref[...] = gv

    row = pl.BlockSpec((tr, cols), lambda i: (i, 0))
    outs = pl.pallas_call(
        body, name=name, grid=(rows // tr,), in_specs=[row] * 4, out_specs=[row] * n_out,
        out_shape=[jax.ShapeDtypeStruct((rows, cols), F32)] * n_out, compiler_params=_params("parallel"),
    )(*(t.reshape(rows, cols) for t in (w, g, m, v)))
    return tuple(t.reshape(shape) for t in outs)


HBM_SPEC = pl.BlockSpec(memory_space=pltpu.HBM)


def _mesh_place():
    x, y, c = lax.axis_index("x"), lax.axis_index("y"), lax.axis_index("c")
    other_chips = [(1 - x, y), (x, 1 - y), (1 - x, 1 - y)]
    return x, y, c, other_chips


def _cast_into_slot(name, w, layer, chip, deps=()):
    _, r, cols = w.shape
    tr = _row_tile(r, cols, budget=BIG_BLOCK)

    def body(chip_ref, w_ref, *rest):
        o_ref = rest[-1]
        o_ref[...] = w_ref[...].astype(o_ref.dtype)

    return pl.pallas_call(
        body, name=name,
        grid_spec=pltpu.PrefetchScalarGridSpec(
            num_scalar_prefetch=1, grid=(r // tr,),
            in_specs=[pl.BlockSpec((None, tr, cols), lambda i, chip_ref: (layer, i, 0))] + [ANY_SPEC] * len(deps),
            out_specs=pl.BlockSpec((None, tr, cols), lambda i, chip_ref: (chip_ref[0], i, 0))),
        out_shape=jax.ShapeDtypeStruct((N_CHIPS, r, cols), BF16), compiler_params=_params("parallel"),
    )(chip, w, *deps)


SEM_SPEC = pl.BlockSpec(memory_space=pltpu.SEMAPHORE)
SPLIT_COPY = pltpu.CompilerParams(has_side_effects=pltpu.SideEffectType.DATAFLOW_SIDE_EFFECTING)
N_OTHER = N_CHIPS - 1
TOKEN_SPEC = pl.BlockSpec(memory_space=pltpu.VMEM)
TOKEN_SHAPE = jax.ShapeDtypeStruct((SUBLANES, LANES), F32)


def _in_hbm(arr):
    return pltpu.with_memory_space_constraint(arr, pltpu.HBM)


def _half_rows(ref, chip_idx, core):
    r2 = ref.shape[1] // 2
    return ref.at[chip_idx, pl.ds(core * r2, r2), :]


def _gather_start(name, fulls, after):
    na = len(fulls)

    def body(*refs):
        f_refs = refs[na + 1:2 * na + 1]
        send_sems, recv_sems = refs[2 * na + 1:3 * na + 1], refs[3 * na + 1:4 * na + 1]
        token = refs[4 * na + 1]
        x, y, c, chips = _mesh_place()
        for a in range(na):
            mine = _half_rows(f_refs[a], 2 * x + y, c)
            for j, (cx, cy) in enumerate(chips):
                pltpu.make_async_remote_copy(
                    src_ref=mine, dst_ref=mine, send_sem=send_sems[a].at[j], recv_sem=recv_sems[a].at[j],
                    device_id=(cx, cy, c), device_id_type=MESH).start()
        token[...] = jnp.zeros_like(token)

    outs = pl.pallas_call(
        body, name=name, in_specs=[HBM_SPEC] * na + [ANY_SPEC],
        out_specs=[HBM_SPEC] * na + [SEM_SPEC] * (2 * na) + [TOKEN_SPEC],
        out_shape=[pltpu.HBM(f.shape, f.dtype) for f in fulls] + [pltpu.SemaphoreType.DMA((N_OTHER,))] * (2 * na)
        + [TOKEN_SHAPE],
        input_output_aliases={a: a for a in range(na)}, compiler_params=SPLIT_COPY,
    )(*[_in_hbm(f) for f in fulls], after)
    return list(outs[:na]), list(outs[na:2 * na]), list(outs[2 * na:3 * na]), outs[3 * na]


def _gather_pass_on(name, full, recv_sems, after):
    def body(f_in, recv_sems, after_ref, f_ref, d2d_send, d2d_recv):
        x, y, c, chips = _mesh_place()
        for j, (cx, cy) in enumerate(chips):
            blk = _half_rows(f_ref, 2 * cx + cy, c)
            pltpu.make_async_remote_copy(
                src_ref=blk, dst_ref=blk, send_sem=d2d_send.at[j], recv_sem=recv_sems.at[j],
                device_id=(cx, cy, c), device_id_type=MESH).wait_recv()
            pltpu.make_async_remote_copy(
                src_ref=blk, dst_ref=blk, send_sem=d2d_send.at[j], recv_sem=d2d_recv.at[j],
                device_id=(x, y, 1 - c), device_id_type=MESH).start()

    return pl.pallas_call(
        body, name=name, in_specs=[HBM_SPEC, SEM_SPEC, ANY_SPEC], out_specs=[HBM_SPEC, SEM_SPEC, SEM_SPEC],
        out_shape=[pltpu.HBM(full.shape, full.dtype)] + [pltpu.SemaphoreType.DMA((N_OTHER,))] * 2,
        input_output_aliases={0: 0}, compiler_params=SPLIT_COPY,
    )(full, recv_sems, after)


def _gather_arrive(name, full, ici_send, d2d_send, d2d_recv, after):
    def body(f_in, ici_send, d2d_send, d2d_recv, after_ref, f_ref):
        x, y, c, chips = _mesh_place()
        for j, (cx, cy) in enumerate(chips):
            mine = _half_rows(f_ref, 2 * x + y, c)
            passed = _half_rows(f_ref, 2 * cx + cy, c)
            theirs = _half_rows(f_ref, 2 * cx + cy, 1 - c)
            pltpu.make_async_remote_copy(
                src_ref=mine, dst_ref=mine, send_sem=ici_send.at[j], recv_sem=d2d_recv.at[j],
                device_id=(cx, cy, c), device_id_type=MESH).wait_send()
            pltpu.make_async_remote_copy(
                src_ref=passed, dst_ref=passed, send_sem=d2d_send.at[j], recv_sem=d2d_recv.at[j],
                device_id=(x, y, 1 - c), device_id_type=MESH).wait_send()
            pltpu.make_async_remote_copy(
                src_ref=theirs, dst_ref=theirs, send_sem=d2d_send.at[j], recv_sem=d2d_recv.at[j],
                device_id=(x, y, 1 - c), device_id_type=MESH).wait_recv()

    return pl.pallas_call(
        body, name=name, in_specs=[HBM_SPEC, SEM_SPEC, SEM_SPEC, SEM_SPEC, ANY_SPEC], out_specs=HBM_SPEC,
        out_shape=pltpu.HBM(full.shape, full.dtype), input_output_aliases={0: 0}, compiler_params=SPLIT_COPY,
    )(full, ici_send, d2d_send, d2d_recv, after)


def _gather_taps(conv_w):
    def body(cw_ref, cwf_ref, send_sems, recv_sems, local_sem):
        x, y, c, chips = _mesh_place()
        k_me = 2 * x + y
        local = pltpu.make_async_copy(cw_ref, cwf_ref.at[k_me], local_sem)
        local.start()
        copies = [pltpu.make_async_remote_copy(
            src_ref=cw_ref, dst_ref=cwf_ref.at[k_me], send_sem=send_sems.at[j], recv_sem=recv_sems.at[j],
            device_id=(cx, cy, c), device_id_type=MESH) for j, (cx, cy) in enumerate(chips)]
        for cp in copies:
            cp.start()
        for j, (cx, cy) in enumerate(chips):
            pltpu.make_async_remote_copy(
                src_ref=cw_ref, dst_ref=cwf_ref.at[2 * cx + cy], send_sem=send_sems.at[j], recv_sem=recv_sems.at[j],
                device_id=(cx, cy, c), device_id_type=MESH).wait_recv()
        for cp in copies:
            cp.wait_send()
        local.wait()

    return pl.pallas_call(
        body, name="gather_taps", in_specs=[HBM_SPEC], out_specs=HBM_SPEC,
        out_shape=jax.ShapeDtypeStruct((N_CHIPS,) + conv_w.shape, conv_w.dtype),
        scratch_shapes=[pltpu.SemaphoreType.DMA((N_OTHER,))] * 2 + [pltpu.SemaphoreType.DMA],
    )(conv_w)


def _sibling_half(g_ref, c):
    r2 = g_ref.shape[1] // 2
    return g_ref.at[:, pl.ds((1 - c) * r2, r2), :]


def _swap_copy(g_ref, land_ref, send_sems, recv_sems, a):
    x, y, c, _ = _mesh_place()
    return pltpu.make_async_remote_copy(
        src_ref=_sibling_half(g_ref, c), dst_ref=land_ref, send_sem=send_sems.at[a], recv_sem=recv_sems.at[a],
        device_id=(x, y, 1 - c), device_id_type=MESH)


def _swap_start(name, gs):
    n = len(gs)

    def body(*refs):
        g_refs, land_refs = refs[n:2 * n], refs[2 * n:3 * n]
        send_sems, recv_sems, token = refs[3 * n:]
        for a in range(n):
            _swap_copy(g_refs[a], land_refs[a], send_sems, recv_sems, a).start()
        token[...] = jnp.zeros_like(token)

    outs = pl.pallas_call(
        body, name=name, in_specs=[HBM_SPEC] * n,
        out_specs=[HBM_SPEC] * (2 * n) + [SEM_SPEC, SEM_SPEC, TOKEN_SPEC],
        out_shape=[pltpu.HBM(g.shape, g.dtype) for g in gs]
        + [pltpu.HBM((g.shape[0], g.shape[1] // 2, g.shape[2]), g.dtype) for g in gs]
        + [pltpu.SemaphoreType.DMA((n,)), pltpu.SemaphoreType.DMA((n,)), TOKEN_SHAPE],
        input_output_aliases={a: a for a in range(n)}, compiler_params=SPLIT_COPY,
    )(*[_in_hbm(g) for g in gs])
    return list(outs[:n]), list(outs[n:2 * n]), outs[2 * n], outs[2 * n + 1], outs[2 * n + 2]


def _swap_wait(name, gs, lands, send_sems, recv_sems, after):
    n = len(gs)

    def body(*refs):
        send_sems, recv_sems = refs[2 * n], refs[2 * n + 1]
        g_refs, land_refs = refs[2 * n + 3:3 * n + 3], refs[3 * n + 3:]
        for a in range(n):
            copy = _swap_copy(g_refs[a], land_refs[a], send_sems, recv_sems, a)
            copy.wait_send()
            copy.wait_recv()

    outs = pl.pallas_call(
        body, name=name, in_specs=[HBM_SPEC] * (2 * n) + [SEM_SPEC, SEM_SPEC, ANY_SPEC],
        out_specs=[HBM_SPEC] * (2 * n),
        out_shape=[pltpu.HBM(t.shape, t.dtype) for t in list(gs) + list(lands)],
        input_output_aliases={a: a for a in range(2 * n)}, compiler_params=SPLIT_COPY,
    )(*gs, *lands, send_sems, recv_sems, after)
    return list(outs[:n]), list(outs[n:])


def _add_core_halves(name, g, sib, core):
    nb, r, cols = g.shape
    r2 = r // 2
    tr = _row_tile(r2, cols, itemsize=2, budget=BIG_BLOCK)
    nrt = r2 // tr

    def body(core_ref, g_ref, s_ref, o_ref):
        o_ref[...] = (g_ref[...].astype(F32) + s_ref[...].astype(F32)).astype(o_ref.dtype)

    return pl.pallas_call(
        body, name=name,
        grid_spec=pltpu.PrefetchScalarGridSpec(
            num_scalar_prefetch=1, grid=(nb, nrt),
            in_specs=[pl.BlockSpec((None, tr, cols), lambda k, i, core_ref: (k, core_ref[0] * nrt + i, 0)),
                      pl.BlockSpec((None, tr, cols), lambda k, i, core_ref: (k, i, 0))],
            out_specs=pl.BlockSpec((None, tr, cols), lambda k, i, core_ref: (k, i, 0))),
        out_shape=jax.ShapeDtypeStruct((nb, r2, cols), BF16), compiler_params=_params("parallel", "parallel"),
    )(core, g, sib)


def _scatter_copies(h_refs, land_refs, send_sems, recv_sems):
    x, y, c, chips = _mesh_place()
    return [pltpu.make_async_remote_copy(
        src_ref=h_ref.at[2 * cx + cy], dst_ref=land_ref.at[j],
        send_sem=send_sems.at[a * N_OTHER + j], recv_sem=recv_sems.at[a * N_OTHER + j],
        device_id=(cx, cy, c), device_id_type=MESH)
        for a, (h_ref, land_ref) in enumerate(zip(h_refs, land_refs)) for j, (cx, cy) in enumerate(chips)]


def _scatter_start(name, hs):
    n = len(hs)

    def body(*refs):
        h_refs, land_refs = refs[n:2 * n], refs[2 * n:3 * n]
        send_sems, recv_sems, token = refs[3 * n:]
        for copy in _scatter_copies(h_refs, land_refs, send_sems, recv_sems):
            copy.start()
        token[...] = jnp.zeros_like(token)

    outs = pl.pallas_call(
        body, name=name, in_specs=[HBM_SPEC] * n,
        out_specs=[HBM_SPEC] * (2 * n) + [SEM_SPEC, SEM_SPEC, TOKEN_SPEC],
        out_shape=[pltpu.HBM(h.shape, h.dtype) for h in hs]
        + [pltpu.HBM((N_OTHER,) + h.shape[1:], h.dtype) for h in hs]
        + [pltpu.SemaphoreType.DMA((n * N_OTHER,)), pltpu.SemaphoreType.DMA((n * N_OTHER,)), TOKEN_SHAPE],
        input_output_aliases={a: a for a in range(n)}, compiler_params=SPLIT_COPY,
    )(*[_in_hbm(h) for h in hs])
    return list(outs[:n]), list(outs[n:2 * n]), outs[2 * n], outs[2 * n + 1], outs[2 * n + 2]


def _scatter_wait(name, hs, lands, send_sems, recv_sems, after):
    afters = tuple(after) if isinstance(after, (tuple, list)) else (after,)
    n = len(hs)

    def body(*refs):
        send_sems, recv_sems = refs[2 * n], refs[2 * n + 1]
        h_refs, land_refs = refs[-2 * n:-n], refs[-n:]
        for copy in _scatter_copies(h_refs, land_refs, send_sems, recv_sems):
            copy.wait_send()
            copy.wait_recv()

    outs = pl.pallas_call(
        body, name=name, in_specs=[HBM_SPEC] * (2 * n) + [SEM_SPEC, SEM_SPEC] + [ANY_SPEC] * len(afters),
        out_specs=[HBM_SPEC] * (2 * n),
        out_shape=[pltpu.HBM(t.shape, t.dtype) for t in list(hs) + list(lands)],
        input_output_aliases={a: a for a in range(2 * n)}, compiler_params=SPLIT_COPY,
    )(*hs, *lands, send_sems, recv_sems, *afters)
    return list(outs[:n]), list(outs[n:])


def _sum_chips(name, hs, rcv, core, chip, layer, n_layers, prev):
    _, r2, cols = hs.shape
    tr = _row_tile(r2, cols, budget=BIG_BLOCK)
    nrt = r2 // tr

    def body(core_ref, chip_ref, h_ref, r_ref, *rest):
        o_ref = rest[-1]
        acc = h_ref[...].astype(F32)
        for j in range(N_CHIPS - 1):
            acc = acc + r_ref[j].astype(F32)
        o_ref[...] = acc

    in_specs = [pl.BlockSpec((None, tr, cols), lambda i, core_ref, chip_ref: (chip_ref[0], i, 0)),
                pl.BlockSpec((N_CHIPS - 1, tr, cols), lambda i, core_ref, chip_ref: (0, i, 0))]
    args = [core, chip, hs, rcv]
    aliases = {}
    if prev is not None:
        in_specs.append(pl.BlockSpec(memory_space=pl.ANY))
        args.append(prev)
        aliases = {4: 0}
    return pl.pallas_call(
        body, name=name,
        grid_spec=pltpu.PrefetchScalarGridSpec(
            num_scalar_prefetch=2, grid=(nrt,), in_specs=in_specs,
            out_specs=pl.BlockSpec((None, tr, cols), lambda i, core_ref, chip_ref: (layer, core_ref[0] * nrt + i, 0))),
        out_shape=jax.ShapeDtypeStruct((n_layers, 2 * r2, cols), F32), input_output_aliases=aliases,
        compiler_params=_params("parallel"),
    )(*args)


def _join_copy(t_ref, send_sems, recv_sems, a):
    x, y, c, _ = _mesh_place()
    r2 = t_ref.shape[1] // 2
    mine = t_ref.at[:, pl.ds(c * r2, r2), :]
    return pltpu.make_async_remote_copy(
        src_ref=mine, dst_ref=mine, send_sem=send_sems.at[a], recv_sem=recv_sems.at[a],
        device_id=(x, y, 1 - c), device_id_type=MESH)


def _join_start(name, ts, deps=()):
    n, nd = len(ts), len(deps)

    def body(*refs):
        t_refs = refs[n + nd:2 * n + nd]
        send_sems, recv_sems = refs[2 * n + nd:]
        for a in range(n):
            _join_copy(t_refs[a], send_sems, recv_sems, a).start()

    outs = pl.pallas_call(
        body, name=name, in_specs=[HBM_SPEC] * n + [ANY_SPEC] * nd, out_specs=[HBM_SPEC] * n + [SEM_SPEC, SEM_SPEC],
        out_shape=[pltpu.HBM(t.shape, t.dtype) for t in ts] + [pltpu.SemaphoreType.DMA((n,))] * 2,
        input_output_aliases={a: a for a in range(n)}, compiler_params=SPLIT_COPY,
    )(*[_in_hbm(t) for t in ts], *deps)
    return list(outs[:n]), outs[n], outs[n + 1]


def _join_wait(name, t, a, send_sems, recv_sems, after):
    def body(t_in, send_sems, recv_sems, after_ref, t_ref):
        copy = _join_copy(t_ref, send_sems, recv_sems, a)
        copy.wait_send()
        copy.wait_recv()

    return pl.pallas_call(
        body, name=name, in_specs=[HBM_SPEC, SEM_SPEC, SEM_SPEC, ANY_SPEC], out_specs=HBM_SPEC,
        out_shape=pltpu.HBM(t.shape, t.dtype), input_output_aliases={0: 0}, compiler_params=SPLIT_COPY,
    )(t, send_sems, recv_sems, after)


def _allreduce_small(p):
    n, _, w = p.shape

    def body(p_ref, o_ref, buf, send_sems, recv_sems):
        x, y, c, _ = _mesh_place()
        me = 4 * x + 2 * y + c
        buf[me] = jnp.sum(p_ref[...], axis=1)
        copies = []
        for pat in range(1, N_DEV):
            fx, fy, fc = (pat >> 2) & 1, (pat >> 1) & 1, pat & 1
            copies.append(pltpu.make_async_remote_copy(
                src_ref=buf.at[me], dst_ref=buf.at[me], send_sem=send_sems.at[pat - 1], recv_sem=recv_sems.at[pat - 1],
                device_id=(x ^ fx, y ^ fy, c ^ fc), device_id_type=MESH))
        for cp in copies:
            cp.start()
        for cp in copies:
            cp.wait()
        acc = buf[0]
        for dev in range(1, N_DEV):
            acc = acc + buf[dev]
        o_ref[...] = acc

    return pl.pallas_call(
        body, name="allreduce_small", in_specs=[pl.BlockSpec(memory_space=pltpu.VMEM)],
        out_specs=pl.BlockSpec(memory_space=pltpu.VMEM), out_shape=jax.ShapeDtypeStruct((n, w), F32),
        scratch_shapes=[pltpu.VMEM((N_DEV, n, w), F32), pltpu.SemaphoreType.DMA((N_DEV - 1,)),
                        pltpu.SemaphoreType.DMA((N_DEV - 1,))],
    )(p)


class _WeightFeed:
    def __init__(self):
        self.fulls, self.ici_send, self.ici_recv, self.d2d = [], [], [], []

    def start(self, name, fulls, after):
        started, send, recv, token = _gather_start(name, fulls, after)
        self.fulls += started
        self.ici_send += send
        self.ici_recv += recv
        self.d2d += [None] * len(fulls)
        self.token = token
        return token

    def _pass_on(self, k, after):
        if k == 0:
            after = self.token
        if k < len(self.fulls) and self.d2d[k] is None:
            self.fulls[k], send, recv = _gather_pass_on(f"gather_pass_{k}", self.fulls[k], self.ici_recv[k], after)
            self.d2d[k] = (send, recv)

    def take(self, k, after):
        self._pass_on(k, after)
        self.fulls[k] = _gather_arrive(f"gather_arrive_{k}", self.fulls[k], self.ici_send[k], *self.d2d[k], after)
        return self.fulls[k]


def _ffn_forward(tag, x, g_pre, g_post, feed, k):
    s, d = x.shape
    h = _norm_fwd(f"{tag}_norm", x, g_pre)
    gu_w = feed.take(k, h)
    gu, a = _ffn_up(f"{tag}_up", h, gu_w)
    dn_w = feed.take(k + 1, a).reshape(-1, d)
    f = dn_w.shape[0]
    tm, tn = _tile(s, 1024), _tile(d, 512)
    y = _mm(f"{tag}_down", a, dn_w, mode="nn", grid=(s // tm, d // tn),
            a_spec=pl.BlockSpec((tm, f), lambda i, j: (i, 0)),
            b_spec=pl.BlockSpec((f, tn), lambda i, j: (0, j)),
            o_spec=pl.BlockSpec((tm, tn), lambda i, j: (i, j)),
            out_shape=jax.ShapeDtypeStruct((s, d), F32))
    x_new = _res_norm(f"{tag}_post", x, y, g_post, FFN_RESIDUAL_WEIGHT)
    return x_new, (x, h, gu, a, y)


class _GradReduce:
    def __init__(self, core, chip, n_layers):
        self.core, self.chip, self.n_layers = core, chip, n_layers
        self.state = {}
        self.bufs = {}

    def start(self, kinds, layer, gs):
        gs, lands, send, recv, token = _swap_start(f"swap_start_{kinds[0]}_{layer}", gs)
        self.state[kinds, layer] = (gs, lands, send, recv)
        return token

    def exchange(self, kinds, layer, after):
        tag = f"{kinds[0]}_{layer}"
        gs, sibs = _swap_wait(f"swap_wait_{tag}", *self.state[kinds, layer], after)
        hs = [_add_core_halves(f"add_cores_{k}_{layer}", g, sib, self.core) for k, g, sib in zip(kinds, gs, sibs)]
        hs, lands, send, recv, token = _scatter_start(f"scatter_start_{tag}", hs)
        self.state[kinds, layer] = (hs, lands, send, recv)
        return token

    def finish(self, kinds, layer, after):
        tag = f"{kinds[0]}_{layer}"
        hs, rcvs = _scatter_wait(f"scatter_wait_{tag}", *self.state.pop((kinds, layer)), after)
        for k, h, rcv in zip(kinds, hs, rcvs):
            self.bufs[k] = _sum_chips(f"sum_chips_{k}_{layer}", h, rcv, self.core, self.chip, layer, self.n_layers,
                                      self.bufs.get(k))
        return self.bufs[kinds[-1]]


def _ffn_backward(tag, dx_new, saved, g_pre, g_post, gu_w, dn_w, red, kinds, layer, deps):
    x, h, gu, a, y = saved
    s, d = x.shape
    nb, fs = gu_w.shape[0], gu_w.shape[2]
    f = dn_w.shape[0]
    fr = f // nb
    dy, dg_post = _norm_bwd(f"{tag}_post_bwd", dx_new, y, g_post, FFN_RESIDUAL_WEIGHT, None, BF16)
    dgu = _ffn_dact(f"{tag}_dact", dy, dn_w, gu, deps)
    dgu4 = dgu.reshape(nb, s, fs)
    tn = _tile(d, 1024)
    d_wd = _mm(f"{tag}_dwd", a, dy, mode="tn", grid=(nb, d // tn),
               a_spec=pl.BlockSpec((s, fr), lambda i, j: (0, i)),
               b_spec=pl.BlockSpec((s, tn), lambda i, j: (0, j)),
               o_spec=pl.BlockSpec((None, fr, tn), lambda i, j: (i, 0, j)),
               out_shape=jax.ShapeDtypeStruct((nb, fr, d), BF16))
    tm, tw = _tile(d, 512), _tile(fs, 1408)
    nw = fs // tw
    d_wgu = _mm(f"{tag}_dwgu", h, dgu4, mode="tn", grid=(nb, nw, d // tm),
                a_spec=pl.BlockSpec((s, tm), lambda k, j, i: (0, i)),
                b_spec=pl.BlockSpec((None, s, tw), lambda k, j, i: (k, 0, j)),
                o_spec=pl.BlockSpec((None, tm, tw), lambda k, j, i: (k, i, j)),
                out_shape=jax.ShapeDtypeStruct((nb, d, fs), BF16))
    started = (red.start(kinds, layer, [d_wgu, d_wd]),)
    ts, td = _tile(s, 1024), _tile(d, 1024)
    dh = _mm(f"{tag}_dh", dgu4, gu_w, mode="nt", grid=(s // ts, d // td, nb),
             a_spec=pl.BlockSpec((None, ts, fs), lambda i, j, k: (k, i, 0)),
             b_spec=pl.BlockSpec((None, td, fs), lambda i, j, k: (k, j, 0)),
             o_spec=pl.BlockSpec((ts, td), lambda i, j, k: (i, j)),
             out_shape=jax.ShapeDtypeStruct((s, d), F32), nk=nb, acc_shape=(ts, td), deps=started)
    dx, dg_pre = _norm_bwd(f"{tag}_pre_bwd", dh, x, g_pre, 1.0, dx_new, F32)
    return dx, dg_pre, dg_post


def _mixer_forward(tag, x, gains, feed, k, conv_taps, dims):
    qd, kvd, cd = dims
    s, d = x.shape
    g_pre, g_a, g_c, g_post = gains
    h = _norm_fwd(f"{tag}_norm", x, g_pre)
    win_w = feed.take(k, h)
    nb, cw = win_w.shape[0], win_w.shape[2]
    tm = _tile(s, 1024)
    z = _mm(f"{tag}_in", h, win_w, mode="nn", grid=(nb, s // tm),
            a_spec=pl.BlockSpec((tm, d), lambda j, i: (i, 0)),
            b_spec=pl.BlockSpec((None, d, cw), lambda j, i: (j, 0, 0)),
            o_spec=pl.BlockSpec((tm, cw), lambda j, i: (i, j)),
            out_shape=jax.ShapeDtypeStruct((s, nb * cw), BF16))
    a, lse = _attn_fwd(f"{tag}_attn", z, qd, kvd)
    c = _conv_fwd(f"{tag}_conv", z, conv_taps, qd + 2 * kvd, cd)
    cat = _cat_norm_fwd(f"{tag}_cat", a, c, g_a, g_c)
    wout_w = feed.take(k + 1, cat).reshape(-1, d)
    mw = qd + cd
    tn = _tile(d, 1024)
    mixed = _mm(f"{tag}_out", cat, wout_w, mode="nn", grid=(s // tm, d // tn),
                a_spec=pl.BlockSpec((tm, mw), lambda i, j: (i, 0)),
                b_spec=pl.BlockSpec((mw, tn), lambda i, j: (0, j)),
                o_spec=pl.BlockSpec((tm, tn), lambda i, j: (i, j)),
                out_shape=jax.ShapeDtypeStruct((s, d), F32))
    x_new = _res_norm(f"{tag}_post", x, mixed, g_post, 1.0)
    return x_new, (x, h, z, a, lse, c, cat, mixed)


def _mixer_backward(tag, dx_new, saved, gains, win_w, conv_taps, wout_w, dims, red, kinds, layer, deps):
    qd, kvd, cd = dims
    x, h, z, a, lse, c, cat, mixed = saved
    s, d = x.shape
    nb, cw = win_w.shape[0], win_w.shape[2]
    g_pre, g_a, g_c, g_post = gains
    mw = qd + cd
    dmixed, dg_post = _norm_bwd(f"{tag}_post_bwd", dx_new, mixed, g_post, 1.0, None, BF16)
    tm, tn = _tile(s, 1024), _tile(mw, 1024)
    dcat = _mm(f"{tag}_dcat", dmixed, wout_w, mode="nt", grid=(s // tm, mw // tn),
               a_spec=pl.BlockSpec((tm, d), lambda i, j: (i, 0)),
               b_spec=pl.BlockSpec((tn, d), lambda i, j: (j, 0)),
               o_spec=pl.BlockSpec((tm, tn), lambda i, j: (i, j)),
               out_shape=jax.ShapeDtypeStruct((s, mw), F32), deps=deps)
    wr = mw // nb
    td = _tile(d, 1024)
    d_wout = _mm(f"{tag}_dwout", cat, dmixed, mode="tn", grid=(nb, d // td),
                 a_spec=pl.BlockSpec((s, wr), lambda i, j: (0, i)),
                 b_spec=pl.BlockSpec((s, td), lambda i, j: (0, j)),
                 o_spec=pl.BlockSpec((None, wr, td), lambda i, j: (i, 0, j)),
                 out_shape=jax.ShapeDtypeStruct((nb, wr, d), BF16))
    da, dc, dg_a, dg_c = _cat_norm_bwd(f"{tag}_cat_bwd", dcat, a, c, g_a, g_c)
    dhc, dbg, dcg, d_taps = _conv_bwd(f"{tag}_conv_bwd", z, conv_taps, dc, qd + 2 * kvd, cd)
    dq, dk, dv = _attn_bwd(f"{tag}_attn_bwd", z, a, lse, da, qd, kvd)
    dz = jnp.concatenate([dq, dk, dv, dhc, dbg, dcg], axis=1)
    th = _tile(d, 512)
    d_win = _mm(f"{tag}_dwin", h, dz, mode="tn", grid=(nb, d // th),
                a_spec=pl.BlockSpec((s, th), lambda k, i: (0, i)),
                b_spec=pl.BlockSpec((s, cw), lambda k, i: (0, k)),
                o_spec=pl.BlockSpec((None, th, cw), lambda k, i: (k, i, 0)),
                out_shape=jax.ShapeDtypeStruct((nb, d, cw), BF16))
    started = (red.start(kinds, layer, [d_win, d_wout]),)
    dh = _mm(f"{tag}_dh", dz, win_w, mode="nt", grid=(s // tm, d // td, nb),
             a_spec=pl.BlockSpec((tm, cw), lambda i, j, k: (i, k)),
             b_spec=pl.BlockSpec((None, td, cw), lambda i, j, k: (k, j, 0)),
             o_spec=pl.BlockSpec((tm, td), lambda i, j, k: (i, j)),
             out_shape=jax.ShapeDtypeStruct((s, d), F32), nk=nb, acc_shape=(tm, td), deps=started)
    dx, dg_pre = _norm_bwd(f"{tag}_pre_bwd", dh, x, g_pre, 1.0, dx_new, F32)
    return dx, d_taps, (dg_pre, dg_a, dg_c, dg_post)


def _pad_cols(v, width):
    return jnp.pad(v, ((0, 0), (0, width - v.shape[1])))


def kernel(x, ffn1_norm_pre, ffn1_w_gate_up, ffn1_w_down, ffn1_norm_post, mix_norm_pre, w_in, conv_w, attn_out_norm, conv_out_norm, w_out, mix_norm_post, ffn2_norm_pre, ffn2_w_gate_up, ffn2_w_down, ffn2_norm_post, loss_target, m_ffn1_norm_pre, m_ffn1_w_gate_up, m_ffn1_w_down, m_ffn1_norm_post, m_mix_norm_pre, m_w_in, m_conv_w, m_attn_out_norm, m_conv_out_norm, m_w_out, m_mix_norm_post, m_ffn2_norm_pre, m_ffn2_w_gate_up, m_ffn2_w_down, m_ffn2_norm_post, v_ffn1_norm_pre, v_ffn1_w_gate_up, v_ffn1_w_down, v_ffn1_norm_post, v_mix_norm_pre, v_w_in, v_conv_w, v_attn_out_norm, v_conv_out_norm, v_w_out, v_mix_norm_post, v_ffn2_norm_pre, v_ffn2_w_gate_up, v_ffn2_w_down, v_ffn2_norm_post):
    _, s, d = x.shape
    n_layers = ffn1_norm_pre.shape[0]
    qd = attn_out_norm.shape[1]
    cd = conv_out_norm.shape[1]
    kvd = qd // Q_PER_KV
    dims = (qd, kvd, cd)
    assert N_CHIPS * w_in.shape[2] == qd + 2 * kvd + 3 * cd and qd + cd == N_CHIPS * w_out.shape[1]
    assert 2 * d <= SMALL_ROWS * LANES * SUBLANES
    chip = 2 * lax.axis_index("x") + lax.axis_index("y")
    chip_arr = chip.astype(jnp.int32).reshape(1)
    core = lax.axis_index("c").astype(jnp.int32).reshape(1)
    kinds = ("gu1", "dn1", "win", "wout", "gu2", "dn2")

    big = (ffn1_w_gate_up, ffn1_w_down, w_in, w_out, ffn2_w_gate_up, ffn2_w_down)
    nk = len(kinds)
    taps_all = _gather_taps(conv_w)
    feed = _WeightFeed()
    order = [(k, w, layer) for layer in range(n_layers) for k, w in zip(kinds, big)]
    k, w, layer = order[0]
    token = feed.start("gather_start_first", [_cast_into_slot(f"cast_{k}_{layer}", w, layer, chip_arr)], taps_all)
    feed.start("gather_start_rest", [_cast_into_slot(f"cast_{k}_{layer}", w, layer, chip_arr, (token,))
                                     for k, w, layer in order[1:]], token)
    taps = jnp.transpose(taps_all, (1, 2, 0, 3)).reshape(n_layers, CONV_WIDTH, cd)
    taps = jnp.pad(taps, ((0, 0), (0, SUBLANES - CONV_WIDTH), (0, 0)))

    def gain(g, layer):
        return g[layer][None, :]

    xs = x[0]
    saved = []
    for layer in range(n_layers):
        t = f"l{layer}"
        k0 = layer * nk
        xs, s1 = _ffn_forward(f"{t}_ffn1", xs, gain(ffn1_norm_pre, layer), gain(ffn1_norm_post, layer), feed, k0)
        mix_gains = (gain(mix_norm_pre, layer), gain(attn_out_norm, layer), gain(conv_out_norm, layer), gain(mix_norm_post, layer))
        xs, s2 = _mixer_forward(f"{t}_mix", xs, mix_gains, feed, k0 + 2, taps[layer], dims)
        xs, s3 = _ffn_forward(f"{t}_ffn2", xs, gain(ffn2_norm_pre, layer), gain(ffn2_norm_post, layer), feed, k0 + 4)
        saved.append((s1, s2, s3, mix_gains))
    wts = {k: [feed.fulls[layer * nk + i] for layer in range(n_layers)] for i, k in enumerate(kinds)}
    for k in ("dn1", "wout", "dn2"):
        wts[k] = [w.reshape(-1, d) for w in wts[k]]
    dxs, loss_part = _loss_head("loss_head", xs, loss_target[0])
    loss = lax.psum(jnp.sum(loss_part), ("x", "y", "c"))

    red = _GradReduce(core, chip_arr, n_layers)
    small = [None] * n_layers
    flow = {"deps": (), "in_flight": None}

    def between(dx, group):
        after = dx
        if flow["in_flight"] is not None:
            after = red.finish(*flow["in_flight"], after)
        flow["deps"] = (red.exchange(*group, after),)
        flow["in_flight"] = group

    for layer in reversed(range(n_layers)):
        t = f"l{layer}"
        s1, s2, s3, mix_gains = saved[layer]
        dxs, p_pre2, p_post2 = _ffn_backward(
            f"{t}_ffn2", dxs, s3, gain(ffn2_norm_pre, layer), gain(ffn2_norm_post, layer),
            wts["gu2"][layer], wts["dn2"][layer], red, ("gu2", "dn2"), layer, flow["deps"])
        between(dxs, (("gu2", "dn2"), layer))
        dxs, p_taps, (p_mpre, p_a, p_c, p_mpost) = _mixer_backward(
            f"{t}_mix", dxs, s2, mix_gains, wts["win"][layer], taps[layer], wts["wout"][layer], dims,
            red, ("win", "wout"), layer, flow["deps"])
        between(dxs, (("win", "wout"), layer))
        dxs, p_pre1, p_post1 = _ffn_backward(
            f"{t}_ffn1", dxs, s1, gain(ffn1_norm_pre, layer), gain(ffn1_norm_post, layer),
            wts["gu1"][layer], wts["dn1"][layer], red, ("gu1", "dn1"), layer, flow["deps"])
        between(dxs, (("gu1", "dn1"), layer))
        tap_rows = jnp.zeros((CONV_WIDTH, SUBLANES, d), F32).at[:, 0, :cd].set(p_taps[:CONV_WIDTH])
        rows = [p_pre1, p_post1, p_mpre, jnp.concatenate([p_a, p_c], axis=1), p_mpost, p_pre2, p_post2]
        rows = jnp.concatenate([jnp.stack(rows), tap_rows], axis=0)
        small[layer] = jnp.pad(rows, ((0, SMALL_ROWS - rows.shape[0]), (0, 0), (0, 0)))
    grad_x = dxs[None]

    weights = dict(ffn1_norm_pre=ffn1_norm_pre, ffn1_w_gate_up=ffn1_w_gate_up, ffn1_w_down=ffn1_w_down, ffn1_norm_post=ffn1_norm_post, mix_norm_pre=mix_norm_pre, w_in=w_in, conv_w=conv_w, attn_out_norm=attn_out_norm, conv_out_norm=conv_out_norm, w_out=w_out, mix_norm_post=mix_norm_post, ffn2_norm_pre=ffn2_norm_pre, ffn2_w_gate_up=ffn2_w_gate_up, ffn2_w_down=ffn2_w_down, ffn2_norm_post=ffn2_norm_post)
    m_in = dict(ffn1_norm_pre=m_ffn1_norm_pre, ffn1_w_gate_up=m_ffn1_w_gate_up, ffn1_w_down=m_ffn1_w_down, ffn1_norm_post=m_ffn1_norm_post, mix_norm_pre=m_mix_norm_pre, w_in=m_w_in, conv_w=m_conv_w, attn_out_norm=m_attn_out_norm, conv_out_norm=m_conv_out_norm, w_out=m_w_out, mix_norm_post=m_mix_norm_post, ffn2_norm_pre=m_ffn2_norm_pre, ffn2_w_gate_up=m_ffn2_w_gate_up, ffn2_w_down=m_ffn2_w_down, ffn2_norm_post=m_ffn2_norm_post)
    v_in = dict(ffn1_norm_pre=v_ffn1_norm_pre, ffn1_w_gate_up=v_ffn1_w_gate_up, ffn1_w_down=v_ffn1_w_down, ffn1_norm_post=v_ffn1_norm_post, mix_norm_pre=v_mix_norm_pre, w_in=v_w_in, conv_w=v_conv_w, attn_out_norm=v_attn_out_norm, conv_out_norm=v_conv_out_norm, w_out=v_w_out, mix_norm_post=v_mix_norm_post, ffn2_norm_pre=v_ffn2_norm_pre, ffn2_w_gate_up=v_ffn2_w_gate_up, ffn2_w_down=v_ffn2_w_down, ffn2_norm_post=v_ffn2_norm_post)
    kind_name = dict(gu1="ffn1_w_gate_up", dn1="ffn1_w_down", win="w_in", wout="w_out", gu2="ffn2_w_gate_up", dn2="ffn2_w_down")
    delta, new_m, new_v, grad = {}, {}, {}, {}

    def join_and_update(name, kind_list, deps, after):
        ts, send_sems, recv_sems = _join_start(name, [red.bufs[k] for k in kind_list], deps)
        for a, k in enumerate(kind_list):
            n = kind_name[k]
            g = _join_wait(f"join_wait_{k}", ts[a], a, send_sems, recv_sems, after)
            delta[n], new_m[n], new_v[n], grad[n] = _adamw(f"adamw_{n}", weights[n], g, m_in[n], v_in[n], True)
            after = delta[n]

    early = ("gu2", "dn2", "win", "wout")
    join_and_update("join_early", early, flow["deps"], dxs)
    late, last_layer = flow["in_flight"]
    red.finish(late, last_layer, [delta[kind_name[e]] for e in early])
    join_and_update("join_late", late, (), delta[kind_name[early[-1]]])

    small_sum = _allreduce_small(jnp.concatenate(small, axis=0)).reshape(n_layers, SMALL_ROWS, d)
    g_ffn1_pre, g_ffn1_post, g_mix_pre = small_sum[:, 0], small_sum[:, 1], small_sum[:, 2]
    g_attn_out, g_conv_out = small_sum[:, 3, :qd], small_sum[:, 3, qd:qd + cd]
    g_mix_post, g_ffn2_pre, g_ffn2_post = small_sum[:, 4], small_sum[:, 5], small_sum[:, 6]
    cc = conv_w.shape[2]
    g_conv = lax.dynamic_slice_in_dim(small_sum[:, 7:7 + CONV_WIDTH, :cd], chip * cc, cc, axis=2)

    grad.update(ffn1_norm_pre=g_ffn1_pre, ffn1_norm_post=g_ffn1_post, mix_norm_pre=g_mix_pre, conv_w=g_conv, attn_out_norm=g_attn_out, conv_out_norm=g_conv_out, mix_norm_post=g_mix_post, ffn2_norm_pre=g_ffn2_pre, ffn2_norm_post=g_ffn2_post)
    names = list(weights)

    vectors = [n for n in names if n not in kind_name.values()]

    def pack(tree):
        flat = jnp.concatenate([tree[n].reshape(-1) for n in vectors])
        return jnp.pad(flat, (0, -flat.size % (SUBLANES * LANES))).reshape(-1, LANES)

    packed = _adamw("adamw_small", pack(weights), pack(grad), pack(m_in), pack(v_in))
    offset = 0
    for n in vectors:
        size = weights[n].size
        for tree, flat in zip((delta, new_m, new_v), packed):
            tree[n] = flat.reshape(-1)[offset:offset + size].reshape(weights[n].shape)
        offset += size

    return (loss, grad_x, *[grad[n] for n in names], *[delta[n] for n in names],
            *[new_m[n] for n in names], *[new_v[n] for n in names])
```

```python
import functools

import jax
import jax.numpy as jnp
from jax import lax
from jax.experimental import pallas as pl
from jax.experimental.pallas import tpu as pltpu

F32 = jnp.float32
BF16 = jnp.bfloat16
MESH = pl.DeviceIdType.MESH

NORM_EPS = 1e-6
HEAD_DIM = 128
Q_PER_KV = 4
CONV_WIDTH = 3
FFN_RESIDUAL_WEIGHT = 0.5
DILATED_BRANCHES = ((128, 1), (512, 4), (2048, 16))
ADAM_LR = 0.001
ADAM_B1 = 0.9
ADAM_B2 = 0.999
ADAM_EPS = 1e-08
ADAM_WD = 0.01
ADAM_STEP = 10

N_CHIPS = 4
N_DEV = 8
V7X_VMEM_BYTES = 64 << 20
VMEM_LIMIT = V7X_VMEM_BYTES - (12 << 20)
SUBLANES = 8
LANES = 128
SMALL_ROWS = 16
BIG_BLOCK = 4 << 20


def _params(*sem):
    return pltpu.CompilerParams(dimension_semantics=sem, vmem_limit_bytes=VMEM_LIMIT)


def _row_tile(rows, cols, itemsize=4, budget=2 << 20):
    t = rows
    while t * cols * itemsize > budget and t % 32 == 0:
        t //= 2
    return t


def _sum_to_sublanes(v):
    r, n = v.shape
    return v.reshape(r // SUBLANES, SUBLANES, n).sum(axis=0)


_DIMS = {
    "nn": (((1,), (0,)), ((), ())),
    "nt": (((1,), (1,)), ((), ())),
    "tn": (((0,), (0,)), ((), ())),
}


ANY_SPEC = pl.BlockSpec(memory_space=pl.ANY)


def _dot(a, b, mode):
    return lax.dot_general(a, b, _DIMS[mode], preferred_element_type=F32)


def _mm(name, a, b, *, mode, grid, a_spec, b_spec, o_spec, out_shape, nk=1, acc_shape=None, deps=()):
    nd = len(deps)

    def body(a_ref, b_ref, *rest):
        o_ref, scratch = rest[nd], rest[nd + 1:]
        r = _dot(a_ref[...], b_ref[...], mode)
        if nk == 1:
            o_ref[...] = r.astype(o_ref.dtype)
        else:
            acc = scratch[0]
            k = pl.program_id(len(grid) - 1)

            @pl.when(k == 0)
            def _():
                acc[...] = r

            @pl.when(k > 0)
            def _():
                acc[...] += r

            @pl.when(k == nk - 1)
            def _():
                o_ref[...] = acc[...].astype(o_ref.dtype)

    sem = ("parallel",) * (len(grid) - (1 if nk > 1 else 0)) + (("arbitrary",) if nk > 1 else ())
    return pl.pallas_call(
        body, name=name, grid=grid, in_specs=[a_spec, b_spec] + [ANY_SPEC] * nd, out_specs=o_spec,
        out_shape=out_shape, scratch_shapes=[pltpu.VMEM(acc_shape, F32)] if nk > 1 else [],
        compiler_params=_params(*sem),
    )(a, b, *deps)


def _tile(n, want):
    if n <= want:
        return n
    best = None
    for t in range(LANES, want + 1, LANES):
        if n % t == 0:
            best = t
    assert best is not None, (n, want)
    return best


def _norm_fwd(name, x, gain):
    s, d = x.shape
    tr = _row_tile(s, d, budget=BIG_BLOCK)

    def body(x_ref, g_ref, o_ref):
        xv = x_ref[...]
        r = lax.rsqrt(jnp.mean(xv * xv, axis=-1, keepdims=True) + NORM_EPS)
        o_ref[...] = (xv * r * g_ref[...]).astype(o_ref.dtype)

    return pl.pallas_call(
        body, name=name, grid=(s // tr,),
        in_specs=[pl.BlockSpec((tr, d), lambda i: (i, 0)), pl.BlockSpec((1, d), lambda i: (0, 0))],
        out_specs=pl.BlockSpec((tr, d), lambda i: (i, 0)),
        out_shape=jax.ShapeDtypeStruct((s, d), BF16), compiler_params=_params("parallel"),
    )(x, gain)


def _res_norm(name, x, y, gain, scale, next_gain=None):
    s, d = x.shape
    tr = _row_tile(s, d, budget=BIG_BLOCK)
    with_next = next_gain is not None

    def body(x_ref, y_ref, g_ref, *rest):
        yv = y_ref[...]
        r = lax.rsqrt(jnp.mean(yv * yv, axis=-1, keepdims=True) + NORM_EPS)
        xn = x_ref[...] + scale * (yv * r * g_ref[...])
        if with_next:
            ng_ref, o_ref, h_ref = rest
            rn = lax.rsqrt(jnp.mean(xn * xn, axis=-1, keepdims=True) + NORM_EPS)
            h_ref[...] = (xn * rn * ng_ref[...]).astype(h_ref.dtype)
        else:
            o_ref, = rest
        o_ref[...] = xn

    row = pl.BlockSpec((tr, d), lambda i: (i, 0))
    vec = pl.BlockSpec((1, d), lambda i: (0, 0))
    outs = pl.pallas_call(
        body, name=name, grid=(s // tr,),
        in_specs=[row, row, vec] + ([vec] if with_next else []), out_specs=[row] * (2 if with_next else 1),
        out_shape=[jax.ShapeDtypeStruct((s, d), F32)] + ([jax.ShapeDtypeStruct((s, d), BF16)] if with_next else []),
        compiler_params=_params("parallel"),
    )(x, y, gain, *((next_gain,) if with_next else ()))
    return (outs[0], outs[1]) if with_next else (outs[0], None)


def _norm_bwd(name, dout, yin, gain, scale, resid, out_dtype):
    s, d = yin.shape
    tr = _row_tile(s, d)
    has_resid = resid is not None

    def body(*refs):
        if has_resid:
            do_ref, y_ref, g_ref, r_ref, di_ref, dg_ref = refs
        else:
            do_ref, y_ref, g_ref, di_ref, dg_ref = refs
        yv = y_ref[...]
        r = lax.rsqrt(jnp.mean(yv * yv, axis=-1, keepdims=True) + NORM_EPS)
        xhat = yv * r
        dn = scale * do_ref[...]
        part = _sum_to_sublanes(dn * xhat)

        @pl.when(pl.program_id(0) == 0)
        def _():
            dg_ref[...] = part

        @pl.when(pl.program_id(0) > 0)
        def _():
            dg_ref[...] += part

        dxn = dn * g_ref[...]
        din = r * (dxn - xhat * jnp.mean(dxn * xhat, axis=-1, keepdims=True))
        if has_resid:
            din = din + r_ref[...]
        di_ref[...] = din.astype(di_ref.dtype)

    row = pl.BlockSpec((tr, d), lambda i: (i, 0))
    vec = pl.BlockSpec((1, d), lambda i: (0, 0))
    ins = [row, row, vec] + ([row] if has_resid else [])
    args = (dout, yin, gain) + ((resid,) if has_resid else ())
    return pl.pallas_call(
        body, name=name, grid=(s // tr,), in_specs=ins,
        out_specs=[row, pl.BlockSpec((SUBLANES, d), lambda i: (0, 0))],
        out_shape=[jax.ShapeDtypeStruct((s, d), out_dtype), jax.ShapeDtypeStruct((SUBLANES, d), F32)],
        compiler_params=_params("arbitrary"),
    )(*args)


def _loss_head(name, y, target):
    s, d = y.shape
    tr = _row_tile(s, d)

    def body(y_ref, t_ref, dy_ref, l_ref):
        e = y_ref[...] - t_ref[...]
        dy_ref[...] = e * (1.0 / d)
        part = _sum_to_sublanes(e * e) * (0.5 / d)

        @pl.when(pl.program_id(0) == 0)
        def _():
            l_ref[...] = part

        @pl.when(pl.program_id(0) > 0)
        def _():
            l_ref[...] += part

    row = pl.BlockSpec((tr, d), lambda i: (i, 0))
    return pl.pallas_call(
        body, name=name, grid=(s // tr,), in_specs=[row, row],
        out_specs=[row, pl.BlockSpec((SUBLANES, d), lambda i: (0, 0))],
        out_shape=[jax.ShapeDtypeStruct((s, d), F32), jax.ShapeDtypeStruct((SUBLANES, d), F32)],
        compiler_params=_params("arbitrary"),
    )(y, target)


def _ffn_up(name, h, gu_w):
    s, d = h.shape
    nb, _, fs = gu_w.shape
    hb = nb // 2
    w = gu_w.reshape(2, hb, d, fs)
    tm = _tile(s, 512)
    tn = _tile(fs, 1408)
    nj = fs // tn

    def body(h_ref, w_ref, gu_ref, a_ref):
        hv = h_ref[...]
        g = _dot(hv, w_ref[0], "nn")
        u = _dot(hv, w_ref[1], "nn")
        gu_ref[0] = g.astype(gu_ref.dtype)
        gu_ref[1] = u.astype(gu_ref.dtype)
        a_ref[...] = (g * jax.nn.sigmoid(g) * u).astype(a_ref.dtype)

    return pl.pallas_call(
        body, name=name, grid=(hb, nj, s // tm),
        in_specs=[pl.BlockSpec((tm, d), lambda jb, jo, i: (i, 0)),
                  pl.BlockSpec((2, None, d, tn), lambda jb, jo, i: (0, jb, 0, jo))],
        out_specs=[pl.BlockSpec((2, None, tm, tn), lambda jb, jo, i: (0, jb, i, jo)),
                   pl.BlockSpec((tm, tn), lambda jb, jo, i: (i, jb * nj + jo))],
        out_shape=[jax.ShapeDtypeStruct((2, hb, s, fs), BF16), jax.ShapeDtypeStruct((s, hb * fs), BF16)],
        compiler_params=_params("parallel", "parallel", "parallel"),
    )(h, w)


def _ffn_dact(name, dy, dn_w, gu, deps=()):
    s, d = dy.shape
    _, hb, _, fs = gu.shape
    tm = _tile(s, 512)
    tn = _tile(fs, 1408)
    nj = fs // tn

    def body(dy_ref, w_ref, gu_ref, *rest):
        o_ref = rest[-1]
        wv = w_ref[...]
        parts = 2 if tm % (2 * SUBLANES * 2) == 0 else 1
        for r in range(parts):
            rows = slice(r * (tm // parts), (r + 1) * (tm // parts))
            da = _dot(dy_ref[rows, :], wv, "nt")
            g = gu_ref[0, rows, :].astype(F32)
            u = gu_ref[1, rows, :].astype(F32)
            sg = jax.nn.sigmoid(g)
            o_ref[0, rows, :] = (da * u * (sg * (1.0 + g * (1.0 - sg)))).astype(o_ref.dtype)
            o_ref[1, rows, :] = (da * (g * sg)).astype(o_ref.dtype)

    blk = pl.BlockSpec((2, None, tm, tn), lambda jb, jo, i: (0, jb, i, jo))
    return pl.pallas_call(
        body, name=name, grid=(hb, nj, s // tm),
        in_specs=[pl.BlockSpec((tm, d), lambda jb, jo, i: (i, 0)),
                  pl.BlockSpec((tn, d), lambda jb, jo, i: (jb * nj + jo, 0)),
                  blk] + [ANY_SPEC] * len(deps),
        out_specs=blk, out_shape=jax.ShapeDtypeStruct(gu.shape, BF16),
        compiler_params=_params("parallel", "parallel", "parallel"),
    )(dy, dn_w, gu, *deps)


_MASKED = -1e30


def _attn_bias(s, tq):
    nd = s // tq
    dist = (jnp.arange(nd)[:, None, None] * tq + jnp.arange(tq)[None, :, None]) - jnp.arange(tq)[None, None, :]
    mult = jnp.zeros(dist.shape, F32)
    for window, dilation in DILATED_BRANCHES:
        mult = mult + ((dist >= 0) & (dist <= window) & (dist % dilation == 0)).astype(F32)
    return jnp.where(mult > 0.0, jnp.log(jnp.maximum(mult, 1.0)), _MASKED)


def _biased(sc, bias, scale):
    tq, tk = bias.shape
    return (sc.reshape(-1, tq, tk) * scale + bias[None]).reshape(sc.shape)


def _attn_specs(s, qd, kvd, tq):
    rw = Q_PER_KV * HEAD_DIM
    qspec = pl.BlockSpec((tq, rw), lambda g, i: (i, g))
    kspec = pl.BlockSpec((s, HEAD_DIM), lambda g, i: (0, qd // HEAD_DIM + g))
    vspec = pl.BlockSpec((s, HEAD_DIM), lambda g, i: (0, (qd + kvd) // HEAD_DIM + g))
    return rw, qspec, kspec, vspec


def _attn_fwd(name, z, qd, kvd):
    s = z.shape[0]
    tq = _tile(s, 256)
    nkv = kvd // HEAD_DIM
    rw, qspec, kspec, vspec = _attn_specs(s, qd, kvd, tq)
    scale = HEAD_DIM ** -0.5

    def body(q_ref, k_ref, v_ref, b_ref, o_ref, l_ref):
        i = pl.program_id(1)
        heads = [slice(h * HEAD_DIM, (h + 1) * HEAD_DIM) for h in range(Q_PER_KV)]
        q_all = jnp.concatenate([q_ref[:, cols] for cols in heads], axis=0)

        def chunk(j, carry):
            mx, den, acc = carry
            k0 = pl.multiple_of(j * tq, tq)
            kc, vc = k_ref[pl.ds(k0, tq), :], v_ref[pl.ds(k0, tq), :]
            sc = _biased(_dot(q_all, kc, "nt"), b_ref[i - j], scale)
            mx_new = jnp.maximum(mx, jnp.max(sc, axis=-1, keepdims=True))
            alpha = jnp.exp(mx - mx_new)
            p = jnp.exp(sc - mx_new)
            return (mx_new, alpha * den + jnp.sum(p, axis=-1, keepdims=True),
                    alpha * acc + _dot(p.astype(BF16), vc, "nn"))

        rows = Q_PER_KV * tq
        init = (jnp.full((rows, 1), _MASKED, F32), jnp.zeros((rows, 1), F32), jnp.zeros((rows, HEAD_DIM), F32))
        mx, den, acc = lax.fori_loop(0, i + 1, chunk, init)
        out = acc / den
        lse = mx + jnp.log(den)
        for h, cols in enumerate(heads):
            o_ref[:, cols] = out[h * tq:(h + 1) * tq]
            l_ref[:, cols] = jnp.broadcast_to(lse[h * tq:(h + 1) * tq], (tq, HEAD_DIM))

    bias = _attn_bias(s, tq)
    return pl.pallas_call(
        body, name=name, grid=(nkv, s // tq),
        in_specs=[qspec, kspec, vspec, pl.BlockSpec(bias.shape, lambda g, i: (0, 0, 0))], out_specs=[qspec, qspec],
        out_shape=[jax.ShapeDtypeStruct((s, qd), F32), jax.ShapeDtypeStruct((s, qd), F32)],
        compiler_params=_params("parallel", "parallel"),
    )(z, z, z, bias)


def _attn_bwd(name, z, o, lse, do, qd, kvd):
    s = z.shape[0]
    tq = _tile(s, 256)
    nkv = kvd // HEAD_DIM
    nq = s // tq
    rw, qspec, kspec, vspec = _attn_specs(s, qd, kvd, tq)
    scale = HEAD_DIM ** -0.5

    def body(q_ref, k_ref, v_ref, o_ref, l_ref, do_ref, b_ref, dq_ref, dk_ref, dv_ref, dk_acc, dv_acc):
        i = pl.program_id(1)
        heads = [slice(h * HEAD_DIM, (h + 1) * HEAD_DIM) for h in range(Q_PER_KV)]

        @pl.when(i == 0)
        def _():
            dk_acc[...] = jnp.zeros_like(dk_acc)
            dv_acc[...] = jnp.zeros_like(dv_acc)

        q_all = jnp.concatenate([q_ref[:, cols] for cols in heads], axis=0)
        do_all = jnp.concatenate([do_ref[:, cols].astype(BF16) for cols in heads], axis=0)
        lse_all = jnp.concatenate([l_ref[:, cols][:, :1] for cols in heads], axis=0)
        delta_all = jnp.concatenate(
            [jnp.sum(do_ref[:, cols] * o_ref[:, cols], axis=-1, keepdims=True) for cols in heads], axis=0)

        def chunk(j, dq):
            k0 = pl.multiple_of(j * tq, tq)
            kc, vc = k_ref[pl.ds(k0, tq), :], v_ref[pl.ds(k0, tq), :]
            p = jnp.exp(_biased(_dot(q_all, kc, "nt"), b_ref[i - j], scale) - lse_all)
            ds = (p * (_dot(do_all, vc, "nt") - delta_all) * scale).astype(BF16)
            dk_acc[pl.ds(k0, tq), :] += _dot(ds, q_all, "tn")
            dv_acc[pl.ds(k0, tq), :] += _dot(p.astype(BF16), do_all, "tn")
            return dq + _dot(ds, kc, "nn")

        dq = lax.fori_loop(0, i + 1, chunk, jnp.zeros((Q_PER_KV * tq, HEAD_DIM), F32))
        for h, cols in enumerate(heads):
            dq_ref[:, cols] = dq[h * tq:(h + 1) * tq].astype(dq_ref.dtype)

        @pl.when(i == nq - 1)
        def _():
            dk_ref[...] = dk_acc[...].astype(dk_ref.dtype)
            dv_ref[...] = dv_acc[...].astype(dv_ref.dtype)

    kvout = pl.BlockSpec((s, HEAD_DIM), lambda g, i: (0, g))
    bias = _attn_bias(s, tq)
    return pl.pallas_call(
        body, name=name, grid=(nkv, nq),
        in_specs=[qspec, kspec, vspec, qspec, qspec, qspec, pl.BlockSpec(bias.shape, lambda g, i: (0, 0, 0))],
        out_specs=[qspec, kvout, kvout],
        out_shape=[jax.ShapeDtypeStruct((s, qd), BF16), jax.ShapeDtypeStruct((s, kvd), BF16),
                   jax.ShapeDtypeStruct((s, kvd), BF16)],
        scratch_shapes=[pltpu.VMEM((s, HEAD_DIM), F32), pltpu.VMEM((s, HEAD_DIM), F32)],
        compiler_params=_params("parallel", "arbitrary"),
    )(z, z, z, o, lse, do, bias)


def _shift_down(v, n):
    rolled = pltpu.roll(v, n, 0)
    t = lax.broadcasted_iota(jnp.int32, v.shape, 0)
    return jnp.where(t >= n, rolled, 0.0)


def _shift_up(v, n):
    rows = v.shape[0]
    rolled = pltpu.roll(v, rows - n, 0)
    t = lax.broadcasted_iota(jnp.int32, v.shape, 0)
    return jnp.where(t < rows - n, rolled, 0.0)


def _conv_specs(s, base, cd, tc):
    zs = [pl.BlockSpec((s, tc), functools.partial(lambda j, off: (0, off + j), off=(base + n * cd) // tc))
          for n in range(3)]
    wspec = pl.BlockSpec((SUBLANES, tc), lambda j: (0, j))
    cspec = pl.BlockSpec((s, tc), lambda j: (0, j))
    return zs, wspec, cspec


def _conv_fwd(name, z, conv_w, base, cd):
    s = z.shape[0]
    tc = _tile(cd, 256)
    zs, wspec, cspec = _conv_specs(s, base, cd, tc)

    def body(h_ref, b_ref, c_ref, w_ref, o_ref):
        u = c_ref[...].astype(F32) * h_ref[...].astype(F32)
        y = w_ref[0:1, :] * _shift_down(u, 2) + w_ref[1:2, :] * _shift_down(u, 1) + w_ref[2:3, :] * u
        o_ref[...] = b_ref[...].astype(F32) * y

    return pl.pallas_call(
        body, name=name, grid=(cd // tc,), in_specs=zs + [wspec], out_specs=cspec,
        out_shape=jax.ShapeDtypeStruct((s, cd), F32), compiler_params=_params("parallel"),
    )(z, z, z, conv_w)


def _conv_bwd(name, z, conv_w, dc, base, cd):
    s = z.shape[0]
    tc = _tile(cd, 256)
    zs, wspec, cspec = _conv_specs(s, base, cd, tc)

    def body(h_ref, b_ref, c_ref, w_ref, dc_ref, dh_ref, db_ref, dcg_ref, dw_ref):
        hv, bv, cv = h_ref[...].astype(F32), b_ref[...].astype(F32), c_ref[...].astype(F32)
        u = cv * hv
        u1, u2 = _shift_down(u, 1), _shift_down(u, 2)
        w0, w1, w2 = w_ref[0:1, :], w_ref[1:2, :], w_ref[2:3, :]
        y = w0 * u2 + w1 * u1 + w2 * u
        dcv = dc_ref[...]
        db_ref[...] = (dcv * y).astype(db_ref.dtype)
        dy = dcv * bv
        du = w2 * dy + w1 * _shift_up(dy, 1) + w0 * _shift_up(dy, 2)
        dh_ref[...] = (du * cv).astype(dh_ref.dtype)
        dcg_ref[...] = (du * hv).astype(dcg_ref.dtype)
        g0 = jnp.sum(dy * u2, axis=0, keepdims=True)
        g1 = jnp.sum(dy * u1, axis=0, keepdims=True)
        g2 = jnp.sum(dy * u, axis=0, keepdims=True)
        r = lax.broadcasted_iota(jnp.int32, (SUBLANES, tc), 0)
        dw_ref[...] = jnp.where(r == 0, g0, jnp.where(r == 1, g1, jnp.where(r == 2, g2, 0.0)))

    return pl.pallas_call(
        body, name=name, grid=(cd // tc,), in_specs=zs + [wspec, cspec],
        out_specs=[cspec, cspec, cspec, wspec],
        out_shape=[jax.ShapeDtypeStruct((s, cd), BF16)] * 3 + [jax.ShapeDtypeStruct((SUBLANES, cd), F32)],
        compiler_params=_params("parallel"),
    )(z, z, z, conv_w, dc)


def _cat_norm_fwd(name, a, c, ga, gc):
    s, qd = a.shape
    cd = c.shape[1]
    tr = _row_tile(s, qd + cd)

    def body(a_ref, c_ref, ga_ref, gc_ref, o_ref):
        av, cv = a_ref[...], c_ref[...]
        ra = lax.rsqrt(jnp.mean(av * av, axis=-1, keepdims=True) + NORM_EPS)
        rc = lax.rsqrt(jnp.mean(cv * cv, axis=-1, keepdims=True) + NORM_EPS)
        o_ref[:, :qd] = (av * ra * ga_ref[...]).astype(o_ref.dtype)
        o_ref[:, qd:] = (cv * rc * gc_ref[...]).astype(o_ref.dtype)

    return pl.pallas_call(
        body, name=name, grid=(s // tr,),
        in_specs=[pl.BlockSpec((tr, qd), lambda i: (i, 0)), pl.BlockSpec((tr, cd), lambda i: (i, 0)),
                  pl.BlockSpec((1, qd), lambda i: (0, 0)), pl.BlockSpec((1, cd), lambda i: (0, 0))],
        out_specs=pl.BlockSpec((tr, qd + cd), lambda i: (i, 0)),
        out_shape=jax.ShapeDtypeStruct((s, qd + cd), BF16), compiler_params=_params("parallel"),
    )(a, c, ga, gc)


def _cat_norm_bwd(name, dcat, a, c, ga, gc):
    s, qd = a.shape
    cd = c.shape[1]
    tr = _row_tile(s, qd + cd)

    def one(dn, yv, gv):
        r = lax.rsqrt(jnp.mean(yv * yv, axis=-1, keepdims=True) + NORM_EPS)
        xhat = yv * r
        dxn = dn * gv
        return r * (dxn - xhat * jnp.mean(dxn * xhat, axis=-1, keepdims=True)), _sum_to_sublanes(dn * xhat)

    def body(d_ref, a_ref, c_ref, ga_ref, gc_ref, da_ref, dc_ref, dga_ref, dgc_ref):
        da, pa = one(d_ref[:, :qd], a_ref[...], ga_ref[...])
        dc, pc = one(d_ref[:, qd:], c_ref[...], gc_ref[...])
        da_ref[...] = da
        dc_ref[...] = dc

        @pl.when(pl.program_id(0) == 0)
        def _():
            dga_ref[...] = pa
            dgc_ref[...] = pc

        @pl.when(pl.program_id(0) > 0)
        def _():
            dga_ref[...] += pa
            dgc_ref[...] += pc

    ra = pl.BlockSpec((tr, qd), lambda i: (i, 0))
    rc = pl.BlockSpec((tr, cd), lambda i: (i, 0))
    return pl.pallas_call(
        body, name=name, grid=(s // tr,),
        in_specs=[pl.BlockSpec((tr, qd + cd), lambda i: (i, 0)), ra, rc,
                  pl.BlockSpec((1, qd), lambda i: (0, 0)), pl.BlockSpec((1, cd), lambda i: (0, 0))],
        out_specs=[ra, rc, pl.BlockSpec((SUBLANES, qd), lambda i: (0, 0)),
                   pl.BlockSpec((SUBLANES, cd), lambda i: (0, 0))],
        out_shape=[jax.ShapeDtypeStruct((s, qd), F32), jax.ShapeDtypeStruct((s, cd), F32),
                   jax.ShapeDtypeStruct((SUBLANES, qd), F32), jax.ShapeDtypeStruct((SUBLANES, cd), F32)],
        compiler_params=_params("arbitrary"),
    )(dcat, a, c, ga, gc)


def _adamw(name, w, g, m, v, emit_grad=False):
    shape = w.shape
    cols = shape[-1]
    rows = w.size // cols
    tr = _row_tile(rows, cols, budget=3 << 19)
    bc1 = 1.0 - ADAM_B1 ** ADAM_STEP
    bc2 = 1.0 - ADAM_B2 ** ADAM_STEP
    n_out = 4 if emit_grad else 3

    def body(w_ref, g_ref, m_ref, v_ref, d_ref, nm_ref, nv_ref, *g_out):
        gv = g_ref[...]
        mv = ADAM_B1 * m_ref[...] + (1.0 - ADAM_B1) * gv
        vv = ADAM_B2 * v_ref[...] + (1.0 - ADAM_B2) * (gv * gv)
        nm_ref[...] = mv
        nv_ref[...] = vv
        d_ref[...] = -ADAM_LR * ((mv / bc1) / (jnp.sqrt(vv / bc2) + ADAM_EPS) + ADAM_WD * w_ref[...])
        for ref in g_out:
            ref[...] = gv

    row = pl.BlockSpec((tr, cols), lambda i: (i, 0))
    outs = pl.pallas_call(
        body, name=name, grid=(rows // tr,), in_specs=[row] * 4, out_specs=[row] * n_out,
        out_shape=[jax.ShapeDtypeStruct((rows, cols), F32)] * n_out, compiler_params=_params("parallel"),
    )(*(t.reshape(rows, cols) for t in (w, g, m, v)))
    return tuple(t.reshape(shape) for t in outs)


HBM_SPEC = pl.BlockSpec(memory_space=pltpu.HBM)


def _mesh_place():
    x, y, c = lax.axis_index("x"), lax.axis_index("y"), lax.axis_index("c")
    other_chips = [(1 - x, y), (x, 1 - y), (1 - x, 1 - y)]
    return x, y, c, other_chips


def _cast_into_slot(name, w, layer, chip, deps=()):
    _, r, cols = w.shape
    tr = _row_tile(r, cols, budget=BIG_BLOCK)

    def body(chip_ref, w_ref, *rest):
        o_ref = rest[-1]
        o_ref[...] = w_ref[...].astype(o_ref.dtype)

    return pl.pallas_call(
        body, name=name,
        grid_spec=pltpu.PrefetchScalarGridSpec(
            num_scalar_prefetch=1, grid=(r // tr,),
            in_specs=[pl.BlockSpec((None, tr, cols), lambda i, chip_ref: (layer, i, 0))] + [ANY_SPEC] * len(deps),
            out_specs=pl.BlockSpec((None, tr, cols), lambda i, chip_ref: (chip_ref[0], i, 0))),
        out_shape=jax.ShapeDtypeStruct((N_CHIPS, r, cols), BF16), compiler_params=_params("parallel"),
    )(chip, w, *deps)


SEM_SPEC = pl.BlockSpec(memory_space=pltpu.SEMAPHORE)
SPLIT_COPY = pltpu.CompilerParams(has_side_effects=pltpu.SideEffectType.DATAFLOW_SIDE_EFFECTING)
N_OTHER = N_CHIPS - 1
TOKEN_SPEC = pl.BlockSpec(memory_space=pltpu.VMEM)
TOKEN_SHAPE = jax.ShapeDtypeStruct((SUBLANES, LANES), F32)


def _in_hbm(arr):
    return pltpu.with_memory_space_constraint(arr, pltpu.HBM)


def _half_rows(ref, chip_idx, core):
    r2 = ref.shape[1] // 2
    return ref.at[chip_idx, pl.ds(core * r2, r2), :]


def _gather_start(name, fulls, after):
    na = len(fulls)

    def body(*refs):
        f_refs = refs[na + 1:2 * na + 1]
        send_sems, recv_sems = refs[2 * na + 1:3 * na + 1], refs[3 * na + 1:4 * na + 1]
        token = refs[4 * na + 1]
        x, y, c, chips = _mesh_place()
        for a in range(na):
            mine = _half_rows(f_refs[a], 2 * x + y, c)
            for j, (cx, cy) in enumerate(chips):
                pltpu.make_async_remote_copy(
                    src_ref=mine, dst_ref=mine, send_sem=send_sems[a].at[j], recv_sem=recv_sems[a].at[j],
                    device_id=(cx, cy, c), device_id_type=MESH).start()
        token[...] = jnp.zeros_like(token)

    outs = pl.pallas_call(
        body, name=name, in_specs=[HBM_SPEC] * na + [ANY_SPEC],
        out_specs=[HBM_SPEC] * na + [SEM_SPEC] * (2 * na) + [TOKEN_SPEC],
        out_shape=[pltpu.HBM(f.shape, f.dtype) for f in fulls] + [pltpu.SemaphoreType.DMA((N_OTHER,))] * (2 * na)
        + [TOKEN_SHAPE],
        input_output_aliases={a: a for a in range(na)}, compiler_params=SPLIT_COPY,
    )(*[_in_hbm(f) for f in fulls], after)
    return list(outs[:na]), list(outs[na:2 * na]), list(outs[2 * na:3 * na]), outs[3 * na]


def _gather_pass_on(name, full, recv_sems, after):
    def body(f_in, recv_sems, after_ref, f_ref, d2d_send, d2d_recv):
        x, y, c, chips = _mesh_place()
        for j, (cx, cy) in enumerate(chips):
            blk = _half_rows(f_ref, 2 * cx + cy, c)
            pltpu.make_async_remote_copy(
                src_ref=blk, dst_ref=blk, send_sem=d2d_send.at[j], recv_sem=recv_sems.at[j],
                device_id=(cx, cy, c), device_id_type=MESH).wait_recv()
            pltpu.make_async_remote_copy(
                src_ref=blk, dst_ref=blk, send_sem=d2d_send.at[j], recv_sem=d2d_recv.at[j],
                device_id=(x, y, 1 - c), device_id_type=MESH).start()

    return pl.pallas_call(
        body, name=name, in_specs=[HBM_SPEC, SEM_SPEC, ANY_SPEC], out_specs=[HBM_SPEC, SEM_SPEC, SEM_SPEC],
        out_shape=[pltpu.HBM(full.shape, full.dtype)] + [pltpu.SemaphoreType.DMA((N_OTHER,))] * 2,
        input_output_aliases={0: 0}, compiler_params=SPLIT_COPY,
    )(full, recv_sems, after)


def _gather_arrive(name, full, ici_send, d2d_send, d2d_recv, after):
    def body(f_in, ici_send, d2d_send, d2d_recv, after_ref, f_ref):
        x, y, c, chips = _mesh_place()
        for j, (cx, cy) in enumerate(chips):
            mine = _half_rows(f_ref, 2 * x + y, c)
            passed = _half_rows(f_ref, 2 * cx + cy, c)
            theirs = _half_rows(f_ref, 2 * cx + cy, 1 - c)
            pltpu.make_async_remote_copy(
                src_ref=mine, dst_ref=mine, send_sem=ici_send.at[j], recv_sem=d2d_recv.at[j],
                device_id=(cx, cy, c), device_id_type=MESH).wait_send()
            pltpu.make_async_remote_copy(
                src_ref=passed, dst_ref=passed, send_sem=d2d_send.at[j], recv_sem=d2d_recv.at[j],
                device_id=(x, y, 1 - c), device_id_type=MESH).wait_send()
            pltpu.make_async_remote_copy(
                src_ref=theirs, dst_ref=theirs, send_sem=d2d_send.at[j], recv_sem=d2d_recv.at[j],
                device_id=(x, y, 1 - c), device_id_type=MESH).wait_recv()

    return pl.pallas_call(
        body, name=name, in_specs=[HBM_SPEC, SEM_SPEC, SEM_SPEC, SEM_SPEC, ANY_SPEC], out_specs=HBM_SPEC,
        out_shape=pltpu.HBM(full.shape, full.dtype), input_output_aliases={0: 0}, compiler_params=SPLIT_COPY,
    )(full, ici_send, d2d_send, d2d_recv, after)


def _gather_taps(conv_w):
    def body(cw_ref, cwf_ref, send_sems, recv_sems, local_sem):
        x, y, c, chips = _mesh_place()
        k_me = 2 * x + y
        local = pltpu.make_async_copy(cw_ref, cwf_ref.at[k_me], local_sem)
        local.start()
        copies = [pltpu.make_async_remote_copy(
            src_ref=cw_ref, dst_ref=cwf_ref.at[k_me], send_sem=send_sems.at[j], recv_sem=recv_sems.at[j],
            device_id=(cx, cy, c), device_id_type=MESH) for j, (cx, cy) in enumerate(chips)]
        for cp in copies:
            cp.start()
        for j, (cx, cy) in enumerate(chips):
            pltpu.make_async_remote_copy(
                src_ref=cw_ref, dst_ref=cwf_ref.at[2 * cx + cy], send_sem=send_sems.at[j], recv_sem=recv_sems.at[j],
                device_id=(cx, cy, c), device_id_type=MESH).wait_recv()
        for cp in copies:
            cp.wait_send()
        local.wait()

    return pl.pallas_call(
        body, name="gather_taps", in_specs=[HBM_SPEC], out_specs=HBM_SPEC,
        out_shape=jax.ShapeDtypeStruct((N_CHIPS,) + conv_w.shape, conv_w.dtype),
        scratch_shapes=[pltpu.SemaphoreType.DMA((N_OTHER,))] * 2 + [pltpu.SemaphoreType.DMA],
    )(conv_w)


def _sibling_half(g_ref, c):
    r2 = g_ref.shape[1] // 2
    return g_ref.at[:, pl.ds((1 - c) * r2, r2), :]


def _swap_copy(g_ref, land_ref, send_sems, recv_sems, a):
    x, y, c, _ = _mesh_place()
    return pltpu.make_async_remote_copy(
        src_ref=_sibling_half(g_ref, c), dst_ref=land_ref, send_sem=send_sems.at[a], recv_sem=recv_sems.at[a],
        device_id=(x, y, 1 - c), device_id_type=MESH)


def _swap_start(name, gs):
    n = len(gs)

    def body(*refs):
        g_refs, land_refs = refs[n:2 * n], refs[2 * n:3 * n]
        send_sems, recv_sems, token = refs[3 * n:]
        for a in range(n):
            _swap_copy(g_refs[a], land_refs[a], send_sems, recv_sems, a).start()
        token[...] = jnp.zeros_like(token)

    outs = pl.pallas_call(
        body, name=name, in_specs=[HBM_SPEC] * n,
        out_specs=[HBM_SPEC] * (2 * n) + [SEM_SPEC, SEM_SPEC, TOKEN_SPEC],
        out_shape=[pltpu.HBM(g.shape, g.dtype) for g in gs]
        + [pltpu.HBM((g.shape[0], g.shape[1] // 2, g.shape[2]), g.dtype) for g in gs]
        + [pltpu.SemaphoreType.DMA((n,)), pltpu.SemaphoreType.DMA((n,)), TOKEN_SHAPE],
        input_output_aliases={a: a for a in range(n)}, compiler_params=SPLIT_COPY,
    )(*[_in_hbm(g) for g in gs])
    return list(outs[:n]), list(outs[n:2 * n]), outs[2 * n], outs[2 * n + 1], outs[2 * n + 2]


def _swap_wait(name, gs, lands, send_sems, recv_sems, after):
    n = len(gs)

    def body(*refs):
        send_sems, recv_sems = refs[2 * n], refs[2 * n + 1]
        g_refs, land_refs = refs[2 * n + 3:3 * n + 3], refs[3 * n + 3:]
        for a in range(n):
            copy = _swap_copy(g_refs[a], land_refs[a], send_sems, recv_sems, a)
            copy.wait_send()
            copy.wait_recv()

    outs = pl.pallas_call(
        body, name=name, in_specs=[HBM_SPEC] * (2 * n) + [SEM_SPEC, SEM_SPEC, ANY_SPEC],
        out_specs=[HBM_SPEC] * (2 * n),
        out_shape=[pltpu.HBM(t.shape, t.dtype) for t in list(gs) + list(lands)],
        input_output_aliases={a: a for a in range(2 * n)}, compiler_params=SPLIT_COPY,
    )(*gs, *lands, send_sems, recv_sems, after)
    return list(outs[:n]), list(outs[n:])


def _add_core_halves(name, g, sib, core):
    nb, r, cols = g.shape
    r2 = r // 2
    tr = _row_tile(r2, cols, itemsize=2, budget=BIG_BLOCK)
    nrt = r2 // tr

    def body(core_ref, g_ref, s_ref, o_ref):
        o_ref[...] = (g_ref[...].astype(F32) + s_ref[...].astype(F32)).astype(o_ref.dtype)

    return pl.pallas_call(
        body, name=name,
        grid_spec=pltpu.PrefetchScalarGridSpec(
            num_scalar_prefetch=1, grid=(nb, nrt),
            in_specs=[pl.BlockSpec((None, tr, cols), lambda k, i, core_ref: (k, core_ref[0] * nrt + i, 0)),
                      pl.BlockSpec((None, tr, cols), lambda k, i, core_ref: (k, i, 0))],
            out_specs=pl.BlockSpec((None, tr, cols), lambda k, i, core_ref: (k, i, 0))),
        out_shape=jax.ShapeDtypeStruct((nb, r2, cols), BF16), compiler_params=_params("parallel", "parallel"),
    )(core, g, sib)


def _scatter_copies(h_refs, land_refs, send_sems, recv_sems):
    x, y, c, chips = _mesh_place()
    return [pltpu.make_async_remote_copy(
        src_ref=h_ref.at[2 * cx + cy], dst_ref=land_ref.at[j],
        send_sem=send_sems.at[a * N_OTHER + j], recv_sem=recv_sems.at[a * N_OTHER + j],
        device_id=(cx, cy, c), device_id_type=MESH)
        for a, (h_ref, land_ref) in enumerate(zip(h_refs, land_refs)) for j, (cx, cy) in enumerate(chips)]


def _scatter_start(name, hs):
    n = len(hs)

    def body(*refs):
        h_refs, land_refs = refs[n:2 * n], refs[2 * n:3 * n]
        send_sems, recv_sems, token = refs[3 * n:]
        for copy in _scatter_copies(h_refs, land_refs, send_sems, recv_sems):
            copy.start()
        token[...] = jnp.zeros_like(token)

    outs = pl.pallas_call(
        body, name=name, in_specs=[HBM_SPEC] * n,
        out_specs=[HBM_SPEC] * (2 * n) + [SEM_SPEC, SEM_SPEC, TOKEN_SPEC],
        out_shape=[pltpu.HBM(h.shape, h.dtype) for h in hs]
        + [pltpu.HBM((N_OTHER,) + h.shape[1:], h.dtype) for h in hs]
        + [pltpu.SemaphoreType.DMA((n * N_OTHER,)), pltpu.SemaphoreType.DMA((n * N_OTHER,)), TOKEN_SHAPE],
        input_output_aliases={a: a for a in range(n)}, compiler_params=SPLIT_COPY,
    )(*[_in_hbm(h) for h in hs])
    return list(outs[:n]), list(outs[n:2 * n]), outs[2 * n], outs[2 * n + 1], outs[2 * n + 2]


def _scatter_wait(name, hs, lands, send_sems, recv_sems, after):
    afters = tuple(after) if isinstance(after, (tuple, list)) else (after,)
    n = len(hs)

    def body(*refs):
        send_sems, recv_sems = refs[2 * n], refs[2 * n + 1]
        h_refs, land_refs = refs[-2 * n:-n], refs[-n:]
        for copy in _scatter_copies(h_refs, land_refs, send_sems, recv_sems):
            copy.wait_send()
            copy.wait_recv()

    outs = pl.pallas_call(
        body, name=name, in_specs=[HBM_SPEC] * (2 * n) + [SEM_SPEC, SEM_SPEC] + [ANY_SPEC] * len(afters),
        out_specs=[HBM_SPEC] * (2 * n),
        out_shape=[pltpu.HBM(t.shape, t.dtype) for t in list(hs) + list(lands)],
        input_output_aliases={a: a for a in range(2 * n)}, compiler_params=SPLIT_COPY,
    )(*hs, *lands, send_sems, recv_sems, *afters)
    return list(outs[:n]), list(outs[n:])


def _sum_chips(name, hs, rcv, core, chip, layer, n_layers, prev):
    _, r2, cols = hs.shape
    tr = _row_tile(r2, cols, budget=BIG_BLOCK)
    nrt = r2 // tr

    def body(core_ref, chip_ref, h_ref, r_ref, *rest):
        o_ref = rest[-1]
        acc = h_ref[...].astype(F32)
        for j in range(N_CHIPS - 1):
            acc = acc + r_ref[j].astype(F32)
        o_ref[...] = acc

    in_specs = [pl.BlockSpec((None, tr, cols), lambda i, core_ref, chip_ref: (chip_ref[0], i, 0)),
                pl.BlockSpec((N_CHIPS - 1, tr, cols), lambda i, core_ref, chip_ref: (0, i, 0))]
    args = [core, chip, hs, rcv]
    aliases = {}
    if prev is not None:
        in_specs.append(pl.BlockSpec(memory_space=pl.ANY))
        args.append(prev)
        aliases = {4: 0}
    return pl.pallas_call(
        body, name=name,
        grid_spec=pltpu.PrefetchScalarGridSpec(
            num_scalar_prefetch=2, grid=(nrt,), in_specs=in_specs,
            out_specs=pl.BlockSpec((None, tr, cols), lambda i, core_ref, chip_ref: (layer, core_ref[0] * nrt + i, 0))),
        out_shape=jax.ShapeDtypeStruct((n_layers, 2 * r2, cols), F32), input_output_aliases=aliases,
        compiler_params=_params("parallel"),
    )(*args)


def _join_copy(t_ref, send_sems, recv_sems, a):
    x, y, c, _ = _mesh_place()
    r2 = t_ref.shape[1] // 2
    mine = t_ref.at[:, pl.ds(c * r2, r2), :]
    return pltpu.make_async_remote_copy(
        src_ref=mine, dst_ref=mine, send_sem=send_sems.at[a], recv_sem=recv_sems.at[a],
        device_id=(x, y, 1 - c), device_id_type=MESH)


def _join_start(name, ts, deps=()):
    n, nd = len(ts), len(deps)

    def body(*refs):
        t_refs = refs[n + nd:2 * n + nd]
        send_sems, recv_sems = refs[2 * n + nd:]
        for a in range(n):
            _join_copy(t_refs[a], send_sems, recv_sems, a).start()

    outs = pl.pallas_call(
        body, name=name, in_specs=[HBM_SPEC] * n + [ANY_SPEC] * nd, out_specs=[HBM_SPEC] * n + [SEM_SPEC, SEM_SPEC],
        out_shape=[pltpu.HBM(t.shape, t.dtype) for t in ts] + [pltpu.SemaphoreType.DMA((n,))] * 2,
        input_output_aliases={a: a for a in range(n)}, compiler_params=SPLIT_COPY,
    )(*[_in_hbm(t) for t in ts], *deps)
    return list(outs[:n]), outs[n], outs[n + 1]


def _join_wait(name, t, a, send_sems, recv_sems, after):
    def body(t_in, send_sems, recv_sems, after_ref, t_ref):
        copy = _join_copy(t_ref, send_sems, recv_sems, a)
        copy.wait_send()
        copy.wait_recv()

    return pl.pallas_call(
        body, name=name, in_specs=[HBM_SPEC, SEM_SPEC, SEM_SPEC, ANY_SPEC], out_specs=HBM_SPEC,
        out_shape=pltpu.HBM(t.shape, t.dtype), input_output_aliases={0: 0}, compiler_params=SPLIT_COPY,
    )(t, send_sems, recv_sems, after)


def _allreduce_small(p):
    n, _, w = p.shape

    def body(p_ref, o_ref, buf, send_sems, recv_sems):
        x, y, c, _ = _mesh_place()
        me = 4 * x + 2 * y + c
        buf[me] = jnp.sum(p_ref[...], axis=1)
        copies = []
        for pat in range(1, N_DEV):
            fx, fy, fc = (pat >> 2) & 1, (pat >> 1) & 1, pat & 1
            copies.append(pltpu.make_async_remote_copy(
                src_ref=buf.at[me], dst_ref=buf.at[me], send_sem=send_sems.at[pat - 1], recv_sem=recv_sems.at[pat - 1],
                device_id=(x ^ fx, y ^ fy, c ^ fc), device_id_type=MESH))
        for cp in copies:
            cp.start()
        for cp in copies:
            cp.wait()
        acc = buf[0]
        for dev in range(1, N_DEV):
            acc = acc + buf[dev]
        o_ref[...] = acc

    return pl.pallas_call(
        body, name="allreduce_small", in_specs=[pl.BlockSpec(memory_space=pltpu.VMEM)],
        out_specs=pl.BlockSpec(memory_space=pltpu.VMEM), out_shape=jax.ShapeDtypeStruct((n, w), F32),
        scratch_shapes=[pltpu.VMEM((N_DEV, n, w), F32), pltpu.SemaphoreType.DMA((N_DEV - 1,)),
                        pltpu.SemaphoreType.DMA((N_DEV - 1,))],
    )(p)


class _WeightFeed:
    def __init__(self):
        self.fulls, self.ici_send, self.ici_recv, self.d2d = [], [], [], []

    def start(self, name, fulls, after):
        started, send, recv, token = _gather_start(name, fulls, after)
        self.fulls += started
        self.ici_send += send
        self.ici_recv += recv
        self.d2d += [None] * len(fulls)
        self.token = token
        return token

    def _pass_on(self, k, after):
        if k == 0:
            after = self.token
        if k < len(self.fulls) and self.d2d[k] is None:
            self.fulls[k], send, recv = _gather_pass_on(f"gather_pass_{k}", self.fulls[k], self.ici_recv[k], after)
            self.d2d[k] = (send, recv)

    def take(self, k, after):
        self._pass_on(k, after)
        self.fulls[k] = _gather_arrive(f"gather_arrive_{k}", self.fulls[k], self.ici_send[k], *self.d2d[k], after)
        return self.fulls[k]


def _ffn_forward(tag, x, h, g_post, next_gain, feed, k):
    s, d = x.shape
    gu_w = feed.take(k, h)
    gu, a = _ffn_up(f"{tag}_up", h, gu_w)
    dn_w = feed.take(k + 1, a).reshape(-1, d)
    f = dn_w.shape[0]
    tm, tn = _tile(s, 1024), _tile(d, 512)
    y = _mm(f"{tag}_down", a, dn_w, mode="nn", grid=(s // tm, d // tn),
            a_spec=pl.BlockSpec((tm, f), lambda i, j: (i, 0)),
            b_spec=pl.BlockSpec((f, tn), lambda i, j: (0, j)),
            o_spec=pl.BlockSpec((tm, tn), lambda i, j: (i, j)),
            out_shape=jax.ShapeDtypeStruct((s, d), F32))
    x_new, h_next = _res_norm(f"{tag}_post", x, y, g_post, FFN_RESIDUAL_WEIGHT, next_gain)
    return x_new, h_next, (x, h, gu, a, y)


class _GradReduce:
    def __init__(self, core, chip, n_layers):
        self.core, self.chip, self.n_layers = core, chip, n_layers
        self.state = {}
        self.bufs = {}

    def start(self, kinds, layer, gs):
        gs, lands, send, recv, token = _swap_start(f"swap_start_{kinds[0]}_{layer}", gs)
        self.state[kinds, layer] = (gs, lands, send, recv)
        return token

    def exchange(self, kinds, layer, after):
        tag = f"{kinds[0]}_{layer}"
        gs, sibs = _swap_wait(f"swap_wait_{tag}", *self.state[kinds, layer], after)
        hs = [_add_core_halves(f"add_cores_{k}_{layer}", g, sib, self.core) for k, g, sib in zip(kinds, gs, sibs)]
        hs, lands, send, recv, token = _scatter_start(f"scatter_start_{tag}", hs)
        self.state[kinds, layer] = (hs, lands, send, recv)
        return token

    def finish(self, kinds, layer, after):
        tag = f"{kinds[0]}_{layer}"
        hs, rcvs = _scatter_wait(f"scatter_wait_{tag}", *self.state.pop((kinds, layer)), after)
        for k, h, rcv in zip(kinds, hs, rcvs):
            self.bufs[k] = _sum_chips(f"sum_chips_{k}_{layer}", h, rcv, self.core, self.chip, layer, self.n_layers,
                                      self.bufs.get(k))
        return self.bufs[kinds[-1]]


def _ffn_backward(tag, dx_new, saved, g_pre, g_post, gu_w, dn_w, red, kinds, layer, deps):
    x, h, gu, a, y = saved
    s, d = x.shape
    nb, fs = gu_w.shape[0], gu_w.shape[2]
    f = dn_w.shape[0]
    fr = f // nb
    dy, dg_post = _norm_bwd(f"{tag}_post_bwd", dx_new, y, g_post, FFN_RESIDUAL_WEIGHT, None, BF16)
    dgu = _ffn_dact(f"{tag}_dact", dy, dn_w, gu, deps)
    dgu4 = dgu.reshape(nb, s, fs)
    tn = _tile(d, 1024)
    d_wd = _mm(f"{tag}_dwd", a, dy, mode="tn", grid=(nb, d // tn),
               a_spec=pl.BlockSpec((s, fr), lambda i, j: (0, i)),
               b_spec=pl.BlockSpec((s, tn), lambda i, j: (0, j)),
               o_spec=pl.BlockSpec((None, fr, tn), lambda i, j: (i, 0, j)),
               out_shape=jax.ShapeDtypeStruct((nb, fr, d), BF16))
    tm, tw = _tile(d, 512), _tile(fs, 1408)
    nw = fs // tw
    d_wgu = _mm(f"{tag}_dwgu", h, dgu4, mode="tn", grid=(nb, nw, d // tm),
                a_spec=pl.BlockSpec((s, tm), lambda k, j, i: (0, i)),
                b_spec=pl.BlockSpec((None, s, tw), lambda k, j, i: (k, 0, j)),
                o_spec=pl.BlockSpec((None, tm, tw), lambda k, j, i: (k, i, j)),
                out_shape=jax.ShapeDtypeStruct((nb, d, fs), BF16))
    started = (red.start(kinds, layer, [d_wgu, d_wd]),)
    ts, td = _tile(s, 1024), _tile(d, 1024)
    dh = _mm(f"{tag}_dh", dgu4, gu_w, mode="nt", grid=(s // ts, d // td, nb),
             a_spec=pl.BlockSpec((None, ts, fs), lambda i, j, k: (k, i, 0)),
             b_spec=pl.BlockSpec((None, td, fs), lambda i, j, k: (k, j, 0)),
             o_spec=pl.BlockSpec((ts, td), lambda i, j, k: (i, j)),
             out_shape=jax.ShapeDtypeStruct((s, d), F32), nk=nb, acc_shape=(ts, td), deps=started)
    dx, dg_pre = _norm_bwd(f"{tag}_pre_bwd", dh, x, g_pre, 1.0, dx_new, F32)
    return dx, dg_pre, dg_post


def _mixer_forward(tag, x, h, gains, next_gain, feed, k, conv_taps, dims):
    qd, kvd, cd = dims
    s, d = x.shape
    _, g_a, g_c, g_post = gains
    win_w = feed.take(k, h)
    nb, cw = win_w.shape[0], win_w.shape[2]
    tm = _tile(s, 1024)
    z = _mm(f"{tag}_in", h, win_w, mode="nn", grid=(nb, s // tm),
            a_spec=pl.BlockSpec((tm, d), lambda j, i: (i, 0)),
            b_spec=pl.BlockSpec((None, d, cw), lambda j, i: (j, 0, 0)),
            o_spec=pl.BlockSpec((tm, cw), lambda j, i: (i, j)),
            out_shape=jax.ShapeDtypeStruct((s, nb * cw), BF16))
    a, lse = _attn_fwd(f"{tag}_attn", z, qd, kvd)
    c = _conv_fwd(f"{tag}_conv", z, conv_taps, qd + 2 * kvd, cd)
    cat = _cat_norm_fwd(f"{tag}_cat", a, c, g_a, g_c)
    wout_w = feed.take(k + 1, cat).reshape(-1, d)
    mw = qd + cd
    tn = _tile(d, 1024)
    mixed = _mm(f"{tag}_out", cat, wout_w, mode="nn", grid=(s // tm, d // tn),
                a_spec=pl.BlockSpec((tm, mw), lambda i, j: (i, 0)),
                b_spec=pl.BlockSpec((mw, tn), lambda i, j: (0, j)),
                o_spec=pl.BlockSpec((tm, tn), lambda i, j: (i, j)),
                out_shape=jax.ShapeDtypeStruct((s, d), F32))
    x_new, h_next = _res_norm(f"{tag}_post", x, mixed, g_post, 1.0, next_gain)
    return x_new, h_next, (x, h, z, a, lse, c, cat, mixed)


def _mixer_backward(tag, dx_new, saved, gains, win_w, conv_taps, wout_w, dims, red, kinds, layer, deps):
    qd, kvd, cd = dims
    x, h, z, a, lse, c, cat, mixed = saved
    s, d = x.shape
    nb, cw = win_w.shape[0], win_w.shape[2]
    g_pre, g_a, g_c, g_post = gains
    mw = qd + cd
    dmixed, dg_post = _norm_bwd(f"{tag}_post_bwd", dx_new, mixed, g_post, 1.0, None, BF16)
    tm, tn = _tile(s, 1024), _tile(mw, 1024)
    dcat = _mm(f"{tag}_dcat", dmixed, wout_w, mode="nt", grid=(s // tm, mw // tn),
               a_spec=pl.BlockSpec((tm, d), lambda i, j: (i, 0)),
               b_spec=pl.BlockSpec((tn, d), lambda i, j: (j, 0)),
               o_spec=pl.BlockSpec((tm, tn), lambda i, j: (i, j)),
               out_shape=jax.ShapeDtypeStruct((s, mw), F32), deps=deps)
    wr = mw // nb
    td = _tile(d, 1024)
    d_wout = _mm(f"{tag}_dwout", cat, dmixed, mode="tn", grid=(nb, d // td),
                 a_spec=pl.BlockSpec((s, wr), lambda i, j: (0, i)),
                 b_spec=pl.BlockSpec((s, td), lambda i, j: (0, j)),
                 o_spec=pl.BlockSpec((None, wr, td), lambda i, j: (i, 0, j)),
                 out_shape=jax.ShapeDtypeStruct((nb, wr, d), BF16))
    da, dc, dg_a, dg_c = _cat_norm_bwd(f"{tag}_cat_bwd", dcat, a, c, g_a, g_c)
    dhc, dbg, dcg, d_taps = _conv_bwd(f"{tag}_conv_bwd", z, conv_taps, dc, qd + 2 * kvd, cd)
    dq, dk, dv = _attn_bwd(f"{tag}_attn_bwd", z, a, lse, da, qd, kvd)
    dz = jnp.concatenate([dq, dk, dv, dhc, dbg, dcg], axis=1)
    th = _tile(d, 512)
    d_win = _mm(f"{tag}_dwin", h, dz, mode="tn", grid=(nb, d // th),
                a_spec=pl.BlockSpec((s, th), lambda k, i: (0, i)),
                b_spec=pl.BlockSpec((s, cw), lambda k, i: (0, k)),
                o_spec=pl.BlockSpec((None, th, cw), lambda k, i: (k, i, 0)),
                out_shape=jax.ShapeDtypeStruct((nb, d, cw), BF16))
    started = (red.start(kinds, layer, [d_win, d_wout]),)
    dh = _mm(f"{tag}_dh", dz, win_w, mode="nt", grid=(s // tm, d // td, nb),
             a_spec=pl.BlockSpec((tm, cw), lambda i, j, k: (i, k)),
             b_spec=pl.BlockSpec((None, td, cw), lambda i, j, k: (k, j, 0)),
             o_spec=pl.BlockSpec((tm, td), lambda i, j, k: (i, j)),
             out_shape=jax.ShapeDtypeStruct((s, d), F32), nk=nb, acc_shape=(tm, td), deps=started)
    dx, dg_pre = _norm_bwd(f"{tag}_pre_bwd", dh, x, g_pre, 1.0, dx_new, F32)
    return dx, d_taps, (dg_pre, dg_a, dg_c, dg_post)


def _pad_cols(v, width):
    return jnp.pad(v, ((0, 0), (0, width - v.shape[1])))


def kernel(x, ffn1_norm_pre, ffn1_w_gate_up, ffn1_w_down, ffn1_norm_post, mix_norm_pre, w_in, conv_w, attn_out_norm, conv_out_norm, w_out, mix_norm_post, ffn2_norm_pre, ffn2_w_gate_up, ffn2_w_down, ffn2_norm_post, loss_target, m_ffn1_norm_pre, m_ffn1_w_gate_up, m_ffn1_w_down, m_ffn1_norm_post, m_mix_norm_pre, m_w_in, m_conv_w, m_attn_out_norm, m_conv_out_norm, m_w_out, m_mix_norm_post, m_ffn2_norm_pre, m_ffn2_w_gate_up, m_ffn2_w_down, m_ffn2_norm_post, v_ffn1_norm_pre, v_ffn1_w_gate_up, v_ffn1_w_down, v_ffn1_norm_post, v_mix_norm_pre, v_w_in, v_conv_w, v_attn_out_norm, v_conv_out_norm, v_w_out, v_mix_norm_post, v_ffn2_norm_pre, v_ffn2_w_gate_up, v_ffn2_w_down, v_ffn2_norm_post):
    _, s, d = x.shape
    n_layers = ffn1_norm_pre.shape[0]
    qd = attn_out_norm.shape[1]
    cd = conv_out_norm.shape[1]
    kvd = qd // Q_PER_KV
    dims = (qd, kvd, cd)
    assert N_CHIPS * w_in.shape[2] == qd + 2 * kvd + 3 * cd and qd + cd == N_CHIPS * w_out.shape[1]
    assert 2 * d <= SMALL_ROWS * LANES * SUBLANES
    chip = 2 * lax.axis_index("x") + lax.axis_index("y")
    chip_arr = chip.astype(jnp.int32).reshape(1)
    core = lax.axis_index("c").astype(jnp.int32).reshape(1)
    kinds = ("gu1", "dn1", "win", "wout", "gu2", "dn2")

    big = (ffn1_w_gate_up, ffn1_w_down, w_in, w_out, ffn2_w_gate_up, ffn2_w_down)
    nk = len(kinds)
    taps_all = _gather_taps(conv_w)
    feed = _WeightFeed()
    order = [(k, w, layer) for layer in range(n_layers) for k, w in zip(kinds, big)]
    k, w, layer = order[0]
    token = feed.start("gather_start_first", [_cast_into_slot(f"cast_{k}_{layer}", w, layer, chip_arr)], taps_all)
    feed.start("gather_start_rest", [_cast_into_slot(f"cast_{k}_{layer}", w, layer, chip_arr, (token,))
                                     for k, w, layer in order[1:]], token)
    taps = jnp.transpose(taps_all, (1, 2, 0, 3)).reshape(n_layers, CONV_WIDTH, cd)
    taps = jnp.pad(taps, ((0, 0), (0, SUBLANES - CONV_WIDTH), (0, 0)))

    def gain(g, layer):
        return g[layer][None, :]

    xs = x[0]
    hs = _norm_fwd("l0_ffn1_norm", xs, gain(ffn1_norm_pre, 0))
    saved = []
    for layer in range(n_layers):
        t = f"l{layer}"
        k0 = layer * nk
        xs, hs, s1 = _ffn_forward(f"{t}_ffn1", xs, hs, gain(ffn1_norm_post, layer), gain(mix_norm_pre, layer), feed, k0)
        mix_gains = (gain(mix_norm_pre, layer), gain(attn_out_norm, layer), gain(conv_out_norm, layer), gain(mix_norm_post, layer))
        xs, hs, s2 = _mixer_forward(f"{t}_mix", xs, hs, mix_gains, gain(ffn2_norm_pre, layer), feed, k0 + 2,
                                    taps[layer], dims)
        following = gain(ffn1_norm_pre, layer + 1) if layer + 1 < n_layers else None
        xs, hs, s3 = _ffn_forward(f"{t}_ffn2", xs, hs, gain(ffn2_norm_post, layer), following, feed, k0 + 4)
        saved.append((s1, s2, s3, mix_gains))
    wts = {k: [feed.fulls[layer * nk + i] for layer in range(n_layers)] for i, k in enumerate(kinds)}
    for k in ("dn1", "wout", "dn2"):
        wts[k] = [w.reshape(-1, d) for w in wts[k]]
    dxs, loss_part = _loss_head("loss_head", xs, loss_target[0])
    loss = lax.psum(jnp.sum(loss_part), ("x", "y", "c"))

    red = _GradReduce(core, chip_arr, n_layers)
    small = [None] * n_layers
    flow = {"deps": (), "in_flight": None}

    def between(dx, group):
        after = dx
        if flow["in_flight"] is not None:
            after = red.finish(*flow["in_flight"], after)
        flow["deps"] = (red.exchange(*group, after),)
        flow["in_flight"] = group

    for layer in reversed(range(n_layers)):
        t = f"l{layer}"
        s1, s2, s3, mix_gains = saved[layer]
        dxs, p_pre2, p_post2 = _ffn_backward(
            f"{t}_ffn2", dxs, s3, gain(ffn2_norm_pre, layer), gain(ffn2_norm_post, layer),
            wts["gu2"][layer], wts["dn2"][layer], red, ("gu2", "dn2"), layer, flow["deps"])
        between(dxs, (("gu2", "dn2"), layer))
        dxs, p_taps, (p_mpre, p_a, p_c, p_mpost) = _mixer_backward(
            f"{t}_mix", dxs, s2, mix_gains, wts["win"][layer], taps[layer], wts["wout"][layer], dims,
            red, ("win", "wout"), layer, flow["deps"])
        between(dxs, (("win", "wout"), layer))
        dxs, p_pre1, p_post1 = _ffn_backward(
            f"{t}_ffn1", dxs, s1, gain(ffn1_norm_pre, layer), gain(ffn1_norm_post, layer),
            wts["gu1"][layer], wts["dn1"][layer], red, ("gu1", "dn1"), layer, flow["deps"])
        between(dxs, (("gu1", "dn1"), layer))
        tap_rows = jnp.zeros((CONV_WIDTH, SUBLANES, d), F32).at[:, 0, :cd].set(p_taps[:CONV_WIDTH])
        rows = [p_pre1, p_post1, p_mpre, jnp.concatenate([p_a, p_c], axis=1), p_mpost, p_pre2, p_post2]
        rows = jnp.concatenate([jnp.stack(rows), tap_rows], axis=0)
        small[layer] = jnp.pad(rows, ((0, SMALL_ROWS - rows.shape[0]), (0, 0), (0, 0)))
    grad_x = dxs[None]

    weights = dict(ffn1_norm_pre=ffn1_norm_pre, ffn1_w_gate_up=ffn1_w_gate_up, ffn1_w_down=ffn1_w_down, ffn1_norm_post=ffn1_norm_post, mix_norm_pre=mix_norm_pre, w_in=w_in, conv_w=conv_w, attn_out_norm=attn_out_norm, conv_out_norm=conv_out_norm, w_out=w_out, mix_norm_post=mix_norm_post, ffn2_norm_pre=ffn2_norm_pre, ffn2_w_gate_up=ffn2_w_gate_up, ffn2_w_down=ffn2_w_down, ffn2_norm_post=ffn2_norm_post)
    m_in = dict(ffn1_norm_pre=m_ffn1_norm_pre, ffn1_w_gate_up=m_ffn1_w_gate_up, ffn1_w_down=m_ffn1_w_down, ffn1_norm_post=m_ffn1_norm_post, mix_norm_pre=m_mix_norm_pre, w_in=m_w_in, conv_w=m_conv_w, attn_out_norm=m_attn_out_norm, conv_out_norm=m_conv_out_norm, w_out=m_w_out, mix_norm_post=m_mix_norm_post, ffn2_norm_pre=m_ffn2_norm_pre, ffn2_w_gate_up=m_ffn2_w_gate_up, ffn2_w_down=m_ffn2_w_down, ffn2_norm_post=m_ffn2_norm_post)
    v_in = dict(ffn1_norm_pre=v_ffn1_norm_pre, ffn1_w_gate_up=v_ffn1_w_gate_up, ffn1_w_down=v_ffn1_w_down, ffn1_norm_post=v_ffn1_norm_post, mix_norm_pre=v_mix_norm_pre, w_in=v_w_in, conv_w=v_conv_w, attn_out_norm=v_attn_out_norm, conv_out_norm=v_conv_out_norm, w_out=v_w_out, mix_norm_post=v_mix_norm_post, ffn2_norm_pre=v_ffn2_norm_pre, ffn2_w_gate_up=v_ffn2_w_gate_up, ffn2_w_down=v_ffn2_w_down, ffn2_norm_post=v_ffn2_norm_post)
    kind_name = dict(gu1="ffn1_w_gate_up", dn1="ffn1_w_down", win="w_in", wout="w_out", gu2="ffn2_w_gate_up", dn2="ffn2_w_down")
    delta, new_m, new_v, grad = {}, {}, {}, {}

    def join_and_update(name, kind_list, deps, after):
        ts, send_sems, recv_sems = _join_start(name, [red.bufs[k] for k in kind_list], deps)
        for a, k in enumerate(kind_list):
            n = kind_name[k]
            g = _join_wait(f"join_wait_{k}", ts[a], a, send_sems, recv_sems, after)
            delta[n], new_m[n], new_v[n], grad[n] = _adamw(f"adamw_{n}", weights[n], g, m_in[n], v_in[n], True)
            after = delta[n]

    early = ("wout", "win", "dn2", "gu2")
    join_and_update("join_early", early, flow["deps"], dxs)
    late, last_layer = flow["in_flight"]
    red.finish(late, last_layer, [delta[kind_name[e]] for e in early])
    join_and_update("join_late", tuple(reversed(late)), (), delta[kind_name[early[-1]]])

    small_sum = _allreduce_small(jnp.concatenate(small, axis=0)).reshape(n_layers, SMALL_ROWS, d)
    g_ffn1_pre, g_ffn1_post, g_mix_pre = small_sum[:, 0], small_sum[:, 1], small_sum[:, 2]
    g_attn_out, g_conv_out = small_sum[:, 3, :qd], small_sum[:, 3, qd:qd + cd]
    g_mix_post, g_ffn2_pre, g_ffn2_post = small_sum[:, 4], small_sum[:, 5], small_sum[:, 6]
    cc = conv_w.shape[2]
    g_conv = lax.dynamic_slice_in_dim(small_sum[:, 7:7 + CONV_WIDTH, :cd], chip * cc, cc, axis=2)

    grad.update(ffn1_norm_pre=g_ffn1_pre, ffn1_norm_post=g_ffn1_post, mix_norm_pre=g_mix_pre, conv_w=g_conv, attn_out_norm=g_attn_out, conv_out_norm=g_conv_out, mix_norm_post=g_mix_post, ffn2_norm_pre=g_ffn2_pre, ffn2_norm_post=g_ffn2_post)
    names = list(weights)

    vectors = [n for n in names if n not in kind_name.values()]

    def pack(tree):
        flat = jnp.concatenate([tree[n].reshape(-1) for n in vectors])
        return jnp.pad(flat, (0, -flat.size % (SUBLANES * LANES))).reshape(-1, LANES)

    packed = _adamw("adamw_small", pack(weights), pack(grad), pack(m_in), pack(v_in))
    offset = 0
    for n in vectors:
        size = weights[n].size
        for tree, flat in zip((delta, new_m, new_v), packed):
            tree[n] = flat.reshape(-1)[offset:offset + size].reshape(weights[n].shape)
        offset += size

    return (loss, grad_x, *[grad[n] for n in names], *[delta[n] for n in names],
            *[new_m[n] for n in names], *[new_v[n] for n in names])
```

```python
import functools

import jax
import jax.numpy as jnp
from jax import lax
from jax.experimental import pallas as pl
from jax.experimental.pallas import tpu as pltpu

F32 = jnp.float32
BF16 = jnp.bfloat16
MESH = pl.DeviceIdType.MESH

NORM_EPS = 1e-6
HEAD_DIM = 128
Q_PER_KV = 4
CONV_WIDTH = 3
FFN_RESIDUAL_WEIGHT = 0.5
DILATED_BRANCHES = ((128, 1), (512, 4), (2048, 16))
ADAM_LR = 0.001
ADAM_B1 = 0.9
ADAM_B2 = 0.999
ADAM_EPS = 1e-08
ADAM_WD = 0.01
ADAM_STEP = 10

N_CHIPS = 4
N_DEV = 8
V7X_VMEM_BYTES = 64 << 20
VMEM_LIMIT = V7X_VMEM_BYTES - (12 << 20)
SUBLANES = 8
LANES = 128
SMALL_ROWS = 16
BIG_BLOCK = 4 << 20


def _params(*sem):
    return pltpu.CompilerParams(dimension_semantics=sem, vmem_limit_bytes=VMEM_LIMIT)


def _row_tile(rows, cols, itemsize=4, budget=2 << 20):
    t = rows
    while t * cols * itemsize > budget and t % 32 == 0:
        t //= 2
    return t


def _sum_to_sublanes(v):
    r, n = v.shape
    return v.reshape(r // SUBLANES, SUBLANES, n).sum(axis=0)


_DIMS = {
    "nn": (((1,), (0,)), ((), ())),
    "nt": (((1,), (1,)), ((), ())),
    "tn": (((0,), (0,)), ((), ())),
}


ANY_SPEC = pl.BlockSpec(memory_space=pl.ANY)


def _dot(a, b, mode):
    return lax.dot_general(a, b, _DIMS[mode], preferred_element_type=F32)


def _mm(name, a, b, *, mode, grid, a_spec, b_spec, o_spec, out_shape, nk=1, acc_shape=None, deps=()):
    nd = len(deps)

    def body(a_ref, b_ref, *rest):
        o_ref, scratch = rest[nd], rest[nd + 1:]
        r = _dot(a_ref[...], b_ref[...], mode)
        if nk == 1:
            o_ref[...] = r.astype(o_ref.dtype)
        else:
            acc = scratch[0]
            k = pl.program_id(len(grid) - 1)

            @pl.when(k == 0)
            def _():
                acc[...] = r

            @pl.when(k > 0)
            def _():
                acc[...] += r

            @pl.when(k == nk - 1)
            def _():
                o_ref[...] = acc[...].astype(o_ref.dtype)

    sem = ("parallel",) * (len(grid) - (1 if nk > 1 else 0)) + (("arbitrary",) if nk > 1 else ())
    return pl.pallas_call(
        body, name=name, grid=grid, in_specs=[a_spec, b_spec] + [ANY_SPEC] * nd, out_specs=o_spec,
        out_shape=out_shape, scratch_shapes=[pltpu.VMEM(acc_shape, F32)] if nk > 1 else [],
        compiler_params=_params(*sem),
    )(a, b, *deps)


def _tile(n, want):
    if n <= want:
        return n
    best = None
    for t in range(LANES, want + 1, LANES):
        if n % t == 0:
            best = t
    assert best is not None, (n, want)
    return best


def _norm_fwd(name, x, gain):
    s, d = x.shape
    tr = _row_tile(s, d, budget=BIG_BLOCK)

    def body(x_ref, g_ref, o_ref):
        xv = x_ref[...]
        r = lax.rsqrt(jnp.mean(xv * xv, axis=-1, keepdims=True) + NORM_EPS)
        o_ref[...] = (xv * r * g_ref[...]).astype(o_ref.dtype)

    return pl.pallas_call(
        body, name=name, grid=(s // tr,),
        in_specs=[pl.BlockSpec((tr, d), lambda i: (i, 0)), pl.BlockSpec((1, d), lambda i: (0, 0))],
        out_specs=pl.BlockSpec((tr, d), lambda i: (i, 0)),
        out_shape=jax.ShapeDtypeStruct((s, d), BF16), compiler_params=_params("parallel"),
    )(x, gain)


def _res_norm(name, x, y, gain, scale, next_gain=None):
    s, d = x.shape
    tr = _row_tile(s, d, budget=BIG_BLOCK)
    with_next = next_gain is not None

    def body(x_ref, y_ref, g_ref, *rest):
        yv = y_ref[...]
        r = lax.rsqrt(jnp.mean(yv * yv, axis=-1, keepdims=True) + NORM_EPS)
        xn = x_ref[...] + scale * (yv * r * g_ref[...])
        if with_next:
            ng_ref, o_ref, h_ref = rest
            rn = lax.rsqrt(jnp.mean(xn * xn, axis=-1, keepdims=True) + NORM_EPS)
            h_ref[...] = (xn * rn * ng_ref[...]).astype(h_ref.dtype)
        else:
            o_ref, = rest
        o_ref[...] = xn

    row = pl.BlockSpec((tr, d), lambda i: (i, 0))
    vec = pl.BlockSpec((1, d), lambda i: (0, 0))
    outs = pl.pallas_call(
        body, name=name, grid=(s // tr,),
        in_specs=[row, row, vec] + ([vec] if with_next else []), out_specs=[row] * (2 if with_next else 1),
        out_shape=[jax.ShapeDtypeStruct((s, d), F32)] + ([jax.ShapeDtypeStruct((s, d), BF16)] if with_next else []),
        compiler_params=_params("parallel"),
    )(x, y, gain, *((next_gain,) if with_next else ()))
    return (outs[0], outs[1]) if with_next else (outs[0], None)


def _rms_bwd(dn, yv, gv):
    r = lax.rsqrt(jnp.mean(yv * yv, axis=-1, keepdims=True) + NORM_EPS)
    xhat = yv * r
    dxn = dn * gv
    return r * (dxn - xhat * jnp.mean(dxn * xhat, axis=-1, keepdims=True)), _sum_to_sublanes(dn * xhat)


def _accumulate(ref, part):
    @pl.when(pl.program_id(0) == 0)
    def _():
        ref[...] = part

    @pl.when(pl.program_id(0) > 0)
    def _():
        ref[...] += part


def _norm_bwd(name, dout, yin, gain, scale, resid, out_dtype, following=None):
    s, d = yin.shape
    tr = _row_tile(s, d)
    has_resid = resid is not None
    chained = following is not None

    def body(*refs):
        refs = list(refs)
        do_ref, y_ref, g_ref = refs[:3]
        del refs[:3]
        r_ref = refs.pop(0) if has_resid else None
        if chained:
            y2_ref, g2_ref = refs[:2]
            del refs[:2]
        di_ref, dg_ref = refs[:2]
        din, part = _rms_bwd(scale * do_ref[...], y_ref[...], g_ref[...])
        _accumulate(dg_ref, part)
        if has_resid:
            din = din + r_ref[...]
        di_ref[...] = din.astype(di_ref.dtype)
        if chained:
            d2_ref, dg2_ref = refs[2:]
            d2, part2 = _rms_bwd(following[2] * din, y2_ref[...], g2_ref[...])
            _accumulate(dg2_ref, part2)
            d2_ref[...] = d2.astype(d2_ref.dtype)

    row = pl.BlockSpec((tr, d), lambda i: (i, 0))
    vec = pl.BlockSpec((1, d), lambda i: (0, 0))
    acc = pl.BlockSpec((SUBLANES, d), lambda i: (0, 0))
    ins = [row, row, vec] + ([row] if has_resid else []) + ([row, vec] if chained else [])
    args = (dout, yin, gain) + ((resid,) if has_resid else ()) + (tuple(following[:2]) if chained else ())
    outs = [row, acc] + ([row, acc] if chained else [])
    shapes = [jax.ShapeDtypeStruct((s, d), out_dtype), jax.ShapeDtypeStruct((SUBLANES, d), F32)]
    if chained:
        shapes += [jax.ShapeDtypeStruct((s, d), BF16), jax.ShapeDtypeStruct((SUBLANES, d), F32)]
    return pl.pallas_call(
        body, name=name, grid=(s // tr,), in_specs=ins, out_specs=outs, out_shape=shapes,
        compiler_params=_params("arbitrary"),
    )(*args)


def _loss_head(name, y, target):
    s, d = y.shape
    tr = _row_tile(s, d)

    def body(y_ref, t_ref, dy_ref, l_ref):
        e = y_ref[...] - t_ref[...]
        dy_ref[...] = e * (1.0 / d)
        part = _sum_to_sublanes(e * e) * (0.5 / d)

        @pl.when(pl.program_id(0) == 0)
        def _():
            l_ref[...] = part

        @pl.when(pl.program_id(0) > 0)
        def _():
            l_ref[...] += part

    row = pl.BlockSpec((tr, d), lambda i: (i, 0))
    return pl.pallas_call(
        body, name=name, grid=(s // tr,), in_specs=[row, row],
        out_specs=[row, pl.BlockSpec((SUBLANES, d), lambda i: (0, 0))],
        out_shape=[jax.ShapeDtypeStruct((s, d), F32), jax.ShapeDtypeStruct((SUBLANES, d), F32)],
        compiler_params=_params("arbitrary"),
    )(y, target)


def _ffn_up(name, h, gu_w):
    s, d = h.shape
    nb, _, fs = gu_w.shape
    hb = nb // 2
    w = gu_w.reshape(2, hb, d, fs)
    tm = _tile(s, 512)
    tn = _tile(fs, 1408)
    nj = fs // tn

    def body(h_ref, w_ref, gu_ref, a_ref):
        hv = h_ref[...]
        g = _dot(hv, w_ref[0], "nn")
        u = _dot(hv, w_ref[1], "nn")
        sg = jax.nn.sigmoid(g)
        silu = g * sg
        gu_ref[0] = (u * (sg * (1.0 + g * (1.0 - sg)))).astype(gu_ref.dtype)
        gu_ref[1] = silu.astype(gu_ref.dtype)
        a_ref[...] = (silu * u).astype(a_ref.dtype)

    return pl.pallas_call(
        body, name=name, grid=(hb, nj, s // tm),
        in_specs=[pl.BlockSpec((tm, d), lambda jb, jo, i: (i, 0)),
                  pl.BlockSpec((2, None, d, tn), lambda jb, jo, i: (0, jb, 0, jo))],
        out_specs=[pl.BlockSpec((2, None, tm, tn), lambda jb, jo, i: (0, jb, i, jo)),
                   pl.BlockSpec((tm, tn), lambda jb, jo, i: (i, jb * nj + jo))],
        out_shape=[jax.ShapeDtypeStruct((2, hb, s, fs), BF16), jax.ShapeDtypeStruct((s, hb * fs), BF16)],
        compiler_params=_params("parallel", "parallel", "parallel"),
    )(h, w)


def _ffn_dact(name, dy, dn_w, gu, deps=()):
    s, d = dy.shape
    _, hb, _, fs = gu.shape
    tm = _tile(s, 512)
    tn = _tile(fs, 1408)
    nj = fs // tn

    def body(dy_ref, w_ref, gu_ref, *rest):
        o_ref = rest[-1]
        wv = w_ref[...]
        parts = 2 if tm % (2 * SUBLANES * 2) == 0 else 1
        for r in range(parts):
            rows = slice(r * (tm // parts), (r + 1) * (tm // parts))
            da = _dot(dy_ref[rows, :], wv, "nt")
            o_ref[0, rows, :] = (da * gu_ref[0, rows, :].astype(F32)).astype(o_ref.dtype)
            o_ref[1, rows, :] = (da * gu_ref[1, rows, :].astype(F32)).astype(o_ref.dtype)

    blk = pl.BlockSpec((2, None, tm, tn), lambda jb, jo, i: (0, jb, i, jo))
    return pl.pallas_call(
        body, name=name, grid=(hb, nj, s // tm),
        in_specs=[pl.BlockSpec((tm, d), lambda jb, jo, i: (i, 0)),
                  pl.BlockSpec((tn, d), lambda jb, jo, i: (jb * nj + jo, 0)),
                  blk] + [ANY_SPEC] * len(deps),
        out_specs=blk, out_shape=jax.ShapeDtypeStruct(gu.shape, BF16),
        compiler_params=_params("parallel", "parallel", "parallel"),
    )(dy, dn_w, gu, *deps)


_MASKED = -1e30


def _attn_bias(s, tq):
    nd = s // tq
    dist = (jnp.arange(nd)[:, None, None] * tq + jnp.arange(tq)[None, :, None]) - jnp.arange(tq)[None, None, :]
    mult = jnp.zeros(dist.shape, F32)
    for window, dilation in DILATED_BRANCHES:
        mult = mult + ((dist >= 0) & (dist <= window) & (dist % dilation == 0)).astype(F32)
    return jnp.where(mult > 0.0, jnp.log(jnp.maximum(mult, 1.0)), _MASKED)


def _biased(sc, bias, scale):
    tq, tk = bias.shape
    return (sc.reshape(-1, tq, tk) * scale + bias[None]).reshape(sc.shape)


def _attn_specs(s, qd, kvd, tq):
    rw = Q_PER_KV * HEAD_DIM
    qspec = pl.BlockSpec((tq, rw), lambda g, i: (i, g))
    kspec = pl.BlockSpec((s, HEAD_DIM), lambda g, i: (0, qd // HEAD_DIM + g))
    vspec = pl.BlockSpec((s, HEAD_DIM), lambda g, i: (0, (qd + kvd) // HEAD_DIM + g))
    return rw, qspec, kspec, vspec


def _attn_fwd(name, z, qd, kvd):
    s = z.shape[0]
    tq = _tile(s, 256)
    nkv = kvd // HEAD_DIM
    rw, qspec, kspec, vspec = _attn_specs(s, qd, kvd, tq)
    scale = HEAD_DIM ** -0.5

    def body(q_ref, k_ref, v_ref, b_ref, o_ref, l_ref):
        i = pl.program_id(1)
        heads = [slice(h * HEAD_DIM, (h + 1) * HEAD_DIM) for h in range(Q_PER_KV)]
        q_all = jnp.concatenate([q_ref[:, cols] for cols in heads], axis=0)

        def chunk(j, carry):
            mx, den, acc = carry
            k0 = pl.multiple_of(j * tq, tq)
            kc, vc = k_ref[pl.ds(k0, tq), :], v_ref[pl.ds(k0, tq), :]
            sc = _biased(_dot(q_all, kc, "nt"), b_ref[i - j], scale)
            mx_new = jnp.maximum(mx, jnp.max(sc, axis=-1, keepdims=True))
            alpha = jnp.exp(mx - mx_new)
            p = jnp.exp(sc - mx_new)
            return (mx_new, alpha * den + jnp.sum(p, axis=-1, keepdims=True),
                    alpha * acc + _dot(p.astype(BF16), vc, "nn"))

        rows = Q_PER_KV * tq
        init = (jnp.full((rows, 1), _MASKED, F32), jnp.zeros((rows, 1), F32), jnp.zeros((rows, HEAD_DIM), F32))
        mx, den, acc = lax.fori_loop(0, i + 1, chunk, init)
        out = acc / den
        lse = mx + jnp.log(den)
        for h, cols in enumerate(heads):
            o_ref[:, cols] = out[h * tq:(h + 1) * tq]
            l_ref[:, cols] = jnp.broadcast_to(lse[h * tq:(h + 1) * tq], (tq, HEAD_DIM))

    bias = _attn_bias(s, tq)
    return pl.pallas_call(
        body, name=name, grid=(nkv, s // tq),
        in_specs=[qspec, kspec, vspec, pl.BlockSpec(bias.shape, lambda g, i: (0, 0, 0))], out_specs=[qspec, qspec],
        out_shape=[jax.ShapeDtypeStruct((s, qd), F32), jax.ShapeDtypeStruct((s, qd), F32)],
        compiler_params=_params("parallel", "parallel"),
    )(z, z, z, bias)


def _attn_bwd(name, z, o, lse, do, qd, kvd):
    s = z.shape[0]
    tq = _tile(s, 256)
    nkv = kvd // HEAD_DIM
    nq = s // tq
    rw, qspec, kspec, vspec = _attn_specs(s, qd, kvd, tq)
    scale = HEAD_DIM ** -0.5

    def body(q_ref, k_ref, v_ref, o_ref, l_ref, do_ref, b_ref, dq_ref, dk_ref, dv_ref, dk_acc, dv_acc):
        i = pl.program_id(1)
        heads = [slice(h * HEAD_DIM, (h + 1) * HEAD_DIM) for h in range(Q_PER_KV)]

        @pl.when(i == 0)
        def _():
            dk_acc[...] = jnp.zeros_like(dk_acc)
            dv_acc[...] = jnp.zeros_like(dv_acc)

        q_all = jnp.concatenate([q_ref[:, cols] for cols in heads], axis=0)
        do_all = jnp.concatenate([do_ref[:, cols].astype(BF16) for cols in heads], axis=0)
        lse_all = jnp.concatenate([l_ref[:, cols][:, :1] for cols in heads], axis=0)
        delta_all = jnp.concatenate(
            [jnp.sum(do_ref[:, cols] * o_ref[:, cols], axis=-1, keepdims=True) for cols in heads], axis=0)

        def chunk(j, dq):
            k0 = pl.multiple_of(j * tq, tq)
            kc, vc = k_ref[pl.ds(k0, tq), :], v_ref[pl.ds(k0, tq), :]
            p = jnp.exp(_biased(_dot(q_all, kc, "nt"), b_ref[i - j], scale) - lse_all)
            ds = (p * (_dot(do_all, vc, "nt") - delta_all) * scale).astype(BF16)
            dk_acc[pl.ds(k0, tq), :] += _dot(ds, q_all, "tn")
            dv_acc[pl.ds(k0, tq), :] += _dot(p.astype(BF16), do_all, "tn")
            return dq + _dot(ds, kc, "nn")

        dq = lax.fori_loop(0, i + 1, chunk, jnp.zeros((Q_PER_KV * tq, HEAD_DIM), F32))
        for h, cols in enumerate(heads):
            dq_ref[:, cols] = dq[h * tq:(h + 1) * tq].astype(dq_ref.dtype)

        @pl.when(i == nq - 1)
        def _():
            dk_ref[...] = dk_acc[...].astype(dk_ref.dtype)
            dv_ref[...] = dv_acc[...].astype(dv_ref.dtype)

    kvout = pl.BlockSpec((s, HEAD_DIM), lambda g, i: (0, g))
    bias = _attn_bias(s, tq)
    return pl.pallas_call(
        body, name=name, grid=(nkv, nq),
        in_specs=[qspec, kspec, vspec, qspec, qspec, qspec, pl.BlockSpec(bias.shape, lambda g, i: (0, 0, 0))],
        out_specs=[qspec, kvout, kvout],
        out_shape=[jax.ShapeDtypeStruct((s, qd), BF16), jax.ShapeDtypeStruct((s, kvd), BF16),
                   jax.ShapeDtypeStruct((s, kvd), BF16)],
        scratch_shapes=[pltpu.VMEM((s, HEAD_DIM), F32), pltpu.VMEM((s, HEAD_DIM), F32)],
        compiler_params=_params("parallel", "arbitrary"),
    )(z, z, z, o, lse, do, bias)


def _shift_down(v, n):
    rolled = pltpu.roll(v, n, 0)
    t = lax.broadcasted_iota(jnp.int32, v.shape, 0)
    return jnp.where(t >= n, rolled, 0.0)


def _shift_up(v, n):
    rows = v.shape[0]
    rolled = pltpu.roll(v, rows - n, 0)
    t = lax.broadcasted_iota(jnp.int32, v.shape, 0)
    return jnp.where(t < rows - n, rolled, 0.0)


def _conv_specs(s, base, cd, tc):
    zs = [pl.BlockSpec((s, tc), functools.partial(lambda j, off: (0, off + j), off=(base + n * cd) // tc))
          for n in range(3)]
    wspec = pl.BlockSpec((SUBLANES, tc), lambda j: (0, j))
    cspec = pl.BlockSpec((s, tc), lambda j: (0, j))
    return zs, wspec, cspec


def _conv_fwd(name, z, conv_w, base, cd):
    s = z.shape[0]
    tc = _tile(cd, 256)
    zs, wspec, cspec = _conv_specs(s, base, cd, tc)

    def body(h_ref, b_ref, c_ref, w_ref, o_ref):
        u = c_ref[...].astype(F32) * h_ref[...].astype(F32)
        y = w_ref[0:1, :] * _shift_down(u, 2) + w_ref[1:2, :] * _shift_down(u, 1) + w_ref[2:3, :] * u
        o_ref[...] = b_ref[...].astype(F32) * y

    return pl.pallas_call(
        body, name=name, grid=(cd // tc,), in_specs=zs + [wspec], out_specs=cspec,
        out_shape=jax.ShapeDtypeStruct((s, cd), F32), compiler_params=_params("parallel"),
    )(z, z, z, conv_w)


def _conv_bwd(name, z, conv_w, dc, base, cd):
    s = z.shape[0]
    tc = _tile(cd, 256)
    zs, wspec, cspec = _conv_specs(s, base, cd, tc)

    def body(h_ref, b_ref, c_ref, w_ref, dc_ref, dh_ref, db_ref, dcg_ref, dw_ref):
        hv, bv, cv = h_ref[...].astype(F32), b_ref[...].astype(F32), c_ref[...].astype(F32)
        u = cv * hv
        u1, u2 = _shift_down(u, 1), _shift_down(u, 2)
        w0, w1, w2 = w_ref[0:1, :], w_ref[1:2, :], w_ref[2:3, :]
        y = w0 * u2 + w1 * u1 + w2 * u
        dcv = dc_ref[...]
        db_ref[...] = (dcv * y).astype(db_ref.dtype)
        dy = dcv * bv
        du = w2 * dy + w1 * _shift_up(dy, 1) + w0 * _shift_up(dy, 2)
        dh_ref[...] = (du * cv).astype(dh_ref.dtype)
        dcg_ref[...] = (du * hv).astype(dcg_ref.dtype)
        g0 = jnp.sum(dy * u2, axis=0, keepdims=True)
        g1 = jnp.sum(dy * u1, axis=0, keepdims=True)
        g2 = jnp.sum(dy * u, axis=0, keepdims=True)
        r = lax.broadcasted_iota(jnp.int32, (SUBLANES, tc), 0)
        dw_ref[...] = jnp.where(r == 0, g0, jnp.where(r == 1, g1, jnp.where(r == 2, g2, 0.0)))

    return pl.pallas_call(
        body, name=name, grid=(cd // tc,), in_specs=zs + [wspec, cspec],
        out_specs=[cspec, cspec, cspec, wspec],
        out_shape=[jax.ShapeDtypeStruct((s, cd), BF16)] * 3 + [jax.ShapeDtypeStruct((SUBLANES, cd), F32)],
        compiler_params=_params("parallel"),
    )(z, z, z, conv_w, dc)


def _cat_norm_fwd(name, a, c, ga, gc):
    s, qd = a.shape
    cd = c.shape[1]
    tr = _row_tile(s, qd + cd)

    def body(a_ref, c_ref, ga_ref, gc_ref, o_ref):
        av, cv = a_ref[...], c_ref[...]
        ra = lax.rsqrt(jnp.mean(av * av, axis=-1, keepdims=True) + NORM_EPS)
        rc = lax.rsqrt(jnp.mean(cv * cv, axis=-1, keepdims=True) + NORM_EPS)
        o_ref[:, :qd] = (av * ra * ga_ref[...]).astype(o_ref.dtype)
        o_ref[:, qd:] = (cv * rc * gc_ref[...]).astype(o_ref.dtype)

    return pl.pallas_call(
        body, name=name, grid=(s // tr,),
        in_specs=[pl.BlockSpec((tr, qd), lambda i: (i, 0)), pl.BlockSpec((tr, cd), lambda i: (i, 0)),
                  pl.BlockSpec((1, qd), lambda i: (0, 0)), pl.BlockSpec((1, cd), lambda i: (0, 0))],
        out_specs=pl.BlockSpec((tr, qd + cd), lambda i: (i, 0)),
        out_shape=jax.ShapeDtypeStruct((s, qd + cd), BF16), compiler_params=_params("parallel"),
    )(a, c, ga, gc)


def _cat_norm_bwd(name, dcat, a, c, ga, gc):
    s, qd = a.shape
    cd = c.shape[1]
    tr = _row_tile(s, qd + cd)

    def one(dn, yv, gv):
        r = lax.rsqrt(jnp.mean(yv * yv, axis=-1, keepdims=True) + NORM_EPS)
        xhat = yv * r
        dxn = dn * gv
        return r * (dxn - xhat * jnp.mean(dxn * xhat, axis=-1, keepdims=True)), _sum_to_sublanes(dn * xhat)

    def body(d_ref, a_ref, c_ref, ga_ref, gc_ref, da_ref, dc_ref, dga_ref, dgc_ref):
        da, pa = one(d_ref[:, :qd], a_ref[...], ga_ref[...])
        dc, pc = one(d_ref[:, qd:], c_ref[...], gc_ref[...])
        da_ref[...] = da
        dc_ref[...] = dc

        @pl.when(pl.program_id(0) == 0)
        def _():
            dga_ref[...] = pa
            dgc_ref[...] = pc

        @pl.when(pl.program_id(0) > 0)
        def _():
            dga_ref[...] += pa
            dgc_ref[...] += pc

    ra = pl.BlockSpec((tr, qd), lambda i: (i, 0))
    rc = pl.BlockSpec((tr, cd), lambda i: (i, 0))
    return pl.pallas_call(
        body, name=name, grid=(s // tr,),
        in_specs=[pl.BlockSpec((tr, qd + cd), lambda i: (i, 0)), ra, rc,
                  pl.BlockSpec((1, qd), lambda i: (0, 0)), pl.BlockSpec((1, cd), lambda i: (0, 0))],
        out_specs=[ra, rc, pl.BlockSpec((SUBLANES, qd), lambda i: (0, 0)),
                   pl.BlockSpec((SUBLANES, cd), lambda i: (0, 0))],
        out_shape=[jax.ShapeDtypeStruct((s, qd), F32), jax.ShapeDtypeStruct((s, cd), F32),
                   jax.ShapeDtypeStruct((SUBLANES, qd), F32), jax.ShapeDtypeStruct((SUBLANES, cd), F32)],
        compiler_params=_params("arbitrary"),
    )(dcat, a, c, ga, gc)


def _adamw(name, w, g, m, v, emit_grad=False):
    shape = w.shape
    cols = shape[-1]
    rows = w.size // cols
    tr = _row_tile(rows, cols, budget=3 << 19)
    bc1 = 1.0 - ADAM_B1 ** ADAM_STEP
    bc2 = 1.0 - ADAM_B2 ** ADAM_STEP
    n_out = 4 if emit_grad else 3

    def body(w_ref, g_ref, m_ref, v_ref, d_ref, nm_ref, nv_ref, *g_out):
        gv = g_ref[...]
        mv = ADAM_B1 * m_ref[...] + (1.0 - ADAM_B1) * gv
        vv = ADAM_B2 * v_ref[...] + (1.0 - ADAM_B2) * (gv * gv)
        nm_ref[...] = mv
        nv_ref[...] = vv
        d_ref[...] = -ADAM_LR * ((mv / bc1) / (jnp.sqrt(vv / bc2) + ADAM_EPS) + ADAM_WD * w_ref[...])
        for ref in g_out:
            ref[...] = gv

    row = pl.BlockSpec((tr, cols), lambda i: (i, 0))
    outs = pl.pallas_call(
        body, name=name, grid=(rows // tr,), in_specs=[row] * 4, out_specs=[row] * n_out,
        out_shape=[jax.ShapeDtypeStruct((rows, cols), F32)] * n_out, compiler_params=_params("parallel"),
    )(*(t.reshape(rows, cols) for t in (w, g, m, v)))
    return tuple(t.reshape(shape) for t in outs)


HBM_SPEC = pl.BlockSpec(memory_space=pltpu.HBM)


def _mesh_place():
    x, y, c = lax.axis_index("x"), lax.axis_index("y"), lax.axis_index("c")
    other_chips = [(1 - x, y), (x, 1 - y), (1 - x, 1 - y)]
    return x, y, c, other_chips


def _cast_into_slot(name, w, layer, chip, deps=()):
    _, r, cols = w.shape
    tr = _row_tile(r, cols, budget=BIG_BLOCK)

    def body(chip_ref, w_ref, *rest):
        o_ref = rest[-1]
        o_ref[...] = w_ref[...].astype(o_ref.dtype)

    return pl.pallas_call(
        body, name=name,
        grid_spec=pltpu.PrefetchScalarGridSpec(
            num_scalar_prefetch=1, grid=(r // tr,),
            in_specs=[pl.BlockSpec((None, tr, cols), lambda i, chip_ref: (layer, i, 0))] + [ANY_SPEC] * len(deps),
            out_specs=pl.BlockSpec((None, tr, cols), lambda i, chip_ref: (chip_ref[0], i, 0))),
        out_shape=jax.ShapeDtypeStruct((N_CHIPS, r, cols), BF16), compiler_params=_params("parallel"),
    )(chip, w, *deps)


SEM_SPEC = pl.BlockSpec(memory_space=pltpu.SEMAPHORE)
SPLIT_COPY = pltpu.CompilerParams(has_side_effects=pltpu.SideEffectType.DATAFLOW_SIDE_EFFECTING)
N_OTHER = N_CHIPS - 1
TOKEN_SPEC = pl.BlockSpec(memory_space=pltpu.VMEM)
TOKEN_SHAPE = jax.ShapeDtypeStruct((SUBLANES, LANES), F32)


def _in_hbm(arr):
    return pltpu.with_memory_space_constraint(arr, pltpu.HBM)


def _half_rows(ref, chip_idx, core):
    r2 = ref.shape[1] // 2
    return ref.at[chip_idx, pl.ds(core * r2, r2), :]


def _gather_start(name, fulls, after):
    na = len(fulls)

    def body(*refs):
        f_refs = refs[na + 1:2 * na + 1]
        send_sems, recv_sems = refs[2 * na + 1:3 * na + 1], refs[3 * na + 1:4 * na + 1]
        token = refs[4 * na + 1]
        x, y, c, chips = _mesh_place()
        for a in range(na):
            mine = _half_rows(f_refs[a], 2 * x + y, c)
            for j, (cx, cy) in enumerate(chips):
                pltpu.make_async_remote_copy(
                    src_ref=mine, dst_ref=mine, send_sem=send_sems[a].at[j], recv_sem=recv_sems[a].at[j],
                    device_id=(cx, cy, c), device_id_type=MESH).start()
        token[...] = jnp.zeros_like(token)

    outs = pl.pallas_call(
        body, name=name, in_specs=[HBM_SPEC] * na + [ANY_SPEC],
        out_specs=[HBM_SPEC] * na + [SEM_SPEC] * (2 * na) + [TOKEN_SPEC],
        out_shape=[pltpu.HBM(f.shape, f.dtype) for f in fulls] + [pltpu.SemaphoreType.DMA((N_OTHER,))] * (2 * na)
        + [TOKEN_SHAPE],
        input_output_aliases={a: a for a in range(na)}, compiler_params=SPLIT_COPY,
    )(*[_in_hbm(f) for f in fulls], after)
    return list(outs[:na]), list(outs[na:2 * na]), list(outs[2 * na:3 * na]), outs[3 * na]


def _gather_pass_on(name, full, recv_sems, after):
    def body(f_in, recv_sems, after_ref, f_ref, d2d_send, d2d_recv):
        x, y, c, chips = _mesh_place()
        for j, (cx, cy) in enumerate(chips):
            blk = _half_rows(f_ref, 2 * cx + cy, c)
            pltpu.make_async_remote_copy(
                src_ref=blk, dst_ref=blk, send_sem=d2d_send.at[j], recv_sem=recv_sems.at[j],
                device_id=(cx, cy, c), device_id_type=MESH).wait_recv()
            pltpu.make_async_remote_copy(
                src_ref=blk, dst_ref=blk, send_sem=d2d_send.at[j], recv_sem=d2d_recv.at[j],
                device_id=(x, y, 1 - c), device_id_type=MESH).start()

    return pl.pallas_call(
        body, name=name, in_specs=[HBM_SPEC, SEM_SPEC, ANY_SPEC], out_specs=[HBM_SPEC, SEM_SPEC, SEM_SPEC],
        out_shape=[pltpu.HBM(full.shape, full.dtype)] + [pltpu.SemaphoreType.DMA((N_OTHER,))] * 2,
        input_output_aliases={0: 0}, compiler_params=SPLIT_COPY,
    )(full, recv_sems, after)


def _gather_arrive(name, full, ici_send, d2d_send, d2d_recv, after):
    def body(f_in, ici_send, d2d_send, d2d_recv, after_ref, f_ref):
        x, y, c, chips = _mesh_place()
        for j, (cx, cy) in enumerate(chips):
            mine = _half_rows(f_ref, 2 * x + y, c)
            passed = _half_rows(f_ref, 2 * cx + cy, c)
            theirs = _half_rows(f_ref, 2 * cx + cy, 1 - c)
            pltpu.make_async_remote_copy(
                src_ref=mine, dst_ref=mine, send_sem=ici_send.at[j], recv_sem=d2d_recv.at[j],
                device_id=(cx, cy, c), device_id_type=MESH).wait_send()
            pltpu.make_async_remote_copy(
                src_ref=passed, dst_ref=passed, send_sem=d2d_send.at[j], recv_sem=d2d_recv.at[j],
                device_id=(x, y, 1 - c), device_id_type=MESH).wait_send()
            pltpu.make_async_remote_copy(
                src_ref=theirs, dst_ref=theirs, send_sem=d2d_send.at[j], recv_sem=d2d_recv.at[j],
                device_id=(x, y, 1 - c), device_id_type=MESH).wait_recv()

    return pl.pallas_call(
        body, name=name, in_specs=[HBM_SPEC, SEM_SPEC, SEM_SPEC, SEM_SPEC, ANY_SPEC], out_specs=HBM_SPEC,
        out_shape=pltpu.HBM(full.shape, full.dtype), input_output_aliases={0: 0}, compiler_params=SPLIT_COPY,
    )(full, ici_send, d2d_send, d2d_recv, after)


def _gather_taps(conv_w):
    def body(cw_ref, cwf_ref, send_sems, recv_sems, local_sem):
        x, y, c, chips = _mesh_place()
        k_me = 2 * x + y
        local = pltpu.make_async_copy(cw_ref, cwf_ref.at[k_me], local_sem)
        local.start()
        copies = [pltpu.make_async_remote_copy(
            src_ref=cw_ref, dst_ref=cwf_ref.at[k_me], send_sem=send_sems.at[j], recv_sem=recv_sems.at[j],
            device_id=(cx, cy, c), device_id_type=MESH) for j, (cx, cy) in enumerate(chips)]
        for cp in copies:
            cp.start()
        for j, (cx, cy) in enumerate(chips):
            pltpu.make_async_remote_copy(
                src_ref=cw_ref, dst_ref=cwf_ref.at[2 * cx + cy], send_sem=send_sems.at[j], recv_sem=recv_sems.at[j],
                device_id=(cx, cy, c), device_id_type=MESH).wait_recv()
        for cp in copies:
            cp.wait_send()
        local.wait()

    return pl.pallas_call(
        body, name="gather_taps", in_specs=[HBM_SPEC], out_specs=HBM_SPEC,
        out_shape=jax.ShapeDtypeStruct((N_CHIPS,) + conv_w.shape, conv_w.dtype),
        scratch_shapes=[pltpu.SemaphoreType.DMA((N_OTHER,))] * 2 + [pltpu.SemaphoreType.DMA],
    )(conv_w)


def _sibling_half(g_ref, c):
    r2 = g_ref.shape[1] // 2
    return g_ref.at[:, pl.ds((1 - c) * r2, r2), :]


def _swap_copy(g_ref, land_ref, send_sems, recv_sems, a):
    x, y, c, _ = _mesh_place()
    return pltpu.make_async_remote_copy(
        src_ref=_sibling_half(g_ref, c), dst_ref=land_ref, send_sem=send_sems.at[a], recv_sem=recv_sems.at[a],
        device_id=(x, y, 1 - c), device_id_type=MESH)


def _swap_start(name, gs):
    n = len(gs)

    def body(*refs):
        g_refs, land_refs = refs[n:2 * n], refs[2 * n:3 * n]
        send_sems, recv_sems, token = refs[3 * n:]
        for a in range(n):
            _swap_copy(g_refs[a], land_refs[a], send_sems, recv_sems, a).start()
        token[...] = jnp.zeros_like(token)

    outs = pl.pallas_call(
        body, name=name, in_specs=[HBM_SPEC] * n,
        out_specs=[HBM_SPEC] * (2 * n) + [SEM_SPEC, SEM_SPEC, TOKEN_SPEC],
        out_shape=[pltpu.HBM(g.shape, g.dtype) for g in gs]
        + [pltpu.HBM((g.shape[0], g.shape[1] // 2, g.shape[2]), g.dtype) for g in gs]
        + [pltpu.SemaphoreType.DMA((n,)), pltpu.SemaphoreType.DMA((n,)), TOKEN_SHAPE],
        input_output_aliases={a: a for a in range(n)}, compiler_params=SPLIT_COPY,
    )(*[_in_hbm(g) for g in gs])
    return list(outs[:n]), list(outs[n:2 * n]), outs[2 * n], outs[2 * n + 1], outs[2 * n + 2]


def _swap_wait(name, gs, lands, send_sems, recv_sems, after):
    n = len(gs)

    def body(*refs):
        send_sems, recv_sems = refs[2 * n], refs[2 * n + 1]
        g_refs, land_refs = refs[2 * n + 3:3 * n + 3], refs[3 * n + 3:]
        for a in range(n):
            copy = _swap_copy(g_refs[a], land_refs[a], send_sems, recv_sems, a)
            copy.wait_send()
            copy.wait_recv()

    outs = pl.pallas_call(
        body, name=name, in_specs=[HBM_SPEC] * (2 * n) + [SEM_SPEC, SEM_SPEC, ANY_SPEC],
        out_specs=[HBM_SPEC] * (2 * n),
        out_shape=[pltpu.HBM(t.shape, t.dtype) for t in list(gs) + list(lands)],
        input_output_aliases={a: a for a in range(2 * n)}, compiler_params=SPLIT_COPY,
    )(*gs, *lands, send_sems, recv_sems, after)
    return list(outs[:n]), list(outs[n:])


def _add_core_halves(name, g, sib, core):
    nb, r, cols = g.shape
    r2 = r // 2
    tr = _row_tile(r2, cols, itemsize=2, budget=BIG_BLOCK)
    nrt = r2 // tr

    def body(core_ref, g_ref, s_ref, o_ref):
        o_ref[...] = (g_ref[...].astype(F32) + s_ref[...].astype(F32)).astype(o_ref.dtype)

    return pl.pallas_call(
        body, name=name,
        grid_spec=pltpu.PrefetchScalarGridSpec(
            num_scalar_prefetch=1, grid=(nb, nrt),
            in_specs=[pl.BlockSpec((None, tr, cols), lambda k, i, core_ref: (k, core_ref[0] * nrt + i, 0)),
                      pl.BlockSpec((None, tr, cols), lambda k, i, core_ref: (k, i, 0))],
            out_specs=pl.BlockSpec((None, tr, cols), lambda k, i, core_ref: (k, i, 0))),
        out_shape=jax.ShapeDtypeStruct((nb, r2, cols), BF16), compiler_params=_params("parallel", "parallel"),
    )(core, g, sib)


def _scatter_copies(h_refs, land_refs, send_sems, recv_sems):
    x, y, c, chips = _mesh_place()
    return [pltpu.make_async_remote_copy(
        src_ref=h_ref.at[2 * cx + cy], dst_ref=land_ref.at[j],
        send_sem=send_sems.at[a * N_OTHER + j], recv_sem=recv_sems.at[a * N_OTHER + j],
        device_id=(cx, cy, c), device_id_type=MESH)
        for a, (h_ref, land_ref) in enumerate(zip(h_refs, land_refs)) for j, (cx, cy) in enumerate(chips)]


def _scatter_start(name, hs):
    n = len(hs)

    def body(*refs):
        h_refs, land_refs = refs[n:2 * n], refs[2 * n:3 * n]
        send_sems, recv_sems, token = refs[3 * n:]
        for copy in _scatter_copies(h_refs, land_refs, send_sems, recv_sems):
            copy.start()
        token[...] = jnp.zeros_like(token)

    outs = pl.pallas_call(
        body, name=name, in_specs=[HBM_SPEC] * n,
        out_specs=[HBM_SPEC] * (2 * n) + [SEM_SPEC, SEM_SPEC, TOKEN_SPEC],
        out_shape=[pltpu.HBM(h.shape, h.dtype) for h in hs]
        + [pltpu.HBM((N_OTHER,) + h.shape[1:], h.dtype) for h in hs]
        + [pltpu.SemaphoreType.DMA((n * N_OTHER,)), pltpu.SemaphoreType.DMA((n * N_OTHER,)), TOKEN_SHAPE],
        input_output_aliases={a: a for a in range(n)}, compiler_params=SPLIT_COPY,
    )(*[_in_hbm(h) for h in hs])
    return list(outs[:n]), list(outs[n:2 * n]), outs[2 * n], outs[2 * n + 1], outs[2 * n + 2]


def _scatter_wait(name, hs, lands, send_sems, recv_sems, after):
    afters = tuple(after) if isinstance(after, (tuple, list)) else (after,)
    n = len(hs)

    def body(*refs):
        send_sems, recv_sems = refs[2 * n], refs[2 * n + 1]
        h_refs, land_refs = refs[-2 * n:-n], refs[-n:]
        for copy in _scatter_copies(h_refs, land_refs, send_sems, recv_sems):
            copy.wait_send()
            copy.wait_recv()

    outs = pl.pallas_call(
        body, name=name, in_specs=[HBM_SPEC] * (2 * n) + [SEM_SPEC, SEM_SPEC] + [ANY_SPEC] * len(afters),
        out_specs=[HBM_SPEC] * (2 * n),
        out_shape=[pltpu.HBM(t.shape, t.dtype) for t in list(hs) + list(lands)],
        input_output_aliases={a: a for a in range(2 * n)}, compiler_params=SPLIT_COPY,
    )(*hs, *lands, send_sems, recv_sems, *afters)
    return list(outs[:n]), list(outs[n:])


def _sum_chips(name, hs, rcv, core, chip, layer, n_layers, prev):
    _, r2, cols = hs.shape
    tr = _row_tile(r2, cols, budget=BIG_BLOCK)
    nrt = r2 // tr

    def body(core_ref, chip_ref, h_ref, r_ref, *rest):
        o_ref = rest[-1]
        acc = h_ref[...].astype(F32)
        for j in range(N_CHIPS - 1):
            acc = acc + r_ref[j].astype(F32)
        o_ref[...] = acc

    in_specs = [pl.BlockSpec((None, tr, cols), lambda i, core_ref, chip_ref: (chip_ref[0], i, 0)),
                pl.BlockSpec((N_CHIPS - 1, tr, cols), lambda i, core_ref, chip_ref: (0, i, 0))]
    args = [core, chip, hs, rcv]
    aliases = {}
    if prev is not None:
        in_specs.append(pl.BlockSpec(memory_space=pl.ANY))
        args.append(prev)
        aliases = {4: 0}
    return pl.pallas_call(
        body, name=name,
        grid_spec=pltpu.PrefetchScalarGridSpec(
            num_scalar_prefetch=2, grid=(nrt,), in_specs=in_specs,
            out_specs=pl.BlockSpec((None, tr, cols), lambda i, core_ref, chip_ref: (layer, core_ref[0] * nrt + i, 0))),
        out_shape=jax.ShapeDtypeStruct((n_layers, 2 * r2, cols), F32), input_output_aliases=aliases,
        compiler_params=_params("parallel"),
    )(*args)


def _join_copy(t_ref, send_sems, recv_sems, a):
    x, y, c, _ = _mesh_place()
    r2 = t_ref.shape[1] // 2
    mine = t_ref.at[:, pl.ds(c * r2, r2), :]
    return pltpu.make_async_remote_copy(
        src_ref=mine, dst_ref=mine, send_sem=send_sems.at[a], recv_sem=recv_sems.at[a],
        device_id=(x, y, 1 - c), device_id_type=MESH)


def _join_start(name, ts, deps=()):
    n, nd = len(ts), len(deps)

    def body(*refs):
        t_refs = refs[n + nd:2 * n + nd]
        send_sems, recv_sems = refs[2 * n + nd:]
        for a in range(n):
            _join_copy(t_refs[a], send_sems, recv_sems, a).start()

    outs = pl.pallas_call(
        body, name=name, in_specs=[HBM_SPEC] * n + [ANY_SPEC] * nd, out_specs=[HBM_SPEC] * n + [SEM_SPEC, SEM_SPEC],
        out_shape=[pltpu.HBM(t.shape, t.dtype) for t in ts] + [pltpu.SemaphoreType.DMA((n,))] * 2,
        input_output_aliases={a: a for a in range(n)}, compiler_params=SPLIT_COPY,
    )(*[_in_hbm(t) for t in ts], *deps)
    return list(outs[:n]), outs[n], outs[n + 1]


def _join_wait(name, t, a, send_sems, recv_sems, after):
    def body(t_in, send_sems, recv_sems, after_ref, t_ref):
        copy = _join_copy(t_ref, send_sems, recv_sems, a)
        copy.wait_send()
        copy.wait_recv()

    return pl.pallas_call(
        body, name=name, in_specs=[HBM_SPEC, SEM_SPEC, SEM_SPEC, ANY_SPEC], out_specs=HBM_SPEC,
        out_shape=pltpu.HBM(t.shape, t.dtype), input_output_aliases={0: 0}, compiler_params=SPLIT_COPY,
    )(t, send_sems, recv_sems, after)


def _allreduce_small(p):
    n, _, w = p.shape

    def body(p_ref, o_ref, buf, send_sems, recv_sems):
        x, y, c, _ = _mesh_place()
        me = 4 * x + 2 * y + c
        buf[me] = jnp.sum(p_ref[...], axis=1)
        copies = []
        for pat in range(1, N_DEV):
            fx, fy, fc = (pat >> 2) & 1, (pat >> 1) & 1, pat & 1
            copies.append(pltpu.make_async_remote_copy(
                src_ref=buf.at[me], dst_ref=buf.at[me], send_sem=send_sems.at[pat - 1], recv_sem=recv_sems.at[pat - 1],
                device_id=(x ^ fx, y ^ fy, c ^ fc), device_id_type=MESH))
        for cp in copies:
            cp.start()
        for cp in copies:
            cp.wait()
        acc = buf[0]
        for dev in range(1, N_DEV):
            acc = acc + buf[dev]
        o_ref[...] = acc

    return pl.pallas_call(
        body, name="allreduce_small", in_specs=[pl.BlockSpec(memory_space=pltpu.VMEM)],
        out_specs=pl.BlockSpec(memory_space=pltpu.VMEM), out_shape=jax.ShapeDtypeStruct((n, w), F32),
        scratch_shapes=[pltpu.VMEM((N_DEV, n, w), F32), pltpu.SemaphoreType.DMA((N_DEV - 1,)),
                        pltpu.SemaphoreType.DMA((N_DEV - 1,))],
    )(p)


class _WeightFeed:
    def __init__(self):
        self.fulls, self.ici_send, self.ici_recv, self.d2d = [], [], [], []

    def start(self, name, fulls, after):
        started, send, recv, token = _gather_start(name, fulls, after)
        self.fulls += started
        self.ici_send += send
        self.ici_recv += recv
        self.d2d += [None] * len(fulls)
        self.token = token
        return token

    def _pass_on(self, k, after):
        if k == 0:
            after = self.token
        if k < len(self.fulls) and self.d2d[k] is None:
            self.fulls[k], send, recv = _gather_pass_on(f"gather_pass_{k}", self.fulls[k], self.ici_recv[k], after)
            self.d2d[k] = (send, recv)

    def take(self, k, after):
        self._pass_on(k, after)
        self.fulls[k] = _gather_arrive(f"gather_arrive_{k}", self.fulls[k], self.ici_send[k], *self.d2d[k], after)
        return self.fulls[k]


def _ffn_forward(tag, x, h, g_post, next_gain, feed, k):
    s, d = x.shape
    gu_w = feed.take(k, h)
    gu, a = _ffn_up(f"{tag}_up", h, gu_w)
    dn_w = feed.take(k + 1, a).reshape(-1, d)
    f = dn_w.shape[0]
    tm, tn = _tile(s, 1024), _tile(d, 512)
    y = _mm(f"{tag}_down", a, dn_w, mode="nn", grid=(s // tm, d // tn),
            a_spec=pl.BlockSpec((tm, f), lambda i, j: (i, 0)),
            b_spec=pl.BlockSpec((f, tn), lambda i, j: (0, j)),
            o_spec=pl.BlockSpec((tm, tn), lambda i, j: (i, j)),
            out_shape=jax.ShapeDtypeStruct((s, d), F32))
    x_new, h_next = _res_norm(f"{tag}_post", x, y, g_post, FFN_RESIDUAL_WEIGHT, next_gain)
    return x_new, h_next, (x, h, gu, a, y)


class _GradReduce:
    def __init__(self, core, chip, n_layers):
        self.core, self.chip, self.n_layers = core, chip, n_layers
        self.state = {}
        self.bufs = {}

    def start(self, kinds, layer, gs):
        gs, lands, send, recv, token = _swap_start(f"swap_start_{kinds[0]}_{layer}", gs)
        self.state[kinds, layer] = (gs, lands, send, recv)
        return token

    def exchange(self, kinds, layer, after):
        tag = f"{kinds[0]}_{layer}"
        gs, sibs = _swap_wait(f"swap_wait_{tag}", *self.state[kinds, layer], after)
        hs = [_add_core_halves(f"add_cores_{k}_{layer}", g, sib, self.core) for k, g, sib in zip(kinds, gs, sibs)]
        hs, lands, send, recv, token = _scatter_start(f"scatter_start_{tag}", hs)
        self.state[kinds, layer] = (hs, lands, send, recv)
        return token

    def finish(self, kinds, layer, after):
        tag = f"{kinds[0]}_{layer}"
        hs, rcvs = _scatter_wait(f"scatter_wait_{tag}", *self.state.pop((kinds, layer)), after)
        for k, h, rcv in zip(kinds, hs, rcvs):
            self.bufs[k] = _sum_chips(f"sum_chips_{k}_{layer}", h, rcv, self.core, self.chip, layer, self.n_layers,
                                      self.bufs.get(k))
        return self.bufs[kinds[-1]]


def _ffn_backward(tag, dx_new, saved, g_pre, g_post, gu_w, dn_w, red, kinds, layer, deps, head, following):
    x, h, gu, a, y = saved
    s, d = x.shape
    nb, fs = gu_w.shape[0], gu_w.shape[2]
    f = dn_w.shape[0]
    fr = f // nb
    dy, dg_post = head or _norm_bwd(f"{tag}_post_bwd", dx_new, y, g_post, FFN_RESIDUAL_WEIGHT, None, BF16)
    dgu = _ffn_dact(f"{tag}_dact", dy, dn_w, gu, deps)
    dgu4 = dgu.reshape(nb, s, fs)
    tn = _tile(d, 1024)
    d_wd = _mm(f"{tag}_dwd", a, dy, mode="tn", grid=(nb, d // tn),
               a_spec=pl.BlockSpec((s, fr), lambda i, j: (0, i)),
               b_spec=pl.BlockSpec((s, tn), lambda i, j: (0, j)),
               o_spec=pl.BlockSpec((None, fr, tn), lambda i, j: (i, 0, j)),
               out_shape=jax.ShapeDtypeStruct((nb, fr, d), BF16))
    tm, tw = _tile(d, 512), _tile(fs, 1408)
    nw = fs // tw
    d_wgu = _mm(f"{tag}_dwgu", h, dgu4, mode="tn", grid=(nb, nw, d // tm),
                a_spec=pl.BlockSpec((s, tm), lambda k, j, i: (0, i)),
                b_spec=pl.BlockSpec((None, s, tw), lambda k, j, i: (k, 0, j)),
                o_spec=pl.BlockSpec((None, tm, tw), lambda k, j, i: (k, i, j)),
                out_shape=jax.ShapeDtypeStruct((nb, d, fs), BF16))
    started = (red.start(kinds, layer, [d_wgu, d_wd]),)
    ts, td = _tile(s, 1024), _tile(d, 1024)
    dh = _mm(f"{tag}_dh", dgu4, gu_w, mode="nt", grid=(s // ts, d // td, nb),
             a_spec=pl.BlockSpec((None, ts, fs), lambda i, j, k: (k, i, 0)),
             b_spec=pl.BlockSpec((None, td, fs), lambda i, j, k: (k, j, 0)),
             o_spec=pl.BlockSpec((ts, td), lambda i, j, k: (i, j)),
             out_shape=jax.ShapeDtypeStruct((s, d), F32), nk=nb, acc_shape=(ts, td), deps=started)
    dx, dg_pre, *next_head = _norm_bwd(f"{tag}_pre_bwd", dh, x, g_pre, 1.0, dx_new, F32, following)
    return dx, dg_pre, dg_post, tuple(next_head) or None


def _mixer_forward(tag, x, h, gains, next_gain, feed, k, conv_taps, dims):
    qd, kvd, cd = dims
    s, d = x.shape
    _, g_a, g_c, g_post = gains
    win_w = feed.take(k, h)
    nb, cw = win_w.shape[0], win_w.shape[2]
    tm = _tile(s, 1024)
    z = _mm(f"{tag}_in", h, win_w, mode="nn", grid=(nb, s // tm),
            a_spec=pl.BlockSpec((tm, d), lambda j, i: (i, 0)),
            b_spec=pl.BlockSpec((None, d, cw), lambda j, i: (j, 0, 0)),
            o_spec=pl.BlockSpec((tm, cw), lambda j, i: (i, j)),
            out_shape=jax.ShapeDtypeStruct((s, nb * cw), BF16))
    a, lse = _attn_fwd(f"{tag}_attn", z, qd, kvd)
    c = _conv_fwd(f"{tag}_conv", z, conv_taps, qd + 2 * kvd, cd)
    cat = _cat_norm_fwd(f"{tag}_cat", a, c, g_a, g_c)
    wout_w = feed.take(k + 1, cat).reshape(-1, d)
    mw = qd + cd
    tn = _tile(d, 1024)
    mixed = _mm(f"{tag}_out", cat, wout_w, mode="nn", grid=(s // tm, d // tn),
                a_spec=pl.BlockSpec((tm, mw), lambda i, j: (i, 0)),
                b_spec=pl.BlockSpec((mw, tn), lambda i, j: (0, j)),
                o_spec=pl.BlockSpec((tm, tn), lambda i, j: (i, j)),
                out_shape=jax.ShapeDtypeStruct((s, d), F32))
    x_new, h_next = _res_norm(f"{tag}_post", x, mixed, g_post, 1.0, next_gain)
    return x_new, h_next, (x, h, z, a, lse, c, cat, mixed)


def _mixer_backward(tag, dx_new, saved, gains, win_w, conv_taps, wout_w, dims, red, kinds, layer, deps, head,
                    following):
    qd, kvd, cd = dims
    x, h, z, a, lse, c, cat, mixed = saved
    s, d = x.shape
    nb, cw = win_w.shape[0], win_w.shape[2]
    g_pre, g_a, g_c, g_post = gains
    mw = qd + cd
    dmixed, dg_post = head or _norm_bwd(f"{tag}_post_bwd", dx_new, mixed, g_post, 1.0, None, BF16)
    tm, tn = _tile(s, 1024), _tile(mw, 1024)
    dcat = _mm(f"{tag}_dcat", dmixed, wout_w, mode="nt", grid=(s // tm, mw // tn),
               a_spec=pl.BlockSpec((tm, d), lambda i, j: (i, 0)),
               b_spec=pl.BlockSpec((tn, d), lambda i, j: (j, 0)),
               o_spec=pl.BlockSpec((tm, tn), lambda i, j: (i, j)),
               out_shape=jax.ShapeDtypeStruct((s, mw), F32), deps=deps)
    wr = mw // nb
    td = _tile(d, 1024)
    d_wout = _mm(f"{tag}_dwout", cat, dmixed, mode="tn", grid=(nb, d // td),
                 a_spec=pl.BlockSpec((s, wr), lambda i, j: (0, i)),
                 b_spec=pl.BlockSpec((s, td), lambda i, j: (0, j)),
                 o_spec=pl.BlockSpec((None, wr, td), lambda i, j: (i, 0, j)),
                 out_shape=jax.ShapeDtypeStruct((nb, wr, d), BF16))
    da, dc, dg_a, dg_c = _cat_norm_bwd(f"{tag}_cat_bwd", dcat, a, c, g_a, g_c)
    dhc, dbg, dcg, d_taps = _conv_bwd(f"{tag}_conv_bwd", z, conv_taps, dc, qd + 2 * kvd, cd)
    dq, dk, dv = _attn_bwd(f"{tag}_attn_bwd", z, a, lse, da, qd, kvd)
    dz = jnp.concatenate([dq, dk, dv, dhc, dbg, dcg], axis=1)
    th = _tile(d, 512)
    d_win = _mm(f"{tag}_dwin", h, dz, mode="tn", grid=(nb, d // th),
                a_spec=pl.BlockSpec((s, th), lambda k, i: (0, i)),
                b_spec=pl.BlockSpec((s, cw), lambda k, i: (0, k)),
                o_spec=pl.BlockSpec((None, th, cw), lambda k, i: (k, i, 0)),
                out_shape=jax.ShapeDtypeStruct((nb, d, cw), BF16))
    started = (red.start(kinds, layer, [d_win, d_wout]),)
    dh = _mm(f"{tag}_dh", dz, win_w, mode="nt", grid=(s // tm, d // td, nb),
             a_spec=pl.BlockSpec((tm, cw), lambda i, j, k: (i, k)),
             b_spec=pl.BlockSpec((None, td, cw), lambda i, j, k: (k, j, 0)),
             o_spec=pl.BlockSpec((tm, td), lambda i, j, k: (i, j)),
             out_shape=jax.ShapeDtypeStruct((s, d), F32), nk=nb, acc_shape=(tm, td), deps=started)
    dx, dg_pre, *next_head = _norm_bwd(f"{tag}_pre_bwd", dh, x, g_pre, 1.0, dx_new, F32, following)
    return dx, d_taps, (dg_pre, dg_a, dg_c, dg_post), tuple(next_head) or None


def _pad_cols(v, width):
    return jnp.pad(v, ((0, 0), (0, width - v.shape[1])))


def kernel(x, ffn1_norm_pre, ffn1_w_gate_up, ffn1_w_down, ffn1_norm_post, mix_norm_pre, w_in, conv_w, attn_out_norm, conv_out_norm, w_out, mix_norm_post, ffn2_norm_pre, ffn2_w_gate_up, ffn2_w_down, ffn2_norm_post, loss_target, m_ffn1_norm_pre, m_ffn1_w_gate_up, m_ffn1_w_down, m_ffn1_norm_post, m_mix_norm_pre, m_w_in, m_conv_w, m_attn_out_norm, m_conv_out_norm, m_w_out, m_mix_norm_post, m_ffn2_norm_pre, m_ffn2_w_gate_up, m_ffn2_w_down, m_ffn2_norm_post, v_ffn1_norm_pre, v_ffn1_w_gate_up, v_ffn1_w_down, v_ffn1_norm_post, v_mix_norm_pre, v_w_in, v_conv_w, v_attn_out_norm, v_conv_out_norm, v_w_out, v_mix_norm_post, v_ffn2_norm_pre, v_ffn2_w_gate_up, v_ffn2_w_down, v_ffn2_norm_post):
    _, s, d = x.shape
    n_layers = ffn1_norm_pre.shape[0]
    qd = attn_out_norm.shape[1]
    cd = conv_out_norm.shape[1]
    kvd = qd // Q_PER_KV
    dims = (qd, kvd, cd)
    assert N_CHIPS * w_in.shape[2] == qd + 2 * kvd + 3 * cd and qd + cd == N_CHIPS * w_out.shape[1]
    assert 2 * d <= SMALL_ROWS * LANES * SUBLANES
    chip = 2 * lax.axis_index("x") + lax.axis_index("y")
    chip_arr = chip.astype(jnp.int32).reshape(1)
    core = lax.axis_index("c").astype(jnp.int32).reshape(1)
    kinds = ("gu1", "dn1", "win", "wout", "gu2", "dn2")

    big = (ffn1_w_gate_up, ffn1_w_down, w_in, w_out, ffn2_w_gate_up, ffn2_w_down)
    nk = len(kinds)
    taps_all = _gather_taps(conv_w)
    feed = _WeightFeed()
    order = [(k, w, layer) for layer in range(n_layers) for k, w in zip(kinds, big)]
    k, w, layer = order[0]
    token = feed.start("gather_start_first", [_cast_into_slot(f"cast_{k}_{layer}", w, layer, chip_arr)], taps_all)
    feed.start("gather_start_rest", [_cast_into_slot(f"cast_{k}_{layer}", w, layer, chip_arr, (token,))
                                     for k, w, layer in order[1:]], token)
    taps = jnp.transpose(taps_all, (1, 2, 0, 3)).reshape(n_layers, CONV_WIDTH, cd)
    taps = jnp.pad(taps, ((0, 0), (0, SUBLANES - CONV_WIDTH), (0, 0)))

    def gain(g, layer):
        return g[layer][None, :]

    xs = x[0]
    hs = _norm_fwd("l0_ffn1_norm", xs, gain(ffn1_norm_pre, 0))
    saved = []
    for layer in range(n_layers):
        t = f"l{layer}"
        k0 = layer * nk
        xs, hs, s1 = _ffn_forward(f"{t}_ffn1", xs, hs, gain(ffn1_norm_post, layer), gain(mix_norm_pre, layer), feed, k0)
        mix_gains = (gain(mix_norm_pre, layer), gain(attn_out_norm, layer), gain(conv_out_norm, layer), gain(mix_norm_post, layer))
        xs, hs, s2 = _mixer_forward(f"{t}_mix", xs, hs, mix_gains, gain(ffn2_norm_pre, layer), feed, k0 + 2,
                                    taps[layer], dims)
        following = gain(ffn1_norm_pre, layer + 1) if layer + 1 < n_layers else None
        xs, hs, s3 = _ffn_forward(f"{t}_ffn2", xs, hs, gain(ffn2_norm_post, layer), following, feed, k0 + 4)
        saved.append((s1, s2, s3, mix_gains))
    wts = {k: [feed.fulls[layer * nk + i] for layer in range(n_layers)] for i, k in enumerate(kinds)}
    for k in ("dn1", "wout", "dn2"):
        wts[k] = [w.reshape(-1, d) for w in wts[k]]
    dxs, loss_part = _loss_head("loss_head", xs, loss_target[0])
    loss = lax.psum(jnp.sum(loss_part), ("x", "y", "c"))

    red = _GradReduce(core, chip_arr, n_layers)
    small = [None] * n_layers
    flow = {"deps": (), "in_flight": None}

    def between(dx, group):
        after = dx
        if flow["in_flight"] is not None:
            after = red.finish(*flow["in_flight"], after)
        flow["deps"] = (red.exchange(*group, after),)
        flow["in_flight"] = group

    head = None
    for layer in reversed(range(n_layers)):
        t = f"l{layer}"
        s1, s2, s3, mix_gains = saved[layer]
        after_ffn2 = (s2[7], mix_gains[3], 1.0)
        after_mix = (s1[4], gain(ffn1_norm_post, layer), FFN_RESIDUAL_WEIGHT)
        after_ffn1 = ((saved[layer - 1][2][4], gain(ffn2_norm_post, layer - 1), FFN_RESIDUAL_WEIGHT)
                      if layer > 0 else None)
        dxs, p_pre2, p_post2, head = _ffn_backward(
            f"{t}_ffn2", dxs, s3, gain(ffn2_norm_pre, layer), gain(ffn2_norm_post, layer),
            wts["gu2"][layer], wts["dn2"][layer], red, ("gu2", "dn2"), layer, flow["deps"], head, after_ffn2)
        between(dxs, (("gu2", "dn2"), layer))
        dxs, p_taps, (p_mpre, p_a, p_c, p_mpost), head = _mixer_backward(
            f"{t}_mix", dxs, s2, mix_gains, wts["win"][layer], taps[layer], wts["wout"][layer], dims,
            red, ("win", "wout"), layer, flow["deps"], head, after_mix)
        between(dxs, (("win", "wout"), layer))
        dxs, p_pre1, p_post1, head = _ffn_backward(
            f"{t}_ffn1", dxs, s1, gain(ffn1_norm_pre, layer), gain(ffn1_norm_post, layer),
            wts["gu1"][layer], wts["dn1"][layer], red, ("gu1", "dn1"), layer, flow["deps"], head, after_ffn1)
        between(dxs, (("gu1", "dn1"), layer))
        tap_rows = jnp.zeros((CONV_WIDTH, SUBLANES, d), F32).at[:, 0, :cd].set(p_taps[:CONV_WIDTH])
        rows = [p_pre1, p_post1, p_mpre, jnp.concatenate([p_a, p_c], axis=1), p_mpost, p_pre2, p_post2]
        rows = jnp.concatenate([jnp.stack(rows), tap_rows], axis=0)
        small[layer] = jnp.pad(rows, ((0, SMALL_ROWS - rows.shape[0]), (0, 0), (0, 0)))
    grad_x = dxs[None]

    weights = dict(ffn1_norm_pre=ffn1_norm_pre, ffn1_w_gate_up=ffn1_w_gate_up, ffn1_w_down=ffn1_w_down, ffn1_norm_post=ffn1_norm_post, mix_norm_pre=mix_norm_pre, w_in=w_in, conv_w=conv_w, attn_out_norm=attn_out_norm, conv_out_norm=conv_out_norm, w_out=w_out, mix_norm_post=mix_norm_post, ffn2_norm_pre=ffn2_norm_pre, ffn2_w_gate_up=ffn2_w_gate_up, ffn2_w_down=ffn2_w_down, ffn2_norm_post=ffn2_norm_post)
    m_in = dict(ffn1_norm_pre=m_ffn1_norm_pre, ffn1_w_gate_up=m_ffn1_w_gate_up, ffn1_w_down=m_ffn1_w_down, ffn1_norm_post=m_ffn1_norm_post, mix_norm_pre=m_mix_norm_pre, w_in=m_w_in, conv_w=m_conv_w, attn_out_norm=m_attn_out_norm, conv_out_norm=m_conv_out_norm, w_out=m_w_out, mix_norm_post=m_mix_norm_post, ffn2_norm_pre=m_ffn2_norm_pre, ffn2_w_gate_up=m_ffn2_w_gate_up, ffn2_w_down=m_ffn2_w_down, ffn2_norm_post=m_ffn2_norm_post)
    v_in = dict(ffn1_norm_pre=v_ffn1_norm_pre, ffn1_w_gate_up=v_ffn1_w_gate_up, ffn1_w_down=v_ffn1_w_down, ffn1_norm_post=v_ffn1_norm_post, mix_norm_pre=v_mix_norm_pre, w_in=v_w_in, conv_w=v_conv_w, attn_out_norm=v_attn_out_norm, conv_out_norm=v_conv_out_norm, w_out=v_w_out, mix_norm_post=v_mix_norm_post, ffn2_norm_pre=v_ffn2_norm_pre, ffn2_w_gate_up=v_ffn2_w_gate_up, ffn2_w_down=v_ffn2_w_down, ffn2_norm_post=v_ffn2_norm_post)
    kind_name = dict(gu1="ffn1_w_gate_up", dn1="ffn1_w_down", win="w_in", wout="w_out", gu2="ffn2_w_gate_up", dn2="ffn2_w_down")
    delta, new_m, new_v, grad = {}, {}, {}, {}

    def join_and_update(name, kind_list, deps, after):
        ts, send_sems, recv_sems = _join_start(name, [red.bufs[k] for k in kind_list], deps)
        for a, k in enumerate(kind_list):
            n = kind_name[k]
            g = _join_wait(f"join_wait_{k}", ts[a], a, send_sems, recv_sems, after)
            delta[n], new_m[n], new_v[n], grad[n] = _adamw(f"adamw_{n}", weights[n], g, m_in[n], v_in[n], True)
            after = delta[n]

    early = ("wout", "win", "dn2", "gu2")
    join_and_update("join_early", early, flow["deps"], dxs)
    late, last_layer = flow["in_flight"]
    red.finish(late, last_layer, [delta[kind_name[e]] for e in early])
    join_and_update("join_late", tuple(reversed(late)), (), delta[kind_name[early[-1]]])

    small_sum = _allreduce_small(jnp.concatenate(small, axis=0)).reshape(n_layers, SMALL_ROWS, d)
    g_ffn1_pre, g_ffn1_post, g_mix_pre = small_sum[:, 0], small_sum[:, 1], small_sum[:, 2]
    g_attn_out, g_conv_out = small_sum[:, 3, :qd], small_sum[:, 3, qd:qd + cd]
    g_mix_post, g_ffn2_pre, g_ffn2_post = small_sum[:, 4], small_sum[:, 5], small_sum[:, 6]
    cc = conv_w.shape[2]
    g_conv = lax.dynamic_slice_in_dim(small_sum[:, 7:7 + CONV_WIDTH, :cd], chip * cc, cc, axis=2)

    grad.update(ffn1_norm_pre=g_ffn1_pre, ffn1_norm_post=g_ffn1_post, mix_norm_pre=g_mix_pre, conv_w=g_conv, attn_out_norm=g_attn_out, conv_out_norm=g_conv_out, mix_norm_post=g_mix_post, ffn2_norm_pre=g_ffn2_pre, ffn2_norm_post=g_ffn2_post)
    names = list(weights)

    vectors = [n for n in names if n not in kind_name.values()]

    def pack(tree):
        flat = jnp.concatenate([tree[n].reshape(-1) for n in vectors])
        return jnp.pad(flat, (0, -flat.size % (SUBLANES * LANES))).reshape(-1, LANES)

    packed = _adamw("adamw_small", pack(weights), pack(grad), pack(m_in), pack(v_in))
    offset = 0
    for n in vectors:
        size = weights[n].size
        for tree, flat in zip((delta, new_m, new_v), packed):
            tree[n] = flat.reshape(-1)[offset:offset + size].reshape(weights[n].shape)
        offset += size

    return (loss, grad_x, *[grad[n] for n in names], *[delta[n] for n in names],
            *[new_m[n] for n in names], *[new_v[n] for n in names])
```

```python
import functools

import jax
import jax.numpy as jnp
from jax import lax
from jax.experimental import pallas as pl
from jax.experimental.pallas import tpu as pltpu

F32 = jnp.float32
BF16 = jnp.bfloat16
MESH = pl.DeviceIdType.MESH

NORM_EPS = 1e-6
HEAD_DIM = 128
Q_PER_KV = 4
CONV_WIDTH = 3
FFN_RESIDUAL_WEIGHT = 0.5
DILATED_BRANCHES = ((128, 1), (512, 4), (2048, 16))
ADAM_LR = 0.001
ADAM_B1 = 0.9
ADAM_B2 = 0.999
ADAM_EPS = 1e-08
ADAM_WD = 0.01
ADAM_STEP = 10

N_CHIPS = 4
N_DEV = 8
V7X_VMEM_BYTES = 64 << 20
VMEM_LIMIT = V7X_VMEM_BYTES - (12 << 20)
SUBLANES = 8
LANES = 128
SMALL_ROWS = 16
BIG_BLOCK = 4 << 20


def _params(*sem):
    return pltpu.CompilerParams(dimension_semantics=sem, vmem_limit_bytes=VMEM_LIMIT)


def _row_tile(rows, cols, itemsize=4, budget=2 << 20):
    t = rows
    while t * cols * itemsize > budget and t % 32 == 0:
        t //= 2
    return t


def _sum_to_sublanes(v):
    r, n = v.shape
    return v.reshape(r // SUBLANES, SUBLANES, n).sum(axis=0)


_DIMS = {
    "nn": (((1,), (0,)), ((), ())),
    "nt": (((1,), (1,)), ((), ())),
    "tn": (((0,), (0,)), ((), ())),
}


ANY_SPEC = pl.BlockSpec(memory_space=pl.ANY)


def _dot(a, b, mode):
    return lax.dot_general(a, b, _DIMS[mode], preferred_element_type=F32)


def _mm(name, a, b, *, mode, grid, a_spec, b_spec, o_spec, out_shape, nk=1, acc_shape=None, deps=()):
    nd = len(deps)

    def body(a_ref, b_ref, *rest):
        o_ref, scratch = rest[nd], rest[nd + 1:]
        r = _dot(a_ref[...], b_ref[...], mode)
        if nk == 1:
            o_ref[...] = r.astype(o_ref.dtype)
        else:
            acc = scratch[0]
            k = pl.program_id(len(grid) - 1)

            @pl.when(k == 0)
            def _():
                acc[...] = r

            @pl.when(k > 0)
            def _():
                acc[...] += r

            @pl.when(k == nk - 1)
            def _():
                o_ref[...] = acc[...].astype(o_ref.dtype)

    sem = ("parallel",) * (len(grid) - (1 if nk > 1 else 0)) + (("arbitrary",) if nk > 1 else ())
    return pl.pallas_call(
        body, name=name, grid=grid, in_specs=[a_spec, b_spec] + [ANY_SPEC] * nd, out_specs=o_spec,
        out_shape=out_shape, scratch_shapes=[pltpu.VMEM(acc_shape, F32)] if nk > 1 else [],
        compiler_params=_params(*sem),
    )(a, b, *deps)


def _tile(n, want):
    if n <= want:
        return n
    best = None
    for t in range(LANES, want + 1, LANES):
        if n % t == 0:
            best = t
    assert best is not None, (n, want)
    return best


def _norm_fwd(name, x, gain):
    s, d = x.shape
    tr = _row_tile(s, d, budget=BIG_BLOCK)

    def body(x_ref, g_ref, o_ref):
        xv = x_ref[...]
        r = lax.rsqrt(jnp.mean(xv * xv, axis=-1, keepdims=True) + NORM_EPS)
        o_ref[...] = (xv * r * g_ref[...]).astype(o_ref.dtype)

    return pl.pallas_call(
        body, name=name, grid=(s // tr,),
        in_specs=[pl.BlockSpec((tr, d), lambda i: (i, 0)), pl.BlockSpec((1, d), lambda i: (0, 0))],
        out_specs=pl.BlockSpec((tr, d), lambda i: (i, 0)),
        out_shape=jax.ShapeDtypeStruct((s, d), BF16), compiler_params=_params("parallel"),
    )(x, gain)


def _res_norm(name, x, y, gain, scale, next_gain=None):
    s, d = x.shape
    tr = _row_tile(s, d, budget=BIG_BLOCK)
    with_next = next_gain is not None

    def body(x_ref, y_ref, g_ref, *rest):
        yv = y_ref[...]
        r = lax.rsqrt(jnp.mean(yv * yv, axis=-1, keepdims=True) + NORM_EPS)
        xn = x_ref[...] + scale * (yv * r * g_ref[...])
        if with_next:
            ng_ref, o_ref, h_ref = rest
            rn = lax.rsqrt(jnp.mean(xn * xn, axis=-1, keepdims=True) + NORM_EPS)
            h_ref[...] = (xn * rn * ng_ref[...]).astype(h_ref.dtype)
        else:
            o_ref, = rest
        o_ref[...] = xn

    row = pl.BlockSpec((tr, d), lambda i: (i, 0))
    vec = pl.BlockSpec((1, d), lambda i: (0, 0))
    outs = pl.pallas_call(
        body, name=name, grid=(s // tr,),
        in_specs=[row, row, vec] + ([vec] if with_next else []), out_specs=[row] * (2 if with_next else 1),
        out_shape=[jax.ShapeDtypeStruct((s, d), F32)] + ([jax.ShapeDtypeStruct((s, d), BF16)] if with_next else []),
        compiler_params=_params("parallel"),
    )(x, y, gain, *((next_gain,) if with_next else ()))
    return (outs[0], outs[1]) if with_next else (outs[0], None)


def _rms_bwd(dn, yv, gv):
    r = lax.rsqrt(jnp.mean(yv * yv, axis=-1, keepdims=True) + NORM_EPS)
    xhat = yv * r
    dxn = dn * gv
    return r * (dxn - xhat * jnp.mean(dxn * xhat, axis=-1, keepdims=True)), _sum_to_sublanes(dn * xhat)


def _accumulate(ref, part):
    @pl.when(pl.program_id(0) == 0)
    def _():
        ref[...] = part

    @pl.when(pl.program_id(0) > 0)
    def _():
        ref[...] += part


def _norm_bwd(name, dout, yin, gain, scale, resid, out_dtype, following=None):
    s, d = yin.shape
    tr = _row_tile(s, d)
    has_resid = resid is not None
    chained = following is not None

    def body(*refs):
        refs = list(refs)
        do_ref, y_ref, g_ref = refs[:3]
        del refs[:3]
        r_ref = refs.pop(0) if has_resid else None
        if chained:
            y2_ref, g2_ref = refs[:2]
            del refs[:2]
        di_ref, dg_ref = refs[:2]
        din, part = _rms_bwd(scale * do_ref[...], y_ref[...], g_ref[...])
        _accumulate(dg_ref, part)
        if has_resid:
            din = din + r_ref[...]
        di_ref[...] = din.astype(di_ref.dtype)
        if chained:
            d2_ref, dg2_ref = refs[2:]
            d2, part2 = _rms_bwd(following[2] * din, y2_ref[...], g2_ref[...])
            _accumulate(dg2_ref, part2)
            d2_ref[...] = d2.astype(d2_ref.dtype)

    row = pl.BlockSpec((tr, d), lambda i: (i, 0))
    vec = pl.BlockSpec((1, d), lambda i: (0, 0))
    acc = pl.BlockSpec((SUBLANES, d), lambda i: (0, 0))
    ins = [row, row, vec] + ([row] if has_resid else []) + ([row, vec] if chained else [])
    args = (dout, yin, gain) + ((resid,) if has_resid else ()) + (tuple(following[:2]) if chained else ())
    outs = [row, acc] + ([row, acc] if chained else [])
    shapes = [jax.ShapeDtypeStruct((s, d), out_dtype), jax.ShapeDtypeStruct((SUBLANES, d), F32)]
    if chained:
        shapes += [jax.ShapeDtypeStruct((s, d), BF16), jax.ShapeDtypeStruct((SUBLANES, d), F32)]
    return pl.pallas_call(
        body, name=name, grid=(s // tr,), in_specs=ins, out_specs=outs, out_shape=shapes,
        compiler_params=_params("arbitrary"),
    )(*args)


def _loss_head(name, y, target):
    s, d = y.shape
    tr = _row_tile(s, d)

    def body(y_ref, t_ref, dy_ref, l_ref):
        e = y_ref[...] - t_ref[...]
        dy_ref[...] = e * (1.0 / d)
        part = _sum_to_sublanes(e * e) * (0.5 / d)

        @pl.when(pl.program_id(0) == 0)
        def _():
            l_ref[...] = part

        @pl.when(pl.program_id(0) > 0)
        def _():
            l_ref[...] += part

    row = pl.BlockSpec((tr, d), lambda i: (i, 0))
    return pl.pallas_call(
        body, name=name, grid=(s // tr,), in_specs=[row, row],
        out_specs=[row, pl.BlockSpec((SUBLANES, d), lambda i: (0, 0))],
        out_shape=[jax.ShapeDtypeStruct((s, d), F32), jax.ShapeDtypeStruct((SUBLANES, d), F32)],
        compiler_params=_params("arbitrary"),
    )(y, target)


def _ffn_up(name, h, gu_w):
    s, d = h.shape
    nb, _, fs = gu_w.shape
    hb = nb // 2
    w = gu_w.reshape(2, hb, d, fs)
    tm = _tile(s, 512)
    tn = _tile(fs, 1408)
    nj = fs // tn

    def body(h_ref, w_ref, gu_ref, a_ref):
        hv = h_ref[...]
        g = _dot(hv, w_ref[0], "nn")
        u = _dot(hv, w_ref[1], "nn")
        sg = jax.nn.sigmoid(g)
        silu = g * sg
        gu_ref[0] = (u * (sg * (1.0 + g * (1.0 - sg)))).astype(gu_ref.dtype)
        gu_ref[1] = silu.astype(gu_ref.dtype)
        a_ref[...] = (silu * u).astype(a_ref.dtype)

    return pl.pallas_call(
        body, name=name, grid=(hb, nj, s // tm),
        in_specs=[pl.BlockSpec((tm, d), lambda jb, jo, i: (i, 0)),
                  pl.BlockSpec((2, None, d, tn), lambda jb, jo, i: (0, jb, 0, jo))],
        out_specs=[pl.BlockSpec((2, None, tm, tn), lambda jb, jo, i: (0, jb, i, jo)),
                   pl.BlockSpec((tm, tn), lambda jb, jo, i: (i, jb * nj + jo))],
        out_shape=[jax.ShapeDtypeStruct((2, hb, s, fs), BF16), jax.ShapeDtypeStruct((s, hb * fs), BF16)],
        compiler_params=_params("parallel", "parallel", "parallel"),
    )(h, w)


def _ffn_dact(name, dy, dn_w, gu, deps=()):
    s, d = dy.shape
    _, hb, _, fs = gu.shape
    tm = _tile(s, 512)
    tn = _tile(fs, 1408)
    nj = fs // tn

    def body(dy_ref, w_ref, gu_ref, *rest):
        o_ref = rest[-1]
        wv = w_ref[...]
        parts = 2 if tm % (2 * SUBLANES * 2) == 0 else 1
        for r in range(parts):
            rows = slice(r * (tm // parts), (r + 1) * (tm // parts))
            da = _dot(dy_ref[rows, :], wv, "nt")
            o_ref[0, rows, :] = (da * gu_ref[0, rows, :].astype(F32)).astype(o_ref.dtype)
            o_ref[1, rows, :] = (da * gu_ref[1, rows, :].astype(F32)).astype(o_ref.dtype)

    blk = pl.BlockSpec((2, None, tm, tn), lambda jb, jo, i: (0, jb, i, jo))
    return pl.pallas_call(
        body, name=name, grid=(hb, nj, s // tm),
        in_specs=[pl.BlockSpec((tm, d), lambda jb, jo, i: (i, 0)),
                  pl.BlockSpec((tn, d), lambda jb, jo, i: (jb * nj + jo, 0)),
                  blk] + [ANY_SPEC] * len(deps),
        out_specs=blk, out_shape=jax.ShapeDtypeStruct(gu.shape, BF16),
        compiler_params=_params("parallel", "parallel", "parallel"),
    )(dy, dn_w, gu, *deps)


_MASKED = -1e30


def _attn_bias(s, tq):
    nd = s // tq
    dist = (jnp.arange(nd)[:, None, None] * tq + jnp.arange(tq)[None, :, None]) - jnp.arange(tq)[None, None, :]
    mult = jnp.zeros(dist.shape, F32)
    for window, dilation in DILATED_BRANCHES:
        mult = mult + ((dist >= 0) & (dist <= window) & (dist % dilation == 0)).astype(F32)
    return jnp.where(mult > 0.0, jnp.log(jnp.maximum(mult, 1.0)), _MASKED)


def _biased(sc, bias, scale):
    tq, tk = bias.shape
    return (sc.reshape(-1, tq, tk) * scale + bias[None]).reshape(sc.shape)


def _attn_specs(s, qd, kvd, tq):
    rw = Q_PER_KV * HEAD_DIM
    qspec = pl.BlockSpec((tq, rw), lambda g, i: (i, g))
    kspec = pl.BlockSpec((s, HEAD_DIM), lambda g, i: (0, qd // HEAD_DIM + g))
    vspec = pl.BlockSpec((s, HEAD_DIM), lambda g, i: (0, (qd + kvd) // HEAD_DIM + g))
    return rw, qspec, kspec, vspec


def _attn_fwd(name, z, qd, kvd):
    s = z.shape[0]
    tq = _tile(s, 256)
    nkv = kvd // HEAD_DIM
    rw, qspec, kspec, vspec = _attn_specs(s, qd, kvd, tq)
    scale = HEAD_DIM ** -0.5

    def body(q_ref, k_ref, v_ref, b_ref, o_ref, l_ref):
        i = pl.program_id(1)
        heads = [slice(h * HEAD_DIM, (h + 1) * HEAD_DIM) for h in range(Q_PER_KV)]
        q_all = jnp.concatenate([q_ref[:, cols] for cols in heads], axis=0)

        def chunk(j, carry):
            mx, den, acc = carry
            k0 = pl.multiple_of(j * tq, tq)
            kc, vc = k_ref[pl.ds(k0, tq), :], v_ref[pl.ds(k0, tq), :]
            sc = _biased(_dot(q_all, kc, "nt"), b_ref[i - j], scale)
            mx_new = jnp.maximum(mx, jnp.max(sc, axis=-1, keepdims=True))
            alpha = jnp.exp(mx - mx_new)
            p = jnp.exp(sc - mx_new)
            return (mx_new, alpha * den + jnp.sum(p, axis=-1, keepdims=True),
                    alpha * acc + _dot(p.astype(BF16), vc, "nn"))

        rows = Q_PER_KV * tq
        init = (jnp.full((rows, 1), _MASKED, F32), jnp.zeros((rows, 1), F32), jnp.zeros((rows, HEAD_DIM), F32))
        mx, den, acc = lax.fori_loop(0, i + 1, chunk, init)
        out = acc / den
        lse = mx + jnp.log(den)
        for h, cols in enumerate(heads):
            o_ref[:, cols] = out[h * tq:(h + 1) * tq]
            l_ref[:, cols] = jnp.broadcast_to(lse[h * tq:(h + 1) * tq], (tq, HEAD_DIM))

    bias = _attn_bias(s, tq)
    return pl.pallas_call(
        body, name=name, grid=(nkv, s // tq),
        in_specs=[qspec, kspec, vspec, pl.BlockSpec(bias.shape, lambda g, i: (0, 0, 0))], out_specs=[qspec, qspec],
        out_shape=[jax.ShapeDtypeStruct((s, qd), F32), jax.ShapeDtypeStruct((s, qd), F32)],
        compiler_params=_params("parallel", "parallel"),
    )(z, z, z, bias)


def _attn_bwd(name, z, o, lse, do, qd, kvd):
    s = z.shape[0]
    tq = _tile(s, 256)
    nkv = kvd // HEAD_DIM
    nq = s // tq
    rw, qspec, kspec, vspec = _attn_specs(s, qd, kvd, tq)
    scale = HEAD_DIM ** -0.5

    def body(q_ref, k_ref, v_ref, o_ref, l_ref, do_ref, b_ref, dq_ref, dk_ref, dv_ref, dk_acc, dv_acc):
        i = pl.program_id(1)
        heads = [slice(h * HEAD_DIM, (h + 1) * HEAD_DIM) for h in range(Q_PER_KV)]

        @pl.when(i == 0)
        def _():
            dk_acc[...] = jnp.zeros_like(dk_acc)
            dv_acc[...] = jnp.zeros_like(dv_acc)

        q_all = jnp.concatenate([q_ref[:, cols] for cols in heads], axis=0)
        do_all = jnp.concatenate([do_ref[:, cols].astype(BF16) for cols in heads], axis=0)
        lse_all = jnp.concatenate([l_ref[:, cols][:, :1] for cols in heads], axis=0)
        delta_all = jnp.concatenate(
            [jnp.sum(do_ref[:, cols] * o_ref[:, cols], axis=-1, keepdims=True) for cols in heads], axis=0)

        def chunk(j, dq):
            k0 = pl.multiple_of(j * tq, tq)
            kc, vc = k_ref[pl.ds(k0, tq), :], v_ref[pl.ds(k0, tq), :]
            p = jnp.exp(_biased(_dot(q_all, kc, "nt"), b_ref[i - j], scale) - lse_all)
            ds = (p * (_dot(do_all, vc, "nt") - delta_all) * scale).astype(BF16)
            dk_acc[pl.ds(k0, tq), :] += _dot(ds, q_all, "tn")
            dv_acc[pl.ds(k0, tq), :] += _dot(p.astype(BF16), do_all, "tn")
            return dq + _dot(ds, kc, "nn")

        dq = lax.fori_loop(0, i + 1, chunk, jnp.zeros((Q_PER_KV * tq, HEAD_DIM), F32))
        for h, cols in enumerate(heads):
            dq_ref[:, cols] = dq[h * tq:(h + 1) * tq].astype(dq_ref.dtype)

        @pl.when(i == nq - 1)
        def _():
            dk_ref[...] = dk_acc[...].astype(dk_ref.dtype)
            dv_ref[...] = dv_acc[...].astype(dv_ref.dtype)

    kvout = pl.BlockSpec((s, HEAD_DIM), lambda g, i: (0, g))
    bias = _attn_bias(s, tq)
    return pl.pallas_call(
        body, name=name, grid=(nkv, nq),
        in_specs=[qspec, kspec, vspec, qspec, qspec, qspec, pl.BlockSpec(bias.shape, lambda g, i: (0, 0, 0))],
        out_specs=[qspec, kvout, kvout],
        out_shape=[jax.ShapeDtypeStruct((s, qd), BF16), jax.ShapeDtypeStruct((s, kvd), BF16),
                   jax.ShapeDtypeStruct((s, kvd), BF16)],
        scratch_shapes=[pltpu.VMEM((s, HEAD_DIM), F32), pltpu.VMEM((s, HEAD_DIM), F32)],
        compiler_params=_params("parallel", "arbitrary"),
    )(z, z, z, o, lse, do, bias)


def _shift_down(v, n):
    rolled = pltpu.roll(v, n, 0)
    t = lax.broadcasted_iota(jnp.int32, v.shape, 0)
    return jnp.where(t >= n, rolled, 0.0)


def _shift_up(v, n):
    rows = v.shape[0]
    rolled = pltpu.roll(v, rows - n, 0)
    t = lax.broadcasted_iota(jnp.int32, v.shape, 0)
    return jnp.where(t < rows - n, rolled, 0.0)


def _conv_specs(s, base, cd, tc):
    zs = [pl.BlockSpec((s, tc), functools.partial(lambda j, off: (0, off + j), off=(base + n * cd) // tc))
          for n in range(3)]
    wspec = pl.BlockSpec((SUBLANES, tc), lambda j: (0, j))
    cspec = pl.BlockSpec((s, tc), lambda j: (0, j))
    return zs, wspec, cspec


def _conv_fwd(name, z, conv_w, base, cd):
    s = z.shape[0]
    tc = _tile(cd, 256)
    zs, wspec, cspec = _conv_specs(s, base, cd, tc)

    def body(h_ref, b_ref, c_ref, w_ref, o_ref):
        u = c_ref[...].astype(F32) * h_ref[...].astype(F32)
        y = w_ref[0:1, :] * _shift_down(u, 2) + w_ref[1:2, :] * _shift_down(u, 1) + w_ref[2:3, :] * u
        o_ref[...] = b_ref[...].astype(F32) * y

    return pl.pallas_call(
        body, name=name, grid=(cd // tc,), in_specs=zs + [wspec], out_specs=cspec,
        out_shape=jax.ShapeDtypeStruct((s, cd), F32), compiler_params=_params("parallel"),
    )(z, z, z, conv_w)


def _conv_bwd(name, z, conv_w, dc, base, cd):
    s = z.shape[0]
    tc = _tile(cd, 256)
    zs, wspec, cspec = _conv_specs(s, base, cd, tc)

    def body(h_ref, b_ref, c_ref, w_ref, dc_ref, dh_ref, db_ref, dcg_ref, dw_ref):
        hv, bv, cv = h_ref[...].astype(F32), b_ref[...].astype(F32), c_ref[...].astype(F32)
        u = cv * hv
        u1, u2 = _shift_down(u, 1), _shift_down(u, 2)
        w0, w1, w2 = w_ref[0:1, :], w_ref[1:2, :], w_ref[2:3, :]
        y = w0 * u2 + w1 * u1 + w2 * u
        dcv = dc_ref[...]
        db_ref[...] = (dcv * y).astype(db_ref.dtype)
        dy = dcv * bv
        du = w2 * dy + w1 * _shift_up(dy, 1) + w0 * _shift_up(dy, 2)
        dh_ref[...] = (du * cv).astype(dh_ref.dtype)
        dcg_ref[...] = (du * hv).astype(dcg_ref.dtype)
        g0 = jnp.sum(dy * u2, axis=0, keepdims=True)
        g1 = jnp.sum(dy * u1, axis=0, keepdims=True)
        g2 = jnp.sum(dy * u, axis=0, keepdims=True)
        r = lax.broadcasted_iota(jnp.int32, (SUBLANES, tc), 0)
        dw_ref[...] = jnp.where(r == 0, g0, jnp.where(r == 1, g1, jnp.where(r == 2, g2, 0.0)))

    return pl.pallas_call(
        body, name=name, grid=(cd // tc,), in_specs=zs + [wspec, cspec],
        out_specs=[cspec, cspec, cspec, wspec],
        out_shape=[jax.ShapeDtypeStruct((s, cd), BF16)] * 3 + [jax.ShapeDtypeStruct((SUBLANES, cd), F32)],
        compiler_params=_params("parallel"),
    )(z, z, z, conv_w, dc)


def _cat_norm_fwd(name, a, c, ga, gc):
    s, qd = a.shape
    cd = c.shape[1]
    tr = _row_tile(s, qd + cd)

    def body(a_ref, c_ref, ga_ref, gc_ref, o_ref):
        av, cv = a_ref[...], c_ref[...]
        ra = lax.rsqrt(jnp.mean(av * av, axis=-1, keepdims=True) + NORM_EPS)
        rc = lax.rsqrt(jnp.mean(cv * cv, axis=-1, keepdims=True) + NORM_EPS)
        o_ref[:, :qd] = (av * ra * ga_ref[...]).astype(o_ref.dtype)
        o_ref[:, qd:] = (cv * rc * gc_ref[...]).astype(o_ref.dtype)

    return pl.pallas_call(
        body, name=name, grid=(s // tr,),
        in_specs=[pl.BlockSpec((tr, qd), lambda i: (i, 0)), pl.BlockSpec((tr, cd), lambda i: (i, 0)),
                  pl.BlockSpec((1, qd), lambda i: (0, 0)), pl.BlockSpec((1, cd), lambda i: (0, 0))],
        out_specs=pl.BlockSpec((tr, qd + cd), lambda i: (i, 0)),
        out_shape=jax.ShapeDtypeStruct((s, qd + cd), BF16), compiler_params=_params("parallel"),
    )(a, c, ga, gc)


def _cat_norm_bwd(name, dcat, a, c, ga, gc):
    s, qd = a.shape
    cd = c.shape[1]
    tr = _row_tile(s, qd + cd)

    def one(dn, yv, gv):
        r = lax.rsqrt(jnp.mean(yv * yv, axis=-1, keepdims=True) + NORM_EPS)
        xhat = yv * r
        dxn = dn * gv
        return r * (dxn - xhat * jnp.mean(dxn * xhat, axis=-1, keepdims=True)), _sum_to_sublanes(dn * xhat)

    def body(d_ref, a_ref, c_ref, ga_ref, gc_ref, da_ref, dc_ref, dga_ref, dgc_ref):
        da, pa = one(d_ref[:, :qd], a_ref[...], ga_ref[...])
        dc, pc = one(d_ref[:, qd:], c_ref[...], gc_ref[...])
        da_ref[...] = da
        dc_ref[...] = dc

        @pl.when(pl.program_id(0) == 0)
        def _():
            dga_ref[...] = pa
            dgc_ref[...] = pc

        @pl.when(pl.program_id(0) > 0)
        def _():
            dga_ref[...] += pa
            dgc_ref[...] += pc

    ra = pl.BlockSpec((tr, qd), lambda i: (i, 0))
    rc = pl.BlockSpec((tr, cd), lambda i: (i, 0))
    return pl.pallas_call(
        body, name=name, grid=(s // tr,),
        in_specs=[pl.BlockSpec((tr, qd + cd), lambda i: (i, 0)), ra, rc,
                  pl.BlockSpec((1, qd), lambda i: (0, 0)), pl.BlockSpec((1, cd), lambda i: (0, 0))],
        out_specs=[ra, rc, pl.BlockSpec((SUBLANES, qd), lambda i: (0, 0)),
                   pl.BlockSpec((SUBLANES, cd), lambda i: (0, 0))],
        out_shape=[jax.ShapeDtypeStruct((s, qd), F32), jax.ShapeDtypeStruct((s, cd), F32),
                   jax.ShapeDtypeStruct((SUBLANES, qd), F32), jax.ShapeDtypeStruct((SUBLANES, cd), F32)],
        compiler_params=_params("arbitrary"),
    )(dcat, a, c, ga, gc)


def _adamw(name, w, g, m, v, emit_grad=False):
    shape = w.shape
    cols = shape[-1]
    rows = w.size // cols
    tr = _row_tile(rows, cols, budget=3 << 19)
    bc1 = 1.0 - ADAM_B1 ** ADAM_STEP
    bc2 = 1.0 - ADAM_B2 ** ADAM_STEP
    n_out = 4 if emit_grad else 3

    def body(w_ref, g_ref, m_ref, v_ref, d_ref, nm_ref, nv_ref, *g_out):
        gv = g_ref[...]
        mv = ADAM_B1 * m_ref[...] + (1.0 - ADAM_B1) * gv
        vv = ADAM_B2 * v_ref[...] + (1.0 - ADAM_B2) * (gv * gv)
        nm_ref[...] = mv
        nv_ref[...] = vv
        d_ref[...] = -ADAM_LR * ((mv / bc1) / (jnp.sqrt(vv / bc2) + ADAM_EPS) + ADAM_WD * w_ref[...])
        for ref in g_out:
            ref[...] = gv

    row = pl.BlockSpec((tr, cols), lambda i: (i, 0))
    outs = pl.pallas_call(
        body, name=name, grid=(rows // tr,), in_specs=[row] * 4, out_specs=[row] * n_out,
        out_shape=[jax.ShapeDtypeStruct((rows, cols), F32)] * n_out, compiler_params=_params("parallel"),
    )(*(t.reshape(rows, cols) for t in (w, g, m, v)))
    return tuple(t.reshape(shape) for t in outs)


HBM_SPEC = pl.BlockSpec(memory_space=pltpu.HBM)


def _mesh_place():
    x, y, c = lax.axis_index("x"), lax.axis_index("y"), lax.axis_index("c")
    other_chips = [(1 - x, y), (x, 1 - y), (1 - x, 1 - y)]
    return x, y, c, other_chips


def _cast_into_slot(name, w, layer, chip, deps=()):
    _, r, cols = w.shape
    tr = _row_tile(r, cols, budget=BIG_BLOCK)

    def body(chip_ref, w_ref, *rest):
        o_ref = rest[-1]
        o_ref[...] = w_ref[...].astype(o_ref.dtype)

    return pl.pallas_call(
        body, name=name,
        grid_spec=pltpu.PrefetchScalarGridSpec(
            num_scalar_prefetch=1, grid=(r // tr,),
            in_specs=[pl.BlockSpec((None, tr, cols), lambda i, chip_ref: (layer, i, 0))] + [ANY_SPEC] * len(deps),
            out_specs=pl.BlockSpec((None, tr, cols), lambda i, chip_ref: (chip_ref[0], i, 0))),
        out_shape=jax.ShapeDtypeStruct((N_CHIPS, r, cols), BF16), compiler_params=_params("parallel"),
    )(chip, w, *deps)


SEM_SPEC = pl.BlockSpec(memory_space=pltpu.SEMAPHORE)
SPLIT_COPY = pltpu.CompilerParams(has_side_effects=pltpu.SideEffectType.DATAFLOW_SIDE_EFFECTING)
N_OTHER = N_CHIPS - 1
TOKEN_SPEC = pl.BlockSpec(memory_space=pltpu.VMEM)
TOKEN_SHAPE = jax.ShapeDtypeStruct((SUBLANES, LANES), F32)


def _in_hbm(arr):
    return pltpu.with_memory_space_constraint(arr, pltpu.HBM)


def _half_rows(ref, chip_idx, core):
    r2 = ref.shape[1] // 2
    return ref.at[chip_idx, pl.ds(core * r2, r2), :]


def _gather_start(name, fulls, after):
    na = len(fulls)

    def body(*refs):
        f_refs = refs[na + 1:2 * na + 1]
        send_sems, recv_sems = refs[2 * na + 1:3 * na + 1], refs[3 * na + 1:4 * na + 1]
        token = refs[4 * na + 1]
        x, y, c, chips = _mesh_place()
        for a in range(na):
            mine = _half_rows(f_refs[a], 2 * x + y, c)
            for j, (cx, cy) in enumerate(chips):
                pltpu.make_async_remote_copy(
                    src_ref=mine, dst_ref=mine, send_sem=send_sems[a].at[j], recv_sem=recv_sems[a].at[j],
                    device_id=(cx, cy, c), device_id_type=MESH).start()
        token[...] = jnp.zeros_like(token)

    outs = pl.pallas_call(
        body, name=name, in_specs=[HBM_SPEC] * na + [ANY_SPEC],
        out_specs=[HBM_SPEC] * na + [SEM_SPEC] * (2 * na) + [TOKEN_SPEC],
        out_shape=[pltpu.HBM(f.shape, f.dtype) for f in fulls] + [pltpu.SemaphoreType.DMA((N_OTHER,))] * (2 * na)
        + [TOKEN_SHAPE],
        input_output_aliases={a: a for a in range(na)}, compiler_params=SPLIT_COPY,
    )(*[_in_hbm(f) for f in fulls], after)
    return list(outs[:na]), list(outs[na:2 * na]), list(outs[2 * na:3 * na]), outs[3 * na]


def _gather_pass_on(name, full, recv_sems, after):
    def body(f_in, recv_sems, after_ref, f_ref, d2d_send, d2d_recv):
        x, y, c, chips = _mesh_place()
        for j, (cx, cy) in enumerate(chips):
            blk = _half_rows(f_ref, 2 * cx + cy, c)
            pltpu.make_async_remote_copy(
                src_ref=blk, dst_ref=blk, send_sem=d2d_send.at[j], recv_sem=recv_sems.at[j],
                device_id=(cx, cy, c), device_id_type=MESH).wait_recv()
            pltpu.make_async_remote_copy(
                src_ref=blk, dst_ref=blk, send_sem=d2d_send.at[j], recv_sem=d2d_recv.at[j],
                device_id=(x, y, 1 - c), device_id_type=MESH).start()

    return pl.pallas_call(
        body, name=name, in_specs=[HBM_SPEC, SEM_SPEC, ANY_SPEC], out_specs=[HBM_SPEC, SEM_SPEC, SEM_SPEC],
        out_shape=[pltpu.HBM(full.shape, full.dtype)] + [pltpu.SemaphoreType.DMA((N_OTHER,))] * 2,
        input_output_aliases={0: 0}, compiler_params=SPLIT_COPY,
    )(full, recv_sems, after)


def _gather_arrive(name, full, ici_send, d2d_send, d2d_recv, after):
    def body(f_in, ici_send, d2d_send, d2d_recv, after_ref, f_ref):
        x, y, c, chips = _mesh_place()
        for j, (cx, cy) in enumerate(chips):
            mine = _half_rows(f_ref, 2 * x + y, c)
            passed = _half_rows(f_ref, 2 * cx + cy, c)
            theirs = _half_rows(f_ref, 2 * cx + cy, 1 - c)
            pltpu.make_async_remote_copy(
                src_ref=mine, dst_ref=mine, send_sem=ici_send.at[j], recv_sem=d2d_recv.at[j],
                device_id=(cx, cy, c), device_id_type=MESH).wait_send()
            pltpu.make_async_remote_copy(
                src_ref=passed, dst_ref=passed, send_sem=d2d_send.at[j], recv_sem=d2d_recv.at[j],
                device_id=(x, y, 1 - c), device_id_type=MESH).wait_send()
            pltpu.make_async_remote_copy(
                src_ref=theirs, dst_ref=theirs, send_sem=d2d_send.at[j], recv_sem=d2d_recv.at[j],
                device_id=(x, y, 1 - c), device_id_type=MESH).wait_recv()

    return pl.pallas_call(
        body, name=name, in_specs=[HBM_SPEC, SEM_SPEC, SEM_SPEC, SEM_SPEC, ANY_SPEC], out_specs=HBM_SPEC,
        out_shape=pltpu.HBM(full.shape, full.dtype), input_output_aliases={0: 0}, compiler_params=SPLIT_COPY,
    )(full, ici_send, d2d_send, d2d_recv, after)


def _gather_taps(conv_w):
    def body(cw_ref, cwf_ref, send_sems, recv_sems, local_sem):
        x, y, c, chips = _mesh_place()
        k_me = 2 * x + y
        local = pltpu.make_async_copy(cw_ref, cwf_ref.at[k_me], local_sem)
        local.start()
        copies = [pltpu.make_async_remote_copy(
            src_ref=cw_ref, dst_ref=cwf_ref.at[k_me], send_sem=send_sems.at[j], recv_sem=recv_sems.at[j],
            device_id=(cx, cy, c), device_id_type=MESH) for j, (cx, cy) in enumerate(chips)]
        for cp in copies:
            cp.start()
        for j, (cx, cy) in enumerate(chips):
            pltpu.make_async_remote_copy(
                src_ref=cw_ref, dst_ref=cwf_ref.at[2 * cx + cy], send_sem=send_sems.at[j], recv_sem=recv_sems.at[j],
                device_id=(cx, cy, c), device_id_type=MESH).wait_recv()
        for cp in copies:
            cp.wait_send()
        local.wait()

    return pl.pallas_call(
        body, name="gather_taps", in_specs=[HBM_SPEC], out_specs=HBM_SPEC,
        out_shape=jax.ShapeDtypeStruct((N_CHIPS,) + conv_w.shape, conv_w.dtype),
        scratch_shapes=[pltpu.SemaphoreType.DMA((N_OTHER,))] * 2 + [pltpu.SemaphoreType.DMA],
    )(conv_w)


def _sibling_half(g_ref, c):
    r2 = g_ref.shape[1] // 2
    return g_ref.at[:, pl.ds((1 - c) * r2, r2), :]


def _swap_copy(g_ref, land_ref, send_sems, recv_sems, a):
    x, y, c, _ = _mesh_place()
    return pltpu.make_async_remote_copy(
        src_ref=_sibling_half(g_ref, c), dst_ref=land_ref, send_sem=send_sems.at[a], recv_sem=recv_sems.at[a],
        device_id=(x, y, 1 - c), device_id_type=MESH)


def _swap_start(name, gs):
    n = len(gs)

    def body(*refs):
        g_refs, land_refs = refs[n:2 * n], refs[2 * n:3 * n]
        send_sems, recv_sems, token = refs[3 * n:]
        for a in range(n):
            _swap_copy(g_refs[a], land_refs[a], send_sems, recv_sems, a).start()
        token[...] = jnp.zeros_like(token)

    outs = pl.pallas_call(
        body, name=name, in_specs=[HBM_SPEC] * n,
        out_specs=[HBM_SPEC] * (2 * n) + [SEM_SPEC, SEM_SPEC, TOKEN_SPEC],
        out_shape=[pltpu.HBM(g.shape, g.dtype) for g in gs]
        + [pltpu.HBM((g.shape[0], g.shape[1] // 2, g.shape[2]), g.dtype) for g in gs]
        + [pltpu.SemaphoreType.DMA((n,)), pltpu.SemaphoreType.DMA((n,)), TOKEN_SHAPE],
        input_output_aliases={a: a for a in range(n)}, compiler_params=SPLIT_COPY,
    )(*[_in_hbm(g) for g in gs])
    return list(outs[:n]), list(outs[n:2 * n]), outs[2 * n], outs[2 * n + 1], outs[2 * n + 2]


def _swap_wait(name, gs, lands, send_sems, recv_sems, after):
    n = len(gs)

    def body(*refs):
        send_sems, recv_sems = refs[2 * n], refs[2 * n + 1]
        g_refs, land_refs = refs[2 * n + 3:3 * n + 3], refs[3 * n + 3:]
        for a in range(n):
            copy = _swap_copy(g_refs[a], land_refs[a], send_sems, recv_sems, a)
            copy.wait_send()
            copy.wait_recv()

    outs = pl.pallas_call(
        body, name=name, in_specs=[HBM_SPEC] * (2 * n) + [SEM_SPEC, SEM_SPEC, ANY_SPEC],
        out_specs=[HBM_SPEC] * (2 * n),
        out_shape=[pltpu.HBM(t.shape, t.dtype) for t in list(gs) + list(lands)],
        input_output_aliases={a: a for a in range(2 * n)}, compiler_params=SPLIT_COPY,
    )(*gs, *lands, send_sems, recv_sems, after)
    return list(outs[:n]), list(outs[n:])


def _add_core_halves(name, g, sib, core):
    nb, r, cols = g.shape
    r2 = r // 2
    tr = _row_tile(r2, cols, itemsize=2, budget=BIG_BLOCK)
    nrt = r2 // tr

    def body(core_ref, g_ref, s_ref, o_ref):
        o_ref[...] = (g_ref[...].astype(F32) + s_ref[...].astype(F32)).astype(o_ref.dtype)

    return pl.pallas_call(
        body, name=name,
        grid_spec=pltpu.PrefetchScalarGridSpec(
            num_scalar_prefetch=1, grid=(nb, nrt),
            in_specs=[pl.BlockSpec((None, tr, cols), lambda k, i, core_ref: (k, core_ref[0] * nrt + i, 0)),
                      pl.BlockSpec((None, tr, cols), lambda k, i, core_ref: (k, i, 0))],
            out_specs=pl.BlockSpec((None, tr, cols), lambda k, i, core_ref: (k, i, 0))),
        out_shape=jax.ShapeDtypeStruct((nb, r2, cols), BF16), compiler_params=_params("parallel", "parallel"),
    )(core, g, sib)


def _scatter_copies(h_refs, land_refs, send_sems, recv_sems):
    x, y, c, chips = _mesh_place()
    return [pltpu.make_async_remote_copy(
        src_ref=h_ref.at[2 * cx + cy], dst_ref=land_ref.at[j],
        send_sem=send_sems.at[a * N_OTHER + j], recv_sem=recv_sems.at[a * N_OTHER + j],
        device_id=(cx, cy, c), device_id_type=MESH)
        for a, (h_ref, land_ref) in enumerate(zip(h_refs, land_refs)) for j, (cx, cy) in enumerate(chips)]


def _scatter_start(name, hs):
    n = len(hs)

    def body(*refs):
        h_refs, land_refs = refs[n:2 * n], refs[2 * n:3 * n]
        send_sems, recv_sems, token = refs[3 * n:]
        for copy in _scatter_copies(h_refs, land_refs, send_sems, recv_sems):
            copy.start()
        token[...] = jnp.zeros_like(token)

    outs = pl.pallas_call(
        body, name=name, in_specs=[HBM_SPEC] * n,
        out_specs=[HBM_SPEC] * (2 * n) + [SEM_SPEC, SEM_SPEC, TOKEN_SPEC],
        out_shape=[pltpu.HBM(h.shape, h.dtype) for h in hs]
        + [pltpu.HBM((N_OTHER,) + h.shape[1:], h.dtype) for h in hs]
        + [pltpu.SemaphoreType.DMA((n * N_OTHER,)), pltpu.SemaphoreType.DMA((n * N_OTHER,)), TOKEN_SHAPE],
        input_output_aliases={a: a for a in range(n)}, compiler_params=SPLIT_COPY,
    )(*[_in_hbm(h) for h in hs])
    return list(outs[:n]), list(outs[n:2 * n]), outs[2 * n], outs[2 * n + 1], outs[2 * n + 2]


def _scatter_wait(name, hs, lands, send_sems, recv_sems, after):
    afters = tuple(after) if isinstance(after, (tuple, list)) else (after,)
    n = len(hs)

    def body(*refs):
        send_sems, recv_sems = refs[2 * n], refs[2 * n + 1]
        h_refs, land_refs = refs[-2 * n:-n], refs[-n:]
        for copy in _scatter_copies(h_refs, land_refs, send_sems, recv_sems):
            copy.wait_send()
            copy.wait_recv()

    outs = pl.pallas_call(
        body, name=name, in_specs=[HBM_SPEC] * (2 * n) + [SEM_SPEC, SEM_SPEC] + [ANY_SPEC] * len(afters),
        out_specs=[HBM_SPEC] * (2 * n),
        out_shape=[pltpu.HBM(t.shape, t.dtype) for t in list(hs) + list(lands)],
        input_output_aliases={a: a for a in range(2 * n)}, compiler_params=SPLIT_COPY,
    )(*hs, *lands, send_sems, recv_sems, *afters)
    return list(outs[:n]), list(outs[n:])


def _sum_chips(name, hs, rcv, core, chip, layer, n_layers, prev):
    _, r2, cols = hs.shape
    tr = _row_tile(r2, cols, budget=BIG_BLOCK)
    nrt = r2 // tr

    def body(core_ref, chip_ref, h_ref, r_ref, *rest):
        o_ref = rest[-1]
        acc = h_ref[...].astype(F32)
        for j in range(N_CHIPS - 1):
            acc = acc + r_ref[j].astype(F32)
        o_ref[...] = acc

    in_specs = [pl.BlockSpec((None, tr, cols), lambda i, core_ref, chip_ref: (chip_ref[0], i, 0)),
                pl.BlockSpec((N_CHIPS - 1, tr, cols), lambda i, core_ref, chip_ref: (0, i, 0))]
    args = [core, chip, hs, rcv]
    aliases = {}
    if prev is not None:
        in_specs.append(pl.BlockSpec(memory_space=pl.ANY))
        args.append(prev)
        aliases = {4: 0}
    return pl.pallas_call(
        body, name=name,
        grid_spec=pltpu.PrefetchScalarGridSpec(
            num_scalar_prefetch=2, grid=(nrt,), in_specs=in_specs,
            out_specs=pl.BlockSpec((None, tr, cols), lambda i, core_ref, chip_ref: (layer, core_ref[0] * nrt + i, 0))),
        out_shape=jax.ShapeDtypeStruct((n_layers, 2 * r2, cols), F32), input_output_aliases=aliases,
        compiler_params=_params("parallel"),
    )(*args)


def _join_copy(t_ref, send_sems, recv_sems, a):
    x, y, c, _ = _mesh_place()
    r2 = t_ref.shape[1] // 2
    mine = t_ref.at[:, pl.ds(c * r2, r2), :]
    return pltpu.make_async_remote_copy(
        src_ref=mine, dst_ref=mine, send_sem=send_sems.at[a], recv_sem=recv_sems.at[a],
        device_id=(x, y, 1 - c), device_id_type=MESH)


def _join_start(name, ts, deps=()):
    n, nd = len(ts), len(deps)

    def body(*refs):
        t_refs = refs[n + nd:2 * n + nd]
        send_sems, recv_sems = refs[2 * n + nd:]
        for a in range(n):
            _join_copy(t_refs[a], send_sems, recv_sems, a).start()

    outs = pl.pallas_call(
        body, name=name, in_specs=[HBM_SPEC] * n + [ANY_SPEC] * nd, out_specs=[HBM_SPEC] * n + [SEM_SPEC, SEM_SPEC],
        out_shape=[pltpu.HBM(t.shape, t.dtype) for t in ts] + [pltpu.SemaphoreType.DMA((n,))] * 2,
        input_output_aliases={a: a for a in range(n)}, compiler_params=SPLIT_COPY,
    )(*[_in_hbm(t) for t in ts], *deps)
    return list(outs[:n]), outs[n], outs[n + 1]


def _join_wait(name, t, a, send_sems, recv_sems, after):
    def body(t_in, send_sems, recv_sems, after_ref, t_ref):
        copy = _join_copy(t_ref, send_sems, recv_sems, a)
        copy.wait_send()
        copy.wait_recv()

    return pl.pallas_call(
        body, name=name, in_specs=[HBM_SPEC, SEM_SPEC, SEM_SPEC, ANY_SPEC], out_specs=HBM_SPEC,
        out_shape=pltpu.HBM(t.shape, t.dtype), input_output_aliases={0: 0}, compiler_params=SPLIT_COPY,
    )(t, send_sems, recv_sems, after)


def _allreduce_small(p):
    n, _, w = p.shape

    def body(p_ref, o_ref, buf, send_sems, recv_sems):
        x, y, c, _ = _mesh_place()
        me = 4 * x + 2 * y + c
        buf[me] = jnp.sum(p_ref[...], axis=1)
        copies = []
        for pat in range(1, N_DEV):
            fx, fy, fc = (pat >> 2) & 1, (pat >> 1) & 1, pat & 1
            copies.append(pltpu.make_async_remote_copy(
                src_ref=buf.at[me], dst_ref=buf.at[me], send_sem=send_sems.at[pat - 1], recv_sem=recv_sems.at[pat - 1],
                device_id=(x ^ fx, y ^ fy, c ^ fc), device_id_type=MESH))
        for cp in copies:
            cp.start()
        for cp in copies:
            cp.wait()
        acc = buf[0]
        for dev in range(1, N_DEV):
            acc = acc + buf[dev]
        o_ref[...] = acc

    return pl.pallas_call(
        body, name="allreduce_small", in_specs=[pl.BlockSpec(memory_space=pltpu.VMEM)],
        out_specs=pl.BlockSpec(memory_space=pltpu.VMEM), out_shape=jax.ShapeDtypeStruct((n, w), F32),
        scratch_shapes=[pltpu.VMEM((N_DEV, n, w), F32), pltpu.SemaphoreType.DMA((N_DEV - 1,)),
                        pltpu.SemaphoreType.DMA((N_DEV - 1,))],
    )(p)


class _WeightFeed:
    def __init__(self):
        self.fulls, self.ici_send, self.ici_recv, self.d2d = [], [], [], []

    def start(self, name, fulls, after):
        started, send, recv, token = _gather_start(name, fulls, after)
        self.fulls += started
        self.ici_send += send
        self.ici_recv += recv
        self.d2d += [None] * len(fulls)
        self.token = token
        return token

    def _pass_on(self, k, after):
        if k == 0:
            after = self.token
        if k < len(self.fulls) and self.d2d[k] is None:
            self.fulls[k], send, recv = _gather_pass_on(f"gather_pass_{k}", self.fulls[k], self.ici_recv[k], after)
            self.d2d[k] = (send, recv)

    def take(self, k, after):
        self._pass_on(k, after)
        self.fulls[k] = _gather_arrive(f"gather_arrive_{k}", self.fulls[k], self.ici_send[k], *self.d2d[k], after)
        return self.fulls[k]


def _ffn_forward(tag, x, h, g_post, next_gain, feed, k):
    s, d = x.shape
    gu_w = feed.take(k, h)
    gu, a = _ffn_up(f"{tag}_up", h, gu_w)
    dn_w = feed.take(k + 1, a).reshape(-1, d)
    f = dn_w.shape[0]
    tm, tn = _tile(s, 1024), _tile(d, 512)
    y = _mm(f"{tag}_down", a, dn_w, mode="nn", grid=(s // tm, d // tn),
            a_spec=pl.BlockSpec((tm, f), lambda i, j: (i, 0)),
            b_spec=pl.BlockSpec((f, tn), lambda i, j: (0, j)),
            o_spec=pl.BlockSpec((tm, tn), lambda i, j: (i, j)),
            out_shape=jax.ShapeDtypeStruct((s, d), F32))
    x_new, h_next = _res_norm(f"{tag}_post", x, y, g_post, FFN_RESIDUAL_WEIGHT, next_gain)
    return x_new, h_next, (x, h, gu, a, y)


class _GradReduce:
    def __init__(self, core, chip, n_layers):
        self.core, self.chip, self.n_layers = core, chip, n_layers
        self.state = {}
        self.bufs = {}

    def start(self, kinds, layer, gs):
        gs, lands, send, recv, token = _swap_start(f"swap_start_{kinds[0]}_{layer}", gs)
        self.state[kinds, layer] = (gs, lands, send, recv)
        return token

    def exchange(self, kinds, layer, after):
        tag = f"{kinds[0]}_{layer}"
        gs, sibs = _swap_wait(f"swap_wait_{tag}", *self.state[kinds, layer], after)
        hs = [_add_core_halves(f"add_cores_{k}_{layer}", g, sib, self.core) for k, g, sib in zip(kinds, gs, sibs)]
        hs, lands, send, recv, token = _scatter_start(f"scatter_start_{tag}", hs)
        self.state[kinds, layer] = (hs, lands, send, recv)
        return token

    def finish(self, kinds, layer, after):
        tag = f"{kinds[0]}_{layer}"
        hs, rcvs = _scatter_wait(f"scatter_wait_{tag}", *self.state.pop((kinds, layer)), after)
        for k, h, rcv in zip(kinds, hs, rcvs):
            self.bufs[k] = _sum_chips(f"sum_chips_{k}_{layer}", h, rcv, self.core, self.chip, layer, self.n_layers,
                                      self.bufs.get(k))
        return self.bufs[kinds[-1]]


def _ffn_backward(tag, dx_new, saved, g_pre, g_post, gu_w, dn_w, red, kinds, layer, deps, head, following,
                  last=False):
    x, h, gu, a, y = saved
    s, d = x.shape
    nb, fs = gu_w.shape[0], gu_w.shape[2]
    f = dn_w.shape[0]
    fr = f // nb
    dy, dg_post = head or _norm_bwd(f"{tag}_post_bwd", dx_new, y, g_post, FFN_RESIDUAL_WEIGHT, None, BF16)
    dgu = _ffn_dact(f"{tag}_dact", dy, dn_w, gu, deps)
    dgu4 = dgu.reshape(nb, s, fs)
    tm, tw = _tile(d, 512), _tile(fs, 1408)
    nw = fs // tw
    d_wgu = _mm(f"{tag}_dwgu", h, dgu4, mode="tn", grid=(nb, nw, d // tm),
                a_spec=pl.BlockSpec((s, tm), lambda k, j, i: (0, i)),
                b_spec=pl.BlockSpec((None, s, tw), lambda k, j, i: (k, 0, j)),
                o_spec=pl.BlockSpec((None, tm, tw), lambda k, j, i: (k, i, j)),
                out_shape=jax.ShapeDtypeStruct((nb, d, fs), BF16))
    first = (red.start(kinds[:1], layer, [d_wgu]),) if last else ()
    tn = _tile(d, 1024)
    d_wd = _mm(f"{tag}_dwd", a, dy, mode="tn", grid=(nb, d // tn),
               a_spec=pl.BlockSpec((s, fr), lambda i, j: (0, i)),
               b_spec=pl.BlockSpec((s, tn), lambda i, j: (0, j)),
               o_spec=pl.BlockSpec((None, fr, tn), lambda i, j: (i, 0, j)),
               out_shape=jax.ShapeDtypeStruct((nb, fr, d), BF16), deps=first)
    if last:
        second = red.start(kinds[1:], layer, [d_wd])
        started = (red.exchange(kinds[:1], layer, second), second)
    else:
        started = (red.start(kinds, layer, [d_wgu, d_wd]),)
    ts, td = _tile(s, 1024), _tile(d, 1024)
    dh = _mm(f"{tag}_dh", dgu4, gu_w, mode="nt", grid=(s // ts, d // td, nb),
             a_spec=pl.BlockSpec((None, ts, fs), lambda i, j, k: (k, i, 0)),
             b_spec=pl.BlockSpec((None, td, fs), lambda i, j, k: (k, j, 0)),
             o_spec=pl.BlockSpec((ts, td), lambda i, j, k: (i, j)),
             out_shape=jax.ShapeDtypeStruct((s, d), F32), nk=nb, acc_shape=(ts, td), deps=started)
    dx, dg_pre, *next_head = _norm_bwd(f"{tag}_pre_bwd", dh, x, g_pre, 1.0, dx_new, F32, following)
    return dx, dg_pre, dg_post, tuple(next_head) or None


def _mixer_forward(tag, x, h, gains, next_gain, feed, k, conv_taps, dims):
    qd, kvd, cd = dims
    s, d = x.shape
    _, g_a, g_c, g_post = gains
    win_w = feed.take(k, h)
    nb, cw = win_w.shape[0], win_w.shape[2]
    tm = _tile(s, 1024)
    z = _mm(f"{tag}_in", h, win_w, mode="nn", grid=(nb, s // tm),
            a_spec=pl.BlockSpec((tm, d), lambda j, i: (i, 0)),
            b_spec=pl.BlockSpec((None, d, cw), lambda j, i: (j, 0, 0)),
            o_spec=pl.BlockSpec((tm, cw), lambda j, i: (i, j)),
            out_shape=jax.ShapeDtypeStruct((s, nb * cw), BF16))
    a, lse = _attn_fwd(f"{tag}_attn", z, qd, kvd)
    c = _conv_fwd(f"{tag}_conv", z, conv_taps, qd + 2 * kvd, cd)
    cat = _cat_norm_fwd(f"{tag}_cat", a, c, g_a, g_c)
    wout_w = feed.take(k + 1, cat).reshape(-1, d)
    mw = qd + cd
    tn = _tile(d, 1024)
    mixed = _mm(f"{tag}_out", cat, wout_w, mode="nn", grid=(s // tm, d // tn),
                a_spec=pl.BlockSpec((tm, mw), lambda i, j: (i, 0)),
                b_spec=pl.BlockSpec((mw, tn), lambda i, j: (0, j)),
                o_spec=pl.BlockSpec((tm, tn), lambda i, j: (i, j)),
                out_shape=jax.ShapeDtypeStruct((s, d), F32))
    x_new, h_next = _res_norm(f"{tag}_post", x, mixed, g_post, 1.0, next_gain)
    return x_new, h_next, (x, h, z, a, lse, c, cat, mixed)


def _mixer_backward(tag, dx_new, saved, gains, win_w, conv_taps, wout_w, dims, red, kinds, layer, deps, head,
                    following):
    qd, kvd, cd = dims
    x, h, z, a, lse, c, cat, mixed = saved
    s, d = x.shape
    nb, cw = win_w.shape[0], win_w.shape[2]
    g_pre, g_a, g_c, g_post = gains
    mw = qd + cd
    dmixed, dg_post = head or _norm_bwd(f"{tag}_post_bwd", dx_new, mixed, g_post, 1.0, None, BF16)
    tm, tn = _tile(s, 1024), _tile(mw, 1024)
    dcat = _mm(f"{tag}_dcat", dmixed, wout_w, mode="nt", grid=(s // tm, mw // tn),
               a_spec=pl.BlockSpec((tm, d), lambda i, j: (i, 0)),
               b_spec=pl.BlockSpec((tn, d), lambda i, j: (j, 0)),
               o_spec=pl.BlockSpec((tm, tn), lambda i, j: (i, j)),
               out_shape=jax.ShapeDtypeStruct((s, mw), F32), deps=deps)
    wr = mw // nb
    td = _tile(d, 1024)
    d_wout = _mm(f"{tag}_dwout", cat, dmixed, mode="tn", grid=(nb, d // td),
                 a_spec=pl.BlockSpec((s, wr), lambda i, j: (0, i)),
                 b_spec=pl.BlockSpec((s, td), lambda i, j: (0, j)),
                 o_spec=pl.BlockSpec((None, wr, td), lambda i, j: (i, 0, j)),
                 out_shape=jax.ShapeDtypeStruct((nb, wr, d), BF16))
    da, dc, dg_a, dg_c = _cat_norm_bwd(f"{tag}_cat_bwd", dcat, a, c, g_a, g_c)
    dhc, dbg, dcg, d_taps = _conv_bwd(f"{tag}_conv_bwd", z, conv_taps, dc, qd + 2 * kvd, cd)
    dq, dk, dv = _attn_bwd(f"{tag}_attn_bwd", z, a, lse, da, qd, kvd)
    dz = jnp.concatenate([dq, dk, dv, dhc, dbg, dcg], axis=1)
    th = _tile(d, 512)
    d_win = _mm(f"{tag}_dwin", h, dz, mode="tn", grid=(nb, d // th),
                a_spec=pl.BlockSpec((s, th), lambda k, i: (0, i)),
                b_spec=pl.BlockSpec((s, cw), lambda k, i: (0, k)),
                o_spec=pl.BlockSpec((None, th, cw), lambda k, i: (k, i, 0)),
                out_shape=jax.ShapeDtypeStruct((nb, d, cw), BF16))
    started = (red.start(kinds, layer, [d_win, d_wout]),)
    dh = _mm(f"{tag}_dh", dz, win_w, mode="nt", grid=(s // tm, d // td, nb),
             a_spec=pl.BlockSpec((tm, cw), lambda i, j, k: (i, k)),
             b_spec=pl.BlockSpec((None, td, cw), lambda i, j, k: (k, j, 0)),
             o_spec=pl.BlockSpec((tm, td), lambda i, j, k: (i, j)),
             out_shape=jax.ShapeDtypeStruct((s, d), F32), nk=nb, acc_shape=(tm, td), deps=started)
    dx, dg_pre, *next_head = _norm_bwd(f"{tag}_pre_bwd", dh, x, g_pre, 1.0, dx_new, F32, following)
    return dx, d_taps, (dg_pre, dg_a, dg_c, dg_post), tuple(next_head) or None


def _pad_cols(v, width):
    return jnp.pad(v, ((0, 0), (0, width - v.shape[1])))


def kernel(x, ffn1_norm_pre, ffn1_w_gate_up, ffn1_w_down, ffn1_norm_post, mix_norm_pre, w_in, conv_w, attn_out_norm, conv_out_norm, w_out, mix_norm_post, ffn2_norm_pre, ffn2_w_gate_up, ffn2_w_down, ffn2_norm_post, loss_target, m_ffn1_norm_pre, m_ffn1_w_gate_up, m_ffn1_w_down, m_ffn1_norm_post, m_mix_norm_pre, m_w_in, m_conv_w, m_attn_out_norm, m_conv_out_norm, m_w_out, m_mix_norm_post, m_ffn2_norm_pre, m_ffn2_w_gate_up, m_ffn2_w_down, m_ffn2_norm_post, v_ffn1_norm_pre, v_ffn1_w_gate_up, v_ffn1_w_down, v_ffn1_norm_post, v_mix_norm_pre, v_w_in, v_conv_w, v_attn_out_norm, v_conv_out_norm, v_w_out, v_mix_norm_post, v_ffn2_norm_pre, v_ffn2_w_gate_up, v_ffn2_w_down, v_ffn2_norm_post):
    _, s, d = x.shape
    n_layers = ffn1_norm_pre.shape[0]
    qd = attn_out_norm.shape[1]
    cd = conv_out_norm.shape[1]
    kvd = qd // Q_PER_KV
    dims = (qd, kvd, cd)
    assert N_CHIPS * w_in.shape[2] == qd + 2 * kvd + 3 * cd and qd + cd == N_CHIPS * w_out.shape[1]
    assert 2 * d <= SMALL_ROWS * LANES * SUBLANES
    chip = 2 * lax.axis_index("x") + lax.axis_index("y")
    chip_arr = chip.astype(jnp.int32).reshape(1)
    core = lax.axis_index("c").astype(jnp.int32).reshape(1)
    kinds = ("gu1", "dn1", "win", "wout", "gu2", "dn2")

    big = (ffn1_w_gate_up, ffn1_w_down, w_in, w_out, ffn2_w_gate_up, ffn2_w_down)
    nk = len(kinds)
    taps_all = _gather_taps(conv_w)
    feed = _WeightFeed()
    order = [(k, w, layer) for layer in range(n_layers) for k, w in zip(kinds, big)]
    k, w, layer = order[0]
    token = feed.start("gather_start_first", [_cast_into_slot(f"cast_{k}_{layer}", w, layer, chip_arr)], taps_all)
    feed.start("gather_start_rest", [_cast_into_slot(f"cast_{k}_{layer}", w, layer, chip_arr, (token,))
                                     for k, w, layer in order[1:]], token)
    taps = jnp.transpose(taps_all, (1, 2, 0, 3)).reshape(n_layers, CONV_WIDTH, cd)
    taps = jnp.pad(taps, ((0, 0), (0, SUBLANES - CONV_WIDTH), (0, 0)))

    def gain(g, layer):
        return g[layer][None, :]

    xs = x[0]
    hs = _norm_fwd("l0_ffn1_norm", xs, gain(ffn1_norm_pre, 0))
    saved = []
    for layer in range(n_layers):
        t = f"l{layer}"
        k0 = layer * nk
        xs, hs, s1 = _ffn_forward(f"{t}_ffn1", xs, hs, gain(ffn1_norm_post, layer), gain(mix_norm_pre, layer), feed, k0)
        mix_gains = (gain(mix_norm_pre, layer), gain(attn_out_norm, layer), gain(conv_out_norm, layer), gain(mix_norm_post, layer))
        xs, hs, s2 = _mixer_forward(f"{t}_mix", xs, hs, mix_gains, gain(ffn2_norm_pre, layer), feed, k0 + 2,
                                    taps[layer], dims)
        following = gain(ffn1_norm_pre, layer + 1) if layer + 1 < n_layers else None
        xs, hs, s3 = _ffn_forward(f"{t}_ffn2", xs, hs, gain(ffn2_norm_post, layer), following, feed, k0 + 4)
        saved.append((s1, s2, s3, mix_gains))
    wts = {k: [feed.fulls[layer * nk + i] for layer in range(n_layers)] for i, k in enumerate(kinds)}
    for k in ("dn1", "wout", "dn2"):
        wts[k] = [w.reshape(-1, d) for w in wts[k]]
    dxs, loss_part = _loss_head("loss_head", xs, loss_target[0])
    loss = lax.psum(jnp.sum(loss_part), ("x", "y", "c"))

    red = _GradReduce(core, chip_arr, n_layers)
    small = [None] * n_layers
    flow = {"deps": (), "in_flight": None}

    def between(dx, group):
        after = dx
        if flow["in_flight"] is not None:
            after = red.finish(*flow["in_flight"], after)
        flow["deps"] = (red.exchange(*group, after),)
        flow["in_flight"] = group

    head = None
    for layer in reversed(range(n_layers)):
        t = f"l{layer}"
        s1, s2, s3, mix_gains = saved[layer]
        after_ffn2 = (s2[7], mix_gains[3], 1.0)
        after_mix = (s1[4], gain(ffn1_norm_post, layer), FFN_RESIDUAL_WEIGHT)
        after_ffn1 = ((saved[layer - 1][2][4], gain(ffn2_norm_post, layer - 1), FFN_RESIDUAL_WEIGHT)
                      if layer > 0 else None)
        dxs, p_pre2, p_post2, head = _ffn_backward(
            f"{t}_ffn2", dxs, s3, gain(ffn2_norm_pre, layer), gain(ffn2_norm_post, layer),
            wts["gu2"][layer], wts["dn2"][layer], red, ("gu2", "dn2"), layer, flow["deps"], head, after_ffn2)
        between(dxs, (("gu2", "dn2"), layer))
        dxs, p_taps, (p_mpre, p_a, p_c, p_mpost), head = _mixer_backward(
            f"{t}_mix", dxs, s2, mix_gains, wts["win"][layer], taps[layer], wts["wout"][layer], dims,
            red, ("win", "wout"), layer, flow["deps"], head, after_mix)
        between(dxs, (("win", "wout"), layer))
        dxs, p_pre1, p_post1, head = _ffn_backward(
            f"{t}_ffn1", dxs, s1, gain(ffn1_norm_pre, layer), gain(ffn1_norm_post, layer),
            wts["gu1"][layer], wts["dn1"][layer], red, ("gu1", "dn1"), layer, flow["deps"], head, after_ffn1,
            last=layer == 0)
        between(dxs, (("dn1",) if layer == 0 else ("gu1", "dn1"), layer))
        tap_rows = jnp.zeros((CONV_WIDTH, SUBLANES, d), F32).at[:, 0, :cd].set(p_taps[:CONV_WIDTH])
        rows = [p_pre1, p_post1, p_mpre, jnp.concatenate([p_a, p_c], axis=1), p_mpost, p_pre2, p_post2]
        rows = jnp.concatenate([jnp.stack(rows), tap_rows], axis=0)
        small[layer] = jnp.pad(rows, ((0, SMALL_ROWS - rows.shape[0]), (0, 0), (0, 0)))
    grad_x = dxs[None]

    weights = dict(ffn1_norm_pre=ffn1_norm_pre, ffn1_w_gate_up=ffn1_w_gate_up, ffn1_w_down=ffn1_w_down, ffn1_norm_post=ffn1_norm_post, mix_norm_pre=mix_norm_pre, w_in=w_in, conv_w=conv_w, attn_out_norm=attn_out_norm, conv_out_norm=conv_out_norm, w_out=w_out, mix_norm_post=mix_norm_post, ffn2_norm_pre=ffn2_norm_pre, ffn2_w_gate_up=ffn2_w_gate_up, ffn2_w_down=ffn2_w_down, ffn2_norm_post=ffn2_norm_post)
    m_in = dict(ffn1_norm_pre=m_ffn1_norm_pre, ffn1_w_gate_up=m_ffn1_w_gate_up, ffn1_w_down=m_ffn1_w_down, ffn1_norm_post=m_ffn1_norm_post, mix_norm_pre=m_mix_norm_pre, w_in=m_w_in, conv_w=m_conv_w, attn_out_norm=m_attn_out_norm, conv_out_norm=m_conv_out_norm, w_out=m_w_out, mix_norm_post=m_mix_norm_post, ffn2_norm_pre=m_ffn2_norm_pre, ffn2_w_gate_up=m_ffn2_w_gate_up, ffn2_w_down=m_ffn2_w_down, ffn2_norm_post=m_ffn2_norm_post)
    v_in = dict(ffn1_norm_pre=v_ffn1_norm_pre, ffn1_w_gate_up=v_ffn1_w_gate_up, ffn1_w_down=v_ffn1_w_down, ffn1_norm_post=v_ffn1_norm_post, mix_norm_pre=v_mix_norm_pre, w_in=v_w_in, conv_w=v_conv_w, attn_out_norm=v_attn_out_norm, conv_out_norm=v_conv_out_norm, w_out=v_w_out, mix_norm_post=v_mix_norm_post, ffn2_norm_pre=v_ffn2_norm_pre, ffn2_w_gate_up=v_ffn2_w_gate_up, ffn2_w_down=v_ffn2_w_down, ffn2_norm_post=v_ffn2_norm_post)
    kind_name = dict(gu1="ffn1_w_gate_up", dn1="ffn1_w_down", win="w_in", wout="w_out", gu2="ffn2_w_gate_up", dn2="ffn2_w_down")
    delta, new_m, new_v, grad = {}, {}, {}, {}

    def join_and_update(name, kind_list, deps, after):
        ts, send_sems, recv_sems = _join_start(name, [red.bufs[k] for k in kind_list], deps)
        for a, k in enumerate(kind_list):
            n = kind_name[k]
            g = _join_wait(f"join_wait_{k}", ts[a], a, send_sems, recv_sems, after)
            delta[n], new_m[n], new_v[n], grad[n] = _adamw(f"adamw_{n}", weights[n], g, m_in[n], v_in[n], True)
            after = delta[n]

    early = ("wout", "win", "dn2", "gu2")
    join_and_update("join_early", early, flow["deps"], dxs)
    done_early = [delta[kind_name[e]] for e in early]
    for group in ((("gu1",), 0), flow["in_flight"]):
        red.finish(*group, done_early)
    join_and_update("join_late", ("dn1", "gu1"), (), done_early[-1])

    small_sum = _allreduce_small(jnp.concatenate(small, axis=0)).reshape(n_layers, SMALL_ROWS, d)
    g_ffn1_pre, g_ffn1_post, g_mix_pre = small_sum[:, 0], small_sum[:, 1], small_sum[:, 2]
    g_attn_out, g_conv_out = small_sum[:, 3, :qd], small_sum[:, 3, qd:qd + cd]
    g_mix_post, g_ffn2_pre, g_ffn2_post = small_sum[:, 4], small_sum[:, 5], small_sum[:, 6]
    cc = conv_w.shape[2]
    g_conv = lax.dynamic_slice_in_dim(small_sum[:, 7:7 + CONV_WIDTH, :cd], chip * cc, cc, axis=2)

    grad.update(ffn1_norm_pre=g_ffn1_pre, ffn1_norm_post=g_ffn1_post, mix_norm_pre=g_mix_pre, conv_w=g_conv, attn_out_norm=g_attn_out, conv_out_norm=g_conv_out, mix_norm_post=g_mix_post, ffn2_norm_pre=g_ffn2_pre, ffn2_norm_post=g_ffn2_post)
    names = list(weights)

    vectors = [n for n in names if n not in kind_name.values()]

    def pack(tree):
        flat = jnp.concatenate([tree[n].reshape(-1) for n in vectors])
        return jnp.pad(flat, (0, -flat.size % (SUBLANES * LANES))).reshape(-1, LANES)

    packed = _adamw("adamw_small", pack(weights), pack(grad), pack(m_in), pack(v_in))
    offset = 0
    for n in vectors:
        size = weights[n].size
        for tree, flat in zip((delta, new_m, new_v), packed):
            tree[n] = flat.reshape(-1)[offset:offset + size].reshape(weights[n].shape)
        offset += size

    return (loss, grad_x, *[grad[n] for n in names], *[delta[n] for n in names],
            *[new_m[n] for n in names], *[new_v[n] for n in names])
```

```python
import functools

import jax
import jax.numpy as jnp
from jax import lax
from jax.experimental import pallas as pl
from jax.experimental.pallas import tpu as pltpu

F32 = jnp.float32
BF16 = jnp.bfloat16
MESH = pl.DeviceIdType.MESH

NORM_EPS = 1e-6
HEAD_DIM = 128
Q_PER_KV = 4
CONV_WIDTH = 3
FFN_RESIDUAL_WEIGHT = 0.5
DILATED_BRANCHES = ((128, 1), (512, 4), (2048, 16))
ADAM_LR = 0.001
ADAM_B1 = 0.9
ADAM_B2 = 0.999
ADAM_EPS = 1e-08
ADAM_WD = 0.01
ADAM_STEP = 10

N_CHIPS = 4
N_DEV = 8
V7X_VMEM_BYTES = 64 << 20
VMEM_LIMIT = V7X_VMEM_BYTES - (12 << 20)
SUBLANES = 8
LANES = 128
SMALL_ROWS = 16
BIG_BLOCK = 4 << 20


def _params(*sem):
    return pltpu.CompilerParams(dimension_semantics=sem, vmem_limit_bytes=VMEM_LIMIT)


def _row_tile(rows, cols, itemsize=4, budget=2 << 20):
    t = rows
    while t * cols * itemsize > budget and t % 32 == 0:
        t //= 2
    return t


def _sum_to_sublanes(v):
    r, n = v.shape
    return v.reshape(r // SUBLANES, SUBLANES, n).sum(axis=0)


_DIMS = {
    "nn": (((1,), (0,)), ((), ())),
    "nt": (((1,), (1,)), ((), ())),
    "tn": (((0,), (0,)), ((), ())),
}


ANY_SPEC = pl.BlockSpec(memory_space=pl.ANY)


def _dot(a, b, mode):
    return lax.dot_general(a, b, _DIMS[mode], preferred_element_type=F32)


def _mm(name, a, b, *, mode, grid, a_spec, b_spec, o_spec, out_shape, nk=1, acc_shape=None, deps=()):
    nd = len(deps)

    def body(a_ref, b_ref, *rest):
        o_ref, scratch = rest[nd], rest[nd + 1:]
        r = _dot(a_ref[...], b_ref[...], mode)
        if nk == 1:
            o_ref[...] = r.astype(o_ref.dtype)
        else:
            acc = scratch[0]
            k = pl.program_id(len(grid) - 1)

            @pl.when(k == 0)
            def _():
                acc[...] = r

            @pl.when(k > 0)
            def _():
                acc[...] += r

            @pl.when(k == nk - 1)
            def _():
                o_ref[...] = acc[...].astype(o_ref.dtype)

    sem = ("parallel",) * (len(grid) - (1 if nk > 1 else 0)) + (("arbitrary",) if nk > 1 else ())
    return pl.pallas_call(
        body, name=name, grid=grid, in_specs=[a_spec, b_spec] + [ANY_SPEC] * nd, out_specs=o_spec,
        out_shape=out_shape, scratch_shapes=[pltpu.VMEM(acc_shape, F32)] if nk > 1 else [],
        compiler_params=_params(*sem),
    )(a, b, *deps)


def _tile(n, want):
    if n <= want:
        return n
    best = None
    for t in range(LANES, want + 1, LANES):
        if n % t == 0:
            best = t
    assert best is not None, (n, want)
    return best


def _norm_fwd(name, x, gain):
    s, d = x.shape
    tr = _row_tile(s, d, budget=BIG_BLOCK)

    def body(x_ref, g_ref, o_ref):
        xv = x_ref[...]
        r = lax.rsqrt(jnp.mean(xv * xv, axis=-1, keepdims=True) + NORM_EPS)
        o_ref[...] = (xv * r * g_ref[...]).astype(o_ref.dtype)

    return pl.pallas_call(
        body, name=name, grid=(s // tr,),
        in_specs=[pl.BlockSpec((tr, d), lambda i: (i, 0)), pl.BlockSpec((1, d), lambda i: (0, 0))],
        out_specs=pl.BlockSpec((tr, d), lambda i: (i, 0)),
        out_shape=jax.ShapeDtypeStruct((s, d), BF16), compiler_params=_params("parallel"),
    )(x, gain)


def _res_norm(name, x, y, gain, scale, next_gain=None):
    s, d = x.shape
    tr = _row_tile(s, d, budget=BIG_BLOCK)
    with_next = next_gain is not None

    def body(x_ref, y_ref, g_ref, *rest):
        yv = y_ref[...]
        r = lax.rsqrt(jnp.mean(yv * yv, axis=-1, keepdims=True) + NORM_EPS)
        xn = x_ref[...] + scale * (yv * r * g_ref[...])
        if with_next:
            ng_ref, o_ref, h_ref = rest
            rn = lax.rsqrt(jnp.mean(xn * xn, axis=-1, keepdims=True) + NORM_EPS)
            h_ref[...] = (xn * rn * ng_ref[...]).astype(h_ref.dtype)
        else:
            o_ref, = rest
        o_ref[...] = xn

    row = pl.BlockSpec((tr, d), lambda i: (i, 0))
    vec = pl.BlockSpec((1, d), lambda i: (0, 0))
    outs = pl.pallas_call(
        body, name=name, grid=(s // tr,),
        in_specs=[row, row, vec] + ([vec] if with_next else []), out_specs=[row] * (2 if with_next else 1),
        out_shape=[jax.ShapeDtypeStruct((s, d), F32)] + ([jax.ShapeDtypeStruct((s, d), BF16)] if with_next else []),
        compiler_params=_params("parallel"),
    )(x, y, gain, *((next_gain,) if with_next else ()))
    return (outs[0], outs[1]) if with_next else (outs[0], None)


def _rms_bwd(dn, yv, gv):
    r = lax.rsqrt(jnp.mean(yv * yv, axis=-1, keepdims=True) + NORM_EPS)
    xhat = yv * r
    dxn = dn * gv
    return r * (dxn - xhat * jnp.mean(dxn * xhat, axis=-1, keepdims=True)), _sum_to_sublanes(dn * xhat)


def _accumulate(ref, part):
    @pl.when(pl.program_id(0) == 0)
    def _():
        ref[...] = part

    @pl.when(pl.program_id(0) > 0)
    def _():
        ref[...] += part


def _norm_bwd(name, dout, yin, gain, scale, resid, out_dtype, following=None):
    s, d = yin.shape
    tr = _row_tile(s, d)
    has_resid = resid is not None
    chained = following is not None

    def body(*refs):
        refs = list(refs)
        do_ref, y_ref, g_ref = refs[:3]
        del refs[:3]
        r_ref = refs.pop(0) if has_resid else None
        if chained:
            y2_ref, g2_ref = refs[:2]
            del refs[:2]
        di_ref, dg_ref = refs[:2]
        din, part = _rms_bwd(scale * do_ref[...], y_ref[...], g_ref[...])
        _accumulate(dg_ref, part)
        if has_resid:
            din = din + r_ref[...]
        di_ref[...] = din.astype(di_ref.dtype)
        if chained:
            d2_ref, dg2_ref = refs[2:]
            d2, part2 = _rms_bwd(following[2] * din, y2_ref[...], g2_ref[...])
            _accumulate(dg2_ref, part2)
            d2_ref[...] = d2.astype(d2_ref.dtype)

    row = pl.BlockSpec((tr, d), lambda i: (i, 0))
    vec = pl.BlockSpec((1, d), lambda i: (0, 0))
    acc = pl.BlockSpec((SUBLANES, d), lambda i: (0, 0))
    ins = [row, row, vec] + ([row] if has_resid else []) + ([row, vec] if chained else [])
    args = (dout, yin, gain) + ((resid,) if has_resid else ()) + (tuple(following[:2]) if chained else ())
    outs = [row, acc] + ([row, acc] if chained else [])
    shapes = [jax.ShapeDtypeStruct((s, d), out_dtype), jax.ShapeDtypeStruct((SUBLANES, d), F32)]
    if chained:
        shapes += [jax.ShapeDtypeStruct((s, d), BF16), jax.ShapeDtypeStruct((SUBLANES, d), F32)]
    return pl.pallas_call(
        body, name=name, grid=(s // tr,), in_specs=ins, out_specs=outs, out_shape=shapes,
        compiler_params=_params("arbitrary"),
    )(*args)


def _loss_head(name, y, target):
    s, d = y.shape
    tr = _row_tile(s, d)

    def body(y_ref, t_ref, dy_ref, l_ref):
        e = y_ref[...] - t_ref[...]
        dy_ref[...] = e * (1.0 / d)
        part = _sum_to_sublanes(e * e) * (0.5 / d)

        @pl.when(pl.program_id(0) == 0)
        def _():
            l_ref[...] = part

        @pl.when(pl.program_id(0) > 0)
        def _():
            l_ref[...] += part

    row = pl.BlockSpec((tr, d), lambda i: (i, 0))
    return pl.pallas_call(
        body, name=name, grid=(s // tr,), in_specs=[row, row],
        out_specs=[row, pl.BlockSpec((SUBLANES, d), lambda i: (0, 0))],
        out_shape=[jax.ShapeDtypeStruct((s, d), F32), jax.ShapeDtypeStruct((SUBLANES, d), F32)],
        compiler_params=_params("arbitrary"),
    )(y, target)


def _ffn_up(name, h, gu_w):
    s, d = h.shape
    nb, _, fs = gu_w.shape
    hb = nb // 2
    w = gu_w.reshape(2, hb, d, fs)
    tm = _tile(s, 512)
    tn = _tile(fs, 1408)
    nj = fs // tn

    def body(h_ref, w_ref, gu_ref, a_ref):
        hv = h_ref[...]
        g = _dot(hv, w_ref[0], "nn")
        u = _dot(hv, w_ref[1], "nn")
        sg = jax.nn.sigmoid(g)
        silu = g * sg
        gu_ref[0] = (u * (sg * (1.0 + g * (1.0 - sg)))).astype(gu_ref.dtype)
        gu_ref[1] = silu.astype(gu_ref.dtype)
        a_ref[...] = (silu * u).astype(a_ref.dtype)

    return pl.pallas_call(
        body, name=name, grid=(hb, nj, s // tm),
        in_specs=[pl.BlockSpec((tm, d), lambda jb, jo, i: (i, 0)),
                  pl.BlockSpec((2, None, d, tn), lambda jb, jo, i: (0, jb, 0, jo))],
        out_specs=[pl.BlockSpec((2, None, tm, tn), lambda jb, jo, i: (0, jb, i, jo)),
                   pl.BlockSpec((tm, tn), lambda jb, jo, i: (i, jb * nj + jo))],
        out_shape=[jax.ShapeDtypeStruct((2, hb, s, fs), BF16), jax.ShapeDtypeStruct((s, hb * fs), BF16)],
        compiler_params=_params("parallel", "parallel", "parallel"),
    )(h, w)


def _ffn_dact(name, dy, dn_w, gu, deps=()):
    s, d = dy.shape
    _, hb, _, fs = gu.shape
    tm = _tile(s, 512)
    tn = _tile(fs, 1408)
    nj = fs // tn

    def body(dy_ref, w_ref, gu_ref, *rest):
        o_ref = rest[-1]
        wv = w_ref[...]
        parts = 2 if tm % (2 * SUBLANES * 2) == 0 else 1
        for r in range(parts):
            rows = slice(r * (tm // parts), (r + 1) * (tm // parts))
            da = _dot(dy_ref[rows, :], wv, "nt")
            o_ref[0, rows, :] = (da * gu_ref[0, rows, :].astype(F32)).astype(o_ref.dtype)
            o_ref[1, rows, :] = (da * gu_ref[1, rows, :].astype(F32)).astype(o_ref.dtype)

    blk = pl.BlockSpec((2, None, tm, tn), lambda jb, jo, i: (0, jb, i, jo))
    return pl.pallas_call(
        body, name=name, grid=(hb, nj, s // tm),
        in_specs=[pl.BlockSpec((tm, d), lambda jb, jo, i: (i, 0)),
                  pl.BlockSpec((tn, d), lambda jb, jo, i: (jb * nj + jo, 0)),
                  blk] + [ANY_SPEC] * len(deps),
        out_specs=blk, out_shape=jax.ShapeDtypeStruct(gu.shape, BF16),
        compiler_params=_params("parallel", "parallel", "parallel"),
    )(dy, dn_w, gu, *deps)


_MASKED = -1e30


def _attn_bias(s, tq):
    nd = s // tq
    dist = (jnp.arange(nd)[:, None, None] * tq + jnp.arange(tq)[None, :, None]) - jnp.arange(tq)[None, None, :]
    mult = jnp.zeros(dist.shape, F32)
    for window, dilation in DILATED_BRANCHES:
        mult = mult + ((dist >= 0) & (dist <= window) & (dist % dilation == 0)).astype(F32)
    return jnp.where(mult > 0.0, jnp.log(jnp.maximum(mult, 1.0)), _MASKED)


def _biased(sc, bias, scale):
    tq, tk = bias.shape
    return (sc.reshape(-1, tq, tk) * scale + bias[None]).reshape(sc.shape)


def _attn_specs(s, qd, kvd, tq):
    rw = Q_PER_KV * HEAD_DIM
    qspec = pl.BlockSpec((tq, rw), lambda g, i: (i, g))
    kspec = pl.BlockSpec((s, HEAD_DIM), lambda g, i: (0, qd // HEAD_DIM + g))
    vspec = pl.BlockSpec((s, HEAD_DIM), lambda g, i: (0, (qd + kvd) // HEAD_DIM + g))
    return rw, qspec, kspec, vspec


def _attn_fwd(name, z, qd, kvd):
    s = z.shape[0]
    tq = _tile(s, 256)
    nkv = kvd // HEAD_DIM
    rw, qspec, kspec, vspec = _attn_specs(s, qd, kvd, tq)
    scale = HEAD_DIM ** -0.5

    def body(q_ref, k_ref, v_ref, b_ref, o_ref, l_ref):
        i = pl.program_id(1)
        heads = [slice(h * HEAD_DIM, (h + 1) * HEAD_DIM) for h in range(Q_PER_KV)]
        q_all = jnp.concatenate([q_ref[:, cols] for cols in heads], axis=0)

        def chunk(j, carry):
            mx, den, acc = carry
            k0 = pl.multiple_of(j * tq, tq)
            kc, vc = k_ref[pl.ds(k0, tq), :], v_ref[pl.ds(k0, tq), :]
            sc = _biased(_dot(q_all, kc, "nt"), b_ref[i - j], scale)
            mx_new = jnp.maximum(mx, jnp.max(sc, axis=-1, keepdims=True))
            alpha = jnp.exp(mx - mx_new)
            p = jnp.exp(sc - mx_new)
            return (mx_new, alpha * den + jnp.sum(p, axis=-1, keepdims=True),
                    alpha * acc + _dot(p.astype(BF16), vc, "nn"))

        rows = Q_PER_KV * tq
        init = (jnp.full((rows, 1), _MASKED, F32), jnp.zeros((rows, 1), F32), jnp.zeros((rows, HEAD_DIM), F32))
        mx, den, acc = lax.fori_loop(0, i + 1, chunk, init)
        out = acc / den
        lse = mx + jnp.log(den)
        for h, cols in enumerate(heads):
            o_ref[:, cols] = out[h * tq:(h + 1) * tq]
            l_ref[:, cols] = jnp.broadcast_to(lse[h * tq:(h + 1) * tq], (tq, HEAD_DIM))

    bias = _attn_bias(s, tq)
    return pl.pallas_call(
        body, name=name, grid=(nkv, s // tq),
        in_specs=[qspec, kspec, vspec, pl.BlockSpec(bias.shape, lambda g, i: (0, 0, 0))], out_specs=[qspec, qspec],
        out_shape=[jax.ShapeDtypeStruct((s, qd), F32), jax.ShapeDtypeStruct((s, qd), F32)],
        compiler_params=_params("parallel", "parallel"),
    )(z, z, z, bias)


def _attn_bwd(name, z, o, lse, do, qd, kvd):
    s = z.shape[0]
    tq = _tile(s, 256)
    nkv = kvd // HEAD_DIM
    nq = s // tq
    rw, qspec, kspec, vspec = _attn_specs(s, qd, kvd, tq)
    scale = HEAD_DIM ** -0.5

    def body(q_ref, k_ref, v_ref, o_ref, l_ref, do_ref, b_ref, dq_ref, dk_ref, dv_ref, dk_acc, dv_acc):
        i = pl.program_id(1)
        heads = [slice(h * HEAD_DIM, (h + 1) * HEAD_DIM) for h in range(Q_PER_KV)]

        @pl.when(i == 0)
        def _():
            dk_acc[...] = jnp.zeros_like(dk_acc)
            dv_acc[...] = jnp.zeros_like(dv_acc)

        q_all = jnp.concatenate([q_ref[:, cols] for cols in heads], axis=0)
        do_all = jnp.concatenate([do_ref[:, cols].astype(BF16) for cols in heads], axis=0)
        lse_all = jnp.concatenate([l_ref[:, cols][:, :1] for cols in heads], axis=0)
        delta_all = jnp.concatenate(
            [jnp.sum(do_ref[:, cols] * o_ref[:, cols], axis=-1, keepdims=True) for cols in heads], axis=0)

        def chunk(j, dq):
            k0 = pl.multiple_of(j * tq, tq)
            kc, vc = k_ref[pl.ds(k0, tq), :], v_ref[pl.ds(k0, tq), :]
            p = jnp.exp(_biased(_dot(q_all, kc, "nt"), b_ref[i - j], scale) - lse_all)
            ds = (p * (_dot(do_all, vc, "nt") - delta_all) * scale).astype(BF16)
            dk_acc[pl.ds(k0, tq), :] += _dot(ds, q_all, "tn")
            dv_acc[pl.ds(k0, tq), :] += _dot(p.astype(BF16), do_all, "tn")
            return dq + _dot(ds, kc, "nn")

        dq = lax.fori_loop(0, i + 1, chunk, jnp.zeros((Q_PER_KV * tq, HEAD_DIM), F32))
        for h, cols in enumerate(heads):
            dq_ref[:, cols] = dq[h * tq:(h + 1) * tq].astype(dq_ref.dtype)

        @pl.when(i == nq - 1)
        def _():
            dk_ref[...] = dk_acc[...].astype(dk_ref.dtype)
            dv_ref[...] = dv_acc[...].astype(dv_ref.dtype)

    kvout = pl.BlockSpec((s, HEAD_DIM), lambda g, i: (0, g))
    bias = _attn_bias(s, tq)
    return pl.pallas_call(
        body, name=name, grid=(nkv, nq),
        in_specs=[qspec, kspec, vspec, qspec, qspec, qspec, pl.BlockSpec(bias.shape, lambda g, i: (0, 0, 0))],
        out_specs=[qspec, kvout, kvout],
        out_shape=[jax.ShapeDtypeStruct((s, qd), BF16), jax.ShapeDtypeStruct((s, kvd), BF16),
                   jax.ShapeDtypeStruct((s, kvd), BF16)],
        scratch_shapes=[pltpu.VMEM((s, HEAD_DIM), F32), pltpu.VMEM((s, HEAD_DIM), F32)],
        compiler_params=_params("parallel", "arbitrary"),
    )(z, z, z, o, lse, do, bias)


def _shift_down(v, n):
    rolled = pltpu.roll(v, n, 0)
    t = lax.broadcasted_iota(jnp.int32, v.shape, 0)
    return jnp.where(t >= n, rolled, 0.0)


def _shift_up(v, n):
    rows = v.shape[0]
    rolled = pltpu.roll(v, rows - n, 0)
    t = lax.broadcasted_iota(jnp.int32, v.shape, 0)
    return jnp.where(t < rows - n, rolled, 0.0)


def _conv_specs(s, base, cd, tc):
    zs = [pl.BlockSpec((s, tc), functools.partial(lambda j, off: (0, off + j), off=(base + n * cd) // tc))
          for n in range(3)]
    wspec = pl.BlockSpec((SUBLANES, tc), lambda j: (0, j))
    cspec = pl.BlockSpec((s, tc), lambda j: (0, j))
    return zs, wspec, cspec


def _conv_fwd(name, z, conv_w, base, cd):
    s = z.shape[0]
    tc = _tile(cd, 256)
    zs, wspec, cspec = _conv_specs(s, base, cd, tc)

    def body(h_ref, b_ref, c_ref, w_ref, o_ref):
        u = c_ref[...].astype(F32) * h_ref[...].astype(F32)
        y = w_ref[0:1, :] * _shift_down(u, 2) + w_ref[1:2, :] * _shift_down(u, 1) + w_ref[2:3, :] * u
        o_ref[...] = b_ref[...].astype(F32) * y

    return pl.pallas_call(
        body, name=name, grid=(cd // tc,), in_specs=zs + [wspec], out_specs=cspec,
        out_shape=jax.ShapeDtypeStruct((s, cd), F32), compiler_params=_params("parallel"),
    )(z, z, z, conv_w)


def _conv_bwd(name, z, conv_w, dc, base, cd):
    s = z.shape[0]
    tc = _tile(cd, 256)
    zs, wspec, cspec = _conv_specs(s, base, cd, tc)

    def body(h_ref, b_ref, c_ref, w_ref, dc_ref, dh_ref, db_ref, dcg_ref, dw_ref):
        hv, bv, cv = h_ref[...].astype(F32), b_ref[...].astype(F32), c_ref[...].astype(F32)
        u = cv * hv
        u1, u2 = _shift_down(u, 1), _shift_down(u, 2)
        w0, w1, w2 = w_ref[0:1, :], w_ref[1:2, :], w_ref[2:3, :]
        y = w0 * u2 + w1 * u1 + w2 * u
        dcv = dc_ref[...]
        db_ref[...] = (dcv * y).astype(db_ref.dtype)
        dy = dcv * bv
        du = w2 * dy + w1 * _shift_up(dy, 1) + w0 * _shift_up(dy, 2)
        dh_ref[...] = (du * cv).astype(dh_ref.dtype)
        dcg_ref[...] = (du * hv).astype(dcg_ref.dtype)
        g0 = jnp.sum(dy * u2, axis=0, keepdims=True)
        g1 = jnp.sum(dy * u1, axis=0, keepdims=True)
        g2 = jnp.sum(dy * u, axis=0, keepdims=True)
        r = lax.broadcasted_iota(jnp.int32, (SUBLANES, tc), 0)
        dw_ref[...] = jnp.where(r == 0, g0, jnp.where(r == 1, g1, jnp.where(r == 2, g2, 0.0)))

    return pl.pallas_call(
        body, name=name, grid=(cd // tc,), in_specs=zs + [wspec, cspec],
        out_specs=[cspec, cspec, cspec, wspec],
        out_shape=[jax.ShapeDtypeStruct((s, cd), BF16)] * 3 + [jax.ShapeDtypeStruct((SUBLANES, cd), F32)],
        compiler_params=_params("parallel"),
    )(z, z, z, conv_w, dc)


def _cat_norm_fwd(name, a, c, ga, gc):
    s, qd = a.shape
    cd = c.shape[1]
    tr = _row_tile(s, qd + cd)

    def body(a_ref, c_ref, ga_ref, gc_ref, o_ref):
        av, cv = a_ref[...], c_ref[...]
        ra = lax.rsqrt(jnp.mean(av * av, axis=-1, keepdims=True) + NORM_EPS)
        rc = lax.rsqrt(jnp.mean(cv * cv, axis=-1, keepdims=True) + NORM_EPS)
        o_ref[:, :qd] = (av * ra * ga_ref[...]).astype(o_ref.dtype)
        o_ref[:, qd:] = (cv * rc * gc_ref[...]).astype(o_ref.dtype)

    return pl.pallas_call(
        body, name=name, grid=(s // tr,),
        in_specs=[pl.BlockSpec((tr, qd), lambda i: (i, 0)), pl.BlockSpec((tr, cd), lambda i: (i, 0)),
                  pl.BlockSpec((1, qd), lambda i: (0, 0)), pl.BlockSpec((1, cd), lambda i: (0, 0))],
        out_specs=pl.BlockSpec((tr, qd + cd), lambda i: (i, 0)),
        out_shape=jax.ShapeDtypeStruct((s, qd + cd), BF16), compiler_params=_params("parallel"),
    )(a, c, ga, gc)


def _cat_norm_bwd(name, dcat, a, c, ga, gc):
    s, qd = a.shape
    cd = c.shape[1]
    tr = _row_tile(s, qd + cd)

    def one(dn, yv, gv):
        r = lax.rsqrt(jnp.mean(yv * yv, axis=-1, keepdims=True) + NORM_EPS)
        xhat = yv * r
        dxn = dn * gv
        return r * (dxn - xhat * jnp.mean(dxn * xhat, axis=-1, keepdims=True)), _sum_to_sublanes(dn * xhat)

    def body(d_ref, a_ref, c_ref, ga_ref, gc_ref, da_ref, dc_ref, dga_ref, dgc_ref):
        da, pa = one(d_ref[:, :qd], a_ref[...], ga_ref[...])
        dc, pc = one(d_ref[:, qd:], c_ref[...], gc_ref[...])
        da_ref[...] = da
        dc_ref[...] = dc

        @pl.when(pl.program_id(0) == 0)
        def _():
            dga_ref[...] = pa
            dgc_ref[...] = pc

        @pl.when(pl.program_id(0) > 0)
        def _():
            dga_ref[...] += pa
            dgc_ref[...] += pc

    ra = pl.BlockSpec((tr, qd), lambda i: (i, 0))
    rc = pl.BlockSpec((tr, cd), lambda i: (i, 0))
    return pl.pallas_call(
        body, name=name, grid=(s // tr,),
        in_specs=[pl.BlockSpec((tr, qd + cd), lambda i: (i, 0)), ra, rc,
                  pl.BlockSpec((1, qd), lambda i: (0, 0)), pl.BlockSpec((1, cd), lambda i: (0, 0))],
        out_specs=[ra, rc, pl.BlockSpec((SUBLANES, qd), lambda i: (0, 0)),
                   pl.BlockSpec((SUBLANES, cd), lambda i: (0, 0))],
        out_shape=[jax.ShapeDtypeStruct((s, qd), F32), jax.ShapeDtypeStruct((s, cd), F32),
                   jax.ShapeDtypeStruct((SUBLANES, qd), F32), jax.ShapeDtypeStruct((SUBLANES, cd), F32)],
        compiler_params=_params("arbitrary"),
    )(dcat, a, c, ga, gc)


def _adamw(name, w, g, m, v, emit_grad=False):
    shape = w.shape
    cols = shape[-1]
    rows = w.size // cols
    tr = _row_tile(rows, cols, budget=3 << 19)
    bc1 = 1.0 - ADAM_B1 ** ADAM_STEP
    bc2 = 1.0 - ADAM_B2 ** ADAM_STEP
    n_out = 4 if emit_grad else 3

    def body(w_ref, g_ref, m_ref, v_ref, d_ref, nm_ref, nv_ref, *g_out):
        gv = g_ref[...]
        mv = ADAM_B1 * m_ref[...] + (1.0 - ADAM_B1) * gv
        vv = ADAM_B2 * v_ref[...] + (1.0 - ADAM_B2) * (gv * gv)
        nm_ref[...] = mv
        nv_ref[...] = vv
        d_ref[...] = -ADAM_LR * ((mv / bc1) / (jnp.sqrt(vv / bc2) + ADAM_EPS) + ADAM_WD * w_ref[...])
        for ref in g_out:
            ref[...] = gv

    row = pl.BlockSpec((tr, cols), lambda i: (i, 0))
    outs = pl.pallas_call(
        body, name=name, grid=(rows // tr,), in_specs=[row] * 4, out_specs=[row] * n_out,
        out_shape=[jax.ShapeDtypeStruct((rows, cols), F32)] * n_out, compiler_params=_params("parallel"),
    )(*(t.reshape(rows, cols) for t in (w, g, m, v)))
    return tuple(t.reshape(shape) for t in outs)


HBM_SPEC = pl.BlockSpec(memory_space=pltpu.HBM)


def _mesh_place():
    x, y, c = lax.axis_index("x"), lax.axis_index("y"), lax.axis_index("c")
    other_chips = [(1 - x, y), (x, 1 - y), (1 - x, 1 - y)]
    return x, y, c, other_chips


def _cast_into_slot(name, w, layer, chip, deps=()):
    _, r, cols = w.shape
    tr = _row_tile(r, cols, budget=BIG_BLOCK)

    def body(chip_ref, w_ref, *rest):
        o_ref = rest[-1]
        o_ref[...] = w_ref[...].astype(o_ref.dtype)

    return pl.pallas_call(
        body, name=name,
        grid_spec=pltpu.PrefetchScalarGridSpec(
            num_scalar_prefetch=1, grid=(r // tr,),
            in_specs=[pl.BlockSpec((None, tr, cols), lambda i, chip_ref: (layer, i, 0))] + [ANY_SPEC] * len(deps),
            out_specs=pl.BlockSpec((None, tr, cols), lambda i, chip_ref: (chip_ref[0], i, 0))),
        out_shape=jax.ShapeDtypeStruct((N_CHIPS, r, cols), BF16), compiler_params=_params("parallel"),
    )(chip, w, *deps)


SEM_SPEC = pl.BlockSpec(memory_space=pltpu.SEMAPHORE)
SPLIT_COPY = pltpu.CompilerParams(has_side_effects=pltpu.SideEffectType.DATAFLOW_SIDE_EFFECTING)
N_OTHER = N_CHIPS - 1
TOKEN_SPEC = pl.BlockSpec(memory_space=pltpu.VMEM)
TOKEN_SHAPE = jax.ShapeDtypeStruct((SUBLANES, LANES), F32)


def _in_hbm(arr):
    return pltpu.with_memory_space_constraint(arr, pltpu.HBM)


def _half_rows(ref, chip_idx, core):
    r2 = ref.shape[1] // 2
    return ref.at[chip_idx, pl.ds(core * r2, r2), :]


def _gather_start(name, fulls, after):
    na = len(fulls)

    def body(*refs):
        f_refs = refs[na + 1:2 * na + 1]
        send_sems, recv_sems = refs[2 * na + 1:3 * na + 1], refs[3 * na + 1:4 * na + 1]
        token = refs[4 * na + 1]
        x, y, c, chips = _mesh_place()
        for a in range(na):
            mine = _half_rows(f_refs[a], 2 * x + y, c)
            for j, (cx, cy) in enumerate(chips):
                pltpu.make_async_remote_copy(
                    src_ref=mine, dst_ref=mine, send_sem=send_sems[a].at[j], recv_sem=recv_sems[a].at[j],
                    device_id=(cx, cy, c), device_id_type=MESH).start()
        token[...] = jnp.zeros_like(token)

    outs = pl.pallas_call(
        body, name=name, in_specs=[HBM_SPEC] * na + [ANY_SPEC],
        out_specs=[HBM_SPEC] * na + [SEM_SPEC] * (2 * na) + [TOKEN_SPEC],
        out_shape=[pltpu.HBM(f.shape, f.dtype) for f in fulls] + [pltpu.SemaphoreType.DMA((N_OTHER,))] * (2 * na)
        + [TOKEN_SHAPE],
        input_output_aliases={a: a for a in range(na)}, compiler_params=SPLIT_COPY,
    )(*[_in_hbm(f) for f in fulls], after)
    return list(outs[:na]), list(outs[na:2 * na]), list(outs[2 * na:3 * na]), outs[3 * na]


def _gather_pass_on(name, full, recv_sems, after):
    def body(f_in, recv_sems, after_ref, f_ref, d2d_send, d2d_recv):
        x, y, c, chips = _mesh_place()
        for j, (cx, cy) in enumerate(chips):
            blk = _half_rows(f_ref, 2 * cx + cy, c)
            pltpu.make_async_remote_copy(
                src_ref=blk, dst_ref=blk, send_sem=d2d_send.at[j], recv_sem=recv_sems.at[j],
                device_id=(cx, cy, c), device_id_type=MESH).wait_recv()
            pltpu.make_async_remote_copy(
                src_ref=blk, dst_ref=blk, send_sem=d2d_send.at[j], recv_sem=d2d_recv.at[j],
                device_id=(x, y, 1 - c), device_id_type=MESH).start()

    return pl.pallas_call(
        body, name=name, in_specs=[HBM_SPEC, SEM_SPEC, ANY_SPEC], out_specs=[HBM_SPEC, SEM_SPEC, SEM_SPEC],
        out_shape=[pltpu.HBM(full.shape, full.dtype)] + [pltpu.SemaphoreType.DMA((N_OTHER,))] * 2,
        input_output_aliases={0: 0}, compiler_params=SPLIT_COPY,
    )(full, recv_sems, after)


def _gather_arrive(name, full, ici_send, d2d_send, d2d_recv, after):
    def body(f_in, ici_send, d2d_send, d2d_recv, after_ref, f_ref):
        x, y, c, chips = _mesh_place()
        for j, (cx, cy) in enumerate(chips):
            mine = _half_rows(f_ref, 2 * x + y, c)
            passed = _half_rows(f_ref, 2 * cx + cy, c)
            theirs = _half_rows(f_ref, 2 * cx + cy, 1 - c)
            pltpu.make_async_remote_copy(
                src_ref=mine, dst_ref=mine, send_sem=ici_send.at[j], recv_sem=d2d_recv.at[j],
                device_id=(cx, cy, c), device_id_type=MESH).wait_send()
            pltpu.make_async_remote_copy(
                src_ref=passed, dst_ref=passed, send_sem=d2d_send.at[j], recv_sem=d2d_recv.at[j],
                device_id=(x, y, 1 - c), device_id_type=MESH).wait_send()
            pltpu.make_async_remote_copy(
                src_ref=theirs, dst_ref=theirs, send_sem=d2d_send.at[j], recv_sem=d2d_recv.at[j],
                device_id=(x, y, 1 - c), device_id_type=MESH).wait_recv()

    return pl.pallas_call(
        body, name=name, in_specs=[HBM_SPEC, SEM_SPEC, SEM_SPEC, SEM_SPEC, ANY_SPEC], out_specs=HBM_SPEC,
        out_shape=pltpu.HBM(full.shape, full.dtype), input_output_aliases={0: 0}, compiler_params=SPLIT_COPY,
    )(full, ici_send, d2d_send, d2d_recv, after)


def _gather_taps(conv_w):
    def body(cw_ref, cwf_ref, send_sems, recv_sems, local_sem):
        x, y, c, chips = _mesh_place()
        k_me = 2 * x + y
        local = pltpu.make_async_copy(cw_ref, cwf_ref.at[k_me], local_sem)
        local.start()
        copies = [pltpu.make_async_remote_copy(
            src_ref=cw_ref, dst_ref=cwf_ref.at[k_me], send_sem=send_sems.at[j], recv_sem=recv_sems.at[j],
            device_id=(cx, cy, c), device_id_type=MESH) for j, (cx, cy) in enumerate(chips)]
        for cp in copies:
            cp.start()
        for j, (cx, cy) in enumerate(chips):
            pltpu.make_async_remote_copy(
                src_ref=cw_ref, dst_ref=cwf_ref.at[2 * cx + cy], send_sem=send_sems.at[j], recv_sem=recv_sems.at[j],
                device_id=(cx, cy, c), device_id_type=MESH).wait_recv()
        for cp in copies:
            cp.wait_send()
        local.wait()

    return pl.pallas_call(
        body, name="gather_taps", in_specs=[HBM_SPEC], out_specs=HBM_SPEC,
        out_shape=jax.ShapeDtypeStruct((N_CHIPS,) + conv_w.shape, conv_w.dtype),
        scratch_shapes=[pltpu.SemaphoreType.DMA((N_OTHER,))] * 2 + [pltpu.SemaphoreType.DMA],
    )(conv_w)


def _sibling_half(g_ref, c):
    r2 = g_ref.shape[1] // 2
    return g_ref.at[:, pl.ds((1 - c) * r2, r2), :]


def _swap_copy(g_ref, land_ref, send_sems, recv_sems, a):
    x, y, c, _ = _mesh_place()
    return pltpu.make_async_remote_copy(
        src_ref=_sibling_half(g_ref, c), dst_ref=land_ref, send_sem=send_sems.at[a], recv_sem=recv_sems.at[a],
        device_id=(x, y, 1 - c), device_id_type=MESH)


def _swap_start(name, gs):
    n = len(gs)

    def body(*refs):
        g_refs, land_refs = refs[n:2 * n], refs[2 * n:3 * n]
        send_sems, recv_sems, token = refs[3 * n:]
        for a in range(n):
            _swap_copy(g_refs[a], land_refs[a], send_sems, recv_sems, a).start()
        token[...] = jnp.zeros_like(token)

    outs = pl.pallas_call(
        body, name=name, in_specs=[HBM_SPEC] * n,
        out_specs=[HBM_SPEC] * (2 * n) + [SEM_SPEC, SEM_SPEC, TOKEN_SPEC],
        out_shape=[pltpu.HBM(g.shape, g.dtype) for g in gs]
        + [pltpu.HBM((g.shape[0], g.shape[1] // 2, g.shape[2]), g.dtype) for g in gs]
        + [pltpu.SemaphoreType.DMA((n,)), pltpu.SemaphoreType.DMA((n,)), TOKEN_SHAPE],
        input_output_aliases={a: a for a in range(n)}, compiler_params=SPLIT_COPY,
    )(*[_in_hbm(g) for g in gs])
    return list(outs[:n]), list(outs[n:2 * n]), outs[2 * n], outs[2 * n + 1], outs[2 * n + 2]


def _swap_wait(name, gs, lands, send_sems, recv_sems, after):
    n = len(gs)

    def body(*refs):
        send_sems, recv_sems = refs[2 * n], refs[2 * n + 1]
        g_refs, land_refs = refs[2 * n + 3:3 * n + 3], refs[3 * n + 3:]
        for a in range(n):
            copy = _swap_copy(g_refs[a], land_refs[a], send_sems, recv_sems, a)
            copy.wait_send()
            copy.wait_recv()

    outs = pl.pallas_call(
        body, name=name, in_specs=[HBM_SPEC] * (2 * n) + [SEM_SPEC, SEM_SPEC, ANY_SPEC],
        out_specs=[HBM_SPEC] * (2 * n),
        out_shape=[pltpu.HBM(t.shape, t.dtype) for t in list(gs) + list(lands)],
        input_output_aliases={a: a for a in range(2 * n)}, compiler_params=SPLIT_COPY,
    )(*gs, *lands, send_sems, recv_sems, after)
    return list(outs[:n]), list(outs[n:])


def _add_core_halves(name, g, sib, core):
    nb, r, cols = g.shape
    r2 = r // 2
    tr = _row_tile(r2, cols, itemsize=2, budget=BIG_BLOCK)
    nrt = r2 // tr

    def body(core_ref, g_ref, s_ref, o_ref):
        o_ref[...] = (g_ref[...].astype(F32) + s_ref[...].astype(F32)).astype(o_ref.dtype)

    return pl.pallas_call(
        body, name=name,
        grid_spec=pltpu.PrefetchScalarGridSpec(
            num_scalar_prefetch=1, grid=(nb, nrt),
            in_specs=[pl.BlockSpec((None, tr, cols), lambda k, i, core_ref: (k, core_ref[0] * nrt + i, 0)),
                      pl.BlockSpec((None, tr, cols), lambda k, i, core_ref: (k, i, 0))],
            out_specs=pl.BlockSpec((None, tr, cols), lambda k, i, core_ref: (k, i, 0))),
        out_shape=jax.ShapeDtypeStruct((nb, r2, cols), BF16), compiler_params=_params("parallel", "parallel"),
    )(core, g, sib)


def _scatter_copies(h_refs, land_refs, send_sems, recv_sems):
    x, y, c, chips = _mesh_place()
    return [pltpu.make_async_remote_copy(
        src_ref=h_ref.at[2 * cx + cy], dst_ref=land_ref.at[j],
        send_sem=send_sems.at[a * N_OTHER + j], recv_sem=recv_sems.at[a * N_OTHER + j],
        device_id=(cx, cy, c), device_id_type=MESH)
        for a, (h_ref, land_ref) in enumerate(zip(h_refs, land_refs)) for j, (cx, cy) in enumerate(chips)]


def _scatter_start(name, hs):
    n = len(hs)

    def body(*refs):
        h_refs, land_refs = refs[n:2 * n], refs[2 * n:3 * n]
        send_sems, recv_sems, token = refs[3 * n:]
        for copy in _scatter_copies(h_refs, land_refs, send_sems, recv_sems):
            copy.start()
        token[...] = jnp.zeros_like(token)

    outs = pl.pallas_call(
        body, name=name, in_specs=[HBM_SPEC] * n,
        out_specs=[HBM_SPEC] * (2 * n) + [SEM_SPEC, SEM_SPEC, TOKEN_SPEC],
        out_shape=[pltpu.HBM(h.shape, h.dtype) for h in hs]
        + [pltpu.HBM((N_OTHER,) + h.shape[1:], h.dtype) for h in hs]
        + [pltpu.SemaphoreType.DMA((n * N_OTHER,)), pltpu.SemaphoreType.DMA((n * N_OTHER,)), TOKEN_SHAPE],
        input_output_aliases={a: a for a in range(n)}, compiler_params=SPLIT_COPY,
    )(*[_in_hbm(h) for h in hs])
    return list(outs[:n]), list(outs[n:2 * n]), outs[2 * n], outs[2 * n + 1], outs[2 * n + 2]


def _scatter_wait(name, hs, lands, send_sems, recv_sems, after):
    afters = tuple(after) if isinstance(after, (tuple, list)) else (after,)
    n = len(hs)

    def body(*refs):
        send_sems, recv_sems = refs[2 * n], refs[2 * n + 1]
        h_refs, land_refs = refs[-2 * n:-n], refs[-n:]
        for copy in _scatter_copies(h_refs, land_refs, send_sems, recv_sems):
            copy.wait_send()
            copy.wait_recv()

    outs = pl.pallas_call(
        body, name=name, in_specs=[HBM_SPEC] * (2 * n) + [SEM_SPEC, SEM_SPEC] + [ANY_SPEC] * len(afters),
        out_specs=[HBM_SPEC] * (2 * n),
        out_shape=[pltpu.HBM(t.shape, t.dtype) for t in list(hs) + list(lands)],
        input_output_aliases={a: a for a in range(2 * n)}, compiler_params=SPLIT_COPY,
    )(*hs, *lands, send_sems, recv_sems, *afters)
    return list(outs[:n]), list(outs[n:])


def _sum_chips(name, hs, rcv, core, chip, layer, n_layers, prev):
    _, r2, cols = hs.shape
    tr = _row_tile(r2, cols, budget=BIG_BLOCK)
    nrt = r2 // tr

    def body(core_ref, chip_ref, h_ref, r_ref, *rest):
        o_ref = rest[-1]
        acc = h_ref[...].astype(F32)
        for j in range(N_CHIPS - 1):
            acc = acc + r_ref[j].astype(F32)
        o_ref[...] = acc

    in_specs = [pl.BlockSpec((None, tr, cols), lambda i, core_ref, chip_ref: (chip_ref[0], i, 0)),
                pl.BlockSpec((N_CHIPS - 1, tr, cols), lambda i, core_ref, chip_ref: (0, i, 0))]
    args = [core, chip, hs, rcv]
    aliases = {}
    if prev is not None:
        in_specs.append(pl.BlockSpec(memory_space=pl.ANY))
        args.append(prev)
        aliases = {4: 0}
    return pl.pallas_call(
        body, name=name,
        grid_spec=pltpu.PrefetchScalarGridSpec(
            num_scalar_prefetch=2, grid=(nrt,), in_specs=in_specs,
            out_specs=pl.BlockSpec((None, tr, cols), lambda i, core_ref, chip_ref: (layer, core_ref[0] * nrt + i, 0))),
        out_shape=jax.ShapeDtypeStruct((n_layers, 2 * r2, cols), F32), input_output_aliases=aliases,
        compiler_params=_params("parallel"),
    )(*args)


def _join_copy(t_ref, send_sems, recv_sems, a):
    x, y, c, _ = _mesh_place()
    r2 = t_ref.shape[1] // 2
    mine = t_ref.at[:, pl.ds(c * r2, r2), :]
    return pltpu.make_async_remote_copy(
        src_ref=mine, dst_ref=mine, send_sem=send_sems.at[a], recv_sem=recv_sems.at[a],
        device_id=(x, y, 1 - c), device_id_type=MESH)


def _join_start(name, ts, deps=()):
    n, nd = len(ts), len(deps)

    def body(*refs):
        t_refs = refs[n + nd:2 * n + nd]
        send_sems, recv_sems = refs[2 * n + nd:]
        for a in range(n):
            _join_copy(t_refs[a], send_sems, recv_sems, a).start()

    outs = pl.pallas_call(
        body, name=name, in_specs=[HBM_SPEC] * n + [ANY_SPEC] * nd, out_specs=[HBM_SPEC] * n + [SEM_SPEC, SEM_SPEC],
        out_shape=[pltpu.HBM(t.shape, t.dtype) for t in ts] + [pltpu.SemaphoreType.DMA((n,))] * 2,
        input_output_aliases={a: a for a in range(n)}, compiler_params=SPLIT_COPY,
    )(*[_in_hbm(t) for t in ts], *deps)
    return list(outs[:n]), outs[n], outs[n + 1]


def _join_wait(name, t, a, send_sems, recv_sems, after):
    def body(t_in, send_sems, recv_sems, after_ref, t_ref):
        copy = _join_copy(t_ref, send_sems, recv_sems, a)
        copy.wait_send()
        copy.wait_recv()

    return pl.pallas_call(
        body, name=name, in_specs=[HBM_SPEC, SEM_SPEC, SEM_SPEC, ANY_SPEC], out_specs=HBM_SPEC,
        out_shape=pltpu.HBM(t.shape, t.dtype), input_output_aliases={0: 0}, compiler_params=SPLIT_COPY,
    )(t, send_sems, recv_sems, after)


def _allreduce_small(p):
    n, _, w = p.shape

    def body(p_ref, o_ref, buf, send_sems, recv_sems):
        x, y, c, _ = _mesh_place()
        me = 4 * x + 2 * y + c
        buf[me] = jnp.sum(p_ref[...], axis=1)
        copies = []
        for pat in range(1, N_DEV):
            fx, fy, fc = (pat >> 2) & 1, (pat >> 1) & 1, pat & 1
            copies.append(pltpu.make_async_remote_copy(
                src_ref=buf.at[me], dst_ref=buf.at[me], send_sem=send_sems.at[pat - 1], recv_sem=recv_sems.at[pat - 1],
                device_id=(x ^ fx, y ^ fy, c ^ fc), device_id_type=MESH))
        for cp in copies:
            cp.start()
        for cp in copies:
            cp.wait()
        acc = buf[0]
        for dev in range(1, N_DEV):
            acc = acc + buf[dev]
        o_ref[...] = acc

    return pl.pallas_call(
        body, name="allreduce_small", in_specs=[pl.BlockSpec(memory_space=pltpu.VMEM)],
        out_specs=pl.BlockSpec(memory_space=pltpu.VMEM), out_shape=jax.ShapeDtypeStruct((n, w), F32),
        scratch_shapes=[pltpu.VMEM((N_DEV, n, w), F32), pltpu.SemaphoreType.DMA((N_DEV - 1,)),
                        pltpu.SemaphoreType.DMA((N_DEV - 1,))],
    )(p)


class _WeightFeed:
    def __init__(self):
        self.fulls, self.ici_send, self.ici_recv, self.d2d = [], [], [], []

    def start(self, name, fulls, after):
        started, send, recv, token = _gather_start(name, fulls, after)
        self.fulls += started
        self.ici_send += send
        self.ici_recv += recv
        self.d2d += [None] * len(fulls)
        self.token = token
        return token

    def _pass_on(self, k, after):
        if k == 0:
            after = self.token
        if k < len(self.fulls) and self.d2d[k] is None:
            self.fulls[k], send, recv = _gather_pass_on(f"gather_pass_{k}", self.fulls[k], self.ici_recv[k], after)
            self.d2d[k] = (send, recv)

    def take(self, k, after):
        self._pass_on(k, after)
        self.fulls[k] = _gather_arrive(f"gather_arrive_{k}", self.fulls[k], self.ici_send[k], *self.d2d[k], after)
        return self.fulls[k]


def _ffn_forward(tag, x, h, g_post, next_gain, feed, k):
    s, d = x.shape
    gu_w = feed.take(k, h)
    gu, a = _ffn_up(f"{tag}_up", h, gu_w)
    dn_w = feed.take(k + 1, a).reshape(-1, d)
    f = dn_w.shape[0]
    tm, tn = _tile(s, 1024), _tile(d, 512)
    y = _mm(f"{tag}_down", a, dn_w, mode="nn", grid=(s // tm, d // tn),
            a_spec=pl.BlockSpec((tm, f), lambda i, j: (i, 0)),
            b_spec=pl.BlockSpec((f, tn), lambda i, j: (0, j)),
            o_spec=pl.BlockSpec((tm, tn), lambda i, j: (i, j)),
            out_shape=jax.ShapeDtypeStruct((s, d), F32))
    x_new, h_next = _res_norm(f"{tag}_post", x, y, g_post, FFN_RESIDUAL_WEIGHT, next_gain)
    return x_new, h_next, (x, h, gu, a, y)


class _GradReduce:
    def __init__(self, core, chip, n_layers):
        self.core, self.chip, self.n_layers = core, chip, n_layers
        self.state = {}
        self.bufs = {}

    def start(self, kinds, layer, gs):
        gs, lands, send, recv, token = _swap_start(f"swap_start_{kinds[0]}_{layer}", gs)
        self.state[kinds, layer] = (gs, lands, send, recv)
        return token

    def exchange(self, kinds, layer, after):
        tag = f"{kinds[0]}_{layer}"
        gs, sibs = _swap_wait(f"swap_wait_{tag}", *self.state[kinds, layer], after)
        hs = [_add_core_halves(f"add_cores_{k}_{layer}", g, sib, self.core) for k, g, sib in zip(kinds, gs, sibs)]
        hs, lands, send, recv, token = _scatter_start(f"scatter_start_{tag}", hs)
        self.state[kinds, layer] = (hs, lands, send, recv)
        return token

    def finish(self, kinds, layer, after):
        tag = f"{kinds[0]}_{layer}"
        hs, rcvs = _scatter_wait(f"scatter_wait_{tag}", *self.state.pop((kinds, layer)), after)
        for k, h, rcv in zip(kinds, hs, rcvs):
            self.bufs[k] = _sum_chips(f"sum_chips_{k}_{layer}", h, rcv, self.core, self.chip, layer, self.n_layers,
                                      self.bufs.get(k))
        return self.bufs[kinds[-1]]


def _ffn_backward(tag, dx_new, saved, g_pre, g_post, gu_w, dn_w, red, kinds, layer, deps, head, following,
                  last=None):
    x, h, gu, a, y = saved
    s, d = x.shape
    nb, fs = gu_w.shape[0], gu_w.shape[2]
    f = dn_w.shape[0]
    fr = f // nb
    dy, dg_post = head or _norm_bwd(f"{tag}_post_bwd", dx_new, y, g_post, FFN_RESIDUAL_WEIGHT, None, BF16)
    dgu = _ffn_dact(f"{tag}_dact", dy, dn_w, gu, deps)
    dgu4 = dgu.reshape(nb, s, fs)
    tm, tw = _tile(d, 512), _tile(fs, 1408)
    nw = fs // tw
    tn = _tile(d, 1024)
    ts, td = _tile(s, 1024), _tile(d, 1024)

    def gate_up_gradient(deps):
        return _mm(f"{tag}_dwgu", h, dgu4, mode="tn", grid=(nb, nw, d // tm),
                   a_spec=pl.BlockSpec((s, tm), lambda k, j, i: (0, i)),
                   b_spec=pl.BlockSpec((None, s, tw), lambda k, j, i: (k, 0, j)),
                   o_spec=pl.BlockSpec((None, tm, tw), lambda k, j, i: (k, i, j)),
                   out_shape=jax.ShapeDtypeStruct((nb, d, fs), BF16), deps=deps)

    def down_gradient(deps):
        return _mm(f"{tag}_dwd", a, dy, mode="tn", grid=(nb, d // tn),
                   a_spec=pl.BlockSpec((s, fr), lambda i, j: (0, i)),
                   b_spec=pl.BlockSpec((s, tn), lambda i, j: (0, j)),
                   o_spec=pl.BlockSpec((None, fr, tn), lambda i, j: (i, 0, j)),
                   out_shape=jax.ShapeDtypeStruct((nb, fr, d), BF16), deps=deps)

    def input_gradient(deps):
        dh = _mm(f"{tag}_dh", dgu4, gu_w, mode="nt", grid=(s // ts, d // td, nb),
                 a_spec=pl.BlockSpec((None, ts, fs), lambda i, j, k: (k, i, 0)),
                 b_spec=pl.BlockSpec((None, td, fs), lambda i, j, k: (k, j, 0)),
                 o_spec=pl.BlockSpec((ts, td), lambda i, j, k: (i, j)),
                 out_shape=jax.ShapeDtypeStruct((s, d), F32), nk=nb, acc_shape=(ts, td), deps=deps)
        return _norm_bwd(f"{tag}_pre_bwd", dh, x, g_pre, 1.0, dx_new, F32, following)

    if last is None:
        started = red.start(kinds, layer, [gate_up_gradient(()), down_gradient(())])
        dx, dg_pre, *next_head = input_gradient((started,))
    else:
        dx, dg_pre, *next_head = input_gradient(())
        first = red.start(kinds[:1], layer, [gate_up_gradient((last(dg_pre, dg_post),))])
        second = red.start(kinds[1:], layer, [down_gradient((first,))])
        red.exchange(kinds[:1], layer, second)
    return dx, dg_pre, dg_post, tuple(next_head) or None


def _mixer_forward(tag, x, h, gains, next_gain, feed, k, conv_taps, dims):
    qd, kvd, cd = dims
    s, d = x.shape
    _, g_a, g_c, g_post = gains
    win_w = feed.take(k, h)
    nb, cw = win_w.shape[0], win_w.shape[2]
    tm = _tile(s, 1024)
    z = _mm(f"{tag}_in", h, win_w, mode="nn", grid=(nb, s // tm),
            a_spec=pl.BlockSpec((tm, d), lambda j, i: (i, 0)),
            b_spec=pl.BlockSpec((None, d, cw), lambda j, i: (j, 0, 0)),
            o_spec=pl.BlockSpec((tm, cw), lambda j, i: (i, j)),
            out_shape=jax.ShapeDtypeStruct((s, nb * cw), BF16))
    a, lse = _attn_fwd(f"{tag}_attn", z, qd, kvd)
    c = _conv_fwd(f"{tag}_conv", z, conv_taps, qd + 2 * kvd, cd)
    cat = _cat_norm_fwd(f"{tag}_cat", a, c, g_a, g_c)
    wout_w = feed.take(k + 1, cat).reshape(-1, d)
    mw = qd + cd
    tn = _tile(d, 1024)
    mixed = _mm(f"{tag}_out", cat, wout_w, mode="nn", grid=(s // tm, d // tn),
                a_spec=pl.BlockSpec((tm, mw), lambda i, j: (i, 0)),
                b_spec=pl.BlockSpec((mw, tn), lambda i, j: (0, j)),
                o_spec=pl.BlockSpec((tm, tn), lambda i, j: (i, j)),
                out_shape=jax.ShapeDtypeStruct((s, d), F32))
    x_new, h_next = _res_norm(f"{tag}_post", x, mixed, g_post, 1.0, next_gain)
    return x_new, h_next, (x, h, z, a, lse, c, cat, mixed)


def _mixer_backward(tag, dx_new, saved, gains, win_w, conv_taps, wout_w, dims, red, kinds, layer, deps, head,
                    following):
    qd, kvd, cd = dims
    x, h, z, a, lse, c, cat, mixed = saved
    s, d = x.shape
    nb, cw = win_w.shape[0], win_w.shape[2]
    g_pre, g_a, g_c, g_post = gains
    mw = qd + cd
    dmixed, dg_post = head or _norm_bwd(f"{tag}_post_bwd", dx_new, mixed, g_post, 1.0, None, BF16)
    tm, tn = _tile(s, 1024), _tile(mw, 1024)
    dcat = _mm(f"{tag}_dcat", dmixed, wout_w, mode="nt", grid=(s // tm, mw // tn),
               a_spec=pl.BlockSpec((tm, d), lambda i, j: (i, 0)),
               b_spec=pl.BlockSpec((tn, d), lambda i, j: (j, 0)),
               o_spec=pl.BlockSpec((tm, tn), lambda i, j: (i, j)),
               out_shape=jax.ShapeDtypeStruct((s, mw), F32), deps=deps)
    wr = mw // nb
    td = _tile(d, 1024)
    d_wout = _mm(f"{tag}_dwout", cat, dmixed, mode="tn", grid=(nb, d // td),
                 a_spec=pl.BlockSpec((s, wr), lambda i, j: (0, i)),
                 b_spec=pl.BlockSpec((s, td), lambda i, j: (0, j)),
                 o_spec=pl.BlockSpec((None, wr, td), lambda i, j: (i, 0, j)),
                 out_shape=jax.ShapeDtypeStruct((nb, wr, d), BF16))
    da, dc, dg_a, dg_c = _cat_norm_bwd(f"{tag}_cat_bwd", dcat, a, c, g_a, g_c)
    dhc, dbg, dcg, d_taps = _conv_bwd(f"{tag}_conv_bwd", z, conv_taps, dc, qd + 2 * kvd, cd)
    dq, dk, dv = _attn_bwd(f"{tag}_attn_bwd", z, a, lse, da, qd, kvd)
    dz = jnp.concatenate([dq, dk, dv, dhc, dbg, dcg], axis=1)
    th = _tile(d, 512)
    d_win = _mm(f"{tag}_dwin", h, dz, mode="tn", grid=(nb, d // th),
                a_spec=pl.BlockSpec((s, th), lambda k, i: (0, i)),
                b_spec=pl.BlockSpec((s, cw), lambda k, i: (0, k)),
                o_spec=pl.BlockSpec((None, th, cw), lambda k, i: (k, i, 0)),
                out_shape=jax.ShapeDtypeStruct((nb, d, cw), BF16))
    started = (red.start(kinds, layer, [d_win, d_wout]),)
    dh = _mm(f"{tag}_dh", dz, win_w, mode="nt", grid=(s // tm, d // td, nb),
             a_spec=pl.BlockSpec((tm, cw), lambda i, j, k: (i, k)),
             b_spec=pl.BlockSpec((None, td, cw), lambda i, j, k: (k, j, 0)),
             o_spec=pl.BlockSpec((tm, td), lambda i, j, k: (i, j)),
             out_shape=jax.ShapeDtypeStruct((s, d), F32), nk=nb, acc_shape=(tm, td), deps=started)
    dx, dg_pre, *next_head = _norm_bwd(f"{tag}_pre_bwd", dh, x, g_pre, 1.0, dx_new, F32, following)
    return dx, d_taps, (dg_pre, dg_a, dg_c, dg_post), tuple(next_head) or None


def _pad_cols(v, width):
    return jnp.pad(v, ((0, 0), (0, width - v.shape[1])))


def kernel(x, ffn1_norm_pre, ffn1_w_gate_up, ffn1_w_down, ffn1_norm_post, mix_norm_pre, w_in, conv_w, attn_out_norm, conv_out_norm, w_out, mix_norm_post, ffn2_norm_pre, ffn2_w_gate_up, ffn2_w_down, ffn2_norm_post, loss_target, m_ffn1_norm_pre, m_ffn1_w_gate_up, m_ffn1_w_down, m_ffn1_norm_post, m_mix_norm_pre, m_w_in, m_conv_w, m_attn_out_norm, m_conv_out_norm, m_w_out, m_mix_norm_post, m_ffn2_norm_pre, m_ffn2_w_gate_up, m_ffn2_w_down, m_ffn2_norm_post, v_ffn1_norm_pre, v_ffn1_w_gate_up, v_ffn1_w_down, v_ffn1_norm_post, v_mix_norm_pre, v_w_in, v_conv_w, v_attn_out_norm, v_conv_out_norm, v_w_out, v_mix_norm_post, v_ffn2_norm_pre, v_ffn2_w_gate_up, v_ffn2_w_down, v_ffn2_norm_post):
    _, s, d = x.shape
    n_layers = ffn1_norm_pre.shape[0]
    qd = attn_out_norm.shape[1]
    cd = conv_out_norm.shape[1]
    kvd = qd // Q_PER_KV
    dims = (qd, kvd, cd)
    assert N_CHIPS * w_in.shape[2] == qd + 2 * kvd + 3 * cd and qd + cd == N_CHIPS * w_out.shape[1]
    assert 2 * d <= SMALL_ROWS * LANES * SUBLANES
    chip = 2 * lax.axis_index("x") + lax.axis_index("y")
    chip_arr = chip.astype(jnp.int32).reshape(1)
    core = lax.axis_index("c").astype(jnp.int32).reshape(1)
    kinds = ("gu1", "dn1", "win", "wout", "gu2", "dn2")

    big = (ffn1_w_gate_up, ffn1_w_down, w_in, w_out, ffn2_w_gate_up, ffn2_w_down)
    nk = len(kinds)
    taps_all = _gather_taps(conv_w)
    feed = _WeightFeed()
    order = [(k, w, layer) for layer in range(n_layers) for k, w in zip(kinds, big)]
    k, w, layer = order[0]
    token = feed.start("gather_start_first", [_cast_into_slot(f"cast_{k}_{layer}", w, layer, chip_arr)], taps_all)
    feed.start("gather_start_rest", [_cast_into_slot(f"cast_{k}_{layer}", w, layer, chip_arr, (token,))
                                     for k, w, layer in order[1:]], token)
    taps = jnp.transpose(taps_all, (1, 2, 0, 3)).reshape(n_layers, CONV_WIDTH, cd)
    taps = jnp.pad(taps, ((0, 0), (0, SUBLANES - CONV_WIDTH), (0, 0)))

    def gain(g, layer):
        return g[layer][None, :]

    xs = x[0]
    hs = _norm_fwd("l0_ffn1_norm", xs, gain(ffn1_norm_pre, 0))
    saved = []
    for layer in range(n_layers):
        t = f"l{layer}"
        k0 = layer * nk
        xs, hs, s1 = _ffn_forward(f"{t}_ffn1", xs, hs, gain(ffn1_norm_post, layer), gain(mix_norm_pre, layer), feed, k0)
        mix_gains = (gain(mix_norm_pre, layer), gain(attn_out_norm, layer), gain(conv_out_norm, layer), gain(mix_norm_post, layer))
        xs, hs, s2 = _mixer_forward(f"{t}_mix", xs, hs, mix_gains, gain(ffn2_norm_pre, layer), feed, k0 + 2,
                                    taps[layer], dims)
        following = gain(ffn1_norm_pre, layer + 1) if layer + 1 < n_layers else None
        xs, hs, s3 = _ffn_forward(f"{t}_ffn2", xs, hs, gain(ffn2_norm_post, layer), following, feed, k0 + 4)
        saved.append((s1, s2, s3, mix_gains))
    wts = {k: [feed.fulls[layer * nk + i] for layer in range(n_layers)] for i, k in enumerate(kinds)}
    for k in ("dn1", "wout", "dn2"):
        wts[k] = [w.reshape(-1, d) for w in wts[k]]
    dxs, loss_part = _loss_head("loss_head", xs, loss_target[0])
    loss = lax.psum(jnp.sum(loss_part), ("x", "y", "c"))

    red = _GradReduce(core, chip_arr, n_layers)
    small = [None] * n_layers
    flow = {"deps": (), "in_flight": None}

    def between(dx, group):
        after = dx
        if flow["in_flight"] is not None:
            after = red.finish(*flow["in_flight"], after)
        flow["deps"] = (red.exchange(*group, after),)
        flow["in_flight"] = group

    head = None
    for layer in reversed(range(n_layers)):
        t = f"l{layer}"
        s1, s2, s3, mix_gains = saved[layer]
        after_ffn2 = (s2[7], mix_gains[3], 1.0)
        after_mix = (s1[4], gain(ffn1_norm_post, layer), FFN_RESIDUAL_WEIGHT)
        after_ffn1 = ((saved[layer - 1][2][4], gain(ffn2_norm_post, layer - 1), FFN_RESIDUAL_WEIGHT)
                      if layer > 0 else None)
        dxs, p_pre2, p_post2, head = _ffn_backward(
            f"{t}_ffn2", dxs, s3, gain(ffn2_norm_pre, layer), gain(ffn2_norm_post, layer),
            wts["gu2"][layer], wts["dn2"][layer], red, ("gu2", "dn2"), layer, flow["deps"], head, after_ffn2)
        between(dxs, (("gu2", "dn2"), layer))
        dxs, p_taps, (p_mpre, p_a, p_c, p_mpost), head = _mixer_backward(
            f"{t}_mix", dxs, s2, mix_gains, wts["win"][layer], taps[layer], wts["wout"][layer], dims,
            red, ("win", "wout"), layer, flow["deps"], head, after_mix)
        between(dxs, (("win", "wout"), layer))
        def pack_small(p_pre1, p_post1):
            tap_rows = jnp.zeros((CONV_WIDTH, SUBLANES, d), F32).at[:, 0, :cd].set(p_taps[:CONV_WIDTH])
            rows = [p_pre1, p_post1, p_mpre, jnp.concatenate([p_a, p_c], axis=1), p_mpost, p_pre2, p_post2]
            rows = jnp.concatenate([jnp.stack(rows), tap_rows], axis=0)
            small[layer] = jnp.pad(rows, ((0, SMALL_ROWS - rows.shape[0]), (0, 0), (0, 0)))

        def reduce_small(p_pre1, p_post1):
            pack_small(p_pre1, p_post1)
            flow["small"] = _allreduce_small(jnp.concatenate(small, axis=0))
            return flow["small"]

        dxs, p_pre1, p_post1, head = _ffn_backward(
            f"{t}_ffn1", dxs, s1, gain(ffn1_norm_pre, layer), gain(ffn1_norm_post, layer),
            wts["gu1"][layer], wts["dn1"][layer], red, ("gu1", "dn1"), layer, flow["deps"], head, after_ffn1,
            last=reduce_small if layer == 0 else None)
        if layer > 0:
            pack_small(p_pre1, p_post1)
        between(dxs, (("dn1",) if layer == 0 else ("gu1", "dn1"), layer))
    grad_x = dxs[None]

    weights = dict(ffn1_norm_pre=ffn1_norm_pre, ffn1_w_gate_up=ffn1_w_gate_up, ffn1_w_down=ffn1_w_down, ffn1_norm_post=ffn1_norm_post, mix_norm_pre=mix_norm_pre, w_in=w_in, conv_w=conv_w, attn_out_norm=attn_out_norm, conv_out_norm=conv_out_norm, w_out=w_out, mix_norm_post=mix_norm_post, ffn2_norm_pre=ffn2_norm_pre, ffn2_w_gate_up=ffn2_w_gate_up, ffn2_w_down=ffn2_w_down, ffn2_norm_post=ffn2_norm_post)
    m_in = dict(ffn1_norm_pre=m_ffn1_norm_pre, ffn1_w_gate_up=m_ffn1_w_gate_up, ffn1_w_down=m_ffn1_w_down, ffn1_norm_post=m_ffn1_norm_post, mix_norm_pre=m_mix_norm_pre, w_in=m_w_in, conv_w=m_conv_w, attn_out_norm=m_attn_out_norm, conv_out_norm=m_conv_out_norm, w_out=m_w_out, mix_norm_post=m_mix_norm_post, ffn2_norm_pre=m_ffn2_norm_pre, ffn2_w_gate_up=m_ffn2_w_gate_up, ffn2_w_down=m_ffn2_w_down, ffn2_norm_post=m_ffn2_norm_post)
    v_in = dict(ffn1_norm_pre=v_ffn1_norm_pre, ffn1_w_gate_up=v_ffn1_w_gate_up, ffn1_w_down=v_ffn1_w_down, ffn1_norm_post=v_ffn1_norm_post, mix_norm_pre=v_mix_norm_pre, w_in=v_w_in, conv_w=v_conv_w, attn_out_norm=v_attn_out_norm, conv_out_norm=v_conv_out_norm, w_out=v_w_out, mix_norm_post=v_mix_norm_post, ffn2_norm_pre=v_ffn2_norm_pre, ffn2_w_gate_up=v_ffn2_w_gate_up, ffn2_w_down=v_ffn2_w_down, ffn2_norm_post=v_ffn2_norm_post)
    kind_name = dict(gu1="ffn1_w_gate_up", dn1="ffn1_w_down", win="w_in", wout="w_out", gu2="ffn2_w_gate_up", dn2="ffn2_w_down")
    delta, new_m, new_v, grad = {}, {}, {}, {}

    def join_and_update(name, kind_list, deps, after):
        ts, send_sems, recv_sems = _join_start(name, [red.bufs[k] for k in kind_list], deps)
        for a, k in enumerate(kind_list):
            n = kind_name[k]
            g = _join_wait(f"join_wait_{k}", ts[a], a, send_sems, recv_sems, after)
            delta[n], new_m[n], new_v[n], grad[n] = _adamw(f"adamw_{n}", weights[n], g, m_in[n], v_in[n], True)
            after = delta[n]

    early = ("wout", "win", "dn2", "gu2")
    join_and_update("join_early", early, flow["deps"], dxs)
    done_early = [delta[kind_name[e]] for e in early]
    for group in ((("gu1",), 0), flow["in_flight"]):
        red.finish(*group, done_early)
    join_and_update("join_late", ("dn1", "gu1"), (), done_early[-1])

    small_sum = flow["small"].reshape(n_layers, SMALL_ROWS, d)
    g_ffn1_pre, g_ffn1_post, g_mix_pre = small_sum[:, 0], small_sum[:, 1], small_sum[:, 2]
    g_attn_out, g_conv_out = small_sum[:, 3, :qd], small_sum[:, 3, qd:qd + cd]
    g_mix_post, g_ffn2_pre, g_ffn2_post = small_sum[:, 4], small_sum[:, 5], small_sum[:, 6]
    cc = conv_w.shape[2]
    g_conv = lax.dynamic_slice_in_dim(small_sum[:, 7:7 + CONV_WIDTH, :cd], chip * cc, cc, axis=2)

    grad.update(ffn1_norm_pre=g_ffn1_pre, ffn1_norm_post=g_ffn1_post, mix_norm_pre=g_mix_pre, conv_w=g_conv, attn_out_norm=g_attn_out, conv_out_norm=g_conv_out, mix_norm_post=g_mix_post, ffn2_norm_pre=g_ffn2_pre, ffn2_norm_post=g_ffn2_post)
    names = list(weights)

    vectors = [n for n in names if n not in kind_name.values()]

    def pack(tree):
        flat = jnp.concatenate([tree[n].reshape(-1) for n in vectors])
        return jnp.pad(flat, (0, -flat.size % (SUBLANES * LANES))).reshape(-1, LANES)

    packed = _adamw("adamw_small", pack(weights), pack(grad), pack(m_in), pack(v_in))
    offset = 0
    for n in vectors:
        size = weights[n].size
        for tree, flat in zip((delta, new_m, new_v), packed):
            tree[n] = flat.reshape(-1)[offset:offset + size].reshape(weights[n].shape)
        offset += size

    return (loss, grad_x, *[grad[n] for n in names], *[delta[n] for n in names],
            *[new_m[n] for n in names], *[new_v[n] for n in names])
```

```python
import functools

import jax
import jax.numpy as jnp
from jax import lax
from jax.experimental import pallas as pl
from jax.experimental.pallas import tpu as pltpu

F32 = jnp.float32
BF16 = jnp.bfloat16
MESH = pl.DeviceIdType.MESH

NORM_EPS = 1e-6
HEAD_DIM = 128
Q_PER_KV = 4
CONV_WIDTH = 3
FFN_RESIDUAL_WEIGHT = 0.5
DILATED_BRANCHES = ((128, 1), (512, 4), (2048, 16))
ADAM_LR = 0.001
ADAM_B1 = 0.9
ADAM_B2 = 0.999
ADAM_EPS = 1e-08
ADAM_WD = 0.01
ADAM_STEP = 10

N_CHIPS = 4
N_DEV = 8
V7X_VMEM_BYTES = 64 << 20
VMEM_LIMIT = V7X_VMEM_BYTES - (12 << 20)
SUBLANES = 8
LANES = 128
SMALL_ROWS = 16
BIG_BLOCK = 4 << 20


def _params(*sem):
    return pltpu.CompilerParams(dimension_semantics=sem, vmem_limit_bytes=VMEM_LIMIT)


def _row_tile(rows, cols, itemsize=4, budget=2 << 20):
    t = rows
    while t * cols * itemsize > budget and t % 32 == 0:
        t //= 2
    return t


def _sum_to_sublanes(v):
    r, n = v.shape
    return v.reshape(r // SUBLANES, SUBLANES, n).sum(axis=0)


_DIMS = {
    "nn": (((1,), (0,)), ((), ())),
    "nt": (((1,), (1,)), ((), ())),
    "tn": (((0,), (0,)), ((), ())),
}


ANY_SPEC = pl.BlockSpec(memory_space=pl.ANY)


def _dot(a, b, mode):
    return lax.dot_general(a, b, _DIMS[mode], preferred_element_type=F32)


def _mm(name, a, b, *, mode, grid, a_spec, b_spec, o_spec, out_shape, nk=1, acc_shape=None, deps=()):
    nd = len(deps)

    def body(a_ref, b_ref, *rest):
        o_ref, scratch = rest[nd], rest[nd + 1:]
        r = _dot(a_ref[...], b_ref[...], mode)
        if nk == 1:
            o_ref[...] = r.astype(o_ref.dtype)
        else:
            acc = scratch[0]
            k = pl.program_id(len(grid) - 1)

            @pl.when(k == 0)
            def _():
                acc[...] = r

            @pl.when(k > 0)
            def _():
                acc[...] += r

            @pl.when(k == nk - 1)
            def _():
                o_ref[...] = acc[...].astype(o_ref.dtype)

    sem = ("parallel",) * (len(grid) - (1 if nk > 1 else 0)) + (("arbitrary",) if nk > 1 else ())
    return pl.pallas_call(
        body, name=name, grid=grid, in_specs=[a_spec, b_spec] + [ANY_SPEC] * nd, out_specs=o_spec,
        out_shape=out_shape, scratch_shapes=[pltpu.VMEM(acc_shape, F32)] if nk > 1 else [],
        compiler_params=_params(*sem),
    )(a, b, *deps)


def _tile(n, want):
    if n <= want:
        return n
    best = None
    for t in range(LANES, want + 1, LANES):
        if n % t == 0:
            best = t
    assert best is not None, (n, want)
    return best


def _norm_fwd(name, x, gain):
    s, d = x.shape
    tr = _row_tile(s, d, budget=BIG_BLOCK)

    def body(x_ref, g_ref, o_ref):
        xv = x_ref[...]
        r = lax.rsqrt(jnp.mean(xv * xv, axis=-1, keepdims=True) + NORM_EPS)
        o_ref[...] = (xv * r * g_ref[...]).astype(o_ref.dtype)

    return pl.pallas_call(
        body, name=name, grid=(s // tr,),
        in_specs=[pl.BlockSpec((tr, d), lambda i: (i, 0)), pl.BlockSpec((1, d), lambda i: (0, 0))],
        out_specs=pl.BlockSpec((tr, d), lambda i: (i, 0)),
        out_shape=jax.ShapeDtypeStruct((s, d), BF16), compiler_params=_params("parallel"),
    )(x, gain)


def _res_norm(name, x, y, gain, scale, next_gain=None):
    s, d = x.shape
    tr = _row_tile(s, d, budget=BIG_BLOCK)
    with_next = next_gain is not None

    def body(x_ref, y_ref, g_ref, *rest):
        yv = y_ref[...]
        r = lax.rsqrt(jnp.mean(yv * yv, axis=-1, keepdims=True) + NORM_EPS)
        xn = x_ref[...] + scale * (yv * r * g_ref[...])
        if with_next:
            ng_ref, o_ref, h_ref = rest
            rn = lax.rsqrt(jnp.mean(xn * xn, axis=-1, keepdims=True) + NORM_EPS)
            h_ref[...] = (xn * rn * ng_ref[...]).astype(h_ref.dtype)
        else:
            o_ref, = rest
        o_ref[...] = xn

    row = pl.BlockSpec((tr, d), lambda i: (i, 0))
    vec = pl.BlockSpec((1, d), lambda i: (0, 0))
    outs = pl.pallas_call(
        body, name=name, grid=(s // tr,),
        in_specs=[row, row, vec] + ([vec] if with_next else []), out_specs=[row] * (2 if with_next else 1),
        out_shape=[jax.ShapeDtypeStruct((s, d), F32)] + ([jax.ShapeDtypeStruct((s, d), BF16)] if with_next else []),
        compiler_params=_params("parallel"),
    )(x, y, gain, *((next_gain,) if with_next else ()))
    return (outs[0], outs[1]) if with_next else (outs[0], None)


def _rms_bwd(dn, yv, gv):
    r = lax.rsqrt(jnp.mean(yv * yv, axis=-1, keepdims=True) + NORM_EPS)
    xhat = yv * r
    dxn = dn * gv
    return r * (dxn - xhat * jnp.mean(dxn * xhat, axis=-1, keepdims=True)), _sum_to_sublanes(dn * xhat)


def _accumulate(ref, part):
    @pl.when(pl.program_id(0) == 0)
    def _():
        ref[...] = part

    @pl.when(pl.program_id(0) > 0)
    def _():
        ref[...] += part


def _norm_bwd(name, dout, yin, gain, scale, resid, out_dtype, following=None):
    s, d = yin.shape
    tr = _row_tile(s, d)
    has_resid = resid is not None
    chained = following is not None

    def body(*refs):
        refs = list(refs)
        do_ref, y_ref, g_ref = refs[:3]
        del refs[:3]
        r_ref = refs.pop(0) if has_resid else None
        if chained:
            y2_ref, g2_ref = refs[:2]
            del refs[:2]
        di_ref, dg_ref = refs[:2]
        din, part = _rms_bwd(scale * do_ref[...], y_ref[...], g_ref[...])
        _accumulate(dg_ref, part)
        if has_resid:
            din = din + r_ref[...]
        di_ref[...] = din.astype(di_ref.dtype)
        if chained:
            d2_ref, dg2_ref = refs[2:]
            d2, part2 = _rms_bwd(following[2] * din, y2_ref[...], g2_ref[...])
            _accumulate(dg2_ref, part2)
            d2_ref[...] = d2.astype(d2_ref.dtype)

    row = pl.BlockSpec((tr, d), lambda i: (i, 0))
    vec = pl.BlockSpec((1, d), lambda i: (0, 0))
    acc = pl.BlockSpec((SUBLANES, d), lambda i: (0, 0))
    ins = [row, row, vec] + ([row] if has_resid else []) + ([row, vec] if chained else [])
    args = (dout, yin, gain) + ((resid,) if has_resid else ()) + (tuple(following[:2]) if chained else ())
    outs = [row, acc] + ([row, acc] if chained else [])
    shapes = [jax.ShapeDtypeStruct((s, d), out_dtype), jax.ShapeDtypeStruct((SUBLANES, d), F32)]
    if chained:
        shapes += [jax.ShapeDtypeStruct((s, d), BF16), jax.ShapeDtypeStruct((SUBLANES, d), F32)]
    return pl.pallas_call(
        body, name=name, grid=(s // tr,), in_specs=ins, out_specs=outs, out_shape=shapes,
        compiler_params=_params("arbitrary"),
    )(*args)


def _loss_head(name, y, target):
    s, d = y.shape
    tr = _row_tile(s, d)

    def body(y_ref, t_ref, dy_ref, l_ref):
        e = y_ref[...] - t_ref[...]
        dy_ref[...] = e * (1.0 / d)
        part = _sum_to_sublanes(e * e) * (0.5 / d)

        @pl.when(pl.program_id(0) == 0)
        def _():
            l_ref[...] = part

        @pl.when(pl.program_id(0) > 0)
        def _():
            l_ref[...] += part

    row = pl.BlockSpec((tr, d), lambda i: (i, 0))
    return pl.pallas_call(
        body, name=name, grid=(s // tr,), in_specs=[row, row],
        out_specs=[row, pl.BlockSpec((SUBLANES, d), lambda i: (0, 0))],
        out_shape=[jax.ShapeDtypeStruct((s, d), F32), jax.ShapeDtypeStruct((SUBLANES, d), F32)],
        compiler_params=_params("arbitrary"),
    )(y, target)


def _ffn_up(name, h, gu_w):
    s, d = h.shape
    nb, _, fs = gu_w.shape
    hb = nb // 2
    w = gu_w.reshape(2, hb, d, fs)
    tm = _tile(s, 512)
    tn = _tile(fs, 1408)
    nj = fs // tn

    def body(h_ref, w_ref, gu_ref, a_ref):
        hv = h_ref[...]
        g = _dot(hv, w_ref[0], "nn")
        u = _dot(hv, w_ref[1], "nn")
        sg = jax.nn.sigmoid(g)
        silu = g * sg
        gu_ref[0] = (u * (sg * (1.0 + g * (1.0 - sg)))).astype(gu_ref.dtype)
        gu_ref[1] = silu.astype(gu_ref.dtype)
        a_ref[...] = (silu * u).astype(a_ref.dtype)

    return pl.pallas_call(
        body, name=name, grid=(hb, nj, s // tm),
        in_specs=[pl.BlockSpec((tm, d), lambda jb, jo, i: (i, 0)),
                  pl.BlockSpec((2, None, d, tn), lambda jb, jo, i: (0, jb, 0, jo))],
        out_specs=[pl.BlockSpec((2, None, tm, tn), lambda jb, jo, i: (0, jb, i, jo)),
                   pl.BlockSpec((tm, tn), lambda jb, jo, i: (i, jb * nj + jo))],
        out_shape=[jax.ShapeDtypeStruct((2, hb, s, fs), BF16), jax.ShapeDtypeStruct((s, hb * fs), BF16)],
        compiler_params=_params("parallel", "parallel", "parallel"),
    )(h, w)


def _ffn_dact(name, dy, dn_w, gu, deps=()):
    s, d = dy.shape
    _, hb, _, fs = gu.shape
    tm = _tile(s, 512)
    tn = _tile(fs, 1408)
    nj = fs // tn

    def body(dy_ref, w_ref, gu_ref, *rest):
        o_ref = rest[-1]
        wv = w_ref[...]
        parts = 2 if tm % (2 * SUBLANES * 2) == 0 else 1
        for r in range(parts):
            rows = slice(r * (tm // parts), (r + 1) * (tm // parts))
            da = _dot(dy_ref[rows, :], wv, "nt")
            o_ref[0, rows, :] = (da * gu_ref[0, rows, :].astype(F32)).astype(o_ref.dtype)
            o_ref[1, rows, :] = (da * gu_ref[1, rows, :].astype(F32)).astype(o_ref.dtype)

    blk = pl.BlockSpec((2, None, tm, tn), lambda jb, jo, i: (0, jb, i, jo))
    return pl.pallas_call(
        body, name=name, grid=(hb, nj, s // tm),
        in_specs=[pl.BlockSpec((tm, d), lambda jb, jo, i: (i, 0)),
                  pl.BlockSpec((tn, d), lambda jb, jo, i: (jb * nj + jo, 0)),
                  blk] + [ANY_SPEC] * len(deps),
        out_specs=blk, out_shape=jax.ShapeDtypeStruct(gu.shape, BF16),
        compiler_params=_params("parallel", "parallel", "parallel"),
    )(dy, dn_w, gu, *deps)


_MASKED = -1e30


def _attn_bias(s, tq):
    nd = s // tq
    dist = (jnp.arange(nd)[:, None, None] * tq + jnp.arange(tq)[None, :, None]) - jnp.arange(tq)[None, None, :]
    mult = jnp.zeros(dist.shape, F32)
    for window, dilation in DILATED_BRANCHES:
        mult = mult + ((dist >= 0) & (dist <= window) & (dist % dilation == 0)).astype(F32)
    return jnp.where(mult > 0.0, jnp.log(jnp.maximum(mult, 1.0)), _MASKED)


def _biased(sc, bias, scale):
    tq, tk = bias.shape
    return (sc.reshape(-1, tq, tk) * scale + bias[None]).reshape(sc.shape)


def _attn_specs(s, qd, kvd, tq):
    rw = Q_PER_KV * HEAD_DIM
    qspec = pl.BlockSpec((tq, rw), lambda g, i: (i, g))
    kspec = pl.BlockSpec((s, HEAD_DIM), lambda g, i: (0, qd // HEAD_DIM + g))
    vspec = pl.BlockSpec((s, HEAD_DIM), lambda g, i: (0, (qd + kvd) // HEAD_DIM + g))
    return rw, qspec, kspec, vspec


def _attn_fwd(name, z, qd, kvd):
    s = z.shape[0]
    tq = _tile(s, 256)
    nkv = kvd // HEAD_DIM
    rw, qspec, kspec, vspec = _attn_specs(s, qd, kvd, tq)
    scale = HEAD_DIM ** -0.5

    def body(q_ref, k_ref, v_ref, b_ref, o_ref, l_ref):
        i = pl.program_id(1)
        heads = [slice(h * HEAD_DIM, (h + 1) * HEAD_DIM) for h in range(Q_PER_KV)]
        q_all = jnp.concatenate([q_ref[:, cols] for cols in heads], axis=0)

        def chunk(j, carry):
            mx, den, acc = carry
            k0 = pl.multiple_of(j * tq, tq)
            kc, vc = k_ref[pl.ds(k0, tq), :], v_ref[pl.ds(k0, tq), :]
            sc = _biased(_dot(q_all, kc, "nt"), b_ref[i - j], scale)
            mx_new = jnp.maximum(mx, jnp.max(sc, axis=-1, keepdims=True))
            alpha = jnp.exp(mx - mx_new)
            p = jnp.exp(sc - mx_new)
            return (mx_new, alpha * den + jnp.sum(p, axis=-1, keepdims=True),
                    alpha * acc + _dot(p.astype(BF16), vc, "nn"))

        rows = Q_PER_KV * tq
        init = (jnp.full((rows, 1), _MASKED, F32), jnp.zeros((rows, 1), F32), jnp.zeros((rows, HEAD_DIM), F32))
        mx, den, acc = lax.fori_loop(0, i + 1, chunk, init)
        out = acc / den
        lse = mx + jnp.log(den)
        for h, cols in enumerate(heads):
            o_ref[:, cols] = out[h * tq:(h + 1) * tq]
            l_ref[:, cols] = jnp.broadcast_to(lse[h * tq:(h + 1) * tq], (tq, HEAD_DIM))

    bias = _attn_bias(s, tq)
    return pl.pallas_call(
        body, name=name, grid=(nkv, s // tq),
        in_specs=[qspec, kspec, vspec, pl.BlockSpec(bias.shape, lambda g, i: (0, 0, 0))], out_specs=[qspec, qspec],
        out_shape=[jax.ShapeDtypeStruct((s, qd), F32), jax.ShapeDtypeStruct((s, qd), F32)],
        compiler_params=_params("parallel", "parallel"),
    )(z, z, z, bias)


def _attn_bwd(name, z, o, lse, do, qd, kvd):
    s = z.shape[0]
    tq = _tile(s, 256)
    nkv = kvd // HEAD_DIM
    nq = s // tq
    rw, qspec, kspec, vspec = _attn_specs(s, qd, kvd, tq)
    scale = HEAD_DIM ** -0.5

    def body(q_ref, k_ref, v_ref, o_ref, l_ref, do_ref, b_ref, dq_ref, dk_ref, dv_ref, dk_acc, dv_acc):
        i = pl.program_id(1)
        heads = [slice(h * HEAD_DIM, (h + 1) * HEAD_DIM) for h in range(Q_PER_KV)]

        @pl.when(i == 0)
        def _():
            dk_acc[...] = jnp.zeros_like(dk_acc)
            dv_acc[...] = jnp.zeros_like(dv_acc)

        q_all = jnp.concatenate([q_ref[:, cols] for cols in heads], axis=0)
        do_all = jnp.concatenate([do_ref[:, cols].astype(BF16) for cols in heads], axis=0)
        lse_all = jnp.concatenate([l_ref[:, cols][:, :1] for cols in heads], axis=0)
        delta_all = jnp.concatenate(
            [jnp.sum(do_ref[:, cols] * o_ref[:, cols], axis=-1, keepdims=True) for cols in heads], axis=0)

        def chunk(j, dq):
            k0 = pl.multiple_of(j * tq, tq)
            kc, vc = k_ref[pl.ds(k0, tq), :], v_ref[pl.ds(k0, tq), :]
            p = jnp.exp(_biased(_dot(q_all, kc, "nt"), b_ref[i - j], scale) - lse_all)
            ds = (p * (_dot(do_all, vc, "nt") - delta_all) * scale).astype(BF16)
            dk_acc[pl.ds(k0, tq), :] += _dot(ds, q_all, "tn")
            dv_acc[pl.ds(k0, tq), :] += _dot(p.astype(BF16), do_all, "tn")
            return dq + _dot(ds, kc, "nn")

        dq = lax.fori_loop(0, i + 1, chunk, jnp.zeros((Q_PER_KV * tq, HEAD_DIM), F32))
        for h, cols in enumerate(heads):
            dq_ref[:, cols] = dq[h * tq:(h + 1) * tq].astype(dq_ref.dtype)

        @pl.when(i == nq - 1)
        def _():
            dk_ref[...] = dk_acc[...].astype(dk_ref.dtype)
            dv_ref[...] = dv_acc[...].astype(dv_ref.dtype)

    kvout = pl.BlockSpec((s, HEAD_DIM), lambda g, i: (0, g))
    bias = _attn_bias(s, tq)
    return pl.pallas_call(
        body, name=name, grid=(nkv, nq),
        in_specs=[qspec, kspec, vspec, qspec, qspec, qspec, pl.BlockSpec(bias.shape, lambda g, i: (0, 0, 0))],
        out_specs=[qspec, kvout, kvout],
        out_shape=[jax.ShapeDtypeStruct((s, qd), BF16), jax.ShapeDtypeStruct((s, kvd), BF16),
                   jax.ShapeDtypeStruct((s, kvd), BF16)],
        scratch_shapes=[pltpu.VMEM((s, HEAD_DIM), F32), pltpu.VMEM((s, HEAD_DIM), F32)],
        compiler_params=_params("parallel", "arbitrary"),
    )(z, z, z, o, lse, do, bias)


def _shift_down(v, n):
    rolled = pltpu.roll(v, n, 0)
    t = lax.broadcasted_iota(jnp.int32, v.shape, 0)
    return jnp.where(t >= n, rolled, 0.0)


def _shift_up(v, n):
    rows = v.shape[0]
    rolled = pltpu.roll(v, rows - n, 0)
    t = lax.broadcasted_iota(jnp.int32, v.shape, 0)
    return jnp.where(t < rows - n, rolled, 0.0)


def _conv_specs(s, base, cd, tc):
    zs = [pl.BlockSpec((s, tc), functools.partial(lambda j, off: (0, off + j), off=(base + n * cd) // tc))
          for n in range(3)]
    wspec = pl.BlockSpec((SUBLANES, tc), lambda j: (0, j))
    cspec = pl.BlockSpec((s, tc), lambda j: (0, j))
    return zs, wspec, cspec


def _conv_fwd(name, z, conv_w, base, cd):
    s = z.shape[0]
    tc = _tile(cd, 256)
    zs, wspec, cspec = _conv_specs(s, base, cd, tc)

    def body(h_ref, b_ref, c_ref, w_ref, o_ref):
        u = c_ref[...].astype(F32) * h_ref[...].astype(F32)
        y = w_ref[0:1, :] * _shift_down(u, 2) + w_ref[1:2, :] * _shift_down(u, 1) + w_ref[2:3, :] * u
        o_ref[...] = b_ref[...].astype(F32) * y

    return pl.pallas_call(
        body, name=name, grid=(cd // tc,), in_specs=zs + [wspec], out_specs=cspec,
        out_shape=jax.ShapeDtypeStruct((s, cd), F32), compiler_params=_params("parallel"),
    )(z, z, z, conv_w)


def _conv_bwd(name, z, conv_w, dc, base, cd):
    s = z.shape[0]
    tc = _tile(cd, 256)
    zs, wspec, cspec = _conv_specs(s, base, cd, tc)

    def body(h_ref, b_ref, c_ref, w_ref, dc_ref, dh_ref, db_ref, dcg_ref, dw_ref):
        hv, bv, cv = h_ref[...].astype(F32), b_ref[...].astype(F32), c_ref[...].astype(F32)
        u = cv * hv
        u1, u2 = _shift_down(u, 1), _shift_down(u, 2)
        w0, w1, w2 = w_ref[0:1, :], w_ref[1:2, :], w_ref[2:3, :]
        y = w0 * u2 + w1 * u1 + w2 * u
        dcv = dc_ref[...]
        db_ref[...] = (dcv * y).astype(db_ref.dtype)
        dy = dcv * bv
        du = w2 * dy + w1 * _shift_up(dy, 1) + w0 * _shift_up(dy, 2)
        dh_ref[...] = (du * cv).astype(dh_ref.dtype)
        dcg_ref[...] = (du * hv).astype(dcg_ref.dtype)
        g0 = jnp.sum(dy * u2, axis=0, keepdims=True)
        g1 = jnp.sum(dy * u1, axis=0, keepdims=True)
        g2 = jnp.sum(dy * u, axis=0, keepdims=True)
        r = lax.broadcasted_iota(jnp.int32, (SUBLANES, tc), 0)
        dw_ref[...] = jnp.where(r == 0, g0, jnp.where(r == 1, g1, jnp.where(r == 2, g2, 0.0)))

    return pl.pallas_call(
        body, name=name, grid=(cd // tc,), in_specs=zs + [wspec, cspec],
        out_specs=[cspec, cspec, cspec, wspec],
        out_shape=[jax.ShapeDtypeStruct((s, cd), BF16)] * 3 + [jax.ShapeDtypeStruct((SUBLANES, cd), F32)],
        compiler_params=_params("parallel"),
    )(z, z, z, conv_w, dc)


def _cat_norm_fwd(name, a, c, ga, gc):
    s, qd = a.shape
    cd = c.shape[1]
    tr = _row_tile(s, qd + cd)

    def body(a_ref, c_ref, ga_ref, gc_ref, o_ref):
        av, cv = a_ref[...], c_ref[...]
        ra = lax.rsqrt(jnp.mean(av * av, axis=-1, keepdims=True) + NORM_EPS)
        rc = lax.rsqrt(jnp.mean(cv * cv, axis=-1, keepdims=True) + NORM_EPS)
        o_ref[:, :qd] = (av * ra * ga_ref[...]).astype(o_ref.dtype)
        o_ref[:, qd:] = (cv * rc * gc_ref[...]).astype(o_ref.dtype)

    return pl.pallas_call(
        body, name=name, grid=(s // tr,),
        in_specs=[pl.BlockSpec((tr, qd), lambda i: (i, 0)), pl.BlockSpec((tr, cd), lambda i: (i, 0)),
                  pl.BlockSpec((1, qd), lambda i: (0, 0)), pl.BlockSpec((1, cd), lambda i: (0, 0))],
        out_specs=pl.BlockSpec((tr, qd + cd), lambda i: (i, 0)),
        out_shape=jax.ShapeDtypeStruct((s, qd + cd), BF16), compiler_params=_params("parallel"),
    )(a, c, ga, gc)


def _cat_norm_bwd(name, dcat, a, c, ga, gc):
    s, qd = a.shape
    cd = c.shape[1]
    tr = _row_tile(s, qd + cd)

    def one(dn, yv, gv):
        r = lax.rsqrt(jnp.mean(yv * yv, axis=-1, keepdims=True) + NORM_EPS)
        xhat = yv * r
        dxn = dn * gv
        return r * (dxn - xhat * jnp.mean(dxn * xhat, axis=-1, keepdims=True)), _sum_to_sublanes(dn * xhat)

    def body(d_ref, a_ref, c_ref, ga_ref, gc_ref, da_ref, dc_ref, dga_ref, dgc_ref):
        da, pa = one(d_ref[:, :qd], a_ref[...], ga_ref[...])
        dc, pc = one(d_ref[:, qd:], c_ref[...], gc_ref[...])
        da_ref[...] = da
        dc_ref[...] = dc

        @pl.when(pl.program_id(0) == 0)
        def _():
            dga_ref[...] = pa
            dgc_ref[...] = pc

        @pl.when(pl.program_id(0) > 0)
        def _():
            dga_ref[...] += pa
            dgc_ref[...] += pc

    ra = pl.BlockSpec((tr, qd), lambda i: (i, 0))
    rc = pl.BlockSpec((tr, cd), lambda i: (i, 0))
    return pl.pallas_call(
        body, name=name, grid=(s // tr,),
        in_specs=[pl.BlockSpec((tr, qd + cd), lambda i: (i, 0)), ra, rc,
                  pl.BlockSpec((1, qd), lambda i: (0, 0)), pl.BlockSpec((1, cd), lambda i: (0, 0))],
        out_specs=[ra, rc, pl.BlockSpec((SUBLANES, qd), lambda i: (0, 0)),
                   pl.BlockSpec((SUBLANES, cd), lambda i: (0, 0))],
        out_shape=[jax.ShapeDtypeStruct((s, qd), F32), jax.ShapeDtypeStruct((s, cd), F32),
                   jax.ShapeDtypeStruct((SUBLANES, qd), F32), jax.ShapeDtypeStruct((SUBLANES, cd), F32)],
        compiler_params=_params("arbitrary"),
    )(dcat, a, c, ga, gc)


def _adamw(name, w, g, m, v, emit_grad=False):
    shape = w.shape
    cols = shape[-1]
    rows = w.size // cols
    tr = _row_tile(rows, cols, budget=3 << 19)
    bc1 = 1.0 - ADAM_B1 ** ADAM_STEP
    bc2 = 1.0 - ADAM_B2 ** ADAM_STEP
    n_out = 4 if emit_grad else 3

    def body(w_ref, g_ref, m_ref, v_ref, d_ref, nm_ref, nv_ref, *g_out):
        gv = g_ref[...]
        mv = ADAM_B1 * m_ref[...] + (1.0 - ADAM_B1) * gv
        vv = ADAM_B2 * v_ref[...] + (1.0 - ADAM_B2) * (gv * gv)
        nm_ref[...] = mv
        nv_ref[...] = vv
        d_ref[...] = -ADAM_LR * ((mv / bc1) / (jnp.sqrt(vv / bc2) + ADAM_EPS) + ADAM_WD * w_ref[...])
        for ref in g_out:
            ref[...] = gv

    row = pl.BlockSpec((tr, cols), lambda i: (i, 0))
    outs = pl.pallas_call(
        body, name=name, grid=(rows // tr,), in_specs=[row] * 4, out_specs=[row] * n_out,
        out_shape=[jax.ShapeDtypeStruct((rows, cols), F32)] * n_out, compiler_params=_params("parallel"),
    )(*(t.reshape(rows, cols) for t in (w, g, m, v)))
    return tuple(t.reshape(shape) for t in outs)


HBM_SPEC = pl.BlockSpec(memory_space=pltpu.HBM)


def _mesh_place():
    x, y, c = lax.axis_index("x"), lax.axis_index("y"), lax.axis_index("c")
    other_chips = [(1 - x, y), (x, 1 - y), (1 - x, 1 - y)]
    return x, y, c, other_chips


def _cast_into_slot(name, w, layer, chip, deps=()):
    _, r, cols = w.shape
    tr = _row_tile(r, cols, budget=BIG_BLOCK)

    def body(chip_ref, w_ref, *rest):
        o_ref = rest[-1]
        o_ref[...] = w_ref[...].astype(o_ref.dtype)

    return pl.pallas_call(
        body, name=name,
        grid_spec=pltpu.PrefetchScalarGridSpec(
            num_scalar_prefetch=1, grid=(r // tr,),
            in_specs=[pl.BlockSpec((None, tr, cols), lambda i, chip_ref: (layer, i, 0))] + [ANY_SPEC] * len(deps),
            out_specs=pl.BlockSpec((None, tr, cols), lambda i, chip_ref: (chip_ref[0], i, 0))),
        out_shape=jax.ShapeDtypeStruct((N_CHIPS, r, cols), BF16), compiler_params=_params("parallel"),
    )(chip, w, *deps)


SEM_SPEC = pl.BlockSpec(memory_space=pltpu.SEMAPHORE)
SPLIT_COPY = pltpu.CompilerParams(has_side_effects=pltpu.SideEffectType.DATAFLOW_SIDE_EFFECTING)
N_OTHER = N_CHIPS - 1
TOKEN_SPEC = pl.BlockSpec(memory_space=pltpu.VMEM)
TOKEN_SHAPE = jax.ShapeDtypeStruct((SUBLANES, LANES), F32)


def _in_hbm(arr):
    return pltpu.with_memory_space_constraint(arr, pltpu.HBM)


def _half_rows(ref, chip_idx, core):
    r2 = ref.shape[1] // 2
    return ref.at[chip_idx, pl.ds(core * r2, r2), :]


def _gather_start(name, fulls, after):
    na = len(fulls)

    def body(*refs):
        f_refs = refs[na + 1:2 * na + 1]
        send_sems, recv_sems = refs[2 * na + 1:3 * na + 1], refs[3 * na + 1:4 * na + 1]
        token = refs[4 * na + 1]
        x, y, c, chips = _mesh_place()
        for a in range(na):
            mine = _half_rows(f_refs[a], 2 * x + y, c)
            for j, (cx, cy) in enumerate(chips):
                pltpu.make_async_remote_copy(
                    src_ref=mine, dst_ref=mine, send_sem=send_sems[a].at[j], recv_sem=recv_sems[a].at[j],
                    device_id=(cx, cy, c), device_id_type=MESH).start()
        token[...] = jnp.zeros_like(token)

    outs = pl.pallas_call(
        body, name=name, in_specs=[HBM_SPEC] * na + [ANY_SPEC],
        out_specs=[HBM_SPEC] * na + [SEM_SPEC] * (2 * na) + [TOKEN_SPEC],
        out_shape=[pltpu.HBM(f.shape, f.dtype) for f in fulls] + [pltpu.SemaphoreType.DMA((N_OTHER,))] * (2 * na)
        + [TOKEN_SHAPE],
        input_output_aliases={a: a for a in range(na)}, compiler_params=SPLIT_COPY,
    )(*[_in_hbm(f) for f in fulls], after)
    return list(outs[:na]), list(outs[na:2 * na]), list(outs[2 * na:3 * na]), outs[3 * na]


def _gather_pass_on(name, full, recv_sems, after):
    def body(f_in, recv_sems, after_ref, f_ref, d2d_send, d2d_recv):
        x, y, c, chips = _mesh_place()
        for j, (cx, cy) in enumerate(chips):
            blk = _half_rows(f_ref, 2 * cx + cy, c)
            pltpu.make_async_remote_copy(
                src_ref=blk, dst_ref=blk, send_sem=d2d_send.at[j], recv_sem=recv_sems.at[j],
                device_id=(cx, cy, c), device_id_type=MESH).wait_recv()
            pltpu.make_async_remote_copy(
                src_ref=blk, dst_ref=blk, send_sem=d2d_send.at[j], recv_sem=d2d_recv.at[j],
                device_id=(x, y, 1 - c), device_id_type=MESH).start()

    return pl.pallas_call(
        body, name=name, in_specs=[HBM_SPEC, SEM_SPEC, ANY_SPEC], out_specs=[HBM_SPEC, SEM_SPEC, SEM_SPEC],
        out_shape=[pltpu.HBM(full.shape, full.dtype)] + [pltpu.SemaphoreType.DMA((N_OTHER,))] * 2,
        input_output_aliases={0: 0}, compiler_params=SPLIT_COPY,
    )(full, recv_sems, after)


def _gather_arrive(name, full, ici_send, d2d_send, d2d_recv, after):
    def body(f_in, ici_send, d2d_send, d2d_recv, after_ref, f_ref):
        x, y, c, chips = _mesh_place()
        for j, (cx, cy) in enumerate(chips):
            mine = _half_rows(f_ref, 2 * x + y, c)
            passed = _half_rows(f_ref, 2 * cx + cy, c)
            theirs = _half_rows(f_ref, 2 * cx + cy, 1 - c)
            pltpu.make_async_remote_copy(
                src_ref=mine, dst_ref=mine, send_sem=ici_send.at[j], recv_sem=d2d_recv.at[j],
                device_id=(cx, cy, c), device_id_type=MESH).wait_send()
            pltpu.make_async_remote_copy(
                src_ref=passed, dst_ref=passed, send_sem=d2d_send.at[j], recv_sem=d2d_recv.at[j],
                device_id=(x, y, 1 - c), device_id_type=MESH).wait_send()
            pltpu.make_async_remote_copy(
                src_ref=theirs, dst_ref=theirs, send_sem=d2d_send.at[j], recv_sem=d2d_recv.at[j],
                device_id=(x, y, 1 - c), device_id_type=MESH).wait_recv()

    return pl.pallas_call(
        body, name=name, in_specs=[HBM_SPEC, SEM_SPEC, SEM_SPEC, SEM_SPEC, ANY_SPEC], out_specs=HBM_SPEC,
        out_shape=pltpu.HBM(full.shape, full.dtype), input_output_aliases={0: 0}, compiler_params=SPLIT_COPY,
    )(full, ici_send, d2d_send, d2d_recv, after)


def _gather_taps(conv_w):
    def body(cw_ref, cwf_ref, send_sems, recv_sems, local_sem):
        x, y, c, chips = _mesh_place()
        k_me = 2 * x + y
        local = pltpu.make_async_copy(cw_ref, cwf_ref.at[k_me], local_sem)
        local.start()
        copies = [pltpu.make_async_remote_copy(
            src_ref=cw_ref, dst_ref=cwf_ref.at[k_me], send_sem=send_sems.at[j], recv_sem=recv_sems.at[j],
            device_id=(cx, cy, c), device_id_type=MESH) for j, (cx, cy) in enumerate(chips)]
        for cp in copies:
            cp.start()
        for j, (cx, cy) in enumerate(chips):
            pltpu.make_async_remote_copy(
                src_ref=cw_ref, dst_ref=cwf_ref.at[2 * cx + cy], send_sem=send_sems.at[j], recv_sem=recv_sems.at[j],
                device_id=(cx, cy, c), device_id_type=MESH).wait_recv()
        for cp in copies:
            cp.wait_send()
        local.wait()

    return pl.pallas_call(
        body, name="gather_taps", in_specs=[HBM_SPEC], out_specs=HBM_SPEC,
        out_shape=jax.ShapeDtypeStruct((N_CHIPS,) + conv_w.shape, conv_w.dtype),
        scratch_shapes=[pltpu.SemaphoreType.DMA((N_OTHER,))] * 2 + [pltpu.SemaphoreType.DMA],
    )(conv_w)


def _sibling_half(g_ref, c):
    r2 = g_ref.shape[1] // 2
    return g_ref.at[:, pl.ds((1 - c) * r2, r2), :]


def _swap_copy(g_ref, land_ref, send_sems, recv_sems, a):
    x, y, c, _ = _mesh_place()
    return pltpu.make_async_remote_copy(
        src_ref=_sibling_half(g_ref, c), dst_ref=land_ref, send_sem=send_sems.at[a], recv_sem=recv_sems.at[a],
        device_id=(x, y, 1 - c), device_id_type=MESH)


def _swap_start(name, gs):
    n = len(gs)

    def body(*refs):
        g_refs, land_refs = refs[n:2 * n], refs[2 * n:3 * n]
        send_sems, recv_sems, token = refs[3 * n:]
        for a in range(n):
            _swap_copy(g_refs[a], land_refs[a], send_sems, recv_sems, a).start()
        token[...] = jnp.zeros_like(token)

    outs = pl.pallas_call(
        body, name=name, in_specs=[HBM_SPEC] * n,
        out_specs=[HBM_SPEC] * (2 * n) + [SEM_SPEC, SEM_SPEC, TOKEN_SPEC],
        out_shape=[pltpu.HBM(g.shape, g.dtype) for g in gs]
        + [pltpu.HBM((g.shape[0], g.shape[1] // 2, g.shape[2]), g.dtype) for g in gs]
        + [pltpu.SemaphoreType.DMA((n,)), pltpu.SemaphoreType.DMA((n,)), TOKEN_SHAPE],
        input_output_aliases={a: a for a in range(n)}, compiler_params=SPLIT_COPY,
    )(*[_in_hbm(g) for g in gs])
    return list(outs[:n]), list(outs[n:2 * n]), outs[2 * n], outs[2 * n + 1], outs[2 * n + 2]


def _swap_wait(name, gs, lands, send_sems, recv_sems, after):
    n = len(gs)

    def body(*refs):
        send_sems, recv_sems = refs[2 * n], refs[2 * n + 1]
        g_refs, land_refs = refs[2 * n + 3:3 * n + 3], refs[3 * n + 3:]
        for a in range(n):
            copy = _swap_copy(g_refs[a], land_refs[a], send_sems, recv_sems, a)
            copy.wait_send()
            copy.wait_recv()

    outs = pl.pallas_call(
        body, name=name, in_specs=[HBM_SPEC] * (2 * n) + [SEM_SPEC, SEM_SPEC, ANY_SPEC],
        out_specs=[HBM_SPEC] * (2 * n),
        out_shape=[pltpu.HBM(t.shape, t.dtype) for t in list(gs) + list(lands)],
        input_output_aliases={a: a for a in range(2 * n)}, compiler_params=SPLIT_COPY,
    )(*gs, *lands, send_sems, recv_sems, after)
    return list(outs[:n]), list(outs[n:])


def _add_core_halves(name, g, sib, core):
    nb, r, cols = g.shape
    r2 = r // 2
    tr = _row_tile(r2, cols, itemsize=2, budget=BIG_BLOCK)
    nrt = r2 // tr

    def body(core_ref, g_ref, s_ref, o_ref):
        o_ref[...] = (g_ref[...].astype(F32) + s_ref[...].astype(F32)).astype(o_ref.dtype)

    return pl.pallas_call(
        body, name=name,
        grid_spec=pltpu.PrefetchScalarGridSpec(
            num_scalar_prefetch=1, grid=(nb, nrt),
            in_specs=[pl.BlockSpec((None, tr, cols), lambda k, i, core_ref: (k, core_ref[0] * nrt + i, 0)),
                      pl.BlockSpec((None, tr, cols), lambda k, i, core_ref: (k, i, 0))],
            out_specs=pl.BlockSpec((None, tr, cols), lambda k, i, core_ref: (k, i, 0))),
        out_shape=jax.ShapeDtypeStruct((nb, r2, cols), BF16), compiler_params=_params("parallel", "parallel"),
    )(core, g, sib)


def _scatter_copies(h_refs, land_refs, send_sems, recv_sems):
    x, y, c, chips = _mesh_place()
    return [pltpu.make_async_remote_copy(
        src_ref=h_ref.at[2 * cx + cy], dst_ref=land_ref.at[j],
        send_sem=send_sems.at[a * N_OTHER + j], recv_sem=recv_sems.at[a * N_OTHER + j],
        device_id=(cx, cy, c), device_id_type=MESH)
        for a, (h_ref, land_ref) in enumerate(zip(h_refs, land_refs)) for j, (cx, cy) in enumerate(chips)]


def _scatter_start(name, hs):
    n = len(hs)

    def body(*refs):
        h_refs, land_refs = refs[n:2 * n], refs[2 * n:3 * n]
        send_sems, recv_sems, token = refs[3 * n:]
        for copy in _scatter_copies(h_refs, land_refs, send_sems, recv_sems):
            copy.start()
        token[...] = jnp.zeros_like(token)

    outs = pl.pallas_call(
        body, name=name, in_specs=[HBM_SPEC] * n,
        out_specs=[HBM_SPEC] * (2 * n) + [SEM_SPEC, SEM_SPEC, TOKEN_SPEC],
        out_shape=[pltpu.HBM(h.shape, h.dtype) for h in hs]
        + [pltpu.HBM((N_OTHER,) + h.shape[1:], h.dtype) for h in hs]
        + [pltpu.SemaphoreType.DMA((n * N_OTHER,)), pltpu.SemaphoreType.DMA((n * N_OTHER,)), TOKEN_SHAPE],
        input_output_aliases={a: a for a in range(n)}, compiler_params=SPLIT_COPY,
    )(*[_in_hbm(h) for h in hs])
    return list(outs[:n]), list(outs[n:2 * n]), outs[2 * n], outs[2 * n + 1], outs[2 * n + 2]


def _scatter_wait(name, hs, lands, send_sems, recv_sems, after):
    afters = tuple(after) if isinstance(after, (tuple, list)) else (after,)
    n = len(hs)

    def body(*refs):
        send_sems, recv_sems = refs[2 * n], refs[2 * n + 1]
        h_refs, land_refs = refs[-2 * n:-n], refs[-n:]
        for copy in _scatter_copies(h_refs, land_refs, send_sems, recv_sems):
            copy.wait_send()
            copy.wait_recv()

    outs = pl.pallas_call(
        body, name=name, in_specs=[HBM_SPEC] * (2 * n) + [SEM_SPEC, SEM_SPEC] + [ANY_SPEC] * len(afters),
        out_specs=[HBM_SPEC] * (2 * n),
        out_shape=[pltpu.HBM(t.shape, t.dtype) for t in list(hs) + list(lands)],
        input_output_aliases={a: a for a in range(2 * n)}, compiler_params=SPLIT_COPY,
    )(*hs, *lands, send_sems, recv_sems, *afters)
    return list(outs[:n]), list(outs[n:])


def _sum_chips(name, hs, rcv, core, chip, layer, n_layers, prev):
    _, r2, cols = hs.shape
    tr = _row_tile(r2, cols, budget=BIG_BLOCK)
    nrt = r2 // tr

    def body(core_ref, chip_ref, h_ref, r_ref, *rest):
        o_ref = rest[-1]
        acc = h_ref[...].astype(F32)
        for j in range(N_CHIPS - 1):
            acc = acc + r_ref[j].astype(F32)
        o_ref[...] = acc

    in_specs = [pl.BlockSpec((None, tr, cols), lambda i, core_ref, chip_ref: (chip_ref[0], i, 0)),
                pl.BlockSpec((N_CHIPS - 1, tr, cols), lambda i, core_ref, chip_ref: (0, i, 0))]
    args = [core, chip, hs, rcv]
    aliases = {}
    if prev is not None:
        in_specs.append(pl.BlockSpec(memory_space=pl.ANY))
        args.append(prev)
        aliases = {4: 0}
    return pl.pallas_call(
        body, name=name,
        grid_spec=pltpu.PrefetchScalarGridSpec(
            num_scalar_prefetch=2, grid=(nrt,), in_specs=in_specs,
            out_specs=pl.BlockSpec((None, tr, cols), lambda i, core_ref, chip_ref: (layer, core_ref[0] * nrt + i, 0))),
        out_shape=jax.ShapeDtypeStruct((n_layers, 2 * r2, cols), F32), input_output_aliases=aliases,
        compiler_params=_params("parallel"),
    )(*args)


def _join_copy(t_ref, send_sems, recv_sems, a):
    x, y, c, _ = _mesh_place()
    r2 = t_ref.shape[1] // 2
    mine = t_ref.at[:, pl.ds(c * r2, r2), :]
    return pltpu.make_async_remote_copy(
        src_ref=mine, dst_ref=mine, send_sem=send_sems.at[a], recv_sem=recv_sems.at[a],
        device_id=(x, y, 1 - c), device_id_type=MESH)


def _join_start(name, ts, deps=()):
    n, nd = len(ts), len(deps)

    def body(*refs):
        t_refs = refs[n + nd:2 * n + nd]
        send_sems, recv_sems = refs[2 * n + nd:]
        for a in range(n):
            _join_copy(t_refs[a], send_sems, recv_sems, a).start()

    outs = pl.pallas_call(
        body, name=name, in_specs=[HBM_SPEC] * n + [ANY_SPEC] * nd, out_specs=[HBM_SPEC] * n + [SEM_SPEC, SEM_SPEC],
        out_shape=[pltpu.HBM(t.shape, t.dtype) for t in ts] + [pltpu.SemaphoreType.DMA((n,))] * 2,
        input_output_aliases={a: a for a in range(n)}, compiler_params=SPLIT_COPY,
    )(*[_in_hbm(t) for t in ts], *deps)
    return list(outs[:n]), outs[n], outs[n + 1]


def _join_wait(name, t, a, send_sems, recv_sems, after):
    def body(t_in, send_sems, recv_sems, after_ref, t_ref):
        copy = _join_copy(t_ref, send_sems, recv_sems, a)
        copy.wait_send()
        copy.wait_recv()

    return pl.pallas_call(
        body, name=name, in_specs=[HBM_SPEC, SEM_SPEC, SEM_SPEC, ANY_SPEC], out_specs=HBM_SPEC,
        out_shape=pltpu.HBM(t.shape, t.dtype), input_output_aliases={0: 0}, compiler_params=SPLIT_COPY,
    )(t, send_sems, recv_sems, after)


def _allreduce_small(p):
    n, _, w = p.shape

    def body(p_ref, o_ref, buf, send_sems, recv_sems):
        x, y, c, _ = _mesh_place()
        me = 4 * x + 2 * y + c
        buf[me] = jnp.sum(p_ref[...], axis=1)
        copies = []
        for pat in range(1, N_DEV):
            fx, fy, fc = (pat >> 2) & 1, (pat >> 1) & 1, pat & 1
            copies.append(pltpu.make_async_remote_copy(
                src_ref=buf.at[me], dst_ref=buf.at[me], send_sem=send_sems.at[pat - 1], recv_sem=recv_sems.at[pat - 1],
                device_id=(x ^ fx, y ^ fy, c ^ fc), device_id_type=MESH))
        for cp in copies:
            cp.start()
        for cp in copies:
            cp.wait()
        acc = buf[0]
        for dev in range(1, N_DEV):
            acc = acc + buf[dev]
        o_ref[...] = acc

    return pl.pallas_call(
        body, name="allreduce_small", in_specs=[pl.BlockSpec(memory_space=pltpu.VMEM)],
        out_specs=pl.BlockSpec(memory_space=pltpu.VMEM), out_shape=jax.ShapeDtypeStruct((n, w), F32),
        scratch_shapes=[pltpu.VMEM((N_DEV, n, w), F32), pltpu.SemaphoreType.DMA((N_DEV - 1,)),
                        pltpu.SemaphoreType.DMA((N_DEV - 1,))],
    )(p)


class _WeightFeed:
    def __init__(self):
        self.fulls, self.ici_send, self.ici_recv, self.d2d = [], [], [], []

    def start(self, name, fulls, after):
        started, send, recv, token = _gather_start(name, fulls, after)
        self.fulls += started
        self.ici_send += send
        self.ici_recv += recv
        self.d2d += [None] * len(fulls)
        self.token = token
        return token

    def _pass_on(self, k, after):
        if k == 0:
            after = self.token
        if k < len(self.fulls) and self.d2d[k] is None:
            self.fulls[k], send, recv = _gather_pass_on(f"gather_pass_{k}", self.fulls[k], self.ici_recv[k], after)
            self.d2d[k] = (send, recv)

    def take(self, k, after):
        self._pass_on(k, after)
        self.fulls[k] = _gather_arrive(f"gather_arrive_{k}", self.fulls[k], self.ici_send[k], *self.d2d[k], after)
        return self.fulls[k]


def _ffn_forward(tag, x, h, g_post, next_gain, feed, k):
    s, d = x.shape
    gu_w = feed.take(k, h)
    gu, a = _ffn_up(f"{tag}_up", h, gu_w)
    dn_w = feed.take(k + 1, a).reshape(-1, d)
    f = dn_w.shape[0]
    tm, tn = _tile(s, 1024), _tile(d, 512)
    y = _mm(f"{tag}_down", a, dn_w, mode="nn", grid=(s // tm, d // tn),
            a_spec=pl.BlockSpec((tm, f), lambda i, j: (i, 0)),
            b_spec=pl.BlockSpec((f, tn), lambda i, j: (0, j)),
            o_spec=pl.BlockSpec((tm, tn), lambda i, j: (i, j)),
            out_shape=jax.ShapeDtypeStruct((s, d), F32))
    x_new, h_next = _res_norm(f"{tag}_post", x, y, g_post, FFN_RESIDUAL_WEIGHT, next_gain)
    return x_new, h_next, (x, h, gu, a, y)


class _GradReduce:
    def __init__(self, core, chip, n_layers):
        self.core, self.chip, self.n_layers = core, chip, n_layers
        self.state = {}
        self.bufs = {}
        self.scatter_tokens = {}

    def start(self, kinds, layer, gs):
        gs, lands, send, recv, token = _swap_start(f"swap_start_{kinds[0]}_{layer}", gs)
        self.state[kinds, layer] = (gs, lands, send, recv)
        return token

    def exchange(self, kinds, layer, after):
        tag = f"{kinds[0]}_{layer}"
        gs, sibs = _swap_wait(f"swap_wait_{tag}", *self.state[kinds, layer], after)
        hs = [_add_core_halves(f"add_cores_{k}_{layer}", g, sib, self.core) for k, g, sib in zip(kinds, gs, sibs)]
        hs, lands, send, recv, token = _scatter_start(f"scatter_start_{tag}", hs)
        self.state[kinds, layer] = (hs, lands, send, recv)
        self.scatter_tokens[kinds, layer] = token
        return token

    def finish(self, kinds, layer, after):
        tag = f"{kinds[0]}_{layer}"
        hs, rcvs = _scatter_wait(f"scatter_wait_{tag}", *self.state.pop((kinds, layer)), after)
        for k, h, rcv in zip(kinds, hs, rcvs):
            self.bufs[k] = _sum_chips(f"sum_chips_{k}_{layer}", h, rcv, self.core, self.chip, layer, self.n_layers,
                                      self.bufs.get(k))
        return self.bufs[kinds[-1]]


def _ffn_backward(tag, dx_new, saved, g_pre, g_post, gu_w, dn_w, red, kinds, layer, deps, head, following,
                  last=None):
    x, h, gu, a, y = saved
    s, d = x.shape
    nb, fs = gu_w.shape[0], gu_w.shape[2]
    f = dn_w.shape[0]
    fr = f // nb
    dy, dg_post = head or _norm_bwd(f"{tag}_post_bwd", dx_new, y, g_post, FFN_RESIDUAL_WEIGHT, None, BF16)
    dgu = _ffn_dact(f"{tag}_dact", dy, dn_w, gu, deps)
    dgu4 = dgu.reshape(nb, s, fs)
    tm, tw = _tile(d, 512), _tile(fs, 1408)
    nw = fs // tw
    tn = _tile(d, 1024)
    ts, td = _tile(s, 1024), _tile(d, 1024)

    def gate_up_gradient(deps):
        return _mm(f"{tag}_dwgu", h, dgu4, mode="tn", grid=(nb, nw, d // tm),
                   a_spec=pl.BlockSpec((s, tm), lambda k, j, i: (0, i)),
                   b_spec=pl.BlockSpec((None, s, tw), lambda k, j, i: (k, 0, j)),
                   o_spec=pl.BlockSpec((None, tm, tw), lambda k, j, i: (k, i, j)),
                   out_shape=jax.ShapeDtypeStruct((nb, d, fs), BF16), deps=deps)

    def down_gradient(deps):
        return _mm(f"{tag}_dwd", a, dy, mode="tn", grid=(nb, d // tn),
                   a_spec=pl.BlockSpec((s, fr), lambda i, j: (0, i)),
                   b_spec=pl.BlockSpec((s, tn), lambda i, j: (0, j)),
                   o_spec=pl.BlockSpec((None, fr, tn), lambda i, j: (i, 0, j)),
                   out_shape=jax.ShapeDtypeStruct((nb, fr, d), BF16), deps=deps)

    def input_gradient(deps):
        dh = _mm(f"{tag}_dh", dgu4, gu_w, mode="nt", grid=(s // ts, d // td, nb),
                 a_spec=pl.BlockSpec((None, ts, fs), lambda i, j, k: (k, i, 0)),
                 b_spec=pl.BlockSpec((None, td, fs), lambda i, j, k: (k, j, 0)),
                 o_spec=pl.BlockSpec((ts, td), lambda i, j, k: (i, j)),
                 out_shape=jax.ShapeDtypeStruct((s, d), F32), nk=nb, acc_shape=(ts, td), deps=deps)
        return _norm_bwd(f"{tag}_pre_bwd", dh, x, g_pre, 1.0, dx_new, F32, following)

    if last is None:
        started = red.start(kinds, layer, [gate_up_gradient(()), down_gradient(())])
        dx, dg_pre, *next_head = input_gradient((started,))
    else:
        dx, dg_pre, *next_head = input_gradient(())
        first = red.start(kinds[:1], layer, [gate_up_gradient((last(dg_pre, dg_post),))])
        second = red.start(kinds[1:], layer, [down_gradient((first,))])
        red.exchange(kinds[:1], layer, second)
    return dx, dg_pre, dg_post, tuple(next_head) or None


def _mixer_forward(tag, x, h, gains, next_gain, feed, k, conv_taps, dims):
    qd, kvd, cd = dims
    s, d = x.shape
    _, g_a, g_c, g_post = gains
    win_w = feed.take(k, h)
    nb, cw = win_w.shape[0], win_w.shape[2]
    tm = _tile(s, 1024)
    z = _mm(f"{tag}_in", h, win_w, mode="nn", grid=(nb, s // tm),
            a_spec=pl.BlockSpec((tm, d), lambda j, i: (i, 0)),
            b_spec=pl.BlockSpec((None, d, cw), lambda j, i: (j, 0, 0)),
            o_spec=pl.BlockSpec((tm, cw), lambda j, i: (i, j)),
            out_shape=jax.ShapeDtypeStruct((s, nb * cw), BF16))
    a, lse = _attn_fwd(f"{tag}_attn", z, qd, kvd)
    c = _conv_fwd(f"{tag}_conv", z, conv_taps, qd + 2 * kvd, cd)
    cat = _cat_norm_fwd(f"{tag}_cat", a, c, g_a, g_c)
    wout_w = feed.take(k + 1, cat).reshape(-1, d)
    mw = qd + cd
    tn = _tile(d, 1024)
    mixed = _mm(f"{tag}_out", cat, wout_w, mode="nn", grid=(s // tm, d // tn),
                a_spec=pl.BlockSpec((tm, mw), lambda i, j: (i, 0)),
                b_spec=pl.BlockSpec((mw, tn), lambda i, j: (0, j)),
                o_spec=pl.BlockSpec((tm, tn), lambda i, j: (i, j)),
                out_shape=jax.ShapeDtypeStruct((s, d), F32))
    x_new, h_next = _res_norm(f"{tag}_post", x, mixed, g_post, 1.0, next_gain)
    return x_new, h_next, (x, h, z, a, lse, c, cat, mixed)


def _mixer_backward(tag, dx_new, saved, gains, win_w, conv_taps, wout_w, dims, red, kinds, layer, deps, head,
                    following):
    qd, kvd, cd = dims
    x, h, z, a, lse, c, cat, mixed = saved
    s, d = x.shape
    nb, cw = win_w.shape[0], win_w.shape[2]
    g_pre, g_a, g_c, g_post = gains
    mw = qd + cd
    dmixed, dg_post = head or _norm_bwd(f"{tag}_post_bwd", dx_new, mixed, g_post, 1.0, None, BF16)
    tm, tn = _tile(s, 1024), _tile(mw, 1024)
    dcat = _mm(f"{tag}_dcat", dmixed, wout_w, mode="nt", grid=(s // tm, mw // tn),
               a_spec=pl.BlockSpec((tm, d), lambda i, j: (i, 0)),
               b_spec=pl.BlockSpec((tn, d), lambda i, j: (j, 0)),
               o_spec=pl.BlockSpec((tm, tn), lambda i, j: (i, j)),
               out_shape=jax.ShapeDtypeStruct((s, mw), F32), deps=deps)
    wr = mw // nb
    td = _tile(d, 1024)
    d_wout = _mm(f"{tag}_dwout", cat, dmixed, mode="tn", grid=(nb, d // td),
                 a_spec=pl.BlockSpec((s, wr), lambda i, j: (0, i)),
                 b_spec=pl.BlockSpec((s, td), lambda i, j: (0, j)),
                 o_spec=pl.BlockSpec((None, wr, td), lambda i, j: (i, 0, j)),
                 out_shape=jax.ShapeDtypeStruct((nb, wr, d), BF16))
    da, dc, dg_a, dg_c = _cat_norm_bwd(f"{tag}_cat_bwd", dcat, a, c, g_a, g_c)
    dhc, dbg, dcg, d_taps = _conv_bwd(f"{tag}_conv_bwd", z, conv_taps, dc, qd + 2 * kvd, cd)
    dq, dk, dv = _attn_bwd(f"{tag}_attn_bwd", z, a, lse, da, qd, kvd)
    dz = jnp.concatenate([dq, dk, dv, dhc, dbg, dcg], axis=1)
    th = _tile(d, 512)
    d_win = _mm(f"{tag}_dwin", h, dz, mode="tn", grid=(nb, d // th),
                a_spec=pl.BlockSpec((s, th), lambda k, i: (0, i)),
                b_spec=pl.BlockSpec((s, cw), lambda k, i: (0, k)),
                o_spec=pl.BlockSpec((None, th, cw), lambda k, i: (k, i, 0)),
                out_shape=jax.ShapeDtypeStruct((nb, d, cw), BF16))
    started = (red.start(kinds, layer, [d_win, d_wout]),)
    dh = _mm(f"{tag}_dh", dz, win_w, mode="nt", grid=(s // tm, d // td, nb),
             a_spec=pl.BlockSpec((tm, cw), lambda i, j, k: (i, k)),
             b_spec=pl.BlockSpec((None, td, cw), lambda i, j, k: (k, j, 0)),
             o_spec=pl.BlockSpec((tm, td), lambda i, j, k: (i, j)),
             out_shape=jax.ShapeDtypeStruct((s, d), F32), nk=nb, acc_shape=(tm, td), deps=started)
    dx, dg_pre, *next_head = _norm_bwd(f"{tag}_pre_bwd", dh, x, g_pre, 1.0, dx_new, F32, following)
    return dx, d_taps, (dg_pre, dg_a, dg_c, dg_post), tuple(next_head) or None


def _pad_cols(v, width):
    return jnp.pad(v, ((0, 0), (0, width - v.shape[1])))


def kernel(x, ffn1_norm_pre, ffn1_w_gate_up, ffn1_w_down, ffn1_norm_post, mix_norm_pre, w_in, conv_w, attn_out_norm, conv_out_norm, w_out, mix_norm_post, ffn2_norm_pre, ffn2_w_gate_up, ffn2_w_down, ffn2_norm_post, loss_target, m_ffn1_norm_pre, m_ffn1_w_gate_up, m_ffn1_w_down, m_ffn1_norm_post, m_mix_norm_pre, m_w_in, m_conv_w, m_attn_out_norm, m_conv_out_norm, m_w_out, m_mix_norm_post, m_ffn2_norm_pre, m_ffn2_w_gate_up, m_ffn2_w_down, m_ffn2_norm_post, v_ffn1_norm_pre, v_ffn1_w_gate_up, v_ffn1_w_down, v_ffn1_norm_post, v_mix_norm_pre, v_w_in, v_conv_w, v_attn_out_norm, v_conv_out_norm, v_w_out, v_mix_norm_post, v_ffn2_norm_pre, v_ffn2_w_gate_up, v_ffn2_w_down, v_ffn2_norm_post):
    _, s, d = x.shape
    n_layers = ffn1_norm_pre.shape[0]
    qd = attn_out_norm.shape[1]
    cd = conv_out_norm.shape[1]
    kvd = qd // Q_PER_KV
    dims = (qd, kvd, cd)
    assert N_CHIPS * w_in.shape[2] == qd + 2 * kvd + 3 * cd and qd + cd == N_CHIPS * w_out.shape[1]
    assert 2 * d <= SMALL_ROWS * LANES * SUBLANES
    chip = 2 * lax.axis_index("x") + lax.axis_index("y")
    chip_arr = chip.astype(jnp.int32).reshape(1)
    core = lax.axis_index("c").astype(jnp.int32).reshape(1)
    kinds = ("gu1", "dn1", "win", "wout", "gu2", "dn2")

    big = (ffn1_w_gate_up, ffn1_w_down, w_in, w_out, ffn2_w_gate_up, ffn2_w_down)
    nk = len(kinds)
    taps_all = _gather_taps(conv_w)
    feed = _WeightFeed()
    order = [(k, w, layer) for layer in range(n_layers) for k, w in zip(kinds, big)]
    k, w, layer = order[0]
    token = feed.start("gather_start_first", [_cast_into_slot(f"cast_{k}_{layer}", w, layer, chip_arr)], taps_all)
    feed.start("gather_start_rest", [_cast_into_slot(f"cast_{k}_{layer}", w, layer, chip_arr, (token,))
                                     for k, w, layer in order[1:]], token)
    taps = jnp.transpose(taps_all, (1, 2, 0, 3)).reshape(n_layers, CONV_WIDTH, cd)
    taps = jnp.pad(taps, ((0, 0), (0, SUBLANES - CONV_WIDTH), (0, 0)))

    def gain(g, layer):
        return g[layer][None, :]

    xs = x[0]
    hs = _norm_fwd("l0_ffn1_norm", xs, gain(ffn1_norm_pre, 0))
    saved = []
    for layer in range(n_layers):
        t = f"l{layer}"
        k0 = layer * nk
        xs, hs, s1 = _ffn_forward(f"{t}_ffn1", xs, hs, gain(ffn1_norm_post, layer), gain(mix_norm_pre, layer), feed, k0)
        mix_gains = (gain(mix_norm_pre, layer), gain(attn_out_norm, layer), gain(conv_out_norm, layer), gain(mix_norm_post, layer))
        xs, hs, s2 = _mixer_forward(f"{t}_mix", xs, hs, mix_gains, gain(ffn2_norm_pre, layer), feed, k0 + 2,
                                    taps[layer], dims)
        following = gain(ffn1_norm_pre, layer + 1) if layer + 1 < n_layers else None
        xs, hs, s3 = _ffn_forward(f"{t}_ffn2", xs, hs, gain(ffn2_norm_post, layer), following, feed, k0 + 4)
        saved.append((s1, s2, s3, mix_gains))
    wts = {k: [feed.fulls[layer * nk + i] for layer in range(n_layers)] for i, k in enumerate(kinds)}
    for k in ("dn1", "wout", "dn2"):
        wts[k] = [w.reshape(-1, d) for w in wts[k]]
    dxs, loss_part = _loss_head("loss_head", xs, loss_target[0])
    loss = lax.psum(jnp.sum(loss_part), ("x", "y", "c"))

    red = _GradReduce(core, chip_arr, n_layers)
    small = [None] * n_layers
    flow = {"deps": (), "in_flight": None}

    def between(dx, group):
        after = dx
        if flow["in_flight"] is not None:
            after = red.finish(*flow["in_flight"], after)
        flow["deps"] = (red.exchange(*group, after),)
        flow["in_flight"] = group

    head = None
    for layer in reversed(range(n_layers)):
        t = f"l{layer}"
        s1, s2, s3, mix_gains = saved[layer]
        after_ffn2 = (s2[7], mix_gains[3], 1.0)
        after_mix = (s1[4], gain(ffn1_norm_post, layer), FFN_RESIDUAL_WEIGHT)
        after_ffn1 = ((saved[layer - 1][2][4], gain(ffn2_norm_post, layer - 1), FFN_RESIDUAL_WEIGHT)
                      if layer > 0 else None)
        dxs, p_pre2, p_post2, head = _ffn_backward(
            f"{t}_ffn2", dxs, s3, gain(ffn2_norm_pre, layer), gain(ffn2_norm_post, layer),
            wts["gu2"][layer], wts["dn2"][layer], red, ("gu2", "dn2"), layer, flow["deps"], head, after_ffn2)
        between(dxs, (("gu2", "dn2"), layer))
        dxs, p_taps, (p_mpre, p_a, p_c, p_mpost), head = _mixer_backward(
            f"{t}_mix", dxs, s2, mix_gains, wts["win"][layer], taps[layer], wts["wout"][layer], dims,
            red, ("win", "wout"), layer, flow["deps"], head, after_mix)
        between(dxs, (("win", "wout"), layer))
        def pack_small(p_pre1, p_post1):
            tap_rows = jnp.zeros((CONV_WIDTH, SUBLANES, d), F32).at[:, 0, :cd].set(p_taps[:CONV_WIDTH])
            rows = [p_pre1, p_post1, p_mpre, jnp.concatenate([p_a, p_c], axis=1), p_mpost, p_pre2, p_post2]
            rows = jnp.concatenate([jnp.stack(rows), tap_rows], axis=0)
            small[layer] = jnp.pad(rows, ((0, SMALL_ROWS - rows.shape[0]), (0, 0), (0, 0)))

        def reduce_small(p_pre1, p_post1):
            pack_small(p_pre1, p_post1)
            flow["small"] = _allreduce_small(jnp.concatenate(small, axis=0))
            return flow["small"]

        dxs, p_pre1, p_post1, head = _ffn_backward(
            f"{t}_ffn1", dxs, s1, gain(ffn1_norm_pre, layer), gain(ffn1_norm_post, layer),
            wts["gu1"][layer], wts["dn1"][layer], red, ("gu1", "dn1"), layer, flow["deps"], head, after_ffn1,
            last=reduce_small if layer == 0 else None)
        if layer > 0:
            pack_small(p_pre1, p_post1)
        between(dxs, (("dn1",) if layer == 0 else ("gu1", "dn1"), layer))
    grad_x = dxs[None]

    weights = dict(ffn1_norm_pre=ffn1_norm_pre, ffn1_w_gate_up=ffn1_w_gate_up, ffn1_w_down=ffn1_w_down, ffn1_norm_post=ffn1_norm_post, mix_norm_pre=mix_norm_pre, w_in=w_in, conv_w=conv_w, attn_out_norm=attn_out_norm, conv_out_norm=conv_out_norm, w_out=w_out, mix_norm_post=mix_norm_post, ffn2_norm_pre=ffn2_norm_pre, ffn2_w_gate_up=ffn2_w_gate_up, ffn2_w_down=ffn2_w_down, ffn2_norm_post=ffn2_norm_post)
    m_in = dict(ffn1_norm_pre=m_ffn1_norm_pre, ffn1_w_gate_up=m_ffn1_w_gate_up, ffn1_w_down=m_ffn1_w_down, ffn1_norm_post=m_ffn1_norm_post, mix_norm_pre=m_mix_norm_pre, w_in=m_w_in, conv_w=m_conv_w, attn_out_norm=m_attn_out_norm, conv_out_norm=m_conv_out_norm, w_out=m_w_out, mix_norm_post=m_mix_norm_post, ffn2_norm_pre=m_ffn2_norm_pre, ffn2_w_gate_up=m_ffn2_w_gate_up, ffn2_w_down=m_ffn2_w_down, ffn2_norm_post=m_ffn2_norm_post)
    v_in = dict(ffn1_norm_pre=v_ffn1_norm_pre, ffn1_w_gate_up=v_ffn1_w_gate_up, ffn1_w_down=v_ffn1_w_down, ffn1_norm_post=v_ffn1_norm_post, mix_norm_pre=v_mix_norm_pre, w_in=v_w_in, conv_w=v_conv_w, attn_out_norm=v_attn_out_norm, conv_out_norm=v_conv_out_norm, w_out=v_w_out, mix_norm_post=v_mix_norm_post, ffn2_norm_pre=v_ffn2_norm_pre, ffn2_w_gate_up=v_ffn2_w_gate_up, ffn2_w_down=v_ffn2_w_down, ffn2_norm_post=v_ffn2_norm_post)
    kind_name = dict(gu1="ffn1_w_gate_up", dn1="ffn1_w_down", win="w_in", wout="w_out", gu2="ffn2_w_gate_up", dn2="ffn2_w_down")
    delta, new_m, new_v, grad = {}, {}, {}, {}

    def join_and_update(name, kind_list, deps, after):
        ts, send_sems, recv_sems = _join_start(name, [red.bufs[k] for k in kind_list], deps)
        for a, k in enumerate(kind_list):
            n = kind_name[k]
            g = _join_wait(f"join_wait_{k}", ts[a], a, send_sems, recv_sems, after)
            delta[n], new_m[n], new_v[n], grad[n] = _adamw(f"adamw_{n}", weights[n], g, m_in[n], v_in[n], True)
            after = delta[n]

    early = ("wout", "win", "dn2", "gu2")
    last_groups = ((("gu1",), 0), flow["in_flight"])
    join_and_update("join_early", early, tuple(red.scatter_tokens[g] for g in last_groups), dxs)
    done_early = [delta[kind_name[e]] for e in early]
    for group in last_groups:
        red.finish(*group, done_early)
    join_and_update("join_late", ("dn1", "gu1"), (), done_early[-1])

    small_sum = flow["small"].reshape(n_layers, SMALL_ROWS, d)
    g_ffn1_pre, g_ffn1_post, g_mix_pre = small_sum[:, 0], small_sum[:, 1], small_sum[:, 2]
    g_attn_out, g_conv_out = small_sum[:, 3, :qd], small_sum[:, 3, qd:qd + cd]
    g_mix_post, g_ffn2_pre, g_ffn2_post = small_sum[:, 4], small_sum[:, 5], small_sum[:, 6]
    cc = conv_w.shape[2]
    g_conv = lax.dynamic_slice_in_dim(small_sum[:, 7:7 + CONV_WIDTH, :cd], chip * cc, cc, axis=2)

    grad.update(ffn1_norm_pre=g_ffn1_pre, ffn1_norm_post=g_ffn1_post, mix_norm_pre=g_mix_pre, conv_w=g_conv, attn_out_norm=g_attn_out, conv_out_norm=g_conv_out, mix_norm_post=g_mix_post, ffn2_norm_pre=g_ffn2_pre, ffn2_norm_post=g_ffn2_post)
    names = list(weights)

    vectors = [n for n in names if n not in kind_name.values()]

    def pack(tree):
        flat = jnp.concatenate([tree[n].reshape(-1) for n in vectors])
        return jnp.pad(flat, (0, -flat.size % (SUBLANES * LANES))).reshape(-1, LANES)

    packed = _adamw("adamw_small", pack(weights), pack(grad), pack(m_in), pack(v_in))
    offset = 0
    for n in vectors:
        size = weights[n].size
        for tree, flat in zip((delta, new_m, new_v), packed):
            tree[n] = flat.reshape(-1)[offset:offset + size].reshape(weights[n].shape)
        offset += size

    return (loss, grad_x, *[grad[n] for n in names], *[delta[n] for n in names],
            *[new_m[n] for n in names], *[new_v[n] for n in names])
```

```python
import functools

import jax
import jax.numpy as jnp
from jax import lax
from jax.experimental import pallas as pl
from jax.experimental.pallas import tpu as pltpu

F32 = jnp.float32
BF16 = jnp.bfloat16
MESH = pl.DeviceIdType.MESH

NORM_EPS = 1e-6
HEAD_DIM = 128
Q_PER_KV = 4
CONV_WIDTH = 3
FFN_RESIDUAL_WEIGHT = 0.5
DILATED_BRANCHES = ((128, 1), (512, 4), (2048, 16))
ADAM_LR = 0.001
ADAM_B1 = 0.9
ADAM_B2 = 0.999
ADAM_EPS = 1e-08
ADAM_WD = 0.01
ADAM_STEP = 10

N_CHIPS = 4
N_DEV = 8
V7X_VMEM_BYTES = 64 << 20
VMEM_LIMIT = V7X_VMEM_BYTES - (12 << 20)
SUBLANES = 8
LANES = 128
SMALL_ROWS = 16
BIG_BLOCK = 4 << 20


def _params(*sem):
    return pltpu.CompilerParams(dimension_semantics=sem, vmem_limit_bytes=VMEM_LIMIT)


def _row_tile(rows, cols, itemsize=4, budget=2 << 20):
    t = rows
    while t * cols * itemsize > budget and t % 32 == 0:
        t //= 2
    return t


def _sum_to_sublanes(v):
    r, n = v.shape
    return v.reshape(r // SUBLANES, SUBLANES, n).sum(axis=0)


_DIMS = {
    "nn": (((1,), (0,)), ((), ())),
    "nt": (((1,), (1,)), ((), ())),
    "tn": (((0,), (0,)), ((), ())),
}


ANY_SPEC = pl.BlockSpec(memory_space=pl.ANY)


def _dot(a, b, mode):
    return lax.dot_general(a, b, _DIMS[mode], preferred_element_type=F32)


def _mm(name, a, b, *, mode, grid, a_spec, b_spec, o_spec, out_shape, nk=1, acc_shape=None, deps=()):
    nd = len(deps)

    def body(a_ref, b_ref, *rest):
        o_ref, scratch = rest[nd], rest[nd + 1:]
        r = _dot(a_ref[...], b_ref[...], mode)
        if nk == 1:
            o_ref[...] = r.astype(o_ref.dtype)
        else:
            acc = scratch[0]
            k = pl.program_id(len(grid) - 1)

            @pl.when(k == 0)
            def _():
                acc[...] = r

            @pl.when(k > 0)
            def _():
                acc[...] += r

            @pl.when(k == nk - 1)
            def _():
                o_ref[...] = acc[...].astype(o_ref.dtype)

    sem = ("parallel",) * (len(grid) - (1 if nk > 1 else 0)) + (("arbitrary",) if nk > 1 else ())
    return pl.pallas_call(
        body, name=name, grid=grid, in_specs=[a_spec, b_spec] + [ANY_SPEC] * nd, out_specs=o_spec,
        out_shape=out_shape, scratch_shapes=[pltpu.VMEM(acc_shape, F32)] if nk > 1 else [],
        compiler_params=_params(*sem),
    )(a, b, *deps)


def _tile(n, want):
    if n <= want:
        return n
    best = None
    for t in range(LANES, want + 1, LANES):
        if n % t == 0:
            best = t
    assert best is not None, (n, want)
    return best


def _norm_fwd(name, x, gain):
    s, d = x.shape
    tr = _row_tile(s, d, budget=BIG_BLOCK)

    def body(x_ref, g_ref, o_ref):
        xv = x_ref[...]
        r = lax.rsqrt(jnp.mean(xv * xv, axis=-1, keepdims=True) + NORM_EPS)
        o_ref[...] = (xv * r * g_ref[...]).astype(o_ref.dtype)

    return pl.pallas_call(
        body, name=name, grid=(s // tr,),
        in_specs=[pl.BlockSpec((tr, d), lambda i: (i, 0)), pl.BlockSpec((1, d), lambda i: (0, 0))],
        out_specs=pl.BlockSpec((tr, d), lambda i: (i, 0)),
        out_shape=jax.ShapeDtypeStruct((s, d), BF16), compiler_params=_params("parallel"),
    )(x, gain)


def _res_norm(name, x, y, gain, scale, next_gain=None):
    s, d = x.shape
    tr = _row_tile(s, d, budget=BIG_BLOCK)
    with_next = next_gain is not None

    def body(x_ref, y_ref, g_ref, *rest):
        yv = y_ref[...]
        r = lax.rsqrt(jnp.mean(yv * yv, axis=-1, keepdims=True) + NORM_EPS)
        xn = x_ref[...] + scale * (yv * r * g_ref[...])
        if with_next:
            ng_ref, o_ref, h_ref = rest
            rn = lax.rsqrt(jnp.mean(xn * xn, axis=-1, keepdims=True) + NORM_EPS)
            h_ref[...] = (xn * rn * ng_ref[...]).astype(h_ref.dtype)
        else:
            o_ref, = rest
        o_ref[...] = xn

    row = pl.BlockSpec((tr, d), lambda i: (i, 0))
    vec = pl.BlockSpec((1, d), lambda i: (0, 0))
    outs = pl.pallas_call(
        body, name=name, grid=(s // tr,),
        in_specs=[row, row, vec] + ([vec] if with_next else []), out_specs=[row] * (2 if with_next else 1),
        out_shape=[jax.ShapeDtypeStruct((s, d), F32)] + ([jax.ShapeDtypeStruct((s, d), BF16)] if with_next else []),
        compiler_params=_params("parallel"),
    )(x, y, gain, *((next_gain,) if with_next else ()))
    return (outs[0], outs[1]) if with_next else (outs[0], None)


def _rms_bwd(dn, yv, gv):
    r = lax.rsqrt(jnp.mean(yv * yv, axis=-1, keepdims=True) + NORM_EPS)
    xhat = yv * r
    dxn = dn * gv
    return r * (dxn - xhat * jnp.mean(dxn * xhat, axis=-1, keepdims=True)), _sum_to_sublanes(dn * xhat)


def _accumulate(ref, part):
    @pl.when(pl.program_id(0) == 0)
    def _():
        ref[...] = part

    @pl.when(pl.program_id(0) > 0)
    def _():
        ref[...] += part


def _norm_bwd(name, dout, yin, gain, scale, resid, out_dtype, following=None):
    s, d = yin.shape
    tr = _row_tile(s, d)
    has_resid = resid is not None
    chained = following is not None

    def body(*refs):
        refs = list(refs)
        do_ref, y_ref, g_ref = refs[:3]
        del refs[:3]
        r_ref = refs.pop(0) if has_resid else None
        if chained:
            y2_ref, g2_ref = refs[:2]
            del refs[:2]
        di_ref, dg_ref = refs[:2]
        din, part = _rms_bwd(scale * do_ref[...], y_ref[...], g_ref[...])
        _accumulate(dg_ref, part)
        if has_resid:
            din = din + r_ref[...]
        di_ref[...] = din.astype(di_ref.dtype)
        if chained:
            d2_ref, dg2_ref = refs[2:]
            d2, part2 = _rms_bwd(following[2] * din, y2_ref[...], g2_ref[...])
            _accumulate(dg2_ref, part2)
            d2_ref[...] = d2.astype(d2_ref.dtype)

    row = pl.BlockSpec((tr, d), lambda i: (i, 0))
    vec = pl.BlockSpec((1, d), lambda i: (0, 0))
    acc = pl.BlockSpec((SUBLANES, d), lambda i: (0, 0))
    ins = [row, row, vec] + ([row] if has_resid else []) + ([row, vec] if chained else [])
    args = (dout, yin, gain) + ((resid,) if has_resid else ()) + (tuple(following[:2]) if chained else ())
    outs = [row, acc] + ([row, acc] if chained else [])
    shapes = [jax.ShapeDtypeStruct((s, d), out_dtype), jax.ShapeDtypeStruct((SUBLANES, d), F32)]
    if chained:
        shapes += [jax.ShapeDtypeStruct((s, d), BF16), jax.ShapeDtypeStruct((SUBLANES, d), F32)]
    return pl.pallas_call(
        body, name=name, grid=(s // tr,), in_specs=ins, out_specs=outs, out_shape=shapes,
        compiler_params=_params("arbitrary"),
    )(*args)


def _loss_head(name, y, target):
    s, d = y.shape
    tr = _row_tile(s, d)

    def body(y_ref, t_ref, dy_ref, l_ref):
        e = y_ref[...] - t_ref[...]
        dy_ref[...] = e * (1.0 / d)
        part = _sum_to_sublanes(e * e) * (0.5 / d)

        @pl.when(pl.program_id(0) == 0)
        def _():
            l_ref[...] = part

        @pl.when(pl.program_id(0) > 0)
        def _():
            l_ref[...] += part

    row = pl.BlockSpec((tr, d), lambda i: (i, 0))
    return pl.pallas_call(
        body, name=name, grid=(s // tr,), in_specs=[row, row],
        out_specs=[row, pl.BlockSpec((SUBLANES, d), lambda i: (0, 0))],
        out_shape=[jax.ShapeDtypeStruct((s, d), F32), jax.ShapeDtypeStruct((SUBLANES, d), F32)],
        compiler_params=_params("arbitrary"),
    )(y, target)


def _ffn_up(name, h, gu_w):
    s, d = h.shape
    nb, _, fs = gu_w.shape
    hb = nb // 2
    w = gu_w.reshape(2, hb, d, fs)
    tm = _tile(s, 512)
    tn = _tile(fs, 1408)
    nj = fs // tn

    def body(h_ref, w_ref, gu_ref, a_ref):
        hv = h_ref[...]
        g = _dot(hv, w_ref[0], "nn")
        u = _dot(hv, w_ref[1], "nn")
        sg = jax.nn.sigmoid(g)
        silu = g * sg
        gu_ref[0] = (u * (sg * (1.0 + g * (1.0 - sg)))).astype(gu_ref.dtype)
        gu_ref[1] = silu.astype(gu_ref.dtype)
        a_ref[...] = (silu * u).astype(a_ref.dtype)

    return pl.pallas_call(
        body, name=name, grid=(hb, nj, s // tm),
        in_specs=[pl.BlockSpec((tm, d), lambda jb, jo, i: (i, 0)),
                  pl.BlockSpec((2, None, d, tn), lambda jb, jo, i: (0, jb, 0, jo))],
        out_specs=[pl.BlockSpec((2, None, tm, tn), lambda jb, jo, i: (0, jb, i, jo)),
                   pl.BlockSpec((tm, tn), lambda jb, jo, i: (i, jb * nj + jo))],
        out_shape=[jax.ShapeDtypeStruct((2, hb, s, fs), BF16), jax.ShapeDtypeStruct((s, hb * fs), BF16)],
        compiler_params=_params("parallel", "parallel", "parallel"),
    )(h, w)


def _ffn_dact(name, dy, dn_w, gu, deps=()):
    s, d = dy.shape
    _, hb, _, fs = gu.shape
    tm = _tile(s, 512)
    tn = _tile(fs, 1408)
    nj = fs // tn

    def body(dy_ref, w_ref, gu_ref, *rest):
        o_ref = rest[-1]
        wv = w_ref[...]
        parts = 2 if tm % (2 * SUBLANES * 2) == 0 else 1
        for r in range(parts):
            rows = slice(r * (tm // parts), (r + 1) * (tm // parts))
            da = _dot(dy_ref[rows, :], wv, "nt")
            o_ref[0, rows, :] = (da * gu_ref[0, rows, :].astype(F32)).astype(o_ref.dtype)
            o_ref[1, rows, :] = (da * gu_ref[1, rows, :].astype(F32)).astype(o_ref.dtype)

    blk = pl.BlockSpec((2, None, tm, tn), lambda jb, jo, i: (0, jb, i, jo))
    return pl.pallas_call(
        body, name=name, grid=(hb, nj, s // tm),
        in_specs=[pl.BlockSpec((tm, d), lambda jb, jo, i: (i, 0)),
                  pl.BlockSpec((tn, d), lambda jb, jo, i: (jb * nj + jo, 0)),
                  blk] + [ANY_SPEC] * len(deps),
        out_specs=blk, out_shape=jax.ShapeDtypeStruct(gu.shape, BF16),
        compiler_params=_params("parallel", "parallel", "parallel"),
    )(dy, dn_w, gu, *deps)


_MASKED = -1e30


def _attn_bias(s, tq):
    nd = s // tq
    dist = (jnp.arange(nd)[:, None, None] * tq + jnp.arange(tq)[None, :, None]) - jnp.arange(tq)[None, None, :]
    mult = jnp.zeros(dist.shape, F32)
    for window, dilation in DILATED_BRANCHES:
        mult = mult + ((dist >= 0) & (dist <= window) & (dist % dilation == 0)).astype(F32)
    return jnp.where(mult > 0.0, jnp.log(jnp.maximum(mult, 1.0)), _MASKED)


def _biased(sc, bias, scale):
    tq, tk = bias.shape
    return (sc.reshape(-1, tq, tk) * scale + bias[None]).reshape(sc.shape)


def _attn_specs(s, qd, kvd, tq):
    rw = Q_PER_KV * HEAD_DIM
    qspec = pl.BlockSpec((tq, rw), lambda g, i: (i, g))
    kspec = pl.BlockSpec((s, HEAD_DIM), lambda g, i: (0, qd // HEAD_DIM + g))
    vspec = pl.BlockSpec((s, HEAD_DIM), lambda g, i: (0, (qd + kvd) // HEAD_DIM + g))
    return rw, qspec, kspec, vspec


def _attn_fwd(name, z, qd, kvd):
    s = z.shape[0]
    tq = _tile(s, 256)
    nkv = kvd // HEAD_DIM
    rw, qspec, kspec, vspec = _attn_specs(s, qd, kvd, tq)
    scale = HEAD_DIM ** -0.5

    def body(q_ref, k_ref, v_ref, b_ref, o_ref, l_ref):
        i = pl.program_id(1)
        heads = [slice(h * HEAD_DIM, (h + 1) * HEAD_DIM) for h in range(Q_PER_KV)]
        q_all = jnp.concatenate([q_ref[:, cols] for cols in heads], axis=0)

        def chunk(j, carry):
            mx, den, acc = carry
            k0 = pl.multiple_of(j * tq, tq)
            kc, vc = k_ref[pl.ds(k0, tq), :], v_ref[pl.ds(k0, tq), :]
            sc = _biased(_dot(q_all, kc, "nt"), b_ref[i - j], scale)
            mx_new = jnp.maximum(mx, jnp.max(sc, axis=-1, keepdims=True))
            alpha = jnp.exp(mx - mx_new)
            p = jnp.exp(sc - mx_new)
            return (mx_new, alpha * den + jnp.sum(p, axis=-1, keepdims=True),
                    alpha * acc + _dot(p.astype(BF16), vc, "nn"))

        rows = Q_PER_KV * tq
        init = (jnp.full((rows, 1), _MASKED, F32), jnp.zeros((rows, 1), F32), jnp.zeros((rows, HEAD_DIM), F32))
        mx, den, acc = lax.fori_loop(0, i + 1, chunk, init)
        out = acc / den
        lse = mx + jnp.log(den)
        for h, cols in enumerate(heads):
            o_ref[:, cols] = out[h * tq:(h + 1) * tq]
            l_ref[:, cols] = jnp.broadcast_to(lse[h * tq:(h + 1) * tq], (tq, HEAD_DIM))

    bias = _attn_bias(s, tq)
    return pl.pallas_call(
        body, name=name, grid=(nkv, s // tq),
        in_specs=[qspec, kspec, vspec, pl.BlockSpec(bias.shape, lambda g, i: (0, 0, 0))], out_specs=[qspec, qspec],
        out_shape=[jax.ShapeDtypeStruct((s, qd), F32), jax.ShapeDtypeStruct((s, qd), F32)],
        compiler_params=_params("parallel", "parallel"),
    )(z, z, z, bias)


def _attn_bwd(name, z, o, lse, do, qd, kvd):
    s = z.shape[0]
    tq = _tile(s, 256)
    nkv = kvd // HEAD_DIM
    nq = s // tq
    rw, qspec, kspec, vspec = _attn_specs(s, qd, kvd, tq)
    scale = HEAD_DIM ** -0.5

    def body(q_ref, k_ref, v_ref, o_ref, l_ref, do_ref, b_ref, dq_ref, dk_ref, dv_ref, dk_acc, dv_acc):
        i = pl.program_id(1)
        heads = [slice(h * HEAD_DIM, (h + 1) * HEAD_DIM) for h in range(Q_PER_KV)]

        @pl.when(i == 0)
        def _():
            dk_acc[...] = jnp.zeros_like(dk_acc)
            dv_acc[...] = jnp.zeros_like(dv_acc)

        q_all = jnp.concatenate([q_ref[:, cols] for cols in heads], axis=0)
        do_all = jnp.concatenate([do_ref[:, cols].astype(BF16) for cols in heads], axis=0)
        lse_all = jnp.concatenate([l_ref[:, cols][:, :1] for cols in heads], axis=0)
        delta_all = jnp.concatenate(
            [jnp.sum(do_ref[:, cols] * o_ref[:, cols], axis=-1, keepdims=True) for cols in heads], axis=0)

        def chunk(j, dq):
            k0 = pl.multiple_of(j * tq, tq)
            kc, vc = k_ref[pl.ds(k0, tq), :], v_ref[pl.ds(k0, tq), :]
            p = jnp.exp(_biased(_dot(q_all, kc, "nt"), b_ref[i - j], scale) - lse_all)
            ds = (p * (_dot(do_all, vc, "nt") - delta_all) * scale).astype(BF16)
            dk_acc[pl.ds(k0, tq), :] += _dot(ds, q_all, "tn")
            dv_acc[pl.ds(k0, tq), :] += _dot(p.astype(BF16), do_all, "tn")
            return dq + _dot(ds, kc, "nn")

        dq = lax.fori_loop(0, i + 1, chunk, jnp.zeros((Q_PER_KV * tq, HEAD_DIM), F32))
        for h, cols in enumerate(heads):
            dq_ref[:, cols] = dq[h * tq:(h + 1) * tq].astype(dq_ref.dtype)

        @pl.when(i == nq - 1)
        def _():
            dk_ref[...] = dk_acc[...].astype(dk_ref.dtype)
            dv_ref[...] = dv_acc[...].astype(dv_ref.dtype)

    kvout = pl.BlockSpec((s, HEAD_DIM), lambda g, i: (0, g))
    bias = _attn_bias(s, tq)
    return pl.pallas_call(
        body, name=name, grid=(nkv, nq),
        in_specs=[qspec, kspec, vspec, qspec, qspec, qspec, pl.BlockSpec(bias.shape, lambda g, i: (0, 0, 0))],
        out_specs=[qspec, kvout, kvout],
        out_shape=[jax.ShapeDtypeStruct((s, qd), BF16), jax.ShapeDtypeStruct((s, kvd), BF16),
                   jax.ShapeDtypeStruct((s, kvd), BF16)],
        scratch_shapes=[pltpu.VMEM((s, HEAD_DIM), F32), pltpu.VMEM((s, HEAD_DIM), F32)],
        compiler_params=_params("parallel", "arbitrary"),
    )(z, z, z, o, lse, do, bias)


def _shift_down(v, n):
    rolled = pltpu.roll(v, n, 0)
    t = lax.broadcasted_iota(jnp.int32, v.shape, 0)
    return jnp.where(t >= n, rolled, 0.0)


def _shift_up(v, n):
    rows = v.shape[0]
    rolled = pltpu.roll(v, rows - n, 0)
    t = lax.broadcasted_iota(jnp.int32, v.shape, 0)
    return jnp.where(t < rows - n, rolled, 0.0)


def _conv_specs(s, base, cd, tc):
    zs = [pl.BlockSpec((s, tc), functools.partial(lambda j, off: (0, off + j), off=(base + n * cd) // tc))
          for n in range(3)]
    wspec = pl.BlockSpec((SUBLANES, tc), lambda j: (0, j))
    cspec = pl.BlockSpec((s, tc), lambda j: (0, j))
    return zs, wspec, cspec


def _conv_fwd(name, z, conv_w, base, cd):
    s = z.shape[0]
    tc = _tile(cd, 256)
    zs, wspec, cspec = _conv_specs(s, base, cd, tc)

    def body(h_ref, b_ref, c_ref, w_ref, o_ref):
        u = c_ref[...].astype(F32) * h_ref[...].astype(F32)
        y = w_ref[0:1, :] * _shift_down(u, 2) + w_ref[1:2, :] * _shift_down(u, 1) + w_ref[2:3, :] * u
        o_ref[...] = b_ref[...].astype(F32) * y

    return pl.pallas_call(
        body, name=name, grid=(cd // tc,), in_specs=zs + [wspec], out_specs=cspec,
        out_shape=jax.ShapeDtypeStruct((s, cd), F32), compiler_params=_params("parallel"),
    )(z, z, z, conv_w)


def _conv_bwd(name, z, conv_w, dc, base, cd):
    s = z.shape[0]
    tc = _tile(cd, 256)
    zs, wspec, cspec = _conv_specs(s, base, cd, tc)

    def body(h_ref, b_ref, c_ref, w_ref, dc_ref, dh_ref, db_ref, dcg_ref, dw_ref):
        hv, bv, cv = h_ref[...].astype(F32), b_ref[...].astype(F32), c_ref[...].astype(F32)
        u = cv * hv
        u1, u2 = _shift_down(u, 1), _shift_down(u, 2)
        w0, w1, w2 = w_ref[0:1, :], w_ref[1:2, :], w_ref[2:3, :]
        y = w0 * u2 + w1 * u1 + w2 * u
        dcv = dc_ref[...]
        db_ref[...] = (dcv * y).astype(db_ref.dtype)
        dy = dcv * bv
        du = w2 * dy + w1 * _shift_up(dy, 1) + w0 * _shift_up(dy, 2)
        dh_ref[...] = (du * cv).astype(dh_ref.dtype)
        dcg_ref[...] = (du * hv).astype(dcg_ref.dtype)
        g0 = jnp.sum(dy * u2, axis=0, keepdims=True)
        g1 = jnp.sum(dy * u1, axis=0, keepdims=True)
        g2 = jnp.sum(dy * u, axis=0, keepdims=True)
        r = lax.broadcasted_iota(jnp.int32, (SUBLANES, tc), 0)
        dw_ref[...] = jnp.where(r == 0, g0, jnp.where(r == 1, g1, jnp.where(r == 2, g2, 0.0)))

    return pl.pallas_call(
        body, name=name, grid=(cd // tc,), in_specs=zs + [wspec, cspec],
        out_specs=[cspec, cspec, cspec, wspec],
        out_shape=[jax.ShapeDtypeStruct((s, cd), BF16)] * 3 + [jax.ShapeDtypeStruct((SUBLANES, cd), F32)],
        compiler_params=_params("parallel"),
    )(z, z, z, conv_w, dc)


def _cat_norm_fwd(name, a, c, ga, gc):
    s, qd = a.shape
    cd = c.shape[1]
    tr = _row_tile(s, qd + cd)

    def body(a_ref, c_ref, ga_ref, gc_ref, o_ref):
        av, cv = a_ref[...], c_ref[...]
        ra = lax.rsqrt(jnp.mean(av * av, axis=-1, keepdims=True) + NORM_EPS)
        rc = lax.rsqrt(jnp.mean(cv * cv, axis=-1, keepdims=True) + NORM_EPS)
        o_ref[:, :qd] = (av * ra * ga_ref[...]).astype(o_ref.dtype)
        o_ref[:, qd:] = (cv * rc * gc_ref[...]).astype(o_ref.dtype)

    return pl.pallas_call(
        body, name=name, grid=(s // tr,),
        in_specs=[pl.BlockSpec((tr, qd), lambda i: (i, 0)), pl.BlockSpec((tr, cd), lambda i: (i, 0)),
                  pl.BlockSpec((1, qd), lambda i: (0, 0)), pl.BlockSpec((1, cd), lambda i: (0, 0))],
        out_specs=pl.BlockSpec((tr, qd + cd), lambda i: (i, 0)),
        out_shape=jax.ShapeDtypeStruct((s, qd + cd), BF16), compiler_params=_params("parallel"),
    )(a, c, ga, gc)


def _cat_norm_bwd(name, dcat, a, c, ga, gc):
    s, qd = a.shape
    cd = c.shape[1]
    tr = _row_tile(s, qd + cd)

    def one(dn, yv, gv):
        r = lax.rsqrt(jnp.mean(yv * yv, axis=-1, keepdims=True) + NORM_EPS)
        xhat = yv * r
        dxn = dn * gv
        return r * (dxn - xhat * jnp.mean(dxn * xhat, axis=-1, keepdims=True)), _sum_to_sublanes(dn * xhat)

    def body(d_ref, a_ref, c_ref, ga_ref, gc_ref, da_ref, dc_ref, dga_ref, dgc_ref):
        da, pa = one(d_ref[:, :qd], a_ref[...], ga_ref[...])
        dc, pc = one(d_ref[:, qd:], c_ref[...], gc_ref[...])
        da_ref[...] = da
        dc_ref[...] = dc

        @pl.when(pl.program_id(0) == 0)
        def _():
            dga_ref[...] = pa
            dgc_ref[...] = pc

        @pl.when(pl.program_id(0) > 0)
        def _():
            dga_ref[...] += pa
            dgc_ref[...] += pc

    ra = pl.BlockSpec((tr, qd), lambda i: (i, 0))
    rc = pl.BlockSpec((tr, cd), lambda i: (i, 0))
    return pl.pallas_call(
        body, name=name, grid=(s // tr,),
        in_specs=[pl.BlockSpec((tr, qd + cd), lambda i: (i, 0)), ra, rc,
                  pl.BlockSpec((1, qd), lambda i: (0, 0)), pl.BlockSpec((1, cd), lambda i: (0, 0))],
        out_specs=[ra, rc, pl.BlockSpec((SUBLANES, qd), lambda i: (0, 0)),
                   pl.BlockSpec((SUBLANES, cd), lambda i: (0, 0))],
        out_shape=[jax.ShapeDtypeStruct((s, qd), F32), jax.ShapeDtypeStruct((s, cd), F32),
                   jax.ShapeDtypeStruct((SUBLANES, qd), F32), jax.ShapeDtypeStruct((SUBLANES, cd), F32)],
        compiler_params=_params("arbitrary"),
    )(dcat, a, c, ga, gc)


def _adamw(name, w, g, m, v, emit_grad=False, layer=None, prev=None):
    shape = w.shape
    cols = shape[-1]
    rows = g.size // cols
    tr = _row_tile(rows, cols, budget=3 << 19)
    first = 0 if layer is None else layer * (rows // tr)
    bc1 = 1.0 - ADAM_B1 ** ADAM_STEP
    bc2 = 1.0 - ADAM_B2 ** ADAM_STEP
    n_out = 4 if emit_grad else 3

    def body(w_ref, g_ref, m_ref, v_ref, *rest):
        d_ref, nm_ref, nv_ref = rest[-n_out:][:3]
        gv = g_ref[...]
        mv = ADAM_B1 * m_ref[...] + (1.0 - ADAM_B1) * gv
        vv = ADAM_B2 * v_ref[...] + (1.0 - ADAM_B2) * (gv * gv)
        nm_ref[...] = mv
        nv_ref[...] = vv
        d_ref[...] = -ADAM_LR * ((mv / bc1) / (jnp.sqrt(vv / bc2) + ADAM_EPS) + ADAM_WD * w_ref[...])
        if emit_grad:
            rest[-1][...] = gv

    row = pl.BlockSpec((tr, cols), lambda i: (first + i, 0))
    g_row = pl.BlockSpec((tr, cols), lambda i: (i, 0))
    prev = tuple(prev) if prev is not None else ()
    total = w.size // cols
    outs = pl.pallas_call(
        body, name=name, grid=(rows // tr,), in_specs=[row, g_row, row, row] + [ANY_SPEC] * len(prev),
        out_specs=[row] * n_out, out_shape=[jax.ShapeDtypeStruct((total, cols), F32)] * n_out,
        input_output_aliases={4 + i: i for i in range(len(prev))}, compiler_params=_params("parallel"),
    )(w.reshape(total, cols), g.reshape(rows, cols), m.reshape(total, cols), v.reshape(total, cols),
      *(t.reshape(total, cols) for t in prev))
    return tuple(t.reshape(shape) for t in outs)


HBM_SPEC = pl.BlockSpec(memory_space=pltpu.HBM)


def _mesh_place():
    x, y, c = lax.axis_index("x"), lax.axis_index("y"), lax.axis_index("c")
    other_chips = [(1 - x, y), (x, 1 - y), (1 - x, 1 - y)]
    return x, y, c, other_chips


def _cast_into_slot(name, w, layer, chip, deps=()):
    _, r, cols = w.shape
    tr = _row_tile(r, cols, budget=BIG_BLOCK)

    def body(chip_ref, w_ref, *rest):
        o_ref = rest[-1]
        o_ref[...] = w_ref[...].astype(o_ref.dtype)

    return pl.pallas_call(
        body, name=name,
        grid_spec=pltpu.PrefetchScalarGridSpec(
            num_scalar_prefetch=1, grid=(r // tr,),
            in_specs=[pl.BlockSpec((None, tr, cols), lambda i, chip_ref: (layer, i, 0))] + [ANY_SPEC] * len(deps),
            out_specs=pl.BlockSpec((None, tr, cols), lambda i, chip_ref: (chip_ref[0], i, 0))),
        out_shape=jax.ShapeDtypeStruct((N_CHIPS, r, cols), BF16), compiler_params=_params("parallel"),
    )(chip, w, *deps)


SEM_SPEC = pl.BlockSpec(memory_space=pltpu.SEMAPHORE)
SPLIT_COPY = pltpu.CompilerParams(has_side_effects=pltpu.SideEffectType.DATAFLOW_SIDE_EFFECTING)
N_OTHER = N_CHIPS - 1
TOKEN_SPEC = pl.BlockSpec(memory_space=pltpu.VMEM)
TOKEN_SHAPE = jax.ShapeDtypeStruct((SUBLANES, LANES), F32)


def _in_hbm(arr):
    return pltpu.with_memory_space_constraint(arr, pltpu.HBM)


def _half_rows(ref, chip_idx, core):
    r2 = ref.shape[1] // 2
    return ref.at[chip_idx, pl.ds(core * r2, r2), :]


def _gather_start(name, fulls, after):
    na = len(fulls)

    def body(*refs):
        f_refs = refs[na + 1:2 * na + 1]
        send_sems, recv_sems = refs[2 * na + 1:3 * na + 1], refs[3 * na + 1:4 * na + 1]
        token = refs[4 * na + 1]
        x, y, c, chips = _mesh_place()
        for a in range(na):
            mine = _half_rows(f_refs[a], 2 * x + y, c)
            for j, (cx, cy) in enumerate(chips):
                pltpu.make_async_remote_copy(
                    src_ref=mine, dst_ref=mine, send_sem=send_sems[a].at[j], recv_sem=recv_sems[a].at[j],
                    device_id=(cx, cy, c), device_id_type=MESH).start()
        token[...] = jnp.zeros_like(token)

    outs = pl.pallas_call(
        body, name=name, in_specs=[HBM_SPEC] * na + [ANY_SPEC],
        out_specs=[HBM_SPEC] * na + [SEM_SPEC] * (2 * na) + [TOKEN_SPEC],
        out_shape=[pltpu.HBM(f.shape, f.dtype) for f in fulls] + [pltpu.SemaphoreType.DMA((N_OTHER,))] * (2 * na)
        + [TOKEN_SHAPE],
        input_output_aliases={a: a for a in range(na)}, compiler_params=SPLIT_COPY,
    )(*[_in_hbm(f) for f in fulls], after)
    return list(outs[:na]), list(outs[na:2 * na]), list(outs[2 * na:3 * na]), outs[3 * na]


def _gather_pass_on(name, full, recv_sems, after):
    def body(f_in, recv_sems, after_ref, f_ref, d2d_send, d2d_recv):
        x, y, c, chips = _mesh_place()
        for j, (cx, cy) in enumerate(chips):
            blk = _half_rows(f_ref, 2 * cx + cy, c)
            pltpu.make_async_remote_copy(
                src_ref=blk, dst_ref=blk, send_sem=d2d_send.at[j], recv_sem=recv_sems.at[j],
                device_id=(cx, cy, c), device_id_type=MESH).wait_recv()
            pltpu.make_async_remote_copy(
                src_ref=blk, dst_ref=blk, send_sem=d2d_send.at[j], recv_sem=d2d_recv.at[j],
                device_id=(x, y, 1 - c), device_id_type=MESH).start()

    return pl.pallas_call(
        body, name=name, in_specs=[HBM_SPEC, SEM_SPEC, ANY_SPEC], out_specs=[HBM_SPEC, SEM_SPEC, SEM_SPEC],
        out_shape=[pltpu.HBM(full.shape, full.dtype)] + [pltpu.SemaphoreType.DMA((N_OTHER,))] * 2,
        input_output_aliases={0: 0}, compiler_params=SPLIT_COPY,
    )(full, recv_sems, after)


def _gather_arrive(name, full, ici_send, d2d_send, d2d_recv, after):
    def body(f_in, ici_send, d2d_send, d2d_recv, after_ref, f_ref):
        x, y, c, chips = _mesh_place()
        for j, (cx, cy) in enumerate(chips):
            mine = _half_rows(f_ref, 2 * x + y, c)
            passed = _half_rows(f_ref, 2 * cx + cy, c)
            theirs = _half_rows(f_ref, 2 * cx + cy, 1 - c)
            pltpu.make_async_remote_copy(
                src_ref=mine, dst_ref=mine, send_sem=ici_send.at[j], recv_sem=d2d_recv.at[j],
                device_id=(cx, cy, c), device_id_type=MESH).wait_send()
            pltpu.make_async_remote_copy(
                src_ref=passed, dst_ref=passed, send_sem=d2d_send.at[j], recv_sem=d2d_recv.at[j],
                device_id=(x, y, 1 - c), device_id_type=MESH).wait_send()
            pltpu.make_async_remote_copy(
                src_ref=theirs, dst_ref=theirs, send_sem=d2d_send.at[j], recv_sem=d2d_recv.at[j],
                device_id=(x, y, 1 - c), device_id_type=MESH).wait_recv()

    return pl.pallas_call(
        body, name=name, in_specs=[HBM_SPEC, SEM_SPEC, SEM_SPEC, SEM_SPEC, ANY_SPEC], out_specs=HBM_SPEC,
        out_shape=pltpu.HBM(full.shape, full.dtype), input_output_aliases={0: 0}, compiler_params=SPLIT_COPY,
    )(full, ici_send, d2d_send, d2d_recv, after)


def _gather_taps(conv_w):
    def body(cw_ref, cwf_ref, send_sems, recv_sems, local_sem):
        x, y, c, chips = _mesh_place()
        k_me = 2 * x + y
        local = pltpu.make_async_copy(cw_ref, cwf_ref.at[k_me], local_sem)
        local.start()
        copies = [pltpu.make_async_remote_copy(
            src_ref=cw_ref, dst_ref=cwf_ref.at[k_me], send_sem=send_sems.at[j], recv_sem=recv_sems.at[j],
            device_id=(cx, cy, c), device_id_type=MESH) for j, (cx, cy) in enumerate(chips)]
        for cp in copies:
            cp.start()
        for j, (cx, cy) in enumerate(chips):
            pltpu.make_async_remote_copy(
                src_ref=cw_ref, dst_ref=cwf_ref.at[2 * cx + cy], send_sem=send_sems.at[j], recv_sem=recv_sems.at[j],
                device_id=(cx, cy, c), device_id_type=MESH).wait_recv()
        for cp in copies:
            cp.wait_send()
        local.wait()

    return pl.pallas_call(
        body, name="gather_taps", in_specs=[HBM_SPEC], out_specs=HBM_SPEC,
        out_shape=jax.ShapeDtypeStruct((N_CHIPS,) + conv_w.shape, conv_w.dtype),
        scratch_shapes=[pltpu.SemaphoreType.DMA((N_OTHER,))] * 2 + [pltpu.SemaphoreType.DMA],
    )(conv_w)


def _sibling_half(g_ref, c):
    r2 = g_ref.shape[1] // 2
    return g_ref.at[:, pl.ds((1 - c) * r2, r2), :]


def _swap_copy(g_ref, land_ref, send_sems, recv_sems, a):
    x, y, c, _ = _mesh_place()
    return pltpu.make_async_remote_copy(
        src_ref=_sibling_half(g_ref, c), dst_ref=land_ref, send_sem=send_sems.at[a], recv_sem=recv_sems.at[a],
        device_id=(x, y, 1 - c), device_id_type=MESH)


def _swap_start(name, gs):
    n = len(gs)

    def body(*refs):
        g_refs, land_refs = refs[n:2 * n], refs[2 * n:3 * n]
        send_sems, recv_sems, token = refs[3 * n:]
        for a in range(n):
            _swap_copy(g_refs[a], land_refs[a], send_sems, recv_sems, a).start()
        token[...] = jnp.zeros_like(token)

    outs = pl.pallas_call(
        body, name=name, in_specs=[HBM_SPEC] * n,
        out_specs=[HBM_SPEC] * (2 * n) + [SEM_SPEC, SEM_SPEC, TOKEN_SPEC],
        out_shape=[pltpu.HBM(g.shape, g.dtype) for g in gs]
        + [pltpu.HBM((g.shape[0], g.shape[1] // 2, g.shape[2]), g.dtype) for g in gs]
        + [pltpu.SemaphoreType.DMA((n,)), pltpu.SemaphoreType.DMA((n,)), TOKEN_SHAPE],
        input_output_aliases={a: a for a in range(n)}, compiler_params=SPLIT_COPY,
    )(*[_in_hbm(g) for g in gs])
    return list(outs[:n]), list(outs[n:2 * n]), outs[2 * n], outs[2 * n + 1], outs[2 * n + 2]


def _swap_wait(name, gs, lands, send_sems, recv_sems, after):
    n = len(gs)

    def body(*refs):
        send_sems, recv_sems = refs[2 * n], refs[2 * n + 1]
        g_refs, land_refs = refs[2 * n + 3:3 * n + 3], refs[3 * n + 3:]
        for a in range(n):
            copy = _swap_copy(g_refs[a], land_refs[a], send_sems, recv_sems, a)
            copy.wait_send()
            copy.wait_recv()

    outs = pl.pallas_call(
        body, name=name, in_specs=[HBM_SPEC] * (2 * n) + [SEM_SPEC, SEM_SPEC, ANY_SPEC],
        out_specs=[HBM_SPEC] * (2 * n),
        out_shape=[pltpu.HBM(t.shape, t.dtype) for t in list(gs) + list(lands)],
        input_output_aliases={a: a for a in range(2 * n)}, compiler_params=SPLIT_COPY,
    )(*gs, *lands, send_sems, recv_sems, after)
    return list(outs[:n]), list(outs[n:])


def _add_core_halves(name, g, sib, core):
    nb, r, cols = g.shape
    r2 = r // 2
    tr = _row_tile(r2, cols, itemsize=2, budget=BIG_BLOCK)
    nrt = r2 // tr

    def body(core_ref, g_ref, s_ref, o_ref):
        o_ref[...] = (g_ref[...].astype(F32) + s_ref[...].astype(F32)).astype(o_ref.dtype)

    return pl.pallas_call(
        body, name=name,
        grid_spec=pltpu.PrefetchScalarGridSpec(
            num_scalar_prefetch=1, grid=(nb, nrt),
            in_specs=[pl.BlockSpec((None, tr, cols), lambda k, i, core_ref: (k, core_ref[0] * nrt + i, 0)),
                      pl.BlockSpec((None, tr, cols), lambda k, i, core_ref: (k, i, 0))],
            out_specs=pl.BlockSpec((None, tr, cols), lambda k, i, core_ref: (k, i, 0))),
        out_shape=jax.ShapeDtypeStruct((nb, r2, cols), BF16), compiler_params=_params("parallel", "parallel"),
    )(core, g, sib)


def _scatter_copies(h_refs, land_refs, send_sems, recv_sems):
    x, y, c, chips = _mesh_place()
    return [pltpu.make_async_remote_copy(
        src_ref=h_ref.at[2 * cx + cy], dst_ref=land_ref.at[j],
        send_sem=send_sems.at[a * N_OTHER + j], recv_sem=recv_sems.at[a * N_OTHER + j],
        device_id=(cx, cy, c), device_id_type=MESH)
        for a, (h_ref, land_ref) in enumerate(zip(h_refs, land_refs)) for j, (cx, cy) in enumerate(chips)]


def _scatter_start(name, hs):
    n = len(hs)

    def body(*refs):
        h_refs, land_refs = refs[n:2 * n], refs[2 * n:3 * n]
        send_sems, recv_sems, token = refs[3 * n:]
        for copy in _scatter_copies(h_refs, land_refs, send_sems, recv_sems):
            copy.start()
        token[...] = jnp.zeros_like(token)

    outs = pl.pallas_call(
        body, name=name, in_specs=[HBM_SPEC] * n,
        out_specs=[HBM_SPEC] * (2 * n) + [SEM_SPEC, SEM_SPEC, TOKEN_SPEC],
        out_shape=[pltpu.HBM(h.shape, h.dtype) for h in hs]
        + [pltpu.HBM((N_OTHER,) + h.shape[1:], h.dtype) for h in hs]
        + [pltpu.SemaphoreType.DMA((n * N_OTHER,)), pltpu.SemaphoreType.DMA((n * N_OTHER,)), TOKEN_SHAPE],
        input_output_aliases={a: a for a in range(n)}, compiler_params=SPLIT_COPY,
    )(*[_in_hbm(h) for h in hs])
    return list(outs[:n]), list(outs[n:2 * n]), outs[2 * n], outs[2 * n + 1], outs[2 * n + 2]


def _scatter_wait(name, hs, lands, send_sems, recv_sems, after):
    afters = tuple(after) if isinstance(after, (tuple, list)) else (after,)
    n = len(hs)

    def body(*refs):
        send_sems, recv_sems = refs[2 * n], refs[2 * n + 1]
        h_refs, land_refs = refs[-2 * n:-n], refs[-n:]
        for copy in _scatter_copies(h_refs, land_refs, send_sems, recv_sems):
            copy.wait_send()
            copy.wait_recv()

    outs = pl.pallas_call(
        body, name=name, in_specs=[HBM_SPEC] * (2 * n) + [SEM_SPEC, SEM_SPEC] + [ANY_SPEC] * len(afters),
        out_specs=[HBM_SPEC] * (2 * n),
        out_shape=[pltpu.HBM(t.shape, t.dtype) for t in list(hs) + list(lands)],
        input_output_aliases={a: a for a in range(2 * n)}, compiler_params=SPLIT_COPY,
    )(*hs, *lands, send_sems, recv_sems, *afters)
    return list(outs[:n]), list(outs[n:])


def _sum_chips(name, hs, rcv, core, chip, layer, n_layers, prev):
    _, r2, cols = hs.shape
    tr = _row_tile(r2, cols, budget=BIG_BLOCK)
    nrt = r2 // tr

    def body(core_ref, chip_ref, h_ref, r_ref, *rest):
        o_ref = rest[-1]
        acc = h_ref[...].astype(F32)
        for j in range(N_CHIPS - 1):
            acc = acc + r_ref[j].astype(F32)
        o_ref[...] = acc

    in_specs = [pl.BlockSpec((None, tr, cols), lambda i, core_ref, chip_ref: (chip_ref[0], i, 0)),
                pl.BlockSpec((N_CHIPS - 1, tr, cols), lambda i, core_ref, chip_ref: (0, i, 0))]
    args = [core, chip, hs, rcv]
    aliases = {}
    if prev is not None:
        in_specs.append(pl.BlockSpec(memory_space=pl.ANY))
        args.append(prev)
        aliases = {4: 0}
    return pl.pallas_call(
        body, name=name,
        grid_spec=pltpu.PrefetchScalarGridSpec(
            num_scalar_prefetch=2, grid=(nrt,), in_specs=in_specs,
            out_specs=pl.BlockSpec((None, tr, cols), lambda i, core_ref, chip_ref: (layer, core_ref[0] * nrt + i, 0))),
        out_shape=jax.ShapeDtypeStruct((n_layers, 2 * r2, cols), F32), input_output_aliases=aliases,
        compiler_params=_params("parallel"),
    )(*args)


def _join_copy(t_ref, send_sems, recv_sems, a):
    x, y, c, _ = _mesh_place()
    r2 = t_ref.shape[1] // 2
    mine = t_ref.at[:, pl.ds(c * r2, r2), :]
    return pltpu.make_async_remote_copy(
        src_ref=mine, dst_ref=mine, send_sem=send_sems.at[a], recv_sem=recv_sems.at[a],
        device_id=(x, y, 1 - c), device_id_type=MESH)


def _join_start(name, ts, deps=()):
    n, nd = len(ts), len(deps)

    def body(*refs):
        t_refs = refs[n + nd:2 * n + nd]
        send_sems, recv_sems = refs[2 * n + nd:]
        for a in range(n):
            _join_copy(t_refs[a], send_sems, recv_sems, a).start()

    outs = pl.pallas_call(
        body, name=name, in_specs=[HBM_SPEC] * n + [ANY_SPEC] * nd, out_specs=[HBM_SPEC] * n + [SEM_SPEC, SEM_SPEC],
        out_shape=[pltpu.HBM(t.shape, t.dtype) for t in ts] + [pltpu.SemaphoreType.DMA((n,))] * 2,
        input_output_aliases={a: a for a in range(n)}, compiler_params=SPLIT_COPY,
    )(*[_in_hbm(t) for t in ts], *deps)
    return list(outs[:n]), outs[n], outs[n + 1]


def _join_wait(name, t, a, send_sems, recv_sems, after):
    def body(t_in, send_sems, recv_sems, after_ref, t_ref):
        copy = _join_copy(t_ref, send_sems, recv_sems, a)
        copy.wait_send()
        copy.wait_recv()

    return pl.pallas_call(
        body, name=name, in_specs=[HBM_SPEC, SEM_SPEC, SEM_SPEC, ANY_SPEC], out_specs=HBM_SPEC,
        out_shape=pltpu.HBM(t.shape, t.dtype), input_output_aliases={0: 0}, compiler_params=SPLIT_COPY,
    )(t, send_sems, recv_sems, after)


def _allreduce_small(p):
    n, _, w = p.shape

    def body(p_ref, o_ref, buf, send_sems, recv_sems):
        x, y, c, _ = _mesh_place()
        me = 4 * x + 2 * y + c
        buf[me] = jnp.sum(p_ref[...], axis=1)
        copies = []
        for pat in range(1, N_DEV):
            fx, fy, fc = (pat >> 2) & 1, (pat >> 1) & 1, pat & 1
            copies.append(pltpu.make_async_remote_copy(
                src_ref=buf.at[me], dst_ref=buf.at[me], send_sem=send_sems.at[pat - 1], recv_sem=recv_sems.at[pat - 1],
                device_id=(x ^ fx, y ^ fy, c ^ fc), device_id_type=MESH))
        for cp in copies:
            cp.start()
        for cp in copies:
            cp.wait()
        acc = buf[0]
        for dev in range(1, N_DEV):
            acc = acc + buf[dev]
        o_ref[...] = acc

    return pl.pallas_call(
        body, name="allreduce_small", in_specs=[pl.BlockSpec(memory_space=pltpu.VMEM)],
        out_specs=pl.BlockSpec(memory_space=pltpu.VMEM), out_shape=jax.ShapeDtypeStruct((n, w), F32),
        scratch_shapes=[pltpu.VMEM((N_DEV, n, w), F32), pltpu.SemaphoreType.DMA((N_DEV - 1,)),
                        pltpu.SemaphoreType.DMA((N_DEV - 1,))],
    )(p)


class _WeightFeed:
    def __init__(self):
        self.fulls, self.ici_send, self.ici_recv, self.d2d = [], [], [], []

    def start(self, name, fulls, after):
        started, send, recv, token = _gather_start(name, fulls, after)
        self.fulls += started
        self.ici_send += send
        self.ici_recv += recv
        self.d2d += [None] * len(fulls)
        self.token = token
        return token

    def _pass_on(self, k, after):
        if k == 0:
            after = self.token
        if k < len(self.fulls) and self.d2d[k] is None:
            self.fulls[k], send, recv = _gather_pass_on(f"gather_pass_{k}", self.fulls[k], self.ici_recv[k], after)
            self.d2d[k] = (send, recv)

    def take(self, k, after):
        self._pass_on(k, after)
        self.fulls[k] = _gather_arrive(f"gather_arrive_{k}", self.fulls[k], self.ici_send[k], *self.d2d[k], after)
        return self.fulls[k]


def _ffn_forward(tag, x, h, g_post, next_gain, feed, k):
    s, d = x.shape
    gu_w = feed.take(k, h)
    gu, a = _ffn_up(f"{tag}_up", h, gu_w)
    dn_w = feed.take(k + 1, a).reshape(-1, d)
    f = dn_w.shape[0]
    tm, tn = _tile(s, 1024), _tile(d, 512)
    y = _mm(f"{tag}_down", a, dn_w, mode="nn", grid=(s // tm, d // tn),
            a_spec=pl.BlockSpec((tm, f), lambda i, j: (i, 0)),
            b_spec=pl.BlockSpec((f, tn), lambda i, j: (0, j)),
            o_spec=pl.BlockSpec((tm, tn), lambda i, j: (i, j)),
            out_shape=jax.ShapeDtypeStruct((s, d), F32))
    x_new, h_next = _res_norm(f"{tag}_post", x, y, g_post, FFN_RESIDUAL_WEIGHT, next_gain)
    return x_new, h_next, (x, h, gu, a, y)


class _GradReduce:
    def __init__(self, core, chip, n_layers, per_layer=()):
        self.core, self.chip, self.n_layers, self.per_layer = core, chip, n_layers, per_layer
        self.state = {}
        self.bufs = {}
        self.scatter_tokens = {}

    def start(self, kinds, layer, gs):
        gs, lands, send, recv, token = _swap_start(f"swap_start_{kinds[0]}_{layer}", gs)
        self.state[kinds, layer] = (gs, lands, send, recv)
        return token

    def exchange(self, kinds, layer, after):
        tag = f"{kinds[0]}_{layer}"
        gs, sibs = _swap_wait(f"swap_wait_{tag}", *self.state[kinds, layer], after)
        hs = [_add_core_halves(f"add_cores_{k}_{layer}", g, sib, self.core) for k, g, sib in zip(kinds, gs, sibs)]
        hs, lands, send, recv, token = _scatter_start(f"scatter_start_{tag}", hs)
        self.state[kinds, layer] = (hs, lands, send, recv)
        self.scatter_tokens[kinds, layer] = token
        return token

    def finish(self, kinds, layer, after):
        tag = f"{kinds[0]}_{layer}"
        hs, rcvs = _scatter_wait(f"scatter_wait_{tag}", *self.state.pop((kinds, layer)), after)
        for k, h, rcv in zip(kinds, hs, rcvs):
            if k in self.per_layer:
                last = self.bufs[k, layer] = _sum_chips(f"sum_chips_{k}_{layer}", h, rcv, self.core, self.chip,
                                                        0, 1, None)
            else:
                last = self.bufs[k] = _sum_chips(f"sum_chips_{k}_{layer}", h, rcv, self.core, self.chip, layer,
                                                 self.n_layers, self.bufs.get(k))
        return last


def _ffn_backward(tag, dx_new, saved, g_pre, g_post, gu_w, dn_w, red, kinds, layer, deps, head, following,
                  last=None):
    x, h, gu, a, y = saved
    s, d = x.shape
    nb, fs = gu_w.shape[0], gu_w.shape[2]
    f = dn_w.shape[0]
    fr = f // nb
    dy, dg_post = head or _norm_bwd(f"{tag}_post_bwd", dx_new, y, g_post, FFN_RESIDUAL_WEIGHT, None, BF16)
    dgu = _ffn_dact(f"{tag}_dact", dy, dn_w, gu, deps)
    dgu4 = dgu.reshape(nb, s, fs)
    tm, tw = _tile(d, 512), _tile(fs, 1408)
    nw = fs // tw
    tn = _tile(d, 1024)
    ts, td = _tile(s, 1024), _tile(d, 1024)

    def gate_up_gradient(deps):
        return _mm(f"{tag}_dwgu", h, dgu4, mode="tn", grid=(nb, nw, d // tm),
                   a_spec=pl.BlockSpec((s, tm), lambda k, j, i: (0, i)),
                   b_spec=pl.BlockSpec((None, s, tw), lambda k, j, i: (k, 0, j)),
                   o_spec=pl.BlockSpec((None, tm, tw), lambda k, j, i: (k, i, j)),
                   out_shape=jax.ShapeDtypeStruct((nb, d, fs), BF16), deps=deps)

    def down_gradient(deps):
        return _mm(f"{tag}_dwd", a, dy, mode="tn", grid=(nb, d // tn),
                   a_spec=pl.BlockSpec((s, fr), lambda i, j: (0, i)),
                   b_spec=pl.BlockSpec((s, tn), lambda i, j: (0, j)),
                   o_spec=pl.BlockSpec((None, fr, tn), lambda i, j: (i, 0, j)),
                   out_shape=jax.ShapeDtypeStruct((nb, fr, d), BF16), deps=deps)

    def input_gradient(deps):
        dh = _mm(f"{tag}_dh", dgu4, gu_w, mode="nt", grid=(s // ts, d // td, nb),
                 a_spec=pl.BlockSpec((None, ts, fs), lambda i, j, k: (k, i, 0)),
                 b_spec=pl.BlockSpec((None, td, fs), lambda i, j, k: (k, j, 0)),
                 o_spec=pl.BlockSpec((ts, td), lambda i, j, k: (i, j)),
                 out_shape=jax.ShapeDtypeStruct((s, d), F32), nk=nb, acc_shape=(ts, td), deps=deps)
        return _norm_bwd(f"{tag}_pre_bwd", dh, x, g_pre, 1.0, dx_new, F32, following)

    if last is None:
        started = red.start(kinds, layer, [gate_up_gradient(()), down_gradient(())])
        dx, dg_pre, *next_head = input_gradient((started,))
    else:
        dx, dg_pre, *next_head = input_gradient(())
        first = red.start(kinds[:1], layer, [gate_up_gradient((last(dg_pre, dg_post),))])
        second = red.start(kinds[1:], layer, [down_gradient((first,))])
        red.exchange(kinds[:1], layer, second)
    return dx, dg_pre, dg_post, tuple(next_head) or None


def _mixer_forward(tag, x, h, gains, next_gain, feed, k, conv_taps, dims):
    qd, kvd, cd = dims
    s, d = x.shape
    _, g_a, g_c, g_post = gains
    win_w = feed.take(k, h)
    nb, cw = win_w.shape[0], win_w.shape[2]
    tm = _tile(s, 1024)
    z = _mm(f"{tag}_in", h, win_w, mode="nn", grid=(nb, s // tm),
            a_spec=pl.BlockSpec((tm, d), lambda j, i: (i, 0)),
            b_spec=pl.BlockSpec((None, d, cw), lambda j, i: (j, 0, 0)),
            o_spec=pl.BlockSpec((tm, cw), lambda j, i: (i, j)),
            out_shape=jax.ShapeDtypeStruct((s, nb * cw), BF16))
    a, lse = _attn_fwd(f"{tag}_attn", z, qd, kvd)
    c = _conv_fwd(f"{tag}_conv", z, conv_taps, qd + 2 * kvd, cd)
    cat = _cat_norm_fwd(f"{tag}_cat", a, c, g_a, g_c)
    wout_w = feed.take(k + 1, cat).reshape(-1, d)
    mw = qd + cd
    tn = _tile(d, 1024)
    mixed = _mm(f"{tag}_out", cat, wout_w, mode="nn", grid=(s // tm, d // tn),
                a_spec=pl.BlockSpec((tm, mw), lambda i, j: (i, 0)),
                b_spec=pl.BlockSpec((mw, tn), lambda i, j: (0, j)),
                o_spec=pl.BlockSpec((tm, tn), lambda i, j: (i, j)),
                out_shape=jax.ShapeDtypeStruct((s, d), F32))
    x_new, h_next = _res_norm(f"{tag}_post", x, mixed, g_post, 1.0, next_gain)
    return x_new, h_next, (x, h, z, a, lse, c, cat, mixed)


def _mixer_backward(tag, dx_new, saved, gains, win_w, conv_taps, wout_w, dims, red, kinds, layer, deps, head,
                    following):
    qd, kvd, cd = dims
    x, h, z, a, lse, c, cat, mixed = saved
    s, d = x.shape
    nb, cw = win_w.shape[0], win_w.shape[2]
    g_pre, g_a, g_c, g_post = gains
    mw = qd + cd
    dmixed, dg_post = head or _norm_bwd(f"{tag}_post_bwd", dx_new, mixed, g_post, 1.0, None, BF16)
    tm, tn = _tile(s, 1024), _tile(mw, 1024)
    dcat = _mm(f"{tag}_dcat", dmixed, wout_w, mode="nt", grid=(s // tm, mw // tn),
               a_spec=pl.BlockSpec((tm, d), lambda i, j: (i, 0)),
               b_spec=pl.BlockSpec((tn, d), lambda i, j: (j, 0)),
               o_spec=pl.BlockSpec((tm, tn), lambda i, j: (i, j)),
               out_shape=jax.ShapeDtypeStruct((s, mw), F32), deps=deps)
    wr = mw // nb
    td = _tile(d, 1024)
    d_wout = _mm(f"{tag}_dwout", cat, dmixed, mode="tn", grid=(nb, d // td),
                 a_spec=pl.BlockSpec((s, wr), lambda i, j: (0, i)),
                 b_spec=pl.BlockSpec((s, td), lambda i, j: (0, j)),
                 o_spec=pl.BlockSpec((None, wr, td), lambda i, j: (i, 0, j)),
                 out_shape=jax.ShapeDtypeStruct((nb, wr, d), BF16))
    da, dc, dg_a, dg_c = _cat_norm_bwd(f"{tag}_cat_bwd", dcat, a, c, g_a, g_c)
    dhc, dbg, dcg, d_taps = _conv_bwd(f"{tag}_conv_bwd", z, conv_taps, dc, qd + 2 * kvd, cd)
    dq, dk, dv = _attn_bwd(f"{tag}_attn_bwd", z, a, lse, da, qd, kvd)
    dz = jnp.concatenate([dq, dk, dv, dhc, dbg, dcg], axis=1)
    th = _tile(d, 512)
    d_win = _mm(f"{tag}_dwin", h, dz, mode="tn", grid=(nb, d // th),
                a_spec=pl.BlockSpec((s, th), lambda k, i: (0, i)),
                b_spec=pl.BlockSpec((s, cw), lambda k, i: (0, k)),
                o_spec=pl.BlockSpec((None, th, cw), lambda k, i: (k, i, 0)),
                out_shape=jax.ShapeDtypeStruct((nb, d, cw), BF16))
    started = (red.start(kinds, layer, [d_win, d_wout]),)
    dh = _mm(f"{tag}_dh", dz, win_w, mode="nt", grid=(s // tm, d // td, nb),
             a_spec=pl.BlockSpec((tm, cw), lambda i, j, k: (i, k)),
             b_spec=pl.BlockSpec((None, td, cw), lambda i, j, k: (k, j, 0)),
             o_spec=pl.BlockSpec((tm, td), lambda i, j, k: (i, j)),
             out_shape=jax.ShapeDtypeStruct((s, d), F32), nk=nb, acc_shape=(tm, td), deps=started)
    dx, dg_pre, *next_head = _norm_bwd(f"{tag}_pre_bwd", dh, x, g_pre, 1.0, dx_new, F32, following)
    return dx, d_taps, (dg_pre, dg_a, dg_c, dg_post), tuple(next_head) or None


def _pad_cols(v, width):
    return jnp.pad(v, ((0, 0), (0, width - v.shape[1])))


def kernel(x, ffn1_norm_pre, ffn1_w_gate_up, ffn1_w_down, ffn1_norm_post, mix_norm_pre, w_in, conv_w, attn_out_norm, conv_out_norm, w_out, mix_norm_post, ffn2_norm_pre, ffn2_w_gate_up, ffn2_w_down, ffn2_norm_post, loss_target, m_ffn1_norm_pre, m_ffn1_w_gate_up, m_ffn1_w_down, m_ffn1_norm_post, m_mix_norm_pre, m_w_in, m_conv_w, m_attn_out_norm, m_conv_out_norm, m_w_out, m_mix_norm_post, m_ffn2_norm_pre, m_ffn2_w_gate_up, m_ffn2_w_down, m_ffn2_norm_post, v_ffn1_norm_pre, v_ffn1_w_gate_up, v_ffn1_w_down, v_ffn1_norm_post, v_mix_norm_pre, v_w_in, v_conv_w, v_attn_out_norm, v_conv_out_norm, v_w_out, v_mix_norm_post, v_ffn2_norm_pre, v_ffn2_w_gate_up, v_ffn2_w_down, v_ffn2_norm_post):
    _, s, d = x.shape
    n_layers = ffn1_norm_pre.shape[0]
    qd = attn_out_norm.shape[1]
    cd = conv_out_norm.shape[1]
    kvd = qd // Q_PER_KV
    dims = (qd, kvd, cd)
    assert N_CHIPS * w_in.shape[2] == qd + 2 * kvd + 3 * cd and qd + cd == N_CHIPS * w_out.shape[1]
    assert 2 * d <= SMALL_ROWS * LANES * SUBLANES
    chip = 2 * lax.axis_index("x") + lax.axis_index("y")
    chip_arr = chip.astype(jnp.int32).reshape(1)
    core = lax.axis_index("c").astype(jnp.int32).reshape(1)
    kinds = ("gu1", "dn1", "win", "wout", "gu2", "dn2")

    big = (ffn1_w_gate_up, ffn1_w_down, w_in, w_out, ffn2_w_gate_up, ffn2_w_down)
    nk = len(kinds)
    taps_all = _gather_taps(conv_w)
    feed = _WeightFeed()
    order = [(k, w, layer) for layer in range(n_layers) for k, w in zip(kinds, big)]
    k, w, layer = order[0]
    token = feed.start("gather_start_first", [_cast_into_slot(f"cast_{k}_{layer}", w, layer, chip_arr)], taps_all)
    feed.start("gather_start_rest", [_cast_into_slot(f"cast_{k}_{layer}", w, layer, chip_arr, (token,))
                                     for k, w, layer in order[1:]], token)
    taps = jnp.transpose(taps_all, (1, 2, 0, 3)).reshape(n_layers, CONV_WIDTH, cd)
    taps = jnp.pad(taps, ((0, 0), (0, SUBLANES - CONV_WIDTH), (0, 0)))

    def gain(g, layer):
        return g[layer][None, :]

    xs = x[0]
    hs = _norm_fwd("l0_ffn1_norm", xs, gain(ffn1_norm_pre, 0))
    saved = []
    for layer in range(n_layers):
        t = f"l{layer}"
        k0 = layer * nk
        xs, hs, s1 = _ffn_forward(f"{t}_ffn1", xs, hs, gain(ffn1_norm_post, layer), gain(mix_norm_pre, layer), feed, k0)
        mix_gains = (gain(mix_norm_pre, layer), gain(attn_out_norm, layer), gain(conv_out_norm, layer), gain(mix_norm_post, layer))
        xs, hs, s2 = _mixer_forward(f"{t}_mix", xs, hs, mix_gains, gain(ffn2_norm_pre, layer), feed, k0 + 2,
                                    taps[layer], dims)
        following = gain(ffn1_norm_pre, layer + 1) if layer + 1 < n_layers else None
        xs, hs, s3 = _ffn_forward(f"{t}_ffn2", xs, hs, gain(ffn2_norm_post, layer), following, feed, k0 + 4)
        saved.append((s1, s2, s3, mix_gains))
    wts = {k: [feed.fulls[layer * nk + i] for layer in range(n_layers)] for i, k in enumerate(kinds)}
    for k in ("dn1", "wout", "dn2"):
        wts[k] = [w.reshape(-1, d) for w in wts[k]]
    dxs, loss_part = _loss_head("loss_head", xs, loss_target[0])
    loss = lax.psum(jnp.sum(loss_part), ("x", "y", "c"))

    red = _GradReduce(core, chip_arr, n_layers, per_layer=("gu1", "dn1"))
    small = [None] * n_layers
    flow = {"deps": (), "in_flight": None}

    def between(dx, group):
        after = dx
        if flow["in_flight"] is not None:
            after = red.finish(*flow["in_flight"], after)
        flow["deps"] = (red.exchange(*group, after),)
        flow["in_flight"] = group

    head = None
    for layer in reversed(range(n_layers)):
        t = f"l{layer}"
        s1, s2, s3, mix_gains = saved[layer]
        after_ffn2 = (s2[7], mix_gains[3], 1.0)
        after_mix = (s1[4], gain(ffn1_norm_post, layer), FFN_RESIDUAL_WEIGHT)
        after_ffn1 = ((saved[layer - 1][2][4], gain(ffn2_norm_post, layer - 1), FFN_RESIDUAL_WEIGHT)
                      if layer > 0 else None)
        dxs, p_pre2, p_post2, head = _ffn_backward(
            f"{t}_ffn2", dxs, s3, gain(ffn2_norm_pre, layer), gain(ffn2_norm_post, layer),
            wts["gu2"][layer], wts["dn2"][layer], red, ("gu2", "dn2"), layer, flow["deps"], head, after_ffn2)
        between(dxs, (("gu2", "dn2"), layer))
        dxs, p_taps, (p_mpre, p_a, p_c, p_mpost), head = _mixer_backward(
            f"{t}_mix", dxs, s2, mix_gains, wts["win"][layer], taps[layer], wts["wout"][layer], dims,
            red, ("win", "wout"), layer, flow["deps"], head, after_mix)
        between(dxs, (("win", "wout"), layer))
        def pack_small(p_pre1, p_post1):
            tap_rows = jnp.zeros((CONV_WIDTH, SUBLANES, d), F32).at[:, 0, :cd].set(p_taps[:CONV_WIDTH])
            rows = [p_pre1, p_post1, p_mpre, jnp.concatenate([p_a, p_c], axis=1), p_mpost, p_pre2, p_post2]
            rows = jnp.concatenate([jnp.stack(rows), tap_rows], axis=0)
            small[layer] = jnp.pad(rows, ((0, SMALL_ROWS - rows.shape[0]), (0, 0), (0, 0)))

        def reduce_small(p_pre1, p_post1):
            pack_small(p_pre1, p_post1)
            flow["small"] = _allreduce_small(jnp.concatenate(small, axis=0))
            return flow["small"]

        dxs, p_pre1, p_post1, head = _ffn_backward(
            f"{t}_ffn1", dxs, s1, gain(ffn1_norm_pre, layer), gain(ffn1_norm_post, layer),
            wts["gu1"][layer], wts["dn1"][layer], red, ("gu1", "dn1"), layer, flow["deps"], head, after_ffn1,
            last=reduce_small if layer == 0 else None)
        if layer > 0:
            pack_small(p_pre1, p_post1)
        between(dxs, (("dn1",) if layer == 0 else ("gu1", "dn1"), layer))
    grad_x = dxs[None]

    weights = dict(ffn1_norm_pre=ffn1_norm_pre, ffn1_w_gate_up=ffn1_w_gate_up, ffn1_w_down=ffn1_w_down, ffn1_norm_post=ffn1_norm_post, mix_norm_pre=mix_norm_pre, w_in=w_in, conv_w=conv_w, attn_out_norm=attn_out_norm, conv_out_norm=conv_out_norm, w_out=w_out, mix_norm_post=mix_norm_post, ffn2_norm_pre=ffn2_norm_pre, ffn2_w_gate_up=ffn2_w_gate_up, ffn2_w_down=ffn2_w_down, ffn2_norm_post=ffn2_norm_post)
    m_in = dict(ffn1_norm_pre=m_ffn1_norm_pre, ffn1_w_gate_up=m_ffn1_w_gate_up, ffn1_w_down=m_ffn1_w_down, ffn1_norm_post=m_ffn1_norm_post, mix_norm_pre=m_mix_norm_pre, w_in=m_w_in, conv_w=m_conv_w, attn_out_norm=m_attn_out_norm, conv_out_norm=m_conv_out_norm, w_out=m_w_out, mix_norm_post=m_mix_norm_post, ffn2_norm_pre=m_ffn2_norm_pre, ffn2_w_gate_up=m_ffn2_w_gate_up, ffn2_w_down=m_ffn2_w_down, ffn2_norm_post=m_ffn2_norm_post)
    v_in = dict(ffn1_norm_pre=v_ffn1_norm_pre, ffn1_w_gate_up=v_ffn1_w_gate_up, ffn1_w_down=v_ffn1_w_down, ffn1_norm_post=v_ffn1_norm_post, mix_norm_pre=v_mix_norm_pre, w_in=v_w_in, conv_w=v_conv_w, attn_out_norm=v_attn_out_norm, conv_out_norm=v_conv_out_norm, w_out=v_w_out, mix_norm_post=v_mix_norm_post, ffn2_norm_pre=v_ffn2_norm_pre, ffn2_w_gate_up=v_ffn2_w_gate_up, ffn2_w_down=v_ffn2_w_down, ffn2_norm_post=v_ffn2_norm_post)
    kind_name = dict(gu1="ffn1_w_gate_up", dn1="ffn1_w_down", win="w_in", wout="w_out", gu2="ffn2_w_gate_up", dn2="ffn2_w_down")
    delta, new_m, new_v, grad = {}, {}, {}, {}

    def join_and_update(name, items, deps, after):
        ts, send_sems, recv_sems = _join_start(name, [red.bufs[it] for it in items], deps)
        for a, it in enumerate(items):
            k, layer = it if isinstance(it, tuple) else (it, None)
            n = kind_name[k]
            tag = n if layer is None else f"{n}_{layer}"
            g = _join_wait(f"join_wait_{tag}", ts[a], a, send_sems, recv_sems, after)
            prev = (delta[n], new_m[n], new_v[n], grad[n]) if n in delta else None
            delta[n], new_m[n], new_v[n], grad[n] = _adamw(f"adamw_{tag}", weights[n], g, m_in[n], v_in[n], True,
                                                           layer, prev)
            after = delta[n]
        return after

    early = ("wout", "win", "dn2", "gu2") + tuple((k, layer) for layer in range(1, n_layers) for k in ("dn1", "gu1"))
    last_groups = ((("gu1",), 0), flow["in_flight"])
    done_early = join_and_update("join_early", early, tuple(red.scatter_tokens[g] for g in last_groups), dxs)
    for group in last_groups:
        red.finish(*group, done_early)
    join_and_update("join_late", (("dn1", 0), ("gu1", 0)), (), done_early)

    small_sum = flow["small"].reshape(n_layers, SMALL_ROWS, d)
    g_ffn1_pre, g_ffn1_post, g_mix_pre = small_sum[:, 0], small_sum[:, 1], small_sum[:, 2]
    g_attn_out, g_conv_out = small_sum[:, 3, :qd], small_sum[:, 3, qd:qd + cd]
    g_mix_post, g_ffn2_pre, g_ffn2_post = small_sum[:, 4], small_sum[:, 5], small_sum[:, 6]
    cc = conv_w.shape[2]
    g_conv = lax.dynamic_slice_in_dim(small_sum[:, 7:7 + CONV_WIDTH, :cd], chip * cc, cc, axis=2)

    grad.update(ffn1_norm_pre=g_ffn1_pre, ffn1_norm_post=g_ffn1_post, mix_norm_pre=g_mix_pre, conv_w=g_conv, attn_out_norm=g_attn_out, conv_out_norm=g_conv_out, mix_norm_post=g_mix_post, ffn2_norm_pre=g_ffn2_pre, ffn2_norm_post=g_ffn2_post)
    names = list(weights)

    vectors = [n for n in names if n not in kind_name.values()]

    def pack(tree):
        flat = jnp.concatenate([tree[n].reshape(-1) for n in vectors])
        return jnp.pad(flat, (0, -flat.size % (SUBLANES * LANES))).reshape(-1, LANES)

    packed = _adamw("adamw_small", pack(weights), pack(grad), pack(m_in), pack(v_in))
    offset = 0
    for n in vectors:
        size = weights[n].size
        for tree, flat in zip((delta, new_m, new_v), packed):
            tree[n] = flat.reshape(-1)[offset:offset + size].reshape(weights[n].shape)
        offset += size

    return (loss, grad_x, *[grad[n] for n in names], *[delta[n] for n in names],
            *[new_m[n] for n in names], *[new_v[n] for n in names])
```

```python
import functools

import jax
import jax.numpy as jnp
from jax import lax
from jax.experimental import pallas as pl
from jax.experimental.pallas import tpu as pltpu

F32 = jnp.float32
BF16 = jnp.bfloat16
MESH = pl.DeviceIdType.MESH

NORM_EPS = 1e-6
HEAD_DIM = 128
Q_PER_KV = 4
CONV_WIDTH = 3
FFN_RESIDUAL_WEIGHT = 0.5
DILATED_BRANCHES = ((128, 1), (512, 4), (2048, 16))
ADAM_LR = 0.001
ADAM_B1 = 0.9
ADAM_B2 = 0.999
ADAM_EPS = 1e-08
ADAM_WD = 0.01
ADAM_STEP = 10

N_CHIPS = 4
N_DEV = 8
V7X_VMEM_BYTES = 64 << 20
VMEM_LIMIT = V7X_VMEM_BYTES - (12 << 20)
SUBLANES = 8
LANES = 128
SMALL_ROWS = 16
BIG_BLOCK = 4 << 20


def _params(*sem):
    return pltpu.CompilerParams(dimension_semantics=sem, vmem_limit_bytes=VMEM_LIMIT)


def _row_tile(rows, cols, itemsize=4, budget=2 << 20):
    t = rows
    while t * cols * itemsize > budget and t % 32 == 0:
        t //= 2
    return t


def _sum_to_sublanes(v):
    r, n = v.shape
    return v.reshape(r // SUBLANES, SUBLANES, n).sum(axis=0)


_DIMS = {
    "nn": (((1,), (0,)), ((), ())),
    "nt": (((1,), (1,)), ((), ())),
    "tn": (((0,), (0,)), ((), ())),
}


ANY_SPEC = pl.BlockSpec(memory_space=pl.ANY)


def _dot(a, b, mode):
    return lax.dot_general(a, b, _DIMS[mode], preferred_element_type=F32)


def _mm(name, a, b, *, mode, grid, a_spec, b_spec, o_spec, out_shape, nk=1, acc_shape=None, deps=()):
    nd = len(deps)

    def body(a_ref, b_ref, *rest):
        o_ref, scratch = rest[nd], rest[nd + 1:]
        r = _dot(a_ref[...], b_ref[...], mode)
        if nk == 1:
            o_ref[...] = r.astype(o_ref.dtype)
        else:
            acc = scratch[0]
            k = pl.program_id(len(grid) - 1)

            @pl.when(k == 0)
            def _():
                acc[...] = r

            @pl.when(k > 0)
            def _():
                acc[...] += r

            @pl.when(k == nk - 1)
            def _():
                o_ref[...] = acc[...].astype(o_ref.dtype)

    sem = ("parallel",) * (len(grid) - (1 if nk > 1 else 0)) + (("arbitrary",) if nk > 1 else ())
    return pl.pallas_call(
        body, name=name, grid=grid, in_specs=[a_spec, b_spec] + [ANY_SPEC] * nd, out_specs=o_spec,
        out_shape=out_shape, scratch_shapes=[pltpu.VMEM(acc_shape, F32)] if nk > 1 else [],
        compiler_params=_params(*sem),
    )(a, b, *deps)


def _tile(n, want):
    if n <= want:
        return n
    best = None
    for t in range(LANES, want + 1, LANES):
        if n % t == 0:
            best = t
    assert best is not None, (n, want)
    return best


def _norm_fwd(name, x, gain):
    s, d = x.shape
    tr = _row_tile(s, d, budget=BIG_BLOCK)

    def body(x_ref, g_ref, o_ref):
        xv = x_ref[...]
        r = lax.rsqrt(jnp.mean(xv * xv, axis=-1, keepdims=True) + NORM_EPS)
        o_ref[...] = (xv * r * g_ref[...]).astype(o_ref.dtype)

    return pl.pallas_call(
        body, name=name, grid=(s // tr,),
        in_specs=[pl.BlockSpec((tr, d), lambda i: (i, 0)), pl.BlockSpec((1, d), lambda i: (0, 0))],
        out_specs=pl.BlockSpec((tr, d), lambda i: (i, 0)),
        out_shape=jax.ShapeDtypeStruct((s, d), BF16), compiler_params=_params("parallel"),
    )(x, gain)


def _res_norm(name, x, y, gain, scale, next_gain=None):
    s, d = x.shape
    tr = _row_tile(s, d, budget=BIG_BLOCK)
    with_next = next_gain is not None

    def body(x_ref, y_ref, g_ref, *rest):
        yv = y_ref[...]
        r = lax.rsqrt(jnp.mean(yv * yv, axis=-1, keepdims=True) + NORM_EPS)
        xn = x_ref[...] + scale * (yv * r * g_ref[...])
        if with_next:
            ng_ref, o_ref, h_ref = rest
            rn = lax.rsqrt(jnp.mean(xn * xn, axis=-1, keepdims=True) + NORM_EPS)
            h_ref[...] = (xn * rn * ng_ref[...]).astype(h_ref.dtype)
        else:
            o_ref, = rest
        o_ref[...] = xn

    row = pl.BlockSpec((tr, d), lambda i: (i, 0))
    vec = pl.BlockSpec((1, d), lambda i: (0, 0))
    outs = pl.pallas_call(
        body, name=name, grid=(s // tr,),
        in_specs=[row, row, vec] + ([vec] if with_next else []), out_specs=[row] * (2 if with_next else 1),
        out_shape=[jax.ShapeDtypeStruct((s, d), F32)] + ([jax.ShapeDtypeStruct((s, d), BF16)] if with_next else []),
        compiler_params=_params("parallel"),
    )(x, y, gain, *((next_gain,) if with_next else ()))
    return (outs[0], outs[1]) if with_next else (outs[0], None)


def _rms_bwd(dn, yv, gv):
    r = lax.rsqrt(jnp.mean(yv * yv, axis=-1, keepdims=True) + NORM_EPS)
    xhat = yv * r
    dxn = dn * gv
    return r * (dxn - xhat * jnp.mean(dxn * xhat, axis=-1, keepdims=True)), _sum_to_sublanes(dn * xhat)


def _accumulate(ref, part):
    @pl.when(pl.program_id(0) == 0)
    def _():
        ref[...] = part

    @pl.when(pl.program_id(0) > 0)
    def _():
        ref[...] += part


def _norm_bwd(name, dout, yin, gain, scale, resid, out_dtype, following=None):
    s, d = yin.shape
    tr = _row_tile(s, d)
    has_resid = resid is not None
    chained = following is not None

    def body(*refs):
        refs = list(refs)
        do_ref, y_ref, g_ref = refs[:3]
        del refs[:3]
        r_ref = refs.pop(0) if has_resid else None
        if chained:
            y2_ref, g2_ref = refs[:2]
            del refs[:2]
        di_ref, dg_ref = refs[:2]
        din, part = _rms_bwd(scale * do_ref[...], y_ref[...], g_ref[...])
        _accumulate(dg_ref, part)
        if has_resid:
            din = din + r_ref[...]
        di_ref[...] = din.astype(di_ref.dtype)
        if chained:
            d2_ref, dg2_ref = refs[2:]
            d2, part2 = _rms_bwd(following[2] * din, y2_ref[...], g2_ref[...])
            _accumulate(dg2_ref, part2)
            d2_ref[...] = d2.astype(d2_ref.dtype)

    row = pl.BlockSpec((tr, d), lambda i: (i, 0))
    vec = pl.BlockSpec((1, d), lambda i: (0, 0))
    acc = pl.BlockSpec((SUBLANES, d), lambda i: (0, 0))
    ins = [row, row, vec] + ([row] if has_resid else []) + ([row, vec] if chained else [])
    args = (dout, yin, gain) + ((resid,) if has_resid else ()) + (tuple(following[:2]) if chained else ())
    outs = [row, acc] + ([row, acc] if chained else [])
    shapes = [jax.ShapeDtypeStruct((s, d), out_dtype), jax.ShapeDtypeStruct((SUBLANES, d), F32)]
    if chained:
        shapes += [jax.ShapeDtypeStruct((s, d), BF16), jax.ShapeDtypeStruct((SUBLANES, d), F32)]
    return pl.pallas_call(
        body, name=name, grid=(s // tr,), in_specs=ins, out_specs=outs, out_shape=shapes,
        compiler_params=_params("arbitrary"),
    )(*args)


def _loss_head(name, y, target, following):
    s, d = y.shape
    tr = _row_tile(s, d)
    y2, gain2, scale2 = following

    def body(y_ref, t_ref, y2_ref, g2_ref, dy_ref, l_ref, d2_ref, dg2_ref):
        e = y_ref[...] - t_ref[...]
        dy = e * (1.0 / d)
        dy_ref[...] = dy
        _accumulate(l_ref, _sum_to_sublanes(e * e) * (0.5 / d))
        d2, part2 = _rms_bwd(scale2 * dy, y2_ref[...], g2_ref[...])
        _accumulate(dg2_ref, part2)
        d2_ref[...] = d2.astype(d2_ref.dtype)

    row = pl.BlockSpec((tr, d), lambda i: (i, 0))
    acc = pl.BlockSpec((SUBLANES, d), lambda i: (0, 0))
    return pl.pallas_call(
        body, name=name, grid=(s // tr,), in_specs=[row, row, row, pl.BlockSpec((1, d), lambda i: (0, 0))],
        out_specs=[row, acc, row, acc],
        out_shape=[jax.ShapeDtypeStruct((s, d), F32), jax.ShapeDtypeStruct((SUBLANES, d), F32),
                   jax.ShapeDtypeStruct((s, d), BF16), jax.ShapeDtypeStruct((SUBLANES, d), F32)],
        compiler_params=_params("arbitrary"),
    )(y, target, y2, gain2)


def _ffn_up(name, h, gu_w):
    s, d = h.shape
    nb, _, fs = gu_w.shape
    hb = nb // 2
    w = gu_w.reshape(2, hb, d, fs)
    tm = _tile(s, 512)
    tn = _tile(fs, 1408)
    nj = fs // tn

    def body(h_ref, w_ref, gu_ref, a_ref):
        hv = h_ref[...]
        g = _dot(hv, w_ref[0], "nn")
        u = _dot(hv, w_ref[1], "nn")
        sg = jax.nn.sigmoid(g)
        silu = g * sg
        gu_ref[0] = (u * (sg * (1.0 + g * (1.0 - sg)))).astype(gu_ref.dtype)
        gu_ref[1] = silu.astype(gu_ref.dtype)
        a_ref[...] = (silu * u).astype(a_ref.dtype)

    return pl.pallas_call(
        body, name=name, grid=(hb, nj, s // tm),
        in_specs=[pl.BlockSpec((tm, d), lambda jb, jo, i: (i, 0)),
                  pl.BlockSpec((2, None, d, tn), lambda jb, jo, i: (0, jb, 0, jo))],
        out_specs=[pl.BlockSpec((2, None, tm, tn), lambda jb, jo, i: (0, jb, i, jo)),
                   pl.BlockSpec((tm, tn), lambda jb, jo, i: (i, jb * nj + jo))],
        out_shape=[jax.ShapeDtypeStruct((2, hb, s, fs), BF16), jax.ShapeDtypeStruct((s, hb * fs), BF16)],
        compiler_params=_params("parallel", "parallel", "parallel"),
    )(h, w)


def _ffn_dact(name, dy, dn_w, gu, deps=()):
    s, d = dy.shape
    _, hb, _, fs = gu.shape
    tm = _tile(s, 512)
    tn = _tile(fs, 1408)
    nj = fs // tn

    def body(dy_ref, w_ref, gu_ref, *rest):
        o_ref = rest[-1]
        wv = w_ref[...]
        parts = 2 if tm % (2 * SUBLANES * 2) == 0 else 1
        for r in range(parts):
            rows = slice(r * (tm // parts), (r + 1) * (tm // parts))
            da = _dot(dy_ref[rows, :], wv, "nt")
            o_ref[0, rows, :] = (da * gu_ref[0, rows, :].astype(F32)).astype(o_ref.dtype)
            o_ref[1, rows, :] = (da * gu_ref[1, rows, :].astype(F32)).astype(o_ref.dtype)

    blk = pl.BlockSpec((2, None, tm, tn), lambda jb, jo, i: (0, jb, i, jo))
    return pl.pallas_call(
        body, name=name, grid=(hb, nj, s // tm),
        in_specs=[pl.BlockSpec((tm, d), lambda jb, jo, i: (i, 0)),
                  pl.BlockSpec((tn, d), lambda jb, jo, i: (jb * nj + jo, 0)),
                  blk] + [ANY_SPEC] * len(deps),
        out_specs=blk, out_shape=jax.ShapeDtypeStruct(gu.shape, BF16),
        compiler_params=_params("parallel", "parallel", "parallel"),
    )(dy, dn_w, gu, *deps)


_MASKED = -1e30


def _attn_bias(s, tq):
    nd = s // tq
    dist = (jnp.arange(nd)[:, None, None] * tq + jnp.arange(tq)[None, :, None]) - jnp.arange(tq)[None, None, :]
    mult = jnp.zeros(dist.shape, F32)
    for window, dilation in DILATED_BRANCHES:
        mult = mult + ((dist >= 0) & (dist <= window) & (dist % dilation == 0)).astype(F32)
    return jnp.where(mult > 0.0, jnp.log(jnp.maximum(mult, 1.0)), _MASKED)


def _biased(sc, bias, scale):
    tq, tk = bias.shape
    return (sc.reshape(-1, tq, tk) * scale + bias[None]).reshape(sc.shape)


def _attn_specs(s, qd, kvd, tq):
    rw = Q_PER_KV * HEAD_DIM
    qspec = pl.BlockSpec((tq, rw), lambda g, i: (i, g))
    kspec = pl.BlockSpec((s, HEAD_DIM), lambda g, i: (0, qd // HEAD_DIM + g))
    vspec = pl.BlockSpec((s, HEAD_DIM), lambda g, i: (0, (qd + kvd) // HEAD_DIM + g))
    return rw, qspec, kspec, vspec


def _attn_fwd(name, z, qd, kvd):
    s = z.shape[0]
    tq = _tile(s, 256)
    nkv = kvd // HEAD_DIM
    rw, qspec, kspec, vspec = _attn_specs(s, qd, kvd, tq)
    scale = HEAD_DIM ** -0.5

    def body(q_ref, k_ref, v_ref, b_ref, o_ref, l_ref):
        i = pl.program_id(1)
        heads = [slice(h * HEAD_DIM, (h + 1) * HEAD_DIM) for h in range(Q_PER_KV)]
        q_all = jnp.concatenate([q_ref[:, cols] for cols in heads], axis=0)

        def chunk(j, carry):
            mx, den, acc = carry
            k0 = pl.multiple_of(j * tq, tq)
            kc, vc = k_ref[pl.ds(k0, tq), :], v_ref[pl.ds(k0, tq), :]
            sc = _biased(_dot(q_all, kc, "nt"), b_ref[i - j], scale)
            mx_new = jnp.maximum(mx, jnp.max(sc, axis=-1, keepdims=True))
            alpha = jnp.exp(mx - mx_new)
            p = jnp.exp(sc - mx_new)
            return (mx_new, alpha * den + jnp.sum(p, axis=-1, keepdims=True),
                    alpha * acc + _dot(p.astype(BF16), vc, "nn"))

        rows = Q_PER_KV * tq
        init = (jnp.full((rows, 1), _MASKED, F32), jnp.zeros((rows, 1), F32), jnp.zeros((rows, HEAD_DIM), F32))
        mx, den, acc = lax.fori_loop(0, i + 1, chunk, init)
        out = acc / den
        lse = mx + jnp.log(den)
        for h, cols in enumerate(heads):
            o_ref[:, cols] = out[h * tq:(h + 1) * tq]
            l_ref[:, cols] = jnp.broadcast_to(lse[h * tq:(h + 1) * tq], (tq, HEAD_DIM))

    bias = _attn_bias(s, tq)
    return pl.pallas_call(
        body, name=name, grid=(nkv, s // tq),
        in_specs=[qspec, kspec, vspec, pl.BlockSpec(bias.shape, lambda g, i: (0, 0, 0))], out_specs=[qspec, qspec],
        out_shape=[jax.ShapeDtypeStruct((s, qd), F32), jax.ShapeDtypeStruct((s, qd), F32)],
        compiler_params=_params("parallel", "parallel"),
    )(z, z, z, bias)


def _attn_bwd(name, z, o, lse, do, qd, kvd):
    s = z.shape[0]
    tq = _tile(s, 256)
    nkv = kvd // HEAD_DIM
    nq = s // tq
    rw, qspec, kspec, vspec = _attn_specs(s, qd, kvd, tq)
    scale = HEAD_DIM ** -0.5

    def body(q_ref, k_ref, v_ref, o_ref, l_ref, do_ref, b_ref, dq_ref, dk_ref, dv_ref, dk_acc, dv_acc):
        i = pl.program_id(1)
        heads = [slice(h * HEAD_DIM, (h + 1) * HEAD_DIM) for h in range(Q_PER_KV)]

        @pl.when(i == 0)
        def _():
            dk_acc[...] = jnp.zeros_like(dk_acc)
            dv_acc[...] = jnp.zeros_like(dv_acc)

        q_all = jnp.concatenate([q_ref[:, cols] for cols in heads], axis=0)
        do_all = jnp.concatenate([do_ref[:, cols].astype(BF16) for cols in heads], axis=0)
        lse_all = jnp.concatenate([l_ref[:, cols][:, :1] for cols in heads], axis=0)
        delta_all = jnp.concatenate(
            [jnp.sum(do_ref[:, cols] * o_ref[:, cols], axis=-1, keepdims=True) for cols in heads], axis=0)

        def chunk(j, dq):
            k0 = pl.multiple_of(j * tq, tq)
            kc, vc = k_ref[pl.ds(k0, tq), :], v_ref[pl.ds(k0, tq), :]
            p = jnp.exp(_biased(_dot(q_all, kc, "nt"), b_ref[i - j], scale) - lse_all)
            ds = (p * (_dot(do_all, vc, "nt") - delta_all) * scale).astype(BF16)
            dk_acc[pl.ds(k0, tq), :] += _dot(ds, q_all, "tn")
            dv_acc[pl.ds(k0, tq), :] += _dot(p.astype(BF16), do_all, "tn")
            return dq + _dot(ds, kc, "nn")

        dq = lax.fori_loop(0, i + 1, chunk, jnp.zeros((Q_PER_KV * tq, HEAD_DIM), F32))
        for h, cols in enumerate(heads):
            dq_ref[:, cols] = dq[h * tq:(h + 1) * tq].astype(dq_ref.dtype)

        @pl.when(i == nq - 1)
        def _():
            dk_ref[...] = dk_acc[...].astype(dk_ref.dtype)
            dv_ref[...] = dv_acc[...].astype(dv_ref.dtype)

    kvout = pl.BlockSpec((s, HEAD_DIM), lambda g, i: (0, g))
    bias = _attn_bias(s, tq)
    return pl.pallas_call(
        body, name=name, grid=(nkv, nq),
        in_specs=[qspec, kspec, vspec, qspec, qspec, qspec, pl.BlockSpec(bias.shape, lambda g, i: (0, 0, 0))],
        out_specs=[qspec, kvout, kvout],
        out_shape=[jax.ShapeDtypeStruct((s, qd), BF16), jax.ShapeDtypeStruct((s, kvd), BF16),
                   jax.ShapeDtypeStruct((s, kvd), BF16)],
        scratch_shapes=[pltpu.VMEM((s, HEAD_DIM), F32), pltpu.VMEM((s, HEAD_DIM), F32)],
        compiler_params=_params("parallel", "arbitrary"),
    )(z, z, z, o, lse, do, bias)


def _shift_down(v, n):
    rolled = pltpu.roll(v, n, 0)
    t = lax.broadcasted_iota(jnp.int32, v.shape, 0)
    return jnp.where(t >= n, rolled, 0.0)


def _shift_up(v, n):
    rows = v.shape[0]
    rolled = pltpu.roll(v, rows - n, 0)
    t = lax.broadcasted_iota(jnp.int32, v.shape, 0)
    return jnp.where(t < rows - n, rolled, 0.0)


def _conv_specs(s, base, cd, tc):
    zs = [pl.BlockSpec((s, tc), functools.partial(lambda j, off: (0, off + j), off=(base + n * cd) // tc))
          for n in range(3)]
    wspec = pl.BlockSpec((SUBLANES, tc), lambda j: (0, j))
    cspec = pl.BlockSpec((s, tc), lambda j: (0, j))
    return zs, wspec, cspec


def _conv_fwd(name, z, conv_w, base, cd):
    s = z.shape[0]
    tc = _tile(cd, 256)
    zs, wspec, cspec = _conv_specs(s, base, cd, tc)

    def body(h_ref, b_ref, c_ref, w_ref, o_ref):
        u = c_ref[...].astype(F32) * h_ref[...].astype(F32)
        y = w_ref[0:1, :] * _shift_down(u, 2) + w_ref[1:2, :] * _shift_down(u, 1) + w_ref[2:3, :] * u
        o_ref[...] = b_ref[...].astype(F32) * y

    return pl.pallas_call(
        body, name=name, grid=(cd // tc,), in_specs=zs + [wspec], out_specs=cspec,
        out_shape=jax.ShapeDtypeStruct((s, cd), F32), compiler_params=_params("parallel"),
    )(z, z, z, conv_w)


def _conv_bwd(name, z, conv_w, dc, base, cd):
    s = z.shape[0]
    tc = _tile(cd, 256)
    zs, wspec, cspec = _conv_specs(s, base, cd, tc)

    def body(h_ref, b_ref, c_ref, w_ref, dc_ref, dh_ref, db_ref, dcg_ref, dw_ref):
        hv, bv, cv = h_ref[...].astype(F32), b_ref[...].astype(F32), c_ref[...].astype(F32)
        u = cv * hv
        u1, u2 = _shift_down(u, 1), _shift_down(u, 2)
        w0, w1, w2 = w_ref[0:1, :], w_ref[1:2, :], w_ref[2:3, :]
        y = w0 * u2 + w1 * u1 + w2 * u
        dcv = dc_ref[...]
        db_ref[...] = (dcv * y).astype(db_ref.dtype)
        dy = dcv * bv
        du = w2 * dy + w1 * _shift_up(dy, 1) + w0 * _shift_up(dy, 2)
        dh_ref[...] = (du * cv).astype(dh_ref.dtype)
        dcg_ref[...] = (du * hv).astype(dcg_ref.dtype)
        g0 = jnp.sum(dy * u2, axis=0, keepdims=True)
        g1 = jnp.sum(dy * u1, axis=0, keepdims=True)
        g2 = jnp.sum(dy * u, axis=0, keepdims=True)
        r = lax.broadcasted_iota(jnp.int32, (SUBLANES, tc), 0)
        dw_ref[...] = jnp.where(r == 0, g0, jnp.where(r == 1, g1, jnp.where(r == 2, g2, 0.0)))

    return pl.pallas_call(
        body, name=name, grid=(cd // tc,), in_specs=zs + [wspec, cspec],
        out_specs=[cspec, cspec, cspec, wspec],
        out_shape=[jax.ShapeDtypeStruct((s, cd), BF16)] * 3 + [jax.ShapeDtypeStruct((SUBLANES, cd), F32)],
        compiler_params=_params("parallel"),
    )(z, z, z, conv_w, dc)


def _cat_norm_fwd(name, a, c, ga, gc):
    s, qd = a.shape
    cd = c.shape[1]
    tr = _row_tile(s, qd + cd)

    def body(a_ref, c_ref, ga_ref, gc_ref, o_ref):
        av, cv = a_ref[...], c_ref[...]
        ra = lax.rsqrt(jnp.mean(av * av, axis=-1, keepdims=True) + NORM_EPS)
        rc = lax.rsqrt(jnp.mean(cv * cv, axis=-1, keepdims=True) + NORM_EPS)
        o_ref[:, :qd] = (av * ra * ga_ref[...]).astype(o_ref.dtype)
        o_ref[:, qd:] = (cv * rc * gc_ref[...]).astype(o_ref.dtype)

    return pl.pallas_call(
        body, name=name, grid=(s // tr,),
        in_specs=[pl.BlockSpec((tr, qd), lambda i: (i, 0)), pl.BlockSpec((tr, cd), lambda i: (i, 0)),
                  pl.BlockSpec((1, qd), lambda i: (0, 0)), pl.BlockSpec((1, cd), lambda i: (0, 0))],
        out_specs=pl.BlockSpec((tr, qd + cd), lambda i: (i, 0)),
        out_shape=jax.ShapeDtypeStruct((s, qd + cd), BF16), compiler_params=_params("parallel"),
    )(a, c, ga, gc)


def _cat_norm_bwd(name, dcat, a, c, ga, gc):
    s, qd = a.shape
    cd = c.shape[1]
    tr = _row_tile(s, qd + cd)

    def one(dn, yv, gv):
        r = lax.rsqrt(jnp.mean(yv * yv, axis=-1, keepdims=True) + NORM_EPS)
        xhat = yv * r
        dxn = dn * gv
        return r * (dxn - xhat * jnp.mean(dxn * xhat, axis=-1, keepdims=True)), _sum_to_sublanes(dn * xhat)

    def body(d_ref, a_ref, c_ref, ga_ref, gc_ref, da_ref, dc_ref, dga_ref, dgc_ref):
        da, pa = one(d_ref[:, :qd], a_ref[...], ga_ref[...])
        dc, pc = one(d_ref[:, qd:], c_ref[...], gc_ref[...])
        da_ref[...] = da
        dc_ref[...] = dc

        @pl.when(pl.program_id(0) == 0)
        def _():
            dga_ref[...] = pa
            dgc_ref[...] = pc

        @pl.when(pl.program_id(0) > 0)
        def _():
            dga_ref[...] += pa
            dgc_ref[...] += pc

    ra = pl.BlockSpec((tr, qd), lambda i: (i, 0))
    rc = pl.BlockSpec((tr, cd), lambda i: (i, 0))
    return pl.pallas_call(
        body, name=name, grid=(s // tr,),
        in_specs=[pl.BlockSpec((tr, qd + cd), lambda i: (i, 0)), ra, rc,
                  pl.BlockSpec((1, qd), lambda i: (0, 0)), pl.BlockSpec((1, cd), lambda i: (0, 0))],
        out_specs=[ra, rc, pl.BlockSpec((SUBLANES, qd), lambda i: (0, 0)),
                   pl.BlockSpec((SUBLANES, cd), lambda i: (0, 0))],
        out_shape=[jax.ShapeDtypeStruct((s, qd), F32), jax.ShapeDtypeStruct((s, cd), F32),
                   jax.ShapeDtypeStruct((SUBLANES, qd), F32), jax.ShapeDtypeStruct((SUBLANES, cd), F32)],
        compiler_params=_params("arbitrary"),
    )(dcat, a, c, ga, gc)


def _adamw(name, w, g, m, v, emit_grad=False, layer=None, prev=None):
    shape = w.shape
    cols = shape[-1]
    rows = g.size // cols
    tr = _row_tile(rows, cols, budget=3 << 19)
    first = 0 if layer is None else layer * (rows // tr)
    bc1 = 1.0 - ADAM_B1 ** ADAM_STEP
    bc2 = 1.0 - ADAM_B2 ** ADAM_STEP
    n_out = 4 if emit_grad else 3

    def body(w_ref, g_ref, m_ref, v_ref, *rest):
        d_ref, nm_ref, nv_ref = rest[-n_out:][:3]
        gv = g_ref[...]
        mv = ADAM_B1 * m_ref[...] + (1.0 - ADAM_B1) * gv
        vv = ADAM_B2 * v_ref[...] + (1.0 - ADAM_B2) * (gv * gv)
        nm_ref[...] = mv
        nv_ref[...] = vv
        d_ref[...] = -ADAM_LR * ((mv / bc1) / (jnp.sqrt(vv / bc2) + ADAM_EPS) + ADAM_WD * w_ref[...])
        if emit_grad:
            rest[-1][...] = gv

    row = pl.BlockSpec((tr, cols), lambda i: (first + i, 0))
    g_row = pl.BlockSpec((tr, cols), lambda i: (i, 0))
    prev = tuple(prev) if prev is not None else ()
    total = w.size // cols
    outs = pl.pallas_call(
        body, name=name, grid=(rows // tr,), in_specs=[row, g_row, row, row] + [ANY_SPEC] * len(prev),
        out_specs=[row] * n_out, out_shape=[jax.ShapeDtypeStruct((total, cols), F32)] * n_out,
        input_output_aliases={4 + i: i for i in range(len(prev))}, compiler_params=_params("parallel"),
    )(w.reshape(total, cols), g.reshape(rows, cols), m.reshape(total, cols), v.reshape(total, cols),
      *(t.reshape(total, cols) for t in prev))
    return tuple(t.reshape(shape) for t in outs)


HBM_SPEC = pl.BlockSpec(memory_space=pltpu.HBM)


def _mesh_place():
    x, y, c = lax.axis_index("x"), lax.axis_index("y"), lax.axis_index("c")
    other_chips = [(1 - x, y), (x, 1 - y), (1 - x, 1 - y)]
    return x, y, c, other_chips


def _cast_into_slot(name, w, layer, chip, deps=()):
    _, r, cols = w.shape
    tr = _row_tile(r, cols, budget=BIG_BLOCK)

    def body(chip_ref, w_ref, *rest):
        o_ref = rest[-1]
        o_ref[...] = w_ref[...].astype(o_ref.dtype)

    return pl.pallas_call(
        body, name=name,
        grid_spec=pltpu.PrefetchScalarGridSpec(
            num_scalar_prefetch=1, grid=(r // tr,),
            in_specs=[pl.BlockSpec((None, tr, cols), lambda i, chip_ref: (layer, i, 0))] + [ANY_SPEC] * len(deps),
            out_specs=pl.BlockSpec((None, tr, cols), lambda i, chip_ref: (chip_ref[0], i, 0))),
        out_shape=jax.ShapeDtypeStruct((N_CHIPS, r, cols), BF16), compiler_params=_params("parallel"),
    )(chip, w, *deps)


SEM_SPEC = pl.BlockSpec(memory_space=pltpu.SEMAPHORE)
SPLIT_COPY = pltpu.CompilerParams(has_side_effects=pltpu.SideEffectType.DATAFLOW_SIDE_EFFECTING)
N_OTHER = N_CHIPS - 1
TOKEN_SPEC = pl.BlockSpec(memory_space=pltpu.VMEM)
TOKEN_SHAPE = jax.ShapeDtypeStruct((SUBLANES, LANES), F32)


def _in_hbm(arr):
    return pltpu.with_memory_space_constraint(arr, pltpu.HBM)


def _half_rows(ref, chip_idx, core):
    r2 = ref.shape[1] // 2
    return ref.at[chip_idx, pl.ds(core * r2, r2), :]


def _gather_start(name, fulls, after):
    na = len(fulls)

    def body(*refs):
        f_refs = refs[na + 1:2 * na + 1]
        send_sems, recv_sems = refs[2 * na + 1:3 * na + 1], refs[3 * na + 1:4 * na + 1]
        token = refs[4 * na + 1]
        x, y, c, chips = _mesh_place()
        for a in range(na):
            mine = _half_rows(f_refs[a], 2 * x + y, c)
            for j, (cx, cy) in enumerate(chips):
                pltpu.make_async_remote_copy(
                    src_ref=mine, dst_ref=mine, send_sem=send_sems[a].at[j], recv_sem=recv_sems[a].at[j],
                    device_id=(cx, cy, c), device_id_type=MESH).start()
        token[...] = jnp.zeros_like(token)

    outs = pl.pallas_call(
        body, name=name, in_specs=[HBM_SPEC] * na + [ANY_SPEC],
        out_specs=[HBM_SPEC] * na + [SEM_SPEC] * (2 * na) + [TOKEN_SPEC],
        out_shape=[pltpu.HBM(f.shape, f.dtype) for f in fulls] + [pltpu.SemaphoreType.DMA((N_OTHER,))] * (2 * na)
        + [TOKEN_SHAPE],
        input_output_aliases={a: a for a in range(na)}, compiler_params=SPLIT_COPY,
    )(*[_in_hbm(f) for f in fulls], after)
    return list(outs[:na]), list(outs[na:2 * na]), list(outs[2 * na:3 * na]), outs[3 * na]


def _gather_pass_on(name, full, recv_sems, after):
    def body(f_in, recv_sems, after_ref, f_ref, d2d_send, d2d_recv):
        x, y, c, chips = _mesh_place()
        for j, (cx, cy) in enumerate(chips):
            blk = _half_rows(f_ref, 2 * cx + cy, c)
            pltpu.make_async_remote_copy(
                src_ref=blk, dst_ref=blk, send_sem=d2d_send.at[j], recv_sem=recv_sems.at[j],
                device_id=(cx, cy, c), device_id_type=MESH).wait_recv()
            pltpu.make_async_remote_copy(
                src_ref=blk, dst_ref=blk, send_sem=d2d_send.at[j], recv_sem=d2d_recv.at[j],
                device_id=(x, y, 1 - c), device_id_type=MESH).start()

    return pl.pallas_call(
        body, name=name, in_specs=[HBM_SPEC, SEM_SPEC, ANY_SPEC], out_specs=[HBM_SPEC, SEM_SPEC, SEM_SPEC],
        out_shape=[pltpu.HBM(full.shape, full.dtype)] + [pltpu.SemaphoreType.DMA((N_OTHER,))] * 2,
        input_output_aliases={0: 0}, compiler_params=SPLIT_COPY,
    )(full, recv_sems, after)


def _gather_arrive(name, full, ici_send, d2d_send, d2d_recv, after):
    def body(f_in, ici_send, d2d_send, d2d_recv, after_ref, f_ref):
        x, y, c, chips = _mesh_place()
        for j, (cx, cy) in enumerate(chips):
            mine = _half_rows(f_ref, 2 * x + y, c)
            passed = _half_rows(f_ref, 2 * cx + cy, c)
            theirs = _half_rows(f_ref, 2 * cx + cy, 1 - c)
            pltpu.make_async_remote_copy(
                src_ref=mine, dst_ref=mine, send_sem=ici_send.at[j], recv_sem=d2d_recv.at[j],
                device_id=(cx, cy, c), device_id_type=MESH).wait_send()
            pltpu.make_async_remote_copy(
                src_ref=passed, dst_ref=passed, send_sem=d2d_send.at[j], recv_sem=d2d_recv.at[j],
                device_id=(x, y, 1 - c), device_id_type=MESH).wait_send()
            pltpu.make_async_remote_copy(
                src_ref=theirs, dst_ref=theirs, send_sem=d2d_send.at[j], recv_sem=d2d_recv.at[j],
                device_id=(x, y, 1 - c), device_id_type=MESH).wait_recv()

    return pl.pallas_call(
        body, name=name, in_specs=[HBM_SPEC, SEM_SPEC, SEM_SPEC, SEM_SPEC, ANY_SPEC], out_specs=HBM_SPEC,
        out_shape=pltpu.HBM(full.shape, full.dtype), input_output_aliases={0: 0}, compiler_params=SPLIT_COPY,
    )(full, ici_send, d2d_send, d2d_recv, after)


def _gather_taps(conv_w):
    def body(cw_ref, cwf_ref, send_sems, recv_sems, local_sem):
        x, y, c, chips = _mesh_place()
        k_me = 2 * x + y
        local = pltpu.make_async_copy(cw_ref, cwf_ref.at[k_me], local_sem)
        local.start()
        copies = [pltpu.make_async_remote_copy(
            src_ref=cw_ref, dst_ref=cwf_ref.at[k_me], send_sem=send_sems.at[j], recv_sem=recv_sems.at[j],
            device_id=(cx, cy, c), device_id_type=MESH) for j, (cx, cy) in enumerate(chips)]
        for cp in copies:
            cp.start()
        for j, (cx, cy) in enumerate(chips):
            pltpu.make_async_remote_copy(
                src_ref=cw_ref, dst_ref=cwf_ref.at[2 * cx + cy], send_sem=send_sems.at[j], recv_sem=recv_sems.at[j],
                device_id=(cx, cy, c), device_id_type=MESH).wait_recv()
        for cp in copies:
            cp.wait_send()
        local.wait()

    return pl.pallas_call(
        body, name="gather_taps", in_specs=[HBM_SPEC], out_specs=HBM_SPEC,
        out_shape=jax.ShapeDtypeStruct((N_CHIPS,) + conv_w.shape, conv_w.dtype),
        scratch_shapes=[pltpu.SemaphoreType.DMA((N_OTHER,))] * 2 + [pltpu.SemaphoreType.DMA],
    )(conv_w)


def _sibling_half(g_ref, c):
    r2 = g_ref.shape[1] // 2
    return g_ref.at[:, pl.ds((1 - c) * r2, r2), :]


def _swap_copy(g_ref, land_ref, send_sems, recv_sems, a):
    x, y, c, _ = _mesh_place()
    return pltpu.make_async_remote_copy(
        src_ref=_sibling_half(g_ref, c), dst_ref=land_ref, send_sem=send_sems.at[a], recv_sem=recv_sems.at[a],
        device_id=(x, y, 1 - c), device_id_type=MESH)


def _swap_start(name, gs):
    n = len(gs)

    def body(*refs):
        g_refs, land_refs = refs[n:2 * n], refs[2 * n:3 * n]
        send_sems, recv_sems, token = refs[3 * n:]
        for a in range(n):
            _swap_copy(g_refs[a], land_refs[a], send_sems, recv_sems, a).start()
        token[...] = jnp.zeros_like(token)

    outs = pl.pallas_call(
        body, name=name, in_specs=[HBM_SPEC] * n,
        out_specs=[HBM_SPEC] * (2 * n) + [SEM_SPEC, SEM_SPEC, TOKEN_SPEC],
        out_shape=[pltpu.HBM(g.shape, g.dtype) for g in gs]
        + [pltpu.HBM((g.shape[0], g.shape[1] // 2, g.shape[2]), g.dtype) for g in gs]
        + [pltpu.SemaphoreType.DMA((n,)), pltpu.SemaphoreType.DMA((n,)), TOKEN_SHAPE],
        input_output_aliases={a: a for a in range(n)}, compiler_params=SPLIT_COPY,
    )(*[_in_hbm(g) for g in gs])
    return list(outs[:n]), list(outs[n:2 * n]), outs[2 * n], outs[2 * n + 1], outs[2 * n + 2]


def _swap_wait(name, gs, lands, send_sems, recv_sems, after):
    n = len(gs)

    def body(*refs):
        send_sems, recv_sems = refs[2 * n], refs[2 * n + 1]
        g_refs, land_refs = refs[2 * n + 3:3 * n + 3], refs[3 * n + 3:]
        for a in range(n):
            copy = _swap_copy(g_refs[a], land_refs[a], send_sems, recv_sems, a)
            copy.wait_send()
            copy.wait_recv()

    outs = pl.pallas_call(
        body, name=name, in_specs=[HBM_SPEC] * (2 * n) + [SEM_SPEC, SEM_SPEC, ANY_SPEC],
        out_specs=[HBM_SPEC] * (2 * n),
        out_shape=[pltpu.HBM(t.shape, t.dtype) for t in list(gs) + list(lands)],
        input_output_aliases={a: a for a in range(2 * n)}, compiler_params=SPLIT_COPY,
    )(*gs, *lands, send_sems, recv_sems, after)
    return list(outs[:n]), list(outs[n:])


def _add_core_halves(name, g, sib, core):
    nb, r, cols = g.shape
    r2 = r // 2
    tr = _row_tile(r2, cols, itemsize=2, budget=BIG_BLOCK)
    nrt = r2 // tr

    def body(core_ref, g_ref, s_ref, o_ref):
        o_ref[...] = (g_ref[...].astype(F32) + s_ref[...].astype(F32)).astype(o_ref.dtype)

    return pl.pallas_call(
        body, name=name,
        grid_spec=pltpu.PrefetchScalarGridSpec(
            num_scalar_prefetch=1, grid=(nb, nrt),
            in_specs=[pl.BlockSpec((None, tr, cols), lambda k, i, core_ref: (k, core_ref[0] * nrt + i, 0)),
                      pl.BlockSpec((None, tr, cols), lambda k, i, core_ref: (k, i, 0))],
            out_specs=pl.BlockSpec((None, tr, cols), lambda k, i, core_ref: (k, i, 0))),
        out_shape=jax.ShapeDtypeStruct((nb, r2, cols), BF16), compiler_params=_params("parallel", "parallel"),
    )(core, g, sib)


def _scatter_copies(h_refs, land_refs, send_sems, recv_sems):
    x, y, c, chips = _mesh_place()
    return [pltpu.make_async_remote_copy(
        src_ref=h_ref.at[2 * cx + cy], dst_ref=land_ref.at[j],
        send_sem=send_sems.at[a * N_OTHER + j], recv_sem=recv_sems.at[a * N_OTHER + j],
        device_id=(cx, cy, c), device_id_type=MESH)
        for a, (h_ref, land_ref) in enumerate(zip(h_refs, land_refs)) for j, (cx, cy) in enumerate(chips)]


def _scatter_start(name, hs):
    n = len(hs)

    def body(*refs):
        h_refs, land_refs = refs[n:2 * n], refs[2 * n:3 * n]
        send_sems, recv_sems, token = refs[3 * n:]
        for copy in _scatter_copies(h_refs, land_refs, send_sems, recv_sems):
            copy.start()
        token[...] = jnp.zeros_like(token)

    outs = pl.pallas_call(
        body, name=name, in_specs=[HBM_SPEC] * n,
        out_specs=[HBM_SPEC] * (2 * n) + [SEM_SPEC, SEM_SPEC, TOKEN_SPEC],
        out_shape=[pltpu.HBM(h.shape, h.dtype) for h in hs]
        + [pltpu.HBM((N_OTHER,) + h.shape[1:], h.dtype) for h in hs]
        + [pltpu.SemaphoreType.DMA((n * N_OTHER,)), pltpu.SemaphoreType.DMA((n * N_OTHER,)), TOKEN_SHAPE],
        input_output_aliases={a: a for a in range(n)}, compiler_params=SPLIT_COPY,
    )(*[_in_hbm(h) for h in hs])
    return list(outs[:n]), list(outs[n:2 * n]), outs[2 * n], outs[2 * n + 1], outs[2 * n + 2]


def _scatter_wait(name, hs, lands, send_sems, recv_sems, after):
    afters = tuple(after) if isinstance(after, (tuple, list)) else (after,)
    n = len(hs)

    def body(*refs):
        send_sems, recv_sems = refs[2 * n], refs[2 * n + 1]
        h_refs, land_refs = refs[-2 * n:-n], refs[-n:]
        for copy in _scatter_copies(h_refs, land_refs, send_sems, recv_sems):
            copy.wait_send()
            copy.wait_recv()

    outs = pl.pallas_call(
        body, name=name, in_specs=[HBM_SPEC] * (2 * n) + [SEM_SPEC, SEM_SPEC] + [ANY_SPEC] * len(afters),
        out_specs=[HBM_SPEC] * (2 * n),
        out_shape=[pltpu.HBM(t.shape, t.dtype) for t in list(hs) + list(lands)],
        input_output_aliases={a: a for a in range(2 * n)}, compiler_params=SPLIT_COPY,
    )(*hs, *lands, send_sems, recv_sems, *afters)
    return list(outs[:n]), list(outs[n:])


def _sum_chips(name, hs, rcv, core, chip, layer, n_layers, prev):
    _, r2, cols = hs.shape
    tr = _row_tile(r2, cols, budget=BIG_BLOCK)
    nrt = r2 // tr

    def body(core_ref, chip_ref, h_ref, r_ref, *rest):
        o_ref = rest[-1]
        acc = h_ref[...].astype(F32)
        for j in range(N_CHIPS - 1):
            acc = acc + r_ref[j].astype(F32)
        o_ref[...] = acc

    in_specs = [pl.BlockSpec((None, tr, cols), lambda i, core_ref, chip_ref: (chip_ref[0], i, 0)),
                pl.BlockSpec((N_CHIPS - 1, tr, cols), lambda i, core_ref, chip_ref: (0, i, 0))]
    args = [core, chip, hs, rcv]
    aliases = {}
    if prev is not None:
        in_specs.append(pl.BlockSpec(memory_space=pl.ANY))
        args.append(prev)
        aliases = {4: 0}
    return pl.pallas_call(
        body, name=name,
        grid_spec=pltpu.PrefetchScalarGridSpec(
            num_scalar_prefetch=2, grid=(nrt,), in_specs=in_specs,
            out_specs=pl.BlockSpec((None, tr, cols), lambda i, core_ref, chip_ref: (layer, core_ref[0] * nrt + i, 0))),
        out_shape=jax.ShapeDtypeStruct((n_layers, 2 * r2, cols), F32), input_output_aliases=aliases,
        compiler_params=_params("parallel"),
    )(*args)


def _join_copy(t_ref, send_sems, recv_sems, a):
    x, y, c, _ = _mesh_place()
    r2 = t_ref.shape[1] // 2
    mine = t_ref.at[:, pl.ds(c * r2, r2), :]
    return pltpu.make_async_remote_copy(
        src_ref=mine, dst_ref=mine, send_sem=send_sems.at[a], recv_sem=recv_sems.at[a],
        device_id=(x, y, 1 - c), device_id_type=MESH)


def _join_start(name, ts, deps=()):
    n, nd = len(ts), len(deps)

    def body(*refs):
        t_refs = refs[n + nd:2 * n + nd]
        send_sems, recv_sems = refs[2 * n + nd:]
        for a in range(n):
            _join_copy(t_refs[a], send_sems, recv_sems, a).start()

    outs = pl.pallas_call(
        body, name=name, in_specs=[HBM_SPEC] * n + [ANY_SPEC] * nd, out_specs=[HBM_SPEC] * n + [SEM_SPEC, SEM_SPEC],
        out_shape=[pltpu.HBM(t.shape, t.dtype) for t in ts] + [pltpu.SemaphoreType.DMA((n,))] * 2,
        input_output_aliases={a: a for a in range(n)}, compiler_params=SPLIT_COPY,
    )(*[_in_hbm(t) for t in ts], *deps)
    return list(outs[:n]), outs[n], outs[n + 1]


def _join_wait(name, t, a, send_sems, recv_sems, after):
    def body(t_in, send_sems, recv_sems, after_ref, t_ref):
        copy = _join_copy(t_ref, send_sems, recv_sems, a)
        copy.wait_send()
        copy.wait_recv()

    return pl.pallas_call(
        body, name=name, in_specs=[HBM_SPEC, SEM_SPEC, SEM_SPEC, ANY_SPEC], out_specs=HBM_SPEC,
        out_shape=pltpu.HBM(t.shape, t.dtype), input_output_aliases={0: 0}, compiler_params=SPLIT_COPY,
    )(t, send_sems, recv_sems, after)


def _allreduce_small(p):
    n, _, w = p.shape

    def body(p_ref, o_ref, buf, send_sems, recv_sems):
        x, y, c, _ = _mesh_place()
        me = 4 * x + 2 * y + c
        buf[me] = jnp.sum(p_ref[...], axis=1)
        copies = []
        for pat in range(1, N_DEV):
            fx, fy, fc = (pat >> 2) & 1, (pat >> 1) & 1, pat & 1
            copies.append(pltpu.make_async_remote_copy(
                src_ref=buf.at[me], dst_ref=buf.at[me], send_sem=send_sems.at[pat - 1], recv_sem=recv_sems.at[pat - 1],
                device_id=(x ^ fx, y ^ fy, c ^ fc), device_id_type=MESH))
        for cp in copies:
            cp.start()
        for cp in copies:
            cp.wait()
        acc = buf[0]
        for dev in range(1, N_DEV):
            acc = acc + buf[dev]
        o_ref[...] = acc

    return pl.pallas_call(
        body, name="allreduce_small", in_specs=[pl.BlockSpec(memory_space=pltpu.VMEM)],
        out_specs=pl.BlockSpec(memory_space=pltpu.VMEM), out_shape=jax.ShapeDtypeStruct((n, w), F32),
        scratch_shapes=[pltpu.VMEM((N_DEV, n, w), F32), pltpu.SemaphoreType.DMA((N_DEV - 1,)),
                        pltpu.SemaphoreType.DMA((N_DEV - 1,))],
    )(p)


class _WeightFeed:
    def __init__(self):
        self.fulls, self.ici_send, self.ici_recv, self.d2d = [], [], [], []

    def start(self, name, fulls, after):
        started, send, recv, token = _gather_start(name, fulls, after)
        self.fulls += started
        self.ici_send += send
        self.ici_recv += recv
        self.d2d += [None] * len(fulls)
        self.token = token
        return token

    def _pass_on(self, k, after):
        if k == 0:
            after = self.token
        if k < len(self.fulls) and self.d2d[k] is None:
            self.fulls[k], send, recv = _gather_pass_on(f"gather_pass_{k}", self.fulls[k], self.ici_recv[k], after)
            self.d2d[k] = (send, recv)

    def take(self, k, after):
        self._pass_on(k, after)
        self.fulls[k] = _gather_arrive(f"gather_arrive_{k}", self.fulls[k], self.ici_send[k], *self.d2d[k], after)
        return self.fulls[k]


def _ffn_forward(tag, x, h, g_post, next_gain, feed, k):
    s, d = x.shape
    gu_w = feed.take(k, h)
    gu, a = _ffn_up(f"{tag}_up", h, gu_w)
    dn_w = feed.take(k + 1, a).reshape(-1, d)
    f = dn_w.shape[0]
    tm, tn = _tile(s, 1024), _tile(d, 512)
    y = _mm(f"{tag}_down", a, dn_w, mode="nn", grid=(s // tm, d // tn),
            a_spec=pl.BlockSpec((tm, f), lambda i, j: (i, 0)),
            b_spec=pl.BlockSpec((f, tn), lambda i, j: (0, j)),
            o_spec=pl.BlockSpec((tm, tn), lambda i, j: (i, j)),
            out_shape=jax.ShapeDtypeStruct((s, d), F32))
    x_new, h_next = _res_norm(f"{tag}_post", x, y, g_post, FFN_RESIDUAL_WEIGHT, next_gain)
    return x_new, h_next, (x, h, gu, a, y)


class _GradReduce:
    def __init__(self, core, chip, n_layers, per_layer=()):
        self.core, self.chip, self.n_layers, self.per_layer = core, chip, n_layers, per_layer
        self.state = {}
        self.bufs = {}
        self.scatter_tokens = {}

    def start(self, kinds, layer, gs):
        gs, lands, send, recv, token = _swap_start(f"swap_start_{kinds[0]}_{layer}", gs)
        self.state[kinds, layer] = (gs, lands, send, recv)
        return token

    def exchange(self, kinds, layer, after):
        tag = f"{kinds[0]}_{layer}"
        gs, sibs = _swap_wait(f"swap_wait_{tag}", *self.state[kinds, layer], after)
        hs = [_add_core_halves(f"add_cores_{k}_{layer}", g, sib, self.core) for k, g, sib in zip(kinds, gs, sibs)]
        hs, lands, send, recv, token = _scatter_start(f"scatter_start_{tag}", hs)
        self.state[kinds, layer] = (hs, lands, send, recv)
        self.scatter_tokens[kinds, layer] = token
        return token

    def finish(self, kinds, layer, after):
        tag = f"{kinds[0]}_{layer}"
        hs, rcvs = _scatter_wait(f"scatter_wait_{tag}", *self.state.pop((kinds, layer)), after)
        for k, h, rcv in zip(kinds, hs, rcvs):
            if k in self.per_layer:
                last = self.bufs[k, layer] = _sum_chips(f"sum_chips_{k}_{layer}", h, rcv, self.core, self.chip,
                                                        0, 1, None)
            else:
                last = self.bufs[k] = _sum_chips(f"sum_chips_{k}_{layer}", h, rcv, self.core, self.chip, layer,
                                                 self.n_layers, self.bufs.get(k))
        return last


def _ffn_backward(tag, dx_new, saved, g_pre, g_post, gu_w, dn_w, red, kinds, layer, deps, head, following,
                  last=None):
    x, h, gu, a, y = saved
    s, d = x.shape
    nb, fs = gu_w.shape[0], gu_w.shape[2]
    f = dn_w.shape[0]
    fr = f // nb
    dy, dg_post = head or _norm_bwd(f"{tag}_post_bwd", dx_new, y, g_post, FFN_RESIDUAL_WEIGHT, None, BF16)
    dgu = _ffn_dact(f"{tag}_dact", dy, dn_w, gu, deps)
    dgu4 = dgu.reshape(nb, s, fs)
    tm, tw = _tile(d, 1024), _tile(fs, 1408)
    nw = fs // tw
    tn = _tile(d, 1024)
    ts, td = _tile(s, 1024), _tile(d, 1024)

    def gate_up_gradient(deps):
        return _mm(f"{tag}_dwgu", h, dgu4, mode="tn", grid=(nb, nw, d // tm),
                   a_spec=pl.BlockSpec((s, tm), lambda k, j, i: (0, i)),
                   b_spec=pl.BlockSpec((None, s, tw), lambda k, j, i: (k, 0, j)),
                   o_spec=pl.BlockSpec((None, tm, tw), lambda k, j, i: (k, i, j)),
                   out_shape=jax.ShapeDtypeStruct((nb, d, fs), BF16), deps=deps)

    def down_gradient(deps):
        return _mm(f"{tag}_dwd", a, dy, mode="tn", grid=(nb, d // tn),
                   a_spec=pl.BlockSpec((s, fr), lambda i, j: (0, i)),
                   b_spec=pl.BlockSpec((s, tn), lambda i, j: (0, j)),
                   o_spec=pl.BlockSpec((None, fr, tn), lambda i, j: (i, 0, j)),
                   out_shape=jax.ShapeDtypeStruct((nb, fr, d), BF16), deps=deps)

    def input_gradient(deps):
        dh = _mm(f"{tag}_dh", dgu4, gu_w, mode="nt", grid=(s // ts, d // td, nb),
                 a_spec=pl.BlockSpec((None, ts, fs), lambda i, j, k: (k, i, 0)),
                 b_spec=pl.BlockSpec((None, td, fs), lambda i, j, k: (k, j, 0)),
                 o_spec=pl.BlockSpec((ts, td), lambda i, j, k: (i, j)),
                 out_shape=jax.ShapeDtypeStruct((s, d), F32), nk=nb, acc_shape=(ts, td), deps=deps)
        return _norm_bwd(f"{tag}_pre_bwd", dh, x, g_pre, 1.0, dx_new, F32, following)

    if last is None:
        started = red.start(kinds, layer, [gate_up_gradient(()), down_gradient(())])
        dx, dg_pre, *next_head = input_gradient((started,))
    else:
        dx, dg_pre, *next_head = input_gradient(())
        first = red.start(kinds[:1], layer, [gate_up_gradient((last(dg_pre, dg_post),))])
        second = red.start(kinds[1:], layer, [down_gradient((first,))])
        red.exchange(kinds[:1], layer, second)
    return dx, dg_pre, dg_post, tuple(next_head) or None


def _mixer_forward(tag, x, h, gains, next_gain, feed, k, conv_taps, dims):
    qd, kvd, cd = dims
    s, d = x.shape
    _, g_a, g_c, g_post = gains
    win_w = feed.take(k, h)
    nb, cw = win_w.shape[0], win_w.shape[2]
    tm = _tile(s, 1024)
    z = _mm(f"{tag}_in", h, win_w, mode="nn", grid=(nb, s // tm),
            a_spec=pl.BlockSpec((tm, d), lambda j, i: (i, 0)),
            b_spec=pl.BlockSpec((None, d, cw), lambda j, i: (j, 0, 0)),
            o_spec=pl.BlockSpec((tm, cw), lambda j, i: (i, j)),
            out_shape=jax.ShapeDtypeStruct((s, nb * cw), BF16))
    a, lse = _attn_fwd(f"{tag}_attn", z, qd, kvd)
    c = _conv_fwd(f"{tag}_conv", z, conv_taps, qd + 2 * kvd, cd)
    cat = _cat_norm_fwd(f"{tag}_cat", a, c, g_a, g_c)
    wout_w = feed.take(k + 1, cat).reshape(-1, d)
    mw = qd + cd
    tn = _tile(d, 1024)
    mixed = _mm(f"{tag}_out", cat, wout_w, mode="nn", grid=(s // tm, d // tn),
                a_spec=pl.BlockSpec((tm, mw), lambda i, j: (i, 0)),
                b_spec=pl.BlockSpec((mw, tn), lambda i, j: (0, j)),
                o_spec=pl.BlockSpec((tm, tn), lambda i, j: (i, j)),
                out_shape=jax.ShapeDtypeStruct((s, d), F32))
    x_new, h_next = _res_norm(f"{tag}_post", x, mixed, g_post, 1.0, next_gain)
    return x_new, h_next, (x, h, z, a, lse, c, cat, mixed)


def _mixer_backward(tag, dx_new, saved, gains, win_w, conv_taps, wout_w, dims, red, kinds, layer, deps, head,
                    following):
    qd, kvd, cd = dims
    x, h, z, a, lse, c, cat, mixed = saved
    s, d = x.shape
    nb, cw = win_w.shape[0], win_w.shape[2]
    g_pre, g_a, g_c, g_post = gains
    mw = qd + cd
    dmixed, dg_post = head or _norm_bwd(f"{tag}_post_bwd", dx_new, mixed, g_post, 1.0, None, BF16)
    tm, tn = _tile(s, 1024), _tile(mw, 1024)
    dcat = _mm(f"{tag}_dcat", dmixed, wout_w, mode="nt", grid=(s // tm, mw // tn),
               a_spec=pl.BlockSpec((tm, d), lambda i, j: (i, 0)),
               b_spec=pl.BlockSpec((tn, d), lambda i, j: (j, 0)),
               o_spec=pl.BlockSpec((tm, tn), lambda i, j: (i, j)),
               out_shape=jax.ShapeDtypeStruct((s, mw), F32), deps=deps)
    wr = mw // nb
    td = _tile(d, 1024)
    d_wout = _mm(f"{tag}_dwout", cat, dmixed, mode="tn", grid=(nb, d // td),
                 a_spec=pl.BlockSpec((s, wr), lambda i, j: (0, i)),
                 b_spec=pl.BlockSpec((s, td), lambda i, j: (0, j)),
                 o_spec=pl.BlockSpec((None, wr, td), lambda i, j: (i, 0, j)),
                 out_shape=jax.ShapeDtypeStruct((nb, wr, d), BF16))
    da, dc, dg_a, dg_c = _cat_norm_bwd(f"{tag}_cat_bwd", dcat, a, c, g_a, g_c)
    dhc, dbg, dcg, d_taps = _conv_bwd(f"{tag}_conv_bwd", z, conv_taps, dc, qd + 2 * kvd, cd)
    dq, dk, dv = _attn_bwd(f"{tag}_attn_bwd", z, a, lse, da, qd, kvd)
    dz = jnp.concatenate([dq, dk, dv, dhc, dbg, dcg], axis=1)
    th = _tile(d, 1024)
    d_win = _mm(f"{tag}_dwin", h, dz, mode="tn", grid=(nb, d // th),
                a_spec=pl.BlockSpec((s, th), lambda k, i: (0, i)),
                b_spec=pl.BlockSpec((s, cw), lambda k, i: (0, k)),
                o_spec=pl.BlockSpec((None, th, cw), lambda k, i: (k, i, 0)),
                out_shape=jax.ShapeDtypeStruct((nb, d, cw), BF16))
    started = (red.start(kinds, layer, [d_win, d_wout]),)
    dh = _mm(f"{tag}_dh", dz, win_w, mode="nt", grid=(s // tm, d // td, nb),
             a_spec=pl.BlockSpec((tm, cw), lambda i, j, k: (i, k)),
             b_spec=pl.BlockSpec((None, td, cw), lambda i, j, k: (k, j, 0)),
             o_spec=pl.BlockSpec((tm, td), lambda i, j, k: (i, j)),
             out_shape=jax.ShapeDtypeStruct((s, d), F32), nk=nb, acc_shape=(tm, td), deps=started)
    dx, dg_pre, *next_head = _norm_bwd(f"{tag}_pre_bwd", dh, x, g_pre, 1.0, dx_new, F32, following)
    return dx, d_taps, (dg_pre, dg_a, dg_c, dg_post), tuple(next_head) or None


def kernel(x, ffn1_norm_pre, ffn1_w_gate_up, ffn1_w_down, ffn1_norm_post, mix_norm_pre, w_in, conv_w, attn_out_norm, conv_out_norm, w_out, mix_norm_post, ffn2_norm_pre, ffn2_w_gate_up, ffn2_w_down, ffn2_norm_post, loss_target, m_ffn1_norm_pre, m_ffn1_w_gate_up, m_ffn1_w_down, m_ffn1_norm_post, m_mix_norm_pre, m_w_in, m_conv_w, m_attn_out_norm, m_conv_out_norm, m_w_out, m_mix_norm_post, m_ffn2_norm_pre, m_ffn2_w_gate_up, m_ffn2_w_down, m_ffn2_norm_post, v_ffn1_norm_pre, v_ffn1_w_gate_up, v_ffn1_w_down, v_ffn1_norm_post, v_mix_norm_pre, v_w_in, v_conv_w, v_attn_out_norm, v_conv_out_norm, v_w_out, v_mix_norm_post, v_ffn2_norm_pre, v_ffn2_w_gate_up, v_ffn2_w_down, v_ffn2_norm_post):
    _, s, d = x.shape
    n_layers = ffn1_norm_pre.shape[0]
    qd = attn_out_norm.shape[1]
    cd = conv_out_norm.shape[1]
    kvd = qd // Q_PER_KV
    dims = (qd, kvd, cd)
    assert N_CHIPS * w_in.shape[2] == qd + 2 * kvd + 3 * cd and qd + cd == N_CHIPS * w_out.shape[1]
    assert 2 * d <= SMALL_ROWS * LANES * SUBLANES
    chip = 2 * lax.axis_index("x") + lax.axis_index("y")
    chip_arr = chip.astype(jnp.int32).reshape(1)
    core = lax.axis_index("c").astype(jnp.int32).reshape(1)
    kinds = ("gu1", "dn1", "win", "wout", "gu2", "dn2")

    big = (ffn1_w_gate_up, ffn1_w_down, w_in, w_out, ffn2_w_gate_up, ffn2_w_down)
    nk = len(kinds)
    taps_all = _gather_taps(conv_w)
    feed = _WeightFeed()
    order = [(k, w, layer) for layer in range(n_layers) for k, w in zip(kinds, big)]
    k, w, layer = order[0]
    token = feed.start("gather_start_first", [_cast_into_slot(f"cast_{k}_{layer}", w, layer, chip_arr)], taps_all)
    feed.start("gather_start_rest", [_cast_into_slot(f"cast_{k}_{layer}", w, layer, chip_arr, (token,))
                                     for k, w, layer in order[1:]], token)
    taps = jnp.transpose(taps_all, (1, 2, 0, 3)).reshape(n_layers, CONV_WIDTH, cd)
    taps = jnp.pad(taps, ((0, 0), (0, SUBLANES - CONV_WIDTH), (0, 0)))

    def gain(g, layer):
        return g[layer][None, :]

    xs = x[0]
    hs = _norm_fwd("l0_ffn1_norm", xs, gain(ffn1_norm_pre, 0))
    saved = []
    for layer in range(n_layers):
        t = f"l{layer}"
        k0 = layer * nk
        xs, hs, s1 = _ffn_forward(f"{t}_ffn1", xs, hs, gain(ffn1_norm_post, layer), gain(mix_norm_pre, layer), feed, k0)
        mix_gains = (gain(mix_norm_pre, layer), gain(attn_out_norm, layer), gain(conv_out_norm, layer), gain(mix_norm_post, layer))
        xs, hs, s2 = _mixer_forward(f"{t}_mix", xs, hs, mix_gains, gain(ffn2_norm_pre, layer), feed, k0 + 2,
                                    taps[layer], dims)
        following = gain(ffn1_norm_pre, layer + 1) if layer + 1 < n_layers else None
        xs, hs, s3 = _ffn_forward(f"{t}_ffn2", xs, hs, gain(ffn2_norm_post, layer), following, feed, k0 + 4)
        saved.append((s1, s2, s3, mix_gains))
    wts = {k: [feed.fulls[layer * nk + i] for layer in range(n_layers)] for i, k in enumerate(kinds)}
    for k in ("dn1", "wout", "dn2"):
        wts[k] = [w.reshape(-1, d) for w in wts[k]]
    top = n_layers - 1
    dxs, loss_part, *head = _loss_head("loss_head", xs, loss_target[0],
                                       (saved[top][2][4], gain(ffn2_norm_post, top), FFN_RESIDUAL_WEIGHT))
    loss = lax.psum(jnp.sum(loss_part), ("x", "y", "c"))

    red = _GradReduce(core, chip_arr, n_layers, per_layer=("gu1", "dn1"))
    small = [None] * n_layers
    flow = {"deps": (), "in_flight": None}

    def between(dx, group):
        after = dx
        if flow["in_flight"] is not None:
            after = red.finish(*flow["in_flight"], after)
        flow["deps"] = (red.exchange(*group, after),)
        flow["in_flight"] = group

    head = tuple(head)
    for layer in reversed(range(n_layers)):
        t = f"l{layer}"
        s1, s2, s3, mix_gains = saved[layer]
        after_ffn2 = (s2[7], mix_gains[3], 1.0)
        after_mix = (s1[4], gain(ffn1_norm_post, layer), FFN_RESIDUAL_WEIGHT)
        after_ffn1 = ((saved[layer - 1][2][4], gain(ffn2_norm_post, layer - 1), FFN_RESIDUAL_WEIGHT)
                      if layer > 0 else None)
        dxs, p_pre2, p_post2, head = _ffn_backward(
            f"{t}_ffn2", dxs, s3, gain(ffn2_norm_pre, layer), gain(ffn2_norm_post, layer),
            wts["gu2"][layer], wts["dn2"][layer], red, ("gu2", "dn2"), layer, flow["deps"], head, after_ffn2)
        between(dxs, (("gu2", "dn2"), layer))
        dxs, p_taps, (p_mpre, p_a, p_c, p_mpost), head = _mixer_backward(
            f"{t}_mix", dxs, s2, mix_gains, wts["win"][layer], taps[layer], wts["wout"][layer], dims,
            red, ("win", "wout"), layer, flow["deps"], head, after_mix)
        between(dxs, (("win", "wout"), layer))
        def pack_small(p_pre1, p_post1):
            tap_rows = jnp.zeros((CONV_WIDTH, SUBLANES, d), F32).at[:, 0, :cd].set(p_taps[:CONV_WIDTH])
            rows = [p_pre1, p_post1, p_mpre, jnp.concatenate([p_a, p_c], axis=1), p_mpost, p_pre2, p_post2]
            rows = jnp.concatenate([jnp.stack(rows), tap_rows], axis=0)
            small[layer] = jnp.pad(rows, ((0, SMALL_ROWS - rows.shape[0]), (0, 0), (0, 0)))

        def reduce_small(p_pre1, p_post1):
            pack_small(p_pre1, p_post1)
            flow["small"] = _allreduce_small(jnp.concatenate(small, axis=0))
            return flow["small"]

        dxs, p_pre1, p_post1, head = _ffn_backward(
            f"{t}_ffn1", dxs, s1, gain(ffn1_norm_pre, layer), gain(ffn1_norm_post, layer),
            wts["gu1"][layer], wts["dn1"][layer], red, ("gu1", "dn1"), layer, flow["deps"], head, after_ffn1,
            last=reduce_small if layer == 0 else None)
        if layer > 0:
            pack_small(p_pre1, p_post1)
        between(dxs, (("dn1",) if layer == 0 else ("gu1", "dn1"), layer))
    grad_x = dxs[None]

    weights = dict(ffn1_norm_pre=ffn1_norm_pre, ffn1_w_gate_up=ffn1_w_gate_up, ffn1_w_down=ffn1_w_down, ffn1_norm_post=ffn1_norm_post, mix_norm_pre=mix_norm_pre, w_in=w_in, conv_w=conv_w, attn_out_norm=attn_out_norm, conv_out_norm=conv_out_norm, w_out=w_out, mix_norm_post=mix_norm_post, ffn2_norm_pre=ffn2_norm_pre, ffn2_w_gate_up=ffn2_w_gate_up, ffn2_w_down=ffn2_w_down, ffn2_norm_post=ffn2_norm_post)
    m_in = dict(ffn1_norm_pre=m_ffn1_norm_pre, ffn1_w_gate_up=m_ffn1_w_gate_up, ffn1_w_down=m_ffn1_w_down, ffn1_norm_post=m_ffn1_norm_post, mix_norm_pre=m_mix_norm_pre, w_in=m_w_in, conv_w=m_conv_w, attn_out_norm=m_attn_out_norm, conv_out_norm=m_conv_out_norm, w_out=m_w_out, mix_norm_post=m_mix_norm_post, ffn2_norm_pre=m_ffn2_norm_pre, ffn2_w_gate_up=m_ffn2_w_gate_up, ffn2_w_down=m_ffn2_w_down, ffn2_norm_post=m_ffn2_norm_post)
    v_in = dict(ffn1_norm_pre=v_ffn1_norm_pre, ffn1_w_gate_up=v_ffn1_w_gate_up, ffn1_w_down=v_ffn1_w_down, ffn1_norm_post=v_ffn1_norm_post, mix_norm_pre=v_mix_norm_pre, w_in=v_w_in, conv_w=v_conv_w, attn_out_norm=v_attn_out_norm, conv_out_norm=v_conv_out_norm, w_out=v_w_out, mix_norm_post=v_mix_norm_post, ffn2_norm_pre=v_ffn2_norm_pre, ffn2_w_gate_up=v_ffn2_w_gate_up, ffn2_w_down=v_ffn2_w_down, ffn2_norm_post=v_ffn2_norm_post)
    kind_name = dict(gu1="ffn1_w_gate_up", dn1="ffn1_w_down", win="w_in", wout="w_out", gu2="ffn2_w_gate_up", dn2="ffn2_w_down")
    delta, new_m, new_v, grad = {}, {}, {}, {}

    def join_and_update(name, items, deps, after):
        ts, send_sems, recv_sems = _join_start(name, [red.bufs[it] for it in items], deps)
        for a, it in enumerate(items):
            k, layer = it if isinstance(it, tuple) else (it, None)
            n = kind_name[k]
            tag = n if layer is None else f"{n}_{layer}"
            g = _join_wait(f"join_wait_{tag}", ts[a], a, send_sems, recv_sems, after)
            prev = (delta[n], new_m[n], new_v[n], grad[n]) if n in delta else None
            delta[n], new_m[n], new_v[n], grad[n] = _adamw(f"adamw_{tag}", weights[n], g, m_in[n], v_in[n], True,
                                                           layer, prev)
            after = delta[n]
        return after

    early = ("wout", "win", "dn2", "gu2") + tuple((k, layer) for layer in range(1, n_layers) for k in ("dn1", "gu1"))
    last_groups = ((("gu1",), 0), flow["in_flight"])
    done_early = join_and_update("join_early", early, tuple(red.scatter_tokens[g] for g in last_groups), dxs)
    for group in last_groups:
        red.finish(*group, done_early)
    join_and_update("join_late", (("dn1", 0), ("gu1", 0)), (), done_early)

    small_sum = flow["small"].reshape(n_layers, SMALL_ROWS, d)
    g_ffn1_pre, g_ffn1_post, g_mix_pre = small_sum[:, 0], small_sum[:, 1], small_sum[:, 2]
    g_attn_out, g_conv_out = small_sum[:, 3, :qd], small_sum[:, 3, qd:qd + cd]
    g_mix_post, g_ffn2_pre, g_ffn2_post = small_sum[:, 4], small_sum[:, 5], small_sum[:, 6]
    cc = conv_w.shape[2]
    g_conv = lax.dynamic_slice_in_dim(small_sum[:, 7:7 + CONV_WIDTH, :cd], chip * cc, cc, axis=2)

    grad.update(ffn1_norm_pre=g_ffn1_pre, ffn1_norm_post=g_ffn1_post, mix_norm_pre=g_mix_pre, conv_w=g_conv, attn_out_norm=g_attn_out, conv_out_norm=g_conv_out, mix_norm_post=g_mix_post, ffn2_norm_pre=g_ffn2_pre, ffn2_norm_post=g_ffn2_post)
    names = list(weights)

    vectors = [n for n in names if n not in kind_name.values()]

    def pack(tree):
        flat = jnp.concatenate([tree[n].reshape(-1) for n in vectors])
        return jnp.pad(flat, (0, -flat.size % (SUBLANES * LANES))).reshape(-1, LANES)

    packed = _adamw("adamw_small", pack(weights), pack(grad), pack(m_in), pack(v_in))
    offset = 0
    for n in vectors:
        size = weights[n].size
        for tree, flat in zip((delta, new_m, new_v), packed):
            tree[n] = flat.reshape(-1)[offset:offset + size].reshape(weights[n].shape)
        offset += size

    return (loss, grad_x, *[grad[n] for n in names], *[delta[n] for n in names],
            *[new_m[n] for n in names], *[new_v[n] for n in names])
```

```python
import functools

import jax
import jax.numpy as jnp
from jax import lax
from jax.experimental import pallas as pl
from jax.experimental.pallas import tpu as pltpu

F32 = jnp.float32
BF16 = jnp.bfloat16
MESH = pl.DeviceIdType.MESH

NORM_EPS = 1e-6
HEAD_DIM = 128
Q_PER_KV = 4
CONV_WIDTH = 3
FFN_RESIDUAL_WEIGHT = 0.5
DILATED_BRANCHES = ((128, 1), (512, 4), (2048, 16))
ADAM_LR = 0.001
ADAM_B1 = 0.9
ADAM_B2 = 0.999
ADAM_EPS = 1e-08
ADAM_WD = 0.01
ADAM_STEP = 10

N_CHIPS = 4
N_DEV = 8
V7X_VMEM_BYTES = 64 << 20
VMEM_LIMIT = V7X_VMEM_BYTES - (12 << 20)
SUBLANES = 8
LANES = 128
SMALL_ROWS = 16
BIG_BLOCK = 4 << 20


def _params(*sem):
    return pltpu.CompilerParams(dimension_semantics=sem, vmem_limit_bytes=VMEM_LIMIT)


def _row_tile(rows, cols, itemsize=4, budget=2 << 20):
    t = rows
    while t * cols * itemsize > budget and t % 32 == 0:
        t //= 2
    return t


def _sum_to_sublanes(v):
    r, n = v.shape
    return v.reshape(r // SUBLANES, SUBLANES, n).sum(axis=0)


_DIMS = {
    "nn": (((1,), (0,)), ((), ())),
    "nt": (((1,), (1,)), ((), ())),
    "tn": (((0,), (0,)), ((), ())),
}


ANY_SPEC = pl.BlockSpec(memory_space=pl.ANY)


def _dot(a, b, mode):
    return lax.dot_general(a, b, _DIMS[mode], preferred_element_type=F32)


def _mm(name, a, b, *, mode, grid, a_spec, b_spec, o_spec, out_shape, nk=1, acc_shape=None, deps=()):
    nd = len(deps)

    def body(a_ref, b_ref, *rest):
        o_ref, scratch = rest[nd], rest[nd + 1:]
        r = _dot(a_ref[...], b_ref[...], mode)
        if nk == 1:
            o_ref[...] = r.astype(o_ref.dtype)
        else:
            acc = scratch[0]
            k = pl.program_id(len(grid) - 1)

            @pl.when(k == 0)
            def _():
                acc[...] = r

            @pl.when(k > 0)
            def _():
                acc[...] += r

            @pl.when(k == nk - 1)
            def _():
                o_ref[...] = acc[...].astype(o_ref.dtype)

    sem = ("parallel",) * (len(grid) - (1 if nk > 1 else 0)) + (("arbitrary",) if nk > 1 else ())
    return pl.pallas_call(
        body, name=name, grid=grid, in_specs=[a_spec, b_spec] + [ANY_SPEC] * nd, out_specs=o_spec,
        out_shape=out_shape, scratch_shapes=[pltpu.VMEM(acc_shape, F32)] if nk > 1 else [],
        compiler_params=_params(*sem),
    )(a, b, *deps)


def _tile(n, want):
    if n <= want:
        return n
    best = None
    for t in range(LANES, want + 1, LANES):
        if n % t == 0:
            best = t
    assert best is not None, (n, want)
    return best


def _norm_fwd(name, x, gain):
    s, d = x.shape
    tr = _row_tile(s, d, budget=BIG_BLOCK)

    def body(x_ref, g_ref, o_ref):
        xv = x_ref[...]
        r = lax.rsqrt(jnp.mean(xv * xv, axis=-1, keepdims=True) + NORM_EPS)
        o_ref[...] = (xv * r * g_ref[...]).astype(o_ref.dtype)

    return pl.pallas_call(
        body, name=name, grid=(s // tr,),
        in_specs=[pl.BlockSpec((tr, d), lambda i: (i, 0)), pl.BlockSpec((1, d), lambda i: (0, 0))],
        out_specs=pl.BlockSpec((tr, d), lambda i: (i, 0)),
        out_shape=jax.ShapeDtypeStruct((s, d), BF16), compiler_params=_params("parallel"),
    )(x, gain)


def _res_norm(name, x, y, gain, scale, next_gain=None):
    s, d = x.shape
    tr = _row_tile(s, d, budget=BIG_BLOCK)
    with_next = next_gain is not None

    def body(x_ref, y_ref, g_ref, *rest):
        yv = y_ref[...]
        r = lax.rsqrt(jnp.mean(yv * yv, axis=-1, keepdims=True) + NORM_EPS)
        xn = x_ref[...] + scale * (yv * r * g_ref[...])
        if with_next:
            ng_ref, o_ref, h_ref = rest
            rn = lax.rsqrt(jnp.mean(xn * xn, axis=-1, keepdims=True) + NORM_EPS)
            h_ref[...] = (xn * rn * ng_ref[...]).astype(h_ref.dtype)
        else:
            o_ref, = rest
        o_ref[...] = xn

    row = pl.BlockSpec((tr, d), lambda i: (i, 0))
    vec = pl.BlockSpec((1, d), lambda i: (0, 0))
    outs = pl.pallas_call(
        body, name=name, grid=(s // tr,),
        in_specs=[row, row, vec] + ([vec] if with_next else []), out_specs=[row] * (2 if with_next else 1),
        out_shape=[jax.ShapeDtypeStruct((s, d), F32)] + ([jax.ShapeDtypeStruct((s, d), BF16)] if with_next else []),
        compiler_params=_params("parallel"),
    )(x, y, gain, *((next_gain,) if with_next else ()))
    return (outs[0], outs[1]) if with_next else (outs[0], None)


def _rms_bwd(dn, yv, gv):
    r = lax.rsqrt(jnp.mean(yv * yv, axis=-1, keepdims=True) + NORM_EPS)
    xhat = yv * r
    dxn = dn * gv
    return r * (dxn - xhat * jnp.mean(dxn * xhat, axis=-1, keepdims=True)), _sum_to_sublanes(dn * xhat)


def _accumulate(ref, part):
    @pl.when(pl.program_id(0) == 0)
    def _():
        ref[...] = part

    @pl.when(pl.program_id(0) > 0)
    def _():
        ref[...] += part


def _norm_bwd(name, dout, yin, gain, scale, resid, out_dtype, following=None):
    s, d = yin.shape
    tr = _row_tile(s, d)
    has_resid = resid is not None
    chained = following is not None

    def body(*refs):
        refs = list(refs)
        do_ref, y_ref, g_ref = refs[:3]
        del refs[:3]
        r_ref = refs.pop(0) if has_resid else None
        if chained:
            y2_ref, g2_ref = refs[:2]
            del refs[:2]
        di_ref, dg_ref = refs[:2]
        din, part = _rms_bwd(scale * do_ref[...], y_ref[...], g_ref[...])
        _accumulate(dg_ref, part)
        if has_resid:
            din = din + r_ref[...]
        di_ref[...] = din.astype(di_ref.dtype)
        if chained:
            d2_ref, dg2_ref = refs[2:]
            d2, part2 = _rms_bwd(following[2] * din, y2_ref[...], g2_ref[...])
            _accumulate(dg2_ref, part2)
            d2_ref[...] = d2.astype(d2_ref.dtype)

    row = pl.BlockSpec((tr, d), lambda i: (i, 0))
    vec = pl.BlockSpec((1, d), lambda i: (0, 0))
    acc = pl.BlockSpec((SUBLANES, d), lambda i: (0, 0))
    ins = [row, row, vec] + ([row] if has_resid else []) + ([row, vec] if chained else [])
    args = (dout, yin, gain) + ((resid,) if has_resid else ()) + (tuple(following[:2]) if chained else ())
    outs = [row, acc] + ([row, acc] if chained else [])
    shapes = [jax.ShapeDtypeStruct((s, d), out_dtype), jax.ShapeDtypeStruct((SUBLANES, d), F32)]
    if chained:
        shapes += [jax.ShapeDtypeStruct((s, d), BF16), jax.ShapeDtypeStruct((SUBLANES, d), F32)]
    return pl.pallas_call(
        body, name=name, grid=(s // tr,), in_specs=ins, out_specs=outs, out_shape=shapes,
        compiler_params=_params("arbitrary"),
    )(*args)


def _loss_head(name, y, target, following):
    s, d = y.shape
    tr = _row_tile(s, d)
    y2, gain2, scale2 = following

    def body(y_ref, t_ref, y2_ref, g2_ref, dy_ref, l_ref, d2_ref, dg2_ref):
        e = y_ref[...] - t_ref[...]
        dy = e * (1.0 / d)
        dy_ref[...] = dy
        _accumulate(l_ref, _sum_to_sublanes(e * e) * (0.5 / d))
        d2, part2 = _rms_bwd(scale2 * dy, y2_ref[...], g2_ref[...])
        _accumulate(dg2_ref, part2)
        d2_ref[...] = d2.astype(d2_ref.dtype)

    row = pl.BlockSpec((tr, d), lambda i: (i, 0))
    acc = pl.BlockSpec((SUBLANES, d), lambda i: (0, 0))
    return pl.pallas_call(
        body, name=name, grid=(s // tr,), in_specs=[row, row, row, pl.BlockSpec((1, d), lambda i: (0, 0))],
        out_specs=[row, acc, row, acc],
        out_shape=[jax.ShapeDtypeStruct((s, d), F32), jax.ShapeDtypeStruct((SUBLANES, d), F32),
                   jax.ShapeDtypeStruct((s, d), BF16), jax.ShapeDtypeStruct((SUBLANES, d), F32)],
        compiler_params=_params("arbitrary"),
    )(y, target, y2, gain2)


def _ffn_up(name, h, gu_w):
    s, d = h.shape
    nb, _, fs = gu_w.shape
    hb = nb // 2
    w = gu_w.reshape(2, hb, d, fs)
    tm = _tile(s, 512)
    tn = _tile(fs, 1408)
    nj = fs // tn

    def body(h_ref, w_ref, gu_ref, a_ref):
        hv = h_ref[...]
        g = _dot(hv, w_ref[0], "nn")
        u = _dot(hv, w_ref[1], "nn")
        sg = jax.nn.sigmoid(g)
        silu = g * sg
        gu_ref[0] = (u * (sg * (1.0 + g * (1.0 - sg)))).astype(gu_ref.dtype)
        gu_ref[1] = silu.astype(gu_ref.dtype)
        a_ref[...] = (silu * u).astype(a_ref.dtype)

    return pl.pallas_call(
        body, name=name, grid=(hb, nj, s // tm),
        in_specs=[pl.BlockSpec((tm, d), lambda jb, jo, i: (i, 0)),
                  pl.BlockSpec((2, None, d, tn), lambda jb, jo, i: (0, jb, 0, jo))],
        out_specs=[pl.BlockSpec((2, None, tm, tn), lambda jb, jo, i: (0, jb, i, jo)),
                   pl.BlockSpec((tm, tn), lambda jb, jo, i: (i, jb * nj + jo))],
        out_shape=[jax.ShapeDtypeStruct((2, hb, s, fs), BF16), jax.ShapeDtypeStruct((s, hb * fs), BF16)],
        compiler_params=_params("parallel", "parallel", "parallel"),
    )(h, w)


def _ffn_dact(name, dy, dn_w, gu, deps=()):
    s, d = dy.shape
    _, hb, _, fs = gu.shape
    tm = _tile(s, 1024)
    tn = _tile(fs, 1408)
    nj = fs // tn

    def body(dy_ref, w_ref, gu_ref, *rest):
        o_ref = rest[-1]
        wv = w_ref[...]
        parts = max(1, tm // 256)
        for r in range(parts):
            rows = slice(r * (tm // parts), (r + 1) * (tm // parts))
            da = _dot(dy_ref[rows, :], wv, "nt")
            o_ref[0, rows, :] = (da * gu_ref[0, rows, :].astype(F32)).astype(o_ref.dtype)
            o_ref[1, rows, :] = (da * gu_ref[1, rows, :].astype(F32)).astype(o_ref.dtype)

    blk = pl.BlockSpec((2, None, tm, tn), lambda jb, jo, i: (0, jb, i, jo))
    return pl.pallas_call(
        body, name=name, grid=(hb, nj, s // tm),
        in_specs=[pl.BlockSpec((tm, d), lambda jb, jo, i: (i, 0)),
                  pl.BlockSpec((tn, d), lambda jb, jo, i: (jb * nj + jo, 0)),
                  blk] + [ANY_SPEC] * len(deps),
        out_specs=blk, out_shape=jax.ShapeDtypeStruct(gu.shape, BF16),
        compiler_params=_params("parallel", "parallel", "parallel"),
    )(dy, dn_w, gu, *deps)


_MASKED = -1e30


def _attn_bias(s, tq):
    nd = s // tq
    dist = (jnp.arange(nd)[:, None, None] * tq + jnp.arange(tq)[None, :, None]) - jnp.arange(tq)[None, None, :]
    mult = jnp.zeros(dist.shape, F32)
    for window, dilation in DILATED_BRANCHES:
        mult = mult + ((dist >= 0) & (dist <= window) & (dist % dilation == 0)).astype(F32)
    return jnp.where(mult > 0.0, jnp.log(jnp.maximum(mult, 1.0)), _MASKED)


def _biased(sc, bias, scale):
    tq, tk = bias.shape
    return (sc.reshape(-1, tq, tk) * scale + bias[None]).reshape(sc.shape)


def _attn_specs(s, qd, kvd, tq):
    rw = Q_PER_KV * HEAD_DIM
    qspec = pl.BlockSpec((tq, rw), lambda g, i: (i, g))
    kspec = pl.BlockSpec((s, HEAD_DIM), lambda g, i: (0, qd // HEAD_DIM + g))
    vspec = pl.BlockSpec((s, HEAD_DIM), lambda g, i: (0, (qd + kvd) // HEAD_DIM + g))
    return rw, qspec, kspec, vspec


def _attn_fwd(name, z, qd, kvd):
    s = z.shape[0]
    tq = _tile(s, 256)
    nkv = kvd // HEAD_DIM
    rw, qspec, kspec, vspec = _attn_specs(s, qd, kvd, tq)
    scale = HEAD_DIM ** -0.5

    def body(q_ref, k_ref, v_ref, b_ref, o_ref, l_ref):
        i = pl.program_id(1)
        heads = [slice(h * HEAD_DIM, (h + 1) * HEAD_DIM) for h in range(Q_PER_KV)]
        q_all = jnp.concatenate([q_ref[:, cols] for cols in heads], axis=0)

        def chunk(j, carry):
            mx, den, acc = carry
            k0 = pl.multiple_of(j * tq, tq)
            kc, vc = k_ref[pl.ds(k0, tq), :], v_ref[pl.ds(k0, tq), :]
            sc = _biased(_dot(q_all, kc, "nt"), b_ref[i - j], scale)
            mx_new = jnp.maximum(mx, jnp.max(sc, axis=-1, keepdims=True))
            alpha = jnp.exp(mx - mx_new)
            p = jnp.exp(sc - mx_new)
            return (mx_new, alpha * den + jnp.sum(p, axis=-1, keepdims=True),
                    alpha * acc + _dot(p.astype(BF16), vc, "nn"))

        rows = Q_PER_KV * tq
        init = (jnp.full((rows, 1), _MASKED, F32), jnp.zeros((rows, 1), F32), jnp.zeros((rows, HEAD_DIM), F32))
        mx, den, acc = lax.fori_loop(0, i + 1, chunk, init)
        out = acc / den
        lse = mx + jnp.log(den)
        for h, cols in enumerate(heads):
            o_ref[:, cols] = out[h * tq:(h + 1) * tq]
            l_ref[:, cols] = jnp.broadcast_to(lse[h * tq:(h + 1) * tq], (tq, HEAD_DIM))

    bias = _attn_bias(s, tq)
    return pl.pallas_call(
        body, name=name, grid=(nkv, s // tq),
        in_specs=[qspec, kspec, vspec, pl.BlockSpec(bias.shape, lambda g, i: (0, 0, 0))], out_specs=[qspec, qspec],
        out_shape=[jax.ShapeDtypeStruct((s, qd), F32), jax.ShapeDtypeStruct((s, qd), F32)],
        compiler_params=_params("parallel", "parallel"),
    )(z, z, z, bias)


def _attn_bwd(name, z, o, lse, do, qd, kvd):
    s = z.shape[0]
    tq = _tile(s, 256)
    nkv = kvd // HEAD_DIM
    nq = s // tq
    rw, qspec, kspec, vspec = _attn_specs(s, qd, kvd, tq)
    scale = HEAD_DIM ** -0.5

    def body(q_ref, k_ref, v_ref, o_ref, l_ref, do_ref, b_ref, dq_ref, dk_ref, dv_ref, dk_acc, dv_acc):
        i = pl.program_id(1)
        heads = [slice(h * HEAD_DIM, (h + 1) * HEAD_DIM) for h in range(Q_PER_KV)]

        @pl.when(i == 0)
        def _():
            dk_acc[...] = jnp.zeros_like(dk_acc)
            dv_acc[...] = jnp.zeros_like(dv_acc)

        q_all = jnp.concatenate([q_ref[:, cols] for cols in heads], axis=0)
        do_all = jnp.concatenate([do_ref[:, cols].astype(BF16) for cols in heads], axis=0)
        lse_all = jnp.concatenate([l_ref[:, cols][:, :1] for cols in heads], axis=0)
        delta_all = jnp.concatenate(
            [jnp.sum(do_ref[:, cols] * o_ref[:, cols], axis=-1, keepdims=True) for cols in heads], axis=0)

        def chunk(j, dq):
            k0 = pl.multiple_of(j * tq, tq)
            kc, vc = k_ref[pl.ds(k0, tq), :], v_ref[pl.ds(k0, tq), :]
            p = jnp.exp(_biased(_dot(q_all, kc, "nt"), b_ref[i - j], scale) - lse_all)
            ds = (p * (_dot(do_all, vc, "nt") - delta_all) * scale).astype(BF16)
            dk_acc[pl.ds(k0, tq), :] += _dot(ds, q_all, "tn")
            dv_acc[pl.ds(k0, tq), :] += _dot(p.astype(BF16), do_all, "tn")
            return dq + _dot(ds, kc, "nn")

        dq = lax.fori_loop(0, i + 1, chunk, jnp.zeros((Q_PER_KV * tq, HEAD_DIM), F32))
        for h, cols in enumerate(heads):
            dq_ref[:, cols] = dq[h * tq:(h + 1) * tq].astype(dq_ref.dtype)

        @pl.when(i == nq - 1)
        def _():
            dk_ref[...] = dk_acc[...].astype(dk_ref.dtype)
            dv_ref[...] = dv_acc[...].astype(dv_ref.dtype)

    kvout = pl.BlockSpec((s, HEAD_DIM), lambda g, i: (0, g))
    bias = _attn_bias(s, tq)
    return pl.pallas_call(
        body, name=name, grid=(nkv, nq),
        in_specs=[qspec, kspec, vspec, qspec, qspec, qspec, pl.BlockSpec(bias.shape, lambda g, i: (0, 0, 0))],
        out_specs=[qspec, kvout, kvout],
        out_shape=[jax.ShapeDtypeStruct((s, qd), BF16), jax.ShapeDtypeStruct((s, kvd), BF16),
                   jax.ShapeDtypeStruct((s, kvd), BF16)],
        scratch_shapes=[pltpu.VMEM((s, HEAD_DIM), F32), pltpu.VMEM((s, HEAD_DIM), F32)],
        compiler_params=_params("parallel", "arbitrary"),
    )(z, z, z, o, lse, do, bias)


def _shift_down(v, n):
    rolled = pltpu.roll(v, n, 0)
    t = lax.broadcasted_iota(jnp.int32, v.shape, 0)
    return jnp.where(t >= n, rolled, 0.0)


def _shift_up(v, n):
    rows = v.shape[0]
    rolled = pltpu.roll(v, rows - n, 0)
    t = lax.broadcasted_iota(jnp.int32, v.shape, 0)
    return jnp.where(t < rows - n, rolled, 0.0)


def _conv_specs(s, base, cd, tc):
    zs = [pl.BlockSpec((s, tc), functools.partial(lambda j, off: (0, off + j), off=(base + n * cd) // tc))
          for n in range(3)]
    wspec = pl.BlockSpec((SUBLANES, tc), lambda j: (0, j))
    cspec = pl.BlockSpec((s, tc), lambda j: (0, j))
    return zs, wspec, cspec


def _conv_fwd(name, z, conv_w, base, cd):
    s = z.shape[0]
    tc = _tile(cd, 256)
    zs, wspec, cspec = _conv_specs(s, base, cd, tc)

    def body(h_ref, b_ref, c_ref, w_ref, o_ref):
        u = c_ref[...].astype(F32) * h_ref[...].astype(F32)
        y = w_ref[0:1, :] * _shift_down(u, 2) + w_ref[1:2, :] * _shift_down(u, 1) + w_ref[2:3, :] * u
        o_ref[...] = b_ref[...].astype(F32) * y

    return pl.pallas_call(
        body, name=name, grid=(cd // tc,), in_specs=zs + [wspec], out_specs=cspec,
        out_shape=jax.ShapeDtypeStruct((s, cd), F32), compiler_params=_params("parallel"),
    )(z, z, z, conv_w)


def _conv_bwd(name, z, conv_w, dc, base, cd):
    s = z.shape[0]
    tc = _tile(cd, 256)
    zs, wspec, cspec = _conv_specs(s, base, cd, tc)

    def body(h_ref, b_ref, c_ref, w_ref, dc_ref, dh_ref, db_ref, dcg_ref, dw_ref):
        hv, bv, cv = h_ref[...].astype(F32), b_ref[...].astype(F32), c_ref[...].astype(F32)
        u = cv * hv
        u1, u2 = _shift_down(u, 1), _shift_down(u, 2)
        w0, w1, w2 = w_ref[0:1, :], w_ref[1:2, :], w_ref[2:3, :]
        y = w0 * u2 + w1 * u1 + w2 * u
        dcv = dc_ref[...]
        db_ref[...] = (dcv * y).astype(db_ref.dtype)
        dy = dcv * bv
        du = w2 * dy + w1 * _shift_up(dy, 1) + w0 * _shift_up(dy, 2)
        dh_ref[...] = (du * cv).astype(dh_ref.dtype)
        dcg_ref[...] = (du * hv).astype(dcg_ref.dtype)
        g0 = jnp.sum(dy * u2, axis=0, keepdims=True)
        g1 = jnp.sum(dy * u1, axis=0, keepdims=True)
        g2 = jnp.sum(dy * u, axis=0, keepdims=True)
        r = lax.broadcasted_iota(jnp.int32, (SUBLANES, tc), 0)
        dw_ref[...] = jnp.where(r == 0, g0, jnp.where(r == 1, g1, jnp.where(r == 2, g2, 0.0)))

    return pl.pallas_call(
        body, name=name, grid=(cd // tc,), in_specs=zs + [wspec, cspec],
        out_specs=[cspec, cspec, cspec, wspec],
        out_shape=[jax.ShapeDtypeStruct((s, cd), BF16)] * 3 + [jax.ShapeDtypeStruct((SUBLANES, cd), F32)],
        compiler_params=_params("parallel"),
    )(z, z, z, conv_w, dc)


def _cat_norm_fwd(name, a, c, ga, gc):
    s, qd = a.shape
    cd = c.shape[1]
    tr = _row_tile(s, qd + cd)

    def body(a_ref, c_ref, ga_ref, gc_ref, o_ref):
        av, cv = a_ref[...], c_ref[...]
        ra = lax.rsqrt(jnp.mean(av * av, axis=-1, keepdims=True) + NORM_EPS)
        rc = lax.rsqrt(jnp.mean(cv * cv, axis=-1, keepdims=True) + NORM_EPS)
        o_ref[:, :qd] = (av * ra * ga_ref[...]).astype(o_ref.dtype)
        o_ref[:, qd:] = (cv * rc * gc_ref[...]).astype(o_ref.dtype)

    return pl.pallas_call(
        body, name=name, grid=(s // tr,),
        in_specs=[pl.BlockSpec((tr, qd), lambda i: (i, 0)), pl.BlockSpec((tr, cd), lambda i: (i, 0)),
                  pl.BlockSpec((1, qd), lambda i: (0, 0)), pl.BlockSpec((1, cd), lambda i: (0, 0))],
        out_specs=pl.BlockSpec((tr, qd + cd), lambda i: (i, 0)),
        out_shape=jax.ShapeDtypeStruct((s, qd + cd), BF16), compiler_params=_params("parallel"),
    )(a, c, ga, gc)


def _cat_norm_bwd(name, dcat, a, c, ga, gc):
    s, qd = a.shape
    cd = c.shape[1]
    tr = _row_tile(s, qd + cd)

    def one(dn, yv, gv):
        r = lax.rsqrt(jnp.mean(yv * yv, axis=-1, keepdims=True) + NORM_EPS)
        xhat = yv * r
        dxn = dn * gv
        return r * (dxn - xhat * jnp.mean(dxn * xhat, axis=-1, keepdims=True)), _sum_to_sublanes(dn * xhat)

    def body(d_ref, a_ref, c_ref, ga_ref, gc_ref, da_ref, dc_ref, dga_ref, dgc_ref):
        da, pa = one(d_ref[:, :qd], a_ref[...], ga_ref[...])
        dc, pc = one(d_ref[:, qd:], c_ref[...], gc_ref[...])
        da_ref[...] = da
        dc_ref[...] = dc

        @pl.when(pl.program_id(0) == 0)
        def _():
            dga_ref[...] = pa
            dgc_ref[...] = pc

        @pl.when(pl.program_id(0) > 0)
        def _():
            dga_ref[...] += pa
            dgc_ref[...] += pc

    ra = pl.BlockSpec((tr, qd), lambda i: (i, 0))
    rc = pl.BlockSpec((tr, cd), lambda i: (i, 0))
    return pl.pallas_call(
        body, name=name, grid=(s // tr,),
        in_specs=[pl.BlockSpec((tr, qd + cd), lambda i: (i, 0)), ra, rc,
                  pl.BlockSpec((1, qd), lambda i: (0, 0)), pl.BlockSpec((1, cd), lambda i: (0, 0))],
        out_specs=[ra, rc, pl.BlockSpec((SUBLANES, qd), lambda i: (0, 0)),
                   pl.BlockSpec((SUBLANES, cd), lambda i: (0, 0))],
        out_shape=[jax.ShapeDtypeStruct((s, qd), F32), jax.ShapeDtypeStruct((s, cd), F32),
                   jax.ShapeDtypeStruct((SUBLANES, qd), F32), jax.ShapeDtypeStruct((SUBLANES, cd), F32)],
        compiler_params=_params("arbitrary"),
    )(dcat, a, c, ga, gc)


def _adamw(name, w, g, m, v, emit_grad=False, layer=None, prev=None):
    shape = w.shape
    cols = shape[-1]
    rows = g.size // cols
    tr = _row_tile(rows, cols, budget=3 << 19)
    first = 0 if layer is None else layer * (rows // tr)
    bc1 = 1.0 - ADAM_B1 ** ADAM_STEP
    bc2 = 1.0 - ADAM_B2 ** ADAM_STEP
    n_out = 4 if emit_grad else 3

    def body(w_ref, g_ref, m_ref, v_ref, *rest):
        d_ref, nm_ref, nv_ref = rest[-n_out:][:3]
        gv = g_ref[...]
        mv = ADAM_B1 * m_ref[...] + (1.0 - ADAM_B1) * gv
        vv = ADAM_B2 * v_ref[...] + (1.0 - ADAM_B2) * (gv * gv)
        nm_ref[...] = mv
        nv_ref[...] = vv
        d_ref[...] = -ADAM_LR * ((mv / bc1) / (jnp.sqrt(vv / bc2) + ADAM_EPS) + ADAM_WD * w_ref[...])
        if emit_grad:
            rest[-1][...] = gv

    row = pl.BlockSpec((tr, cols), lambda i: (first + i, 0))
    g_row = pl.BlockSpec((tr, cols), lambda i: (i, 0))
    prev = tuple(prev) if prev is not None else ()
    total = w.size // cols
    outs = pl.pallas_call(
        body, name=name, grid=(rows // tr,), in_specs=[row, g_row, row, row] + [ANY_SPEC] * len(prev),
        out_specs=[row] * n_out, out_shape=[jax.ShapeDtypeStruct((total, cols), F32)] * n_out,
        input_output_aliases={4 + i: i for i in range(len(prev))}, compiler_params=_params("parallel"),
    )(w.reshape(total, cols), g.reshape(rows, cols), m.reshape(total, cols), v.reshape(total, cols),
      *(t.reshape(total, cols) for t in prev))
    return tuple(t.reshape(shape) for t in outs)


HBM_SPEC = pl.BlockSpec(memory_space=pltpu.HBM)


def _mesh_place():
    x, y, c = lax.axis_index("x"), lax.axis_index("y"), lax.axis_index("c")
    other_chips = [(1 - x, y), (x, 1 - y), (1 - x, 1 - y)]
    return x, y, c, other_chips


def _cast_into_slot(name, w, layer, chip, deps=()):
    _, r, cols = w.shape
    tr = _row_tile(r, cols, budget=BIG_BLOCK)

    def body(chip_ref, w_ref, *rest):
        o_ref = rest[-1]
        o_ref[...] = w_ref[...].astype(o_ref.dtype)

    return pl.pallas_call(
        body, name=name,
        grid_spec=pltpu.PrefetchScalarGridSpec(
            num_scalar_prefetch=1, grid=(r // tr,),
            in_specs=[pl.BlockSpec((None, tr, cols), lambda i, chip_ref: (layer, i, 0))] + [ANY_SPEC] * len(deps),
            out_specs=pl.BlockSpec((None, tr, cols), lambda i, chip_ref: (chip_ref[0], i, 0))),
        out_shape=jax.ShapeDtypeStruct((N_CHIPS, r, cols), BF16), compiler_params=_params("parallel"),
    )(chip, w, *deps)


SEM_SPEC = pl.BlockSpec(memory_space=pltpu.SEMAPHORE)
SPLIT_COPY = pltpu.CompilerParams(has_side_effects=pltpu.SideEffectType.DATAFLOW_SIDE_EFFECTING)
N_OTHER = N_CHIPS - 1
TOKEN_SPEC = pl.BlockSpec(memory_space=pltpu.VMEM)
TOKEN_SHAPE = jax.ShapeDtypeStruct((SUBLANES, LANES), F32)


def _in_hbm(arr):
    return pltpu.with_memory_space_constraint(arr, pltpu.HBM)


def _half_rows(ref, chip_idx, core):
    r2 = ref.shape[1] // 2
    return ref.at[chip_idx, pl.ds(core * r2, r2), :]


def _gather_start(name, fulls, after):
    na = len(fulls)

    def body(*refs):
        f_refs = refs[na + 1:2 * na + 1]
        send_sems, recv_sems = refs[2 * na + 1:3 * na + 1], refs[3 * na + 1:4 * na + 1]
        token = refs[4 * na + 1]
        x, y, c, chips = _mesh_place()
        for a in range(na):
            mine = _half_rows(f_refs[a], 2 * x + y, c)
            for j, (cx, cy) in enumerate(chips):
                pltpu.make_async_remote_copy(
                    src_ref=mine, dst_ref=mine, send_sem=send_sems[a].at[j], recv_sem=recv_sems[a].at[j],
                    device_id=(cx, cy, c), device_id_type=MESH).start()
        token[...] = jnp.zeros_like(token)

    outs = pl.pallas_call(
        body, name=name, in_specs=[HBM_SPEC] * na + [ANY_SPEC],
        out_specs=[HBM_SPEC] * na + [SEM_SPEC] * (2 * na) + [TOKEN_SPEC],
        out_shape=[pltpu.HBM(f.shape, f.dtype) for f in fulls] + [pltpu.SemaphoreType.DMA((N_OTHER,))] * (2 * na)
        + [TOKEN_SHAPE],
        input_output_aliases={a: a for a in range(na)}, compiler_params=SPLIT_COPY,
    )(*[_in_hbm(f) for f in fulls], after)
    return list(outs[:na]), list(outs[na:2 * na]), list(outs[2 * na:3 * na]), outs[3 * na]


def _gather_pass_on(name, full, recv_sems, after):
    def body(f_in, recv_sems, after_ref, f_ref, d2d_send, d2d_recv):
        x, y, c, chips = _mesh_place()
        for j, (cx, cy) in enumerate(chips):
            blk = _half_rows(f_ref, 2 * cx + cy, c)
            pltpu.make_async_remote_copy(
                src_ref=blk, dst_ref=blk, send_sem=d2d_send.at[j], recv_sem=recv_sems.at[j],
                device_id=(cx, cy, c), device_id_type=MESH).wait_recv()
            pltpu.make_async_remote_copy(
                src_ref=blk, dst_ref=blk, send_sem=d2d_send.at[j], recv_sem=d2d_recv.at[j],
                device_id=(x, y, 1 - c), device_id_type=MESH).start()

    return pl.pallas_call(
        body, name=name, in_specs=[HBM_SPEC, SEM_SPEC, ANY_SPEC], out_specs=[HBM_SPEC, SEM_SPEC, SEM_SPEC],
        out_shape=[pltpu.HBM(full.shape, full.dtype)] + [pltpu.SemaphoreType.DMA((N_OTHER,))] * 2,
        input_output_aliases={0: 0}, compiler_params=SPLIT_COPY,
    )(full, recv_sems, after)


def _gather_arrive(name, full, ici_send, d2d_send, d2d_recv, after):
    def body(f_in, ici_send, d2d_send, d2d_recv, after_ref, f_ref):
        x, y, c, chips = _mesh_place()
        for j, (cx, cy) in enumerate(chips):
            mine = _half_rows(f_ref, 2 * x + y, c)
            passed = _half_rows(f_ref, 2 * cx + cy, c)
            theirs = _half_rows(f_ref, 2 * cx + cy, 1 - c)
            pltpu.make_async_remote_copy(
                src_ref=mine, dst_ref=mine, send_sem=ici_send.at[j], recv_sem=d2d_recv.at[j],
                device_id=(cx, cy, c), device_id_type=MESH).wait_send()
            pltpu.make_async_remote_copy(
                src_ref=passed, dst_ref=passed, send_sem=d2d_send.at[j], recv_sem=d2d_recv.at[j],
                device_id=(x, y, 1 - c), device_id_type=MESH).wait_send()
            pltpu.make_async_remote_copy(
                src_ref=theirs, dst_ref=theirs, send_sem=d2d_send.at[j], recv_sem=d2d_recv.at[j],
                device_id=(x, y, 1 - c), device_id_type=MESH).wait_recv()

    return pl.pallas_call(
        body, name=name, in_specs=[HBM_SPEC, SEM_SPEC, SEM_SPEC, SEM_SPEC, ANY_SPEC], out_specs=HBM_SPEC,
        out_shape=pltpu.HBM(full.shape, full.dtype), input_output_aliases={0: 0}, compiler_params=SPLIT_COPY,
    )(full, ici_send, d2d_send, d2d_recv, after)


def _gather_taps(conv_w):
    def body(cw_ref, cwf_ref, send_sems, recv_sems, local_sem):
        x, y, c, chips = _mesh_place()
        k_me = 2 * x + y
        local = pltpu.make_async_copy(cw_ref, cwf_ref.at[k_me], local_sem)
        local.start()
        copies = [pltpu.make_async_remote_copy(
            src_ref=cw_ref, dst_ref=cwf_ref.at[k_me], send_sem=send_sems.at[j], recv_sem=recv_sems.at[j],
            device_id=(cx, cy, c), device_id_type=MESH) for j, (cx, cy) in enumerate(chips)]
        for cp in copies:
            cp.start()
        for j, (cx, cy) in enumerate(chips):
            pltpu.make_async_remote_copy(
                src_ref=cw_ref, dst_ref=cwf_ref.at[2 * cx + cy], send_sem=send_sems.at[j], recv_sem=recv_sems.at[j],
                device_id=(cx, cy, c), device_id_type=MESH).wait_recv()
        for cp in copies:
            cp.wait_send()
        local.wait()

    return pl.pallas_call(
        body, name="gather_taps", in_specs=[HBM_SPEC], out_specs=HBM_SPEC,
        out_shape=jax.ShapeDtypeStruct((N_CHIPS,) + conv_w.shape, conv_w.dtype),
        scratch_shapes=[pltpu.SemaphoreType.DMA((N_OTHER,))] * 2 + [pltpu.SemaphoreType.DMA],
    )(conv_w)


def _sibling_half(g_ref, c):
    r2 = g_ref.shape[1] // 2
    return g_ref.at[:, pl.ds((1 - c) * r2, r2), :]


def _swap_copy(g_ref, land_ref, send_sems, recv_sems, a):
    x, y, c, _ = _mesh_place()
    return pltpu.make_async_remote_copy(
        src_ref=_sibling_half(g_ref, c), dst_ref=land_ref, send_sem=send_sems.at[a], recv_sem=recv_sems.at[a],
        device_id=(x, y, 1 - c), device_id_type=MESH)


def _swap_start(name, gs):
    n = len(gs)

    def body(*refs):
        g_refs, land_refs = refs[n:2 * n], refs[2 * n:3 * n]
        send_sems, recv_sems, token = refs[3 * n:]
        for a in range(n):
            _swap_copy(g_refs[a], land_refs[a], send_sems, recv_sems, a).start()
        token[...] = jnp.zeros_like(token)

    outs = pl.pallas_call(
        body, name=name, in_specs=[HBM_SPEC] * n,
        out_specs=[HBM_SPEC] * (2 * n) + [SEM_SPEC, SEM_SPEC, TOKEN_SPEC],
        out_shape=[pltpu.HBM(g.shape, g.dtype) for g in gs]
        + [pltpu.HBM((g.shape[0], g.shape[1] // 2, g.shape[2]), g.dtype) for g in gs]
        + [pltpu.SemaphoreType.DMA((n,)), pltpu.SemaphoreType.DMA((n,)), TOKEN_SHAPE],
        input_output_aliases={a: a for a in range(n)}, compiler_params=SPLIT_COPY,
    )(*[_in_hbm(g) for g in gs])
    return list(outs[:n]), list(outs[n:2 * n]), outs[2 * n], outs[2 * n + 1], outs[2 * n + 2]


def _swap_wait(name, gs, lands, send_sems, recv_sems, after):
    n = len(gs)

    def body(*refs):
        send_sems, recv_sems = refs[2 * n], refs[2 * n + 1]
        g_refs, land_refs = refs[2 * n + 3:3 * n + 3], refs[3 * n + 3:]
        for a in range(n):
            copy = _swap_copy(g_refs[a], land_refs[a], send_sems, recv_sems, a)
            copy.wait_send()
            copy.wait_recv()

    outs = pl.pallas_call(
        body, name=name, in_specs=[HBM_SPEC] * (2 * n) + [SEM_SPEC, SEM_SPEC, ANY_SPEC],
        out_specs=[HBM_SPEC] * (2 * n),
        out_shape=[pltpu.HBM(t.shape, t.dtype) for t in list(gs) + list(lands)],
        input_output_aliases={a: a for a in range(2 * n)}, compiler_params=SPLIT_COPY,
    )(*gs, *lands, send_sems, recv_sems, after)
    return list(outs[:n]), list(outs[n:])


def _add_core_halves(name, g, sib, core):
    nb, r, cols = g.shape
    r2 = r // 2
    tr = _row_tile(r2, cols, itemsize=2, budget=BIG_BLOCK)
    nrt = r2 // tr

    def body(core_ref, g_ref, s_ref, o_ref):
        o_ref[...] = (g_ref[...].astype(F32) + s_ref[...].astype(F32)).astype(o_ref.dtype)

    return pl.pallas_call(
        body, name=name,
        grid_spec=pltpu.PrefetchScalarGridSpec(
            num_scalar_prefetch=1, grid=(nb, nrt),
            in_specs=[pl.BlockSpec((None, tr, cols), lambda k, i, core_ref: (k, core_ref[0] * nrt + i, 0)),
                      pl.BlockSpec((None, tr, cols), lambda k, i, core_ref: (k, i, 0))],
            out_specs=pl.BlockSpec((None, tr, cols), lambda k, i, core_ref: (k, i, 0))),
        out_shape=jax.ShapeDtypeStruct((nb, r2, cols), BF16), compiler_params=_params("parallel", "parallel"),
    )(core, g, sib)


def _scatter_copies(h_refs, land_refs, send_sems, recv_sems):
    x, y, c, chips = _mesh_place()
    return [pltpu.make_async_remote_copy(
        src_ref=h_ref.at[2 * cx + cy], dst_ref=land_ref.at[j],
        send_sem=send_sems.at[a * N_OTHER + j], recv_sem=recv_sems.at[a * N_OTHER + j],
        device_id=(cx, cy, c), device_id_type=MESH)
        for a, (h_ref, land_ref) in enumerate(zip(h_refs, land_refs)) for j, (cx, cy) in enumerate(chips)]


def _scatter_start(name, hs):
    n = len(hs)

    def body(*refs):
        h_refs, land_refs = refs[n:2 * n], refs[2 * n:3 * n]
        send_sems, recv_sems, token = refs[3 * n:]
        for copy in _scatter_copies(h_refs, land_refs, send_sems, recv_sems):
            copy.start()
        token[...] = jnp.zeros_like(token)

    outs = pl.pallas_call(
        body, name=name, in_specs=[HBM_SPEC] * n,
        out_specs=[HBM_SPEC] * (2 * n) + [SEM_SPEC, SEM_SPEC, TOKEN_SPEC],
        out_shape=[pltpu.HBM(h.shape, h.dtype) for h in hs]
        + [pltpu.HBM((N_OTHER,) + h.shape[1:], h.dtype) for h in hs]
        + [pltpu.SemaphoreType.DMA((n * N_OTHER,)), pltpu.SemaphoreType.DMA((n * N_OTHER,)), TOKEN_SHAPE],
        input_output_aliases={a: a for a in range(n)}, compiler_params=SPLIT_COPY,
    )(*[_in_hbm(h) for h in hs])
    return list(outs[:n]), list(outs[n:2 * n]), outs[2 * n], outs[2 * n + 1], outs[2 * n + 2]


def _scatter_wait(name, hs, lands, send_sems, recv_sems, after):
    afters = tuple(after) if isinstance(after, (tuple, list)) else (after,)
    n = len(hs)

    def body(*refs):
        send_sems, recv_sems = refs[2 * n], refs[2 * n + 1]
        h_refs, land_refs = refs[-2 * n:-n], refs[-n:]
        for copy in _scatter_copies(h_refs, land_refs, send_sems, recv_sems):
            copy.wait_send()
            copy.wait_recv()

    outs = pl.pallas_call(
        body, name=name, in_specs=[HBM_SPEC] * (2 * n) + [SEM_SPEC, SEM_SPEC] + [ANY_SPEC] * len(afters),
        out_specs=[HBM_SPEC] * (2 * n),
        out_shape=[pltpu.HBM(t.shape, t.dtype) for t in list(hs) + list(lands)],
        input_output_aliases={a: a for a in range(2 * n)}, compiler_params=SPLIT_COPY,
    )(*hs, *lands, send_sems, recv_sems, *afters)
    return list(outs[:n]), list(outs[n:])


def _sum_chips(name, hs, rcv, core, chip, layer, n_layers, prev):
    _, r2, cols = hs.shape
    tr = _row_tile(r2, cols, budget=BIG_BLOCK)
    nrt = r2 // tr

    def body(core_ref, chip_ref, h_ref, r_ref, *rest):
        o_ref = rest[-1]
        acc = h_ref[...].astype(F32)
        for j in range(N_CHIPS - 1):
            acc = acc + r_ref[j].astype(F32)
        o_ref[...] = acc

    in_specs = [pl.BlockSpec((None, tr, cols), lambda i, core_ref, chip_ref: (chip_ref[0], i, 0)),
                pl.BlockSpec((N_CHIPS - 1, tr, cols), lambda i, core_ref, chip_ref: (0, i, 0))]
    args = [core, chip, hs, rcv]
    aliases = {}
    if prev is not None:
        in_specs.append(pl.BlockSpec(memory_space=pl.ANY))
        args.append(prev)
        aliases = {4: 0}
    return pl.pallas_call(
        body, name=name,
        grid_spec=pltpu.PrefetchScalarGridSpec(
            num_scalar_prefetch=2, grid=(nrt,), in_specs=in_specs,
            out_specs=pl.BlockSpec((None, tr, cols), lambda i, core_ref, chip_ref: (layer, core_ref[0] * nrt + i, 0))),
        out_shape=jax.ShapeDtypeStruct((n_layers, 2 * r2, cols), F32), input_output_aliases=aliases,
        compiler_params=_params("parallel"),
    )(*args)


def _join_copy(t_ref, send_sems, recv_sems, a):
    x, y, c, _ = _mesh_place()
    r2 = t_ref.shape[1] // 2
    mine = t_ref.at[:, pl.ds(c * r2, r2), :]
    return pltpu.make_async_remote_copy(
        src_ref=mine, dst_ref=mine, send_sem=send_sems.at[a], recv_sem=recv_sems.at[a],
        device_id=(x, y, 1 - c), device_id_type=MESH)


def _join_start(name, ts, deps=()):
    n, nd = len(ts), len(deps)

    def body(*refs):
        t_refs = refs[n + nd:2 * n + nd]
        send_sems, recv_sems = refs[2 * n + nd:]
        for a in range(n):
            _join_copy(t_refs[a], send_sems, recv_sems, a).start()

    outs = pl.pallas_call(
        body, name=name, in_specs=[HBM_SPEC] * n + [ANY_SPEC] * nd, out_specs=[HBM_SPEC] * n + [SEM_SPEC, SEM_SPEC],
        out_shape=[pltpu.HBM(t.shape, t.dtype) for t in ts] + [pltpu.SemaphoreType.DMA((n,))] * 2,
        input_output_aliases={a: a for a in range(n)}, compiler_params=SPLIT_COPY,
    )(*[_in_hbm(t) for t in ts], *deps)
    return list(outs[:n]), outs[n], outs[n + 1]


def _join_wait(name, t, a, send_sems, recv_sems, after):
    def body(t_in, send_sems, recv_sems, after_ref, t_ref):
        copy = _join_copy(t_ref, send_sems, recv_sems, a)
        copy.wait_send()
        copy.wait_recv()

    return pl.pallas_call(
        body, name=name, in_specs=[HBM_SPEC, SEM_SPEC, SEM_SPEC, ANY_SPEC], out_specs=HBM_SPEC,
        out_shape=pltpu.HBM(t.shape, t.dtype), input_output_aliases={0: 0}, compiler_params=SPLIT_COPY,
    )(t, send_sems, recv_sems, after)


def _allreduce_small(p):
    n, _, w = p.shape

    def body(p_ref, o_ref, buf, send_sems, recv_sems):
        x, y, c, _ = _mesh_place()
        me = 4 * x + 2 * y + c
        buf[me] = jnp.sum(p_ref[...], axis=1)
        copies = []
        for pat in range(1, N_DEV):
            fx, fy, fc = (pat >> 2) & 1, (pat >> 1) & 1, pat & 1
            copies.append(pltpu.make_async_remote_copy(
                src_ref=buf.at[me], dst_ref=buf.at[me], send_sem=send_sems.at[pat - 1], recv_sem=recv_sems.at[pat - 1],
                device_id=(x ^ fx, y ^ fy, c ^ fc), device_id_type=MESH))
        for cp in copies:
            cp.start()
        for cp in copies:
            cp.wait()
        acc = buf[0]
        for dev in range(1, N_DEV):
            acc = acc + buf[dev]
        o_ref[...] = acc

    return pl.pallas_call(
        body, name="allreduce_small", in_specs=[pl.BlockSpec(memory_space=pltpu.VMEM)],
        out_specs=pl.BlockSpec(memory_space=pltpu.VMEM), out_shape=jax.ShapeDtypeStruct((n, w), F32),
        scratch_shapes=[pltpu.VMEM((N_DEV, n, w), F32), pltpu.SemaphoreType.DMA((N_DEV - 1,)),
                        pltpu.SemaphoreType.DMA((N_DEV - 1,))],
    )(p)


class _WeightFeed:
    def __init__(self):
        self.fulls, self.ici_send, self.ici_recv, self.d2d = [], [], [], []

    def start(self, name, fulls, after):
        started, send, recv, token = _gather_start(name, fulls, after)
        self.fulls += started
        self.ici_send += send
        self.ici_recv += recv
        self.d2d += [None] * len(fulls)
        self.token = token
        return token

    def _pass_on(self, k, after):
        if k == 0:
            after = self.token
        if k < len(self.fulls) and self.d2d[k] is None:
            self.fulls[k], send, recv = _gather_pass_on(f"gather_pass_{k}", self.fulls[k], self.ici_recv[k], after)
            self.d2d[k] = (send, recv)

    def take(self, k, after):
        self._pass_on(k, after)
        self.fulls[k] = _gather_arrive(f"gather_arrive_{k}", self.fulls[k], self.ici_send[k], *self.d2d[k], after)
        return self.fulls[k]


def _ffn_forward(tag, x, h, g_post, next_gain, feed, k):
    s, d = x.shape
    gu_w = feed.take(k, h)
    gu, a = _ffn_up(f"{tag}_up", h, gu_w)
    dn_w = feed.take(k + 1, a).reshape(-1, d)
    f = dn_w.shape[0]
    tm, tn = _tile(s, 1024), _tile(d, 512)
    y = _mm(f"{tag}_down", a, dn_w, mode="nn", grid=(s // tm, d // tn),
            a_spec=pl.BlockSpec((tm, f), lambda i, j: (i, 0)),
            b_spec=pl.BlockSpec((f, tn), lambda i, j: (0, j)),
            o_spec=pl.BlockSpec((tm, tn), lambda i, j: (i, j)),
            out_shape=jax.ShapeDtypeStruct((s, d), F32))
    x_new, h_next = _res_norm(f"{tag}_post", x, y, g_post, FFN_RESIDUAL_WEIGHT, next_gain)
    return x_new, h_next, (x, h, gu, a, y)


class _GradReduce:
    def __init__(self, core, chip, n_layers, per_layer=()):
        self.core, self.chip, self.n_layers, self.per_layer = core, chip, n_layers, per_layer
        self.state = {}
        self.bufs = {}
        self.scatter_tokens = {}

    def start(self, kinds, layer, gs):
        gs, lands, send, recv, token = _swap_start(f"swap_start_{kinds[0]}_{layer}", gs)
        self.state[kinds, layer] = (gs, lands, send, recv)
        return token

    def exchange(self, kinds, layer, after):
        tag = f"{kinds[0]}_{layer}"
        gs, sibs = _swap_wait(f"swap_wait_{tag}", *self.state[kinds, layer], after)
        hs = [_add_core_halves(f"add_cores_{k}_{layer}", g, sib, self.core) for k, g, sib in zip(kinds, gs, sibs)]
        hs, lands, send, recv, token = _scatter_start(f"scatter_start_{tag}", hs)
        self.state[kinds, layer] = (hs, lands, send, recv)
        self.scatter_tokens[kinds, layer] = token
        return token

    def finish(self, kinds, layer, after):
        tag = f"{kinds[0]}_{layer}"
        hs, rcvs = _scatter_wait(f"scatter_wait_{tag}", *self.state.pop((kinds, layer)), after)
        for k, h, rcv in zip(kinds, hs, rcvs):
            if k in self.per_layer:
                last = self.bufs[k, layer] = _sum_chips(f"sum_chips_{k}_{layer}", h, rcv, self.core, self.chip,
                                                        0, 1, None)
            else:
                last = self.bufs[k] = _sum_chips(f"sum_chips_{k}_{layer}", h, rcv, self.core, self.chip, layer,
                                                 self.n_layers, self.bufs.get(k))
        return last


def _ffn_backward(tag, dx_new, saved, g_pre, g_post, gu_w, dn_w, red, kinds, layer, deps, head, following,
                  last=None):
    x, h, gu, a, y = saved
    s, d = x.shape
    nb, fs = gu_w.shape[0], gu_w.shape[2]
    f = dn_w.shape[0]
    fr = f // nb
    dy, dg_post = head or _norm_bwd(f"{tag}_post_bwd", dx_new, y, g_post, FFN_RESIDUAL_WEIGHT, None, BF16)
    dgu = _ffn_dact(f"{tag}_dact", dy, dn_w, gu, deps)
    dgu4 = dgu.reshape(nb, s, fs)
    tm, tw = _tile(d, 1024), _tile(fs, 1408)
    nw = fs // tw
    tn = _tile(d, 1024)
    ts, td = _tile(s, 1024), _tile(d, 1024)

    def gate_up_gradient(deps):
        return _mm(f"{tag}_dwgu", h, dgu4, mode="tn", grid=(nb, nw, d // tm),
                   a_spec=pl.BlockSpec((s, tm), lambda k, j, i: (0, i)),
                   b_spec=pl.BlockSpec((None, s, tw), lambda k, j, i: (k, 0, j)),
                   o_spec=pl.BlockSpec((None, tm, tw), lambda k, j, i: (k, i, j)),
                   out_shape=jax.ShapeDtypeStruct((nb, d, fs), BF16), deps=deps)

    def down_gradient(deps):
        return _mm(f"{tag}_dwd", a, dy, mode="tn", grid=(nb, d // tn),
                   a_spec=pl.BlockSpec((s, fr), lambda i, j: (0, i)),
                   b_spec=pl.BlockSpec((s, tn), lambda i, j: (0, j)),
                   o_spec=pl.BlockSpec((None, fr, tn), lambda i, j: (i, 0, j)),
                   out_shape=jax.ShapeDtypeStruct((nb, fr, d), BF16), deps=deps)

    def input_gradient(deps):
        dh = _mm(f"{tag}_dh", dgu4, gu_w, mode="nt", grid=(s // ts, d // td, nb),
                 a_spec=pl.BlockSpec((None, ts, fs), lambda i, j, k: (k, i, 0)),
                 b_spec=pl.BlockSpec((None, td, fs), lambda i, j, k: (k, j, 0)),
                 o_spec=pl.BlockSpec((ts, td), lambda i, j, k: (i, j)),
                 out_shape=jax.ShapeDtypeStruct((s, d), F32), nk=nb, acc_shape=(ts, td), deps=deps)
        return _norm_bwd(f"{tag}_pre_bwd", dh, x, g_pre, 1.0, dx_new, F32, following)

    if last is None:
        started = red.start(kinds, layer, [gate_up_gradient(()), down_gradient(())])
        dx, dg_pre, *next_head = input_gradient((started,))
    else:
        dx, dg_pre, *next_head = input_gradient(())
        first = red.start(kinds[:1], layer, [gate_up_gradient((last(dg_pre, dg_post),))])
        second = red.start(kinds[1:], layer, [down_gradient((first,))])
        red.exchange(kinds[:1], layer, second)
    return dx, dg_pre, dg_post, tuple(next_head) or None


def _mixer_forward(tag, x, h, gains, next_gain, feed, k, conv_taps, dims):
    qd, kvd, cd = dims
    s, d = x.shape
    _, g_a, g_c, g_post = gains
    win_w = feed.take(k, h)
    nb, cw = win_w.shape[0], win_w.shape[2]
    tm = _tile(s, 1024)
    z = _mm(f"{tag}_in", h, win_w, mode="nn", grid=(nb, s // tm),
            a_spec=pl.BlockSpec((tm, d), lambda j, i: (i, 0)),
            b_spec=pl.BlockSpec((None, d, cw), lambda j, i: (j, 0, 0)),
            o_spec=pl.BlockSpec((tm, cw), lambda j, i: (i, j)),
            out_shape=jax.ShapeDtypeStruct((s, nb * cw), BF16))
    a, lse = _attn_fwd(f"{tag}_attn", z, qd, kvd)
    c = _conv_fwd(f"{tag}_conv", z, conv_taps, qd + 2 * kvd, cd)
    cat = _cat_norm_fwd(f"{tag}_cat", a, c, g_a, g_c)
    wout_w = feed.take(k + 1, cat).reshape(-1, d)
    mw = qd + cd
    tn = _tile(d, 1024)
    mixed = _mm(f"{tag}_out", cat, wout_w, mode="nn", grid=(s // tm, d // tn),
                a_spec=pl.BlockSpec((tm, mw), lambda i, j: (i, 0)),
                b_spec=pl.BlockSpec((mw, tn), lambda i, j: (0, j)),
                o_spec=pl.BlockSpec((tm, tn), lambda i, j: (i, j)),
                out_shape=jax.ShapeDtypeStruct((s, d), F32))
    x_new, h_next = _res_norm(f"{tag}_post", x, mixed, g_post, 1.0, next_gain)
    return x_new, h_next, (x, h, z, a, lse, c, cat, mixed)


def _mixer_backward(tag, dx_new, saved, gains, win_w, conv_taps, wout_w, dims, red, kinds, layer, deps, head,
                    following):
    qd, kvd, cd = dims
    x, h, z, a, lse, c, cat, mixed = saved
    s, d = x.shape
    nb, cw = win_w.shape[0], win_w.shape[2]
    g_pre, g_a, g_c, g_post = gains
    mw = qd + cd
    dmixed, dg_post = head or _norm_bwd(f"{tag}_post_bwd", dx_new, mixed, g_post, 1.0, None, BF16)
    tm, tn = _tile(s, 1024), _tile(mw, 1024)
    dcat = _mm(f"{tag}_dcat", dmixed, wout_w, mode="nt", grid=(s // tm, mw // tn),
               a_spec=pl.BlockSpec((tm, d), lambda i, j: (i, 0)),
               b_spec=pl.BlockSpec((tn, d), lambda i, j: (j, 0)),
               o_spec=pl.BlockSpec((tm, tn), lambda i, j: (i, j)),
               out_shape=jax.ShapeDtypeStruct((s, mw), F32), deps=deps)
    wr = mw // nb
    td = _tile(d, 1024)
    d_wout = _mm(f"{tag}_dwout", cat, dmixed, mode="tn", grid=(nb, d // td),
                 a_spec=pl.BlockSpec((s, wr), lambda i, j: (0, i)),
                 b_spec=pl.BlockSpec((s, td), lambda i, j: (0, j)),
                 o_spec=pl.BlockSpec((None, wr, td), lambda i, j: (i, 0, j)),
                 out_shape=jax.ShapeDtypeStruct((nb, wr, d), BF16))
    da, dc, dg_a, dg_c = _cat_norm_bwd(f"{tag}_cat_bwd", dcat, a, c, g_a, g_c)
    dhc, dbg, dcg, d_taps = _conv_bwd(f"{tag}_conv_bwd", z, conv_taps, dc, qd + 2 * kvd, cd)
    dq, dk, dv = _attn_bwd(f"{tag}_attn_bwd", z, a, lse, da, qd, kvd)
    dz = jnp.concatenate([dq, dk, dv, dhc, dbg, dcg], axis=1)
    th = _tile(d, 1024)
    d_win = _mm(f"{tag}_dwin", h, dz, mode="tn", grid=(nb, d // th),
                a_spec=pl.BlockSpec((s, th), lambda k, i: (0, i)),
                b_spec=pl.BlockSpec((s, cw), lambda k, i: (0, k)),
                o_spec=pl.BlockSpec((None, th, cw), lambda k, i: (k, i, 0)),
                out_shape=jax.ShapeDtypeStruct((nb, d, cw), BF16))
    started = (red.start(kinds, layer, [d_win, d_wout]),)
    dh = _mm(f"{tag}_dh", dz, win_w, mode="nt", grid=(s // tm, d // td, nb),
             a_spec=pl.BlockSpec((tm, cw), lambda i, j, k: (i, k)),
             b_spec=pl.BlockSpec((None, td, cw), lambda i, j, k: (k, j, 0)),
             o_spec=pl.BlockSpec((tm, td), lambda i, j, k: (i, j)),
             out_shape=jax.ShapeDtypeStruct((s, d), F32), nk=nb, acc_shape=(tm, td), deps=started)
    dx, dg_pre, *next_head = _norm_bwd(f"{tag}_pre_bwd", dh, x, g_pre, 1.0, dx_new, F32, following)
    return dx, d_taps, (dg_pre, dg_a, dg_c, dg_post), tuple(next_head) or None


def kernel(x, ffn1_norm_pre, ffn1_w_gate_up, ffn1_w_down, ffn1_norm_post, mix_norm_pre, w_in, conv_w, attn_out_norm, conv_out_norm, w_out, mix_norm_post, ffn2_norm_pre, ffn2_w_gate_up, ffn2_w_down, ffn2_norm_post, loss_target, m_ffn1_norm_pre, m_ffn1_w_gate_up, m_ffn1_w_down, m_ffn1_norm_post, m_mix_norm_pre, m_w_in, m_conv_w, m_attn_out_norm, m_conv_out_norm, m_w_out, m_mix_norm_post, m_ffn2_norm_pre, m_ffn2_w_gate_up, m_ffn2_w_down, m_ffn2_norm_post, v_ffn1_norm_pre, v_ffn1_w_gate_up, v_ffn1_w_down, v_ffn1_norm_post, v_mix_norm_pre, v_w_in, v_conv_w, v_attn_out_norm, v_conv_out_norm, v_w_out, v_mix_norm_post, v_ffn2_norm_pre, v_ffn2_w_gate_up, v_ffn2_w_down, v_ffn2_norm_post):
    _, s, d = x.shape
    n_layers = ffn1_norm_pre.shape[0]
    qd = attn_out_norm.shape[1]
    cd = conv_out_norm.shape[1]
    kvd = qd // Q_PER_KV
    dims = (qd, kvd, cd)
    assert N_CHIPS * w_in.shape[2] == qd + 2 * kvd + 3 * cd and qd + cd == N_CHIPS * w_out.shape[1]
    assert 2 * d <= SMALL_ROWS * LANES * SUBLANES
    chip = 2 * lax.axis_index("x") + lax.axis_index("y")
    chip_arr = chip.astype(jnp.int32).reshape(1)
    core = lax.axis_index("c").astype(jnp.int32).reshape(1)
    kinds = ("gu1", "dn1", "win", "wout", "gu2", "dn2")

    big = (ffn1_w_gate_up, ffn1_w_down, w_in, w_out, ffn2_w_gate_up, ffn2_w_down)
    nk = len(kinds)
    taps_all = _gather_taps(conv_w)
    feed = _WeightFeed()
    order = [(k, w, layer) for layer in range(n_layers) for k, w in zip(kinds, big)]
    k, w, layer = order[0]
    token = feed.start("gather_start_first", [_cast_into_slot(f"cast_{k}_{layer}", w, layer, chip_arr)], taps_all)
    feed.start("gather_start_rest", [_cast_into_slot(f"cast_{k}_{layer}", w, layer, chip_arr, (token,))
                                     for k, w, layer in order[1:]], token)
    taps = jnp.transpose(taps_all, (1, 2, 0, 3)).reshape(n_layers, CONV_WIDTH, cd)
    taps = jnp.pad(taps, ((0, 0), (0, SUBLANES - CONV_WIDTH), (0, 0)))

    def gain(g, layer):
        return g[layer][None, :]

    xs = x[0]
    hs = _norm_fwd("l0_ffn1_norm", xs, gain(ffn1_norm_pre, 0))
    saved = []
    for layer in range(n_layers):
        t = f"l{layer}"
        k0 = layer * nk
        xs, hs, s1 = _ffn_forward(f"{t}_ffn1", xs, hs, gain(ffn1_norm_post, layer), gain(mix_norm_pre, layer), feed, k0)
        mix_gains = (gain(mix_norm_pre, layer), gain(attn_out_norm, layer), gain(conv_out_norm, layer), gain(mix_norm_post, layer))
        xs, hs, s2 = _mixer_forward(f"{t}_mix", xs, hs, mix_gains, gain(ffn2_norm_pre, layer), feed, k0 + 2,
                                    taps[layer], dims)
        following = gain(ffn1_norm_pre, layer + 1) if layer + 1 < n_layers else None
        xs, hs, s3 = _ffn_forward(f"{t}_ffn2", xs, hs, gain(ffn2_norm_post, layer), following, feed, k0 + 4)
        saved.append((s1, s2, s3, mix_gains))
    wts = {k: [feed.fulls[layer * nk + i] for layer in range(n_layers)] for i, k in enumerate(kinds)}
    for k in ("dn1", "wout", "dn2"):
        wts[k] = [w.reshape(-1, d) for w in wts[k]]
    top = n_layers - 1
    dxs, loss_part, *head = _loss_head("loss_head", xs, loss_target[0],
                                       (saved[top][2][4], gain(ffn2_norm_post, top), FFN_RESIDUAL_WEIGHT))
    loss = lax.psum(jnp.sum(loss_part), ("x", "y", "c"))

    red = _GradReduce(core, chip_arr, n_layers, per_layer=("gu1", "dn1"))
    small = [None] * n_layers
    flow = {"deps": (), "in_flight": None}

    def between(dx, group):
        after = dx
        if flow["in_flight"] is not None:
            after = red.finish(*flow["in_flight"], after)
        flow["deps"] = (red.exchange(*group, after),)
        flow["in_flight"] = group

    head = tuple(head)
    for layer in reversed(range(n_layers)):
        t = f"l{layer}"
        s1, s2, s3, mix_gains = saved[layer]
        after_ffn2 = (s2[7], mix_gains[3], 1.0)
        after_mix = (s1[4], gain(ffn1_norm_post, layer), FFN_RESIDUAL_WEIGHT)
        after_ffn1 = ((saved[layer - 1][2][4], gain(ffn2_norm_post, layer - 1), FFN_RESIDUAL_WEIGHT)
                      if layer > 0 else None)
        dxs, p_pre2, p_post2, head = _ffn_backward(
            f"{t}_ffn2", dxs, s3, gain(ffn2_norm_pre, layer), gain(ffn2_norm_post, layer),
            wts["gu2"][layer], wts["dn2"][layer], red, ("gu2", "dn2"), layer, flow["deps"], head, after_ffn2)
        between(dxs, (("gu2", "dn2"), layer))
        dxs, p_taps, (p_mpre, p_a, p_c, p_mpost), head = _mixer_backward(
            f"{t}_mix", dxs, s2, mix_gains, wts["win"][layer], taps[layer], wts["wout"][layer], dims,
            red, ("win", "wout"), layer, flow["deps"], head, after_mix)
        between(dxs, (("win", "wout"), layer))
        def pack_small(p_pre1, p_post1):
            tap_rows = jnp.zeros((CONV_WIDTH, SUBLANES, d), F32).at[:, 0, :cd].set(p_taps[:CONV_WIDTH])
            rows = [p_pre1, p_post1, p_mpre, jnp.concatenate([p_a, p_c], axis=1), p_mpost, p_pre2, p_post2]
            rows = jnp.concatenate([jnp.stack(rows), tap_rows], axis=0)
            small[layer] = jnp.pad(rows, ((0, SMALL_ROWS - rows.shape[0]), (0, 0), (0, 0)))

        def reduce_small(p_pre1, p_post1):
            pack_small(p_pre1, p_post1)
            flow["small"] = _allreduce_small(jnp.concatenate(small, axis=0))
            return flow["small"]

        dxs, p_pre1, p_post1, head = _ffn_backward(
            f"{t}_ffn1", dxs, s1, gain(ffn1_norm_pre, layer), gain(ffn1_norm_post, layer),
            wts["gu1"][layer], wts["dn1"][layer], red, ("gu1", "dn1"), layer, flow["deps"], head, after_ffn1,
            last=reduce_small if layer == 0 else None)
        if layer > 0:
            pack_small(p_pre1, p_post1)
        between(dxs, (("dn1",) if layer == 0 else ("gu1", "dn1"), layer))
    grad_x = dxs[None]

    weights = dict(ffn1_norm_pre=ffn1_norm_pre, ffn1_w_gate_up=ffn1_w_gate_up, ffn1_w_down=ffn1_w_down, ffn1_norm_post=ffn1_norm_post, mix_norm_pre=mix_norm_pre, w_in=w_in, conv_w=conv_w, attn_out_norm=attn_out_norm, conv_out_norm=conv_out_norm, w_out=w_out, mix_norm_post=mix_norm_post, ffn2_norm_pre=ffn2_norm_pre, ffn2_w_gate_up=ffn2_w_gate_up, ffn2_w_down=ffn2_w_down, ffn2_norm_post=ffn2_norm_post)
    m_in = dict(ffn1_norm_pre=m_ffn1_norm_pre, ffn1_w_gate_up=m_ffn1_w_gate_up, ffn1_w_down=m_ffn1_w_down, ffn1_norm_post=m_ffn1_norm_post, mix_norm_pre=m_mix_norm_pre, w_in=m_w_in, conv_w=m_conv_w, attn_out_norm=m_attn_out_norm, conv_out_norm=m_conv_out_norm, w_out=m_w_out, mix_norm_post=m_mix_norm_post, ffn2_norm_pre=m_ffn2_norm_pre, ffn2_w_gate_up=m_ffn2_w_gate_up, ffn2_w_down=m_ffn2_w_down, ffn2_norm_post=m_ffn2_norm_post)
    v_in = dict(ffn1_norm_pre=v_ffn1_norm_pre, ffn1_w_gate_up=v_ffn1_w_gate_up, ffn1_w_down=v_ffn1_w_down, ffn1_norm_post=v_ffn1_norm_post, mix_norm_pre=v_mix_norm_pre, w_in=v_w_in, conv_w=v_conv_w, attn_out_norm=v_attn_out_norm, conv_out_norm=v_conv_out_norm, w_out=v_w_out, mix_norm_post=v_mix_norm_post, ffn2_norm_pre=v_ffn2_norm_pre, ffn2_w_gate_up=v_ffn2_w_gate_up, ffn2_w_down=v_ffn2_w_down, ffn2_norm_post=v_ffn2_norm_post)
    kind_name = dict(gu1="ffn1_w_gate_up", dn1="ffn1_w_down", win="w_in", wout="w_out", gu2="ffn2_w_gate_up", dn2="ffn2_w_down")
    delta, new_m, new_v, grad = {}, {}, {}, {}

    def join_and_update(name, items, deps, after):
        ts, send_sems, recv_sems = _join_start(name, [red.bufs[it] for it in items], deps)
        for a, it in enumerate(items):
            k, layer = it if isinstance(it, tuple) else (it, None)
            n = kind_name[k]
            tag = n if layer is None else f"{n}_{layer}"
            g = _join_wait(f"join_wait_{tag}", ts[a], a, send_sems, recv_sems, after)
            prev = (delta[n], new_m[n], new_v[n], grad[n]) if n in delta else None
            delta[n], new_m[n], new_v[n], grad[n] = _adamw(f"adamw_{tag}", weights[n], g, m_in[n], v_in[n], True,
                                                           layer, prev)
            after = delta[n]
        return after

    early = ("wout", "win", "dn2", "gu2") + tuple((k, layer) for layer in range(1, n_layers) for k in ("dn1", "gu1"))
    last_groups = ((("gu1",), 0), flow["in_flight"])
    done_early = join_and_update("join_early", early, tuple(red.scatter_tokens[g] for g in last_groups), dxs)
    for group in last_groups:
        red.finish(*group, done_early)
    join_and_update("join_late", (("dn1", 0), ("gu1", 0)), (), done_early)

    small_sum = flow["small"].reshape(n_layers, SMALL_ROWS, d)
    g_ffn1_pre, g_ffn1_post, g_mix_pre = small_sum[:, 0], small_sum[:, 1], small_sum[:, 2]
    g_attn_out, g_conv_out = small_sum[:, 3, :qd], small_sum[:, 3, qd:qd + cd]
    g_mix_post, g_ffn2_pre, g_ffn2_post = small_sum[:, 4], small_sum[:, 5], small_sum[:, 6]
    cc = conv_w.shape[2]
    g_conv = lax.dynamic_slice_in_dim(small_sum[:, 7:7 + CONV_WIDTH, :cd], chip * cc, cc, axis=2)

    grad.update(ffn1_norm_pre=g_ffn1_pre, ffn1_norm_post=g_ffn1_post, mix_norm_pre=g_mix_pre, conv_w=g_conv, attn_out_norm=g_attn_out, conv_out_norm=g_conv_out, mix_norm_post=g_mix_post, ffn2_norm_pre=g_ffn2_pre, ffn2_norm_post=g_ffn2_post)
    names = list(weights)

    vectors = [n for n in names if n not in kind_name.values()]

    def pack(tree):
        flat = jnp.concatenate([tree[n].reshape(-1) for n in vectors])
        return jnp.pad(flat, (0, -flat.size % (SUBLANES * LANES))).reshape(-1, LANES)

    packed = _adamw("adamw_small", pack(weights), pack(grad), pack(m_in), pack(v_in))
    offset = 0
    for n in vectors:
        size = weights[n].size
        for tree, flat in zip((delta, new_m, new_v), packed):
            tree[n] = flat.reshape(-1)[offset:offset + size].reshape(weights[n].shape)
        offset += size

    return (loss, grad_x, *[grad[n] for n in names], *[delta[n] for n in names],
            *[new_m[n] for n in names], *[new_v[n] for n in names])
```

```python
import functools

import jax
import jax.numpy as jnp
from jax import lax
from jax.experimental import pallas as pl
from jax.experimental.pallas import tpu as pltpu

F32 = jnp.float32
BF16 = jnp.bfloat16
MESH = pl.DeviceIdType.MESH

NORM_EPS = 1e-6
HEAD_DIM = 128
Q_PER_KV = 4
CONV_WIDTH = 3
FFN_RESIDUAL_WEIGHT = 0.5
DILATED_BRANCHES = ((128, 1), (512, 4), (2048, 16))
ADAM_LR = 0.001
ADAM_B1 = 0.9
ADAM_B2 = 0.999
ADAM_EPS = 1e-08
ADAM_WD = 0.01
ADAM_STEP = 10

N_CHIPS = 4
N_DEV = 8
V7X_VMEM_BYTES = 64 << 20
VMEM_LIMIT = V7X_VMEM_BYTES - (12 << 20)
SUBLANES = 8
LANES = 128
SMALL_ROWS = 16
BIG_BLOCK = 6 << 20


def _params(*sem):
    return pltpu.CompilerParams(dimension_semantics=sem, vmem_limit_bytes=VMEM_LIMIT)


def _row_tile(rows, cols, itemsize=4, budget=2 << 20):
    t = rows
    while t * cols * itemsize > budget and t % 32 == 0:
        t //= 2
    return t


def _sum_to_sublanes(v):
    r, n = v.shape
    return v.reshape(r // SUBLANES, SUBLANES, n).sum(axis=0)


_DIMS = {
    "nn": (((1,), (0,)), ((), ())),
    "nt": (((1,), (1,)), ((), ())),
    "tn": (((0,), (0,)), ((), ())),
}


ANY_SPEC = pl.BlockSpec(memory_space=pl.ANY)


def _dot(a, b, mode):
    return lax.dot_general(a, b, _DIMS[mode], preferred_element_type=F32)


def _mm(name, a, b, *, mode, grid, a_spec, b_spec, o_spec, out_shape, nk=1, acc_shape=None, deps=()):
    nd = len(deps)

    def body(a_ref, b_ref, *rest):
        o_ref, scratch = rest[nd], rest[nd + 1:]
        r = _dot(a_ref[...], b_ref[...], mode)
        if nk == 1:
            o_ref[...] = r.astype(o_ref.dtype)
        else:
            acc = scratch[0]
            k = pl.program_id(len(grid) - 1)

            @pl.when(k == 0)
            def _():
                acc[...] = r

            @pl.when(k > 0)
            def _():
                acc[...] += r

            @pl.when(k == nk - 1)
            def _():
                o_ref[...] = acc[...].astype(o_ref.dtype)

    sem = ("parallel",) * (len(grid) - (1 if nk > 1 else 0)) + (("arbitrary",) if nk > 1 else ())
    return pl.pallas_call(
        body, name=name, grid=grid, in_specs=[a_spec, b_spec] + [ANY_SPEC] * nd, out_specs=o_spec,
        out_shape=out_shape, scratch_shapes=[pltpu.VMEM(acc_shape, F32)] if nk > 1 else [],
        compiler_params=_params(*sem),
    )(a, b, *deps)


def _tile(n, want):
    if n <= want:
        return n
    best = None
    for t in range(LANES, want + 1, LANES):
        if n % t == 0:
            best = t
    assert best is not None, (n, want)
    return best


def _norm_fwd(name, x, gain):
    s, d = x.shape
    tr = _row_tile(s, d, budget=BIG_BLOCK)

    def body(x_ref, g_ref, o_ref):
        xv = x_ref[...]
        r = lax.rsqrt(jnp.mean(xv * xv, axis=-1, keepdims=True) + NORM_EPS)
        o_ref[...] = (xv * r * g_ref[...]).astype(o_ref.dtype)

    return pl.pallas_call(
        body, name=name, grid=(s // tr,),
        in_specs=[pl.BlockSpec((tr, d), lambda i: (i, 0)), pl.BlockSpec((1, d), lambda i: (0, 0))],
        out_specs=pl.BlockSpec((tr, d), lambda i: (i, 0)),
        out_shape=jax.ShapeDtypeStruct((s, d), BF16), compiler_params=_params("parallel"),
    )(x, gain)


def _res_norm(name, x, y, gain, scale, next_gain=None):
    s, d = x.shape
    tr = _row_tile(s, d, budget=BIG_BLOCK)
    with_next = next_gain is not None

    def body(x_ref, y_ref, g_ref, *rest):
        yv = y_ref[...]
        r = lax.rsqrt(jnp.mean(yv * yv, axis=-1, keepdims=True) + NORM_EPS)
        xn = x_ref[...] + scale * (yv * r * g_ref[...])
        if with_next:
            ng_ref, o_ref, h_ref = rest
            rn = lax.rsqrt(jnp.mean(xn * xn, axis=-1, keepdims=True) + NORM_EPS)
            h_ref[...] = (xn * rn * ng_ref[...]).astype(h_ref.dtype)
        else:
            o_ref, = rest
        o_ref[...] = xn

    row = pl.BlockSpec((tr, d), lambda i: (i, 0))
    vec = pl.BlockSpec((1, d), lambda i: (0, 0))
    outs = pl.pallas_call(
        body, name=name, grid=(s // tr,),
        in_specs=[row, row, vec] + ([vec] if with_next else []), out_specs=[row] * (2 if with_next else 1),
        out_shape=[jax.ShapeDtypeStruct((s, d), F32)] + ([jax.ShapeDtypeStruct((s, d), BF16)] if with_next else []),
        compiler_params=_params("parallel"),
    )(x, y, gain, *((next_gain,) if with_next else ()))
    return (outs[0], outs[1]) if with_next else (outs[0], None)


def _rms_bwd(dn, yv, gv):
    r = lax.rsqrt(jnp.mean(yv * yv, axis=-1, keepdims=True) + NORM_EPS)
    xhat = yv * r
    dxn = dn * gv
    return r * (dxn - xhat * jnp.mean(dxn * xhat, axis=-1, keepdims=True)), _sum_to_sublanes(dn * xhat)


def _accumulate(ref, part):
    @pl.when(pl.program_id(0) == 0)
    def _():
        ref[...] = part

    @pl.when(pl.program_id(0) > 0)
    def _():
        ref[...] += part


def _norm_bwd(name, dout, yin, gain, scale, resid, out_dtype, following=None):
    s, d = yin.shape
    tr = _row_tile(s, d)
    has_resid = resid is not None
    chained = following is not None

    def body(*refs):
        refs = list(refs)
        do_ref, y_ref, g_ref = refs[:3]
        del refs[:3]
        r_ref = refs.pop(0) if has_resid else None
        if chained:
            y2_ref, g2_ref = refs[:2]
            del refs[:2]
        di_ref, dg_ref = refs[:2]
        din, part = _rms_bwd(scale * do_ref[...], y_ref[...], g_ref[...])
        _accumulate(dg_ref, part)
        if has_resid:
            din = din + r_ref[...]
        di_ref[...] = din.astype(di_ref.dtype)
        if chained:
            d2_ref, dg2_ref = refs[2:]
            d2, part2 = _rms_bwd(following[2] * din, y2_ref[...], g2_ref[...])
            _accumulate(dg2_ref, part2)
            d2_ref[...] = d2.astype(d2_ref.dtype)

    row = pl.BlockSpec((tr, d), lambda i: (i, 0))
    vec = pl.BlockSpec((1, d), lambda i: (0, 0))
    acc = pl.BlockSpec((SUBLANES, d), lambda i: (0, 0))
    ins = [row, row, vec] + ([row] if has_resid else []) + ([row, vec] if chained else [])
    args = (dout, yin, gain) + ((resid,) if has_resid else ()) + (tuple(following[:2]) if chained else ())
    outs = [row, acc] + ([row, acc] if chained else [])
    shapes = [jax.ShapeDtypeStruct((s, d), out_dtype), jax.ShapeDtypeStruct((SUBLANES, d), F32)]
    if chained:
        shapes += [jax.ShapeDtypeStruct((s, d), BF16), jax.ShapeDtypeStruct((SUBLANES, d), F32)]
    return pl.pallas_call(
        body, name=name, grid=(s // tr,), in_specs=ins, out_specs=outs, out_shape=shapes,
        compiler_params=_params("arbitrary"),
    )(*args)


def _loss_head(name, y, target, following):
    s, d = y.shape
    tr = _row_tile(s, d)
    y2, gain2, scale2 = following

    def body(y_ref, t_ref, y2_ref, g2_ref, dy_ref, l_ref, d2_ref, dg2_ref):
        e = y_ref[...] - t_ref[...]
        dy = e * (1.0 / d)
        dy_ref[...] = dy
        _accumulate(l_ref, _sum_to_sublanes(e * e) * (0.5 / d))
        d2, part2 = _rms_bwd(scale2 * dy, y2_ref[...], g2_ref[...])
        _accumulate(dg2_ref, part2)
        d2_ref[...] = d2.astype(d2_ref.dtype)

    row = pl.BlockSpec((tr, d), lambda i: (i, 0))
    acc = pl.BlockSpec((SUBLANES, d), lambda i: (0, 0))
    return pl.pallas_call(
        body, name=name, grid=(s // tr,), in_specs=[row, row, row, pl.BlockSpec((1, d), lambda i: (0, 0))],
        out_specs=[row, acc, row, acc],
        out_shape=[jax.ShapeDtypeStruct((s, d), F32), jax.ShapeDtypeStruct((SUBLANES, d), F32),
                   jax.ShapeDtypeStruct((s, d), BF16), jax.ShapeDtypeStruct((SUBLANES, d), F32)],
        compiler_params=_params("arbitrary"),
    )(y, target, y2, gain2)


def _ffn_up(name, h, gu_w):
    s, d = h.shape
    nb, _, fs = gu_w.shape
    hb = nb // 2
    w = gu_w.reshape(2, hb, d, fs)
    tm = _tile(s, 512)
    tn = _tile(fs, 1408)
    nj = fs // tn

    def body(h_ref, w_ref, gu_ref, a_ref):
        hv = h_ref[...]
        g = _dot(hv, w_ref[0], "nn")
        u = _dot(hv, w_ref[1], "nn")
        sg = jax.nn.sigmoid(g)
        silu = g * sg
        gu_ref[0] = (u * (sg * (1.0 + g * (1.0 - sg)))).astype(gu_ref.dtype)
        gu_ref[1] = silu.astype(gu_ref.dtype)
        a_ref[...] = (silu * u).astype(a_ref.dtype)

    return pl.pallas_call(
        body, name=name, grid=(hb, nj, s // tm),
        in_specs=[pl.BlockSpec((tm, d), lambda jb, jo, i: (i, 0)),
                  pl.BlockSpec((2, None, d, tn), lambda jb, jo, i: (0, jb, 0, jo))],
        out_specs=[pl.BlockSpec((2, None, tm, tn), lambda jb, jo, i: (0, jb, i, jo)),
                   pl.BlockSpec((tm, tn), lambda jb, jo, i: (i, jb * nj + jo))],
        out_shape=[jax.ShapeDtypeStruct((2, hb, s, fs), BF16), jax.ShapeDtypeStruct((s, hb * fs), BF16)],
        compiler_params=_params("parallel", "parallel", "parallel"),
    )(h, w)


def _ffn_dact(name, dy, dn_w, gu, deps=()):
    s, d = dy.shape
    _, hb, _, fs = gu.shape
    tm = _tile(s, 1024)
    tn = _tile(fs, 1408)
    nj = fs // tn

    def body(dy_ref, w_ref, gu_ref, *rest):
        o_ref = rest[-1]
        wv = w_ref[...]
        parts = max(1, tm // 256)
        for r in range(parts):
            rows = slice(r * (tm // parts), (r + 1) * (tm // parts))
            da = _dot(dy_ref[rows, :], wv, "nt")
            o_ref[0, rows, :] = (da * gu_ref[0, rows, :].astype(F32)).astype(o_ref.dtype)
            o_ref[1, rows, :] = (da * gu_ref[1, rows, :].astype(F32)).astype(o_ref.dtype)

    blk = pl.BlockSpec((2, None, tm, tn), lambda jb, jo, i: (0, jb, i, jo))
    return pl.pallas_call(
        body, name=name, grid=(hb, nj, s // tm),
        in_specs=[pl.BlockSpec((tm, d), lambda jb, jo, i: (i, 0)),
                  pl.BlockSpec((tn, d), lambda jb, jo, i: (jb * nj + jo, 0)),
                  blk] + [ANY_SPEC] * len(deps),
        out_specs=blk, out_shape=jax.ShapeDtypeStruct(gu.shape, BF16),
        compiler_params=_params("parallel", "parallel", "parallel"),
    )(dy, dn_w, gu, *deps)


_MASKED = -1e30


def _attn_bias(s, tq):
    nd = s // tq
    dist = (jnp.arange(nd)[:, None, None] * tq + jnp.arange(tq)[None, :, None]) - jnp.arange(tq)[None, None, :]
    mult = jnp.zeros(dist.shape, F32)
    for window, dilation in DILATED_BRANCHES:
        mult = mult + ((dist >= 0) & (dist <= window) & (dist % dilation == 0)).astype(F32)
    return jnp.where(mult > 0.0, jnp.log(jnp.maximum(mult, 1.0)), _MASKED)


def _biased(sc, bias, scale):
    tq, tk = bias.shape
    return (sc.reshape(-1, tq, tk) * scale + bias[None]).reshape(sc.shape)


def _attn_specs(s, qd, kvd, tq):
    rw = Q_PER_KV * HEAD_DIM
    qspec = pl.BlockSpec((tq, rw), lambda g, i: (i, g))
    kspec = pl.BlockSpec((s, HEAD_DIM), lambda g, i: (0, qd // HEAD_DIM + g))
    vspec = pl.BlockSpec((s, HEAD_DIM), lambda g, i: (0, (qd + kvd) // HEAD_DIM + g))
    return rw, qspec, kspec, vspec


def _attn_fwd(name, z, qd, kvd):
    s = z.shape[0]
    tq = _tile(s, 256)
    nkv = kvd // HEAD_DIM
    rw, qspec, kspec, vspec = _attn_specs(s, qd, kvd, tq)
    scale = HEAD_DIM ** -0.5

    def body(q_ref, k_ref, v_ref, b_ref, o_ref, l_ref):
        i = pl.program_id(1)
        heads = [slice(h * HEAD_DIM, (h + 1) * HEAD_DIM) for h in range(Q_PER_KV)]
        q_all = jnp.concatenate([q_ref[:, cols] for cols in heads], axis=0)

        def chunk(j, carry):
            mx, den, acc = carry
            k0 = pl.multiple_of(j * tq, tq)
            kc, vc = k_ref[pl.ds(k0, tq), :], v_ref[pl.ds(k0, tq), :]
            sc = _biased(_dot(q_all, kc, "nt"), b_ref[i - j], scale)
            mx_new = jnp.maximum(mx, jnp.max(sc, axis=-1, keepdims=True))
            alpha = jnp.exp(mx - mx_new)
            p = jnp.exp(sc - mx_new)
            return (mx_new, alpha * den + jnp.sum(p, axis=-1, keepdims=True),
                    alpha * acc + _dot(p.astype(BF16), vc, "nn"))

        rows = Q_PER_KV * tq
        init = (jnp.full((rows, 1), _MASKED, F32), jnp.zeros((rows, 1), F32), jnp.zeros((rows, HEAD_DIM), F32))
        mx, den, acc = lax.fori_loop(0, i + 1, chunk, init)
        out = acc / den
        lse = mx + jnp.log(den)
        for h, cols in enumerate(heads):
            o_ref[:, cols] = out[h * tq:(h + 1) * tq]
            l_ref[:, cols] = jnp.broadcast_to(lse[h * tq:(h + 1) * tq], (tq, HEAD_DIM))

    bias = _attn_bias(s, tq)
    return pl.pallas_call(
        body, name=name, grid=(nkv, s // tq),
        in_specs=[qspec, kspec, vspec, pl.BlockSpec(bias.shape, lambda g, i: (0, 0, 0))], out_specs=[qspec, qspec],
        out_shape=[jax.ShapeDtypeStruct((s, qd), F32), jax.ShapeDtypeStruct((s, qd), F32)],
        compiler_params=_params("parallel", "parallel"),
    )(z, z, z, bias)


def _attn_bwd(name, z, o, lse, do, qd, kvd):
    s = z.shape[0]
    tq = _tile(s, 256)
    nkv = kvd // HEAD_DIM
    nq = s // tq
    rw, qspec, kspec, vspec = _attn_specs(s, qd, kvd, tq)
    scale = HEAD_DIM ** -0.5

    def body(q_ref, k_ref, v_ref, o_ref, l_ref, do_ref, b_ref, dq_ref, dk_ref, dv_ref, dk_acc, dv_acc):
        i = pl.program_id(1)
        heads = [slice(h * HEAD_DIM, (h + 1) * HEAD_DIM) for h in range(Q_PER_KV)]

        @pl.when(i == 0)
        def _():
            dk_acc[...] = jnp.zeros_like(dk_acc)
            dv_acc[...] = jnp.zeros_like(dv_acc)

        q_all = jnp.concatenate([q_ref[:, cols] for cols in heads], axis=0)
        do_all = jnp.concatenate([do_ref[:, cols].astype(BF16) for cols in heads], axis=0)
        lse_all = jnp.concatenate([l_ref[:, cols][:, :1] for cols in heads], axis=0)
        delta_all = jnp.concatenate(
            [jnp.sum(do_ref[:, cols] * o_ref[:, cols], axis=-1, keepdims=True) for cols in heads], axis=0)

        def chunk(j, dq):
            k0 = pl.multiple_of(j * tq, tq)
            kc, vc = k_ref[pl.ds(k0, tq), :], v_ref[pl.ds(k0, tq), :]
            p = jnp.exp(_biased(_dot(q_all, kc, "nt"), b_ref[i - j], scale) - lse_all)
            ds = (p * (_dot(do_all, vc, "nt") - delta_all) * scale).astype(BF16)
            dk_acc[pl.ds(k0, tq), :] += _dot(ds, q_all, "tn")
            dv_acc[pl.ds(k0, tq), :] += _dot(p.astype(BF16), do_all, "tn")
            return dq + _dot(ds, kc, "nn")

        dq = lax.fori_loop(0, i + 1, chunk, jnp.zeros((Q_PER_KV * tq, HEAD_DIM), F32))
        for h, cols in enumerate(heads):
            dq_ref[:, cols] = dq[h * tq:(h + 1) * tq].astype(dq_ref.dtype)

        @pl.when(i == nq - 1)
        def _():
            dk_ref[...] = dk_acc[...].astype(dk_ref.dtype)
            dv_ref[...] = dv_acc[...].astype(dv_ref.dtype)

    kvout = pl.BlockSpec((s, HEAD_DIM), lambda g, i: (0, g))
    bias = _attn_bias(s, tq)
    return pl.pallas_call(
        body, name=name, grid=(nkv, nq),
        in_specs=[qspec, kspec, vspec, qspec, qspec, qspec, pl.BlockSpec(bias.shape, lambda g, i: (0, 0, 0))],
        out_specs=[qspec, kvout, kvout],
        out_shape=[jax.ShapeDtypeStruct((s, qd), BF16), jax.ShapeDtypeStruct((s, kvd), BF16),
                   jax.ShapeDtypeStruct((s, kvd), BF16)],
        scratch_shapes=[pltpu.VMEM((s, HEAD_DIM), F32), pltpu.VMEM((s, HEAD_DIM), F32)],
        compiler_params=_params("parallel", "arbitrary"),
    )(z, z, z, o, lse, do, bias)


def _shift_down(v, n):
    rolled = pltpu.roll(v, n, 0)
    t = lax.broadcasted_iota(jnp.int32, v.shape, 0)
    return jnp.where(t >= n, rolled, 0.0)


def _shift_up(v, n):
    rows = v.shape[0]
    rolled = pltpu.roll(v, rows - n, 0)
    t = lax.broadcasted_iota(jnp.int32, v.shape, 0)
    return jnp.where(t < rows - n, rolled, 0.0)


def _conv_specs(s, base, cd, tc):
    zs = [pl.BlockSpec((s, tc), functools.partial(lambda j, off: (0, off + j), off=(base + n * cd) // tc))
          for n in range(3)]
    wspec = pl.BlockSpec((SUBLANES, tc), lambda j: (0, j))
    cspec = pl.BlockSpec((s, tc), lambda j: (0, j))
    return zs, wspec, cspec


def _conv_fwd(name, z, conv_w, base, cd):
    s = z.shape[0]
    tc = _tile(cd, 256)
    zs, wspec, cspec = _conv_specs(s, base, cd, tc)

    def body(h_ref, b_ref, c_ref, w_ref, o_ref):
        u = c_ref[...].astype(F32) * h_ref[...].astype(F32)
        y = w_ref[0:1, :] * _shift_down(u, 2) + w_ref[1:2, :] * _shift_down(u, 1) + w_ref[2:3, :] * u
        o_ref[...] = b_ref[...].astype(F32) * y

    return pl.pallas_call(
        body, name=name, grid=(cd // tc,), in_specs=zs + [wspec], out_specs=cspec,
        out_shape=jax.ShapeDtypeStruct((s, cd), F32), compiler_params=_params("parallel"),
    )(z, z, z, conv_w)


def _conv_bwd(name, z, conv_w, dc, base, cd):
    s = z.shape[0]
    tc = _tile(cd, 256)
    zs, wspec, cspec = _conv_specs(s, base, cd, tc)

    def body(h_ref, b_ref, c_ref, w_ref, dc_ref, dh_ref, db_ref, dcg_ref, dw_ref):
        hv, bv, cv = h_ref[...].astype(F32), b_ref[...].astype(F32), c_ref[...].astype(F32)
        u = cv * hv
        u1, u2 = _shift_down(u, 1), _shift_down(u, 2)
        w0, w1, w2 = w_ref[0:1, :], w_ref[1:2, :], w_ref[2:3, :]
        y = w0 * u2 + w1 * u1 + w2 * u
        dcv = dc_ref[...]
        db_ref[...] = (dcv * y).astype(db_ref.dtype)
        dy = dcv * bv
        du = w2 * dy + w1 * _shift_up(dy, 1) + w0 * _shift_up(dy, 2)
        dh_ref[...] = (du * cv).astype(dh_ref.dtype)
        dcg_ref[...] = (du * hv).astype(dcg_ref.dtype)
        g0 = jnp.sum(dy * u2, axis=0, keepdims=True)
        g1 = jnp.sum(dy * u1, axis=0, keepdims=True)
        g2 = jnp.sum(dy * u, axis=0, keepdims=True)
        r = lax.broadcasted_iota(jnp.int32, (SUBLANES, tc), 0)
        dw_ref[...] = jnp.where(r == 0, g0, jnp.where(r == 1, g1, jnp.where(r == 2, g2, 0.0)))

    return pl.pallas_call(
        body, name=name, grid=(cd // tc,), in_specs=zs + [wspec, cspec],
        out_specs=[cspec, cspec, cspec, wspec],
        out_shape=[jax.ShapeDtypeStruct((s, cd), BF16)] * 3 + [jax.ShapeDtypeStruct((SUBLANES, cd), F32)],
        compiler_params=_params("parallel"),
    )(z, z, z, conv_w, dc)


def _cat_norm_fwd(name, a, c, ga, gc):
    s, qd = a.shape
    cd = c.shape[1]
    tr = _row_tile(s, qd + cd)

    def body(a_ref, c_ref, ga_ref, gc_ref, o_ref):
        av, cv = a_ref[...], c_ref[...]
        ra = lax.rsqrt(jnp.mean(av * av, axis=-1, keepdims=True) + NORM_EPS)
        rc = lax.rsqrt(jnp.mean(cv * cv, axis=-1, keepdims=True) + NORM_EPS)
        o_ref[:, :qd] = (av * ra * ga_ref[...]).astype(o_ref.dtype)
        o_ref[:, qd:] = (cv * rc * gc_ref[...]).astype(o_ref.dtype)

    return pl.pallas_call(
        body, name=name, grid=(s // tr,),
        in_specs=[pl.BlockSpec((tr, qd), lambda i: (i, 0)), pl.BlockSpec((tr, cd), lambda i: (i, 0)),
                  pl.BlockSpec((1, qd), lambda i: (0, 0)), pl.BlockSpec((1, cd), lambda i: (0, 0))],
        out_specs=pl.BlockSpec((tr, qd + cd), lambda i: (i, 0)),
        out_shape=jax.ShapeDtypeStruct((s, qd + cd), BF16), compiler_params=_params("parallel"),
    )(a, c, ga, gc)


def _cat_norm_bwd(name, dcat, a, c, ga, gc):
    s, qd = a.shape
    cd = c.shape[1]
    tr = _row_tile(s, qd + cd)

    def one(dn, yv, gv):
        r = lax.rsqrt(jnp.mean(yv * yv, axis=-1, keepdims=True) + NORM_EPS)
        xhat = yv * r
        dxn = dn * gv
        return r * (dxn - xhat * jnp.mean(dxn * xhat, axis=-1, keepdims=True)), _sum_to_sublanes(dn * xhat)

    def body(d_ref, a_ref, c_ref, ga_ref, gc_ref, da_ref, dc_ref, dga_ref, dgc_ref):
        da, pa = one(d_ref[:, :qd], a_ref[...], ga_ref[...])
        dc, pc = one(d_ref[:, qd:], c_ref[...], gc_ref[...])
        da_ref[...] = da
        dc_ref[...] = dc

        @pl.when(pl.program_id(0) == 0)
        def _():
            dga_ref[...] = pa
            dgc_ref[...] = pc

        @pl.when(pl.program_id(0) > 0)
        def _():
            dga_ref[...] += pa
            dgc_ref[...] += pc

    ra = pl.BlockSpec((tr, qd), lambda i: (i, 0))
    rc = pl.BlockSpec((tr, cd), lambda i: (i, 0))
    return pl.pallas_call(
        body, name=name, grid=(s // tr,),
        in_specs=[pl.BlockSpec((tr, qd + cd), lambda i: (i, 0)), ra, rc,
                  pl.BlockSpec((1, qd), lambda i: (0, 0)), pl.BlockSpec((1, cd), lambda i: (0, 0))],
        out_specs=[ra, rc, pl.BlockSpec((SUBLANES, qd), lambda i: (0, 0)),
                   pl.BlockSpec((SUBLANES, cd), lambda i: (0, 0))],
        out_shape=[jax.ShapeDtypeStruct((s, qd), F32), jax.ShapeDtypeStruct((s, cd), F32),
                   jax.ShapeDtypeStruct((SUBLANES, qd), F32), jax.ShapeDtypeStruct((SUBLANES, cd), F32)],
        compiler_params=_params("arbitrary"),
    )(dcat, a, c, ga, gc)


def _adamw(name, w, g, m, v, emit_grad=False, layer=None, prev=None):
    shape = w.shape
    cols = shape[-1]
    rows = g.size // cols
    tr = _row_tile(rows, cols, budget=3 << 19)
    first = 0 if layer is None else layer * (rows // tr)
    bc1 = 1.0 - ADAM_B1 ** ADAM_STEP
    bc2 = 1.0 - ADAM_B2 ** ADAM_STEP
    n_out = 4 if emit_grad else 3

    def body(w_ref, g_ref, m_ref, v_ref, *rest):
        d_ref, nm_ref, nv_ref = rest[-n_out:][:3]
        gv = g_ref[...]
        mv = ADAM_B1 * m_ref[...] + (1.0 - ADAM_B1) * gv
        vv = ADAM_B2 * v_ref[...] + (1.0 - ADAM_B2) * (gv * gv)
        nm_ref[...] = mv
        nv_ref[...] = vv
        d_ref[...] = -ADAM_LR * ((mv / bc1) / (jnp.sqrt(vv / bc2) + ADAM_EPS) + ADAM_WD * w_ref[...])
        if emit_grad:
            rest[-1][...] = gv

    row = pl.BlockSpec((tr, cols), lambda i: (first + i, 0))
    g_row = pl.BlockSpec((tr, cols), lambda i: (i, 0))
    prev = tuple(prev) if prev is not None else ()
    total = w.size // cols
    outs = pl.pallas_call(
        body, name=name, grid=(rows // tr,), in_specs=[row, g_row, row, row] + [ANY_SPEC] * len(prev),
        out_specs=[row] * n_out, out_shape=[jax.ShapeDtypeStruct((total, cols), F32)] * n_out,
        input_output_aliases={4 + i: i for i in range(len(prev))}, compiler_params=_params("parallel"),
    )(w.reshape(total, cols), g.reshape(rows, cols), m.reshape(total, cols), v.reshape(total, cols),
      *(t.reshape(total, cols) for t in prev))
    return tuple(t.reshape(shape) for t in outs)


HBM_SPEC = pl.BlockSpec(memory_space=pltpu.HBM)


def _mesh_place():
    x, y, c = lax.axis_index("x"), lax.axis_index("y"), lax.axis_index("c")
    other_chips = [(1 - x, y), (x, 1 - y), (1 - x, 1 - y)]
    return x, y, c, other_chips


def _cast_into_slot(name, w, layer, chip, deps=()):
    _, r, cols = w.shape
    tr = _row_tile(r, cols, budget=BIG_BLOCK)

    def body(chip_ref, w_ref, *rest):
        o_ref = rest[-1]
        o_ref[...] = w_ref[...].astype(o_ref.dtype)

    return pl.pallas_call(
        body, name=name,
        grid_spec=pltpu.PrefetchScalarGridSpec(
            num_scalar_prefetch=1, grid=(r // tr,),
            in_specs=[pl.BlockSpec((None, tr, cols), lambda i, chip_ref: (layer, i, 0))] + [ANY_SPEC] * len(deps),
            out_specs=pl.BlockSpec((None, tr, cols), lambda i, chip_ref: (chip_ref[0], i, 0))),
        out_shape=jax.ShapeDtypeStruct((N_CHIPS, r, cols), BF16), compiler_params=_params("parallel"),
    )(chip, w, *deps)


SEM_SPEC = pl.BlockSpec(memory_space=pltpu.SEMAPHORE)
SPLIT_COPY = pltpu.CompilerParams(has_side_effects=pltpu.SideEffectType.DATAFLOW_SIDE_EFFECTING)
N_OTHER = N_CHIPS - 1
TOKEN_SPEC = pl.BlockSpec(memory_space=pltpu.VMEM)
TOKEN_SHAPE = jax.ShapeDtypeStruct((SUBLANES, LANES), F32)


def _in_hbm(arr):
    return pltpu.with_memory_space_constraint(arr, pltpu.HBM)


def _half_rows(ref, chip_idx, core):
    r2 = ref.shape[1] // 2
    return ref.at[chip_idx, pl.ds(core * r2, r2), :]


def _gather_start(name, fulls, after):
    na = len(fulls)

    def body(*refs):
        f_refs = refs[na + 1:2 * na + 1]
        send_sems, recv_sems = refs[2 * na + 1:3 * na + 1], refs[3 * na + 1:4 * na + 1]
        token = refs[4 * na + 1]
        x, y, c, chips = _mesh_place()
        for a in range(na):
            mine = _half_rows(f_refs[a], 2 * x + y, c)
            for j, (cx, cy) in enumerate(chips):
                pltpu.make_async_remote_copy(
                    src_ref=mine, dst_ref=mine, send_sem=send_sems[a].at[j], recv_sem=recv_sems[a].at[j],
                    device_id=(cx, cy, c), device_id_type=MESH).start()
        token[...] = jnp.zeros_like(token)

    outs = pl.pallas_call(
        body, name=name, in_specs=[HBM_SPEC] * na + [ANY_SPEC],
        out_specs=[HBM_SPEC] * na + [SEM_SPEC] * (2 * na) + [TOKEN_SPEC],
        out_shape=[pltpu.HBM(f.shape, f.dtype) for f in fulls] + [pltpu.SemaphoreType.DMA((N_OTHER,))] * (2 * na)
        + [TOKEN_SHAPE],
        input_output_aliases={a: a for a in range(na)}, compiler_params=SPLIT_COPY,
    )(*[_in_hbm(f) for f in fulls], after)
    return list(outs[:na]), list(outs[na:2 * na]), list(outs[2 * na:3 * na]), outs[3 * na]


def _gather_pass_on(name, full, recv_sems, after):
    def body(f_in, recv_sems, after_ref, f_ref, d2d_send, d2d_recv):
        x, y, c, chips = _mesh_place()
        for j, (cx, cy) in enumerate(chips):
            blk = _half_rows(f_ref, 2 * cx + cy, c)
            pltpu.make_async_remote_copy(
                src_ref=blk, dst_ref=blk, send_sem=d2d_send.at[j], recv_sem=recv_sems.at[j],
                device_id=(cx, cy, c), device_id_type=MESH).wait_recv()
            pltpu.make_async_remote_copy(
                src_ref=blk, dst_ref=blk, send_sem=d2d_send.at[j], recv_sem=d2d_recv.at[j],
                device_id=(x, y, 1 - c), device_id_type=MESH).start()

    return pl.pallas_call(
        body, name=name, in_specs=[HBM_SPEC, SEM_SPEC, ANY_SPEC], out_specs=[HBM_SPEC, SEM_SPEC, SEM_SPEC],
        out_shape=[pltpu.HBM(full.shape, full.dtype)] + [pltpu.SemaphoreType.DMA((N_OTHER,))] * 2,
        input_output_aliases={0: 0}, compiler_params=SPLIT_COPY,
    )(full, recv_sems, after)


def _gather_arrive(name, full, ici_send, d2d_send, d2d_recv, after):
    def body(f_in, ici_send, d2d_send, d2d_recv, after_ref, f_ref):
        x, y, c, chips = _mesh_place()
        for j, (cx, cy) in enumerate(chips):
            mine = _half_rows(f_ref, 2 * x + y, c)
            passed = _half_rows(f_ref, 2 * cx + cy, c)
            theirs = _half_rows(f_ref, 2 * cx + cy, 1 - c)
            pltpu.make_async_remote_copy(
                src_ref=mine, dst_ref=mine, send_sem=ici_send.at[j], recv_sem=d2d_recv.at[j],
                device_id=(cx, cy, c), device_id_type=MESH).wait_send()
            pltpu.make_async_remote_copy(
                src_ref=passed, dst_ref=passed, send_sem=d2d_send.at[j], recv_sem=d2d_recv.at[j],
                device_id=(x, y, 1 - c), device_id_type=MESH).wait_send()
            pltpu.make_async_remote_copy(
                src_ref=theirs, dst_ref=theirs, send_sem=d2d_send.at[j], recv_sem=d2d_recv.at[j],
                device_id=(x, y, 1 - c), device_id_type=MESH).wait_recv()

    return pl.pallas_call(
        body, name=name, in_specs=[HBM_SPEC, SEM_SPEC, SEM_SPEC, SEM_SPEC, ANY_SPEC], out_specs=HBM_SPEC,
        out_shape=pltpu.HBM(full.shape, full.dtype), input_output_aliases={0: 0}, compiler_params=SPLIT_COPY,
    )(full, ici_send, d2d_send, d2d_recv, after)


def _gather_taps(conv_w):
    def body(cw_ref, cwf_ref, send_sems, recv_sems, local_sem):
        x, y, c, chips = _mesh_place()
        k_me = 2 * x + y
        local = pltpu.make_async_copy(cw_ref, cwf_ref.at[k_me], local_sem)
        local.start()
        copies = [pltpu.make_async_remote_copy(
            src_ref=cw_ref, dst_ref=cwf_ref.at[k_me], send_sem=send_sems.at[j], recv_sem=recv_sems.at[j],
            device_id=(cx, cy, c), device_id_type=MESH) for j, (cx, cy) in enumerate(chips)]
        for cp in copies:
            cp.start()
        for j, (cx, cy) in enumerate(chips):
            pltpu.make_async_remote_copy(
                src_ref=cw_ref, dst_ref=cwf_ref.at[2 * cx + cy], send_sem=send_sems.at[j], recv_sem=recv_sems.at[j],
                device_id=(cx, cy, c), device_id_type=MESH).wait_recv()
        for cp in copies:
            cp.wait_send()
        local.wait()

    return pl.pallas_call(
        body, name="gather_taps", in_specs=[HBM_SPEC], out_specs=HBM_SPEC,
        out_shape=jax.ShapeDtypeStruct((N_CHIPS,) + conv_w.shape, conv_w.dtype),
        scratch_shapes=[pltpu.SemaphoreType.DMA((N_OTHER,))] * 2 + [pltpu.SemaphoreType.DMA],
    )(conv_w)


def _sibling_half(g_ref, c):
    r2 = g_ref.shape[1] // 2
    return g_ref.at[:, pl.ds((1 - c) * r2, r2), :]


def _swap_copy(g_ref, land_ref, send_sems, recv_sems, a):
    x, y, c, _ = _mesh_place()
    return pltpu.make_async_remote_copy(
        src_ref=_sibling_half(g_ref, c), dst_ref=land_ref, send_sem=send_sems.at[a], recv_sem=recv_sems.at[a],
        device_id=(x, y, 1 - c), device_id_type=MESH)


def _swap_start(name, gs):
    n = len(gs)

    def body(*refs):
        g_refs, land_refs = refs[n:2 * n], refs[2 * n:3 * n]
        send_sems, recv_sems, token = refs[3 * n:]
        for a in range(n):
            _swap_copy(g_refs[a], land_refs[a], send_sems, recv_sems, a).start()
        token[...] = jnp.zeros_like(token)

    outs = pl.pallas_call(
        body, name=name, in_specs=[HBM_SPEC] * n,
        out_specs=[HBM_SPEC] * (2 * n) + [SEM_SPEC, SEM_SPEC, TOKEN_SPEC],
        out_shape=[pltpu.HBM(g.shape, g.dtype) for g in gs]
        + [pltpu.HBM((g.shape[0], g.shape[1] // 2, g.shape[2]), g.dtype) for g in gs]
        + [pltpu.SemaphoreType.DMA((n,)), pltpu.SemaphoreType.DMA((n,)), TOKEN_SHAPE],
        input_output_aliases={a: a for a in range(n)}, compiler_params=SPLIT_COPY,
    )(*[_in_hbm(g) for g in gs])
    return list(outs[:n]), list(outs[n:2 * n]), outs[2 * n], outs[2 * n + 1], outs[2 * n + 2]


def _swap_wait(name, gs, lands, send_sems, recv_sems, after):
    n = len(gs)

    def body(*refs):
        send_sems, recv_sems = refs[2 * n], refs[2 * n + 1]
        g_refs, land_refs = refs[2 * n + 3:3 * n + 3], refs[3 * n + 3:]
        for a in range(n):
            copy = _swap_copy(g_refs[a], land_refs[a], send_sems, recv_sems, a)
            copy.wait_send()
            copy.wait_recv()

    outs = pl.pallas_call(
        body, name=name, in_specs=[HBM_SPEC] * (2 * n) + [SEM_SPEC, SEM_SPEC, ANY_SPEC],
        out_specs=[HBM_SPEC] * (2 * n),
        out_shape=[pltpu.HBM(t.shape, t.dtype) for t in list(gs) + list(lands)],
        input_output_aliases={a: a for a in range(2 * n)}, compiler_params=SPLIT_COPY,
    )(*gs, *lands, send_sems, recv_sems, after)
    return list(outs[:n]), list(outs[n:])


def _add_core_halves(name, g, sib, core):
    nb, r, cols = g.shape
    r2 = r // 2
    tr = _row_tile(r2, cols, itemsize=2, budget=BIG_BLOCK)
    nrt = r2 // tr

    def body(core_ref, g_ref, s_ref, o_ref):
        o_ref[...] = (g_ref[...].astype(F32) + s_ref[...].astype(F32)).astype(o_ref.dtype)

    return pl.pallas_call(
        body, name=name,
        grid_spec=pltpu.PrefetchScalarGridSpec(
            num_scalar_prefetch=1, grid=(nb, nrt),
            in_specs=[pl.BlockSpec((None, tr, cols), lambda k, i, core_ref: (k, core_ref[0] * nrt + i, 0)),
                      pl.BlockSpec((None, tr, cols), lambda k, i, core_ref: (k, i, 0))],
            out_specs=pl.BlockSpec((None, tr, cols), lambda k, i, core_ref: (k, i, 0))),
        out_shape=jax.ShapeDtypeStruct((nb, r2, cols), BF16), compiler_params=_params("parallel", "parallel"),
    )(core, g, sib)


def _scatter_copies(h_refs, land_refs, send_sems, recv_sems):
    x, y, c, chips = _mesh_place()
    return [pltpu.make_async_remote_copy(
        src_ref=h_ref.at[2 * cx + cy], dst_ref=land_ref.at[j],
        send_sem=send_sems.at[a * N_OTHER + j], recv_sem=recv_sems.at[a * N_OTHER + j],
        device_id=(cx, cy, c), device_id_type=MESH)
        for a, (h_ref, land_ref) in enumerate(zip(h_refs, land_refs)) for j, (cx, cy) in enumerate(chips)]


def _scatter_start(name, hs):
    n = len(hs)

    def body(*refs):
        h_refs, land_refs = refs[n:2 * n], refs[2 * n:3 * n]
        send_sems, recv_sems, token = refs[3 * n:]
        for copy in _scatter_copies(h_refs, land_refs, send_sems, recv_sems):
            copy.start()
        token[...] = jnp.zeros_like(token)

    outs = pl.pallas_call(
        body, name=name, in_specs=[HBM_SPEC] * n,
        out_specs=[HBM_SPEC] * (2 * n) + [SEM_SPEC, SEM_SPEC, TOKEN_SPEC],
        out_shape=[pltpu.HBM(h.shape, h.dtype) for h in hs]
        + [pltpu.HBM((N_OTHER,) + h.shape[1:], h.dtype) for h in hs]
        + [pltpu.SemaphoreType.DMA((n * N_OTHER,)), pltpu.SemaphoreType.DMA((n * N_OTHER,)), TOKEN_SHAPE],
        input_output_aliases={a: a for a in range(n)}, compiler_params=SPLIT_COPY,
    )(*[_in_hbm(h) for h in hs])
    return list(outs[:n]), list(outs[n:2 * n]), outs[2 * n], outs[2 * n + 1], outs[2 * n + 2]


def _scatter_wait(name, hs, lands, send_sems, recv_sems, after):
    afters = tuple(after) if isinstance(after, (tuple, list)) else (after,)
    n = len(hs)

    def body(*refs):
        send_sems, recv_sems = refs[2 * n], refs[2 * n + 1]
        h_refs, land_refs = refs[-2 * n:-n], refs[-n:]
        for copy in _scatter_copies(h_refs, land_refs, send_sems, recv_sems):
            copy.wait_send()
            copy.wait_recv()

    outs = pl.pallas_call(
        body, name=name, in_specs=[HBM_SPEC] * (2 * n) + [SEM_SPEC, SEM_SPEC] + [ANY_SPEC] * len(afters),
        out_specs=[HBM_SPEC] * (2 * n),
        out_shape=[pltpu.HBM(t.shape, t.dtype) for t in list(hs) + list(lands)],
        input_output_aliases={a: a for a in range(2 * n)}, compiler_params=SPLIT_COPY,
    )(*hs, *lands, send_sems, recv_sems, *afters)
    return list(outs[:n]), list(outs[n:])


def _sum_chips(name, hs, rcv, core, chip, layer, n_layers, prev):
    _, r2, cols = hs.shape
    tr = _row_tile(r2, cols, budget=BIG_BLOCK)
    nrt = r2 // tr

    def body(core_ref, chip_ref, h_ref, r_ref, *rest):
        o_ref = rest[-1]
        acc = h_ref[...].astype(F32)
        for j in range(N_CHIPS - 1):
            acc = acc + r_ref[j].astype(F32)
        o_ref[...] = acc

    in_specs = [pl.BlockSpec((None, tr, cols), lambda i, core_ref, chip_ref: (chip_ref[0], i, 0)),
                pl.BlockSpec((N_CHIPS - 1, tr, cols), lambda i, core_ref, chip_ref: (0, i, 0))]
    args = [core, chip, hs, rcv]
    aliases = {}
    if prev is not None:
        in_specs.append(pl.BlockSpec(memory_space=pl.ANY))
        args.append(prev)
        aliases = {4: 0}
    return pl.pallas_call(
        body, name=name,
        grid_spec=pltpu.PrefetchScalarGridSpec(
            num_scalar_prefetch=2, grid=(nrt,), in_specs=in_specs,
            out_specs=pl.BlockSpec((None, tr, cols), lambda i, core_ref, chip_ref: (layer, core_ref[0] * nrt + i, 0))),
        out_shape=jax.ShapeDtypeStruct((n_layers, 2 * r2, cols), F32), input_output_aliases=aliases,
        compiler_params=_params("parallel"),
    )(*args)


def _join_copy(t_ref, send_sems, recv_sems, a):
    x, y, c, _ = _mesh_place()
    r2 = t_ref.shape[1] // 2
    mine = t_ref.at[:, pl.ds(c * r2, r2), :]
    return pltpu.make_async_remote_copy(
        src_ref=mine, dst_ref=mine, send_sem=send_sems.at[a], recv_sem=recv_sems.at[a],
        device_id=(x, y, 1 - c), device_id_type=MESH)


def _join_start(name, ts, deps=()):
    n, nd = len(ts), len(deps)

    def body(*refs):
        t_refs = refs[n + nd:2 * n + nd]
        send_sems, recv_sems = refs[2 * n + nd:]
        for a in range(n):
            _join_copy(t_refs[a], send_sems, recv_sems, a).start()

    outs = pl.pallas_call(
        body, name=name, in_specs=[HBM_SPEC] * n + [ANY_SPEC] * nd, out_specs=[HBM_SPEC] * n + [SEM_SPEC, SEM_SPEC],
        out_shape=[pltpu.HBM(t.shape, t.dtype) for t in ts] + [pltpu.SemaphoreType.DMA((n,))] * 2,
        input_output_aliases={a: a for a in range(n)}, compiler_params=SPLIT_COPY,
    )(*[_in_hbm(t) for t in ts], *deps)
    return list(outs[:n]), outs[n], outs[n + 1]


def _join_wait(name, t, a, send_sems, recv_sems, after):
    def body(t_in, send_sems, recv_sems, after_ref, t_ref):
        copy = _join_copy(t_ref, send_sems, recv_sems, a)
        copy.wait_send()
        copy.wait_recv()

    return pl.pallas_call(
        body, name=name, in_specs=[HBM_SPEC, SEM_SPEC, SEM_SPEC, ANY_SPEC], out_specs=HBM_SPEC,
        out_shape=pltpu.HBM(t.shape, t.dtype), input_output_aliases={0: 0}, compiler_params=SPLIT_COPY,
    )(t, send_sems, recv_sems, after)


def _allreduce_small(p):
    n, _, w = p.shape

    def body(p_ref, o_ref, buf, send_sems, recv_sems):
        x, y, c, _ = _mesh_place()
        me = 4 * x + 2 * y + c
        buf[me] = jnp.sum(p_ref[...], axis=1)
        copies = []
        for pat in range(1, N_DEV):
            fx, fy, fc = (pat >> 2) & 1, (pat >> 1) & 1, pat & 1
            copies.append(pltpu.make_async_remote_copy(
                src_ref=buf.at[me], dst_ref=buf.at[me], send_sem=send_sems.at[pat - 1], recv_sem=recv_sems.at[pat - 1],
                device_id=(x ^ fx, y ^ fy, c ^ fc), device_id_type=MESH))
        for cp in copies:
            cp.start()
        for cp in copies:
            cp.wait()
        acc = buf[0]
        for dev in range(1, N_DEV):
            acc = acc + buf[dev]
        o_ref[...] = acc

    return pl.pallas_call(
        body, name="allreduce_small", in_specs=[pl.BlockSpec(memory_space=pltpu.VMEM)],
        out_specs=pl.BlockSpec(memory_space=pltpu.VMEM), out_shape=jax.ShapeDtypeStruct((n, w), F32),
        scratch_shapes=[pltpu.VMEM((N_DEV, n, w), F32), pltpu.SemaphoreType.DMA((N_DEV - 1,)),
                        pltpu.SemaphoreType.DMA((N_DEV - 1,))],
    )(p)


class _WeightFeed:
    def __init__(self):
        self.fulls, self.ici_send, self.ici_recv, self.d2d = [], [], [], []

    def start(self, name, fulls, after):
        started, send, recv, token = _gather_start(name, fulls, after)
        self.fulls += started
        self.ici_send += send
        self.ici_recv += recv
        self.d2d += [None] * len(fulls)
        self.token = token
        return token

    def _pass_on(self, k, after):
        if k == 0:
            after = self.token
        if k < len(self.fulls) and self.d2d[k] is None:
            self.fulls[k], send, recv = _gather_pass_on(f"gather_pass_{k}", self.fulls[k], self.ici_recv[k], after)
            self.d2d[k] = (send, recv)

    def take(self, k, after):
        self._pass_on(k, after)
        self.fulls[k] = _gather_arrive(f"gather_arrive_{k}", self.fulls[k], self.ici_send[k], *self.d2d[k], after)
        return self.fulls[k]


def _ffn_forward(tag, x, h, g_post, next_gain, feed, k):
    s, d = x.shape
    gu_w = feed.take(k, h)
    gu, a = _ffn_up(f"{tag}_up", h, gu_w)
    dn_w = feed.take(k + 1, a).reshape(-1, d)
    f = dn_w.shape[0]
    tm, tn = _tile(s, 1024), _tile(d, 512)
    y = _mm(f"{tag}_down", a, dn_w, mode="nn", grid=(s // tm, d // tn),
            a_spec=pl.BlockSpec((tm, f), lambda i, j: (i, 0)),
            b_spec=pl.BlockSpec((f, tn), lambda i, j: (0, j)),
            o_spec=pl.BlockSpec((tm, tn), lambda i, j: (i, j)),
            out_shape=jax.ShapeDtypeStruct((s, d), F32))
    x_new, h_next = _res_norm(f"{tag}_post", x, y, g_post, FFN_RESIDUAL_WEIGHT, next_gain)
    return x_new, h_next, (x, h, gu, a, y)


class _GradReduce:
    def __init__(self, core, chip, n_layers, per_layer=()):
        self.core, self.chip, self.n_layers, self.per_layer = core, chip, n_layers, per_layer
        self.state = {}
        self.bufs = {}
        self.scatter_tokens = {}

    def start(self, kinds, layer, gs):
        gs, lands, send, recv, token = _swap_start(f"swap_start_{kinds[0]}_{layer}", gs)
        self.state[kinds, layer] = (gs, lands, send, recv)
        return token

    def exchange(self, kinds, layer, after):
        tag = f"{kinds[0]}_{layer}"
        gs, sibs = _swap_wait(f"swap_wait_{tag}", *self.state[kinds, layer], after)
        hs = [_add_core_halves(f"add_cores_{k}_{layer}", g, sib, self.core) for k, g, sib in zip(kinds, gs, sibs)]
        hs, lands, send, recv, token = _scatter_start(f"scatter_start_{tag}", hs)
        self.state[kinds, layer] = (hs, lands, send, recv)
        self.scatter_tokens[kinds, layer] = token
        return token

    def finish(self, kinds, layer, after):
        tag = f"{kinds[0]}_{layer}"
        hs, rcvs = _scatter_wait(f"scatter_wait_{tag}", *self.state.pop((kinds, layer)), after)
        for k, h, rcv in zip(kinds, hs, rcvs):
            if k in self.per_layer:
                last = self.bufs[k, layer] = _sum_chips(f"sum_chips_{k}_{layer}", h, rcv, self.core, self.chip,
                                                        0, 1, None)
            else:
                last = self.bufs[k] = _sum_chips(f"sum_chips_{k}_{layer}", h, rcv, self.core, self.chip, layer,
                                                 self.n_layers, self.bufs.get(k))
        return last


def _ffn_backward(tag, dx_new, saved, g_pre, g_post, gu_w, dn_w, red, kinds, layer, deps, head, following,
                  last=None):
    x, h, gu, a, y = saved
    s, d = x.shape
    nb, fs = gu_w.shape[0], gu_w.shape[2]
    f = dn_w.shape[0]
    fr = f // nb
    dy, dg_post = head or _norm_bwd(f"{tag}_post_bwd", dx_new, y, g_post, FFN_RESIDUAL_WEIGHT, None, BF16)
    dgu = _ffn_dact(f"{tag}_dact", dy, dn_w, gu, deps)
    dgu4 = dgu.reshape(nb, s, fs)
    tm, tw = _tile(d, 1024), _tile(fs, 1408)
    nw = fs // tw
    tn = _tile(d, 1024)
    ts, td = _tile(s, 1024), _tile(d, 1024)

    def gate_up_gradient(deps):
        return _mm(f"{tag}_dwgu", h, dgu4, mode="tn", grid=(nb, nw, d // tm),
                   a_spec=pl.BlockSpec((s, tm), lambda k, j, i: (0, i)),
                   b_spec=pl.BlockSpec((None, s, tw), lambda k, j, i: (k, 0, j)),
                   o_spec=pl.BlockSpec((None, tm, tw), lambda k, j, i: (k, i, j)),
                   out_shape=jax.ShapeDtypeStruct((nb, d, fs), BF16), deps=deps)

    def down_gradient(deps):
        return _mm(f"{tag}_dwd", a, dy, mode="tn", grid=(nb, d // tn),
                   a_spec=pl.BlockSpec((s, fr), lambda i, j: (0, i)),
                   b_spec=pl.BlockSpec((s, tn), lambda i, j: (0, j)),
                   o_spec=pl.BlockSpec((None, fr, tn), lambda i, j: (i, 0, j)),
                   out_shape=jax.ShapeDtypeStruct((nb, fr, d), BF16), deps=deps)

    def input_gradient(deps):
        dh = _mm(f"{tag}_dh", dgu4, gu_w, mode="nt", grid=(s // ts, d // td, nb),
                 a_spec=pl.BlockSpec((None, ts, fs), lambda i, j, k: (k, i, 0)),
                 b_spec=pl.BlockSpec((None, td, fs), lambda i, j, k: (k, j, 0)),
                 o_spec=pl.BlockSpec((ts, td), lambda i, j, k: (i, j)),
                 out_shape=jax.ShapeDtypeStruct((s, d), F32), nk=nb, acc_shape=(ts, td), deps=deps)
        return _norm_bwd(f"{tag}_pre_bwd", dh, x, g_pre, 1.0, dx_new, F32, following)

    if last is None:
        started = red.start(kinds, layer, [gate_up_gradient(()), down_gradient(())])
        dx, dg_pre, *next_head = input_gradient((started,))
    else:
        dx, dg_pre, *next_head = input_gradient(())
        first = red.start(kinds[:1], layer, [gate_up_gradient((last(dg_pre, dg_post),))])
        second = red.start(kinds[1:], layer, [down_gradient((first,))])
        red.exchange(kinds[:1], layer, second)
    return dx, dg_pre, dg_post, tuple(next_head) or None


def _mixer_forward(tag, x, h, gains, next_gain, feed, k, conv_taps, dims):
    qd, kvd, cd = dims
    s, d = x.shape
    _, g_a, g_c, g_post = gains
    win_w = feed.take(k, h)
    nb, cw = win_w.shape[0], win_w.shape[2]
    tm = _tile(s, 1024)
    z = _mm(f"{tag}_in", h, win_w, mode="nn", grid=(nb, s // tm),
            a_spec=pl.BlockSpec((tm, d), lambda j, i: (i, 0)),
            b_spec=pl.BlockSpec((None, d, cw), lambda j, i: (j, 0, 0)),
            o_spec=pl.BlockSpec((tm, cw), lambda j, i: (i, j)),
            out_shape=jax.ShapeDtypeStruct((s, nb * cw), BF16))
    a, lse = _attn_fwd(f"{tag}_attn", z, qd, kvd)
    c = _conv_fwd(f"{tag}_conv", z, conv_taps, qd + 2 * kvd, cd)
    cat = _cat_norm_fwd(f"{tag}_cat", a, c, g_a, g_c)
    wout_w = feed.take(k + 1, cat).reshape(-1, d)
    mw = qd + cd
    tn = _tile(d, 1024)
    mixed = _mm(f"{tag}_out", cat, wout_w, mode="nn", grid=(s // tm, d // tn),
                a_spec=pl.BlockSpec((tm, mw), lambda i, j: (i, 0)),
                b_spec=pl.BlockSpec((mw, tn), lambda i, j: (0, j)),
                o_spec=pl.BlockSpec((tm, tn), lambda i, j: (i, j)),
                out_shape=jax.ShapeDtypeStruct((s, d), F32))
    x_new, h_next = _res_norm(f"{tag}_post", x, mixed, g_post, 1.0, next_gain)
    return x_new, h_next, (x, h, z, a, lse, c, cat, mixed)


def _mixer_backward(tag, dx_new, saved, gains, win_w, conv_taps, wout_w, dims, red, kinds, layer, deps, head,
                    following):
    qd, kvd, cd = dims
    x, h, z, a, lse, c, cat, mixed = saved
    s, d = x.shape
    nb, cw = win_w.shape[0], win_w.shape[2]
    g_pre, g_a, g_c, g_post = gains
    mw = qd + cd
    dmixed, dg_post = head or _norm_bwd(f"{tag}_post_bwd", dx_new, mixed, g_post, 1.0, None, BF16)
    tm, tn = _tile(s, 1024), _tile(mw, 1024)
    dcat = _mm(f"{tag}_dcat", dmixed, wout_w, mode="nt", grid=(s // tm, mw // tn),
               a_spec=pl.BlockSpec((tm, d), lambda i, j: (i, 0)),
               b_spec=pl.BlockSpec((tn, d), lambda i, j: (j, 0)),
               o_spec=pl.BlockSpec((tm, tn), lambda i, j: (i, j)),
               out_shape=jax.ShapeDtypeStruct((s, mw), F32), deps=deps)
    wr = mw // nb
    td = _tile(d, 1024)
    d_wout = _mm(f"{tag}_dwout", cat, dmixed, mode="tn", grid=(nb, d // td),
                 a_spec=pl.BlockSpec((s, wr), lambda i, j: (0, i)),
                 b_spec=pl.BlockSpec((s, td), lambda i, j: (0, j)),
                 o_spec=pl.BlockSpec((None, wr, td), lambda i, j: (i, 0, j)),
                 out_shape=jax.ShapeDtypeStruct((nb, wr, d), BF16))
    da, dc, dg_a, dg_c = _cat_norm_bwd(f"{tag}_cat_bwd", dcat, a, c, g_a, g_c)
    dhc, dbg, dcg, d_taps = _conv_bwd(f"{tag}_conv_bwd", z, conv_taps, dc, qd + 2 * kvd, cd)
    dq, dk, dv = _attn_bwd(f"{tag}_attn_bwd", z, a, lse, da, qd, kvd)
    dz = jnp.concatenate([dq, dk, dv, dhc, dbg, dcg], axis=1)
    th = _tile(d, 1024)
    d_win = _mm(f"{tag}_dwin", h, dz, mode="tn", grid=(nb, d // th),
                a_spec=pl.BlockSpec((s, th), lambda k, i: (0, i)),
                b_spec=pl.BlockSpec((s, cw), lambda k, i: (0, k)),
                o_spec=pl.BlockSpec((None, th, cw), lambda k, i: (k, i, 0)),
                out_shape=jax.ShapeDtypeStruct((nb, d, cw), BF16))
    started = (red.start(kinds, layer, [d_win, d_wout]),)
    dh = _mm(f"{tag}_dh", dz, win_w, mode="nt", grid=(s // tm, d // td, nb),
             a_spec=pl.BlockSpec((tm, cw), lambda i, j, k: (i, k)),
             b_spec=pl.BlockSpec((None, td, cw), lambda i, j, k: (k, j, 0)),
             o_spec=pl.BlockSpec((tm, td), lambda i, j, k: (i, j)),
             out_shape=jax.ShapeDtypeStruct((s, d), F32), nk=nb, acc_shape=(tm, td), deps=started)
    dx, dg_pre, *next_head = _norm_bwd(f"{tag}_pre_bwd", dh, x, g_pre, 1.0, dx_new, F32, following)
    return dx, d_taps, (dg_pre, dg_a, dg_c, dg_post), tuple(next_head) or None


def kernel(x, ffn1_norm_pre, ffn1_w_gate_up, ffn1_w_down, ffn1_norm_post, mix_norm_pre, w_in, conv_w, attn_out_norm, conv_out_norm, w_out, mix_norm_post, ffn2_norm_pre, ffn2_w_gate_up, ffn2_w_down, ffn2_norm_post, loss_target, m_ffn1_norm_pre, m_ffn1_w_gate_up, m_ffn1_w_down, m_ffn1_norm_post, m_mix_norm_pre, m_w_in, m_conv_w, m_attn_out_norm, m_conv_out_norm, m_w_out, m_mix_norm_post, m_ffn2_norm_pre, m_ffn2_w_gate_up, m_ffn2_w_down, m_ffn2_norm_post, v_ffn1_norm_pre, v_ffn1_w_gate_up, v_ffn1_w_down, v_ffn1_norm_post, v_mix_norm_pre, v_w_in, v_conv_w, v_attn_out_norm, v_conv_out_norm, v_w_out, v_mix_norm_post, v_ffn2_norm_pre, v_ffn2_w_gate_up, v_ffn2_w_down, v_ffn2_norm_post):
    _, s, d = x.shape
    n_layers = ffn1_norm_pre.shape[0]
    qd = attn_out_norm.shape[1]
    cd = conv_out_norm.shape[1]
    kvd = qd // Q_PER_KV
    dims = (qd, kvd, cd)
    assert N_CHIPS * w_in.shape[2] == qd + 2 * kvd + 3 * cd and qd + cd == N_CHIPS * w_out.shape[1]
    assert 2 * d <= SMALL_ROWS * LANES * SUBLANES
    chip = 2 * lax.axis_index("x") + lax.axis_index("y")
    chip_arr = chip.astype(jnp.int32).reshape(1)
    core = lax.axis_index("c").astype(jnp.int32).reshape(1)
    kinds = ("gu1", "dn1", "win", "wout", "gu2", "dn2")

    big = (ffn1_w_gate_up, ffn1_w_down, w_in, w_out, ffn2_w_gate_up, ffn2_w_down)
    nk = len(kinds)
    taps_all = _gather_taps(conv_w)
    feed = _WeightFeed()
    order = [(k, w, layer) for layer in range(n_layers) for k, w in zip(kinds, big)]
    k, w, layer = order[0]
    token = feed.start("gather_start_first", [_cast_into_slot(f"cast_{k}_{layer}", w, layer, chip_arr)], taps_all)
    feed.start("gather_start_rest", [_cast_into_slot(f"cast_{k}_{layer}", w, layer, chip_arr, (token,))
                                     for k, w, layer in order[1:]], token)
    taps = jnp.transpose(taps_all, (1, 2, 0, 3)).reshape(n_layers, CONV_WIDTH, cd)
    taps = jnp.pad(taps, ((0, 0), (0, SUBLANES - CONV_WIDTH), (0, 0)))

    def gain(g, layer):
        return g[layer][None, :]

    xs = x[0]
    hs = _norm_fwd("l0_ffn1_norm", xs, gain(ffn1_norm_pre, 0))
    saved = []
    for layer in range(n_layers):
        t = f"l{layer}"
        k0 = layer * nk
        xs, hs, s1 = _ffn_forward(f"{t}_ffn1", xs, hs, gain(ffn1_norm_post, layer), gain(mix_norm_pre, layer), feed, k0)
        mix_gains = (gain(mix_norm_pre, layer), gain(attn_out_norm, layer), gain(conv_out_norm, layer), gain(mix_norm_post, layer))
        xs, hs, s2 = _mixer_forward(f"{t}_mix", xs, hs, mix_gains, gain(ffn2_norm_pre, layer), feed, k0 + 2,
                                    taps[layer], dims)
        following = gain(ffn1_norm_pre, layer + 1) if layer + 1 < n_layers else None
        xs, hs, s3 = _ffn_forward(f"{t}_ffn2", xs, hs, gain(ffn2_norm_post, layer), following, feed, k0 + 4)
        saved.append((s1, s2, s3, mix_gains))
    wts = {k: [feed.fulls[layer * nk + i] for layer in range(n_layers)] for i, k in enumerate(kinds)}
    for k in ("dn1", "wout", "dn2"):
        wts[k] = [w.reshape(-1, d) for w in wts[k]]
    top = n_layers - 1
    dxs, loss_part, *head = _loss_head("loss_head", xs, loss_target[0],
                                       (saved[top][2][4], gain(ffn2_norm_post, top), FFN_RESIDUAL_WEIGHT))
    loss = lax.psum(jnp.sum(loss_part), ("x", "y", "c"))

    red = _GradReduce(core, chip_arr, n_layers, per_layer=("gu1", "dn1"))
    small = [None] * n_layers
    flow = {"deps": (), "in_flight": None}

    def between(dx, group):
        after = dx
        if flow["in_flight"] is not None:
            after = red.finish(*flow["in_flight"], after)
        flow["deps"] = (red.exchange(*group, after),)
        flow["in_flight"] = group

    head = tuple(head)
    for layer in reversed(range(n_layers)):
        t = f"l{layer}"
        s1, s2, s3, mix_gains = saved[layer]
        after_ffn2 = (s2[7], mix_gains[3], 1.0)
        after_mix = (s1[4], gain(ffn1_norm_post, layer), FFN_RESIDUAL_WEIGHT)
        after_ffn1 = ((saved[layer - 1][2][4], gain(ffn2_norm_post, layer - 1), FFN_RESIDUAL_WEIGHT)
                      if layer > 0 else None)
        dxs, p_pre2, p_post2, head = _ffn_backward(
            f"{t}_ffn2", dxs, s3, gain(ffn2_norm_pre, layer), gain(ffn2_norm_post, layer),
            wts["gu2"][layer], wts["dn2"][layer], red, ("gu2", "dn2"), layer, flow["deps"], head, after_ffn2)
        between(dxs, (("gu2", "dn2"), layer))
        dxs, p_taps, (p_mpre, p_a, p_c, p_mpost), head = _mixer_backward(
            f"{t}_mix", dxs, s2, mix_gains, wts["win"][layer], taps[layer], wts["wout"][layer], dims,
            red, ("win", "wout"), layer, flow["deps"], head, after_mix)
        between(dxs, (("win", "wout"), layer))
        def pack_small(p_pre1, p_post1):
            tap_rows = jnp.zeros((CONV_WIDTH, SUBLANES, d), F32).at[:, 0, :cd].set(p_taps[:CONV_WIDTH])
            rows = [p_pre1, p_post1, p_mpre, jnp.concatenate([p_a, p_c], axis=1), p_mpost, p_pre2, p_post2]
            rows = jnp.concatenate([jnp.stack(rows), tap_rows], axis=0)
            small[layer] = jnp.pad(rows, ((0, SMALL_ROWS - rows.shape[0]), (0, 0), (0, 0)))

        def reduce_small(p_pre1, p_post1):
            pack_small(p_pre1, p_post1)
            flow["small"] = _allreduce_small(jnp.concatenate(small, axis=0))
            return flow["small"]

        dxs, p_pre1, p_post1, head = _ffn_backward(
            f"{t}_ffn1", dxs, s1, gain(ffn1_norm_pre, layer), gain(ffn1_norm_post, layer),
            wts["gu1"][layer], wts["dn1"][layer], red, ("gu1", "dn1"), layer, flow["deps"], head, after_ffn1,
            last=reduce_small if layer == 0 else None)
        if layer > 0:
            pack_small(p_pre1, p_post1)
        between(dxs, (("dn1",) if layer == 0 else ("gu1", "dn1"), layer))
    grad_x = dxs[None]

    weights = dict(ffn1_norm_pre=ffn1_norm_pre, ffn1_w_gate_up=ffn1_w_gate_up, ffn1_w_down=ffn1_w_down, ffn1_norm_post=ffn1_norm_post, mix_norm_pre=mix_norm_pre, w_in=w_in, conv_w=conv_w, attn_out_norm=attn_out_norm, conv_out_norm=conv_out_norm, w_out=w_out, mix_norm_post=mix_norm_post, ffn2_norm_pre=ffn2_norm_pre, ffn2_w_gate_up=ffn2_w_gate_up, ffn2_w_down=ffn2_w_down, ffn2_norm_post=ffn2_norm_post)
    m_in = dict(ffn1_norm_pre=m_ffn1_norm_pre, ffn1_w_gate_up=m_ffn1_w_gate_up, ffn1_w_down=m_ffn1_w_down, ffn1_norm_post=m_ffn1_norm_post, mix_norm_pre=m_mix_norm_pre, w_in=m_w_in, conv_w=m_conv_w, attn_out_norm=m_attn_out_norm, conv_out_norm=m_conv_out_norm, w_out=m_w_out, mix_norm_post=m_mix_norm_post, ffn2_norm_pre=m_ffn2_norm_pre, ffn2_w_gate_up=m_ffn2_w_gate_up, ffn2_w_down=m_ffn2_w_down, ffn2_norm_post=m_ffn2_norm_post)
    v_in = dict(ffn1_norm_pre=v_ffn1_norm_pre, ffn1_w_gate_up=v_ffn1_w_gate_up, ffn1_w_down=v_ffn1_w_down, ffn1_norm_post=v_ffn1_norm_post, mix_norm_pre=v_mix_norm_pre, w_in=v_w_in, conv_w=v_conv_w, attn_out_norm=v_attn_out_norm, conv_out_norm=v_conv_out_norm, w_out=v_w_out, mix_norm_post=v_mix_norm_post, ffn2_norm_pre=v_ffn2_norm_pre, ffn2_w_gate_up=v_ffn2_w_gate_up, ffn2_w_down=v_ffn2_w_down, ffn2_norm_post=v_ffn2_norm_post)
    kind_name = dict(gu1="ffn1_w_gate_up", dn1="ffn1_w_down", win="w_in", wout="w_out", gu2="ffn2_w_gate_up", dn2="ffn2_w_down")
    delta, new_m, new_v, grad = {}, {}, {}, {}

    def join_and_update(name, items, deps, after):
        ts, send_sems, recv_sems = _join_start(name, [red.bufs[it] for it in items], deps)
        for a, it in enumerate(items):
            k, layer = it if isinstance(it, tuple) else (it, None)
            n = kind_name[k]
            tag = n if layer is None else f"{n}_{layer}"
            g = _join_wait(f"join_wait_{tag}", ts[a], a, send_sems, recv_sems, after)
            prev = (delta[n], new_m[n], new_v[n], grad[n]) if n in delta else None
            delta[n], new_m[n], new_v[n], grad[n] = _adamw(f"adamw_{tag}", weights[n], g, m_in[n], v_in[n], True,
                                                           layer, prev)
            after = delta[n]
        return after

    early = ("wout", "win", "dn2", "gu2") + tuple((k, layer) for layer in range(1, n_layers) for k in ("dn1", "gu1"))
    last_groups = ((("gu1",), 0), flow["in_flight"])
    done_early = join_and_update("join_early", early, tuple(red.scatter_tokens[g] for g in last_groups), dxs)
    for group in last_groups:
        red.finish(*group, done_early)
    join_and_update("join_late", (("dn1", 0), ("gu1", 0)), (), done_early)

    small_sum = flow["small"].reshape(n_layers, SMALL_ROWS, d)
    g_ffn1_pre, g_ffn1_post, g_mix_pre = small_sum[:, 0], small_sum[:, 1], small_sum[:, 2]
    g_attn_out, g_conv_out = small_sum[:, 3, :qd], small_sum[:, 3, qd:qd + cd]
    g_mix_post, g_ffn2_pre, g_ffn2_post = small_sum[:, 4], small_sum[:, 5], small_sum[:, 6]
    cc = conv_w.shape[2]
    g_conv = lax.dynamic_slice_in_dim(small_sum[:, 7:7 + CONV_WIDTH, :cd], chip * cc, cc, axis=2)

    grad.update(ffn1_norm_pre=g_ffn1_pre, ffn1_norm_post=g_ffn1_post, mix_norm_pre=g_mix_pre, conv_w=g_conv, attn_out_norm=g_attn_out, conv_out_norm=g_conv_out, mix_norm_post=g_mix_post, ffn2_norm_pre=g_ffn2_pre, ffn2_norm_post=g_ffn2_post)
    names = list(weights)

    vectors = [n for n in names if n not in kind_name.values()]

    def pack(tree):
        flat = jnp.concatenate([tree[n].reshape(-1) for n in vectors])
        return jnp.pad(flat, (0, -flat.size % (SUBLANES * LANES))).reshape(-1, LANES)

    packed = _adamw("adamw_small", pack(weights), pack(grad), pack(m_in), pack(v_in))
    offset = 0
    for n in vectors:
        size = weights[n].size
        for tree, flat in zip((delta, new_m, new_v), packed):
            tree[n] = flat.reshape(-1)[offset:offset + size].reshape(weights[n].shape)
        offset += size

    return (loss, grad_x, *[grad[n] for n in names], *[delta[n] for n in names],
            *[new_m[n] for n in names], *[new_v[n] for n in names])
```

```python
import functools

import jax
import jax.numpy as jnp
from jax import lax
from jax.experimental import pallas as pl
from jax.experimental.pallas import tpu as pltpu

F32 = jnp.float32
BF16 = jnp.bfloat16
MESH = pl.DeviceIdType.MESH

NORM_EPS = 1e-6
HEAD_DIM = 128
Q_PER_KV = 4
CONV_WIDTH = 3
FFN_RESIDUAL_WEIGHT = 0.5
DILATED_BRANCHES = ((128, 1), (512, 4), (2048, 16))
ADAM_LR = 0.001
ADAM_B1 = 0.9
ADAM_B2 = 0.999
ADAM_EPS = 1e-08
ADAM_WD = 0.01
ADAM_STEP = 10

N_CHIPS = 4
N_DEV = 8
V7X_VMEM_BYTES = 64 << 20
VMEM_LIMIT = V7X_VMEM_BYTES - (12 << 20)
SUBLANES = 8
LANES = 128
SMALL_ROWS = 16
BIG_BLOCK = 4 << 20


def _params(*sem):
    return pltpu.CompilerParams(dimension_semantics=sem, vmem_limit_bytes=VMEM_LIMIT)


def _row_tile(rows, cols, itemsize=4, budget=2 << 20):
    t = rows
    while t * cols * itemsize > budget and t % 32 == 0:
        t //= 2
    return t


def _sum_to_sublanes(v):
    r, n = v.shape
    return v.reshape(r // SUBLANES, SUBLANES, n).sum(axis=0)


_DIMS = {
    "nn": (((1,), (0,)), ((), ())),
    "nt": (((1,), (1,)), ((), ())),
    "tn": (((0,), (0,)), ((), ())),
}


ANY_SPEC = pl.BlockSpec(memory_space=pl.ANY)


def _dot(a, b, mode):
    return lax.dot_general(a, b, _DIMS[mode], preferred_element_type=F32)


def _mm(name, a, b, *, mode, grid, a_spec, b_spec, o_spec, out_shape, nk=1, acc_shape=None, deps=()):
    nd = len(deps)

    def body(a_ref, b_ref, *rest):
        o_ref, scratch = rest[nd], rest[nd + 1:]
        r = _dot(a_ref[...], b_ref[...], mode)
        if nk == 1:
            o_ref[...] = r.astype(o_ref.dtype)
        else:
            acc = scratch[0]
            k = pl.program_id(len(grid) - 1)

            @pl.when(k == 0)
            def _():
                acc[...] = r

            @pl.when(k > 0)
            def _():
                acc[...] += r

            @pl.when(k == nk - 1)
            def _():
                o_ref[...] = acc[...].astype(o_ref.dtype)

    sem = ("parallel",) * (len(grid) - (1 if nk > 1 else 0)) + (("arbitrary",) if nk > 1 else ())
    return pl.pallas_call(
        body, name=name, grid=grid, in_specs=[a_spec, b_spec] + [ANY_SPEC] * nd, out_specs=o_spec,
        out_shape=out_shape, scratch_shapes=[pltpu.VMEM(acc_shape, F32)] if nk > 1 else [],
        compiler_params=_params(*sem),
    )(a, b, *deps)


def _tile(n, want):
    if n <= want:
        return n
    best = None
    for t in range(LANES, want + 1, LANES):
        if n % t == 0:
            best = t
    assert best is not None, (n, want)
    return best


def _norm_fwd(name, x, gain):
    s, d = x.shape
    tr = _row_tile(s, d, budget=BIG_BLOCK)

    def body(x_ref, g_ref, o_ref):
        xv = x_ref[...]
        r = lax.rsqrt(jnp.mean(xv * xv, axis=-1, keepdims=True) + NORM_EPS)
        o_ref[...] = (xv * r * g_ref[...]).astype(o_ref.dtype)

    return pl.pallas_call(
        body, name=name, grid=(s // tr,),
        in_specs=[pl.BlockSpec((tr, d), lambda i: (i, 0)), pl.BlockSpec((1, d), lambda i: (0, 0))],
        out_specs=pl.BlockSpec((tr, d), lambda i: (i, 0)),
        out_shape=jax.ShapeDtypeStruct((s, d), BF16), compiler_params=_params("parallel"),
    )(x, gain)


def _res_norm(name, x, y, gain, scale, next_gain=None):
    s, d = x.shape
    tr = _row_tile(s, d, budget=BIG_BLOCK)
    with_next = next_gain is not None

    def body(x_ref, y_ref, g_ref, *rest):
        yv = y_ref[...]
        r = lax.rsqrt(jnp.mean(yv * yv, axis=-1, keepdims=True) + NORM_EPS)
        xn = x_ref[...] + scale * (yv * r * g_ref[...])
        if with_next:
            ng_ref, o_ref, h_ref = rest
            rn = lax.rsqrt(jnp.mean(xn * xn, axis=-1, keepdims=True) + NORM_EPS)
            h_ref[...] = (xn * rn * ng_ref[...]).astype(h_ref.dtype)
        else:
            o_ref, = rest
        o_ref[...] = xn

    row = pl.BlockSpec((tr, d), lambda i: (i, 0))
    vec = pl.BlockSpec((1, d), lambda i: (0, 0))
    outs = pl.pallas_call(
        body, name=name, grid=(s // tr,),
        in_specs=[row, row, vec] + ([vec] if with_next else []), out_specs=[row] * (2 if with_next else 1),
        out_shape=[jax.ShapeDtypeStruct((s, d), F32)] + ([jax.ShapeDtypeStruct((s, d), BF16)] if with_next else []),
        compiler_params=_params("parallel"),
    )(x, y, gain, *((next_gain,) if with_next else ()))
    return (outs[0], outs[1]) if with_next else (outs[0], None)


def _rms_bwd(dn, yv, gv):
    r = lax.rsqrt(jnp.mean(yv * yv, axis=-1, keepdims=True) + NORM_EPS)
    xhat = yv * r
    dxn = dn * gv
    return r * (dxn - xhat * jnp.mean(dxn * xhat, axis=-1, keepdims=True)), _sum_to_sublanes(dn * xhat)


def _accumulate(ref, part):
    @pl.when(pl.program_id(0) == 0)
    def _():
        ref[...] = part

    @pl.when(pl.program_id(0) > 0)
    def _():
        ref[...] += part


def _norm_bwd(name, dout, yin, gain, scale, resid, out_dtype, following=None):
    s, d = yin.shape
    tr = _row_tile(s, d)
    has_resid = resid is not None
    chained = following is not None

    def body(*refs):
        refs = list(refs)
        do_ref, y_ref, g_ref = refs[:3]
        del refs[:3]
        r_ref = refs.pop(0) if has_resid else None
        if chained:
            y2_ref, g2_ref = refs[:2]
            del refs[:2]
        di_ref, dg_ref = refs[:2]
        din, part = _rms_bwd(scale * do_ref[...], y_ref[...], g_ref[...])
        _accumulate(dg_ref, part)
        if has_resid:
            din = din + r_ref[...]
        di_ref[...] = din.astype(di_ref.dtype)
        if chained:
            d2_ref, dg2_ref = refs[2:]
            d2, part2 = _rms_bwd(following[2] * din, y2_ref[...], g2_ref[...])
            _accumulate(dg2_ref, part2)
            d2_ref[...] = d2.astype(d2_ref.dtype)

    row = pl.BlockSpec((tr, d), lambda i: (i, 0))
    vec = pl.BlockSpec((1, d), lambda i: (0, 0))
    acc = pl.BlockSpec((SUBLANES, d), lambda i: (0, 0))
    ins = [row, row, vec] + ([row] if has_resid else []) + ([row, vec] if chained else [])
    args = (dout, yin, gain) + ((resid,) if has_resid else ()) + (tuple(following[:2]) if chained else ())
    outs = [row, acc] + ([row, acc] if chained else [])
    shapes = [jax.ShapeDtypeStruct((s, d), out_dtype), jax.ShapeDtypeStruct((SUBLANES, d), F32)]
    if chained:
        shapes += [jax.ShapeDtypeStruct((s, d), BF16), jax.ShapeDtypeStruct((SUBLANES, d), F32)]
    return pl.pallas_call(
        body, name=name, grid=(s // tr,), in_specs=ins, out_specs=outs, out_shape=shapes,
        compiler_params=_params("arbitrary"),
    )(*args)


def _loss_head(name, y, target, following):
    s, d = y.shape
    tr = _row_tile(s, d)
    y2, gain2, scale2 = following

    def body(y_ref, t_ref, y2_ref, g2_ref, dy_ref, l_ref, d2_ref, dg2_ref):
        e = y_ref[...] - t_ref[...]
        dy = e * (1.0 / d)
        dy_ref[...] = dy
        _accumulate(l_ref, _sum_to_sublanes(e * e) * (0.5 / d))
        d2, part2 = _rms_bwd(scale2 * dy, y2_ref[...], g2_ref[...])
        _accumulate(dg2_ref, part2)
        d2_ref[...] = d2.astype(d2_ref.dtype)

    row = pl.BlockSpec((tr, d), lambda i: (i, 0))
    acc = pl.BlockSpec((SUBLANES, d), lambda i: (0, 0))
    return pl.pallas_call(
        body, name=name, grid=(s // tr,), in_specs=[row, row, row, pl.BlockSpec((1, d), lambda i: (0, 0))],
        out_specs=[row, acc, row, acc],
        out_shape=[jax.ShapeDtypeStruct((s, d), F32), jax.ShapeDtypeStruct((SUBLANES, d), F32),
                   jax.ShapeDtypeStruct((s, d), BF16), jax.ShapeDtypeStruct((SUBLANES, d), F32)],
        compiler_params=_params("arbitrary"),
    )(y, target, y2, gain2)


def _ffn_up(name, h, gu_w):
    s, d = h.shape
    nb, _, fs = gu_w.shape
    hb = nb // 2
    w = gu_w.reshape(2, hb, d, fs)
    tm = _tile(s, 512)
    tn = _tile(fs, 1408)
    nj = fs // tn

    def body(h_ref, w_ref, gu_ref, a_ref):
        hv = h_ref[...]
        g = _dot(hv, w_ref[0], "nn")
        u = _dot(hv, w_ref[1], "nn")
        sg = jax.nn.sigmoid(g)
        silu = g * sg
        gu_ref[0] = (u * (sg * (1.0 + g * (1.0 - sg)))).astype(gu_ref.dtype)
        gu_ref[1] = silu.astype(gu_ref.dtype)
        a_ref[...] = (silu * u).astype(a_ref.dtype)

    return pl.pallas_call(
        body, name=name, grid=(hb, nj, s // tm),
        in_specs=[pl.BlockSpec((tm, d), lambda jb, jo, i: (i, 0)),
                  pl.BlockSpec((2, None, d, tn), lambda jb, jo, i: (0, jb, 0, jo))],
        out_specs=[pl.BlockSpec((2, None, tm, tn), lambda jb, jo, i: (0, jb, i, jo)),
                   pl.BlockSpec((tm, tn), lambda jb, jo, i: (i, jb * nj + jo))],
        out_shape=[jax.ShapeDtypeStruct((2, hb, s, fs), BF16), jax.ShapeDtypeStruct((s, hb * fs), BF16)],
        compiler_params=_params("parallel", "parallel", "parallel"),
    )(h, w)


def _ffn_dact(name, dy, dn_w, gu, deps=()):
    s, d = dy.shape
    _, hb, _, fs = gu.shape
    tm = _tile(s, 1024)
    tn = _tile(fs, 1408)
    nj = fs // tn

    def body(dy_ref, w_ref, gu_ref, *rest):
        o_ref = rest[-1]
        wv = w_ref[...]
        parts = max(1, tm // 256)
        for r in range(parts):
            rows = slice(r * (tm // parts), (r + 1) * (tm // parts))
            da = _dot(dy_ref[rows, :], wv, "nt")
            o_ref[0, rows, :] = (da * gu_ref[0, rows, :].astype(F32)).astype(o_ref.dtype)
            o_ref[1, rows, :] = (da * gu_ref[1, rows, :].astype(F32)).astype(o_ref.dtype)

    blk = pl.BlockSpec((2, None, tm, tn), lambda jb, jo, i: (0, jb, i, jo))
    return pl.pallas_call(
        body, name=name, grid=(hb, nj, s // tm),
        in_specs=[pl.BlockSpec((tm, d), lambda jb, jo, i: (i, 0)),
                  pl.BlockSpec((tn, d), lambda jb, jo, i: (jb * nj + jo, 0)),
                  blk] + [ANY_SPEC] * len(deps),
        out_specs=blk, out_shape=jax.ShapeDtypeStruct(gu.shape, BF16),
        compiler_params=_params("parallel", "parallel", "parallel"),
    )(dy, dn_w, gu, *deps)


_MASKED = -1e30


def _attn_bias(s, tq):
    nd = s // tq
    dist = (jnp.arange(nd)[:, None, None] * tq + jnp.arange(tq)[None, :, None]) - jnp.arange(tq)[None, None, :]
    mult = jnp.zeros(dist.shape, F32)
    for window, dilation in DILATED_BRANCHES:
        mult = mult + ((dist >= 0) & (dist <= window) & (dist % dilation == 0)).astype(F32)
    return jnp.where(mult > 0.0, jnp.log(jnp.maximum(mult, 1.0)), _MASKED)


def _biased(sc, bias, scale):
    tq, tk = bias.shape
    return (sc.reshape(-1, tq, tk) * scale + bias[None]).reshape(sc.shape)


def _attn_specs(s, qd, kvd, tq):
    rw = Q_PER_KV * HEAD_DIM
    qspec = pl.BlockSpec((tq, rw), lambda g, i: (i, g))
    kspec = pl.BlockSpec((s, HEAD_DIM), lambda g, i: (0, qd // HEAD_DIM + g))
    vspec = pl.BlockSpec((s, HEAD_DIM), lambda g, i: (0, (qd + kvd) // HEAD_DIM + g))
    return rw, qspec, kspec, vspec


def _attn_fwd(name, z, qd, kvd):
    s = z.shape[0]
    tq = _tile(s, 256)
    nkv = kvd // HEAD_DIM
    rw, qspec, kspec, vspec = _attn_specs(s, qd, kvd, tq)
    scale = HEAD_DIM ** -0.5

    def body(q_ref, k_ref, v_ref, b_ref, o_ref, l_ref):
        i = pl.program_id(1)
        heads = [slice(h * HEAD_DIM, (h + 1) * HEAD_DIM) for h in range(Q_PER_KV)]
        q_all = jnp.concatenate([q_ref[:, cols] for cols in heads], axis=0)

        def chunk(j, carry):
            mx, den, acc = carry
            k0 = pl.multiple_of(j * tq, tq)
            kc, vc = k_ref[pl.ds(k0, tq), :], v_ref[pl.ds(k0, tq), :]
            sc = _biased(_dot(q_all, kc, "nt"), b_ref[i - j], scale)
            mx_new = jnp.maximum(mx, jnp.max(sc, axis=-1, keepdims=True))
            alpha = jnp.exp(mx - mx_new)
            p = jnp.exp(sc - mx_new)
            return (mx_new, alpha * den + jnp.sum(p, axis=-1, keepdims=True),
                    alpha * acc + _dot(p.astype(BF16), vc, "nn"))

        rows = Q_PER_KV * tq
        init = (jnp.full((rows, 1), _MASKED, F32), jnp.zeros((rows, 1), F32), jnp.zeros((rows, HEAD_DIM), F32))
        mx, den, acc = lax.fori_loop(0, i + 1, chunk, init)
        out = acc / den
        lse = mx + jnp.log(den)
        for h, cols in enumerate(heads):
            o_ref[:, cols] = out[h * tq:(h + 1) * tq]
            l_ref[:, cols] = jnp.broadcast_to(lse[h * tq:(h + 1) * tq], (tq, HEAD_DIM))

    bias = _attn_bias(s, tq)
    return pl.pallas_call(
        body, name=name, grid=(nkv, s // tq),
        in_specs=[qspec, kspec, vspec, pl.BlockSpec(bias.shape, lambda g, i: (0, 0, 0))], out_specs=[qspec, qspec],
        out_shape=[jax.ShapeDtypeStruct((s, qd), F32), jax.ShapeDtypeStruct((s, qd), F32)],
        compiler_params=_params("parallel", "parallel"),
    )(z, z, z, bias)


def _attn_bwd(name, z, o, lse, do, qd, kvd):
    s = z.shape[0]
    tq = _tile(s, 512)
    nkv = kvd // HEAD_DIM
    nq = s // tq
    rw, qspec, kspec, vspec = _attn_specs(s, qd, kvd, tq)
    scale = HEAD_DIM ** -0.5

    def body(q_ref, k_ref, v_ref, o_ref, l_ref, do_ref, b_ref, dq_ref, dk_ref, dv_ref, dk_acc, dv_acc):
        i = pl.program_id(1)
        heads = [slice(h * HEAD_DIM, (h + 1) * HEAD_DIM) for h in range(Q_PER_KV)]

        @pl.when(i == 0)
        def _():
            dk_acc[...] = jnp.zeros_like(dk_acc)
            dv_acc[...] = jnp.zeros_like(dv_acc)

        q_all = jnp.concatenate([q_ref[:, cols] for cols in heads], axis=0)
        do_all = jnp.concatenate([do_ref[:, cols].astype(BF16) for cols in heads], axis=0)
        lse_all = jnp.concatenate([l_ref[:, cols][:, :1] for cols in heads], axis=0)
        delta_all = jnp.concatenate(
            [jnp.sum(do_ref[:, cols] * o_ref[:, cols], axis=-1, keepdims=True) for cols in heads], axis=0)

        def chunk(j, dq):
            k0 = pl.multiple_of(j * tq, tq)
            kc, vc = k_ref[pl.ds(k0, tq), :], v_ref[pl.ds(k0, tq), :]
            p = jnp.exp(_biased(_dot(q_all, kc, "nt"), b_ref[i - j], scale) - lse_all)
            ds = (p * (_dot(do_all, vc, "nt") - delta_all) * scale).astype(BF16)
            dk_acc[pl.ds(k0, tq), :] += _dot(ds, q_all, "tn")
            dv_acc[pl.ds(k0, tq), :] += _dot(p.astype(BF16), do_all, "tn")
            return dq + _dot(ds, kc, "nn")

        dq = lax.fori_loop(0, i + 1, chunk, jnp.zeros((Q_PER_KV * tq, HEAD_DIM), F32))
        for h, cols in enumerate(heads):
            dq_ref[:, cols] = dq[h * tq:(h + 1) * tq].astype(dq_ref.dtype)

        @pl.when(i == nq - 1)
        def _():
            dk_ref[...] = dk_acc[...].astype(dk_ref.dtype)
            dv_ref[...] = dv_acc[...].astype(dv_ref.dtype)

    kvout = pl.BlockSpec((s, HEAD_DIM), lambda g, i: (0, g))
    bias = _attn_bias(s, tq)
    return pl.pallas_call(
        body, name=name, grid=(nkv, nq),
        in_specs=[qspec, kspec, vspec, qspec, qspec, qspec, pl.BlockSpec(bias.shape, lambda g, i: (0, 0, 0))],
        out_specs=[qspec, kvout, kvout],
        out_shape=[jax.ShapeDtypeStruct((s, qd), BF16), jax.ShapeDtypeStruct((s, kvd), BF16),
                   jax.ShapeDtypeStruct((s, kvd), BF16)],
        scratch_shapes=[pltpu.VMEM((s, HEAD_DIM), F32), pltpu.VMEM((s, HEAD_DIM), F32)],
        compiler_params=_params("parallel", "arbitrary"),
    )(z, z, z, o, lse, do, bias)


def _shift_down(v, n):
    rolled = pltpu.roll(v, n, 0)
    t = lax.broadcasted_iota(jnp.int32, v.shape, 0)
    return jnp.where(t >= n, rolled, 0.0)


def _shift_up(v, n):
    rows = v.shape[0]
    rolled = pltpu.roll(v, rows - n, 0)
    t = lax.broadcasted_iota(jnp.int32, v.shape, 0)
    return jnp.where(t < rows - n, rolled, 0.0)


def _conv_specs(s, base, cd, tc):
    zs = [pl.BlockSpec((s, tc), functools.partial(lambda j, off: (0, off + j), off=(base + n * cd) // tc))
          for n in range(3)]
    wspec = pl.BlockSpec((SUBLANES, tc), lambda j: (0, j))
    cspec = pl.BlockSpec((s, tc), lambda j: (0, j))
    return zs, wspec, cspec


def _conv_fwd(name, z, conv_w, base, cd):
    s = z.shape[0]
    tc = _tile(cd, 256)
    zs, wspec, cspec = _conv_specs(s, base, cd, tc)

    def body(h_ref, b_ref, c_ref, w_ref, o_ref):
        u = c_ref[...].astype(F32) * h_ref[...].astype(F32)
        y = w_ref[0:1, :] * _shift_down(u, 2) + w_ref[1:2, :] * _shift_down(u, 1) + w_ref[2:3, :] * u
        o_ref[...] = b_ref[...].astype(F32) * y

    return pl.pallas_call(
        body, name=name, grid=(cd // tc,), in_specs=zs + [wspec], out_specs=cspec,
        out_shape=jax.ShapeDtypeStruct((s, cd), F32), compiler_params=_params("parallel"),
    )(z, z, z, conv_w)


def _conv_bwd(name, z, conv_w, dc, base, cd):
    s = z.shape[0]
    tc = _tile(cd, 256)
    zs, wspec, cspec = _conv_specs(s, base, cd, tc)

    def body(h_ref, b_ref, c_ref, w_ref, dc_ref, dh_ref, db_ref, dcg_ref, dw_ref):
        hv, bv, cv = h_ref[...].astype(F32), b_ref[...].astype(F32), c_ref[...].astype(F32)
        u = cv * hv
        u1, u2 = _shift_down(u, 1), _shift_down(u, 2)
        w0, w1, w2 = w_ref[0:1, :], w_ref[1:2, :], w_ref[2:3, :]
        y = w0 * u2 + w1 * u1 + w2 * u
        dcv = dc_ref[...]
        db_ref[...] = (dcv * y).astype(db_ref.dtype)
        dy = dcv * bv
        du = w2 * dy + w1 * _shift_up(dy, 1) + w0 * _shift_up(dy, 2)
        dh_ref[...] = (du * cv).astype(dh_ref.dtype)
        dcg_ref[...] = (du * hv).astype(dcg_ref.dtype)
        g0 = jnp.sum(dy * u2, axis=0, keepdims=True)
        g1 = jnp.sum(dy * u1, axis=0, keepdims=True)
        g2 = jnp.sum(dy * u, axis=0, keepdims=True)
        r = lax.broadcasted_iota(jnp.int32, (SUBLANES, tc), 0)
        dw_ref[...] = jnp.where(r == 0, g0, jnp.where(r == 1, g1, jnp.where(r == 2, g2, 0.0)))

    return pl.pallas_call(
        body, name=name, grid=(cd // tc,), in_specs=zs + [wspec, cspec],
        out_specs=[cspec, cspec, cspec, wspec],
        out_shape=[jax.ShapeDtypeStruct((s, cd), BF16)] * 3 + [jax.ShapeDtypeStruct((SUBLANES, cd), F32)],
        compiler_params=_params("parallel"),
    )(z, z, z, conv_w, dc)


def _cat_norm_fwd(name, a, c, ga, gc):
    s, qd = a.shape
    cd = c.shape[1]
    tr = _row_tile(s, qd + cd)

    def body(a_ref, c_ref, ga_ref, gc_ref, o_ref):
        av, cv = a_ref[...], c_ref[...]
        ra = lax.rsqrt(jnp.mean(av * av, axis=-1, keepdims=True) + NORM_EPS)
        rc = lax.rsqrt(jnp.mean(cv * cv, axis=-1, keepdims=True) + NORM_EPS)
        o_ref[:, :qd] = (av * ra * ga_ref[...]).astype(o_ref.dtype)
        o_ref[:, qd:] = (cv * rc * gc_ref[...]).astype(o_ref.dtype)

    return pl.pallas_call(
        body, name=name, grid=(s // tr,),
        in_specs=[pl.BlockSpec((tr, qd), lambda i: (i, 0)), pl.BlockSpec((tr, cd), lambda i: (i, 0)),
                  pl.BlockSpec((1, qd), lambda i: (0, 0)), pl.BlockSpec((1, cd), lambda i: (0, 0))],
        out_specs=pl.BlockSpec((tr, qd + cd), lambda i: (i, 0)),
        out_shape=jax.ShapeDtypeStruct((s, qd + cd), BF16), compiler_params=_params("parallel"),
    )(a, c, ga, gc)


def _cat_norm_bwd(name, dcat, a, c, ga, gc):
    s, qd = a.shape
    cd = c.shape[1]
    tr = _row_tile(s, qd + cd)

    def one(dn, yv, gv):
        r = lax.rsqrt(jnp.mean(yv * yv, axis=-1, keepdims=True) + NORM_EPS)
        xhat = yv * r
        dxn = dn * gv
        return r * (dxn - xhat * jnp.mean(dxn * xhat, axis=-1, keepdims=True)), _sum_to_sublanes(dn * xhat)

    def body(d_ref, a_ref, c_ref, ga_ref, gc_ref, da_ref, dc_ref, dga_ref, dgc_ref):
        da, pa = one(d_ref[:, :qd], a_ref[...], ga_ref[...])
        dc, pc = one(d_ref[:, qd:], c_ref[...], gc_ref[...])
        da_ref[...] = da
        dc_ref[...] = dc

        @pl.when(pl.program_id(0) == 0)
        def _():
            dga_ref[...] = pa
            dgc_ref[...] = pc

        @pl.when(pl.program_id(0) > 0)
        def _():
            dga_ref[...] += pa
            dgc_ref[...] += pc

    ra = pl.BlockSpec((tr, qd), lambda i: (i, 0))
    rc = pl.BlockSpec((tr, cd), lambda i: (i, 0))
    return pl.pallas_call(
        body, name=name, grid=(s // tr,),
        in_specs=[pl.BlockSpec((tr, qd + cd), lambda i: (i, 0)), ra, rc,
                  pl.BlockSpec((1, qd), lambda i: (0, 0)), pl.BlockSpec((1, cd), lambda i: (0, 0))],
        out_specs=[ra, rc, pl.BlockSpec((SUBLANES, qd), lambda i: (0, 0)),
                   pl.BlockSpec((SUBLANES, cd), lambda i: (0, 0))],
        out_shape=[jax.ShapeDtypeStruct((s, qd), F32), jax.ShapeDtypeStruct((s, cd), F32),
                   jax.ShapeDtypeStruct((SUBLANES, qd), F32), jax.ShapeDtypeStruct((SUBLANES, cd), F32)],
        compiler_params=_params("arbitrary"),
    )(dcat, a, c, ga, gc)


def _adamw(name, w, g, m, v, emit_grad=False, layer=None, prev=None):
    shape = w.shape
    cols = shape[-1]
    rows = g.size // cols
    tr = _row_tile(rows, cols, budget=3 << 19)
    first = 0 if layer is None else layer * (rows // tr)
    bc1 = 1.0 - ADAM_B1 ** ADAM_STEP
    bc2 = 1.0 - ADAM_B2 ** ADAM_STEP
    n_out = 4 if emit_grad else 3

    def body(w_ref, g_ref, m_ref, v_ref, *rest):
        d_ref, nm_ref, nv_ref = rest[-n_out:][:3]
        gv = g_ref[...]
        mv = ADAM_B1 * m_ref[...] + (1.0 - ADAM_B1) * gv
        vv = ADAM_B2 * v_ref[...] + (1.0 - ADAM_B2) * (gv * gv)
        nm_ref[...] = mv
        nv_ref[...] = vv
        d_ref[...] = -ADAM_LR * ((mv / bc1) / (jnp.sqrt(vv / bc2) + ADAM_EPS) + ADAM_WD * w_ref[...])
        if emit_grad:
            rest[-1][...] = gv

    row = pl.BlockSpec((tr, cols), lambda i: (first + i, 0))
    g_row = pl.BlockSpec((tr, cols), lambda i: (i, 0))
    prev = tuple(prev) if prev is not None else ()
    total = w.size // cols
    outs = pl.pallas_call(
        body, name=name, grid=(rows // tr,), in_specs=[row, g_row, row, row] + [ANY_SPEC] * len(prev),
        out_specs=[row] * n_out, out_shape=[jax.ShapeDtypeStruct((total, cols), F32)] * n_out,
        input_output_aliases={4 + i: i for i in range(len(prev))}, compiler_params=_params("parallel"),
    )(w.reshape(total, cols), g.reshape(rows, cols), m.reshape(total, cols), v.reshape(total, cols),
      *(t.reshape(total, cols) for t in prev))
    return tuple(t.reshape(shape) for t in outs)


HBM_SPEC = pl.BlockSpec(memory_space=pltpu.HBM)


def _mesh_place():
    x, y, c = lax.axis_index("x"), lax.axis_index("y"), lax.axis_index("c")
    other_chips = [(1 - x, y), (x, 1 - y), (1 - x, 1 - y)]
    return x, y, c, other_chips


def _cast_into_slot(name, w, layer, chip, deps=()):
    _, r, cols = w.shape
    tr = _row_tile(r, cols, budget=BIG_BLOCK)

    def body(chip_ref, w_ref, *rest):
        o_ref = rest[-1]
        o_ref[...] = w_ref[...].astype(o_ref.dtype)

    return pl.pallas_call(
        body, name=name,
        grid_spec=pltpu.PrefetchScalarGridSpec(
            num_scalar_prefetch=1, grid=(r // tr,),
            in_specs=[pl.BlockSpec((None, tr, cols), lambda i, chip_ref: (layer, i, 0))] + [ANY_SPEC] * len(deps),
            out_specs=pl.BlockSpec((None, tr, cols), lambda i, chip_ref: (chip_ref[0], i, 0))),
        out_shape=jax.ShapeDtypeStruct((N_CHIPS, r, cols), BF16), compiler_params=_params("parallel"),
    )(chip, w, *deps)


SEM_SPEC = pl.BlockSpec(memory_space=pltpu.SEMAPHORE)
SPLIT_COPY = pltpu.CompilerParams(has_side_effects=pltpu.SideEffectType.DATAFLOW_SIDE_EFFECTING)
N_OTHER = N_CHIPS - 1
TOKEN_SPEC = pl.BlockSpec(memory_space=pltpu.VMEM)
TOKEN_SHAPE = jax.ShapeDtypeStruct((SUBLANES, LANES), F32)


def _in_hbm(arr):
    return pltpu.with_memory_space_constraint(arr, pltpu.HBM)


def _half_rows(ref, chip_idx, core):
    r2 = ref.shape[1] // 2
    return ref.at[chip_idx, pl.ds(core * r2, r2), :]


def _gather_start(name, fulls, after):
    na = len(fulls)

    def body(*refs):
        f_refs = refs[na + 1:2 * na + 1]
        send_sems, recv_sems = refs[2 * na + 1:3 * na + 1], refs[3 * na + 1:4 * na + 1]
        token = refs[4 * na + 1]
        x, y, c, chips = _mesh_place()
        for a in range(na):
            mine = _half_rows(f_refs[a], 2 * x + y, c)
            for j, (cx, cy) in enumerate(chips):
                pltpu.make_async_remote_copy(
                    src_ref=mine, dst_ref=mine, send_sem=send_sems[a].at[j], recv_sem=recv_sems[a].at[j],
                    device_id=(cx, cy, c), device_id_type=MESH).start()
        token[...] = jnp.zeros_like(token)

    outs = pl.pallas_call(
        body, name=name, in_specs=[HBM_SPEC] * na + [ANY_SPEC],
        out_specs=[HBM_SPEC] * na + [SEM_SPEC] * (2 * na) + [TOKEN_SPEC],
        out_shape=[pltpu.HBM(f.shape, f.dtype) for f in fulls] + [pltpu.SemaphoreType.DMA((N_OTHER,))] * (2 * na)
        + [TOKEN_SHAPE],
        input_output_aliases={a: a for a in range(na)}, compiler_params=SPLIT_COPY,
    )(*[_in_hbm(f) for f in fulls], after)
    return list(outs[:na]), list(outs[na:2 * na]), list(outs[2 * na:3 * na]), outs[3 * na]


def _gather_pass_on(name, full, recv_sems, after):
    def body(f_in, recv_sems, after_ref, f_ref, d2d_send, d2d_recv):
        x, y, c, chips = _mesh_place()
        for j, (cx, cy) in enumerate(chips):
            blk = _half_rows(f_ref, 2 * cx + cy, c)
            pltpu.make_async_remote_copy(
                src_ref=blk, dst_ref=blk, send_sem=d2d_send.at[j], recv_sem=recv_sems.at[j],
                device_id=(cx, cy, c), device_id_type=MESH).wait_recv()
            pltpu.make_async_remote_copy(
                src_ref=blk, dst_ref=blk, send_sem=d2d_send.at[j], recv_sem=d2d_recv.at[j],
                device_id=(x, y, 1 - c), device_id_type=MESH).start()

    return pl.pallas_call(
        body, name=name, in_specs=[HBM_SPEC, SEM_SPEC, ANY_SPEC], out_specs=[HBM_SPEC, SEM_SPEC, SEM_SPEC],
        out_shape=[pltpu.HBM(full.shape, full.dtype)] + [pltpu.SemaphoreType.DMA((N_OTHER,))] * 2,
        input_output_aliases={0: 0}, compiler_params=SPLIT_COPY,
    )(full, recv_sems, after)


def _gather_arrive(name, full, ici_send, d2d_send, d2d_recv, after):
    def body(f_in, ici_send, d2d_send, d2d_recv, after_ref, f_ref):
        x, y, c, chips = _mesh_place()
        for j, (cx, cy) in enumerate(chips):
            mine = _half_rows(f_ref, 2 * x + y, c)
            passed = _half_rows(f_ref, 2 * cx + cy, c)
            theirs = _half_rows(f_ref, 2 * cx + cy, 1 - c)
            pltpu.make_async_remote_copy(
                src_ref=mine, dst_ref=mine, send_sem=ici_send.at[j], recv_sem=d2d_recv.at[j],
                device_id=(cx, cy, c), device_id_type=MESH).wait_send()
            pltpu.make_async_remote_copy(
                src_ref=passed, dst_ref=passed, send_sem=d2d_send.at[j], recv_sem=d2d_recv.at[j],
                device_id=(x, y, 1 - c), device_id_type=MESH).wait_send()
            pltpu.make_async_remote_copy(
                src_ref=theirs, dst_ref=theirs, send_sem=d2d_send.at[j], recv_sem=d2d_recv.at[j],
                device_id=(x, y, 1 - c), device_id_type=MESH).wait_recv()

    return pl.pallas_call(
        body, name=name, in_specs=[HBM_SPEC, SEM_SPEC, SEM_SPEC, SEM_SPEC, ANY_SPEC], out_specs=HBM_SPEC,
        out_shape=pltpu.HBM(full.shape, full.dtype), input_output_aliases={0: 0}, compiler_params=SPLIT_COPY,
    )(full, ici_send, d2d_send, d2d_recv, after)


def _gather_taps(conv_w):
    def body(cw_ref, cwf_ref, send_sems, recv_sems, local_sem):
        x, y, c, chips = _mesh_place()
        k_me = 2 * x + y
        local = pltpu.make_async_copy(cw_ref, cwf_ref.at[k_me], local_sem)
        local.start()
        copies = [pltpu.make_async_remote_copy(
            src_ref=cw_ref, dst_ref=cwf_ref.at[k_me], send_sem=send_sems.at[j], recv_sem=recv_sems.at[j],
            device_id=(cx, cy, c), device_id_type=MESH) for j, (cx, cy) in enumerate(chips)]
        for cp in copies:
            cp.start()
        for j, (cx, cy) in enumerate(chips):
            pltpu.make_async_remote_copy(
                src_ref=cw_ref, dst_ref=cwf_ref.at[2 * cx + cy], send_sem=send_sems.at[j], recv_sem=recv_sems.at[j],
                device_id=(cx, cy, c), device_id_type=MESH).wait_recv()
        for cp in copies:
            cp.wait_send()
        local.wait()

    return pl.pallas_call(
        body, name="gather_taps", in_specs=[HBM_SPEC], out_specs=HBM_SPEC,
        out_shape=jax.ShapeDtypeStruct((N_CHIPS,) + conv_w.shape, conv_w.dtype),
        scratch_shapes=[pltpu.SemaphoreType.DMA((N_OTHER,))] * 2 + [pltpu.SemaphoreType.DMA],
    )(conv_w)


def _sibling_half(g_ref, c):
    r2 = g_ref.shape[1] // 2
    return g_ref.at[:, pl.ds((1 - c) * r2, r2), :]


def _swap_copy(g_ref, land_ref, send_sems, recv_sems, a):
    x, y, c, _ = _mesh_place()
    return pltpu.make_async_remote_copy(
        src_ref=_sibling_half(g_ref, c), dst_ref=land_ref, send_sem=send_sems.at[a], recv_sem=recv_sems.at[a],
        device_id=(x, y, 1 - c), device_id_type=MESH)


def _swap_start(name, gs):
    n = len(gs)

    def body(*refs):
        g_refs, land_refs = refs[n:2 * n], refs[2 * n:3 * n]
        send_sems, recv_sems, token = refs[3 * n:]
        for a in range(n):
            _swap_copy(g_refs[a], land_refs[a], send_sems, recv_sems, a).start()
        token[...] = jnp.zeros_like(token)

    outs = pl.pallas_call(
        body, name=name, in_specs=[HBM_SPEC] * n,
        out_specs=[HBM_SPEC] * (2 * n) + [SEM_SPEC, SEM_SPEC, TOKEN_SPEC],
        out_shape=[pltpu.HBM(g.shape, g.dtype) for g in gs]
        + [pltpu.HBM((g.shape[0], g.shape[1] // 2, g.shape[2]), g.dtype) for g in gs]
        + [pltpu.SemaphoreType.DMA((n,)), pltpu.SemaphoreType.DMA((n,)), TOKEN_SHAPE],
        input_output_aliases={a: a for a in range(n)}, compiler_params=SPLIT_COPY,
    )(*[_in_hbm(g) for g in gs])
    return list(outs[:n]), list(outs[n:2 * n]), outs[2 * n], outs[2 * n + 1], outs[2 * n + 2]


def _swap_wait(name, gs, lands, send_sems, recv_sems, after):
    n = len(gs)

    def body(*refs):
        send_sems, recv_sems = refs[2 * n], refs[2 * n + 1]
        g_refs, land_refs = refs[2 * n + 3:3 * n + 3], refs[3 * n + 3:]
        for a in range(n):
            copy = _swap_copy(g_refs[a], land_refs[a], send_sems, recv_sems, a)
            copy.wait_send()
            copy.wait_recv()

    outs = pl.pallas_call(
        body, name=name, in_specs=[HBM_SPEC] * (2 * n) + [SEM_SPEC, SEM_SPEC, ANY_SPEC],
        out_specs=[HBM_SPEC] * (2 * n),
        out_shape=[pltpu.HBM(t.shape, t.dtype) for t in list(gs) + list(lands)],
        input_output_aliases={a: a for a in range(2 * n)}, compiler_params=SPLIT_COPY,
    )(*gs, *lands, send_sems, recv_sems, after)
    return list(outs[:n]), list(outs[n:])


def _add_core_halves(name, g, sib, core):
    nb, r, cols = g.shape
    r2 = r // 2
    tr = _row_tile(r2, cols, itemsize=2, budget=BIG_BLOCK)
    nrt = r2 // tr

    def body(core_ref, g_ref, s_ref, o_ref):
        o_ref[...] = (g_ref[...].astype(F32) + s_ref[...].astype(F32)).astype(o_ref.dtype)

    return pl.pallas_call(
        body, name=name,
        grid_spec=pltpu.PrefetchScalarGridSpec(
            num_scalar_prefetch=1, grid=(nb, nrt),
            in_specs=[pl.BlockSpec((None, tr, cols), lambda k, i, core_ref: (k, core_ref[0] * nrt + i, 0)),
                      pl.BlockSpec((None, tr, cols), lambda k, i, core_ref: (k, i, 0))],
            out_specs=pl.BlockSpec((None, tr, cols), lambda k, i, core_ref: (k, i, 0))),
        out_shape=jax.ShapeDtypeStruct((nb, r2, cols), BF16), compiler_params=_params("parallel", "parallel"),
    )(core, g, sib)


def _scatter_copies(h_refs, land_refs, send_sems, recv_sems):
    x, y, c, chips = _mesh_place()
    return [pltpu.make_async_remote_copy(
        src_ref=h_ref.at[2 * cx + cy], dst_ref=land_ref.at[j],
        send_sem=send_sems.at[a * N_OTHER + j], recv_sem=recv_sems.at[a * N_OTHER + j],
        device_id=(cx, cy, c), device_id_type=MESH)
        for a, (h_ref, land_ref) in enumerate(zip(h_refs, land_refs)) for j, (cx, cy) in enumerate(chips)]


def _scatter_start(name, hs):
    n = len(hs)

    def body(*refs):
        h_refs, land_refs = refs[n:2 * n], refs[2 * n:3 * n]
        send_sems, recv_sems, token = refs[3 * n:]
        for copy in _scatter_copies(h_refs, land_refs, send_sems, recv_sems):
            copy.start()
        token[...] = jnp.zeros_like(token)

    outs = pl.pallas_call(
        body, name=name, in_specs=[HBM_SPEC] * n,
        out_specs=[HBM_SPEC] * (2 * n) + [SEM_SPEC, SEM_SPEC, TOKEN_SPEC],
        out_shape=[pltpu.HBM(h.shape, h.dtype) for h in hs]
        + [pltpu.HBM((N_OTHER,) + h.shape[1:], h.dtype) for h in hs]
        + [pltpu.SemaphoreType.DMA((n * N_OTHER,)), pltpu.SemaphoreType.DMA((n * N_OTHER,)), TOKEN_SHAPE],
        input_output_aliases={a: a for a in range(n)}, compiler_params=SPLIT_COPY,
    )(*[_in_hbm(h) for h in hs])
    return list(outs[:n]), list(outs[n:2 * n]), outs[2 * n], outs[2 * n + 1], outs[2 * n + 2]


def _scatter_wait(name, hs, lands, send_sems, recv_sems, after):
    afters = tuple(after) if isinstance(after, (tuple, list)) else (after,)
    n = len(hs)

    def body(*refs):
        send_sems, recv_sems = refs[2 * n], refs[2 * n + 1]
        h_refs, land_refs = refs[-2 * n:-n], refs[-n:]
        for copy in _scatter_copies(h_refs, land_refs, send_sems, recv_sems):
            copy.wait_send()
            copy.wait_recv()

    outs = pl.pallas_call(
        body, name=name, in_specs=[HBM_SPEC] * (2 * n) + [SEM_SPEC, SEM_SPEC] + [ANY_SPEC] * len(afters),
        out_specs=[HBM_SPEC] * (2 * n),
        out_shape=[pltpu.HBM(t.shape, t.dtype) for t in list(hs) + list(lands)],
        input_output_aliases={a: a for a in range(2 * n)}, compiler_params=SPLIT_COPY,
    )(*hs, *lands, send_sems, recv_sems, *afters)
    return list(outs[:n]), list(outs[n:])


def _sum_chips(name, hs, rcv, core, chip, layer, n_layers, prev):
    _, r2, cols = hs.shape
    tr = _row_tile(r2, cols, budget=BIG_BLOCK)
    nrt = r2 // tr

    def body(core_ref, chip_ref, h_ref, r_ref, *rest):
        o_ref = rest[-1]
        acc = h_ref[...].astype(F32)
        for j in range(N_CHIPS - 1):
            acc = acc + r_ref[j].astype(F32)
        o_ref[...] = acc

    in_specs = [pl.BlockSpec((None, tr, cols), lambda i, core_ref, chip_ref: (chip_ref[0], i, 0)),
                pl.BlockSpec((N_CHIPS - 1, tr, cols), lambda i, core_ref, chip_ref: (0, i, 0))]
    args = [core, chip, hs, rcv]
    aliases = {}
    if prev is not None:
        in_specs.append(pl.BlockSpec(memory_space=pl.ANY))
        args.append(prev)
        aliases = {4: 0}
    return pl.pallas_call(
        body, name=name,
        grid_spec=pltpu.PrefetchScalarGridSpec(
            num_scalar_prefetch=2, grid=(nrt,), in_specs=in_specs,
            out_specs=pl.BlockSpec((None, tr, cols), lambda i, core_ref, chip_ref: (layer, core_ref[0] * nrt + i, 0))),
        out_shape=jax.ShapeDtypeStruct((n_layers, 2 * r2, cols), F32), input_output_aliases=aliases,
        compiler_params=_params("parallel"),
    )(*args)


def _join_copy(t_ref, send_sems, recv_sems, a):
    x, y, c, _ = _mesh_place()
    r2 = t_ref.shape[1] // 2
    mine = t_ref.at[:, pl.ds(c * r2, r2), :]
    return pltpu.make_async_remote_copy(
        src_ref=mine, dst_ref=mine, send_sem=send_sems.at[a], recv_sem=recv_sems.at[a],
        device_id=(x, y, 1 - c), device_id_type=MESH)


def _join_start(name, ts, deps=()):
    n, nd = len(ts), len(deps)

    def body(*refs):
        t_refs = refs[n + nd:2 * n + nd]
        send_sems, recv_sems = refs[2 * n + nd:]
        for a in range(n):
            _join_copy(t_refs[a], send_sems, recv_sems, a).start()

    outs = pl.pallas_call(
        body, name=name, in_specs=[HBM_SPEC] * n + [ANY_SPEC] * nd, out_specs=[HBM_SPEC] * n + [SEM_SPEC, SEM_SPEC],
        out_shape=[pltpu.HBM(t.shape, t.dtype) for t in ts] + [pltpu.SemaphoreType.DMA((n,))] * 2,
        input_output_aliases={a: a for a in range(n)}, compiler_params=SPLIT_COPY,
    )(*[_in_hbm(t) for t in ts], *deps)
    return list(outs[:n]), outs[n], outs[n + 1]


def _join_wait(name, t, a, send_sems, recv_sems, after):
    def body(t_in, send_sems, recv_sems, after_ref, t_ref):
        copy = _join_copy(t_ref, send_sems, recv_sems, a)
        copy.wait_send()
        copy.wait_recv()

    return pl.pallas_call(
        body, name=name, in_specs=[HBM_SPEC, SEM_SPEC, SEM_SPEC, ANY_SPEC], out_specs=HBM_SPEC,
        out_shape=pltpu.HBM(t.shape, t.dtype), input_output_aliases={0: 0}, compiler_params=SPLIT_COPY,
    )(t, send_sems, recv_sems, after)


def _allreduce_small(p):
    n, _, w = p.shape

    def body(p_ref, o_ref, buf, send_sems, recv_sems):
        x, y, c, _ = _mesh_place()
        me = 4 * x + 2 * y + c
        buf[me] = jnp.sum(p_ref[...], axis=1)
        copies = []
        for pat in range(1, N_DEV):
            fx, fy, fc = (pat >> 2) & 1, (pat >> 1) & 1, pat & 1
            copies.append(pltpu.make_async_remote_copy(
                src_ref=buf.at[me], dst_ref=buf.at[me], send_sem=send_sems.at[pat - 1], recv_sem=recv_sems.at[pat - 1],
                device_id=(x ^ fx, y ^ fy, c ^ fc), device_id_type=MESH))
        for cp in copies:
            cp.start()
        for cp in copies:
            cp.wait()
        acc = buf[0]
        for dev in range(1, N_DEV):
            acc = acc + buf[dev]
        o_ref[...] = acc

    return pl.pallas_call(
        body, name="allreduce_small", in_specs=[pl.BlockSpec(memory_space=pltpu.VMEM)],
        out_specs=pl.BlockSpec(memory_space=pltpu.VMEM), out_shape=jax.ShapeDtypeStruct((n, w), F32),
        scratch_shapes=[pltpu.VMEM((N_DEV, n, w), F32), pltpu.SemaphoreType.DMA((N_DEV - 1,)),
                        pltpu.SemaphoreType.DMA((N_DEV - 1,))],
    )(p)


class _WeightFeed:
    def __init__(self):
        self.fulls, self.ici_send, self.ici_recv, self.d2d = [], [], [], []

    def start(self, name, fulls, after):
        started, send, recv, token = _gather_start(name, fulls, after)
        self.fulls += started
        self.ici_send += send
        self.ici_recv += recv
        self.d2d += [None] * len(fulls)
        self.token = token
        return token

    def _pass_on(self, k, after):
        if k == 0:
            after = self.token
        if k < len(self.fulls) and self.d2d[k] is None:
            self.fulls[k], send, recv = _gather_pass_on(f"gather_pass_{k}", self.fulls[k], self.ici_recv[k], after)
            self.d2d[k] = (send, recv)

    def take(self, k, after):
        self._pass_on(k, after)
        self.fulls[k] = _gather_arrive(f"gather_arrive_{k}", self.fulls[k], self.ici_send[k], *self.d2d[k], after)
        return self.fulls[k]


def _ffn_forward(tag, x, h, g_post, next_gain, feed, k):
    s, d = x.shape
    gu_w = feed.take(k, h)
    gu, a = _ffn_up(f"{tag}_up", h, gu_w)
    dn_w = feed.take(k + 1, a).reshape(-1, d)
    f = dn_w.shape[0]
    tm, tn = _tile(s, 1024), _tile(d, 512)
    y = _mm(f"{tag}_down", a, dn_w, mode="nn", grid=(s // tm, d // tn),
            a_spec=pl.BlockSpec((tm, f), lambda i, j: (i, 0)),
            b_spec=pl.BlockSpec((f, tn), lambda i, j: (0, j)),
            o_spec=pl.BlockSpec((tm, tn), lambda i, j: (i, j)),
            out_shape=jax.ShapeDtypeStruct((s, d), F32))
    x_new, h_next = _res_norm(f"{tag}_post", x, y, g_post, FFN_RESIDUAL_WEIGHT, next_gain)
    return x_new, h_next, (x, h, gu, a, y)


class _GradReduce:
    def __init__(self, core, chip, n_layers, per_layer=()):
        self.core, self.chip, self.n_layers, self.per_layer = core, chip, n_layers, per_layer
        self.state = {}
        self.bufs = {}
        self.scatter_tokens = {}

    def start(self, kinds, layer, gs):
        gs, lands, send, recv, token = _swap_start(f"swap_start_{kinds[0]}_{layer}", gs)
        self.state[kinds, layer] = (gs, lands, send, recv)
        return token

    def exchange(self, kinds, layer, after):
        tag = f"{kinds[0]}_{layer}"
        gs, sibs = _swap_wait(f"swap_wait_{tag}", *self.state[kinds, layer], after)
        hs = [_add_core_halves(f"add_cores_{k}_{layer}", g, sib, self.core) for k, g, sib in zip(kinds, gs, sibs)]
        hs, lands, send, recv, token = _scatter_start(f"scatter_start_{tag}", hs)
        self.state[kinds, layer] = (hs, lands, send, recv)
        self.scatter_tokens[kinds, layer] = token
        return token

    def finish(self, kinds, layer, after):
        tag = f"{kinds[0]}_{layer}"
        hs, rcvs = _scatter_wait(f"scatter_wait_{tag}", *self.state.pop((kinds, layer)), after)
        for k, h, rcv in zip(kinds, hs, rcvs):
            if k in self.per_layer:
                last = self.bufs[k, layer] = _sum_chips(f"sum_chips_{k}_{layer}", h, rcv, self.core, self.chip,
                                                        0, 1, None)
            else:
                last = self.bufs[k] = _sum_chips(f"sum_chips_{k}_{layer}", h, rcv, self.core, self.chip, layer,
                                                 self.n_layers, self.bufs.get(k))
        return last


def _ffn_backward(tag, dx_new, saved, g_pre, g_post, gu_w, dn_w, red, kinds, layer, deps, head, following,
                  last=None):
    x, h, gu, a, y = saved
    s, d = x.shape
    nb, fs = gu_w.shape[0], gu_w.shape[2]
    f = dn_w.shape[0]
    fr = f // nb
    dy, dg_post = head or _norm_bwd(f"{tag}_post_bwd", dx_new, y, g_post, FFN_RESIDUAL_WEIGHT, None, BF16)
    dgu = _ffn_dact(f"{tag}_dact", dy, dn_w, gu, deps)
    dgu4 = dgu.reshape(nb, s, fs)
    tm, tw = _tile(d, 1024), _tile(fs, 1408)
    nw = fs // tw
    tn = _tile(d, 1024)
    ts, td = _tile(s, 1024), _tile(d, 1024)

    def gate_up_gradient(deps):
        return _mm(f"{tag}_dwgu", h, dgu4, mode="tn", grid=(nb, nw, d // tm),
                   a_spec=pl.BlockSpec((s, tm), lambda k, j, i: (0, i)),
                   b_spec=pl.BlockSpec((None, s, tw), lambda k, j, i: (k, 0, j)),
                   o_spec=pl.BlockSpec((None, tm, tw), lambda k, j, i: (k, i, j)),
                   out_shape=jax.ShapeDtypeStruct((nb, d, fs), BF16), deps=deps)

    def down_gradient(deps):
        return _mm(f"{tag}_dwd", a, dy, mode="tn", grid=(nb, d // tn),
                   a_spec=pl.BlockSpec((s, fr), lambda i, j: (0, i)),
                   b_spec=pl.BlockSpec((s, tn), lambda i, j: (0, j)),
                   o_spec=pl.BlockSpec((None, fr, tn), lambda i, j: (i, 0, j)),
                   out_shape=jax.ShapeDtypeStruct((nb, fr, d), BF16), deps=deps)

    def input_gradient(deps):
        dh = _mm(f"{tag}_dh", dgu4, gu_w, mode="nt", grid=(s // ts, d // td, nb),
                 a_spec=pl.BlockSpec((None, ts, fs), lambda i, j, k: (k, i, 0)),
                 b_spec=pl.BlockSpec((None, td, fs), lambda i, j, k: (k, j, 0)),
                 o_spec=pl.BlockSpec((ts, td), lambda i, j, k: (i, j)),
                 out_shape=jax.ShapeDtypeStruct((s, d), F32), nk=nb, acc_shape=(ts, td), deps=deps)
        return _norm_bwd(f"{tag}_pre_bwd", dh, x, g_pre, 1.0, dx_new, F32, following)

    if last is None:
        started = red.start(kinds, layer, [gate_up_gradient(()), down_gradient(())])
        dx, dg_pre, *next_head = input_gradient((started,))
    else:
        dx, dg_pre, *next_head = input_gradient(())
        first = red.start(kinds[:1], layer, [gate_up_gradient((last(dg_pre, dg_post),))])
        second = red.start(kinds[1:], layer, [down_gradient((first,))])
        red.exchange(kinds[:1], layer, second)
    return dx, dg_pre, dg_post, tuple(next_head) or None


def _mixer_forward(tag, x, h, gains, next_gain, feed, k, conv_taps, dims):
    qd, kvd, cd = dims
    s, d = x.shape
    _, g_a, g_c, g_post = gains
    win_w = feed.take(k, h)
    nb, cw = win_w.shape[0], win_w.shape[2]
    tm = _tile(s, 1024)
    z = _mm(f"{tag}_in", h, win_w, mode="nn", grid=(nb, s // tm),
            a_spec=pl.BlockSpec((tm, d), lambda j, i: (i, 0)),
            b_spec=pl.BlockSpec((None, d, cw), lambda j, i: (j, 0, 0)),
            o_spec=pl.BlockSpec((tm, cw), lambda j, i: (i, j)),
            out_shape=jax.ShapeDtypeStruct((s, nb * cw), BF16))
    a, lse = _attn_fwd(f"{tag}_attn", z, qd, kvd)
    c = _conv_fwd(f"{tag}_conv", z, conv_taps, qd + 2 * kvd, cd)
    cat = _cat_norm_fwd(f"{tag}_cat", a, c, g_a, g_c)
    wout_w = feed.take(k + 1, cat).reshape(-1, d)
    mw = qd + cd
    tn = _tile(d, 1024)
    mixed = _mm(f"{tag}_out", cat, wout_w, mode="nn", grid=(s // tm, d // tn),
                a_spec=pl.BlockSpec((tm, mw), lambda i, j: (i, 0)),
                b_spec=pl.BlockSpec((mw, tn), lambda i, j: (0, j)),
                o_spec=pl.BlockSpec((tm, tn), lambda i, j: (i, j)),
                out_shape=jax.ShapeDtypeStruct((s, d), F32))
    x_new, h_next = _res_norm(f"{tag}_post", x, mixed, g_post, 1.0, next_gain)
    return x_new, h_next, (x, h, z, a, lse, c, cat, mixed)


def _mixer_backward(tag, dx_new, saved, gains, win_w, conv_taps, wout_w, dims, red, kinds, layer, deps, head,
                    following):
    qd, kvd, cd = dims
    x, h, z, a, lse, c, cat, mixed = saved
    s, d = x.shape
    nb, cw = win_w.shape[0], win_w.shape[2]
    g_pre, g_a, g_c, g_post = gains
    mw = qd + cd
    dmixed, dg_post = head or _norm_bwd(f"{tag}_post_bwd", dx_new, mixed, g_post, 1.0, None, BF16)
    tm, tn = _tile(s, 1024), _tile(mw, 1024)
    dcat = _mm(f"{tag}_dcat", dmixed, wout_w, mode="nt", grid=(s // tm, mw // tn),
               a_spec=pl.BlockSpec((tm, d), lambda i, j: (i, 0)),
               b_spec=pl.BlockSpec((tn, d), lambda i, j: (j, 0)),
               o_spec=pl.BlockSpec((tm, tn), lambda i, j: (i, j)),
               out_shape=jax.ShapeDtypeStruct((s, mw), F32), deps=deps)
    wr = mw // nb
    td = _tile(d, 1024)
    d_wout = _mm(f"{tag}_dwout", cat, dmixed, mode="tn", grid=(nb, d // td),
                 a_spec=pl.BlockSpec((s, wr), lambda i, j: (0, i)),
                 b_spec=pl.BlockSpec((s, td), lambda i, j: (0, j)),
                 o_spec=pl.BlockSpec((None, wr, td), lambda i, j: (i, 0, j)),
                 out_shape=jax.ShapeDtypeStruct((nb, wr, d), BF16))
    da, dc, dg_a, dg_c = _cat_norm_bwd(f"{tag}_cat_bwd", dcat, a, c, g_a, g_c)
    dhc, dbg, dcg, d_taps = _conv_bwd(f"{tag}_conv_bwd", z, conv_taps, dc, qd + 2 * kvd, cd)
    dq, dk, dv = _attn_bwd(f"{tag}_attn_bwd", z, a, lse, da, qd, kvd)
    dz = jnp.concatenate([dq, dk, dv, dhc, dbg, dcg], axis=1)
    th = _tile(d, 1024)
    d_win = _mm(f"{tag}_dwin", h, dz, mode="tn", grid=(nb, d // th),
                a_spec=pl.BlockSpec((s, th), lambda k, i: (0, i)),
                b_spec=pl.BlockSpec((s, cw), lambda k, i: (0, k)),
                o_spec=pl.BlockSpec((None, th, cw), lambda k, i: (k, i, 0)),
                out_shape=jax.ShapeDtypeStruct((nb, d, cw), BF16))
    started = (red.start(kinds, layer, [d_win, d_wout]),)
    dh = _mm(f"{tag}_dh", dz, win_w, mode="nt", grid=(s // tm, d // td, nb),
             a_spec=pl.BlockSpec((tm, cw), lambda i, j, k: (i, k)),
             b_spec=pl.BlockSpec((None, td, cw), lambda i, j, k: (k, j, 0)),
             o_spec=pl.BlockSpec((tm, td), lambda i, j, k: (i, j)),
             out_shape=jax.ShapeDtypeStruct((s, d), F32), nk=nb, acc_shape=(tm, td), deps=started)
    dx, dg_pre, *next_head = _norm_bwd(f"{tag}_pre_bwd", dh, x, g_pre, 1.0, dx_new, F32, following)
    return dx, d_taps, (dg_pre, dg_a, dg_c, dg_post), tuple(next_head) or None


def kernel(x, ffn1_norm_pre, ffn1_w_gate_up, ffn1_w_down, ffn1_norm_post, mix_norm_pre, w_in, conv_w, attn_out_norm, conv_out_norm, w_out, mix_norm_post, ffn2_norm_pre, ffn2_w_gate_up, ffn2_w_down, ffn2_norm_post, loss_target, m_ffn1_norm_pre, m_ffn1_w_gate_up, m_ffn1_w_down, m_ffn1_norm_post, m_mix_norm_pre, m_w_in, m_conv_w, m_attn_out_norm, m_conv_out_norm, m_w_out, m_mix_norm_post, m_ffn2_norm_pre, m_ffn2_w_gate_up, m_ffn2_w_down, m_ffn2_norm_post, v_ffn1_norm_pre, v_ffn1_w_gate_up, v_ffn1_w_down, v_ffn1_norm_post, v_mix_norm_pre, v_w_in, v_conv_w, v_attn_out_norm, v_conv_out_norm, v_w_out, v_mix_norm_post, v_ffn2_norm_pre, v_ffn2_w_gate_up, v_ffn2_w_down, v_ffn2_norm_post):
    _, s, d = x.shape
    n_layers = ffn1_norm_pre.shape[0]
    qd = attn_out_norm.shape[1]
    cd = conv_out_norm.shape[1]
    kvd = qd // Q_PER_KV
    dims = (qd, kvd, cd)
    assert N_CHIPS * w_in.shape[2] == qd + 2 * kvd + 3 * cd and qd + cd == N_CHIPS * w_out.shape[1]
    assert 2 * d <= SMALL_ROWS * LANES * SUBLANES
    chip = 2 * lax.axis_index("x") + lax.axis_index("y")
    chip_arr = chip.astype(jnp.int32).reshape(1)
    core = lax.axis_index("c").astype(jnp.int32).reshape(1)
    kinds = ("gu1", "dn1", "win", "wout", "gu2", "dn2")

    big = (ffn1_w_gate_up, ffn1_w_down, w_in, w_out, ffn2_w_gate_up, ffn2_w_down)
    nk = len(kinds)
    taps_all = _gather_taps(conv_w)
    feed = _WeightFeed()
    order = [(k, w, layer) for layer in range(n_layers) for k, w in zip(kinds, big)]
    k, w, layer = order[0]
    token = feed.start("gather_start_first", [_cast_into_slot(f"cast_{k}_{layer}", w, layer, chip_arr)], taps_all)
    feed.start("gather_start_rest", [_cast_into_slot(f"cast_{k}_{layer}", w, layer, chip_arr, (token,))
                                     for k, w, layer in order[1:]], token)
    taps = jnp.transpose(taps_all, (1, 2, 0, 3)).reshape(n_layers, CONV_WIDTH, cd)
    taps = jnp.pad(taps, ((0, 0), (0, SUBLANES - CONV_WIDTH), (0, 0)))

    def gain(g, layer):
        return g[layer][None, :]

    xs = x[0]
    hs = _norm_fwd("l0_ffn1_norm", xs, gain(ffn1_norm_pre, 0))
    saved = []
    for layer in range(n_layers):
        t = f"l{layer}"
        k0 = layer * nk
        xs, hs, s1 = _ffn_forward(f"{t}_ffn1", xs, hs, gain(ffn1_norm_post, layer), gain(mix_norm_pre, layer), feed, k0)
        mix_gains = (gain(mix_norm_pre, layer), gain(attn_out_norm, layer), gain(conv_out_norm, layer), gain(mix_norm_post, layer))
        xs, hs, s2 = _mixer_forward(f"{t}_mix", xs, hs, mix_gains, gain(ffn2_norm_pre, layer), feed, k0 + 2,
                                    taps[layer], dims)
        following = gain(ffn1_norm_pre, layer + 1) if layer + 1 < n_layers else None
        xs, hs, s3 = _ffn_forward(f"{t}_ffn2", xs, hs, gain(ffn2_norm_post, layer), following, feed, k0 + 4)
        saved.append((s1, s2, s3, mix_gains))
    wts = {k: [feed.fulls[layer * nk + i] for layer in range(n_layers)] for i, k in enumerate(kinds)}
    for k in ("dn1", "wout", "dn2"):
        wts[k] = [w.reshape(-1, d) for w in wts[k]]
    top = n_layers - 1
    dxs, loss_part, *head = _loss_head("loss_head", xs, loss_target[0],
                                       (saved[top][2][4], gain(ffn2_norm_post, top), FFN_RESIDUAL_WEIGHT))
    loss = lax.psum(jnp.sum(loss_part), ("x", "y", "c"))

    red = _GradReduce(core, chip_arr, n_layers, per_layer=("gu1", "dn1"))
    small = [None] * n_layers
    flow = {"deps": (), "in_flight": None}

    def between(dx, group):
        after = dx
        if flow["in_flight"] is not None:
            after = red.finish(*flow["in_flight"], after)
        flow["deps"] = (red.exchange(*group, after),)
        flow["in_flight"] = group

    head = tuple(head)
    for layer in reversed(range(n_layers)):
        t = f"l{layer}"
        s1, s2, s3, mix_gains = saved[layer]
        after_ffn2 = (s2[7], mix_gains[3], 1.0)
        after_mix = (s1[4], gain(ffn1_norm_post, layer), FFN_RESIDUAL_WEIGHT)
        after_ffn1 = ((saved[layer - 1][2][4], gain(ffn2_norm_post, layer - 1), FFN_RESIDUAL_WEIGHT)
                      if layer > 0 else None)
        dxs, p_pre2, p_post2, head = _ffn_backward(
            f"{t}_ffn2", dxs, s3, gain(ffn2_norm_pre, layer), gain(ffn2_norm_post, layer),
            wts["gu2"][layer], wts["dn2"][layer], red, ("gu2", "dn2"), layer, flow["deps"], head, after_ffn2)
        between(dxs, (("gu2", "dn2"), layer))
        dxs, p_taps, (p_mpre, p_a, p_c, p_mpost), head = _mixer_backward(
            f"{t}_mix", dxs, s2, mix_gains, wts["win"][layer], taps[layer], wts["wout"][layer], dims,
            red, ("win", "wout"), layer, flow["deps"], head, after_mix)
        between(dxs, (("win", "wout"), layer))
        def pack_small(p_pre1, p_post1):
            tap_rows = jnp.zeros((CONV_WIDTH, SUBLANES, d), F32).at[:, 0, :cd].set(p_taps[:CONV_WIDTH])
            rows = [p_pre1, p_post1, p_mpre, jnp.concatenate([p_a, p_c], axis=1), p_mpost, p_pre2, p_post2]
            rows = jnp.concatenate([jnp.stack(rows), tap_rows], axis=0)
            small[layer] = jnp.pad(rows, ((0, SMALL_ROWS - rows.shape[0]), (0, 0), (0, 0)))

        def reduce_small(p_pre1, p_post1):
            pack_small(p_pre1, p_post1)
            flow["small"] = _allreduce_small(jnp.concatenate(small, axis=0))
            return flow["small"]

        dxs, p_pre1, p_post1, head = _ffn_backward(
            f"{t}_ffn1", dxs, s1, gain(ffn1_norm_pre, layer), gain(ffn1_norm_post, layer),
            wts["gu1"][layer], wts["dn1"][layer], red, ("gu1", "dn1"), layer, flow["deps"], head, after_ffn1,
            last=reduce_small if layer == 0 else None)
        if layer > 0:
            pack_small(p_pre1, p_post1)
        between(dxs, (("dn1",) if layer == 0 else ("gu1", "dn1"), layer))
    grad_x = dxs[None]

    weights = dict(ffn1_norm_pre=ffn1_norm_pre, ffn1_w_gate_up=ffn1_w_gate_up, ffn1_w_down=ffn1_w_down, ffn1_norm_post=ffn1_norm_post, mix_norm_pre=mix_norm_pre, w_in=w_in, conv_w=conv_w, attn_out_norm=attn_out_norm, conv_out_norm=conv_out_norm, w_out=w_out, mix_norm_post=mix_norm_post, ffn2_norm_pre=ffn2_norm_pre, ffn2_w_gate_up=ffn2_w_gate_up, ffn2_w_down=ffn2_w_down, ffn2_norm_post=ffn2_norm_post)
    m_in = dict(ffn1_norm_pre=m_ffn1_norm_pre, ffn1_w_gate_up=m_ffn1_w_gate_up, ffn1_w_down=m_ffn1_w_down, ffn1_norm_post=m_ffn1_norm_post, mix_norm_pre=m_mix_norm_pre, w_in=m_w_in, conv_w=m_conv_w, attn_out_norm=m_attn_out_norm, conv_out_norm=m_conv_out_norm, w_out=m_w_out, mix_norm_post=m_mix_norm_post, ffn2_norm_pre=m_ffn2_norm_pre, ffn2_w_gate_up=m_ffn2_w_gate_up, ffn2_w_down=m_ffn2_w_down, ffn2_norm_post=m_ffn2_norm_post)
    v_in = dict(ffn1_norm_pre=v_ffn1_norm_pre, ffn1_w_gate_up=v_ffn1_w_gate_up, ffn1_w_down=v_ffn1_w_down, ffn1_norm_post=v_ffn1_norm_post, mix_norm_pre=v_mix_norm_pre, w_in=v_w_in, conv_w=v_conv_w, attn_out_norm=v_attn_out_norm, conv_out_norm=v_conv_out_norm, w_out=v_w_out, mix_norm_post=v_mix_norm_post, ffn2_norm_pre=v_ffn2_norm_pre, ffn2_w_gate_up=v_ffn2_w_gate_up, ffn2_w_down=v_ffn2_w_down, ffn2_norm_post=v_ffn2_norm_post)
    kind_name = dict(gu1="ffn1_w_gate_up", dn1="ffn1_w_down", win="w_in", wout="w_out", gu2="ffn2_w_gate_up", dn2="ffn2_w_down")
    delta, new_m, new_v, grad = {}, {}, {}, {}

    def join_and_update(name, items, deps, after):
        ts, send_sems, recv_sems = _join_start(name, [red.bufs[it] for it in items], deps)
        for a, it in enumerate(items):
            k, layer = it if isinstance(it, tuple) else (it, None)
            n = kind_name[k]
            tag = n if layer is None else f"{n}_{layer}"
            g = _join_wait(f"join_wait_{tag}", ts[a], a, send_sems, recv_sems, after)
            prev = (delta[n], new_m[n], new_v[n], grad[n]) if n in delta else None
            delta[n], new_m[n], new_v[n], grad[n] = _adamw(f"adamw_{tag}", weights[n], g, m_in[n], v_in[n], True,
                                                           layer, prev)
            after = delta[n]
        return after

    early = ("wout", "win", "dn2", "gu2") + tuple((k, layer) for layer in range(1, n_layers) for k in ("dn1", "gu1"))
    last_groups = ((("gu1",), 0), flow["in_flight"])
    done_early = join_and_update("join_early", early, tuple(red.scatter_tokens[g] for g in last_groups), dxs)
    for group in last_groups:
        red.finish(*group, done_early)
    join_and_update("join_late", (("dn1", 0), ("gu1", 0)), (), done_early)

    small_sum = flow["small"].reshape(n_layers, SMALL_ROWS, d)
    g_ffn1_pre, g_ffn1_post, g_mix_pre = small_sum[:, 0], small_sum[:, 1], small_sum[:, 2]
    g_attn_out, g_conv_out = small_sum[:, 3, :qd], small_sum[:, 3, qd:qd + cd]
    g_mix_post, g_ffn2_pre, g_ffn2_post = small_sum[:, 4], small_sum[:, 5], small_sum[:, 6]
    cc = conv_w.shape[2]
    g_conv = lax.dynamic_slice_in_dim(small_sum[:, 7:7 + CONV_WIDTH, :cd], chip * cc, cc, axis=2)

    grad.update(ffn1_norm_pre=g_ffn1_pre, ffn1_norm_post=g_ffn1_post, mix_norm_pre=g_mix_pre, conv_w=g_conv, attn_out_norm=g_attn_out, conv_out_norm=g_conv_out, mix_norm_post=g_mix_post, ffn2_norm_pre=g_ffn2_pre, ffn2_norm_post=g_ffn2_post)
    names = list(weights)

    vectors = [n for n in names if n not in kind_name.values()]

    def pack(tree):
        flat = jnp.concatenate([tree[n].reshape(-1) for n in vectors])
        return jnp.pad(flat, (0, -flat.size % (SUBLANES * LANES))).reshape(-1, LANES)

    packed = _adamw("adamw_small", pack(weights), pack(grad), pack(m_in), pack(v_in))
    offset = 0
    for n in vectors:
        size = weights[n].size
        for tree, flat in zip((delta, new_m, new_v), packed):
            tree[n] = flat.reshape(-1)[offset:offset + size].reshape(weights[n].shape)
        offset += size

    return (loss, grad_x, *[grad[n] for n in names], *[delta[n] for n in names],
            *[new_m[n] for n in names], *[new_v[n] for n in names])
```

```python
import functools

import jax
import jax.numpy as jnp
from jax import lax
from jax.experimental import pallas as pl
from jax.experimental.pallas import tpu as pltpu

F32 = jnp.float32
BF16 = jnp.bfloat16
MESH = pl.DeviceIdType.MESH

NORM_EPS = 1e-6
HEAD_DIM = 128
Q_PER_KV = 4
CONV_WIDTH = 3
FFN_RESIDUAL_WEIGHT = 0.5
DILATED_BRANCHES = ((128, 1), (512, 4), (2048, 16))
ADAM_LR = 0.001
ADAM_B1 = 0.9
ADAM_B2 = 0.999
ADAM_EPS = 1e-08
ADAM_WD = 0.01
ADAM_STEP = 10

N_CHIPS = 4
N_DEV = 8
V7X_VMEM_BYTES = 64 << 20
VMEM_LIMIT = V7X_VMEM_BYTES - (12 << 20)
SUBLANES = 8
LANES = 128
SMALL_ROWS = 16
BIG_BLOCK = 4 << 20


def _params(*sem):
    return pltpu.CompilerParams(dimension_semantics=sem, vmem_limit_bytes=VMEM_LIMIT)


def _row_tile(rows, cols, itemsize=4, budget=2 << 20):
    t = rows
    while t * cols * itemsize > budget and t % 32 == 0:
        t //= 2
    return t


def _sum_to_sublanes(v):
    r, n = v.shape
    return v.reshape(r // SUBLANES, SUBLANES, n).sum(axis=0)


_DIMS = {
    "nn": (((1,), (0,)), ((), ())),
    "nt": (((1,), (1,)), ((), ())),
    "tn": (((0,), (0,)), ((), ())),
}


ANY_SPEC = pl.BlockSpec(memory_space=pl.ANY)


def _dot(a, b, mode):
    return lax.dot_general(a, b, _DIMS[mode], preferred_element_type=F32)


def _mm(name, a, b, *, mode, grid, a_spec, b_spec, o_spec, out_shape, nk=1, acc_shape=None, deps=()):
    nd = len(deps)

    def body(a_ref, b_ref, *rest):
        o_ref, scratch = rest[nd], rest[nd + 1:]
        r = _dot(a_ref[...], b_ref[...], mode)
        if nk == 1:
            o_ref[...] = r.astype(o_ref.dtype)
        else:
            acc = scratch[0]
            k = pl.program_id(len(grid) - 1)

            @pl.when(k == 0)
            def _():
                acc[...] = r

            @pl.when(k > 0)
            def _():
                acc[...] += r

            @pl.when(k == nk - 1)
            def _():
                o_ref[...] = acc[...].astype(o_ref.dtype)

    sem = ("parallel",) * (len(grid) - (1 if nk > 1 else 0)) + (("arbitrary",) if nk > 1 else ())
    return pl.pallas_call(
        body, name=name, grid=grid, in_specs=[a_spec, b_spec] + [ANY_SPEC] * nd, out_specs=o_spec,
        out_shape=out_shape, scratch_shapes=[pltpu.VMEM(acc_shape, F32)] if nk > 1 else [],
        compiler_params=_params(*sem),
    )(a, b, *deps)


def _tile(n, want):
    if n <= want:
        return n
    best = None
    for t in range(LANES, want + 1, LANES):
        if n % t == 0:
            best = t
    assert best is not None, (n, want)
    return best


def _norm_fwd(name, x, gain):
    s, d = x.shape
    tr = _row_tile(s, d, budget=BIG_BLOCK)

    def body(x_ref, g_ref, o_ref):
        xv = x_ref[...]
        r = lax.rsqrt(jnp.mean(xv * xv, axis=-1, keepdims=True) + NORM_EPS)
        o_ref[...] = (xv * r * g_ref[...]).astype(o_ref.dtype)

    return pl.pallas_call(
        body, name=name, grid=(s // tr,),
        in_specs=[pl.BlockSpec((tr, d), lambda i: (i, 0)), pl.BlockSpec((1, d), lambda i: (0, 0))],
        out_specs=pl.BlockSpec((tr, d), lambda i: (i, 0)),
        out_shape=jax.ShapeDtypeStruct((s, d), BF16), compiler_params=_params("parallel"),
    )(x, gain)


def _res_norm(name, x, y, gain, scale, next_gain=None):
    s, d = x.shape
    tr = _row_tile(s, d, budget=BIG_BLOCK)
    with_next = next_gain is not None

    def body(x_ref, y_ref, g_ref, *rest):
        yv = y_ref[...]
        r = lax.rsqrt(jnp.mean(yv * yv, axis=-1, keepdims=True) + NORM_EPS)
        xn = x_ref[...] + scale * (yv * r * g_ref[...])
        if with_next:
            ng_ref, o_ref, h_ref = rest
            rn = lax.rsqrt(jnp.mean(xn * xn, axis=-1, keepdims=True) + NORM_EPS)
            h_ref[...] = (xn * rn * ng_ref[...]).astype(h_ref.dtype)
        else:
            o_ref, = rest
        o_ref[...] = xn

    row = pl.BlockSpec((tr, d), lambda i: (i, 0))
    vec = pl.BlockSpec((1, d), lambda i: (0, 0))
    outs = pl.pallas_call(
        body, name=name, grid=(s // tr,),
        in_specs=[row, row, vec] + ([vec] if with_next else []), out_specs=[row] * (2 if with_next else 1),
        out_shape=[jax.ShapeDtypeStruct((s, d), F32)] + ([jax.ShapeDtypeStruct((s, d), BF16)] if with_next else []),
        compiler_params=_params("parallel"),
    )(x, y, gain, *((next_gain,) if with_next else ()))
    return (outs[0], outs[1]) if with_next else (outs[0], None)


def _rms_bwd(dn, yv, gv):
    r = lax.rsqrt(jnp.mean(yv * yv, axis=-1, keepdims=True) + NORM_EPS)
    xhat = yv * r
    dxn = dn * gv
    return r * (dxn - xhat * jnp.mean(dxn * xhat, axis=-1, keepdims=True)), _sum_to_sublanes(dn * xhat)


def _accumulate(ref, part):
    @pl.when(pl.program_id(0) == 0)
    def _():
        ref[...] = part

    @pl.when(pl.program_id(0) > 0)
    def _():
        ref[...] += part


def _norm_bwd(name, dout, yin, gain, scale, resid, out_dtype, following=None):
    s, d = yin.shape
    tr = _row_tile(s, d)
    has_resid = resid is not None
    chained = following is not None

    def body(*refs):
        refs = list(refs)
        do_ref, y_ref, g_ref = refs[:3]
        del refs[:3]
        r_ref = refs.pop(0) if has_resid else None
        if chained:
            y2_ref, g2_ref = refs[:2]
            del refs[:2]
        di_ref, dg_ref = refs[:2]
        din, part = _rms_bwd(scale * do_ref[...], y_ref[...], g_ref[...])
        _accumulate(dg_ref, part)
        if has_resid:
            din = din + r_ref[...]
        di_ref[...] = din.astype(di_ref.dtype)
        if chained:
            d2_ref, dg2_ref = refs[2:]
            d2, part2 = _rms_bwd(following[2] * din, y2_ref[...], g2_ref[...])
            _accumulate(dg2_ref, part2)
            d2_ref[...] = d2.astype(d2_ref.dtype)

    row = pl.BlockSpec((tr, d), lambda i: (i, 0))
    vec = pl.BlockSpec((1, d), lambda i: (0, 0))
    acc = pl.BlockSpec((SUBLANES, d), lambda i: (0, 0))
    ins = [row, row, vec] + ([row] if has_resid else []) + ([row, vec] if chained else [])
    args = (dout, yin, gain) + ((resid,) if has_resid else ()) + (tuple(following[:2]) if chained else ())
    outs = [row, acc] + ([row, acc] if chained else [])
    shapes = [jax.ShapeDtypeStruct((s, d), out_dtype), jax.ShapeDtypeStruct((SUBLANES, d), F32)]
    if chained:
        shapes += [jax.ShapeDtypeStruct((s, d), BF16), jax.ShapeDtypeStruct((SUBLANES, d), F32)]
    return pl.pallas_call(
        body, name=name, grid=(s // tr,), in_specs=ins, out_specs=outs, out_shape=shapes,
        compiler_params=_params("arbitrary"),
    )(*args)


def _loss_head(name, y, target, following):
    s, d = y.shape
    tr = _row_tile(s, d)
    y2, gain2, scale2 = following

    def body(y_ref, t_ref, y2_ref, g2_ref, dy_ref, l_ref, d2_ref, dg2_ref):
        e = y_ref[...] - t_ref[...]
        dy = e * (1.0 / d)
        dy_ref[...] = dy
        _accumulate(l_ref, _sum_to_sublanes(e * e) * (0.5 / d))
        d2, part2 = _rms_bwd(scale2 * dy, y2_ref[...], g2_ref[...])
        _accumulate(dg2_ref, part2)
        d2_ref[...] = d2.astype(d2_ref.dtype)

    row = pl.BlockSpec((tr, d), lambda i: (i, 0))
    acc = pl.BlockSpec((SUBLANES, d), lambda i: (0, 0))
    return pl.pallas_call(
        body, name=name, grid=(s // tr,), in_specs=[row, row, row, pl.BlockSpec((1, d), lambda i: (0, 0))],
        out_specs=[row, acc, row, acc],
        out_shape=[jax.ShapeDtypeStruct((s, d), F32), jax.ShapeDtypeStruct((SUBLANES, d), F32),
                   jax.ShapeDtypeStruct((s, d), BF16), jax.ShapeDtypeStruct((SUBLANES, d), F32)],
        compiler_params=_params("arbitrary"),
    )(y, target, y2, gain2)


def _ffn_up(name, h, gu_w):
    s, d = h.shape
    nb, _, fs = gu_w.shape
    hb = nb // 2
    w = gu_w.reshape(2, hb, d, fs)
    tm = _tile(s, 512)
    tn = _tile(fs, 1408)
    nj = fs // tn

    def body(h_ref, w_ref, gu_ref, a_ref):
        hv = h_ref[...]
        g = _dot(hv, w_ref[0], "nn")
        u = _dot(hv, w_ref[1], "nn")
        sg = jax.nn.sigmoid(g)
        silu = g * sg
        gu_ref[0] = (u * (sg * (1.0 + g * (1.0 - sg)))).astype(gu_ref.dtype)
        gu_ref[1] = silu.astype(gu_ref.dtype)
        a_ref[...] = (silu * u).astype(a_ref.dtype)

    return pl.pallas_call(
        body, name=name, grid=(hb, nj, s // tm),
        in_specs=[pl.BlockSpec((tm, d), lambda jb, jo, i: (i, 0)),
                  pl.BlockSpec((2, None, d, tn), lambda jb, jo, i: (0, jb, 0, jo))],
        out_specs=[pl.BlockSpec((2, None, tm, tn), lambda jb, jo, i: (0, jb, i, jo)),
                   pl.BlockSpec((tm, tn), lambda jb, jo, i: (i, jb * nj + jo))],
        out_shape=[jax.ShapeDtypeStruct((2, hb, s, fs), BF16), jax.ShapeDtypeStruct((s, hb * fs), BF16)],
        compiler_params=_params("parallel", "parallel", "parallel"),
    )(h, w)


def _ffn_dact(name, dy, dn_w, gu, deps=()):
    s, d = dy.shape
    _, hb, _, fs = gu.shape
    tm = _tile(s, 1024)
    tn = _tile(fs, 1408)
    nj = fs // tn

    def body(dy_ref, w_ref, gu_ref, *rest):
        o_ref = rest[-1]
        wv = w_ref[...]
        parts = max(1, tm // 256)
        for r in range(parts):
            rows = slice(r * (tm // parts), (r + 1) * (tm // parts))
            da = _dot(dy_ref[rows, :], wv, "nt")
            o_ref[0, rows, :] = (da * gu_ref[0, rows, :].astype(F32)).astype(o_ref.dtype)
            o_ref[1, rows, :] = (da * gu_ref[1, rows, :].astype(F32)).astype(o_ref.dtype)

    blk = pl.BlockSpec((2, None, tm, tn), lambda jb, jo, i: (0, jb, i, jo))
    return pl.pallas_call(
        body, name=name, grid=(hb, nj, s // tm),
        in_specs=[pl.BlockSpec((tm, d), lambda jb, jo, i: (i, 0)),
                  pl.BlockSpec((tn, d), lambda jb, jo, i: (jb * nj + jo, 0)),
                  blk] + [ANY_SPEC] * len(deps),
        out_specs=blk, out_shape=jax.ShapeDtypeStruct(gu.shape, BF16),
        compiler_params=_params("parallel", "parallel", "parallel"),
    )(dy, dn_w, gu, *deps)


_MASKED = -1e30


def _attn_bias(s, tq):
    nd = s // tq
    dist = (jnp.arange(nd)[:, None, None] * tq + jnp.arange(tq)[None, :, None]) - jnp.arange(tq)[None, None, :]
    mult = jnp.zeros(dist.shape, F32)
    for window, dilation in DILATED_BRANCHES:
        mult = mult + ((dist >= 0) & (dist <= window) & (dist % dilation == 0)).astype(F32)
    return jnp.where(mult > 0.0, jnp.log(jnp.maximum(mult, 1.0)), _MASKED)


def _biased(sc, bias, scale):
    tq, tk = bias.shape
    return (sc.reshape(-1, tq, tk) * scale + bias[None]).reshape(sc.shape)


def _attn_specs(s, qd, kvd, tq):
    rw = Q_PER_KV * HEAD_DIM
    qspec = pl.BlockSpec((tq, rw), lambda g, i: (i, g))
    kspec = pl.BlockSpec((s, HEAD_DIM), lambda g, i: (0, qd // HEAD_DIM + g))
    vspec = pl.BlockSpec((s, HEAD_DIM), lambda g, i: (0, (qd + kvd) // HEAD_DIM + g))
    return rw, qspec, kspec, vspec


def _attn_fwd(name, z, qd, kvd):
    s = z.shape[0]
    tq = _tile(s, 256)
    nkv = kvd // HEAD_DIM
    rw, qspec, kspec, vspec = _attn_specs(s, qd, kvd, tq)
    scale = HEAD_DIM ** -0.5

    def body(q_ref, k_ref, v_ref, b_ref, o_ref, l_ref):
        i = pl.program_id(1)
        heads = [slice(h * HEAD_DIM, (h + 1) * HEAD_DIM) for h in range(Q_PER_KV)]
        q_all = jnp.concatenate([q_ref[:, cols] for cols in heads], axis=0)

        def chunk(j, carry):
            mx, den, acc = carry
            k0 = pl.multiple_of(j * tq, tq)
            kc, vc = k_ref[pl.ds(k0, tq), :], v_ref[pl.ds(k0, tq), :]
            sc = _biased(_dot(q_all, kc, "nt"), b_ref[i - j], scale)
            mx_new = jnp.maximum(mx, jnp.max(sc, axis=-1, keepdims=True))
            alpha = jnp.exp(mx - mx_new)
            p = jnp.exp(sc - mx_new)
            return (mx_new, alpha * den + jnp.sum(p, axis=-1, keepdims=True),
                    alpha * acc + _dot(p.astype(BF16), vc, "nn"))

        rows = Q_PER_KV * tq
        init = (jnp.full((rows, 1), _MASKED, F32), jnp.zeros((rows, 1), F32), jnp.zeros((rows, HEAD_DIM), F32))
        mx, den, acc = lax.fori_loop(0, i + 1, chunk, init)
        out = acc / den
        lse = mx + jnp.log(den)
        for h, cols in enumerate(heads):
            o_ref[:, cols] = out[h * tq:(h + 1) * tq]
            l_ref[:, cols] = jnp.broadcast_to(lse[h * tq:(h + 1) * tq], (tq, HEAD_DIM))

    bias = _attn_bias(s, tq)
    return pl.pallas_call(
        body, name=name, grid=(nkv, s // tq),
        in_specs=[qspec, kspec, vspec, pl.BlockSpec(bias.shape, lambda g, i: (0, 0, 0))], out_specs=[qspec, qspec],
        out_shape=[jax.ShapeDtypeStruct((s, qd), F32), jax.ShapeDtypeStruct((s, qd), F32)],
        compiler_params=_params("parallel", "parallel"),
    )(z, z, z, bias)


def _attn_bwd(name, z, o, lse, do, qd, kvd):
    s = z.shape[0]
    tq = _tile(s, 512)
    nkv = kvd // HEAD_DIM
    nq = s // tq
    rw, qspec, kspec, vspec = _attn_specs(s, qd, kvd, tq)
    scale = HEAD_DIM ** -0.5

    def body(q_ref, k_ref, v_ref, o_ref, l_ref, do_ref, b_ref, dq_ref, dk_ref, dv_ref, dk_acc, dv_acc):
        i = pl.program_id(1)
        heads = [slice(h * HEAD_DIM, (h + 1) * HEAD_DIM) for h in range(Q_PER_KV)]

        @pl.when(i == 0)
        def _():
            dk_acc[...] = jnp.zeros_like(dk_acc)
            dv_acc[...] = jnp.zeros_like(dv_acc)

        q_all = jnp.concatenate([q_ref[:, cols] for cols in heads], axis=0)
        do_all = jnp.concatenate([do_ref[:, cols].astype(BF16) for cols in heads], axis=0)
        lse_all = jnp.concatenate([l_ref[:, cols][:, :1] for cols in heads], axis=0)
        delta_all = jnp.concatenate(
            [jnp.sum(do_ref[:, cols] * o_ref[:, cols], axis=-1, keepdims=True) for cols in heads], axis=0)

        def chunk(j, dq):
            k0 = pl.multiple_of(j * tq, tq)
            kc, vc = k_ref[pl.ds(k0, tq), :], v_ref[pl.ds(k0, tq), :]
            p = jnp.exp(_biased(_dot(q_all, kc, "nt"), b_ref[i - j], scale) - lse_all)
            ds = (p * (_dot(do_all, vc, "nt") - delta_all) * scale).astype(BF16)
            dk_acc[pl.ds(k0, tq), :] += _dot(ds, q_all, "tn")
            dv_acc[pl.ds(k0, tq), :] += _dot(p.astype(BF16), do_all, "tn")
            return dq + _dot(ds, kc, "nn")

        dq = lax.fori_loop(0, i + 1, chunk, jnp.zeros((Q_PER_KV * tq, HEAD_DIM), F32))
        for h, cols in enumerate(heads):
            dq_ref[:, cols] = dq[h * tq:(h + 1) * tq].astype(dq_ref.dtype)

        @pl.when(i == nq - 1)
        def _():
            dk_ref[...] = dk_acc[...].astype(dk_ref.dtype)
            dv_ref[...] = dv_acc[...].astype(dv_ref.dtype)

    kvout = pl.BlockSpec((s, HEAD_DIM), lambda g, i: (0, g))
    bias = _attn_bias(s, tq)
    return pl.pallas_call(
        body, name=name, grid=(nkv, nq),
        in_specs=[qspec, kspec, vspec, qspec, qspec, qspec, pl.BlockSpec(bias.shape, lambda g, i: (0, 0, 0))],
        out_specs=[qspec, kvout, kvout],
        out_shape=[jax.ShapeDtypeStruct((s, qd), BF16), jax.ShapeDtypeStruct((s, kvd), BF16),
                   jax.ShapeDtypeStruct((s, kvd), BF16)],
        scratch_shapes=[pltpu.VMEM((s, HEAD_DIM), F32), pltpu.VMEM((s, HEAD_DIM), F32)],
        compiler_params=_params("parallel", "arbitrary"),
    )(z, z, z, o, lse, do, bias)


def _shift_down(v, n):
    rolled = pltpu.roll(v, n, 0)
    t = lax.broadcasted_iota(jnp.int32, v.shape, 0)
    return jnp.where(t >= n, rolled, 0.0)


def _shift_up(v, n):
    rows = v.shape[0]
    rolled = pltpu.roll(v, rows - n, 0)
    t = lax.broadcasted_iota(jnp.int32, v.shape, 0)
    return jnp.where(t < rows - n, rolled, 0.0)


def _conv_specs(s, base, cd, tc):
    zs = [pl.BlockSpec((s, tc), functools.partial(lambda j, off: (0, off + j), off=(base + n * cd) // tc))
          for n in range(3)]
    wspec = pl.BlockSpec((SUBLANES, tc), lambda j: (0, j))
    cspec = pl.BlockSpec((s, tc), lambda j: (0, j))
    return zs, wspec, cspec


def _conv_fwd(name, z, conv_w, base, cd):
    s = z.shape[0]
    tc = _tile(cd, 256)
    zs, wspec, cspec = _conv_specs(s, base, cd, tc)

    def body(h_ref, b_ref, c_ref, w_ref, o_ref):
        u = c_ref[...].astype(F32) * h_ref[...].astype(F32)
        y = w_ref[0:1, :] * _shift_down(u, 2) + w_ref[1:2, :] * _shift_down(u, 1) + w_ref[2:3, :] * u
        o_ref[...] = b_ref[...].astype(F32) * y

    return pl.pallas_call(
        body, name=name, grid=(cd // tc,), in_specs=zs + [wspec], out_specs=cspec,
        out_shape=jax.ShapeDtypeStruct((s, cd), F32), compiler_params=_params("parallel"),
    )(z, z, z, conv_w)


def _conv_bwd(name, z, conv_w, dc, base, cd):
    s = z.shape[0]
    tc = _tile(cd, 256)
    zs, wspec, cspec = _conv_specs(s, base, cd, tc)

    def body(h_ref, b_ref, c_ref, w_ref, dc_ref, dh_ref, db_ref, dcg_ref, dw_ref):
        hv, bv, cv = h_ref[...].astype(F32), b_ref[...].astype(F32), c_ref[...].astype(F32)
        u = cv * hv
        u1, u2 = _shift_down(u, 1), _shift_down(u, 2)
        w0, w1, w2 = w_ref[0:1, :], w_ref[1:2, :], w_ref[2:3, :]
        y = w0 * u2 + w1 * u1 + w2 * u
        dcv = dc_ref[...]
        db_ref[...] = (dcv * y).astype(db_ref.dtype)
        dy = dcv * bv
        du = w2 * dy + w1 * _shift_up(dy, 1) + w0 * _shift_up(dy, 2)
        dh_ref[...] = (du * cv).astype(dh_ref.dtype)
        dcg_ref[...] = (du * hv).astype(dcg_ref.dtype)
        g0 = jnp.sum(dy * u2, axis=0, keepdims=True)
        g1 = jnp.sum(dy * u1, axis=0, keepdims=True)
        g2 = jnp.sum(dy * u, axis=0, keepdims=True)
        r = lax.broadcasted_iota(jnp.int32, (SUBLANES, tc), 0)
        dw_ref[...] = jnp.where(r == 0, g0, jnp.where(r == 1, g1, jnp.where(r == 2, g2, 0.0)))

    return pl.pallas_call(
        body, name=name, grid=(cd // tc,), in_specs=zs + [wspec, cspec],
        out_specs=[cspec, cspec, cspec, wspec],
        out_shape=[jax.ShapeDtypeStruct((s, cd), BF16)] * 3 + [jax.ShapeDtypeStruct((SUBLANES, cd), F32)],
        compiler_params=_params("parallel"),
    )(z, z, z, conv_w, dc)


def _cat_norm_fwd(name, a, c, ga, gc):
    s, qd = a.shape
    cd = c.shape[1]
    tr = _row_tile(s, qd + cd)

    def body(a_ref, c_ref, ga_ref, gc_ref, o_ref):
        av, cv = a_ref[...], c_ref[...]
        ra = lax.rsqrt(jnp.mean(av * av, axis=-1, keepdims=True) + NORM_EPS)
        rc = lax.rsqrt(jnp.mean(cv * cv, axis=-1, keepdims=True) + NORM_EPS)
        o_ref[:, :qd] = (av * ra * ga_ref[...]).astype(o_ref.dtype)
        o_ref[:, qd:] = (cv * rc * gc_ref[...]).astype(o_ref.dtype)

    return pl.pallas_call(
        body, name=name, grid=(s // tr,),
        in_specs=[pl.BlockSpec((tr, qd), lambda i: (i, 0)), pl.BlockSpec((tr, cd), lambda i: (i, 0)),
                  pl.BlockSpec((1, qd), lambda i: (0, 0)), pl.BlockSpec((1, cd), lambda i: (0, 0))],
        out_specs=pl.BlockSpec((tr, qd + cd), lambda i: (i, 0)),
        out_shape=jax.ShapeDtypeStruct((s, qd + cd), BF16), compiler_params=_params("parallel"),
    )(a, c, ga, gc)


def _cat_norm_bwd(name, dcat, a, c, ga, gc):
    s, qd = a.shape
    cd = c.shape[1]
    tr = _row_tile(s, qd + cd)

    def one(dn, yv, gv):
        r = lax.rsqrt(jnp.mean(yv * yv, axis=-1, keepdims=True) + NORM_EPS)
        xhat = yv * r
        dxn = dn * gv
        return r * (dxn - xhat * jnp.mean(dxn * xhat, axis=-1, keepdims=True)), _sum_to_sublanes(dn * xhat)

    def body(d_ref, a_ref, c_ref, ga_ref, gc_ref, da_ref, dc_ref, dga_ref, dgc_ref):
        da, pa = one(d_ref[:, :qd], a_ref[...], ga_ref[...])
        dc, pc = one(d_ref[:, qd:], c_ref[...], gc_ref[...])
        da_ref[...] = da
        dc_ref[...] = dc

        @pl.when(pl.program_id(0) == 0)
        def _():
            dga_ref[...] = pa
            dgc_ref[...] = pc

        @pl.when(pl.program_id(0) > 0)
        def _():
            dga_ref[...] += pa
            dgc_ref[...] += pc

    ra = pl.BlockSpec((tr, qd), lambda i: (i, 0))
    rc = pl.BlockSpec((tr, cd), lambda i: (i, 0))
    return pl.pallas_call(
        body, name=name, grid=(s // tr,),
        in_specs=[pl.BlockSpec((tr, qd + cd), lambda i: (i, 0)), ra, rc,
                  pl.BlockSpec((1, qd), lambda i: (0, 0)), pl.BlockSpec((1, cd), lambda i: (0, 0))],
        out_specs=[ra, rc, pl.BlockSpec((SUBLANES, qd), lambda i: (0, 0)),
                   pl.BlockSpec((SUBLANES, cd), lambda i: (0, 0))],
        out_shape=[jax.ShapeDtypeStruct((s, qd), F32), jax.ShapeDtypeStruct((s, cd), F32),
                   jax.ShapeDtypeStruct((SUBLANES, qd), F32), jax.ShapeDtypeStruct((SUBLANES, cd), F32)],
        compiler_params=_params("arbitrary"),
    )(dcat, a, c, ga, gc)


def _adamw(name, w, g, m, v, emit_grad=False, layer=None, prev=None):
    shape = w.shape
    cols = shape[-1]
    rows = g.size // cols
    tr = _row_tile(rows, cols, budget=3 << 19)
    first = 0 if layer is None else layer * (rows // tr)
    bc1 = 1.0 - ADAM_B1 ** ADAM_STEP
    bc2 = 1.0 - ADAM_B2 ** ADAM_STEP
    n_out = 4 if emit_grad else 3

    def body(w_ref, g_ref, m_ref, v_ref, *rest):
        d_ref, nm_ref, nv_ref = rest[-n_out:][:3]
        gv = g_ref[...]
        mv = ADAM_B1 * m_ref[...] + (1.0 - ADAM_B1) * gv
        vv = ADAM_B2 * v_ref[...] + (1.0 - ADAM_B2) * (gv * gv)
        nm_ref[...] = mv
        nv_ref[...] = vv
        d_ref[...] = -ADAM_LR * ((mv / bc1) / (jnp.sqrt(vv / bc2) + ADAM_EPS) + ADAM_WD * w_ref[...])
        if emit_grad:
            rest[-1][...] = gv

    row = pl.BlockSpec((tr, cols), lambda i: (first + i, 0))
    g_row = pl.BlockSpec((tr, cols), lambda i: (i, 0))
    prev = tuple(prev) if prev is not None else ()
    total = w.size // cols
    outs = pl.pallas_call(
        body, name=name, grid=(rows // tr,), in_specs=[row, g_row, row, row] + [ANY_SPEC] * len(prev),
        out_specs=[row] * n_out, out_shape=[jax.ShapeDtypeStruct((total, cols), F32)] * n_out,
        input_output_aliases={4 + i: i for i in range(len(prev))}, compiler_params=_params("parallel"),
    )(w.reshape(total, cols), g.reshape(rows, cols), m.reshape(total, cols), v.reshape(total, cols),
      *(t.reshape(total, cols) for t in prev))
    return tuple(t.reshape(shape) for t in outs)


HBM_SPEC = pl.BlockSpec(memory_space=pltpu.HBM)


def _mesh_place():
    x, y, c = lax.axis_index("x"), lax.axis_index("y"), lax.axis_index("c")
    other_chips = [(1 - x, y), (x, 1 - y), (1 - x, 1 - y)]
    return x, y, c, other_chips


def _cast_into_slot(name, w, layer, chip, deps=()):
    _, r, cols = w.shape
    tr = _row_tile(r, cols, budget=BIG_BLOCK)

    def body(chip_ref, w_ref, *rest):
        o_ref = rest[-1]
        o_ref[...] = w_ref[...].astype(o_ref.dtype)

    return pl.pallas_call(
        body, name=name,
        grid_spec=pltpu.PrefetchScalarGridSpec(
            num_scalar_prefetch=1, grid=(r // tr,),
            in_specs=[pl.BlockSpec((None, tr, cols), lambda i, chip_ref: (layer, i, 0))] + [ANY_SPEC] * len(deps),
            out_specs=pl.BlockSpec((None, tr, cols), lambda i, chip_ref: (chip_ref[0], i, 0))),
        out_shape=jax.ShapeDtypeStruct((N_CHIPS, r, cols), BF16), compiler_params=_params("parallel"),
    )(chip, w, *deps)


SEM_SPEC = pl.BlockSpec(memory_space=pltpu.SEMAPHORE)
SPLIT_COPY = pltpu.CompilerParams(has_side_effects=pltpu.SideEffectType.DATAFLOW_SIDE_EFFECTING)
N_OTHER = N_CHIPS - 1
TOKEN_SPEC = pl.BlockSpec(memory_space=pltpu.VMEM)
TOKEN_SHAPE = jax.ShapeDtypeStruct((SUBLANES, LANES), F32)


def _in_hbm(arr):
    return pltpu.with_memory_space_constraint(arr, pltpu.HBM)


def _half_rows(ref, chip_idx, core):
    r2 = ref.shape[1] // 2
    return ref.at[chip_idx, pl.ds(core * r2, r2), :]


def _gather_start(name, fulls, after):
    na = len(fulls)

    def body(*refs):
        f_refs = refs[na + 1:2 * na + 1]
        send_sems, recv_sems = refs[2 * na + 1:3 * na + 1], refs[3 * na + 1:4 * na + 1]
        token = refs[4 * na + 1]
        x, y, c, chips = _mesh_place()
        for a in range(na):
            mine = _half_rows(f_refs[a], 2 * x + y, c)
            for j, (cx, cy) in enumerate(chips):
                pltpu.make_async_remote_copy(
                    src_ref=mine, dst_ref=mine, send_sem=send_sems[a].at[j], recv_sem=recv_sems[a].at[j],
                    device_id=(cx, cy, c), device_id_type=MESH).start()
        token[...] = jnp.zeros_like(token)

    outs = pl.pallas_call(
        body, name=name, in_specs=[HBM_SPEC] * na + [ANY_SPEC],
        out_specs=[HBM_SPEC] * na + [SEM_SPEC] * (2 * na) + [TOKEN_SPEC],
        out_shape=[pltpu.HBM(f.shape, f.dtype) for f in fulls] + [pltpu.SemaphoreType.DMA((N_OTHER,))] * (2 * na)
        + [TOKEN_SHAPE],
        input_output_aliases={a: a for a in range(na)}, compiler_params=SPLIT_COPY,
    )(*[_in_hbm(f) for f in fulls], after)
    return list(outs[:na]), list(outs[na:2 * na]), list(outs[2 * na:3 * na]), outs[3 * na]


def _gather_pass_on(name, full, recv_sems, after):
    def body(f_in, recv_sems, after_ref, f_ref, d2d_send, d2d_recv):
        x, y, c, chips = _mesh_place()
        for j, (cx, cy) in enumerate(chips):
            blk = _half_rows(f_ref, 2 * cx + cy, c)
            pltpu.make_async_remote_copy(
                src_ref=blk, dst_ref=blk, send_sem=d2d_send.at[j], recv_sem=recv_sems.at[j],
                device_id=(cx, cy, c), device_id_type=MESH).wait_recv()
            pltpu.make_async_remote_copy(
                src_ref=blk, dst_ref=blk, send_sem=d2d_send.at[j], recv_sem=d2d_recv.at[j],
                device_id=(x, y, 1 - c), device_id_type=MESH).start()

    return pl.pallas_call(
        body, name=name, in_specs=[HBM_SPEC, SEM_SPEC, ANY_SPEC], out_specs=[HBM_SPEC, SEM_SPEC, SEM_SPEC],
        out_shape=[pltpu.HBM(full.shape, full.dtype)] + [pltpu.SemaphoreType.DMA((N_OTHER,))] * 2,
        input_output_aliases={0: 0}, compiler_params=SPLIT_COPY,
    )(full, recv_sems, after)


def _gather_arrive(name, full, ici_send, d2d_send, d2d_recv, after):
    def body(f_in, ici_send, d2d_send, d2d_recv, after_ref, f_ref):
        x, y, c, chips = _mesh_place()
        for j, (cx, cy) in enumerate(chips):
            mine = _half_rows(f_ref, 2 * x + y, c)
            passed = _half_rows(f_ref, 2 * cx + cy, c)
            theirs = _half_rows(f_ref, 2 * cx + cy, 1 - c)
            pltpu.make_async_remote_copy(
                src_ref=mine, dst_ref=mine, send_sem=ici_send.at[j], recv_sem=d2d_recv.at[j],
                device_id=(cx, cy, c), device_id_type=MESH).wait_send()
            pltpu.make_async_remote_copy(
                src_ref=passed, dst_ref=passed, send_sem=d2d_send.at[j], recv_sem=d2d_recv.at[j],
                device_id=(x, y, 1 - c), device_id_type=MESH).wait_send()
            pltpu.make_async_remote_copy(
                src_ref=theirs, dst_ref=theirs, send_sem=d2d_send.at[j], recv_sem=d2d_recv.at[j],
                device_id=(x, y, 1 - c), device_id_type=MESH).wait_recv()

    return pl.pallas_call(
        body, name=name, in_specs=[HBM_SPEC, SEM_SPEC, SEM_SPEC, SEM_SPEC, ANY_SPEC], out_specs=HBM_SPEC,
        out_shape=pltpu.HBM(full.shape, full.dtype), input_output_aliases={0: 0}, compiler_params=SPLIT_COPY,
    )(full, ici_send, d2d_send, d2d_recv, after)


def _gather_taps(conv_w):
    def body(cw_ref, cwf_ref, send_sems, recv_sems, local_sem):
        x, y, c, chips = _mesh_place()
        k_me = 2 * x + y
        local = pltpu.make_async_copy(cw_ref, cwf_ref.at[k_me], local_sem)
        local.start()
        copies = [pltpu.make_async_remote_copy(
            src_ref=cw_ref, dst_ref=cwf_ref.at[k_me], send_sem=send_sems.at[j], recv_sem=recv_sems.at[j],
            device_id=(cx, cy, c), device_id_type=MESH) for j, (cx, cy) in enumerate(chips)]
        for cp in copies:
            cp.start()
        for j, (cx, cy) in enumerate(chips):
            pltpu.make_async_remote_copy(
                src_ref=cw_ref, dst_ref=cwf_ref.at[2 * cx + cy], send_sem=send_sems.at[j], recv_sem=recv_sems.at[j],
                device_id=(cx, cy, c), device_id_type=MESH).wait_recv()
        for cp in copies:
            cp.wait_send()
        local.wait()

    return pl.pallas_call(
        body, name="gather_taps", in_specs=[HBM_SPEC], out_specs=HBM_SPEC,
        out_shape=jax.ShapeDtypeStruct((N_CHIPS,) + conv_w.shape, conv_w.dtype),
        scratch_shapes=[pltpu.SemaphoreType.DMA((N_OTHER,))] * 2 + [pltpu.SemaphoreType.DMA],
    )(conv_w)


def _sibling_half(g_ref, c):
    r2 = g_ref.shape[1] // 2
    return g_ref.at[:, pl.ds((1 - c) * r2, r2), :]


def _swap_copy(g_ref, land_ref, send_sems, recv_sems, a):
    x, y, c, _ = _mesh_place()
    return pltpu.make_async_remote_copy(
        src_ref=_sibling_half(g_ref, c), dst_ref=land_ref, send_sem=send_sems.at[a], recv_sem=recv_sems.at[a],
        device_id=(x, y, 1 - c), device_id_type=MESH)


def _swap_start(name, gs):
    n = len(gs)

    def body(*refs):
        g_refs, land_refs = refs[n:2 * n], refs[2 * n:3 * n]
        send_sems, recv_sems, token = refs[3 * n:]
        for a in range(n):
            _swap_copy(g_refs[a], land_refs[a], send_sems, recv_sems, a).start()
        token[...] = jnp.zeros_like(token)

    outs = pl.pallas_call(
        body, name=name, in_specs=[HBM_SPEC] * n,
        out_specs=[HBM_SPEC] * (2 * n) + [SEM_SPEC, SEM_SPEC, TOKEN_SPEC],
        out_shape=[pltpu.HBM(g.shape, g.dtype) for g in gs]
        + [pltpu.HBM((g.shape[0], g.shape[1] // 2, g.shape[2]), g.dtype) for g in gs]
        + [pltpu.SemaphoreType.DMA((n,)), pltpu.SemaphoreType.DMA((n,)), TOKEN_SHAPE],
        input_output_aliases={a: a for a in range(n)}, compiler_params=SPLIT_COPY,
    )(*[_in_hbm(g) for g in gs])
    return list(outs[:n]), list(outs[n:2 * n]), outs[2 * n], outs[2 * n + 1], outs[2 * n + 2]


def _swap_wait(name, gs, lands, send_sems, recv_sems, after):
    n = len(gs)

    def body(*refs):
        send_sems, recv_sems = refs[2 * n], refs[2 * n + 1]
        g_refs, land_refs = refs[2 * n + 3:3 * n + 3], refs[3 * n + 3:]
        for a in range(n):
            copy = _swap_copy(g_refs[a], land_refs[a], send_sems, recv_sems, a)
            copy.wait_send()
            copy.wait_recv()

    outs = pl.pallas_call(
        body, name=name, in_specs=[HBM_SPEC] * (2 * n) + [SEM_SPEC, SEM_SPEC, ANY_SPEC],
        out_specs=[HBM_SPEC] * (2 * n),
        out_shape=[pltpu.HBM(t.shape, t.dtype) for t in list(gs) + list(lands)],
        input_output_aliases={a: a for a in range(2 * n)}, compiler_params=SPLIT_COPY,
    )(*gs, *lands, send_sems, recv_sems, after)
    return list(outs[:n]), list(outs[n:])


def _add_core_halves(name, g, sib, core):
    nb, r, cols = g.shape
    r2 = r // 2
    tr = _row_tile(r2, cols, itemsize=2, budget=BIG_BLOCK)
    nrt = r2 // tr

    def body(core_ref, g_ref, s_ref, o_ref):
        o_ref[...] = (g_ref[...].astype(F32) + s_ref[...].astype(F32)).astype(o_ref.dtype)

    return pl.pallas_call(
        body, name=name,
        grid_spec=pltpu.PrefetchScalarGridSpec(
            num_scalar_prefetch=1, grid=(nb, nrt),
            in_specs=[pl.BlockSpec((None, tr, cols), lambda k, i, core_ref: (k, core_ref[0] * nrt + i, 0)),
                      pl.BlockSpec((None, tr, cols), lambda k, i, core_ref: (k, i, 0))],
            out_specs=pl.BlockSpec((None, tr, cols), lambda k, i, core_ref: (k, i, 0))),
        out_shape=jax.ShapeDtypeStruct((nb, r2, cols), BF16), compiler_params=_params("parallel", "parallel"),
    )(core, g, sib)


def _scatter_copies(h_refs, land_refs, send_sems, recv_sems):
    x, y, c, chips = _mesh_place()
    return [pltpu.make_async_remote_copy(
        src_ref=h_ref.at[2 * cx + cy], dst_ref=land_ref.at[j],
        send_sem=send_sems.at[a * N_OTHER + j], recv_sem=recv_sems.at[a * N_OTHER + j],
        device_id=(cx, cy, c), device_id_type=MESH)
        for a, (h_ref, land_ref) in enumerate(zip(h_refs, land_refs)) for j, (cx, cy) in enumerate(chips)]


def _scatter_start(name, hs):
    n = len(hs)

    def body(*refs):
        h_refs, land_refs = refs[n:2 * n], refs[2 * n:3 * n]
        send_sems, recv_sems, token = refs[3 * n:]
        for copy in _scatter_copies(h_refs, land_refs, send_sems, recv_sems):
            copy.start()
        token[...] = jnp.zeros_like(token)

    outs = pl.pallas_call(
        body, name=name, in_specs=[HBM_SPEC] * n,
        out_specs=[HBM_SPEC] * (2 * n) + [SEM_SPEC, SEM_SPEC, TOKEN_SPEC],
        out_shape=[pltpu.HBM(h.shape, h.dtype) for h in hs]
        + [pltpu.HBM((N_OTHER,) + h.shape[1:], h.dtype) for h in hs]
        + [pltpu.SemaphoreType.DMA((n * N_OTHER,)), pltpu.SemaphoreType.DMA((n * N_OTHER,)), TOKEN_SHAPE],
        input_output_aliases={a: a for a in range(n)}, compiler_params=SPLIT_COPY,
    )(*[_in_hbm(h) for h in hs])
    return list(outs[:n]), list(outs[n:2 * n]), outs[2 * n], outs[2 * n + 1], outs[2 * n + 2]


def _scatter_wait(name, hs, lands, send_sems, recv_sems, after):
    afters = tuple(after) if isinstance(after, (tuple, list)) else (after,)
    n = len(hs)

    def body(*refs):
        send_sems, recv_sems = refs[2 * n], refs[2 * n + 1]
        h_refs, land_refs = refs[-2 * n:-n], refs[-n:]
        for copy in _scatter_copies(h_refs, land_refs, send_sems, recv_sems):
            copy.wait_send()
            copy.wait_recv()

    outs = pl.pallas_call(
        body, name=name, in_specs=[HBM_SPEC] * (2 * n) + [SEM_SPEC, SEM_SPEC] + [ANY_SPEC] * len(afters),
        out_specs=[HBM_SPEC] * (2 * n),
        out_shape=[pltpu.HBM(t.shape, t.dtype) for t in list(hs) + list(lands)],
        input_output_aliases={a: a for a in range(2 * n)}, compiler_params=SPLIT_COPY,
    )(*hs, *lands, send_sems, recv_sems, *afters)
    return list(outs[:n]), list(outs[n:])


def _sum_chips(name, hs, rcv, core, chip, layer, n_layers, prev):
    _, r2, cols = hs.shape
    tr = _row_tile(r2, cols, budget=BIG_BLOCK)
    nrt = r2 // tr

    def body(core_ref, chip_ref, h_ref, r_ref, *rest):
        o_ref = rest[-1]
        acc = h_ref[...].astype(F32)
        for j in range(N_CHIPS - 1):
            acc = acc + r_ref[j].astype(F32)
        o_ref[...] = acc

    in_specs = [pl.BlockSpec((None, tr, cols), lambda i, core_ref, chip_ref: (chip_ref[0], i, 0)),
                pl.BlockSpec((N_CHIPS - 1, tr, cols), lambda i, core_ref, chip_ref: (0, i, 0))]
    args = [core, chip, hs, rcv]
    aliases = {}
    if prev is not None:
        in_specs.append(pl.BlockSpec(memory_space=pl.ANY))
        args.append(prev)
        aliases = {4: 0}
    return pl.pallas_call(
        body, name=name,
        grid_spec=pltpu.PrefetchScalarGridSpec(
            num_scalar_prefetch=2, grid=(nrt,), in_specs=in_specs,
            out_specs=pl.BlockSpec((None, tr, cols), lambda i, core_ref, chip_ref: (layer, core_ref[0] * nrt + i, 0))),
        out_shape=jax.ShapeDtypeStruct((n_layers, 2 * r2, cols), F32), input_output_aliases=aliases,
        compiler_params=_params("parallel"),
    )(*args)


def _join_copy(t_ref, send_sems, recv_sems, a):
    x, y, c, _ = _mesh_place()
    r2 = t_ref.shape[1] // 2
    mine = t_ref.at[:, pl.ds(c * r2, r2), :]
    return pltpu.make_async_remote_copy(
        src_ref=mine, dst_ref=mine, send_sem=send_sems.at[a], recv_sem=recv_sems.at[a],
        device_id=(x, y, 1 - c), device_id_type=MESH)


def _join_start(name, ts, deps=()):
    n, nd = len(ts), len(deps)

    def body(*refs):
        t_refs = refs[n + nd:2 * n + nd]
        send_sems, recv_sems = refs[2 * n + nd:]
        for a in range(n):
            _join_copy(t_refs[a], send_sems, recv_sems, a).start()

    outs = pl.pallas_call(
        body, name=name, in_specs=[HBM_SPEC] * n + [ANY_SPEC] * nd, out_specs=[HBM_SPEC] * n + [SEM_SPEC, SEM_SPEC],
        out_shape=[pltpu.HBM(t.shape, t.dtype) for t in ts] + [pltpu.SemaphoreType.DMA((n,))] * 2,
        input_output_aliases={a: a for a in range(n)}, compiler_params=SPLIT_COPY,
    )(*[_in_hbm(t) for t in ts], *deps)
    return list(outs[:n]), outs[n], outs[n + 1]


def _join_wait(name, t, a, send_sems, recv_sems, after):
    def body(t_in, send_sems, recv_sems, after_ref, t_ref):
        copy = _join_copy(t_ref, send_sems, recv_sems, a)
        copy.wait_send()
        copy.wait_recv()

    return pl.pallas_call(
        body, name=name, in_specs=[HBM_SPEC, SEM_SPEC, SEM_SPEC, ANY_SPEC], out_specs=HBM_SPEC,
        out_shape=pltpu.HBM(t.shape, t.dtype), input_output_aliases={0: 0}, compiler_params=SPLIT_COPY,
    )(t, send_sems, recv_sems, after)


def _allreduce_small(p):
    n, _, w = p.shape

    def body(p_ref, o_ref, buf, send_sems, recv_sems):
        x, y, c, _ = _mesh_place()
        me = 4 * x + 2 * y + c
        buf[me] = jnp.sum(p_ref[...], axis=1)
        copies = []
        for pat in range(1, N_DEV):
            fx, fy, fc = (pat >> 2) & 1, (pat >> 1) & 1, pat & 1
            copies.append(pltpu.make_async_remote_copy(
                src_ref=buf.at[me], dst_ref=buf.at[me], send_sem=send_sems.at[pat - 1], recv_sem=recv_sems.at[pat - 1],
                device_id=(x ^ fx, y ^ fy, c ^ fc), device_id_type=MESH))
        for cp in copies:
            cp.start()
        for cp in copies:
            cp.wait()
        acc = buf[0]
        for dev in range(1, N_DEV):
            acc = acc + buf[dev]
        o_ref[...] = acc

    return pl.pallas_call(
        body, name="allreduce_small", in_specs=[pl.BlockSpec(memory_space=pltpu.VMEM)],
        out_specs=pl.BlockSpec(memory_space=pltpu.VMEM), out_shape=jax.ShapeDtypeStruct((n, w), F32),
        scratch_shapes=[pltpu.VMEM((N_DEV, n, w), F32), pltpu.SemaphoreType.DMA((N_DEV - 1,)),
                        pltpu.SemaphoreType.DMA((N_DEV - 1,))],
    )(p)


class _WeightFeed:
    def __init__(self):
        self.fulls, self.ici_send, self.ici_recv, self.d2d = [], [], [], []

    def start(self, name, fulls, after):
        started, send, recv, token = _gather_start(name, fulls, after)
        self.fulls += started
        self.ici_send += send
        self.ici_recv += recv
        self.d2d += [None] * len(fulls)
        self.token = token
        return token

    def _pass_on(self, k, after):
        if k == 0:
            after = self.token
        if k < len(self.fulls) and self.d2d[k] is None:
            self.fulls[k], send, recv = _gather_pass_on(f"gather_pass_{k}", self.fulls[k], self.ici_recv[k], after)
            self.d2d[k] = (send, recv)

    def take(self, k, after):
        self._pass_on(k, after)
        self.fulls[k] = _gather_arrive(f"gather_arrive_{k}", self.fulls[k], self.ici_send[k], *self.d2d[k], after)
        return self.fulls[k]


def _ffn_forward(tag, x, h, g_post, next_gain, feed, k):
    s, d = x.shape
    gu_w = feed.take(k, h)
    gu, a = _ffn_up(f"{tag}_up", h, gu_w)
    dn_w = feed.take(k + 1, a).reshape(-1, d)
    f = dn_w.shape[0]
    tm, tn = _tile(s, 1024), _tile(d, 512)
    y = _mm(f"{tag}_down", a, dn_w, mode="nn", grid=(s // tm, d // tn),
            a_spec=pl.BlockSpec((tm, f), lambda i, j: (i, 0)),
            b_spec=pl.BlockSpec((f, tn), lambda i, j: (0, j)),
            o_spec=pl.BlockSpec((tm, tn), lambda i, j: (i, j)),
            out_shape=jax.ShapeDtypeStruct((s, d), F32))
    x_new, h_next = _res_norm(f"{tag}_post", x, y, g_post, FFN_RESIDUAL_WEIGHT, next_gain)
    return x_new, h_next, (x, h, gu, a, y)


class _GradReduce:
    def __init__(self, core, chip, n_layers, per_layer=()):
        self.core, self.chip, self.n_layers, self.per_layer = core, chip, n_layers, per_layer
        self.state = {}
        self.bufs = {}
        self.scatter_tokens = {}

    def start(self, kinds, layer, gs):
        gs, lands, send, recv, token = _swap_start(f"swap_start_{kinds[0]}_{layer}", gs)
        self.state[kinds, layer] = (gs, lands, send, recv)
        return token

    def exchange(self, kinds, layer, after):
        tag = f"{kinds[0]}_{layer}"
        gs, sibs = _swap_wait(f"swap_wait_{tag}", *self.state[kinds, layer], after)
        hs = [_add_core_halves(f"add_cores_{k}_{layer}", g, sib, self.core) for k, g, sib in zip(kinds, gs, sibs)]
        hs, lands, send, recv, token = _scatter_start(f"scatter_start_{tag}", hs)
        self.state[kinds, layer] = (hs, lands, send, recv)
        self.scatter_tokens[kinds, layer] = token
        return token

    def finish(self, kinds, layer, after):
        tag = f"{kinds[0]}_{layer}"
        hs, rcvs = _scatter_wait(f"scatter_wait_{tag}", *self.state.pop((kinds, layer)), after)
        for k, h, rcv in zip(kinds, hs, rcvs):
            if k in self.per_layer:
                last = self.bufs[k, layer] = _sum_chips(f"sum_chips_{k}_{layer}", h, rcv, self.core, self.chip,
                                                        0, 1, None)
            else:
                last = self.bufs[k] = _sum_chips(f"sum_chips_{k}_{layer}", h, rcv, self.core, self.chip, layer,
                                                 self.n_layers, self.bufs.get(k))
        return last


def _ffn_backward(tag, dx_new, saved, g_pre, g_post, gu_w, dn_w, red, kinds, layer, deps, head, following,
                  last=None):
    x, h, gu, a, y = saved
    s, d = x.shape
    nb, fs = gu_w.shape[0], gu_w.shape[2]
    f = dn_w.shape[0]
    fr = f // nb
    dy, dg_post = head or _norm_bwd(f"{tag}_post_bwd", dx_new, y, g_post, FFN_RESIDUAL_WEIGHT, None, BF16)
    dgu = _ffn_dact(f"{tag}_dact", dy, dn_w, gu, deps)
    dgu4 = dgu.reshape(nb, s, fs)
    tm, tw = _tile(d, 1024), _tile(fs, 1408)
    nw = fs // tw
    tn = _tile(d, 1024)
    ts, td = _tile(s, 1024), _tile(d, 1024)

    def gate_up_gradient(deps):
        return _mm(f"{tag}_dwgu", h, dgu4, mode="tn", grid=(nb, nw, d // tm),
                   a_spec=pl.BlockSpec((s, tm), lambda k, j, i: (0, i)),
                   b_spec=pl.BlockSpec((None, s, tw), lambda k, j, i: (k, 0, j)),
                   o_spec=pl.BlockSpec((None, tm, tw), lambda k, j, i: (k, i, j)),
                   out_shape=jax.ShapeDtypeStruct((nb, d, fs), BF16), deps=deps)

    def down_gradient(deps):
        return _mm(f"{tag}_dwd", a, dy, mode="tn", grid=(nb, d // tn),
                   a_spec=pl.BlockSpec((s, fr), lambda i, j: (0, i)),
                   b_spec=pl.BlockSpec((s, tn), lambda i, j: (0, j)),
                   o_spec=pl.BlockSpec((None, fr, tn), lambda i, j: (i, 0, j)),
                   out_shape=jax.ShapeDtypeStruct((nb, fr, d), BF16), deps=deps)

    def input_gradient(deps):
        dh = _mm(f"{tag}_dh", dgu4, gu_w, mode="nt", grid=(s // ts, d // td, nb),
                 a_spec=pl.BlockSpec((None, ts, fs), lambda i, j, k: (k, i, 0)),
                 b_spec=pl.BlockSpec((None, td, fs), lambda i, j, k: (k, j, 0)),
                 o_spec=pl.BlockSpec((ts, td), lambda i, j, k: (i, j)),
                 out_shape=jax.ShapeDtypeStruct((s, d), F32), nk=nb, acc_shape=(ts, td), deps=deps)
        return _norm_bwd(f"{tag}_pre_bwd", dh, x, g_pre, 1.0, dx_new, F32, following)

    if last is None:
        first = red.start(kinds[:1], layer, [gate_up_gradient(())])
        second = red.start(kinds[1:], layer, [down_gradient((first,))])
        dx, dg_pre, *next_head = input_gradient((red.exchange(kinds[:1], layer, second), second))
    else:
        dx, dg_pre, *next_head = input_gradient(())
        first = red.start(kinds[:1], layer, [gate_up_gradient((last(dg_pre, dg_post),))])
        second = red.start(kinds[1:], layer, [down_gradient((first,))])
        red.exchange(kinds[:1], layer, second)
    return dx, dg_pre, dg_post, tuple(next_head) or None


def _mixer_forward(tag, x, h, gains, next_gain, feed, k, conv_taps, dims):
    qd, kvd, cd = dims
    s, d = x.shape
    _, g_a, g_c, g_post = gains
    win_w = feed.take(k, h)
    nb, cw = win_w.shape[0], win_w.shape[2]
    tm = _tile(s, 1024)
    z = _mm(f"{tag}_in", h, win_w, mode="nn", grid=(nb, s // tm),
            a_spec=pl.BlockSpec((tm, d), lambda j, i: (i, 0)),
            b_spec=pl.BlockSpec((None, d, cw), lambda j, i: (j, 0, 0)),
            o_spec=pl.BlockSpec((tm, cw), lambda j, i: (i, j)),
            out_shape=jax.ShapeDtypeStruct((s, nb * cw), BF16))
    a, lse = _attn_fwd(f"{tag}_attn", z, qd, kvd)
    c = _conv_fwd(f"{tag}_conv", z, conv_taps, qd + 2 * kvd, cd)
    cat = _cat_norm_fwd(f"{tag}_cat", a, c, g_a, g_c)
    wout_w = feed.take(k + 1, cat).reshape(-1, d)
    mw = qd + cd
    tn = _tile(d, 1024)
    mixed = _mm(f"{tag}_out", cat, wout_w, mode="nn", grid=(s // tm, d // tn),
                a_spec=pl.BlockSpec((tm, mw), lambda i, j: (i, 0)),
                b_spec=pl.BlockSpec((mw, tn), lambda i, j: (0, j)),
                o_spec=pl.BlockSpec((tm, tn), lambda i, j: (i, j)),
                out_shape=jax.ShapeDtypeStruct((s, d), F32))
    x_new, h_next = _res_norm(f"{tag}_post", x, mixed, g_post, 1.0, next_gain)
    return x_new, h_next, (x, h, z, a, lse, c, cat, mixed)


def _mixer_backward(tag, dx_new, saved, gains, win_w, conv_taps, wout_w, dims, red, kinds, layer, deps, head,
                    following):
    qd, kvd, cd = dims
    x, h, z, a, lse, c, cat, mixed = saved
    s, d = x.shape
    nb, cw = win_w.shape[0], win_w.shape[2]
    g_pre, g_a, g_c, g_post = gains
    mw = qd + cd
    dmixed, dg_post = head or _norm_bwd(f"{tag}_post_bwd", dx_new, mixed, g_post, 1.0, None, BF16)
    tm, tn = _tile(s, 1024), _tile(mw, 1024)
    dcat = _mm(f"{tag}_dcat", dmixed, wout_w, mode="nt", grid=(s // tm, mw // tn),
               a_spec=pl.BlockSpec((tm, d), lambda i, j: (i, 0)),
               b_spec=pl.BlockSpec((tn, d), lambda i, j: (j, 0)),
               o_spec=pl.BlockSpec((tm, tn), lambda i, j: (i, j)),
               out_shape=jax.ShapeDtypeStruct((s, mw), F32), deps=deps)
    wr = mw // nb
    td = _tile(d, 1024)
    d_wout = _mm(f"{tag}_dwout", cat, dmixed, mode="tn", grid=(nb, d // td),
                 a_spec=pl.BlockSpec((s, wr), lambda i, j: (0, i)),
                 b_spec=pl.BlockSpec((s, td), lambda i, j: (0, j)),
                 o_spec=pl.BlockSpec((None, wr, td), lambda i, j: (i, 0, j)),
                 out_shape=jax.ShapeDtypeStruct((nb, wr, d), BF16))
    da, dc, dg_a, dg_c = _cat_norm_bwd(f"{tag}_cat_bwd", dcat, a, c, g_a, g_c)
    dhc, dbg, dcg, d_taps = _conv_bwd(f"{tag}_conv_bwd", z, conv_taps, dc, qd + 2 * kvd, cd)
    dq, dk, dv = _attn_bwd(f"{tag}_attn_bwd", z, a, lse, da, qd, kvd)
    dz = jnp.concatenate([dq, dk, dv, dhc, dbg, dcg], axis=1)
    th = _tile(d, 1024)
    d_win = _mm(f"{tag}_dwin", h, dz, mode="tn", grid=(nb, d // th),
                a_spec=pl.BlockSpec((s, th), lambda k, i: (0, i)),
                b_spec=pl.BlockSpec((s, cw), lambda k, i: (0, k)),
                o_spec=pl.BlockSpec((None, th, cw), lambda k, i: (k, i, 0)),
                out_shape=jax.ShapeDtypeStruct((nb, d, cw), BF16))
    started = (red.start(kinds, layer, [d_win, d_wout]),)
    dh = _mm(f"{tag}_dh", dz, win_w, mode="nt", grid=(s // tm, d // td, nb),
             a_spec=pl.BlockSpec((tm, cw), lambda i, j, k: (i, k)),
             b_spec=pl.BlockSpec((None, td, cw), lambda i, j, k: (k, j, 0)),
             o_spec=pl.BlockSpec((tm, td), lambda i, j, k: (i, j)),
             out_shape=jax.ShapeDtypeStruct((s, d), F32), nk=nb, acc_shape=(tm, td), deps=started)
    dx, dg_pre, *next_head = _norm_bwd(f"{tag}_pre_bwd", dh, x, g_pre, 1.0, dx_new, F32, following)
    return dx, d_taps, (dg_pre, dg_a, dg_c, dg_post), tuple(next_head) or None


def kernel(x, ffn1_norm_pre, ffn1_w_gate_up, ffn1_w_down, ffn1_norm_post, mix_norm_pre, w_in, conv_w, attn_out_norm, conv_out_norm, w_out, mix_norm_post, ffn2_norm_pre, ffn2_w_gate_up, ffn2_w_down, ffn2_norm_post, loss_target, m_ffn1_norm_pre, m_ffn1_w_gate_up, m_ffn1_w_down, m_ffn1_norm_post, m_mix_norm_pre, m_w_in, m_conv_w, m_attn_out_norm, m_conv_out_norm, m_w_out, m_mix_norm_post, m_ffn2_norm_pre, m_ffn2_w_gate_up, m_ffn2_w_down, m_ffn2_norm_post, v_ffn1_norm_pre, v_ffn1_w_gate_up, v_ffn1_w_down, v_ffn1_norm_post, v_mix_norm_pre, v_w_in, v_conv_w, v_attn_out_norm, v_conv_out_norm, v_w_out, v_mix_norm_post, v_ffn2_norm_pre, v_ffn2_w_gate_up, v_ffn2_w_down, v_ffn2_norm_post):
    _, s, d = x.shape
    n_layers = ffn1_norm_pre.shape[0]
    qd = attn_out_norm.shape[1]
    cd = conv_out_norm.shape[1]
    kvd = qd // Q_PER_KV
    dims = (qd, kvd, cd)
    assert N_CHIPS * w_in.shape[2] == qd + 2 * kvd + 3 * cd and qd + cd == N_CHIPS * w_out.shape[1]
    assert 2 * d <= SMALL_ROWS * LANES * SUBLANES
    chip = 2 * lax.axis_index("x") + lax.axis_index("y")
    chip_arr = chip.astype(jnp.int32).reshape(1)
    core = lax.axis_index("c").astype(jnp.int32).reshape(1)
    kinds = ("gu1", "dn1", "win", "wout", "gu2", "dn2")

    big = (ffn1_w_gate_up, ffn1_w_down, w_in, w_out, ffn2_w_gate_up, ffn2_w_down)
    nk = len(kinds)
    taps_all = _gather_taps(conv_w)
    feed = _WeightFeed()
    order = [(k, w, layer) for layer in range(n_layers) for k, w in zip(kinds, big)]
    k, w, layer = order[0]
    token = feed.start("gather_start_first", [_cast_into_slot(f"cast_{k}_{layer}", w, layer, chip_arr)], taps_all)
    feed.start("gather_start_rest", [_cast_into_slot(f"cast_{k}_{layer}", w, layer, chip_arr, (token,))
                                     for k, w, layer in order[1:]], token)
    taps = jnp.transpose(taps_all, (1, 2, 0, 3)).reshape(n_layers, CONV_WIDTH, cd)
    taps = jnp.pad(taps, ((0, 0), (0, SUBLANES - CONV_WIDTH), (0, 0)))

    def gain(g, layer):
        return g[layer][None, :]

    xs = x[0]
    hs = _norm_fwd("l0_ffn1_norm", xs, gain(ffn1_norm_pre, 0))
    saved = []
    for layer in range(n_layers):
        t = f"l{layer}"
        k0 = layer * nk
        xs, hs, s1 = _ffn_forward(f"{t}_ffn1", xs, hs, gain(ffn1_norm_post, layer), gain(mix_norm_pre, layer), feed, k0)
        mix_gains = (gain(mix_norm_pre, layer), gain(attn_out_norm, layer), gain(conv_out_norm, layer), gain(mix_norm_post, layer))
        xs, hs, s2 = _mixer_forward(f"{t}_mix", xs, hs, mix_gains, gain(ffn2_norm_pre, layer), feed, k0 + 2,
                                    taps[layer], dims)
        following = gain(ffn1_norm_pre, layer + 1) if layer + 1 < n_layers else None
        xs, hs, s3 = _ffn_forward(f"{t}_ffn2", xs, hs, gain(ffn2_norm_post, layer), following, feed, k0 + 4)
        saved.append((s1, s2, s3, mix_gains))
    wts = {k: [feed.fulls[layer * nk + i] for layer in range(n_layers)] for i, k in enumerate(kinds)}
    for k in ("dn1", "wout", "dn2"):
        wts[k] = [w.reshape(-1, d) for w in wts[k]]
    top = n_layers - 1
    dxs, loss_part, *head = _loss_head("loss_head", xs, loss_target[0],
                                       (saved[top][2][4], gain(ffn2_norm_post, top), FFN_RESIDUAL_WEIGHT))
    loss = lax.psum(jnp.sum(loss_part), ("x", "y", "c"))

    red = _GradReduce(core, chip_arr, n_layers, per_layer=("gu1", "dn1"))
    small = [None] * n_layers
    flow = {"deps": (), "in_flight": ()}

    def between(dx, group, also=()):
        after = dx
        for g in flow["in_flight"]:
            after = red.finish(*g, after)
        flow["deps"] = (red.exchange(*group, after),)
        flow["in_flight"] = tuple(also) + (group,)

    head = tuple(head)
    for layer in reversed(range(n_layers)):
        t = f"l{layer}"
        s1, s2, s3, mix_gains = saved[layer]
        after_ffn2 = (s2[7], mix_gains[3], 1.0)
        after_mix = (s1[4], gain(ffn1_norm_post, layer), FFN_RESIDUAL_WEIGHT)
        after_ffn1 = ((saved[layer - 1][2][4], gain(ffn2_norm_post, layer - 1), FFN_RESIDUAL_WEIGHT)
                      if layer > 0 else None)
        dxs, p_pre2, p_post2, head = _ffn_backward(
            f"{t}_ffn2", dxs, s3, gain(ffn2_norm_pre, layer), gain(ffn2_norm_post, layer),
            wts["gu2"][layer], wts["dn2"][layer], red, ("gu2", "dn2"), layer, flow["deps"], head, after_ffn2)
        between(dxs, (("dn2",), layer), also=((("gu2",), layer),))
        dxs, p_taps, (p_mpre, p_a, p_c, p_mpost), head = _mixer_backward(
            f"{t}_mix", dxs, s2, mix_gains, wts["win"][layer], taps[layer], wts["wout"][layer], dims,
            red, ("win", "wout"), layer, flow["deps"], head, after_mix)
        between(dxs, (("win", "wout"), layer))
        def pack_small(p_pre1, p_post1):
            tap_rows = jnp.zeros((CONV_WIDTH, SUBLANES, d), F32).at[:, 0, :cd].set(p_taps[:CONV_WIDTH])
            rows = [p_pre1, p_post1, p_mpre, jnp.concatenate([p_a, p_c], axis=1), p_mpost, p_pre2, p_post2]
            rows = jnp.concatenate([jnp.stack(rows), tap_rows], axis=0)
            small[layer] = jnp.pad(rows, ((0, SMALL_ROWS - rows.shape[0]), (0, 0), (0, 0)))

        def reduce_small(p_pre1, p_post1):
            pack_small(p_pre1, p_post1)
            flow["small"] = _allreduce_small(jnp.concatenate(small, axis=0))
            return flow["small"]

        dxs, p_pre1, p_post1, head = _ffn_backward(
            f"{t}_ffn1", dxs, s1, gain(ffn1_norm_pre, layer), gain(ffn1_norm_post, layer),
            wts["gu1"][layer], wts["dn1"][layer], red, ("gu1", "dn1"), layer, flow["deps"], head, after_ffn1,
            last=reduce_small if layer == 0 else None)
        if layer > 0:
            pack_small(p_pre1, p_post1)
        between(dxs, (("dn1",), layer), also=((("gu1",), layer),))
    grad_x = dxs[None]

    weights = dict(ffn1_norm_pre=ffn1_norm_pre, ffn1_w_gate_up=ffn1_w_gate_up, ffn1_w_down=ffn1_w_down, ffn1_norm_post=ffn1_norm_post, mix_norm_pre=mix_norm_pre, w_in=w_in, conv_w=conv_w, attn_out_norm=attn_out_norm, conv_out_norm=conv_out_norm, w_out=w_out, mix_norm_post=mix_norm_post, ffn2_norm_pre=ffn2_norm_pre, ffn2_w_gate_up=ffn2_w_gate_up, ffn2_w_down=ffn2_w_down, ffn2_norm_post=ffn2_norm_post)
    m_in = dict(ffn1_norm_pre=m_ffn1_norm_pre, ffn1_w_gate_up=m_ffn1_w_gate_up, ffn1_w_down=m_ffn1_w_down, ffn1_norm_post=m_ffn1_norm_post, mix_norm_pre=m_mix_norm_pre, w_in=m_w_in, conv_w=m_conv_w, attn_out_norm=m_attn_out_norm, conv_out_norm=m_conv_out_norm, w_out=m_w_out, mix_norm_post=m_mix_norm_post, ffn2_norm_pre=m_ffn2_norm_pre, ffn2_w_gate_up=m_ffn2_w_gate_up, ffn2_w_down=m_ffn2_w_down, ffn2_norm_post=m_ffn2_norm_post)
    v_in = dict(ffn1_norm_pre=v_ffn1_norm_pre, ffn1_w_gate_up=v_ffn1_w_gate_up, ffn1_w_down=v_ffn1_w_down, ffn1_norm_post=v_ffn1_norm_post, mix_norm_pre=v_mix_norm_pre, w_in=v_w_in, conv_w=v_conv_w, attn_out_norm=v_attn_out_norm, conv_out_norm=v_conv_out_norm, w_out=v_w_out, mix_norm_post=v_mix_norm_post, ffn2_norm_pre=v_ffn2_norm_pre, ffn2_w_gate_up=v_ffn2_w_gate_up, ffn2_w_down=v_ffn2_w_down, ffn2_norm_post=v_ffn2_norm_post)
    kind_name = dict(gu1="ffn1_w_gate_up", dn1="ffn1_w_down", win="w_in", wout="w_out", gu2="ffn2_w_gate_up", dn2="ffn2_w_down")
    delta, new_m, new_v, grad = {}, {}, {}, {}

    def join_and_update(name, items, deps, after):
        ts, send_sems, recv_sems = _join_start(name, [red.bufs[it] for it in items], deps)
        for a, it in enumerate(items):
            k, layer = it if isinstance(it, tuple) else (it, None)
            n = kind_name[k]
            tag = n if layer is None else f"{n}_{layer}"
            g = _join_wait(f"join_wait_{tag}", ts[a], a, send_sems, recv_sems, after)
            prev = (delta[n], new_m[n], new_v[n], grad[n]) if n in delta else None
            delta[n], new_m[n], new_v[n], grad[n] = _adamw(f"adamw_{tag}", weights[n], g, m_in[n], v_in[n], True,
                                                           layer, prev)
            after = delta[n]
        return after

    early = ("wout", "win", "dn2", "gu2") + tuple((k, layer) for layer in range(1, n_layers) for k in ("dn1", "gu1"))
    last_groups = flow["in_flight"]
    done_early = join_and_update("join_early", early, tuple(red.scatter_tokens[g] for g in last_groups), dxs)
    for group in last_groups:
        red.finish(*group, done_early)
    join_and_update("join_late", (("dn1", 0), ("gu1", 0)), (), done_early)

    small_sum = flow["small"].reshape(n_layers, SMALL_ROWS, d)
    g_ffn1_pre, g_ffn1_post, g_mix_pre = small_sum[:, 0], small_sum[:, 1], small_sum[:, 2]
    g_attn_out, g_conv_out = small_sum[:, 3, :qd], small_sum[:, 3, qd:qd + cd]
    g_mix_post, g_ffn2_pre, g_ffn2_post = small_sum[:, 4], small_sum[:, 5], small_sum[:, 6]
    cc = conv_w.shape[2]
    g_conv = lax.dynamic_slice_in_dim(small_sum[:, 7:7 + CONV_WIDTH, :cd], chip * cc, cc, axis=2)

    grad.update(ffn1_norm_pre=g_ffn1_pre, ffn1_norm_post=g_ffn1_post, mix_norm_pre=g_mix_pre, conv_w=g_conv, attn_out_norm=g_attn_out, conv_out_norm=g_conv_out, mix_norm_post=g_mix_post, ffn2_norm_pre=g_ffn2_pre, ffn2_norm_post=g_ffn2_post)
    names = list(weights)

    vectors = [n for n in names if n not in kind_name.values()]

    def pack(tree):
        flat = jnp.concatenate([tree[n].reshape(-1) for n in vectors])
        return jnp.pad(flat, (0, -flat.size % (SUBLANES * LANES))).reshape(-1, LANES)

    packed = _adamw("adamw_small", pack(weights), pack(grad), pack(m_in), pack(v_in))
    offset = 0
    for n in vectors:
        size = weights[n].size
        for tree, flat in zip((delta, new_m, new_v), packed):
            tree[n] = flat.reshape(-1)[offset:offset + size].reshape(weights[n].shape)
        offset += size

    return (loss, grad_x, *[grad[n] for n in names], *[delta[n] for n in names],
            *[new_m[n] for n in names], *[new_v[n] for n in names])
```

```python
import functools

import jax
import jax.numpy as jnp
from jax import lax
from jax.experimental import pallas as pl
from jax.experimental.pallas import tpu as pltpu

F32 = jnp.float32
BF16 = jnp.bfloat16
MESH = pl.DeviceIdType.MESH

NORM_EPS = 1e-6
HEAD_DIM = 128
Q_PER_KV = 4
CONV_WIDTH = 3
FFN_RESIDUAL_WEIGHT = 0.5
DILATED_BRANCHES = ((128, 1), (512, 4), (2048, 16))
ADAM_LR = 0.001
ADAM_B1 = 0.9
ADAM_B2 = 0.999
ADAM_EPS = 1e-08
ADAM_WD = 0.01
ADAM_STEP = 10

N_CHIPS = 4
N_DEV = 8
V7X_VMEM_BYTES = 64 << 20
VMEM_LIMIT = V7X_VMEM_BYTES - (12 << 20)
SUBLANES = 8
LANES = 128
SMALL_ROWS = 16
BIG_BLOCK = 4 << 20


def _params(*sem):
    return pltpu.CompilerParams(dimension_semantics=sem, vmem_limit_bytes=VMEM_LIMIT)


def _row_tile(rows, cols, itemsize=4, budget=2 << 20):
    t = rows
    while t * cols * itemsize > budget and t % 32 == 0:
        t //= 2
    return t


def _sum_to_sublanes(v):
    r, n = v.shape
    return v.reshape(r // SUBLANES, SUBLANES, n).sum(axis=0)


_DIMS = {
    "nn": (((1,), (0,)), ((), ())),
    "nt": (((1,), (1,)), ((), ())),
    "tn": (((0,), (0,)), ((), ())),
}


ANY_SPEC = pl.BlockSpec(memory_space=pl.ANY)


def _dot(a, b, mode):
    return lax.dot_general(a, b, _DIMS[mode], preferred_element_type=F32)


def _mm(name, a, b, *, mode, grid, a_spec, b_spec, o_spec, out_shape, nk=1, acc_shape=None, deps=()):
    nd = len(deps)

    def body(a_ref, b_ref, *rest):
        o_ref, scratch = rest[nd], rest[nd + 1:]
        r = _dot(a_ref[...], b_ref[...], mode)
        if nk == 1:
            o_ref[...] = r.astype(o_ref.dtype)
        else:
            acc = scratch[0]
            k = pl.program_id(len(grid) - 1)

            @pl.when(k == 0)
            def _():
                acc[...] = r

            @pl.when(k > 0)
            def _():
                acc[...] += r

            @pl.when(k == nk - 1)
            def _():
                o_ref[...] = acc[...].astype(o_ref.dtype)

    sem = ("parallel",) * (len(grid) - (1 if nk > 1 else 0)) + (("arbitrary",) if nk > 1 else ())
    return pl.pallas_call(
        body, name=name, grid=grid, in_specs=[a_spec, b_spec] + [ANY_SPEC] * nd, out_specs=o_spec,
        out_shape=out_shape, scratch_shapes=[pltpu.VMEM(acc_shape, F32)] if nk > 1 else [],
        compiler_params=_params(*sem),
    )(a, b, *deps)


def _mm_blocked_k(name, a, b, tm, tn, deps=()):
    m = a.shape[0]
    nb, n, k = b.shape

    def body(a_ref, b_ref, *rest):
        o_ref = rest[-1]
        acc = _dot(a_ref[:, 0:k], b_ref[0], "nt")
        for blk in range(1, nb):
            acc = acc + _dot(a_ref[:, blk * k:(blk + 1) * k], b_ref[blk], "nt")
        o_ref[...] = acc

    return pl.pallas_call(
        body, name=name, grid=(m // tm, n // tn),
        in_specs=[pl.BlockSpec((tm, nb * k), lambda i, j: (i, 0)), pl.BlockSpec((nb, tn, k), lambda i, j: (0, j, 0))]
        + [ANY_SPEC] * len(deps),
        out_specs=pl.BlockSpec((tm, tn), lambda i, j: (i, j)), out_shape=jax.ShapeDtypeStruct((m, n), F32),
        compiler_params=_params("parallel", "parallel"),
    )(a, b, *deps)


def _tile(n, want):
    if n <= want:
        return n
    best = None
    for t in range(LANES, want + 1, LANES):
        if n % t == 0:
            best = t
    assert best is not None, (n, want)
    return best


def _norm_fwd(name, x, gain):
    s, d = x.shape
    tr = _row_tile(s, d, budget=BIG_BLOCK)

    def body(x_ref, g_ref, o_ref):
        xv = x_ref[...]
        r = lax.rsqrt(jnp.mean(xv * xv, axis=-1, keepdims=True) + NORM_EPS)
        o_ref[...] = (xv * r * g_ref[...]).astype(o_ref.dtype)

    return pl.pallas_call(
        body, name=name, grid=(s // tr,),
        in_specs=[pl.BlockSpec((tr, d), lambda i: (i, 0)), pl.BlockSpec((1, d), lambda i: (0, 0))],
        out_specs=pl.BlockSpec((tr, d), lambda i: (i, 0)),
        out_shape=jax.ShapeDtypeStruct((s, d), BF16), compiler_params=_params("parallel"),
    )(x, gain)


def _res_norm(name, x, y, gain, scale, next_gain=None):
    s, d = x.shape
    tr = _row_tile(s, d, budget=BIG_BLOCK)
    with_next = next_gain is not None

    def body(x_ref, y_ref, g_ref, *rest):
        yv = y_ref[...]
        r = lax.rsqrt(jnp.mean(yv * yv, axis=-1, keepdims=True) + NORM_EPS)
        xn = x_ref[...] + scale * (yv * r * g_ref[...])
        if with_next:
            ng_ref, o_ref, h_ref = rest
            rn = lax.rsqrt(jnp.mean(xn * xn, axis=-1, keepdims=True) + NORM_EPS)
            h_ref[...] = (xn * rn * ng_ref[...]).astype(h_ref.dtype)
        else:
            o_ref, = rest
        o_ref[...] = xn

    row = pl.BlockSpec((tr, d), lambda i: (i, 0))
    vec = pl.BlockSpec((1, d), lambda i: (0, 0))
    outs = pl.pallas_call(
        body, name=name, grid=(s // tr,),
        in_specs=[row, row, vec] + ([vec] if with_next else []), out_specs=[row] * (2 if with_next else 1),
        out_shape=[jax.ShapeDtypeStruct((s, d), F32)] + ([jax.ShapeDtypeStruct((s, d), BF16)] if with_next else []),
        compiler_params=_params("parallel"),
    )(x, y, gain, *((next_gain,) if with_next else ()))
    return (outs[0], outs[1]) if with_next else (outs[0], None)


def _rms_bwd(dn, yv, gv):
    r = lax.rsqrt(jnp.mean(yv * yv, axis=-1, keepdims=True) + NORM_EPS)
    xhat = yv * r
    dxn = dn * gv
    return r * (dxn - xhat * jnp.mean(dxn * xhat, axis=-1, keepdims=True)), _sum_to_sublanes(dn * xhat)


def _accumulate(ref, part):
    @pl.when(pl.program_id(0) == 0)
    def _():
        ref[...] = part

    @pl.when(pl.program_id(0) > 0)
    def _():
        ref[...] += part


def _norm_bwd(name, dout, yin, gain, scale, resid, out_dtype, following=None):
    s, d = yin.shape
    tr = _row_tile(s, d)
    has_resid = resid is not None
    chained = following is not None

    def body(*refs):
        refs = list(refs)
        do_ref, y_ref, g_ref = refs[:3]
        del refs[:3]
        r_ref = refs.pop(0) if has_resid else None
        if chained:
            y2_ref, g2_ref = refs[:2]
            del refs[:2]
        di_ref, dg_ref = refs[:2]
        din, part = _rms_bwd(scale * do_ref[...], y_ref[...], g_ref[...])
        _accumulate(dg_ref, part)
        if has_resid:
            din = din + r_ref[...]
        di_ref[...] = din.astype(di_ref.dtype)
        if chained:
            d2_ref, dg2_ref = refs[2:]
            d2, part2 = _rms_bwd(following[2] * din, y2_ref[...], g2_ref[...])
            _accumulate(dg2_ref, part2)
            d2_ref[...] = d2.astype(d2_ref.dtype)

    row = pl.BlockSpec((tr, d), lambda i: (i, 0))
    vec = pl.BlockSpec((1, d), lambda i: (0, 0))
    acc = pl.BlockSpec((SUBLANES, d), lambda i: (0, 0))
    ins = [row, row, vec] + ([row] if has_resid else []) + ([row, vec] if chained else [])
    args = (dout, yin, gain) + ((resid,) if has_resid else ()) + (tuple(following[:2]) if chained else ())
    outs = [row, acc] + ([row, acc] if chained else [])
    shapes = [jax.ShapeDtypeStruct((s, d), out_dtype), jax.ShapeDtypeStruct((SUBLANES, d), F32)]
    if chained:
        shapes += [jax.ShapeDtypeStruct((s, d), BF16), jax.ShapeDtypeStruct((SUBLANES, d), F32)]
    return pl.pallas_call(
        body, name=name, grid=(s // tr,), in_specs=ins, out_specs=outs, out_shape=shapes,
        compiler_params=_params("arbitrary"),
    )(*args)


def _loss_head(name, y, target, following):
    s, d = y.shape
    tr = _row_tile(s, d)
    y2, gain2, scale2 = following

    def body(y_ref, t_ref, y2_ref, g2_ref, dy_ref, l_ref, d2_ref, dg2_ref):
        e = y_ref[...] - t_ref[...]
        dy = e * (1.0 / d)
        dy_ref[...] = dy
        _accumulate(l_ref, _sum_to_sublanes(e * e) * (0.5 / d))
        d2, part2 = _rms_bwd(scale2 * dy, y2_ref[...], g2_ref[...])
        _accumulate(dg2_ref, part2)
        d2_ref[...] = d2.astype(d2_ref.dtype)

    row = pl.BlockSpec((tr, d), lambda i: (i, 0))
    acc = pl.BlockSpec((SUBLANES, d), lambda i: (0, 0))
    return pl.pallas_call(
        body, name=name, grid=(s // tr,), in_specs=[row, row, row, pl.BlockSpec((1, d), lambda i: (0, 0))],
        out_specs=[row, acc, row, acc],
        out_shape=[jax.ShapeDtypeStruct((s, d), F32), jax.ShapeDtypeStruct((SUBLANES, d), F32),
                   jax.ShapeDtypeStruct((s, d), BF16), jax.ShapeDtypeStruct((SUBLANES, d), F32)],
        compiler_params=_params("arbitrary"),
    )(y, target, y2, gain2)


def _ffn_up(name, h, gu_w):
    s, d = h.shape
    nb, _, fs = gu_w.shape
    hb = nb // 2
    w = gu_w.reshape(2, hb, d, fs)
    tm = _tile(s, 512)
    tn = _tile(fs, 1408)
    nj = fs // tn

    def body(h_ref, w_ref, gu_ref, a_ref):
        hv = h_ref[...]
        g = _dot(hv, w_ref[0], "nn")
        u = _dot(hv, w_ref[1], "nn")
        sg = jax.nn.sigmoid(g)
        silu = g * sg
        gu_ref[0] = (u * (sg * (1.0 + g * (1.0 - sg)))).astype(gu_ref.dtype)
        gu_ref[1] = silu.astype(gu_ref.dtype)
        a_ref[...] = (silu * u).astype(a_ref.dtype)

    return pl.pallas_call(
        body, name=name, grid=(hb, nj, s // tm),
        in_specs=[pl.BlockSpec((tm, d), lambda jb, jo, i: (i, 0)),
                  pl.BlockSpec((2, None, d, tn), lambda jb, jo, i: (0, jb, 0, jo))],
        out_specs=[pl.BlockSpec((2, None, tm, tn), lambda jb, jo, i: (0, jb, i, jo)),
                   pl.BlockSpec((tm, tn), lambda jb, jo, i: (i, jb * nj + jo))],
        out_shape=[jax.ShapeDtypeStruct((2, hb, s, fs), BF16), jax.ShapeDtypeStruct((s, hb * fs), BF16)],
        compiler_params=_params("parallel", "parallel", "parallel"),
    )(h, w)


def _ffn_dact(name, dy, dn_w, gu, deps=()):
    s, d = dy.shape
    _, hb, _, fs = gu.shape
    tm = _tile(s, 1024)
    tn = _tile(fs, 1408)
    nj = fs // tn

    def body(dy_ref, w_ref, gu_ref, *rest):
        o_ref = rest[-1]
        wv = w_ref[...]
        parts = max(1, tm // 256)
        for r in range(parts):
            rows = slice(r * (tm // parts), (r + 1) * (tm // parts))
            da = _dot(dy_ref[rows, :], wv, "nt")
            o_ref[0, rows, :] = (da * gu_ref[0, rows, :].astype(F32)).astype(o_ref.dtype)
            o_ref[1, rows, :] = (da * gu_ref[1, rows, :].astype(F32)).astype(o_ref.dtype)

    blk = pl.BlockSpec((2, None, tm, tn), lambda jb, jo, i: (0, jb, i, jo))
    return pl.pallas_call(
        body, name=name, grid=(hb, nj, s // tm),
        in_specs=[pl.BlockSpec((tm, d), lambda jb, jo, i: (i, 0)),
                  pl.BlockSpec((tn, d), lambda jb, jo, i: (jb * nj + jo, 0)),
                  blk] + [ANY_SPEC] * len(deps),
        out_specs=blk, out_shape=jax.ShapeDtypeStruct(gu.shape, BF16),
        compiler_params=_params("parallel", "parallel", "parallel"),
    )(dy, dn_w, gu, *deps)


_MASKED = -1e30


def _attn_bias(s, tq):
    nd = s // tq
    dist = (jnp.arange(nd)[:, None, None] * tq + jnp.arange(tq)[None, :, None]) - jnp.arange(tq)[None, None, :]
    mult = jnp.zeros(dist.shape, F32)
    for window, dilation in DILATED_BRANCHES:
        mult = mult + ((dist >= 0) & (dist <= window) & (dist % dilation == 0)).astype(F32)
    return jnp.where(mult > 0.0, jnp.log(jnp.maximum(mult, 1.0)), _MASKED)


def _biased(sc, bias, scale):
    tq, tk = bias.shape
    return (sc.reshape(-1, tq, tk) * scale + bias[None]).reshape(sc.shape)


def _attn_specs(s, qd, kvd, tq):
    rw = Q_PER_KV * HEAD_DIM
    qspec = pl.BlockSpec((tq, rw), lambda g, i: (i, g))
    kspec = pl.BlockSpec((s, HEAD_DIM), lambda g, i: (0, qd // HEAD_DIM + g))
    vspec = pl.BlockSpec((s, HEAD_DIM), lambda g, i: (0, (qd + kvd) // HEAD_DIM + g))
    return rw, qspec, kspec, vspec


def _attn_fwd(name, z, qd, kvd):
    s = z.shape[0]
    tq = _tile(s, 256)
    nkv = kvd // HEAD_DIM
    rw, qspec, kspec, vspec = _attn_specs(s, qd, kvd, tq)
    scale = HEAD_DIM ** -0.5

    def body(q_ref, k_ref, v_ref, b_ref, o_ref, l_ref):
        i = pl.program_id(1)
        heads = [slice(h * HEAD_DIM, (h + 1) * HEAD_DIM) for h in range(Q_PER_KV)]
        q_all = jnp.concatenate([q_ref[:, cols] for cols in heads], axis=0)

        def chunk(j, carry):
            mx, den, acc = carry
            k0 = pl.multiple_of(j * tq, tq)
            kc, vc = k_ref[pl.ds(k0, tq), :], v_ref[pl.ds(k0, tq), :]
            sc = _biased(_dot(q_all, kc, "nt"), b_ref[i - j], scale)
            mx_new = jnp.maximum(mx, jnp.max(sc, axis=-1, keepdims=True))
            alpha = jnp.exp(mx - mx_new)
            p = jnp.exp(sc - mx_new)
            return (mx_new, alpha * den + jnp.sum(p, axis=-1, keepdims=True),
                    alpha * acc + _dot(p.astype(BF16), vc, "nn"))

        rows = Q_PER_KV * tq
        init = (jnp.full((rows, 1), _MASKED, F32), jnp.zeros((rows, 1), F32), jnp.zeros((rows, HEAD_DIM), F32))
        mx, den, acc = lax.fori_loop(0, i + 1, chunk, init)
        out = acc / den
        lse = mx + jnp.log(den)
        for h, cols in enumerate(heads):
            o_ref[:, cols] = out[h * tq:(h + 1) * tq]
            l_ref[:, cols] = jnp.broadcast_to(lse[h * tq:(h + 1) * tq], (tq, HEAD_DIM))

    bias = _attn_bias(s, tq)
    return pl.pallas_call(
        body, name=name, grid=(nkv, s // tq),
        in_specs=[qspec, kspec, vspec, pl.BlockSpec(bias.shape, lambda g, i: (0, 0, 0))], out_specs=[qspec, qspec],
        out_shape=[jax.ShapeDtypeStruct((s, qd), F32), jax.ShapeDtypeStruct((s, qd), F32)],
        compiler_params=_params("parallel", "parallel"),
    )(z, z, z, bias)


def _attn_bwd(name, z, o, lse, do, qd, kvd):
    s = z.shape[0]
    tq = _tile(s, 512)
    nkv = kvd // HEAD_DIM
    nq = s // tq
    rw, qspec, kspec, vspec = _attn_specs(s, qd, kvd, tq)
    scale = HEAD_DIM ** -0.5

    def body(q_ref, k_ref, v_ref, o_ref, l_ref, do_ref, b_ref, dq_ref, dk_ref, dv_ref, dk_acc, dv_acc):
        i = pl.program_id(1)
        heads = [slice(h * HEAD_DIM, (h + 1) * HEAD_DIM) for h in range(Q_PER_KV)]

        @pl.when(i == 0)
        def _():
            dk_acc[...] = jnp.zeros_like(dk_acc)
            dv_acc[...] = jnp.zeros_like(dv_acc)

        q_all = jnp.concatenate([q_ref[:, cols] for cols in heads], axis=0)
        do_all = jnp.concatenate([do_ref[:, cols].astype(BF16) for cols in heads], axis=0)
        lse_all = jnp.concatenate([l_ref[:, cols][:, :1] for cols in heads], axis=0)
        delta_all = jnp.concatenate(
            [jnp.sum(do_ref[:, cols] * o_ref[:, cols], axis=-1, keepdims=True) for cols in heads], axis=0)

        def chunk(j, dq):
            k0 = pl.multiple_of(j * tq, tq)
            kc, vc = k_ref[pl.ds(k0, tq), :], v_ref[pl.ds(k0, tq), :]
            p = jnp.exp(_biased(_dot(q_all, kc, "nt"), b_ref[i - j], scale) - lse_all)
            ds = (p * (_dot(do_all, vc, "nt") - delta_all) * scale).astype(BF16)
            dk_acc[pl.ds(k0, tq), :] += _dot(ds, q_all, "tn")
            dv_acc[pl.ds(k0, tq), :] += _dot(p.astype(BF16), do_all, "tn")
            return dq + _dot(ds, kc, "nn")

        dq = lax.fori_loop(0, i + 1, chunk, jnp.zeros((Q_PER_KV * tq, HEAD_DIM), F32))
        for h, cols in enumerate(heads):
            dq_ref[:, cols] = dq[h * tq:(h + 1) * tq].astype(dq_ref.dtype)

        @pl.when(i == nq - 1)
        def _():
            dk_ref[...] = dk_acc[...].astype(dk_ref.dtype)
            dv_ref[...] = dv_acc[...].astype(dv_ref.dtype)

    kvout = pl.BlockSpec((s, HEAD_DIM), lambda g, i: (0, g))
    bias = _attn_bias(s, tq)
    return pl.pallas_call(
        body, name=name, grid=(nkv, nq),
        in_specs=[qspec, kspec, vspec, qspec, qspec, qspec, pl.BlockSpec(bias.shape, lambda g, i: (0, 0, 0))],
        out_specs=[qspec, kvout, kvout],
        out_shape=[jax.ShapeDtypeStruct((s, qd), BF16), jax.ShapeDtypeStruct((s, kvd), BF16),
                   jax.ShapeDtypeStruct((s, kvd), BF16)],
        scratch_shapes=[pltpu.VMEM((s, HEAD_DIM), F32), pltpu.VMEM((s, HEAD_DIM), F32)],
        compiler_params=_params("parallel", "arbitrary"),
    )(z, z, z, o, lse, do, bias)


def _shift_down(v, n):
    rolled = pltpu.roll(v, n, 0)
    t = lax.broadcasted_iota(jnp.int32, v.shape, 0)
    return jnp.where(t >= n, rolled, 0.0)


def _shift_up(v, n):
    rows = v.shape[0]
    rolled = pltpu.roll(v, rows - n, 0)
    t = lax.broadcasted_iota(jnp.int32, v.shape, 0)
    return jnp.where(t < rows - n, rolled, 0.0)


def _conv_specs(s, base, cd, tc):
    zs = [pl.BlockSpec((s, tc), functools.partial(lambda j, off: (0, off + j), off=(base + n * cd) // tc))
          for n in range(3)]
    wspec = pl.BlockSpec((SUBLANES, tc), lambda j: (0, j))
    cspec = pl.BlockSpec((s, tc), lambda j: (0, j))
    return zs, wspec, cspec


def _conv_fwd(name, z, conv_w, base, cd):
    s = z.shape[0]
    tc = _tile(cd, 256)
    zs, wspec, cspec = _conv_specs(s, base, cd, tc)

    def body(h_ref, b_ref, c_ref, w_ref, o_ref):
        u = c_ref[...].astype(F32) * h_ref[...].astype(F32)
        y = w_ref[0:1, :] * _shift_down(u, 2) + w_ref[1:2, :] * _shift_down(u, 1) + w_ref[2:3, :] * u
        o_ref[...] = b_ref[...].astype(F32) * y

    return pl.pallas_call(
        body, name=name, grid=(cd // tc,), in_specs=zs + [wspec], out_specs=cspec,
        out_shape=jax.ShapeDtypeStruct((s, cd), F32), compiler_params=_params("parallel"),
    )(z, z, z, conv_w)


def _conv_bwd(name, z, conv_w, dc, base, cd):
    s = z.shape[0]
    tc = _tile(cd, 256)
    zs, wspec, cspec = _conv_specs(s, base, cd, tc)

    def body(h_ref, b_ref, c_ref, w_ref, dc_ref, dh_ref, db_ref, dcg_ref, dw_ref):
        hv, bv, cv = h_ref[...].astype(F32), b_ref[...].astype(F32), c_ref[...].astype(F32)
        u = cv * hv
        u1, u2 = _shift_down(u, 1), _shift_down(u, 2)
        w0, w1, w2 = w_ref[0:1, :], w_ref[1:2, :], w_ref[2:3, :]
        y = w0 * u2 + w1 * u1 + w2 * u
        dcv = dc_ref[...]
        db_ref[...] = (dcv * y).astype(db_ref.dtype)
        dy = dcv * bv
        du = w2 * dy + w1 * _shift_up(dy, 1) + w0 * _shift_up(dy, 2)
        dh_ref[...] = (du * cv).astype(dh_ref.dtype)
        dcg_ref[...] = (du * hv).astype(dcg_ref.dtype)
        g0 = jnp.sum(dy * u2, axis=0, keepdims=True)
        g1 = jnp.sum(dy * u1, axis=0, keepdims=True)
        g2 = jnp.sum(dy * u, axis=0, keepdims=True)
        r = lax.broadcasted_iota(jnp.int32, (SUBLANES, tc), 0)
        dw_ref[...] = jnp.where(r == 0, g0, jnp.where(r == 1, g1, jnp.where(r == 2, g2, 0.0)))

    return pl.pallas_call(
        body, name=name, grid=(cd // tc,), in_specs=zs + [wspec, cspec],
        out_specs=[cspec, cspec, cspec, wspec],
        out_shape=[jax.ShapeDtypeStruct((s, cd), BF16)] * 3 + [jax.ShapeDtypeStruct((SUBLANES, cd), F32)],
        compiler_params=_params("parallel"),
    )(z, z, z, conv_w, dc)


def _cat_norm_fwd(name, a, c, ga, gc):
    s, qd = a.shape
    cd = c.shape[1]
    tr = _row_tile(s, qd + cd)

    def body(a_ref, c_ref, ga_ref, gc_ref, o_ref):
        av, cv = a_ref[...], c_ref[...]
        ra = lax.rsqrt(jnp.mean(av * av, axis=-1, keepdims=True) + NORM_EPS)
        rc = lax.rsqrt(jnp.mean(cv * cv, axis=-1, keepdims=True) + NORM_EPS)
        o_ref[:, :qd] = (av * ra * ga_ref[...]).astype(o_ref.dtype)
        o_ref[:, qd:] = (cv * rc * gc_ref[...]).astype(o_ref.dtype)

    return pl.pallas_call(
        body, name=name, grid=(s // tr,),
        in_specs=[pl.BlockSpec((tr, qd), lambda i: (i, 0)), pl.BlockSpec((tr, cd), lambda i: (i, 0)),
                  pl.BlockSpec((1, qd), lambda i: (0, 0)), pl.BlockSpec((1, cd), lambda i: (0, 0))],
        out_specs=pl.BlockSpec((tr, qd + cd), lambda i: (i, 0)),
        out_shape=jax.ShapeDtypeStruct((s, qd + cd), BF16), compiler_params=_params("parallel"),
    )(a, c, ga, gc)


def _cat_norm_bwd(name, dcat, a, c, ga, gc):
    s, qd = a.shape
    cd = c.shape[1]
    tr = _row_tile(s, qd + cd)

    def one(dn, yv, gv):
        r = lax.rsqrt(jnp.mean(yv * yv, axis=-1, keepdims=True) + NORM_EPS)
        xhat = yv * r
        dxn = dn * gv
        return r * (dxn - xhat * jnp.mean(dxn * xhat, axis=-1, keepdims=True)), _sum_to_sublanes(dn * xhat)

    def body(d_ref, a_ref, c_ref, ga_ref, gc_ref, da_ref, dc_ref, dga_ref, dgc_ref):
        da, pa = one(d_ref[:, :qd], a_ref[...], ga_ref[...])
        dc, pc = one(d_ref[:, qd:], c_ref[...], gc_ref[...])
        da_ref[...] = da
        dc_ref[...] = dc

        @pl.when(pl.program_id(0) == 0)
        def _():
            dga_ref[...] = pa
            dgc_ref[...] = pc

        @pl.when(pl.program_id(0) > 0)
        def _():
            dga_ref[...] += pa
            dgc_ref[...] += pc

    ra = pl.BlockSpec((tr, qd), lambda i: (i, 0))
    rc = pl.BlockSpec((tr, cd), lambda i: (i, 0))
    return pl.pallas_call(
        body, name=name, grid=(s // tr,),
        in_specs=[pl.BlockSpec((tr, qd + cd), lambda i: (i, 0)), ra, rc,
                  pl.BlockSpec((1, qd), lambda i: (0, 0)), pl.BlockSpec((1, cd), lambda i: (0, 0))],
        out_specs=[ra, rc, pl.BlockSpec((SUBLANES, qd), lambda i: (0, 0)),
                   pl.BlockSpec((SUBLANES, cd), lambda i: (0, 0))],
        out_shape=[jax.ShapeDtypeStruct((s, qd), F32), jax.ShapeDtypeStruct((s, cd), F32),
                   jax.ShapeDtypeStruct((SUBLANES, qd), F32), jax.ShapeDtypeStruct((SUBLANES, cd), F32)],
        compiler_params=_params("arbitrary"),
    )(dcat, a, c, ga, gc)


def _adamw(name, w, g, m, v, emit_grad=False, layer=None, prev=None):
    shape = w.shape
    cols = shape[-1]
    rows = g.size // cols
    tr = _row_tile(rows, cols, budget=3 << 19)
    first = 0 if layer is None else layer * (rows // tr)
    bc1 = 1.0 - ADAM_B1 ** ADAM_STEP
    bc2 = 1.0 - ADAM_B2 ** ADAM_STEP
    n_out = 4 if emit_grad else 3

    def body(w_ref, g_ref, m_ref, v_ref, *rest):
        d_ref, nm_ref, nv_ref = rest[-n_out:][:3]
        gv = g_ref[...]
        mv = ADAM_B1 * m_ref[...] + (1.0 - ADAM_B1) * gv
        vv = ADAM_B2 * v_ref[...] + (1.0 - ADAM_B2) * (gv * gv)
        nm_ref[...] = mv
        nv_ref[...] = vv
        d_ref[...] = -ADAM_LR * ((mv / bc1) / (jnp.sqrt(vv / bc2) + ADAM_EPS) + ADAM_WD * w_ref[...])
        if emit_grad:
            rest[-1][...] = gv

    row = pl.BlockSpec((tr, cols), lambda i: (first + i, 0))
    g_row = pl.BlockSpec((tr, cols), lambda i: (i, 0))
    prev = tuple(prev) if prev is not None else ()
    total = w.size // cols
    outs = pl.pallas_call(
        body, name=name, grid=(rows // tr,), in_specs=[row, g_row, row, row] + [ANY_SPEC] * len(prev),
        out_specs=[row] * n_out, out_shape=[jax.ShapeDtypeStruct((total, cols), F32)] * n_out,
        input_output_aliases={4 + i: i for i in range(len(prev))}, compiler_params=_params("parallel"),
    )(w.reshape(total, cols), g.reshape(rows, cols), m.reshape(total, cols), v.reshape(total, cols),
      *(t.reshape(total, cols) for t in prev))
    return tuple(t.reshape(shape) for t in outs)


HBM_SPEC = pl.BlockSpec(memory_space=pltpu.HBM)


def _mesh_place():
    x, y, c = lax.axis_index("x"), lax.axis_index("y"), lax.axis_index("c")
    other_chips = [(1 - x, y), (x, 1 - y), (1 - x, 1 - y)]
    return x, y, c, other_chips


def _cast_into_slot(name, w, layer, chip, deps=()):
    _, r, cols = w.shape
    tr = _row_tile(r, cols, budget=BIG_BLOCK)

    def body(chip_ref, w_ref, *rest):
        o_ref = rest[-1]
        o_ref[...] = w_ref[...].astype(o_ref.dtype)

    return pl.pallas_call(
        body, name=name,
        grid_spec=pltpu.PrefetchScalarGridSpec(
            num_scalar_prefetch=1, grid=(r // tr,),
            in_specs=[pl.BlockSpec((None, tr, cols), lambda i, chip_ref: (layer, i, 0))] + [ANY_SPEC] * len(deps),
            out_specs=pl.BlockSpec((None, tr, cols), lambda i, chip_ref: (chip_ref[0], i, 0))),
        out_shape=jax.ShapeDtypeStruct((N_CHIPS, r, cols), BF16), compiler_params=_params("parallel"),
    )(chip, w, *deps)


SEM_SPEC = pl.BlockSpec(memory_space=pltpu.SEMAPHORE)
SPLIT_COPY = pltpu.CompilerParams(has_side_effects=pltpu.SideEffectType.DATAFLOW_SIDE_EFFECTING)
N_OTHER = N_CHIPS - 1
TOKEN_SPEC = pl.BlockSpec(memory_space=pltpu.VMEM)
TOKEN_SHAPE = jax.ShapeDtypeStruct((SUBLANES, LANES), F32)


def _in_hbm(arr):
    return pltpu.with_memory_space_constraint(arr, pltpu.HBM)


def _half_rows(ref, chip_idx, core):
    r2 = ref.shape[1] // 2
    return ref.at[chip_idx, pl.ds(core * r2, r2), :]


def _gather_start(name, fulls, after):
    na = len(fulls)

    def body(*refs):
        f_refs = refs[na + 1:2 * na + 1]
        send_sems, recv_sems = refs[2 * na + 1:3 * na + 1], refs[3 * na + 1:4 * na + 1]
        token = refs[4 * na + 1]
        x, y, c, chips = _mesh_place()
        for a in range(na):
            mine = _half_rows(f_refs[a], 2 * x + y, c)
            for j, (cx, cy) in enumerate(chips):
                pltpu.make_async_remote_copy(
                    src_ref=mine, dst_ref=mine, send_sem=send_sems[a].at[j], recv_sem=recv_sems[a].at[j],
                    device_id=(cx, cy, c), device_id_type=MESH).start()
        token[...] = jnp.zeros_like(token)

    outs = pl.pallas_call(
        body, name=name, in_specs=[HBM_SPEC] * na + [ANY_SPEC],
        out_specs=[HBM_SPEC] * na + [SEM_SPEC] * (2 * na) + [TOKEN_SPEC],
        out_shape=[pltpu.HBM(f.shape, f.dtype) for f in fulls] + [pltpu.SemaphoreType.DMA((N_OTHER,))] * (2 * na)
        + [TOKEN_SHAPE],
        input_output_aliases={a: a for a in range(na)}, compiler_params=SPLIT_COPY,
    )(*[_in_hbm(f) for f in fulls], after)
    return list(outs[:na]), list(outs[na:2 * na]), list(outs[2 * na:3 * na]), outs[3 * na]


def _gather_pass_on(name, full, recv_sems, after):
    def body(f_in, recv_sems, after_ref, f_ref, d2d_send, d2d_recv):
        x, y, c, chips = _mesh_place()
        for j, (cx, cy) in enumerate(chips):
            blk = _half_rows(f_ref, 2 * cx + cy, c)
            pltpu.make_async_remote_copy(
                src_ref=blk, dst_ref=blk, send_sem=d2d_send.at[j], recv_sem=recv_sems.at[j],
                device_id=(cx, cy, c), device_id_type=MESH).wait_recv()
            pltpu.make_async_remote_copy(
                src_ref=blk, dst_ref=blk, send_sem=d2d_send.at[j], recv_sem=d2d_recv.at[j],
                device_id=(x, y, 1 - c), device_id_type=MESH).start()

    return pl.pallas_call(
        body, name=name, in_specs=[HBM_SPEC, SEM_SPEC, ANY_SPEC], out_specs=[HBM_SPEC, SEM_SPEC, SEM_SPEC],
        out_shape=[pltpu.HBM(full.shape, full.dtype)] + [pltpu.SemaphoreType.DMA((N_OTHER,))] * 2,
        input_output_aliases={0: 0}, compiler_params=SPLIT_COPY,
    )(full, recv_sems, after)


def _gather_arrive(name, full, ici_send, d2d_send, d2d_recv, after):
    def body(f_in, ici_send, d2d_send, d2d_recv, after_ref, f_ref):
        x, y, c, chips = _mesh_place()
        for j, (cx, cy) in enumerate(chips):
            mine = _half_rows(f_ref, 2 * x + y, c)
            passed = _half_rows(f_ref, 2 * cx + cy, c)
            theirs = _half_rows(f_ref, 2 * cx + cy, 1 - c)
            pltpu.make_async_remote_copy(
                src_ref=mine, dst_ref=mine, send_sem=ici_send.at[j], recv_sem=d2d_recv.at[j],
                device_id=(cx, cy, c), device_id_type=MESH).wait_send()
            pltpu.make_async_remote_copy(
                src_ref=passed, dst_ref=passed, send_sem=d2d_send.at[j], recv_sem=d2d_recv.at[j],
                device_id=(x, y, 1 - c), device_id_type=MESH).wait_send()
            pltpu.make_async_remote_copy(
                src_ref=theirs, dst_ref=theirs, send_sem=d2d_send.at[j], recv_sem=d2d_recv.at[j],
                device_id=(x, y, 1 - c), device_id_type=MESH).wait_recv()

    return pl.pallas_call(
        body, name=name, in_specs=[HBM_SPEC, SEM_SPEC, SEM_SPEC, SEM_SPEC, ANY_SPEC], out_specs=HBM_SPEC,
        out_shape=pltpu.HBM(full.shape, full.dtype), input_output_aliases={0: 0}, compiler_params=SPLIT_COPY,
    )(full, ici_send, d2d_send, d2d_recv, after)


def _gather_taps(conv_w):
    def body(cw_ref, cwf_ref, send_sems, recv_sems, local_sem):
        x, y, c, chips = _mesh_place()
        k_me = 2 * x + y
        local = pltpu.make_async_copy(cw_ref, cwf_ref.at[k_me], local_sem)
        local.start()
        copies = [pltpu.make_async_remote_copy(
            src_ref=cw_ref, dst_ref=cwf_ref.at[k_me], send_sem=send_sems.at[j], recv_sem=recv_sems.at[j],
            device_id=(cx, cy, c), device_id_type=MESH) for j, (cx, cy) in enumerate(chips)]
        for cp in copies:
            cp.start()
        for j, (cx, cy) in enumerate(chips):
            pltpu.make_async_remote_copy(
                src_ref=cw_ref, dst_ref=cwf_ref.at[2 * cx + cy], send_sem=send_sems.at[j], recv_sem=recv_sems.at[j],
                device_id=(cx, cy, c), device_id_type=MESH).wait_recv()
        for cp in copies:
            cp.wait_send()
        local.wait()

    return pl.pallas_call(
        body, name="gather_taps", in_specs=[HBM_SPEC], out_specs=HBM_SPEC,
        out_shape=jax.ShapeDtypeStruct((N_CHIPS,) + conv_w.shape, conv_w.dtype),
        scratch_shapes=[pltpu.SemaphoreType.DMA((N_OTHER,))] * 2 + [pltpu.SemaphoreType.DMA],
    )(conv_w)


def _sibling_half(g_ref, c):
    r2 = g_ref.shape[1] // 2
    return g_ref.at[:, pl.ds((1 - c) * r2, r2), :]


def _swap_copy(g_ref, land_ref, send_sems, recv_sems, a):
    x, y, c, _ = _mesh_place()
    return pltpu.make_async_remote_copy(
        src_ref=_sibling_half(g_ref, c), dst_ref=land_ref, send_sem=send_sems.at[a], recv_sem=recv_sems.at[a],
        device_id=(x, y, 1 - c), device_id_type=MESH)


def _swap_start(name, gs):
    n = len(gs)

    def body(*refs):
        g_refs, land_refs = refs[n:2 * n], refs[2 * n:3 * n]
        send_sems, recv_sems, token = refs[3 * n:]
        for a in range(n):
            _swap_copy(g_refs[a], land_refs[a], send_sems, recv_sems, a).start()
        token[...] = jnp.zeros_like(token)

    outs = pl.pallas_call(
        body, name=name, in_specs=[HBM_SPEC] * n,
        out_specs=[HBM_SPEC] * (2 * n) + [SEM_SPEC, SEM_SPEC, TOKEN_SPEC],
        out_shape=[pltpu.HBM(g.shape, g.dtype) for g in gs]
        + [pltpu.HBM((g.shape[0], g.shape[1] // 2, g.shape[2]), g.dtype) for g in gs]
        + [pltpu.SemaphoreType.DMA((n,)), pltpu.SemaphoreType.DMA((n,)), TOKEN_SHAPE],
        input_output_aliases={a: a for a in range(n)}, compiler_params=SPLIT_COPY,
    )(*[_in_hbm(g) for g in gs])
    return list(outs[:n]), list(outs[n:2 * n]), outs[2 * n], outs[2 * n + 1], outs[2 * n + 2]


def _swap_wait(name, gs, lands, send_sems, recv_sems, after):
    n = len(gs)

    def body(*refs):
        send_sems, recv_sems = refs[2 * n], refs[2 * n + 1]
        g_refs, land_refs = refs[2 * n + 3:3 * n + 3], refs[3 * n + 3:]
        for a in range(n):
            copy = _swap_copy(g_refs[a], land_refs[a], send_sems, recv_sems, a)
            copy.wait_send()
            copy.wait_recv()

    outs = pl.pallas_call(
        body, name=name, in_specs=[HBM_SPEC] * (2 * n) + [SEM_SPEC, SEM_SPEC, ANY_SPEC],
        out_specs=[HBM_SPEC] * (2 * n),
        out_shape=[pltpu.HBM(t.shape, t.dtype) for t in list(gs) + list(lands)],
        input_output_aliases={a: a for a in range(2 * n)}, compiler_params=SPLIT_COPY,
    )(*gs, *lands, send_sems, recv_sems, after)
    return list(outs[:n]), list(outs[n:])


def _add_core_halves(name, g, sib, core):
    nb, r, cols = g.shape
    r2 = r // 2
    tr = _row_tile(r2, cols, itemsize=2, budget=BIG_BLOCK)
    nrt = r2 // tr

    def body(core_ref, g_ref, s_ref, o_ref):
        o_ref[...] = (g_ref[...].astype(F32) + s_ref[...].astype(F32)).astype(o_ref.dtype)

    return pl.pallas_call(
        body, name=name,
        grid_spec=pltpu.PrefetchScalarGridSpec(
            num_scalar_prefetch=1, grid=(nb, nrt),
            in_specs=[pl.BlockSpec((None, tr, cols), lambda k, i, core_ref: (k, core_ref[0] * nrt + i, 0)),
                      pl.BlockSpec((None, tr, cols), lambda k, i, core_ref: (k, i, 0))],
            out_specs=pl.BlockSpec((None, tr, cols), lambda k, i, core_ref: (k, i, 0))),
        out_shape=jax.ShapeDtypeStruct((nb, r2, cols), BF16), compiler_params=_params("parallel", "parallel"),
    )(core, g, sib)


def _scatter_copies(h_refs, land_refs, send_sems, recv_sems):
    x, y, c, chips = _mesh_place()
    return [pltpu.make_async_remote_copy(
        src_ref=h_ref.at[2 * cx + cy], dst_ref=land_ref.at[j],
        send_sem=send_sems.at[a * N_OTHER + j], recv_sem=recv_sems.at[a * N_OTHER + j],
        device_id=(cx, cy, c), device_id_type=MESH)
        for a, (h_ref, land_ref) in enumerate(zip(h_refs, land_refs)) for j, (cx, cy) in enumerate(chips)]


def _scatter_start(name, hs):
    n = len(hs)

    def body(*refs):
        h_refs, land_refs = refs[n:2 * n], refs[2 * n:3 * n]
        send_sems, recv_sems, token = refs[3 * n:]
        for copy in _scatter_copies(h_refs, land_refs, send_sems, recv_sems):
            copy.start()
        token[...] = jnp.zeros_like(token)

    outs = pl.pallas_call(
        body, name=name, in_specs=[HBM_SPEC] * n,
        out_specs=[HBM_SPEC] * (2 * n) + [SEM_SPEC, SEM_SPEC, TOKEN_SPEC],
        out_shape=[pltpu.HBM(h.shape, h.dtype) for h in hs]
        + [pltpu.HBM((N_OTHER,) + h.shape[1:], h.dtype) for h in hs]
        + [pltpu.SemaphoreType.DMA((n * N_OTHER,)), pltpu.SemaphoreType.DMA((n * N_OTHER,)), TOKEN_SHAPE],
        input_output_aliases={a: a for a in range(n)}, compiler_params=SPLIT_COPY,
    )(*[_in_hbm(h) for h in hs])
    return list(outs[:n]), list(outs[n:2 * n]), outs[2 * n], outs[2 * n + 1], outs[2 * n + 2]


def _scatter_wait(name, hs, lands, send_sems, recv_sems, after):
    afters = tuple(after) if isinstance(after, (tuple, list)) else (after,)
    n = len(hs)

    def body(*refs):
        send_sems, recv_sems = refs[2 * n], refs[2 * n + 1]
        h_refs, land_refs = refs[-2 * n:-n], refs[-n:]
        for copy in _scatter_copies(h_refs, land_refs, send_sems, recv_sems):
            copy.wait_send()
            copy.wait_recv()

    outs = pl.pallas_call(
        body, name=name, in_specs=[HBM_SPEC] * (2 * n) + [SEM_SPEC, SEM_SPEC] + [ANY_SPEC] * len(afters),
        out_specs=[HBM_SPEC] * (2 * n),
        out_shape=[pltpu.HBM(t.shape, t.dtype) for t in list(hs) + list(lands)],
        input_output_aliases={a: a for a in range(2 * n)}, compiler_params=SPLIT_COPY,
    )(*hs, *lands, send_sems, recv_sems, *afters)
    return list(outs[:n]), list(outs[n:])


def _sum_chips(name, hs, rcv, core, chip, layer, n_layers, prev):
    _, r2, cols = hs.shape
    tr = _row_tile(r2, cols, budget=BIG_BLOCK)
    nrt = r2 // tr

    def body(core_ref, chip_ref, h_ref, r_ref, *rest):
        o_ref = rest[-1]
        acc = h_ref[...].astype(F32)
        for j in range(N_CHIPS - 1):
            acc = acc + r_ref[j].astype(F32)
        o_ref[...] = acc

    in_specs = [pl.BlockSpec((None, tr, cols), lambda i, core_ref, chip_ref: (chip_ref[0], i, 0)),
                pl.BlockSpec((N_CHIPS - 1, tr, cols), lambda i, core_ref, chip_ref: (0, i, 0))]
    args = [core, chip, hs, rcv]
    aliases = {}
    if prev is not None:
        in_specs.append(pl.BlockSpec(memory_space=pl.ANY))
        args.append(prev)
        aliases = {4: 0}
    return pl.pallas_call(
        body, name=name,
        grid_spec=pltpu.PrefetchScalarGridSpec(
            num_scalar_prefetch=2, grid=(nrt,), in_specs=in_specs,
            out_specs=pl.BlockSpec((None, tr, cols), lambda i, core_ref, chip_ref: (layer, core_ref[0] * nrt + i, 0))),
        out_shape=jax.ShapeDtypeStruct((n_layers, 2 * r2, cols), F32), input_output_aliases=aliases,
        compiler_params=_params("parallel"),
    )(*args)


def _join_copy(t_ref, send_sems, recv_sems, a):
    x, y, c, _ = _mesh_place()
    r2 = t_ref.shape[1] // 2
    mine = t_ref.at[:, pl.ds(c * r2, r2), :]
    return pltpu.make_async_remote_copy(
        src_ref=mine, dst_ref=mine, send_sem=send_sems.at[a], recv_sem=recv_sems.at[a],
        device_id=(x, y, 1 - c), device_id_type=MESH)


def _join_start(name, ts, deps=()):
    n, nd = len(ts), len(deps)

    def body(*refs):
        t_refs = refs[n + nd:2 * n + nd]
        send_sems, recv_sems = refs[2 * n + nd:]
        for a in range(n):
            _join_copy(t_refs[a], send_sems, recv_sems, a).start()

    outs = pl.pallas_call(
        body, name=name, in_specs=[HBM_SPEC] * n + [ANY_SPEC] * nd, out_specs=[HBM_SPEC] * n + [SEM_SPEC, SEM_SPEC],
        out_shape=[pltpu.HBM(t.shape, t.dtype) for t in ts] + [pltpu.SemaphoreType.DMA((n,))] * 2,
        input_output_aliases={a: a for a in range(n)}, compiler_params=SPLIT_COPY,
    )(*[_in_hbm(t) for t in ts], *deps)
    return list(outs[:n]), outs[n], outs[n + 1]


def _join_wait(name, t, a, send_sems, recv_sems, after):
    def body(t_in, send_sems, recv_sems, after_ref, t_ref):
        copy = _join_copy(t_ref, send_sems, recv_sems, a)
        copy.wait_send()
        copy.wait_recv()

    return pl.pallas_call(
        body, name=name, in_specs=[HBM_SPEC, SEM_SPEC, SEM_SPEC, ANY_SPEC], out_specs=HBM_SPEC,
        out_shape=pltpu.HBM(t.shape, t.dtype), input_output_aliases={0: 0}, compiler_params=SPLIT_COPY,
    )(t, send_sems, recv_sems, after)


def _allreduce_small(p):
    n, _, w = p.shape

    def body(p_ref, o_ref, buf, send_sems, recv_sems):
        x, y, c, _ = _mesh_place()
        me = 4 * x + 2 * y + c
        buf[me] = jnp.sum(p_ref[...], axis=1)
        copies = []
        for pat in range(1, N_DEV):
            fx, fy, fc = (pat >> 2) & 1, (pat >> 1) & 1, pat & 1
            copies.append(pltpu.make_async_remote_copy(
                src_ref=buf.at[me], dst_ref=buf.at[me], send_sem=send_sems.at[pat - 1], recv_sem=recv_sems.at[pat - 1],
                device_id=(x ^ fx, y ^ fy, c ^ fc), device_id_type=MESH))
        for cp in copies:
            cp.start()
        for cp in copies:
            cp.wait()
        acc = buf[0]
        for dev in range(1, N_DEV):
            acc = acc + buf[dev]
        o_ref[...] = acc

    return pl.pallas_call(
        body, name="allreduce_small", in_specs=[pl.BlockSpec(memory_space=pltpu.VMEM)],
        out_specs=pl.BlockSpec(memory_space=pltpu.VMEM), out_shape=jax.ShapeDtypeStruct((n, w), F32),
        scratch_shapes=[pltpu.VMEM((N_DEV, n, w), F32), pltpu.SemaphoreType.DMA((N_DEV - 1,)),
                        pltpu.SemaphoreType.DMA((N_DEV - 1,))],
    )(p)


class _WeightFeed:
    def __init__(self):
        self.fulls, self.ici_send, self.ici_recv, self.d2d = [], [], [], []

    def start(self, name, fulls, after):
        started, send, recv, token = _gather_start(name, fulls, after)
        self.fulls += started
        self.ici_send += send
        self.ici_recv += recv
        self.d2d += [None] * len(fulls)
        self.token = token
        return token

    def _pass_on(self, k, after):
        if k == 0:
            after = self.token
        if k < len(self.fulls) and self.d2d[k] is None:
            self.fulls[k], send, recv = _gather_pass_on(f"gather_pass_{k}", self.fulls[k], self.ici_recv[k], after)
            self.d2d[k] = (send, recv)

    def take(self, k, after):
        self._pass_on(k, after)
        self.fulls[k] = _gather_arrive(f"gather_arrive_{k}", self.fulls[k], self.ici_send[k], *self.d2d[k], after)
        return self.fulls[k]


def _ffn_forward(tag, x, h, g_post, next_gain, feed, k):
    s, d = x.shape
    gu_w = feed.take(k, h)
    gu, a = _ffn_up(f"{tag}_up", h, gu_w)
    dn_w = feed.take(k + 1, a).reshape(-1, d)
    f = dn_w.shape[0]
    tm, tn = _tile(s, 1024), _tile(d, 512)
    y = _mm(f"{tag}_down", a, dn_w, mode="nn", grid=(s // tm, d // tn),
            a_spec=pl.BlockSpec((tm, f), lambda i, j: (i, 0)),
            b_spec=pl.BlockSpec((f, tn), lambda i, j: (0, j)),
            o_spec=pl.BlockSpec((tm, tn), lambda i, j: (i, j)),
            out_shape=jax.ShapeDtypeStruct((s, d), F32))
    x_new, h_next = _res_norm(f"{tag}_post", x, y, g_post, FFN_RESIDUAL_WEIGHT, next_gain)
    return x_new, h_next, (x, h, gu, a, y)


class _GradReduce:
    def __init__(self, core, chip, n_layers, per_layer=()):
        self.core, self.chip, self.n_layers, self.per_layer = core, chip, n_layers, per_layer
        self.state = {}
        self.bufs = {}
        self.scatter_tokens = {}

    def start(self, kinds, layer, gs):
        gs, lands, send, recv, token = _swap_start(f"swap_start_{kinds[0]}_{layer}", gs)
        self.state[kinds, layer] = (gs, lands, send, recv)
        return token

    def exchange(self, kinds, layer, after):
        tag = f"{kinds[0]}_{layer}"
        gs, sibs = _swap_wait(f"swap_wait_{tag}", *self.state[kinds, layer], after)
        hs = [_add_core_halves(f"add_cores_{k}_{layer}", g, sib, self.core) for k, g, sib in zip(kinds, gs, sibs)]
        hs, lands, send, recv, token = _scatter_start(f"scatter_start_{tag}", hs)
        self.state[kinds, layer] = (hs, lands, send, recv)
        self.scatter_tokens[kinds, layer] = token
        return token

    def finish(self, kinds, layer, after):
        tag = f"{kinds[0]}_{layer}"
        hs, rcvs = _scatter_wait(f"scatter_wait_{tag}", *self.state.pop((kinds, layer)), after)
        for k, h, rcv in zip(kinds, hs, rcvs):
            if k in self.per_layer:
                last = self.bufs[k, layer] = _sum_chips(f"sum_chips_{k}_{layer}", h, rcv, self.core, self.chip,
                                                        0, 1, None)
            else:
                last = self.bufs[k] = _sum_chips(f"sum_chips_{k}_{layer}", h, rcv, self.core, self.chip, layer,
                                                 self.n_layers, self.bufs.get(k))
        return last


def _ffn_backward(tag, dx_new, saved, g_pre, g_post, gu_w, dn_w, red, kinds, layer, deps, head, following,
                  last=None):
    x, h, gu, a, y = saved
    s, d = x.shape
    nb, fs = gu_w.shape[0], gu_w.shape[2]
    f = dn_w.shape[0]
    fr = f // nb
    dy, dg_post = head or _norm_bwd(f"{tag}_post_bwd", dx_new, y, g_post, FFN_RESIDUAL_WEIGHT, None, BF16)
    dgu = _ffn_dact(f"{tag}_dact", dy, dn_w, gu, deps)
    dgu4 = dgu.reshape(nb, s, fs)
    tm, tw = _tile(d, 1024), _tile(fs, 1408)
    nw = fs // tw
    tn = _tile(d, 1024)
    ts, td = _tile(s, 1024), _tile(d, 1024)

    def gate_up_gradient(deps):
        return _mm(f"{tag}_dwgu", h, dgu4, mode="tn", grid=(nb, nw, d // tm),
                   a_spec=pl.BlockSpec((s, tm), lambda k, j, i: (0, i)),
                   b_spec=pl.BlockSpec((None, s, tw), lambda k, j, i: (k, 0, j)),
                   o_spec=pl.BlockSpec((None, tm, tw), lambda k, j, i: (k, i, j)),
                   out_shape=jax.ShapeDtypeStruct((nb, d, fs), BF16), deps=deps)

    def down_gradient(deps):
        return _mm(f"{tag}_dwd", a, dy, mode="tn", grid=(nb, d // tn),
                   a_spec=pl.BlockSpec((s, fr), lambda i, j: (0, i)),
                   b_spec=pl.BlockSpec((s, tn), lambda i, j: (0, j)),
                   o_spec=pl.BlockSpec((None, fr, tn), lambda i, j: (i, 0, j)),
                   out_shape=jax.ShapeDtypeStruct((nb, fr, d), BF16), deps=deps)

    def input_gradient(deps):
        dh = _mm(f"{tag}_dh", dgu4, gu_w, mode="nt", grid=(s // ts, d // td, nb),
                 a_spec=pl.BlockSpec((None, ts, fs), lambda i, j, k: (k, i, 0)),
                 b_spec=pl.BlockSpec((None, td, fs), lambda i, j, k: (k, j, 0)),
                 o_spec=pl.BlockSpec((ts, td), lambda i, j, k: (i, j)),
                 out_shape=jax.ShapeDtypeStruct((s, d), F32), nk=nb, acc_shape=(ts, td), deps=deps)
        return _norm_bwd(f"{tag}_pre_bwd", dh, x, g_pre, 1.0, dx_new, F32, following)

    if last is None:
        first = red.start(kinds[:1], layer, [gate_up_gradient(())])
        second = red.start(kinds[1:], layer, [down_gradient((first,))])
        dx, dg_pre, *next_head = input_gradient((red.exchange(kinds[:1], layer, second), second))
    else:
        dx, dg_pre, *next_head = input_gradient(())
        first = red.start(kinds[:1], layer, [gate_up_gradient((last(dg_pre, dg_post),))])
        second = red.start(kinds[1:], layer, [down_gradient((first,))])
        red.exchange(kinds[:1], layer, second)
    return dx, dg_pre, dg_post, tuple(next_head) or None


def _mixer_forward(tag, x, h, gains, next_gain, feed, k, conv_taps, dims):
    qd, kvd, cd = dims
    s, d = x.shape
    _, g_a, g_c, g_post = gains
    win_w = feed.take(k, h)
    nb, cw = win_w.shape[0], win_w.shape[2]
    tm = _tile(s, 1024)
    z = _mm(f"{tag}_in", h, win_w, mode="nn", grid=(nb, s // tm),
            a_spec=pl.BlockSpec((tm, d), lambda j, i: (i, 0)),
            b_spec=pl.BlockSpec((None, d, cw), lambda j, i: (j, 0, 0)),
            o_spec=pl.BlockSpec((tm, cw), lambda j, i: (i, j)),
            out_shape=jax.ShapeDtypeStruct((s, nb * cw), BF16))
    a, lse = _attn_fwd(f"{tag}_attn", z, qd, kvd)
    c = _conv_fwd(f"{tag}_conv", z, conv_taps, qd + 2 * kvd, cd)
    cat = _cat_norm_fwd(f"{tag}_cat", a, c, g_a, g_c)
    wout_w = feed.take(k + 1, cat).reshape(-1, d)
    mw = qd + cd
    tn = _tile(d, 1024)
    mixed = _mm(f"{tag}_out", cat, wout_w, mode="nn", grid=(s // tm, d // tn),
                a_spec=pl.BlockSpec((tm, mw), lambda i, j: (i, 0)),
                b_spec=pl.BlockSpec((mw, tn), lambda i, j: (0, j)),
                o_spec=pl.BlockSpec((tm, tn), lambda i, j: (i, j)),
                out_shape=jax.ShapeDtypeStruct((s, d), F32))
    x_new, h_next = _res_norm(f"{tag}_post", x, mixed, g_post, 1.0, next_gain)
    return x_new, h_next, (x, h, z, a, lse, c, cat, mixed)


def _mixer_backward(tag, dx_new, saved, gains, win_w, conv_taps, wout_w, dims, red, kinds, layer, deps, head,
                    following):
    qd, kvd, cd = dims
    x, h, z, a, lse, c, cat, mixed = saved
    s, d = x.shape
    nb, cw = win_w.shape[0], win_w.shape[2]
    g_pre, g_a, g_c, g_post = gains
    mw = qd + cd
    dmixed, dg_post = head or _norm_bwd(f"{tag}_post_bwd", dx_new, mixed, g_post, 1.0, None, BF16)
    tm, tn = _tile(s, 1024), _tile(mw, 1024)
    dcat = _mm(f"{tag}_dcat", dmixed, wout_w, mode="nt", grid=(s // tm, mw // tn),
               a_spec=pl.BlockSpec((tm, d), lambda i, j: (i, 0)),
               b_spec=pl.BlockSpec((tn, d), lambda i, j: (j, 0)),
               o_spec=pl.BlockSpec((tm, tn), lambda i, j: (i, j)),
               out_shape=jax.ShapeDtypeStruct((s, mw), F32), deps=deps)
    wr = mw // nb
    td = _tile(d, 1024)
    d_wout = _mm(f"{tag}_dwout", cat, dmixed, mode="tn", grid=(nb, d // td),
                 a_spec=pl.BlockSpec((s, wr), lambda i, j: (0, i)),
                 b_spec=pl.BlockSpec((s, td), lambda i, j: (0, j)),
                 o_spec=pl.BlockSpec((None, wr, td), lambda i, j: (i, 0, j)),
                 out_shape=jax.ShapeDtypeStruct((nb, wr, d), BF16))
    da, dc, dg_a, dg_c = _cat_norm_bwd(f"{tag}_cat_bwd", dcat, a, c, g_a, g_c)
    dhc, dbg, dcg, d_taps = _conv_bwd(f"{tag}_conv_bwd", z, conv_taps, dc, qd + 2 * kvd, cd)
    dq, dk, dv = _attn_bwd(f"{tag}_attn_bwd", z, a, lse, da, qd, kvd)
    dz = jnp.concatenate([dq, dk, dv, dhc, dbg, dcg], axis=1)
    th = _tile(d, 1024)
    d_win = _mm(f"{tag}_dwin", h, dz, mode="tn", grid=(nb, d // th),
                a_spec=pl.BlockSpec((s, th), lambda k, i: (0, i)),
                b_spec=pl.BlockSpec((s, cw), lambda k, i: (0, k)),
                o_spec=pl.BlockSpec((None, th, cw), lambda k, i: (k, i, 0)),
                out_shape=jax.ShapeDtypeStruct((nb, d, cw), BF16))
    started = (red.start(kinds, layer, [d_win, d_wout]),)
    dh = _mm_blocked_k(f"{tag}_dh", dz, win_w, tm, _tile(d, 512), started)
    dx, dg_pre, *next_head = _norm_bwd(f"{tag}_pre_bwd", dh, x, g_pre, 1.0, dx_new, F32, following)
    return dx, d_taps, (dg_pre, dg_a, dg_c, dg_post), tuple(next_head) or None


def kernel(x, ffn1_norm_pre, ffn1_w_gate_up, ffn1_w_down, ffn1_norm_post, mix_norm_pre, w_in, conv_w, attn_out_norm, conv_out_norm, w_out, mix_norm_post, ffn2_norm_pre, ffn2_w_gate_up, ffn2_w_down, ffn2_norm_post, loss_target, m_ffn1_norm_pre, m_ffn1_w_gate_up, m_ffn1_w_down, m_ffn1_norm_post, m_mix_norm_pre, m_w_in, m_conv_w, m_attn_out_norm, m_conv_out_norm, m_w_out, m_mix_norm_post, m_ffn2_norm_pre, m_ffn2_w_gate_up, m_ffn2_w_down, m_ffn2_norm_post, v_ffn1_norm_pre, v_ffn1_w_gate_up, v_ffn1_w_down, v_ffn1_norm_post, v_mix_norm_pre, v_w_in, v_conv_w, v_attn_out_norm, v_conv_out_norm, v_w_out, v_mix_norm_post, v_ffn2_norm_pre, v_ffn2_w_gate_up, v_ffn2_w_down, v_ffn2_norm_post):
    _, s, d = x.shape
    n_layers = ffn1_norm_pre.shape[0]
    qd = attn_out_norm.shape[1]
    cd = conv_out_norm.shape[1]
    kvd = qd // Q_PER_KV
    dims = (qd, kvd, cd)
    assert N_CHIPS * w_in.shape[2] == qd + 2 * kvd + 3 * cd and qd + cd == N_CHIPS * w_out.shape[1]
    assert 2 * d <= SMALL_ROWS * LANES * SUBLANES
    chip = 2 * lax.axis_index("x") + lax.axis_index("y")
    chip_arr = chip.astype(jnp.int32).reshape(1)
    core = lax.axis_index("c").astype(jnp.int32).reshape(1)
    kinds = ("gu1", "dn1", "win", "wout", "gu2", "dn2")

    big = (ffn1_w_gate_up, ffn1_w_down, w_in, w_out, ffn2_w_gate_up, ffn2_w_down)
    nk = len(kinds)
    taps_all = _gather_taps(conv_w)
    feed = _WeightFeed()
    order = [(k, w, layer) for layer in range(n_layers) for k, w in zip(kinds, big)]
    k, w, layer = order[0]
    token = feed.start("gather_start_first", [_cast_into_slot(f"cast_{k}_{layer}", w, layer, chip_arr)], taps_all)
    feed.start("gather_start_rest", [_cast_into_slot(f"cast_{k}_{layer}", w, layer, chip_arr, (token,))
                                     for k, w, layer in order[1:]], token)
    taps = jnp.transpose(taps_all, (1, 2, 0, 3)).reshape(n_layers, CONV_WIDTH, cd)
    taps = jnp.pad(taps, ((0, 0), (0, SUBLANES - CONV_WIDTH), (0, 0)))

    def gain(g, layer):
        return g[layer][None, :]

    xs = x[0]
    hs = _norm_fwd("l0_ffn1_norm", xs, gain(ffn1_norm_pre, 0))
    saved = []
    for layer in range(n_layers):
        t = f"l{layer}"
        k0 = layer * nk
        xs, hs, s1 = _ffn_forward(f"{t}_ffn1", xs, hs, gain(ffn1_norm_post, layer), gain(mix_norm_pre, layer), feed, k0)
        mix_gains = (gain(mix_norm_pre, layer), gain(attn_out_norm, layer), gain(conv_out_norm, layer), gain(mix_norm_post, layer))
        xs, hs, s2 = _mixer_forward(f"{t}_mix", xs, hs, mix_gains, gain(ffn2_norm_pre, layer), feed, k0 + 2,
                                    taps[layer], dims)
        following = gain(ffn1_norm_pre, layer + 1) if layer + 1 < n_layers else None
        xs, hs, s3 = _ffn_forward(f"{t}_ffn2", xs, hs, gain(ffn2_norm_post, layer), following, feed, k0 + 4)
        saved.append((s1, s2, s3, mix_gains))
    wts = {k: [feed.fulls[layer * nk + i] for layer in range(n_layers)] for i, k in enumerate(kinds)}
    for k in ("dn1", "wout", "dn2"):
        wts[k] = [w.reshape(-1, d) for w in wts[k]]
    top = n_layers - 1
    dxs, loss_part, *head = _loss_head("loss_head", xs, loss_target[0],
                                       (saved[top][2][4], gain(ffn2_norm_post, top), FFN_RESIDUAL_WEIGHT))
    loss = lax.psum(jnp.sum(loss_part), ("x", "y", "c"))

    red = _GradReduce(core, chip_arr, n_layers, per_layer=("gu1", "dn1"))
    small = [None] * n_layers
    flow = {"deps": (), "in_flight": ()}

    def between(dx, group, also=()):
        after = dx
        for g in flow["in_flight"]:
            after = red.finish(*g, after)
        flow["deps"] = (red.exchange(*group, after),)
        flow["in_flight"] = tuple(also) + (group,)

    head = tuple(head)
    for layer in reversed(range(n_layers)):
        t = f"l{layer}"
        s1, s2, s3, mix_gains = saved[layer]
        after_ffn2 = (s2[7], mix_gains[3], 1.0)
        after_mix = (s1[4], gain(ffn1_norm_post, layer), FFN_RESIDUAL_WEIGHT)
        after_ffn1 = ((saved[layer - 1][2][4], gain(ffn2_norm_post, layer - 1), FFN_RESIDUAL_WEIGHT)
                      if layer > 0 else None)
        dxs, p_pre2, p_post2, head = _ffn_backward(
            f"{t}_ffn2", dxs, s3, gain(ffn2_norm_pre, layer), gain(ffn2_norm_post, layer),
            wts["gu2"][layer], wts["dn2"][layer], red, ("gu2", "dn2"), layer, flow["deps"], head, after_ffn2)
        between(dxs, (("dn2",), layer), also=((("gu2",), layer),))
        dxs, p_taps, (p_mpre, p_a, p_c, p_mpost), head = _mixer_backward(
            f"{t}_mix", dxs, s2, mix_gains, wts["win"][layer], taps[layer], wts["wout"][layer], dims,
            red, ("win", "wout"), layer, flow["deps"], head, after_mix)
        between(dxs, (("win", "wout"), layer))
        def pack_small(p_pre1, p_post1):
            tap_rows = jnp.zeros((CONV_WIDTH, SUBLANES, d), F32).at[:, 0, :cd].set(p_taps[:CONV_WIDTH])
            rows = [p_pre1, p_post1, p_mpre, jnp.concatenate([p_a, p_c], axis=1), p_mpost, p_pre2, p_post2]
            rows = jnp.concatenate([jnp.stack(rows), tap_rows], axis=0)
            small[layer] = jnp.pad(rows, ((0, SMALL_ROWS - rows.shape[0]), (0, 0), (0, 0)))

        def reduce_small(p_pre1, p_post1):
            pack_small(p_pre1, p_post1)
            flow["small"] = _allreduce_small(jnp.concatenate(small, axis=0))
            return flow["small"]

        dxs, p_pre1, p_post1, head = _ffn_backward(
            f"{t}_ffn1", dxs, s1, gain(ffn1_norm_pre, layer), gain(ffn1_norm_post, layer),
            wts["gu1"][layer], wts["dn1"][layer], red, ("gu1", "dn1"), layer, flow["deps"], head, after_ffn1,
            last=reduce_small if layer == 0 else None)
        if layer > 0:
            pack_small(p_pre1, p_post1)
        between(dxs, (("dn1",), layer), also=((("gu1",), layer),))
    grad_x = dxs[None]

    weights = dict(ffn1_norm_pre=ffn1_norm_pre, ffn1_w_gate_up=ffn1_w_gate_up, ffn1_w_down=ffn1_w_down, ffn1_norm_post=ffn1_norm_post, mix_norm_pre=mix_norm_pre, w_in=w_in, conv_w=conv_w, attn_out_norm=attn_out_norm, conv_out_norm=conv_out_norm, w_out=w_out, mix_norm_post=mix_norm_post, ffn2_norm_pre=ffn2_norm_pre, ffn2_w_gate_up=ffn2_w_gate_up, ffn2_w_down=ffn2_w_down, ffn2_norm_post=ffn2_norm_post)
    m_in = dict(ffn1_norm_pre=m_ffn1_norm_pre, ffn1_w_gate_up=m_ffn1_w_gate_up, ffn1_w_down=m_ffn1_w_down, ffn1_norm_post=m_ffn1_norm_post, mix_norm_pre=m_mix_norm_pre, w_in=m_w_in, conv_w=m_conv_w, attn_out_norm=m_attn_out_norm, conv_out_norm=m_conv_out_norm, w_out=m_w_out, mix_norm_post=m_mix_norm_post, ffn2_norm_pre=m_ffn2_norm_pre, ffn2_w_gate_up=m_ffn2_w_gate_up, ffn2_w_down=m_ffn2_w_down, ffn2_norm_post=m_ffn2_norm_post)
    v_in = dict(ffn1_norm_pre=v_ffn1_norm_pre, ffn1_w_gate_up=v_ffn1_w_gate_up, ffn1_w_down=v_ffn1_w_down, ffn1_norm_post=v_ffn1_norm_post, mix_norm_pre=v_mix_norm_pre, w_in=v_w_in, conv_w=v_conv_w, attn_out_norm=v_attn_out_norm, conv_out_norm=v_conv_out_norm, w_out=v_w_out, mix_norm_post=v_mix_norm_post, ffn2_norm_pre=v_ffn2_norm_pre, ffn2_w_gate_up=v_ffn2_w_gate_up, ffn2_w_down=v_ffn2_w_down, ffn2_norm_post=v_ffn2_norm_post)
    kind_name = dict(gu1="ffn1_w_gate_up", dn1="ffn1_w_down", win="w_in", wout="w_out", gu2="ffn2_w_gate_up", dn2="ffn2_w_down")
    delta, new_m, new_v, grad = {}, {}, {}, {}

    def join_and_update(name, items, deps, after):
        ts, send_sems, recv_sems = _join_start(name, [red.bufs[it] for it in items], deps)
        for a, it in enumerate(items):
            k, layer = it if isinstance(it, tuple) else (it, None)
            n = kind_name[k]
            tag = n if layer is None else f"{n}_{layer}"
            g = _join_wait(f"join_wait_{tag}", ts[a], a, send_sems, recv_sems, after)
            prev = (delta[n], new_m[n], new_v[n], grad[n]) if n in delta else None
            delta[n], new_m[n], new_v[n], grad[n] = _adamw(f"adamw_{tag}", weights[n], g, m_in[n], v_in[n], True,
                                                           layer, prev)
            after = delta[n]
        return after

    early = ("wout", "win", "dn2", "gu2") + tuple((k, layer) for layer in range(1, n_layers) for k in ("dn1", "gu1"))
    last_groups = flow["in_flight"]
    done_early = join_and_update("join_early", early, tuple(red.scatter_tokens[g] for g in last_groups), dxs)
    for group in last_groups:
        red.finish(*group, done_early)
    join_and_update("join_late", (("dn1", 0), ("gu1", 0)), (), done_early)

    small_sum = flow["small"].reshape(n_layers, SMALL_ROWS, d)
    g_ffn1_pre, g_ffn1_post, g_mix_pre = small_sum[:, 0], small_sum[:, 1], small_sum[:, 2]
    g_attn_out, g_conv_out = small_sum[:, 3, :qd], small_sum[:, 3, qd:qd + cd]
    g_mix_post, g_ffn2_pre, g_ffn2_post = small_sum[:, 4], small_sum[:, 5], small_sum[:, 6]
    cc = conv_w.shape[2]
    g_conv = lax.dynamic_slice_in_dim(small_sum[:, 7:7 + CONV_WIDTH, :cd], chip * cc, cc, axis=2)

    grad.update(ffn1_norm_pre=g_ffn1_pre, ffn1_norm_post=g_ffn1_post, mix_norm_pre=g_mix_pre, conv_w=g_conv, attn_out_norm=g_attn_out, conv_out_norm=g_conv_out, mix_norm_post=g_mix_post, ffn2_norm_pre=g_ffn2_pre, ffn2_norm_post=g_ffn2_post)
    names = list(weights)

    vectors = [n for n in names if n not in kind_name.values()]

    def pack(tree):
        flat = jnp.concatenate([tree[n].reshape(-1) for n in vectors])
        return jnp.pad(flat, (0, -flat.size % (SUBLANES * LANES))).reshape(-1, LANES)

    packed = _adamw("adamw_small", pack(weights), pack(grad), pack(m_in), pack(v_in))
    offset = 0
    for n in vectors:
        size = weights[n].size
        for tree, flat in zip((delta, new_m, new_v), packed):
            tree[n] = flat.reshape(-1)[offset:offset + size].reshape(weights[n].shape)
        offset += size

    return (loss, grad_x, *[grad[n] for n in names], *[delta[n] for n in names],
            *[new_m[n] for n in names], *[new_v[n] for n in names])
```
